```python
import math
import jax
import jax.numpy as jnp
from jax import lax
import numpy as np


D_MODEL = 1024
BATCH = 8
SEQ = 2048
DEPTH = 2

CTX_LEN = 256
GRID_W = 64
ROPE_BASE = 10000.0
Q_BLOCK = 128
EPS = 1e-6
NEG_INF = -1e30

MLA_HEADS = 8
MLA_Q_RANK = 384
MLA_KV_RANK = 256
MLA_NOPE = 64
MLA_ROPE = 32
MLA_V = 64
SWA_HEADS = 8
SWA_KV_HEADS = 2
SWA_HEAD_DIM = 64
WINDOW = 128
DIFF_HEADS = 8
DIFF_HALF = 64
DIFF_V = 128
N_EXPERTS = 64
N_GROUPS = 8
TOPK_GROUPS = 4
TOP_K = 8
EXPERT_FF = 256
SHARED_FF = 256
ROUTED_SCALE = 2.5
MOE_BLOCK = 128

EVEN_SPLITS = [MLA_Q_RANK, MLA_KV_RANK, MLA_ROPE, SWA_HEADS * SWA_HEAD_DIM,
               SWA_KV_HEADS * SWA_HEAD_DIM, SWA_KV_HEADS * SWA_HEAD_DIM]
EVEN_IN = sum(EVEN_SPLITS)
EVEN_OUT = MLA_HEADS * MLA_V + SWA_HEADS * SWA_HEAD_DIM
ODD_IN = DIFF_HEADS * (4 * DIFF_HALF + DIFF_V)
ODD_OUT = DIFF_HEADS * DIFF_V
N_EVEN = (DEPTH + 1) // 2
N_ODD = DEPTH // 2

kernel_name = 'hybrid_mla_swa_diffattn_moe_dit'


def split_last(x, sizes):
    points = [int(p) for p in np.cumsum(sizes)[:-1]]
    return jnp.split(x, points, axis=-1)


def rmsnorm(x, g):
    xf = x.astype(jnp.float32)
    y = xf * lax.rsqrt(jnp.mean(xf * xf, axis=-1, keepdims=True) + EPS)
    return (y * g.astype(jnp.float32)).astype(x.dtype)


def modulate(h, shift, scale):
    return h * (1 + scale) + shift


def axial_rope_tables(rows, rot_dim):
    half = rot_dim // 2
    inv_freq = ROPE_BASE ** (-jnp.arange(0, half, 2, dtype=jnp.float32) / half)
    row = jnp.repeat(jnp.arange(rows, dtype=jnp.float32), GRID_W)
    col = jnp.tile(jnp.arange(GRID_W, dtype=jnp.float32), rows)
    ang_r = row[:, None] * inv_freq
    ang_c = col[:, None] * inv_freq
    ang = jnp.concatenate([ang_r, ang_r, ang_c, ang_c], axis=-1)
    return jnp.cos(ang), jnp.sin(ang)


def rotate_half(x):
    x1, x2 = jnp.split(x, 2, axis=-1)
    return jnp.concatenate([-x2, x1], axis=-1)


def apply_rope(x, cos, sin):
    x_r, x_c = jnp.split(x, 2, axis=-1)
    rot = jnp.concatenate([rotate_half(x_r), rotate_half(x_c)], axis=-1)
    return (x * cos[None, :, None, :] + rot * sin[None, :, None, :]).astype(x.dtype)


def full_attention(q, k, v):
    s = jnp.einsum('bqhd,bkhd->bhqk', q, k).astype(jnp.float32) * (q.shape[-1] ** -0.5)
    p = jax.nn.softmax(s, axis=-1).astype(v.dtype)
    return jnp.einsum('bhqk,bkhd->bqhd', p, v)


def sweep_query_blocks(fn, *qs):
    b, n = qs[0].shape[:2]
    nb = n // Q_BLOCK
    blocks = tuple(jnp.moveaxis(q.reshape((b, nb, Q_BLOCK) + q.shape[2:]), 1, 0) for q in qs)
    out = lax.map(lambda qb: fn(*qb), blocks)
    out = jnp.moveaxis(out, 0, 1)
    return out.reshape((b, n) + out.shape[3:])


def softmax_with_sink(s, sink):
    m = jnp.maximum(s.max(axis=-1, keepdims=True), sink)
    e = jnp.exp(s - m)
    return e / (e.sum(axis=-1, keepdims=True) + jnp.exp(sink - m))


def window_sink_attention(q, k, v, k_ctx, v_ctx, sink):
    b, n, hq, d = q.shape
    nb = n // Q_BLOCK
    g = hq // SWA_KV_HEADS
    qb = q.reshape(b, nb, Q_BLOCK, SWA_KV_HEADS, g, d)

    def band(t):
        tp = jnp.pad(t, ((0, 0), (Q_BLOCK, Q_BLOCK), (0, 0), (0, 0)))
        tp = tp.reshape(b, nb + 2, Q_BLOCK, SWA_KV_HEADS, d)
        return jnp.concatenate([tp[:, :-2], tp[:, 1:-1], tp[:, 2:]], axis=2)

    kb, vb = band(k), band(v)
    qq = jnp.arange(Q_BLOCK)[:, None]
    kk = jnp.arange(3 * Q_BLOCK)[None, :]
    rel = kk - Q_BLOCK - qq
    key_pos = jnp.arange(nb)[:, None, None] * Q_BLOCK - Q_BLOCK + kk[None]
    valid = (jnp.abs(rel) <= WINDOW)[None] & (key_pos >= 0) & (key_pos < n)
    scale = d ** -0.5
    s_band = jnp.einsum('bnqhgd,bnkhd->bnhgqk', qb, kb).astype(jnp.float32) * scale
    s_band = jnp.where(valid[None, :, None, None], s_band, NEG_INF)
    s_ctx = jnp.einsum('bnqhgd,bchd->bnhgqc', qb, k_ctx).astype(jnp.float32) * scale
    s = jnp.concatenate([s_band, s_ctx], axis=-1)
    sink_l = sink.astype(jnp.float32).reshape(SWA_KV_HEADS, g)[:, :, None, None]
    p = softmax_with_sink(s, sink_l).astype(v.dtype)
    p_band, p_ctx = p[..., :3 * Q_BLOCK], p[..., 3 * Q_BLOCK:]
    o = (jnp.einsum('bnhgqk,bnkhd->bnqhgd', p_band, vb)
         + jnp.einsum('bnhgqc,bchd->bnqhgd', p_ctx, v_ctx))
    return o.reshape(b, n, hq, d)


def ctx_sink_attention(q, k, v, sink):
    b, n, hq, d = q.shape
    g = hq // SWA_KV_HEADS
    qg = q.reshape(b, n, SWA_KV_HEADS, g, d)
    s = jnp.einsum('bqhgd,bkhd->bhgqk', qg, k).astype(jnp.float32) * (d ** -0.5)
    sink_l = sink.astype(jnp.float32).reshape(SWA_KV_HEADS, g)[:, :, None, None]
    p = softmax_with_sink(s, sink_l).astype(v.dtype)
    return jnp.einsum('bhgqk,bkhd->bqhgd', p, v).reshape(b, n, hq, d)


def even_mixer(a_lat, a_ctx, rope_mla, rope_swa, w_in, g_q, g_kv, w_uq, w_ukv, sink, w_out, ctx_out):
    def project(a):
        b, n = a.shape[:2]
        cq, ckv, kr, qs, ks, vs = split_last(a @ w_in, EVEN_SPLITS)
        q = (rmsnorm(cq, g_q) @ w_uq).reshape(b, n, MLA_HEADS, MLA_NOPE + MLA_ROPE)
        kv = (rmsnorm(ckv, g_kv) @ w_ukv).reshape(b, n, MLA_HEADS, MLA_NOPE + MLA_V)
        q_nope, q_rope = split_last(q, [MLA_NOPE, MLA_ROPE])
        k_nope, v_m = split_last(kv, [MLA_NOPE, MLA_V])
        return (q_nope, q_rope, k_nope, kr[:, :, None, :], v_m,
                qs.reshape(b, n, SWA_HEADS, SWA_HEAD_DIM),
                ks.reshape(b, n, SWA_KV_HEADS, SWA_HEAD_DIM),
                vs.reshape(b, n, SWA_KV_HEADS, SWA_HEAD_DIM))

    def mla_qk(q_nope, q_rope, k_nope, k_rope):
        q = jnp.concatenate([q_nope, q_rope], axis=-1)
        k_rope_h = jnp.broadcast_to(k_rope, k_nope.shape[:3] + (MLA_ROPE,))
        return q, jnp.concatenate([k_nope, k_rope_h], axis=-1)

    b, n = a_lat.shape[:2]
    lq_n, lq_r, lk_n, lk_r, lv_m, lq_s, lk_s, lv_s = project(a_lat)
    cq_n, cq_r, ck_n, ck_r, cv_m, cq_s, ck_s, cv_s = project(a_ctx)
    lq_m, lk_m = mla_qk(lq_n, apply_rope(lq_r, *rope_mla), lk_n, apply_rope(lk_r, *rope_mla))
    ck_m = mla_qk(cq_n, cq_r, ck_n, ck_r)[1]
    lq_s = apply_rope(lq_s, *rope_swa)
    lk_s = apply_rope(lk_s, *rope_swa)
    k_all = jnp.concatenate([ck_m, lk_m], axis=1)
    v_all = jnp.concatenate([cv_m, lv_m], axis=1)
    o_mla = sweep_query_blocks(lambda qb: full_attention(qb, k_all, v_all), lq_m)
    o_swa = window_sink_attention(lq_s, lk_s, lv_s, ck_s, cv_s, sink)
    y_lat = jnp.concatenate([o_mla.reshape(b, n, -1), o_swa.reshape(b, n, -1)], axis=-1) @ w_out
    y_ctx = None
    if ctx_out:
        bc, nc = a_ctx.shape[:2]
        cq_m = mla_qk(cq_n, cq_r, ck_n, ck_r)[0]
        o_mla_c = full_attention(cq_m, ck_m, cv_m)
        o_swa_c = ctx_sink_attention(cq_s, ck_s, cv_s, sink)
        y_ctx = jnp.concatenate([o_mla_c.reshape(bc, nc, -1), o_swa_c.reshape(bc, nc, -1)], axis=-1) @ w_out
    return y_lat, y_ctx


def diff_attention(q1, q2, k1, k2, v, lam):
    scale = q1.shape[-1] ** -0.5
    a1 = jax.nn.softmax(jnp.einsum('bqhd,bkhd->bhqk', q1, k1).astype(jnp.float32) * scale, axis=-1)
    a2 = jax.nn.softmax(jnp.einsum('bqhd,bkhd->bhqk', q2, k2).astype(jnp.float32) * scale, axis=-1)
    return jnp.einsum('bhqk,bkhd->bqhd', (a1 - lam * a2).astype(v.dtype), v)


def odd_mixer(a_lat, a_ctx, rope, w_in, lam_q1, lam_k1, lam_q2, lam_k2, g_sub, w_out, lam_init, ctx_out):
    def project(a):
        b, n = a.shape[:2]
        p = (a @ w_in).reshape(b, n, DIFF_HEADS, 4 * DIFF_HALF + DIFF_V)
        return split_last(p, [DIFF_HALF] * 4 + [DIFF_V])

    def finish(o):
        b, n = o.shape[:2]
        o = rmsnorm(o, g_sub) * (1.0 - lam_init)
        return o.reshape(b, n, ODD_OUT) @ w_out

    f32 = jnp.float32
    lam = (jnp.exp(jnp.sum(lam_q1.astype(f32) * lam_k1.astype(f32)))
           - jnp.exp(jnp.sum(lam_q2.astype(f32) * lam_k2.astype(f32))) + lam_init)
    lq1, lq2, lk1, lk2, lv = project(a_lat)
    cq1, cq2, ck1, ck2, cv = project(a_ctx)
    lq1, lq2, lk1, lk2 = (apply_rope(t, *rope) for t in (lq1, lq2, lk1, lk2))
    k1_all = jnp.concatenate([ck1, lk1], axis=1)
    k2_all = jnp.concatenate([ck2, lk2], axis=1)
    v_all = jnp.concatenate([cv, lv], axis=1)
    o_lat = sweep_query_blocks(lambda a, b: diff_attention(a, b, k1_all, k2_all, v_all, lam), lq1, lq2)
    y_lat = finish(o_lat)
    y_ctx = None
    if ctx_out:
        y_ctx = finish(diff_attention(cq1, cq2, ck1, ck2, cv, lam))
    return y_lat, y_ctx


def route(xf, w_router, b_router):
    n = xf.shape[0]
    scores = jax.nn.sigmoid((xf @ w_router).astype(jnp.float32))
    biased = scores + b_router.astype(jnp.float32)
    grouped = biased.reshape(n, N_GROUPS, N_EXPERTS // N_GROUPS)
    group_score = lax.top_k(grouped, 2)[0].sum(axis=-1)
    _, keep = lax.top_k(group_score, TOPK_GROUPS)
    group_mask = jax.nn.one_hot(keep, N_GROUPS, dtype=jnp.float32).sum(axis=1) > 0
    expert_mask = jnp.repeat(group_mask, N_EXPERTS // N_GROUPS, axis=1)
    _, top_idx = lax.top_k(jnp.where(expert_mask, biased, NEG_INF), TOP_K)
    top_s = jnp.take_along_axis(scores, top_idx, axis=1)
    top_w = top_s / top_s.sum(axis=-1, keepdims=True) * ROUTED_SCALE
    return top_idx, top_w


def routed_experts(xf, top_idx, top_w, w_gate, w_up, w_down):
    n_tok, d = xf.shape
    n_assign = n_tok * TOP_K
    n_slots = ((n_assign + MOE_BLOCK - 1) // MOE_BLOCK + N_EXPERTS) * MOE_BLOCK
    n_blocks = n_slots // MOE_BLOCK
    e_flat = top_idx.reshape(-1)
    order = jnp.argsort(e_flat)
    e_sorted = e_flat[order]
    counts = jnp.bincount(e_flat, length=N_EXPERTS)
    padded = (counts + MOE_BLOCK - 1) // MOE_BLOCK * MOE_BLOCK
    pad_end = jnp.cumsum(padded)
    pad_start = pad_end - padded
    raw_start = jnp.cumsum(counts) - counts
    dest = pad_start[e_sorted] + jnp.arange(n_assign) - raw_start[e_sorted]
    slot_tok = jnp.full((n_slots,), n_tok, jnp.int32).at[dest].set((order // TOP_K).astype(jnp.int32))
    slot_w = jnp.zeros((n_slots,), xf.dtype).at[dest].set(top_w.reshape(-1)[order].astype(xf.dtype))
    block_expert = jnp.minimum(
        jnp.searchsorted(pad_end, jnp.arange(n_blocks) * MOE_BLOCK, side='right'), N_EXPERTS - 1)
    x_pad = jnp.concatenate([xf, jnp.zeros((1, d), xf.dtype)], axis=0)

    def body(acc, blk):
        tok, w, e = blk
        xb = x_pad[tok]
        h = jax.nn.silu(xb @ w_gate[e]) * (xb @ w_up[e])
        return acc.at[tok].add((h @ w_down[e]) * w[:, None]), None

    acc, _ = lax.scan(body, jnp.zeros_like(x_pad),
                      (slot_tok.reshape(n_blocks, MOE_BLOCK), slot_w.reshape(n_blocks, MOE_BLOCK), block_expert))
    return acc[:n_tok]


def moe_ffn(xf, w_router, b_router, we_gate, we_up, we_down, ws_gate, ws_up, ws_down):
    top_idx, top_w = route(xf, w_router, b_router)
    routed = routed_experts(xf, top_idx, top_w, we_gate, we_up, we_down)
    shared = (jax.nn.silu(xf @ ws_gate) * (xf @ ws_up)) @ ws_down
    return routed + shared


def setup_inputs(seed: int = 0) -> dict:
    key = jax.random.key(seed)
    ks = iter(jax.random.split(key, 32))
    D = D_MODEL

    def nrm(shape, scale=1.0):
        return jax.random.normal(next(ks), shape, jnp.float32) * scale

    def gain(shape):
        return 1.0 + nrm(shape, 0.02)

    return {
        'x': nrm((BATCH, SEQ, D)),
        'c': nrm((BATCH, D)),
        'ctx': nrm((BATCH, CTX_LEN, D)),
        'c_ctx': nrm((D,)),
        'w_ada': nrm((DEPTH, D, 6 * D), 0.5 * D ** -0.5),
        'b_ada': nrm((DEPTH, 6 * D), 0.02),
        'g_mix': gain((DEPTH, D)),
        'g_ffn': gain((DEPTH, D)),
        'wa_in': nrm((N_EVEN, D, EVEN_IN), D ** -0.5),
        'ga_q': gain((N_EVEN, MLA_Q_RANK)),
        'ga_kv': gain((N_EVEN, MLA_KV_RANK)),
        'wa_uq': nrm((N_EVEN, MLA_Q_RANK, MLA_HEADS * (MLA_NOPE + MLA_ROPE)), MLA_Q_RANK ** -0.5),
        'wa_ukv': nrm((N_EVEN, MLA_KV_RANK, MLA_HEADS * (MLA_NOPE + MLA_V)), MLA_KV_RANK ** -0.5),
        'wa_sink': nrm((N_EVEN, SWA_HEADS)),
        'wa_out': nrm((N_EVEN, EVEN_OUT, D), EVEN_OUT ** -0.5),
        'wc_in': nrm((N_ODD, D, ODD_IN), D ** -0.5),
        'lam_q1': nrm((N_ODD, DIFF_HALF), 0.1),
        'lam_k1': nrm((N_ODD, DIFF_HALF), 0.1),
        'lam_q2': nrm((N_ODD, DIFF_HALF), 0.1),
        'lam_k2': nrm((N_ODD, DIFF_HALF), 0.1),
        'gc_sub': gain((N_ODD, DIFF_V)),
        'wc_out': nrm((N_ODD, ODD_OUT, D), ODD_OUT ** -0.5),
        'w_router': nrm((DEPTH, D, N_EXPERTS), D ** -0.5),
        'b_router': nrm((DEPTH, N_EXPERTS), 0.01),
        'we_gate': nrm((DEPTH, N_EXPERTS, D, EXPERT_FF), D ** -0.5),
        'we_up': nrm((DEPTH, N_EXPERTS, D, EXPERT_FF), D ** -0.5),
        'we_down': nrm((DEPTH, N_EXPERTS, EXPERT_FF, D), EXPERT_FF ** -0.5),
        'ws_gate': nrm((DEPTH, D, SHARED_FF), D ** -0.5),
        'ws_up': nrm((DEPTH, D, SHARED_FF), D ** -0.5),
        'ws_down': nrm((DEPTH, SHARED_FF, D), SHARED_FF ** -0.5),
        'g_final': gain((D,)),
    }


def reference(x, c, ctx, c_ctx, w_ada, b_ada, g_mix, g_ffn, wa_in, ga_q, ga_kv, wa_uq, wa_ukv,
              wa_sink, wa_out, wc_in, lam_q1, lam_k1, lam_q2, lam_k2, gc_sub, wc_out, w_router,
              b_router, we_gate, we_up, we_down, ws_gate, ws_up, ws_down, g_final):
    b, n, d = x.shape
    n_ctx = ctx.shape[1]
    rows = n // GRID_W
    rope_mla = axial_rope_tables(rows, MLA_ROPE)
    rope_64 = axial_rope_tables(rows, SWA_HEAD_DIM)
    h_lat, h_ctx = x, ctx
    for layer in range(DEPTH):
        last = layer == DEPTH - 1
        mod_l = (jax.nn.silu(c) @ w_ada[layer] + b_ada[layer])[:, None, :]
        mod_c = (jax.nn.silu(c_ctx) @ w_ada[layer] + b_ada[layer])[None, None, :]
        sh_ml, sc_ml, gt_ml, sh_fl, sc_fl, gt_fl = jnp.split(mod_l, 6, axis=-1)
        sh_mc, sc_mc, gt_mc, sh_fc, sc_fc, gt_fc = jnp.split(mod_c, 6, axis=-1)
        a_lat = modulate(rmsnorm(h_lat, g_mix[layer]), sh_ml, sc_ml)
        a_ctx = modulate(rmsnorm(h_ctx, g_mix[layer]), sh_mc, sc_mc)
        if layer % 2 == 0:
            i = layer // 2
            y_lat, y_ctx = even_mixer(a_lat, a_ctx, rope_mla, rope_64, wa_in[i], ga_q[i], ga_kv[i],
                                      wa_uq[i], wa_ukv[i], wa_sink[i], wa_out[i], not last)
        else:
            i = layer // 2
            lam_init = 0.8 - 0.6 * math.exp(-0.3 * layer)
            y_lat, y_ctx = odd_mixer(a_lat, a_ctx, rope_64, wc_in[i], lam_q1[i], lam_k1[i],
                                     lam_q2[i], lam_k2[i], gc_sub[i], wc_out[i], lam_init, not last)
        h_lat = h_lat + gt_ml * y_lat
        f_lat = modulate(rmsnorm(h_lat, g_ffn[layer]), sh_fl, sc_fl).reshape(b * n, d)
        moe_args = (w_router[layer], b_router[layer], we_gate[layer], we_up[layer], we_down[layer],
                    ws_gate[layer], ws_up[layer], ws_down[layer])
        if last:
            y = moe_ffn(f_lat, *moe_args)
            h_lat = h_lat + gt_fl * y.reshape(b, n, d)
        else:
            h_ctx = h_ctx + gt_mc * y_ctx
            f_ctx = modulate(rmsnorm(h_ctx, g_ffn[layer]), sh_fc, sc_fc).reshape(b * n_ctx, d)
            y = moe_ffn(jnp.concatenate([f_lat, f_ctx], axis=0), *moe_args)
            h_lat = h_lat + gt_fl * y[:b * n].reshape(b, n, d)
            h_ctx = h_ctx + gt_fc * y[b * n:].reshape(b, n_ctx, d)
    return rmsnorm(h_lat, g_final)
```

```python
import functools
import math

import jax
import jax.numpy as jnp
from jax import lax
from jax.experimental import pallas as pl
from jax.experimental.pallas import tpu as pltpu

F32 = jnp.float32
BF16 = jnp.bfloat16

D_MODEL = 1024
BATCH = 8
SEQ = 2048
DEPTH = 2
CTX_LEN = 256
GRID_W = 64
ROPE_BASE = 10000.0
EPS = 1e-6
NEG_INF = -1e30

MLA_HEADS = 8
MLA_Q_RANK = 384
MLA_KV_RANK = 256
MLA_NOPE = 64
MLA_ROPE = 32
MLA_V = 64
SWA_HEADS = 8
SWA_KV_HEADS = 2
SWA_HEAD_DIM = 64
SWA_GROUP = SWA_HEADS // SWA_KV_HEADS
WINDOW = 128
DIFF_HEADS = 8
DIFF_HALF = 64
DIFF_V = 128
N_EXPERTS = 64
N_GROUPS = 8
GROUP_SIZE = N_EXPERTS // N_GROUPS
TOPK_GROUPS = 4
TOP_K = 8
EXPERT_FF = 256
SHARED_FF = 256
ROUTED_SCALE = 2.5

LANES = 128
N_LAT = BATCH * SEQ
N_CTX = BATCH * CTX_LEN
N_TOK = N_LAT + N_CTX
MOD_ROWS = 16

TM = 256
LAT_BLOCKS_PER_BATCH = SEQ // TM
TQ_MLA = 512
TQ_DIFF = 512
SWA_BLOCK = 128
MOE_ROWS = 256
TM_COMB = 256
VMEM_LIMIT = 56 * 1024 * 1024

C_CQ = 0
C_CKV = C_CQ + MLA_Q_RANK
C_KR = C_CKV + MLA_KV_RANK
C_QS = C_KR + LANES
C_KS = C_QS + SWA_HEADS * SWA_HEAD_DIM
C_VS = C_KS + SWA_KV_HEADS * SWA_HEAD_DIM
C_END = C_VS + SWA_KV_HEADS * SWA_HEAD_DIM


def _cparams(*sem):
    return pltpu.CompilerParams(dimension_semantics=sem, vmem_limit_bytes=VMEM_LIMIT)


def _dot(a, b):
    return jnp.dot(a, b, preferred_element_type=F32)


def _dot_nt(a, b):
    return lax.dot_general(a, b, (((1,), (1,)), ((), ())), preferred_element_type=F32)


def _rms(x):
    return x * lax.rsqrt(jnp.mean(x * x, axis=-1, keepdims=True) + EPS)


def _silu(x):
    return x * jax.nn.sigmoid(x)


def _rope(x, cos, sin_signed, shift):
    n = x.shape[-1]
    lane = lax.broadcasted_iota(jnp.int32, x.shape, 1)
    first = (lane & shift) == 0
    rot = jnp.where(first, pltpu.roll(x, n - shift, 1), pltpu.roll(x, shift, 1))
    return x * cos + rot * sin_signed


def _mod_index(i):
    return jnp.minimum(i // LAT_BLOCKS_PER_BATCH, BATCH)


def _rope_index(i):
    return jnp.where(i < N_LAT // TM, i % LAT_BLOCKS_PER_BATCH, LAT_BLOCKS_PER_BATCH)


ADA_TN = 1536


def _ada_kernel(c_ref, w_ref, b_ref, o_ref):
    s = _silu(c_ref[...]).astype(BF16)
    o_ref[0] = _dot(s, w_ref[0].astype(BF16)) + b_ref[0]


def _ada(cc, w_ada, b_ada):
    n_out = w_ada.shape[-1]
    return pl.pallas_call(
        _ada_kernel,
        grid=(DEPTH, n_out // ADA_TN),
        in_specs=[
            pl.BlockSpec((MOD_ROWS, D_MODEL), lambda l, j: (0, 0)),
            pl.BlockSpec((1, D_MODEL, ADA_TN), lambda l, j: (l, 0, j)),
            pl.BlockSpec((1, 1, ADA_TN), lambda l, j: (l, 0, j)),
        ],
        out_specs=pl.BlockSpec((1, MOD_ROWS, ADA_TN), lambda l, j: (l, 0, j)),
        out_shape=jax.ShapeDtypeStruct((DEPTH, MOD_ROWS, n_out), F32),
        compiler_params=_cparams("parallel", "parallel"),
        name="ada",
    )(cc, w_ada, b_ada.reshape(DEPTH, 1, n_out))


def _proj0_kernel(h_ref, g_ref, sh_ref, sc_ref, win_ref, gq_ref, gkv_ref, wuq_ref, wukv_ref,
                  cm_ref, sm_ref, c64_ref, s64_ref,
                  qm_ref, km_ref, vm_ref, qs_ref, ks_ref, vs_ref):
    a = _rms(h_ref[...]) * g_ref[...]
    a = a * (1.0 + sc_ref[0]) + sh_ref[0]
    p = _dot(a.astype(BF16), win_ref[...])
    nq = _rms(p[:, C_CQ:C_CKV]) * gq_ref[...]
    nkv = _rms(p[:, C_CKV:C_KR]) * gkv_ref[...]
    q = _dot(nq.astype(BF16), wuq_ref[...])
    kv = _dot(nkv.astype(BF16), wukv_ref[...])
    cm, sm = cm_ref[...], sm_ref[...]
    c64, s64 = c64_ref[...], s64_ref[...]
    kr = _rope(p[:, C_KR:C_QS], cm, sm, MLA_ROPE // 4)
    q_scale = (MLA_NOPE + MLA_ROPE) ** -0.5
    for h in range(MLA_HEADS):
        sl = slice(h * LANES, (h + 1) * LANES)
        qm_ref[:, sl] = (_rope(q[:, sl], cm, sm, MLA_ROPE // 4) * q_scale).astype(BF16)
        km_ref[:, sl] = (kv[:, sl] + kr).astype(BF16)
    vm_ref[...] = kv[:, MLA_HEADS * LANES:].astype(BF16)
    s_scale = SWA_HEAD_DIM ** -0.5
    for g in range(SWA_GROUP):
        sl = slice(g * LANES, (g + 1) * LANES)
        qs_ref[:, sl] = (_rope(p[:, C_QS + g * LANES:C_QS + (g + 1) * LANES], c64, s64,
                               SWA_HEAD_DIM // 4) * s_scale).astype(BF16)
    ks_ref[...] = _rope(p[:, C_KS:C_VS], c64, s64, SWA_HEAD_DIM // 4).astype(BF16)
    vs_ref[...] = p[:, C_VS:C_END].astype(BF16)


def _proj0(h, g_mix, sh, sc, win, gq, gkv, wuq, wukv, cm, sm, c64, s64):
    row = lambda i: (i, 0)
    const = lambda i: (0, 0)
    mod = lambda i: (_mod_index(i), 0, 0)
    rope = lambda i: (_rope_index(i), 0)
    widths = (MLA_HEADS * LANES, MLA_HEADS * LANES, MLA_HEADS * MLA_V,
              SWA_HEADS * SWA_HEAD_DIM, SWA_KV_HEADS * SWA_HEAD_DIM, SWA_KV_HEADS * SWA_HEAD_DIM)
    return pl.pallas_call(
        _proj0_kernel,
        grid=(N_TOK // TM,),
        in_specs=[
            pl.BlockSpec((TM, D_MODEL), row),
            pl.BlockSpec((1, D_MODEL), const),
            pl.BlockSpec((1, 1, D_MODEL), mod),
            pl.BlockSpec((1, 1, D_MODEL), mod),
            pl.BlockSpec(win.shape, const),
            pl.BlockSpec(gq.shape, const),
            pl.BlockSpec(gkv.shape, const),
            pl.BlockSpec(wuq.shape, const),
            pl.BlockSpec(wukv.shape, const),
            pl.BlockSpec((TM, LANES), rope),
            pl.BlockSpec((TM, LANES), rope),
            pl.BlockSpec((TM, LANES), rope),
            pl.BlockSpec((TM, LANES), rope),
        ],
        out_specs=[pl.BlockSpec((TM, w), row) for w in widths],
        out_shape=[jax.ShapeDtypeStruct((N_TOK, w), BF16) for w in widths],
        compiler_params=_cparams("parallel"),
        name="proj0",
    )(h, g_mix, sh, sc, win, gq, gkv, wuq, wukv, cm, sm, c64, s64)


def _proj1_kernel(h_ref, g_ref, sh_ref, sc_ref, w_ref, c64_ref, s64_ref, q_ref, k_ref, v_ref):
    a = _rms(h_ref[...]) * g_ref[...]
    a = a * (1.0 + sc_ref[0]) + sh_ref[0]
    p = _dot(a.astype(BF16), w_ref[...])
    c64, s64 = c64_ref[...], s64_ref[...]
    width = DIFF_HEADS * LANES
    scale = DIFF_HALF ** -0.5
    for h in range(DIFF_HEADS):
        sl = slice(h * LANES, (h + 1) * LANES)
        q_ref[:, sl] = (_rope(p[:, sl], c64, s64, DIFF_HALF // 4) * scale).astype(BF16)
        k_ref[:, sl] = _rope(p[:, width + h * LANES:width + (h + 1) * LANES], c64, s64,
                             DIFF_HALF // 4).astype(BF16)
    v_ref[...] = p[:, 2 * width:].astype(BF16)


def _proj1(h, g_mix, sh, sc, w, c64, s64):
    row = lambda i: (i, 0)
    const = lambda i: (0, 0)
    mod = lambda i: (_mod_index(i), 0, 0)
    rope = lambda i: (_rope_index(i), 0)
    width = DIFF_HEADS * LANES
    return pl.pallas_call(
        _proj1_kernel,
        grid=(N_TOK // TM,),
        in_specs=[
            pl.BlockSpec((TM, D_MODEL), row),
            pl.BlockSpec((1, D_MODEL), const),
            pl.BlockSpec((1, 1, D_MODEL), mod),
            pl.BlockSpec((1, 1, D_MODEL), mod),
            pl.BlockSpec(w.shape, const),
            pl.BlockSpec((TM, LANES), rope),
            pl.BlockSpec((TM, LANES), rope),
        ],
        out_specs=[pl.BlockSpec((TM, width), row)] * 3,
        out_shape=[jax.ShapeDtypeStruct((N_TOK, width), BF16)] * 3,
        compiler_params=_cparams("parallel"),
        name="proj1",
    )(h, g_mix, sh, sc, w, c64, s64)


def _mla_kernel(*refs, n_kv):
    q_ref = refs[0]
    k_refs = refs[1:1 + n_kv]
    v_refs = refs[1 + n_kv:1 + 2 * n_kv]
    o_ref = refs[1 + 2 * n_kv]
    outs = []
    for hh in range(2):
        sl = slice(hh * LANES, (hh + 1) * LANES)
        qh = q_ref[:, sl]
        ss = [_dot_nt(qh, k[:, sl]) for k in k_refs]
        m = functools.reduce(jnp.maximum, [jnp.max(s, axis=-1, keepdims=True) for s in ss])
        es = [jnp.exp(s - m) for s in ss]
        l = functools.reduce(jnp.add, [jnp.sum(e, axis=-1, keepdims=True) for e in es])
        o = functools.reduce(jnp.add, [_dot(e.astype(BF16), v[...]) for e, v in zip(es, v_refs)])
        outs.append(o / l)
    lane = lax.broadcasted_iota(jnp.int32, outs[0].shape, 1)
    o_ref[...] = jnp.where(lane < MLA_V, outs[0], outs[1]).astype(BF16)


def _mla_attention(qm, km, vm, *, ctx_queries):
    pairs = MLA_HEADS // 2
    if ctx_queries:
        tq, n_q = CTX_LEN, 1
        q_map = lambda b, h, i: (N_LAT // CTX_LEN + b, h)
        kv_specs = [(CTX_LEN, lambda b, h, i: (N_LAT // CTX_LEN + b, h))]
        rows = N_CTX
        o_map = lambda b, h, i: (b, h)
    else:
        tq, n_q = TQ_MLA, SEQ // TQ_MLA
        q_map = lambda b, h, i: (b * n_q + i, h)
        kv_specs = [(CTX_LEN, lambda b, h, i: (N_LAT // CTX_LEN + b, h)), (SEQ, lambda b, h, i: (b, h))]
        rows = N_LAT
        o_map = q_map
    n_kv = len(kv_specs)
    in_specs = [pl.BlockSpec((tq, 2 * LANES), q_map)]
    in_specs += [pl.BlockSpec((n, 2 * LANES), m) for n, m in kv_specs]
    in_specs += [pl.BlockSpec((n, LANES), m) for n, m in kv_specs]
    return pl.pallas_call(
        functools.partial(_mla_kernel, n_kv=n_kv),
        grid=(BATCH, pairs, n_q),
        in_specs=in_specs,
        out_specs=pl.BlockSpec((tq, LANES), o_map),
        out_shape=jax.ShapeDtypeStruct((rows, MLA_HEADS * MLA_V), BF16),
        compiler_params=_cparams("parallel", "parallel", "parallel"),
        name="mla_ctx" if ctx_queries else "mla_lat",
    )(qm, *([km] * n_kv), *([vm] * n_kv))


def _swa_kernel(sink_ref, *refs, band, n_blocks):
    q_ref = refs[0]
    if band:
        kp, kc, kn, kx, vp, vc, vn, vx, o_ref = refs[1:]
        n = pl.program_id(1)
        k_band = jnp.concatenate([kp[...], kc[...], kn[...]], axis=0)
        v_band = jnp.concatenate([vp[...], vc[...], vn[...]], axis=0)
        qq = lax.broadcasted_iota(jnp.int32, (SWA_BLOCK, 3 * SWA_BLOCK), 0)
        kk = lax.broadcasted_iota(jnp.int32, (SWA_BLOCK, 3 * SWA_BLOCK), 1)
        rel = kk - SWA_BLOCK - qq
        key_pos = (n - 1) * SWA_BLOCK + kk
        valid = (jnp.abs(rel) <= WINDOW) & (key_pos >= 0) & (key_pos < n_blocks * SWA_BLOCK)
    else:
        kx, vx, o_ref = refs[1:]
    k_ctx = kx[...]
    v_ctx = vx[...]
    rows = q_ref.shape[0]
    lane = lax.broadcasted_iota(jnp.int32, (rows, LANES), 1)
    low = lane < SWA_HEAD_DIM
    for g in range(SWA_GROUP):
        qg = q_ref[:, g * LANES:(g + 1) * LANES]
        halves = []
        for hk in range(SWA_KV_HEADS):
            qh = jnp.where(low if hk == 0 else jnp.logical_not(low), qg, jnp.zeros_like(qg))
            sink = sink_ref[hk * SWA_GROUP + g]
            s_ctx = _dot_nt(qh, k_ctx)
            m = jnp.maximum(jnp.max(s_ctx, axis=-1, keepdims=True), sink)
            if band:
                s_band = jnp.where(valid, _dot_nt(qh, k_band), NEG_INF)
                m = jnp.maximum(m, jnp.max(s_band, axis=-1, keepdims=True))
            e_ctx = jnp.exp(s_ctx - m)
            denom = jnp.sum(e_ctx, axis=-1, keepdims=True) + jnp.exp(sink - m)
            if band:
                e_band = jnp.exp(s_band - m)
                denom = denom + jnp.sum(e_band, axis=-1, keepdims=True)
            inv = 1.0 / denom
            o = _dot((e_ctx * inv).astype(BF16), v_ctx)
            if band:
                o = o + _dot((e_band * inv).astype(BF16), v_band)
            halves.append(o)
        o_ref[:, g * LANES:(g + 1) * LANES] = jnp.where(low, halves[0], halves[1]).astype(BF16)


def _swa_attention(sink, qs, ks, vs, *, ctx_queries):
    width = SWA_HEADS * SWA_HEAD_DIM
    kvw = SWA_KV_HEADS * SWA_HEAD_DIM
    ctx_map = lambda b, i: (N_LAT // CTX_LEN + b, 0)
    smem = pl.BlockSpec(memory_space=pltpu.SMEM)
    if ctx_queries:
        grid = (BATCH, 1)
        in_specs = [smem, pl.BlockSpec((CTX_LEN, width), ctx_map),
                    pl.BlockSpec((CTX_LEN, kvw), ctx_map), pl.BlockSpec((CTX_LEN, kvw), ctx_map)]
        args = (sink, qs, ks, vs)
        out_spec = pl.BlockSpec((CTX_LEN, width), lambda b, i: (b, 0))
        rows = N_CTX
        n_blocks = 1
    else:
        n_blocks = SEQ // SWA_BLOCK
        grid = (BATCH, n_blocks)
        cur = lambda b, i: (b * n_blocks + i, 0)
        prev = lambda b, i: (b * n_blocks + jnp.maximum(i - 1, 0), 0)
        nxt = lambda b, i: (b * n_blocks + jnp.minimum(i + 1, n_blocks - 1), 0)
        band_specs = [pl.BlockSpec((SWA_BLOCK, kvw), m) for m in (prev, cur, nxt)]
        in_specs = ([smem, pl.BlockSpec((SWA_BLOCK, width), cur)]
                    + band_specs + [pl.BlockSpec((CTX_LEN, kvw), ctx_map)]
                    + band_specs + [pl.BlockSpec((CTX_LEN, kvw), ctx_map)])
        args = (sink, qs, ks, ks, ks, ks, vs, vs, vs, vs)
        out_spec = pl.BlockSpec((SWA_BLOCK, width), cur)
        rows = N_LAT
    return pl.pallas_call(
        functools.partial(_swa_kernel, band=not ctx_queries, n_blocks=n_blocks),
        grid=grid,
        in_specs=in_specs,
        out_specs=out_spec,
        out_shape=jax.ShapeDtypeStruct((rows, width), BF16),
        compiler_params=_cparams("parallel", "parallel"),
        name="swa_ctx" if ctx_queries else "swa_lat",
    )(*args)


def _diff_kernel(q_ref, kc_ref, kl_ref, vc_ref, vl_ref, lam_ref, g_ref, o_ref, *, lam_init):
    lam = (jnp.exp(jnp.sum(lam_ref[0:1, :] * lam_ref[1:2, :], axis=-1, keepdims=True))
           - jnp.exp(jnp.sum(lam_ref[2:3, :] * lam_ref[3:4, :], axis=-1, keepdims=True)) + lam_init)
    q = q_ref[...]
    lane = lax.broadcasted_iota(jnp.int32, q.shape, 1)
    low = lane < DIFF_HALF
    zero = jnp.zeros_like(q)
    k_refs = (kc_ref, kl_ref)
    v_refs = (vc_ref, vl_ref)

    def softmax(qh):
        ss = [_dot_nt(qh, k[...]) for k in k_refs]
        m = functools.reduce(jnp.maximum, [jnp.max(s, axis=-1, keepdims=True) for s in ss])
        es = [jnp.exp(s - m) for s in ss]
        l = functools.reduce(jnp.add, [jnp.sum(e, axis=-1, keepdims=True) for e in es])
        return es, 1.0 / l

    e1, inv1 = softmax(jnp.where(low, q, zero))
    e2, inv2 = softmax(jnp.where(low, zero, q))
    inv2 = inv2 * lam
    o = functools.reduce(jnp.add, [_dot((a * inv1 - b * inv2).astype(BF16), v[...])
                                   for a, b, v in zip(e1, e2, v_refs)])
    o = _rms(o) * g_ref[...] * (1.0 - lam_init)
    o_ref[...] = o.astype(BF16)


def _diff_attention(q, k, v, lam_vecs, g_sub, lam_init):
    n_q = SEQ // TQ_DIFF
    q_map = lambda b, h, i: (b * n_q + i, h)
    ctx_map = lambda b, h, i: (N_LAT // CTX_LEN + b, h)
    lat_map = lambda b, h, i: (b, h)
    const = lambda b, h, i: (0, 0)
    return pl.pallas_call(
        functools.partial(_diff_kernel, lam_init=lam_init),
        grid=(BATCH, DIFF_HEADS, n_q),
        in_specs=[
            pl.BlockSpec((TQ_DIFF, LANES), q_map),
            pl.BlockSpec((CTX_LEN, LANES), ctx_map),
            pl.BlockSpec((SEQ, LANES), lat_map),
            pl.BlockSpec((CTX_LEN, LANES), ctx_map),
            pl.BlockSpec((SEQ, LANES), lat_map),
            pl.BlockSpec(lam_vecs.shape, const),
            pl.BlockSpec(g_sub.shape, const),
        ],
        out_specs=pl.BlockSpec((TQ_DIFF, LANES), q_map),
        out_shape=jax.ShapeDtypeStruct((N_LAT, DIFF_HEADS * DIFF_V), BF16),
        compiler_params=_cparams("parallel", "parallel", "parallel"),
        name="diff_attn",
    )(q, k, k, v, v, lam_vecs, g_sub)


def _route(logits_t, bias, idx_ref, w_ref):
    tm = logits_t.shape[1]
    scores = jax.nn.sigmoid(logits_t)
    biased = scores + bias
    sub = lax.broadcasted_iota(jnp.int32, (GROUP_SIZE, tm), 0).astype(F32)
    grp_scores, grp_biased, grp_index = [], [], []
    group_score = []
    for g in range(N_GROUPS):
        sl = slice(g * GROUP_SIZE, (g + 1) * GROUP_SIZE)
        bg = biased[sl, :]
        grp_scores.append(scores[sl, :])
        grp_biased.append(bg)
        grp_index.append(sub + float(g * GROUP_SIZE))
        m1 = jnp.max(bg, axis=0, keepdims=True)
        first = jnp.min(jnp.where(bg == m1, sub, float(GROUP_SIZE)), axis=0, keepdims=True)
        m2 = jnp.max(jnp.where(sub == first, -jnp.inf, bg), axis=0, keepdims=True)
        group_score.append(m1 + m2)
    keep = [jnp.zeros((1, tm), F32) for _ in range(N_GROUPS)]
    for _ in range(TOPK_GROUPS):
        m = functools.reduce(jnp.maximum, group_score)
        found = jnp.zeros((1, tm), F32)
        for g in range(N_GROUPS):
            hit = jnp.where(group_score[g] == m, 1.0 - found, 0.0)
            keep[g] = keep[g] + hit
            found = found + hit
            group_score[g] = jnp.where(hit > 0.0, -jnp.inf, group_score[g])
    vals = [jnp.where(keep[g] > 0.0, grp_biased[g], NEG_INF) for g in range(N_GROUPS)]
    picked = []
    for _ in range(TOP_K):
        m = jnp.max(functools.reduce(jnp.maximum, vals), axis=0, keepdims=True)
        cand = [jnp.where(vals[g] == m, grp_index[g], float(N_EXPERTS)) for g in range(N_GROUPS)]
        ei = jnp.min(functools.reduce(jnp.minimum, cand), axis=0, keepdims=True)
        sel = [grp_index[g] == ei for g in range(N_GROUPS)]
        s_k = functools.reduce(jnp.add, [jnp.where(sel[g], grp_scores[g], 0.0) for g in range(N_GROUPS)])
        picked.append((ei, jnp.sum(s_k, axis=0, keepdims=True)))
        vals = [jnp.where(sel[g], -jnp.inf, vals[g]) for g in range(N_GROUPS)]
    total = functools.reduce(jnp.add, [s for _, s in picked])
    for k, (ei, s) in enumerate(picked):
        idx_ref[k:k + 1, :] = ei.astype(jnp.int32)
        w_ref[k:k + 1, :] = s / total * ROUTED_SCALE


def _post_kernel(o_ref, h_ref, wout_ref, gtm_ref, g_ref, sh_ref, sc_ref, gtf_ref,
                 wrh_ref, wrl_ref, br_ref, wsg_ref, wsu_ref, wsd_ref,
                 hs_ref, f_ref, idx_ref, w_ref):
    h1 = h_ref[...] + gtm_ref[0] * _dot(o_ref[...], wout_ref[...])
    f = _rms(h1) * g_ref[...]
    f = f * (1.0 + sc_ref[0]) + sh_ref[0]
    f_hi = f.astype(BF16)
    f_lo = (f - f_hi.astype(F32)).astype(BF16)
    f_ref[...] = f_hi
    logits_t = (_dot_nt(wrh_ref[...], f_hi) + _dot_nt(wrh_ref[...], f_lo) + _dot_nt(wrl_ref[...], f_hi))
    _route(logits_t, br_ref[...], idx_ref, w_ref)
    mid = _silu(_dot(f_hi, wsg_ref[...])) * _dot(f_hi, wsu_ref[...])
    shared = _dot(mid.astype(BF16), wsd_ref[...])
    hs_ref[...] = h1 + gtf_ref[0] * shared


def _post(o, h, wout, gtm, g_ffn, sh, sc, gtf, wrh, wrl, br, wsg, wsu, wsd, n_rows):
    row = lambda i: (i, 0)
    col = lambda i: (0, i)
    const = lambda i: (0, 0)
    mod = lambda i: (_mod_index(i), 0, 0)
    vec = pl.BlockSpec((1, 1, D_MODEL), mod)
    full = lambda a: pl.BlockSpec(a.shape, const)
    return pl.pallas_call(
        _post_kernel,
        grid=(n_rows // TM,),
        in_specs=[
            pl.BlockSpec((TM, D_MODEL), row),
            pl.BlockSpec((TM, D_MODEL), row),
            full(wout), vec, full(g_ffn), vec, vec, vec,
            full(wrh), full(wrl), full(br), full(wsg), full(wsu), full(wsd),
        ],
        out_specs=[
            pl.BlockSpec((TM, D_MODEL), row),
            pl.BlockSpec((TM, D_MODEL), row),
            pl.BlockSpec((TOP_K, TM), col),
            pl.BlockSpec((TOP_K, TM), col),
        ],
        out_shape=[
            jax.ShapeDtypeStruct((n_rows, D_MODEL), F32),
            jax.ShapeDtypeStruct((n_rows, D_MODEL), BF16),
            jax.ShapeDtypeStruct((TOP_K, n_rows), jnp.int32),
            jax.ShapeDtypeStruct((TOP_K, n_rows), F32),
        ],
        compiler_params=_cparams("parallel"),
        name="post_attn",
    )(o, h, wout, gtm, g_ffn, sh, sc, gtf, wrh, wrl, br, wsg, wsu, wsd)


def _moe_kernel(be_ref, nused_ref, x_ref, wg_ref, wu_ref, wd_ref, o_ref, wg_s, wu_s, wd_s):
    i = pl.program_id(0)
    e = be_ref[i]
    changed = jnp.logical_or(i == 0, e != be_ref[jnp.maximum(i - 1, 0)])

    @pl.when(changed)
    def _():
        wg_s[...] = wg_ref[0].astype(BF16)
        wu_s[...] = wu_ref[0].astype(BF16)
        wd_s[...] = wd_ref[0].astype(BF16)

    @pl.when(i < nused_ref[0])
    def _():
        x = x_ref[...]
        mid = _silu(_dot(x, wg_s[...])) * _dot(x, wu_s[...])
        o_ref[...] = _dot(mid.astype(BF16), wd_s[...])


def _moe(block_expert, n_used, xs, wg, wu, wd):
    n_blocks = xs.shape[0] // MOE_ROWS
    grid_spec = pltpu.PrefetchScalarGridSpec(
        num_scalar_prefetch=2,
        grid=(n_blocks,),
        in_specs=[
            pl.BlockSpec((MOE_ROWS, D_MODEL), lambda i, be, nu: (i, 0)),
            pl.BlockSpec((1, D_MODEL, EXPERT_FF), lambda i, be, nu: (be[i], 0, 0)),
            pl.BlockSpec((1, D_MODEL, EXPERT_FF), lambda i, be, nu: (be[i], 0, 0)),
            pl.BlockSpec((1, EXPERT_FF, D_MODEL), lambda i, be, nu: (be[i], 0, 0)),
        ],
        out_specs=pl.BlockSpec((MOE_ROWS, D_MODEL), lambda i, be, nu: (i, 0)),
        scratch_shapes=[
            pltpu.VMEM((D_MODEL, EXPERT_FF), BF16),
            pltpu.VMEM((D_MODEL, EXPERT_FF), BF16),
            pltpu.VMEM((EXPERT_FF, D_MODEL), BF16),
        ],
    )
    return pl.pallas_call(
        _moe_kernel,
        grid_spec=grid_spec,
        out_shape=jax.ShapeDtypeStruct((xs.shape[0], D_MODEL), F32),
        compiler_params=_cparams("arbitrary"),
        name="moe_experts",
    )(block_expert, n_used, xs, wg, wu, wd)


def _combine_kernel(y_ref, w_ref, hs_ref, gtf_ref, gfin_ref, o_ref, *, final):
    w = w_ref[...]
    acc = y_ref[0] * w[:, 0:1]
    for k in range(1, TOP_K):
        acc = acc + y_ref[k] * w[:, k:k + 1]
    out = hs_ref[...] + gtf_ref[0] * acc
    if final:
        out = _rms(out) * gfin_ref[...]
    o_ref[...] = out


def _combine(y, w, hs, gtf, g_final, *, final):
    n_rows = hs.shape[0]
    return pl.pallas_call(
        functools.partial(_combine_kernel, final=final),
        grid=(n_rows // TM_COMB,),
        in_specs=[
            pl.BlockSpec((TOP_K, TM_COMB, D_MODEL), lambda i: (0, i, 0)),
            pl.BlockSpec((TM_COMB, TOP_K), lambda i: (i, 0)),
            pl.BlockSpec((TM_COMB, D_MODEL), lambda i: (i, 0)),
            pl.BlockSpec((1, 1, D_MODEL), lambda i: (jnp.minimum(i // (SEQ // TM_COMB), BATCH), 0, 0)),
            pl.BlockSpec((1, D_MODEL), lambda i: (0, 0)),
        ],
        out_specs=pl.BlockSpec((TM_COMB, D_MODEL), lambda i: (i, 0)),
        out_shape=jax.ShapeDtypeStruct((n_rows, D_MODEL), F32),
        compiler_params=_cparams("parallel"),
        name="combine_final" if final else "combine",
    )(y, w, hs, gtf, g_final)


def _rope_tables():
    rows = SEQ // GRID_W

    def angles(rot_dim):
        half = rot_dim // 2
        inv_freq = ROPE_BASE ** (-jnp.arange(0, half, 2, dtype=F32) / half)
        row = jnp.repeat(jnp.arange(rows, dtype=F32), GRID_W)
        col = jnp.tile(jnp.arange(GRID_W, dtype=F32), rows)
        ang_r = row[:, None] * inv_freq
        ang_c = col[:, None] * inv_freq
        return jnp.concatenate([ang_r, ang_r, ang_c, ang_c], axis=-1)

    def signed(sin, quarter):
        sign = jnp.where((jnp.arange(sin.shape[-1]) // quarter) % 2 == 0, -1.0, 1.0)
        return sin * sign

    def with_identity(cos, sin):
        cos = jnp.concatenate([cos, jnp.ones((TM, LANES), F32)], axis=0)
        sin = jnp.concatenate([sin, jnp.zeros((TM, LANES), F32)], axis=0)
        return cos, sin

    a64 = angles(SWA_HEAD_DIM)
    c64 = jnp.tile(jnp.cos(a64), (1, LANES // SWA_HEAD_DIM))
    s64 = jnp.tile(signed(jnp.sin(a64), SWA_HEAD_DIM // 4), (1, LANES // SWA_HEAD_DIM))
    a32 = angles(MLA_ROPE)
    pad_lo = MLA_NOPE
    pad_hi = LANES - MLA_NOPE - MLA_ROPE
    cm = jnp.concatenate([jnp.ones((SEQ, pad_lo), F32), jnp.cos(a32), jnp.ones((SEQ, pad_hi), F32)], axis=-1)
    sm = jnp.concatenate([jnp.zeros((SEQ, pad_lo), F32), signed(jnp.sin(a32), MLA_ROPE // 4),
                          jnp.zeros((SEQ, pad_hi), F32)], axis=-1)
    return with_identity(cm, sm) + with_identity(c64, s64)


def _layer0_weights(wa_in, wa_uq, wa_ukv, wa_out):
    d = D_MODEL
    cq, ckv, kr, qs, ks, vs = jnp.split(
        wa_in, [C_CKV, C_KR, C_KR + MLA_ROPE, C_KR + MLA_ROPE + SWA_HEADS * SWA_HEAD_DIM,
                C_KR + MLA_ROPE + (SWA_HEADS + SWA_KV_HEADS) * SWA_HEAD_DIM], axis=-1)
    kr_pad = jnp.concatenate([jnp.zeros((d, MLA_NOPE), F32), kr,
                              jnp.zeros((d, LANES - MLA_NOPE - MLA_ROPE), F32)], axis=-1)
    qs_pair = qs.reshape(d, SWA_KV_HEADS, SWA_GROUP, SWA_HEAD_DIM).transpose(0, 2, 1, 3).reshape(d, -1)
    win = jnp.concatenate([cq, ckv, kr_pad, qs_pair, ks, vs], axis=-1).astype(BF16)
    uq = wa_uq.reshape(MLA_Q_RANK, MLA_HEADS, MLA_NOPE + MLA_ROPE)
    uq = jnp.pad(uq, ((0, 0), (0, 0), (0, LANES - MLA_NOPE - MLA_ROPE))).reshape(MLA_Q_RANK, -1)
    ukv = wa_ukv.reshape(MLA_KV_RANK, MLA_HEADS, MLA_NOPE + MLA_V)
    uk = jnp.pad(ukv[:, :, :MLA_NOPE], ((0, 0), (0, 0), (0, LANES - MLA_NOPE))).reshape(MLA_KV_RANK, -1)
    uv = ukv[:, :, MLA_NOPE:].reshape(MLA_KV_RANK, -1)
    wukv = jnp.concatenate([uk, uv], axis=-1)
    n_mla = MLA_HEADS * MLA_V
    out_swa = wa_out[n_mla:].reshape(SWA_KV_HEADS, SWA_GROUP, SWA_HEAD_DIM, d).transpose(1, 0, 2, 3)
    wout = jnp.concatenate([wa_out[:n_mla], out_swa.reshape(-1, d)], axis=0)
    return win, uq.astype(BF16), wukv.astype(BF16), wout.astype(BF16)


def _layer1_weights(wc_in):
    per_head = 4 * DIFF_HALF + DIFF_V
    w = wc_in.reshape(D_MODEL, DIFF_HEADS, per_head)
    q = w[:, :, :2 * DIFF_HALF].reshape(D_MODEL, -1)
    k = w[:, :, 2 * DIFF_HALF:4 * DIFF_HALF].reshape(D_MODEL, -1)
    v = w[:, :, 4 * DIFF_HALF:].reshape(D_MODEL, -1)
    return jnp.concatenate([q, k, v], axis=-1).astype(BF16)


def _dispatch(top_idx):
    n_tok = top_idx.shape[1]
    n_assign = n_tok * TOP_K
    n_blocks = -(-n_assign // MOE_ROWS) + N_EXPERTS
    onehot = (top_idx[:, :, None] == jnp.arange(N_EXPERTS, dtype=jnp.int32)).sum(axis=0).astype(jnp.int32)
    incl = jnp.cumsum(onehot, axis=0)
    counts = incl[-1]
    padded = (counts + MOE_ROWS - 1) // MOE_ROWS * MOE_ROWS
    pad_end = jnp.cumsum(padded)
    pad_start = pad_end - padded
    base = incl - onehot + pad_start[None, :]
    dest = jnp.take_along_axis(base, top_idx.T, axis=1).T
    tok = jnp.broadcast_to(jnp.arange(n_tok, dtype=jnp.int32)[None, :], dest.shape)
    slot_tok = jnp.zeros((n_blocks * MOE_ROWS,), jnp.int32).at[dest.reshape(-1)].set(tok.reshape(-1))
    block_expert = jnp.minimum(
        jnp.searchsorted(pad_end, jnp.arange(n_blocks, dtype=jnp.int32) * MOE_ROWS, side='right'),
        N_EXPERTS - 1).astype(jnp.int32)
    n_used = (pad_end[-1] // MOE_ROWS).astype(jnp.int32).reshape(1)
    return slot_tok, dest, block_expert, n_used


def _moe_layer(f, top_idx, top_w, hs, gtf, g_final, wg, wu, wd, *, final):
    slot_tok, dest, block_expert, n_used = _dispatch(top_idx)
    xs = jnp.take(f, slot_tok, axis=0)
    ys = _moe(block_expert, n_used, xs, wg, wu, wd)
    y = jnp.take(ys, dest, axis=0)
    return _combine(y, top_w.T, hs, gtf, g_final, final=final)


def kernel(x, c, ctx, c_ctx, w_ada, b_ada, g_mix, g_ffn, wa_in, ga_q, ga_kv, wa_uq, wa_ukv, wa_sink, wa_out,
           wc_in, lam_q1, lam_k1, lam_q2, lam_k2, gc_sub, wc_out, w_router, b_router, we_gate, we_up, we_down,
           ws_gate, ws_up, ws_down, g_final):
    d = D_MODEL
    h = jnp.concatenate([x.reshape(N_LAT, d), ctx.reshape(N_CTX, d)], axis=0)
    cc = jnp.concatenate([c, c_ctx[None, :], jnp.zeros((MOD_ROWS - BATCH - 1, d), F32)], axis=0)
    mod = _ada(cc, w_ada, b_ada).reshape(DEPTH, MOD_ROWS, 6, 1, d)

    def mod_vec(layer, j):
        return mod[layer, :, j]

    cm, sm, c64, s64 = _rope_tables()
    g_final2 = g_final.reshape(1, d)

    def router_weights(layer):
        wt = w_router[layer].T
        hi = wt.astype(BF16)
        lo = (wt - hi.astype(F32)).astype(BF16)
        return hi, lo, b_router[layer].reshape(N_EXPERTS, 1)

    def shared_weights(layer):
        return ws_gate[layer].astype(BF16), ws_up[layer].astype(BF16), ws_down[layer].astype(BF16)

    win, wuq, wukv, wout0 = _layer0_weights(wa_in[0], wa_uq[0], wa_ukv[0], wa_out[0])
    qm, km, vm, qs, ks, vs = _proj0(h, g_mix[0].reshape(1, d), mod_vec(0, 0), mod_vec(0, 1), win,
                                    ga_q[0].reshape(1, -1), ga_kv[0].reshape(1, -1), wuq, wukv, cm, sm, c64, s64)
    sink = wa_sink[0].astype(F32)
    o_mla = jnp.concatenate([_mla_attention(qm, km, vm, ctx_queries=False),
                             _mla_attention(qm, km, vm, ctx_queries=True)], axis=0)
    o_swa = jnp.concatenate([_swa_attention(sink, qs, ks, vs, ctx_queries=False),
                             _swa_attention(sink, qs, ks, vs, ctx_queries=True)], axis=0)
    o = jnp.concatenate([o_mla, o_swa], axis=-1)
    hs, f, top_idx, top_w = _post(o, h, wout0, mod_vec(0, 2), g_ffn[0].reshape(1, d), mod_vec(0, 3),
                                  mod_vec(0, 4), mod_vec(0, 5), *router_weights(0), *shared_weights(0), N_TOK)
    h = _moe_layer(f, top_idx, top_w, hs, mod_vec(0, 5), g_final2, we_gate[0], we_up[0], we_down[0], final=False)

    layer = 1
    lam_init = 0.8 - 0.6 * math.exp(-0.3 * layer)
    q1, k1, v1 = _proj1(h, g_mix[1].reshape(1, d), mod_vec(1, 0), mod_vec(1, 1), _layer1_weights(wc_in[0]),
                        c64, s64)
    lam_vecs = jnp.zeros((8, LANES), F32).at[:4, :DIFF_HALF].set(
        jnp.stack([lam_q1[0], lam_k1[0], lam_q2[0], lam_k2[0]]).astype(F32))
    o = _diff_attention(q1, k1, v1, lam_vecs, gc_sub[0].reshape(1, DIFF_V), lam_init)
    hs, f, top_idx, top_w = _post(o, h, wc_out[0].astype(BF16), mod_vec(1, 2), g_ffn[1].reshape(1, d),
                                  mod_vec(1, 3), mod_vec(1, 4), mod_vec(1, 5), *router_weights(1),
                                  *shared_weights(1), N_LAT)
    out = _moe_layer(f, top_idx, top_w, hs, mod_vec(1, 5), g_final2, we_gate[1], we_up[1], we_down[1], final=True)
    return out.reshape(BATCH, SEQ, d)
```

```python
import functools
import math

import jax
import jax.numpy as jnp
from jax import lax
from jax.experimental import pallas as pl
from jax.experimental.pallas import tpu as pltpu
from jax.experimental.pallas import tpu_sc as plsc

F32 = jnp.float32
BF16 = jnp.bfloat16

D_MODEL = 1024
BATCH = 8
SEQ = 2048
DEPTH = 2
CTX_LEN = 256
GRID_W = 64
ROPE_BASE = 10000.0
EPS = 1e-6
NEG_INF = -1e30

MLA_HEADS = 8
MLA_Q_RANK = 384
MLA_KV_RANK = 256
MLA_NOPE = 64
MLA_ROPE = 32
MLA_V = 64
SWA_HEADS = 8
SWA_KV_HEADS = 2
SWA_HEAD_DIM = 64
SWA_GROUP = SWA_HEADS // SWA_KV_HEADS
WINDOW = 128
DIFF_HEADS = 8
DIFF_HALF = 64
DIFF_V = 128
N_EXPERTS = 64
N_GROUPS = 8
GROUP_SIZE = N_EXPERTS // N_GROUPS
TOPK_GROUPS = 4
TOP_K = 8
EXPERT_FF = 256
SHARED_FF = 256
ROUTED_SCALE = 2.5

LANES = 128
N_LAT = BATCH * SEQ
N_CTX = BATCH * CTX_LEN
N_TOK = N_LAT + N_CTX
MOD_ROWS = 16

TM = 256
LAT_BLOCKS_PER_BATCH = SEQ // TM
TQ_MLA = 512
TQ_DIFF = 512
SWA_BLOCK = 128
MOE_ROWS = 256
TM_COMB = 256
VMEM_LIMIT = 56 * 1024 * 1024

C_CQ = 0
C_CKV = C_CQ + MLA_Q_RANK
C_KR = C_CKV + MLA_KV_RANK
C_QS = C_KR + LANES
C_KS = C_QS + SWA_HEADS * SWA_HEAD_DIM
C_VS = C_KS + SWA_KV_HEADS * SWA_HEAD_DIM
C_END = C_VS + SWA_KV_HEADS * SWA_HEAD_DIM


def _cparams(*sem):
    return pltpu.CompilerParams(dimension_semantics=sem, vmem_limit_bytes=VMEM_LIMIT)


def _dot(a, b):
    return jnp.dot(a, b, preferred_element_type=F32)


def _dot_nt(a, b):
    return lax.dot_general(a, b, (((1,), (1,)), ((), ())), preferred_element_type=F32)


def _rms(x):
    return x * lax.rsqrt(jnp.mean(x * x, axis=-1, keepdims=True) + EPS)


def _silu(x):
    return x * jax.nn.sigmoid(x)


def _rope(x, cos, sin_signed, shift):
    n = x.shape[-1]
    lane = lax.broadcasted_iota(jnp.int32, x.shape, 1)
    first = (lane & shift) == 0
    rot = jnp.where(first, pltpu.roll(x, n - shift, 1), pltpu.roll(x, shift, 1))
    return x * cos + rot * sin_signed


def _mod_index(i):
    return jnp.minimum(i // LAT_BLOCKS_PER_BATCH, BATCH)


def _rope_index(i):
    return jnp.where(i < N_LAT // TM, i % LAT_BLOCKS_PER_BATCH, LAT_BLOCKS_PER_BATCH)


ADA_TN = 1536


def _ada_kernel(c_ref, w_ref, b_ref, o_ref):
    s = _silu(c_ref[...]).astype(BF16)
    o_ref[0] = _dot(s, w_ref[0].astype(BF16)) + b_ref[0]


def _ada(cc, w_ada, b_ada):
    n_out = w_ada.shape[-1]
    return pl.pallas_call(
        _ada_kernel,
        grid=(DEPTH, n_out // ADA_TN),
        in_specs=[
            pl.BlockSpec((MOD_ROWS, D_MODEL), lambda l, j: (0, 0)),
            pl.BlockSpec((1, D_MODEL, ADA_TN), lambda l, j: (l, 0, j)),
            pl.BlockSpec((1, 1, ADA_TN), lambda l, j: (l, 0, j)),
        ],
        out_specs=pl.BlockSpec((1, MOD_ROWS, ADA_TN), lambda l, j: (l, 0, j)),
        out_shape=jax.ShapeDtypeStruct((DEPTH, MOD_ROWS, n_out), F32),
        compiler_params=_cparams("parallel", "parallel"),
        name="ada",
    )(cc, w_ada, b_ada.reshape(DEPTH, 1, n_out))


def _proj0_kernel(h_ref, g_ref, sh_ref, sc_ref, win_ref, gq_ref, gkv_ref, wuq_ref, wukv_ref,
                  cm_ref, sm_ref, c64_ref, s64_ref,
                  qm_ref, km_ref, vm_ref, qs_ref, ks_ref, vs_ref):
    a = _rms(h_ref[...]) * g_ref[...]
    a = a * (1.0 + sc_ref[0]) + sh_ref[0]
    p = _dot(a.astype(BF16), win_ref[...])
    nq = _rms(p[:, C_CQ:C_CKV]) * gq_ref[...]
    nkv = _rms(p[:, C_CKV:C_KR]) * gkv_ref[...]
    q = _dot(nq.astype(BF16), wuq_ref[...])
    kv = _dot(nkv.astype(BF16), wukv_ref[...])
    cm, sm = cm_ref[...], sm_ref[...]
    c64, s64 = c64_ref[...], s64_ref[...]
    kr = _rope(p[:, C_KR:C_QS], cm, sm, MLA_ROPE // 4)
    q_scale = (MLA_NOPE + MLA_ROPE) ** -0.5
    for h in range(MLA_HEADS):
        sl = slice(h * LANES, (h + 1) * LANES)
        qm_ref[:, sl] = (_rope(q[:, sl], cm, sm, MLA_ROPE // 4) * q_scale).astype(BF16)
        km_ref[:, sl] = (kv[:, sl] + kr).astype(BF16)
    vm_ref[...] = kv[:, MLA_HEADS * LANES:].astype(BF16)
    s_scale = SWA_HEAD_DIM ** -0.5
    for g in range(SWA_GROUP):
        sl = slice(g * LANES, (g + 1) * LANES)
        qs_ref[:, sl] = (_rope(p[:, C_QS + g * LANES:C_QS + (g + 1) * LANES], c64, s64,
                               SWA_HEAD_DIM // 4) * s_scale).astype(BF16)
    ks_ref[...] = _rope(p[:, C_KS:C_VS], c64, s64, SWA_HEAD_DIM // 4).astype(BF16)
    vs_ref[...] = p[:, C_VS:C_END].astype(BF16)


def _proj0(h, g_mix, sh, sc, win, gq, gkv, wuq, wukv, cm, sm, c64, s64):
    row = lambda i: (i, 0)
    const = lambda i: (0, 0)
    mod = lambda i: (_mod_index(i), 0, 0)
    rope = lambda i: (_rope_index(i), 0)
    widths = (MLA_HEADS * LANES, MLA_HEADS * LANES, MLA_HEADS * MLA_V,
              SWA_HEADS * SWA_HEAD_DIM, SWA_KV_HEADS * SWA_HEAD_DIM, SWA_KV_HEADS * SWA_HEAD_DIM)
    return pl.pallas_call(
        _proj0_kernel,
        grid=(N_TOK // TM,),
        in_specs=[
            pl.BlockSpec((TM, D_MODEL), row),
            pl.BlockSpec((1, D_MODEL), const),
            pl.BlockSpec((1, 1, D_MODEL), mod),
            pl.BlockSpec((1, 1, D_MODEL), mod),
            pl.BlockSpec(win.shape, const),
            pl.BlockSpec(gq.shape, const),
            pl.BlockSpec(gkv.shape, const),
            pl.BlockSpec(wuq.shape, const),
            pl.BlockSpec(wukv.shape, const),
            pl.BlockSpec((TM, LANES), rope),
            pl.BlockSpec((TM, LANES), rope),
            pl.BlockSpec((TM, LANES), rope),
            pl.BlockSpec((TM, LANES), rope),
        ],
        out_specs=[pl.BlockSpec((TM, w), row) for w in widths],
        out_shape=[jax.ShapeDtypeStruct((N_TOK, w), BF16) for w in widths],
        compiler_params=_cparams("parallel"),
        name="proj0",
    )(h, g_mix, sh, sc, win, gq, gkv, wuq, wukv, cm, sm, c64, s64)


def _proj1_kernel(h_ref, g_ref, sh_ref, sc_ref, w_ref, c64_ref, s64_ref, q_ref, k_ref, v_ref):
    a = _rms(h_ref[...]) * g_ref[...]
    a = a * (1.0 + sc_ref[0]) + sh_ref[0]
    p = _dot(a.astype(BF16), w_ref[...])
    c64, s64 = c64_ref[...], s64_ref[...]
    width = DIFF_HEADS * LANES
    scale = DIFF_HALF ** -0.5
    for h in range(DIFF_HEADS):
        sl = slice(h * LANES, (h + 1) * LANES)
        q_ref[:, sl] = (_rope(p[:, sl], c64, s64, DIFF_HALF // 4) * scale).astype(BF16)
        k_ref[:, sl] = _rope(p[:, width + h * LANES:width + (h + 1) * LANES], c64, s64,
                             DIFF_HALF // 4).astype(BF16)
    v_ref[...] = p[:, 2 * width:].astype(BF16)


def _proj1(h, g_mix, sh, sc, w, c64, s64):
    row = lambda i: (i, 0)
    const = lambda i: (0, 0)
    mod = lambda i: (_mod_index(i), 0, 0)
    rope = lambda i: (_rope_index(i), 0)
    width = DIFF_HEADS * LANES
    return pl.pallas_call(
        _proj1_kernel,
        grid=(N_TOK // TM,),
        in_specs=[
            pl.BlockSpec((TM, D_MODEL), row),
            pl.BlockSpec((1, D_MODEL), const),
            pl.BlockSpec((1, 1, D_MODEL), mod),
            pl.BlockSpec((1, 1, D_MODEL), mod),
            pl.BlockSpec(w.shape, const),
            pl.BlockSpec((TM, LANES), rope),
            pl.BlockSpec((TM, LANES), rope),
        ],
        out_specs=[pl.BlockSpec((TM, width), row)] * 3,
        out_shape=[jax.ShapeDtypeStruct((N_TOK, width), BF16)] * 3,
        compiler_params=_cparams("parallel"),
        name="proj1",
    )(h, g_mix, sh, sc, w, c64, s64)


def _mla_kernel(*refs, n_kv):
    q_ref = refs[0]
    k_refs = refs[1:1 + n_kv]
    v_refs = refs[1 + n_kv:1 + 2 * n_kv]
    o_ref = refs[1 + 2 * n_kv]
    outs = []
    for hh in range(2):
        sl = slice(hh * LANES, (hh + 1) * LANES)
        qh = q_ref[:, sl]
        ss = [_dot_nt(qh, k[:, sl]) for k in k_refs]
        m = functools.reduce(jnp.maximum, [jnp.max(s, axis=-1, keepdims=True) for s in ss])
        es = [jnp.exp(s - m) for s in ss]
        l = functools.reduce(jnp.add, [jnp.sum(e, axis=-1, keepdims=True) for e in es])
        o = functools.reduce(jnp.add, [_dot(e.astype(BF16), v[...]) for e, v in zip(es, v_refs)])
        outs.append(o / l)
    lane = lax.broadcasted_iota(jnp.int32, outs[0].shape, 1)
    o_ref[...] = jnp.where(lane < MLA_V, outs[0], outs[1]).astype(BF16)


def _mla_attention(qm, km, vm, *, ctx_queries):
    pairs = MLA_HEADS // 2
    if ctx_queries:
        tq, n_q = CTX_LEN, 1
        q_map = lambda b, h, i: (N_LAT // CTX_LEN + b, h)
        kv_specs = [(CTX_LEN, lambda b, h, i: (N_LAT // CTX_LEN + b, h))]
        rows = N_CTX
        o_map = lambda b, h, i: (b, h)
    else:
        tq, n_q = TQ_MLA, SEQ // TQ_MLA
        q_map = lambda b, h, i: (b * n_q + i, h)
        kv_specs = [(CTX_LEN, lambda b, h, i: (N_LAT // CTX_LEN + b, h)), (SEQ, lambda b, h, i: (b, h))]
        rows = N_LAT
        o_map = q_map
    n_kv = len(kv_specs)
    in_specs = [pl.BlockSpec((tq, 2 * LANES), q_map)]
    in_specs += [pl.BlockSpec((n, 2 * LANES), m) for n, m in kv_specs]
    in_specs += [pl.BlockSpec((n, LANES), m) for n, m in kv_specs]
    return pl.pallas_call(
        functools.partial(_mla_kernel, n_kv=n_kv),
        grid=(BATCH, pairs, n_q),
        in_specs=in_specs,
        out_specs=pl.BlockSpec((tq, LANES), o_map),
        out_shape=jax.ShapeDtypeStruct((rows, MLA_HEADS * MLA_V), BF16),
        compiler_params=_cparams("parallel", "parallel", "parallel"),
        name="mla_ctx" if ctx_queries else "mla_lat",
    )(qm, *([km] * n_kv), *([vm] * n_kv))


def _swa_kernel(sink_ref, *refs, band, n_blocks):
    q_ref = refs[0]
    if band:
        kp, kc, kn, kx, vp, vc, vn, vx, o_ref = refs[1:]
        n = pl.program_id(1)
        k_band = jnp.concatenate([kp[...], kc[...], kn[...]], axis=0)
        v_band = jnp.concatenate([vp[...], vc[...], vn[...]], axis=0)
        qq = lax.broadcasted_iota(jnp.int32, (SWA_BLOCK, 3 * SWA_BLOCK), 0)
        kk = lax.broadcasted_iota(jnp.int32, (SWA_BLOCK, 3 * SWA_BLOCK), 1)
        rel = kk - SWA_BLOCK - qq
        key_pos = (n - 1) * SWA_BLOCK + kk
        valid = (jnp.abs(rel) <= WINDOW) & (key_pos >= 0) & (key_pos < n_blocks * SWA_BLOCK)
    else:
        kx, vx, o_ref = refs[1:]
    k_ctx = kx[...]
    v_ctx = vx[...]
    rows = q_ref.shape[0]
    lane = lax.broadcasted_iota(jnp.int32, (rows, LANES), 1)
    low = lane < SWA_HEAD_DIM
    for g in range(SWA_GROUP):
        qg = q_ref[:, g * LANES:(g + 1) * LANES]
        halves = []
        for hk in range(SWA_KV_HEADS):
            qh = jnp.where(low if hk == 0 else jnp.logical_not(low), qg, jnp.zeros_like(qg))
            sink = sink_ref[hk * SWA_GROUP + g]
            s_ctx = _dot_nt(qh, k_ctx)
            m = jnp.maximum(jnp.max(s_ctx, axis=-1, keepdims=True), sink)
            if band:
                s_band = jnp.where(valid, _dot_nt(qh, k_band), NEG_INF)
                m = jnp.maximum(m, jnp.max(s_band, axis=-1, keepdims=True))
            e_ctx = jnp.exp(s_ctx - m)
            denom = jnp.sum(e_ctx, axis=-1, keepdims=True) + jnp.exp(sink - m)
            if band:
                e_band = jnp.exp(s_band - m)
                denom = denom + jnp.sum(e_band, axis=-1, keepdims=True)
            inv = 1.0 / denom
            o = _dot((e_ctx * inv).astype(BF16), v_ctx)
            if band:
                o = o + _dot((e_band * inv).astype(BF16), v_band)
            halves.append(o)
        o_ref[:, g * LANES:(g + 1) * LANES] = jnp.where(low, halves[0], halves[1]).astype(BF16)


def _swa_attention(sink, qs, ks, vs, *, ctx_queries):
    width = SWA_HEADS * SWA_HEAD_DIM
    kvw = SWA_KV_HEADS * SWA_HEAD_DIM
    ctx_map = lambda b, i: (N_LAT // CTX_LEN + b, 0)
    smem = pl.BlockSpec(memory_space=pltpu.SMEM)
    if ctx_queries:
        grid = (BATCH, 1)
        in_specs = [smem, pl.BlockSpec((CTX_LEN, width), ctx_map),
                    pl.BlockSpec((CTX_LEN, kvw), ctx_map), pl.BlockSpec((CTX_LEN, kvw), ctx_map)]
        args = (sink, qs, ks, vs)
        out_spec = pl.BlockSpec((CTX_LEN, width), lambda b, i: (b, 0))
        rows = N_CTX
        n_blocks = 1
    else:
        n_blocks = SEQ // SWA_BLOCK
        grid = (BATCH, n_blocks)
        cur = lambda b, i: (b * n_blocks + i, 0)
        prev = lambda b, i: (b * n_blocks + jnp.maximum(i - 1, 0), 0)
        nxt = lambda b, i: (b * n_blocks + jnp.minimum(i + 1, n_blocks - 1), 0)
        band_specs = [pl.BlockSpec((SWA_BLOCK, kvw), m) for m in (prev, cur, nxt)]
        in_specs = ([smem, pl.BlockSpec((SWA_BLOCK, width), cur)]
                    + band_specs + [pl.BlockSpec((CTX_LEN, kvw), ctx_map)]
                    + band_specs + [pl.BlockSpec((CTX_LEN, kvw), ctx_map)])
        args = (sink, qs, ks, ks, ks, ks, vs, vs, vs, vs)
        out_spec = pl.BlockSpec((SWA_BLOCK, width), cur)
        rows = N_LAT
    return pl.pallas_call(
        functools.partial(_swa_kernel, band=not ctx_queries, n_blocks=n_blocks),
        grid=grid,
        in_specs=in_specs,
        out_specs=out_spec,
        out_shape=jax.ShapeDtypeStruct((rows, width), BF16),
        compiler_params=_cparams("parallel", "parallel"),
        name="swa_ctx" if ctx_queries else "swa_lat",
    )(*args)


def _diff_kernel(q_ref, kc_ref, kl_ref, vc_ref, vl_ref, lam_ref, g_ref, o_ref, *, lam_init):
    lam = (jnp.exp(jnp.sum(lam_ref[0:1, :] * lam_ref[1:2, :], axis=-1, keepdims=True))
           - jnp.exp(jnp.sum(lam_ref[2:3, :] * lam_ref[3:4, :], axis=-1, keepdims=True)) + lam_init)
    q = q_ref[...]
    lane = lax.broadcasted_iota(jnp.int32, q.shape, 1)
    low = lane < DIFF_HALF
    zero = jnp.zeros_like(q)
    k_refs = (kc_ref, kl_ref)
    v_refs = (vc_ref, vl_ref)

    def softmax(qh):
        ss = [_dot_nt(qh, k[...]) for k in k_refs]
        m = functools.reduce(jnp.maximum, [jnp.max(s, axis=-1, keepdims=True) for s in ss])
        es = [jnp.exp(s - m) for s in ss]
        l = functools.reduce(jnp.add, [jnp.sum(e, axis=-1, keepdims=True) for e in es])
        return es, 1.0 / l

    e1, inv1 = softmax(jnp.where(low, q, zero))
    e2, inv2 = softmax(jnp.where(low, zero, q))
    inv2 = inv2 * lam
    o = functools.reduce(jnp.add, [_dot((a * inv1 - b * inv2).astype(BF16), v[...])
                                   for a, b, v in zip(e1, e2, v_refs)])
    o = _rms(o) * g_ref[...] * (1.0 - lam_init)
    o_ref[...] = o.astype(BF16)


def _diff_attention(q, k, v, lam_vecs, g_sub, lam_init):
    n_q = SEQ // TQ_DIFF
    q_map = lambda b, h, i: (b * n_q + i, h)
    ctx_map = lambda b, h, i: (N_LAT // CTX_LEN + b, h)
    lat_map = lambda b, h, i: (b, h)
    const = lambda b, h, i: (0, 0)
    return pl.pallas_call(
        functools.partial(_diff_kernel, lam_init=lam_init),
        grid=(BATCH, DIFF_HEADS, n_q),
        in_specs=[
            pl.BlockSpec((TQ_DIFF, LANES), q_map),
            pl.BlockSpec((CTX_LEN, LANES), ctx_map),
            pl.BlockSpec((SEQ, LANES), lat_map),
            pl.BlockSpec((CTX_LEN, LANES), ctx_map),
            pl.BlockSpec((SEQ, LANES), lat_map),
            pl.BlockSpec(lam_vecs.shape, const),
            pl.BlockSpec(g_sub.shape, const),
        ],
        out_specs=pl.BlockSpec((TQ_DIFF, LANES), q_map),
        out_shape=jax.ShapeDtypeStruct((N_LAT, DIFF_HEADS * DIFF_V), BF16),
        compiler_params=_cparams("parallel", "parallel", "parallel"),
        name="diff_attn",
    )(q, k, k, v, v, lam_vecs, g_sub)


def _pack_bf16_pairs(x):
    n = x.shape[1] // 2
    lo = pltpu.bitcast(x[:, :n].astype(BF16).astype(F32), jnp.uint32)
    hi = pltpu.bitcast(x[:, n:].astype(BF16).astype(F32), jnp.uint32)
    return pltpu.bitcast((lo >> 16) | hi, jnp.int32)


def _unpack_bf16_pairs(w):
    u = pltpu.bitcast(w, jnp.uint32)
    lo = pltpu.bitcast(u << 16, F32)
    hi = pltpu.bitcast(u & jnp.uint32(0xFFFF0000), F32)
    return lo, hi


def _route(logits_t, bias, cnt_ref, tri_ref, idx_ref, w_ref, rank_ref):
    tm = logits_t.shape[1]
    scores = jax.nn.sigmoid(logits_t)
    biased = scores + bias
    sub = lax.broadcasted_iota(jnp.int32, (GROUP_SIZE, tm), 0).astype(F32)
    grp_scores, grp_biased, grp_index = [], [], []
    group_score = []
    for g in range(N_GROUPS):
        sl = slice(g * GROUP_SIZE, (g + 1) * GROUP_SIZE)
        bg = biased[sl, :]
        grp_scores.append(scores[sl, :])
        grp_biased.append(bg)
        grp_index.append(sub + float(g * GROUP_SIZE))
        m1 = jnp.max(bg, axis=0, keepdims=True)
        first = jnp.min(jnp.where(bg == m1, sub, float(GROUP_SIZE)), axis=0, keepdims=True)
        m2 = jnp.max(jnp.where(sub == first, -jnp.inf, bg), axis=0, keepdims=True)
        group_score.append(m1 + m2)
    keep = [jnp.zeros((1, tm), F32) for _ in range(N_GROUPS)]
    for _ in range(TOPK_GROUPS):
        m = functools.reduce(jnp.maximum, group_score)
        found = jnp.zeros((1, tm), F32)
        for g in range(N_GROUPS):
            hit = jnp.where(group_score[g] == m, 1.0 - found, 0.0)
            keep[g] = keep[g] + hit
            found = found + hit
            group_score[g] = jnp.where(hit > 0.0, -jnp.inf, group_score[g])
    vals = [jnp.where(keep[g] > 0.0, grp_biased[g], NEG_INF) for g in range(N_GROUPS)]
    chosen = [jnp.zeros((GROUP_SIZE, tm), F32) for _ in range(N_GROUPS)]
    picked = []
    for _ in range(TOP_K):
        m = jnp.max(functools.reduce(jnp.maximum, vals), axis=0, keepdims=True)
        cand = [jnp.where(vals[g] == m, grp_index[g], float(N_EXPERTS)) for g in range(N_GROUPS)]
        ei = jnp.min(functools.reduce(jnp.minimum, cand), axis=0, keepdims=True)
        sel = [grp_index[g] == ei for g in range(N_GROUPS)]
        s_k = functools.reduce(jnp.add, [jnp.where(sel[g], grp_scores[g], 0.0) for g in range(N_GROUPS)])
        picked.append((ei, jnp.sum(s_k, axis=0, keepdims=True)))
        vals = [jnp.where(sel[g], -jnp.inf, vals[g]) for g in range(N_GROUPS)]
        chosen = [jnp.where(sel[g], 1.0, chosen[g]) for g in range(N_GROUPS)]
    total = functools.reduce(jnp.add, [s for _, s in picked])
    chosen_all = jnp.concatenate(chosen, axis=0)
    rank_all = _dot(chosen_all.astype(BF16), tri_ref[...]) + cnt_ref[...]
    cnt_ref[...] = cnt_ref[...] + jnp.sum(chosen_all, axis=1, keepdims=True)
    for k, (ei, s) in enumerate(picked):
        idx_ref[k:k + 1, :] = ei.astype(jnp.int32)
        w_ref[k:k + 1, :] = s / total * ROUTED_SCALE
        r = functools.reduce(jnp.add, [
            jnp.where(grp_index[g] == ei, rank_all[g * GROUP_SIZE:(g + 1) * GROUP_SIZE, :], 0.0)
            for g in range(N_GROUPS)])
        rank_ref[k:k + 1, :] = jnp.sum(r, axis=0, keepdims=True).astype(jnp.int32)


def _post_kernel(o_ref, h_ref, wout_ref, gtm_ref, g_ref, sh_ref, sc_ref, gtf_ref,
                 wrh_ref, wrl_ref, br_ref, tri_ref, wsg_ref, wsu_ref, wsd_ref,
                 hs_ref, f_ref, idx_ref, w_ref, rank_ref, cnt_out_ref, cnt_ref):
    @pl.when(pl.program_id(0) == 0)
    def _():
        cnt_ref[...] = jnp.zeros_like(cnt_ref)

    h1 = h_ref[...] + gtm_ref[0] * _dot(o_ref[...], wout_ref[...])
    f = _rms(h1) * g_ref[...]
    f = f * (1.0 + sc_ref[0]) + sh_ref[0]
    f_hi = f.astype(BF16)
    f_lo = (f - f_hi.astype(F32)).astype(BF16)
    f_ref[...] = _pack_bf16_pairs(f)
    logits_t = (_dot_nt(wrh_ref[...], f_hi) + _dot_nt(wrh_ref[...], f_lo) + _dot_nt(wrl_ref[...], f_hi))
    _route(logits_t, br_ref[...], cnt_ref, tri_ref, idx_ref, w_ref, rank_ref)
    cnt_out_ref[...] = jnp.broadcast_to(cnt_ref[...], cnt_out_ref.shape).astype(jnp.int32)
    mid = _silu(_dot(f_hi, wsg_ref[...])) * _dot(f_hi, wsu_ref[...])
    shared = _dot(mid.astype(BF16), wsd_ref[...])
    hs_ref[...] = h1 + gtf_ref[0] * shared


def _post(o, h, wout, gtm, g_ffn, sh, sc, gtf, wrh, wrl, br, wsg, wsu, wsd, n_rows):
    row = lambda i: (i, 0)
    col = lambda i: (0, i)
    const = lambda i: (0, 0)
    mod = lambda i: (_mod_index(i), 0, 0)
    vec = pl.BlockSpec((1, 1, D_MODEL), mod)
    full = lambda a: pl.BlockSpec(a.shape, const)
    tri = (jnp.arange(TM)[:, None] < jnp.arange(TM)[None, :]).astype(BF16)
    return pl.pallas_call(
        _post_kernel,
        grid=(n_rows // TM,),
        in_specs=[
            pl.BlockSpec((TM, D_MODEL), row),
            pl.BlockSpec((TM, D_MODEL), row),
            full(wout), vec, full(g_ffn), vec, vec, vec,
            full(wrh), full(wrl), full(br), full(tri), full(wsg), full(wsu), full(wsd),
        ],
        out_specs=[
            pl.BlockSpec((TM, D_MODEL), row),
            pl.BlockSpec((TM, D_MODEL // 2), row),
            pl.BlockSpec((TOP_K, TM), col),
            pl.BlockSpec((TOP_K, TM), col),
            pl.BlockSpec((TOP_K, TM), col),
            pl.BlockSpec((N_EXPERTS, LANES), const),
        ],
        out_shape=[
            jax.ShapeDtypeStruct((n_rows, D_MODEL), F32),
            jax.ShapeDtypeStruct((n_rows, D_MODEL // 2), jnp.int32),
            jax.ShapeDtypeStruct((TOP_K, n_rows), jnp.int32),
            jax.ShapeDtypeStruct((TOP_K, n_rows), F32),
            jax.ShapeDtypeStruct((TOP_K, n_rows), jnp.int32),
            jax.ShapeDtypeStruct((N_EXPERTS, LANES), jnp.int32),
        ],
        scratch_shapes=[pltpu.VMEM((N_EXPERTS, 1), F32)],
        compiler_params=_cparams("arbitrary"),
        name="post_attn",
    )(o, h, wout, gtm, g_ffn, sh, sc, gtf, wrh, wrl, br, tri, wsg, wsu, wsd)


DEST_TN = 2048


def _dest_kernel(ps_ref, idx_ref, rank_ref, o_ref):
    idx = idx_ref[...]
    acc = rank_ref[...]
    for e in range(N_EXPERTS):
        acc = acc + jnp.where(idx == e, ps_ref[e], 0)
    o_ref[...] = acc


def _dest(pad_start, top_idx, rank):
    n_tok = top_idx.shape[1]
    blk = pl.BlockSpec((TOP_K, DEST_TN), lambda i: (0, i))
    return pl.pallas_call(
        _dest_kernel,
        grid=(n_tok // DEST_TN,),
        in_specs=[pl.BlockSpec(memory_space=pltpu.SMEM), blk, blk],
        out_specs=blk,
        out_shape=jax.ShapeDtypeStruct((TOP_K, n_tok), jnp.int32),
        compiler_params=_cparams("parallel"),
        name="dest",
    )(pad_start, top_idx, rank)


SC_CORES = 2
SC_SUBCORES = 16
SC_WORKERS = SC_CORES * SC_SUBCORES
SC_ROWS = 64
ROW_WORDS = D_MODEL // 2


def _sc_mesh():
    return plsc.VectorSubcoreMesh(core_axis_name="core", subcore_axis_name="subcore")


def _sc_worker():
    return lax.axis_index("subcore") * SC_CORES + lax.axis_index("core")


def _sc_dispatch(f_packed, dest_flat, n_slots):
    n_tok = f_packed.shape[0]
    per_worker = n_tok // SC_WORKERS
    n_chunks = per_worker // SC_ROWS
    assert n_chunks * SC_ROWS * SC_WORKERS == n_tok

    def body(f_hbm, dest_hbm, xs_hbm, rows_v, *rest):
        idx_v, sem = rest[:TOP_K], rest[TOP_K]
        wid = _sc_worker()

        @pl.loop(0, n_chunks)
        def _(c):
            base = pl.multiple_of(wid * per_worker + c * SC_ROWS, 8)
            pltpu.sync_copy(f_hbm.at[pl.ds(base, SC_ROWS)], rows_v)
            for k in range(TOP_K):
                pltpu.sync_copy(dest_hbm.at[pl.ds(pl.multiple_of(k * n_tok + base, 8), SC_ROWS)], idx_v[k])
            copies = [pltpu.async_copy(rows_v, xs_hbm.at[idx_v[k]], sem) for k in range(TOP_K)]
            for cp in copies:
                cp.wait()

    return pl.kernel(
        body,
        out_type=jax.ShapeDtypeStruct((n_slots, ROW_WORDS), jnp.int32),
        mesh=_sc_mesh(),
        scratch_types=([pltpu.VMEM((SC_ROWS, ROW_WORDS), jnp.int32)]
                       + [pltpu.VMEM((SC_ROWS,), jnp.int32) for _ in range(TOP_K)]
                       + [pltpu.SemaphoreType.DMA]),
        name="sc_dispatch",
    )(f_packed, dest_flat)


def _sc_gather(ys, dest_flat):
    n_idx = dest_flat.shape[0]
    per_worker = n_idx // SC_WORKERS
    n_chunks = per_worker // SC_ROWS
    assert n_chunks * SC_ROWS * SC_WORKERS == n_idx

    def body(ys_hbm, dest_hbm, out_hbm, rows_v, idx_v, sem):
        wid = _sc_worker()

        @pl.loop(0, n_chunks)
        def _(c):
            base = pl.multiple_of(wid * per_worker + c * SC_ROWS, 8)
            pltpu.sync_copy(dest_hbm.at[pl.ds(base, SC_ROWS)], idx_v)
            pltpu.async_copy(ys_hbm.at[idx_v], rows_v, sem).wait()
            pltpu.sync_copy(rows_v, out_hbm.at[pl.ds(base, SC_ROWS)])

    return pl.kernel(
        body,
        out_type=jax.ShapeDtypeStruct((n_idx, ROW_WORDS), jnp.int32),
        mesh=_sc_mesh(),
        scratch_types=[pltpu.VMEM((SC_ROWS, ROW_WORDS), jnp.int32), pltpu.VMEM((SC_ROWS,), jnp.int32),
                       pltpu.SemaphoreType.DMA],
        name="sc_gather",
    )(ys, dest_flat)


def _moe_kernel(be_ref, nused_ref, x_ref, wg_ref, wu_ref, wd_ref, o_ref, wg_s, wu_s, wd_s):
    i = pl.program_id(0)
    e = be_ref[i]
    changed = jnp.logical_or(i == 0, e != be_ref[jnp.maximum(i - 1, 0)])

    @pl.when(changed)
    def _():
        wg_s[...] = wg_ref[0].astype(BF16)
        wu_s[...] = wu_ref[0].astype(BF16)
        wd_s[...] = wd_ref[0].astype(BF16)

    @pl.when(i < nused_ref[0])
    def _():
        lo, hi = _unpack_bf16_pairs(x_ref[...])
        lo, hi = lo.astype(BF16), hi.astype(BF16)
        half = D_MODEL // 2
        gate = _dot(lo, wg_s[:half, :]) + _dot(hi, wg_s[half:, :])
        up = _dot(lo, wu_s[:half, :]) + _dot(hi, wu_s[half:, :])
        mid = _silu(gate) * up
        o_ref[...] = _pack_bf16_pairs(_dot(mid.astype(BF16), wd_s[...]))


def _moe(block_expert, n_used, xs, wg, wu, wd):
    n_blocks = xs.shape[0] // MOE_ROWS
    grid_spec = pltpu.PrefetchScalarGridSpec(
        num_scalar_prefetch=2,
        grid=(n_blocks,),
        in_specs=[
            pl.BlockSpec((MOE_ROWS, ROW_WORDS), lambda i, be, nu: (i, 0)),
            pl.BlockSpec((1, D_MODEL, EXPERT_FF), lambda i, be, nu: (be[i], 0, 0)),
            pl.BlockSpec((1, D_MODEL, EXPERT_FF), lambda i, be, nu: (be[i], 0, 0)),
            pl.BlockSpec((1, EXPERT_FF, D_MODEL), lambda i, be, nu: (be[i], 0, 0)),
        ],
        out_specs=pl.BlockSpec((MOE_ROWS, ROW_WORDS), lambda i, be, nu: (i, 0)),
        scratch_shapes=[
            pltpu.VMEM((D_MODEL, EXPERT_FF), BF16),
            pltpu.VMEM((D_MODEL, EXPERT_FF), BF16),
            pltpu.VMEM((EXPERT_FF, D_MODEL), BF16),
        ],
    )
    return pl.pallas_call(
        _moe_kernel,
        grid_spec=grid_spec,
        out_shape=jax.ShapeDtypeStruct((xs.shape[0], ROW_WORDS), jnp.int32),
        compiler_params=_cparams("arbitrary"),
        name="moe_experts",
    )(block_expert, n_used, xs, wg, wu, wd)


def _combine_kernel(y_ref, w_ref, hs_ref, gtf_ref, gfin_ref, o_ref, *, final):
    w = w_ref[...]
    acc_lo = acc_hi = None
    for k in range(TOP_K):
        lo, hi = _unpack_bf16_pairs(y_ref[k])
        wk = w[:, k:k + 1]
        acc_lo = lo * wk if k == 0 else acc_lo + lo * wk
        acc_hi = hi * wk if k == 0 else acc_hi + hi * wk
    out = hs_ref[...] + gtf_ref[0] * jnp.concatenate([acc_lo, acc_hi], axis=1)
    if final:
        out = _rms(out) * gfin_ref[...]
    o_ref[...] = out


def _combine(y, w, hs, gtf, g_final, *, final):
    n_rows = hs.shape[0]
    return pl.pallas_call(
        functools.partial(_combine_kernel, final=final),
        grid=(n_rows // TM_COMB,),
        in_specs=[
            pl.BlockSpec((TOP_K, TM_COMB, ROW_WORDS), lambda i: (0, i, 0)),
            pl.BlockSpec((TM_COMB, TOP_K), lambda i: (i, 0)),
            pl.BlockSpec((TM_COMB, D_MODEL), lambda i: (i, 0)),
            pl.BlockSpec((1, 1, D_MODEL), lambda i: (jnp.minimum(i // (SEQ // TM_COMB), BATCH), 0, 0)),
            pl.BlockSpec((1, D_MODEL), lambda i: (0, 0)),
        ],
        out_specs=pl.BlockSpec((TM_COMB, D_MODEL), lambda i: (i, 0)),
        out_shape=jax.ShapeDtypeStruct((n_rows, D_MODEL), F32),
        compiler_params=_cparams("parallel"),
        name="combine_final" if final else "combine",
    )(y, w, hs, gtf, g_final)


def _rope_tables():
    rows = SEQ // GRID_W

    def angles(rot_dim):
        half = rot_dim // 2
        inv_freq = ROPE_BASE ** (-jnp.arange(0, half, 2, dtype=F32) / half)
        row = jnp.repeat(jnp.arange(rows, dtype=F32), GRID_W)
        col = jnp.tile(jnp.arange(GRID_W, dtype=F32), rows)
        ang_r = row[:, None] * inv_freq
        ang_c = col[:, None] * inv_freq
        return jnp.concatenate([ang_r, ang_r, ang_c, ang_c], axis=-1)

    def signed(sin, quarter):
        sign = jnp.where((jnp.arange(sin.shape[-1]) // quarter) % 2 == 0, -1.0, 1.0)
        return sin * sign

    def with_identity(cos, sin):
        cos = jnp.concatenate([cos, jnp.ones((TM, LANES), F32)], axis=0)
        sin = jnp.concatenate([sin, jnp.zeros((TM, LANES), F32)], axis=0)
        return cos, sin

    a64 = angles(SWA_HEAD_DIM)
    c64 = jnp.tile(jnp.cos(a64), (1, LANES // SWA_HEAD_DIM))
    s64 = jnp.tile(signed(jnp.sin(a64), SWA_HEAD_DIM // 4), (1, LANES // SWA_HEAD_DIM))
    a32 = angles(MLA_ROPE)
    pad_lo = MLA_NOPE
    pad_hi = LANES - MLA_NOPE - MLA_ROPE
    cm = jnp.concatenate([jnp.ones((SEQ, pad_lo), F32), jnp.cos(a32), jnp.ones((SEQ, pad_hi), F32)], axis=-1)
    sm = jnp.concatenate([jnp.zeros((SEQ, pad_lo), F32), signed(jnp.sin(a32), MLA_ROPE // 4),
                          jnp.zeros((SEQ, pad_hi), F32)], axis=-1)
    return with_identity(cm, sm) + with_identity(c64, s64)


def _layer0_weights(wa_in, wa_uq, wa_ukv, wa_out):
    d = D_MODEL
    cq, ckv, kr, qs, ks, vs = jnp.split(
        wa_in, [C_CKV, C_KR, C_KR + MLA_ROPE, C_KR + MLA_ROPE + SWA_HEADS * SWA_HEAD_DIM,
                C_KR + MLA_ROPE + (SWA_HEADS + SWA_KV_HEADS) * SWA_HEAD_DIM], axis=-1)
    kr_pad = jnp.concatenate([jnp.zeros((d, MLA_NOPE), F32), kr,
                              jnp.zeros((d, LANES - MLA_NOPE - MLA_ROPE), F32)], axis=-1)
    qs_pair = qs.reshape(d, SWA_KV_HEADS, SWA_GROUP, SWA_HEAD_DIM).transpose(0, 2, 1, 3).reshape(d, -1)
    win = jnp.concatenate([cq, ckv, kr_pad, qs_pair, ks, vs], axis=-1).astype(BF16)
    uq = wa_uq.reshape(MLA_Q_RANK, MLA_HEADS, MLA_NOPE + MLA_ROPE)
    uq = jnp.pad(uq, ((0, 0), (0, 0), (0, LANES - MLA_NOPE - MLA_ROPE))).reshape(MLA_Q_RANK, -1)
    ukv = wa_ukv.reshape(MLA_KV_RANK, MLA_HEADS, MLA_NOPE + MLA_V)
    uk = jnp.pad(ukv[:, :, :MLA_NOPE], ((0, 0), (0, 0), (0, LANES - MLA_NOPE))).reshape(MLA_KV_RANK, -1)
    uv = ukv[:, :, MLA_NOPE:].reshape(MLA_KV_RANK, -1)
    wukv = jnp.concatenate([uk, uv], axis=-1)
    n_mla = MLA_HEADS * MLA_V
    out_swa = wa_out[n_mla:].reshape(SWA_KV_HEADS, SWA_GROUP, SWA_HEAD_DIM, d).transpose(1, 0, 2, 3)
    wout = jnp.concatenate([wa_out[:n_mla], out_swa.reshape(-1, d)], axis=0)
    return win, uq.astype(BF16), wukv.astype(BF16), wout.astype(BF16)


def _layer1_weights(wc_in):
    per_head = 4 * DIFF_HALF + DIFF_V
    w = wc_in.reshape(D_MODEL, DIFF_HEADS, per_head)
    q = w[:, :, :2 * DIFF_HALF].reshape(D_MODEL, -1)
    k = w[:, :, 2 * DIFF_HALF:4 * DIFF_HALF].reshape(D_MODEL, -1)
    v = w[:, :, 4 * DIFF_HALF:].reshape(D_MODEL, -1)
    return jnp.concatenate([q, k, v], axis=-1).astype(BF16)


def _block_tables(counts, n_blocks):
    padded = (counts + MOE_ROWS - 1) // MOE_ROWS * MOE_ROWS
    pad_end = jnp.cumsum(padded)
    pad_start = (pad_end - padded).astype(jnp.int32)
    block_expert = jnp.minimum(
        jnp.searchsorted(pad_end, jnp.arange(n_blocks, dtype=jnp.int32) * MOE_ROWS, side='right'),
        N_EXPERTS - 1).astype(jnp.int32)
    n_used = (pad_end[-1] // MOE_ROWS).astype(jnp.int32).reshape(1)
    return pad_start, block_expert, n_used


def _moe_layer(f_packed, top_idx, top_w, rank, counts, hs, gtf, g_final, wg, wu, wd, *, final):
    n_tok = top_idx.shape[1]
    n_blocks = -(-(n_tok * TOP_K) // MOE_ROWS) + N_EXPERTS
    pad_start, block_expert, n_used = _block_tables(counts[:, 0], n_blocks)
    dest_flat = _dest(pad_start, top_idx, rank).reshape(-1)
    xs = _sc_dispatch(f_packed, dest_flat, n_blocks * MOE_ROWS)
    ys = _moe(block_expert, n_used, xs, wg, wu, wd)
    y = _sc_gather(ys, dest_flat).reshape(TOP_K, n_tok, ROW_WORDS)
    return _combine(y, top_w.T, hs, gtf, g_final, final=final)


def kernel(x, c, ctx, c_ctx, w_ada, b_ada, g_mix, g_ffn, wa_in, ga_q, ga_kv, wa_uq, wa_ukv, wa_sink, wa_out,
           wc_in, lam_q1, lam_k1, lam_q2, lam_k2, gc_sub, wc_out, w_router, b_router, we_gate, we_up, we_down,
           ws_gate, ws_up, ws_down, g_final):
    d = D_MODEL
    h = jnp.concatenate([x.reshape(N_LAT, d), ctx.reshape(N_CTX, d)], axis=0)
    cc = jnp.concatenate([c, c_ctx[None, :], jnp.zeros((MOD_ROWS - BATCH - 1, d), F32)], axis=0)
    mod = _ada(cc, w_ada, b_ada).reshape(DEPTH, MOD_ROWS, 6, 1, d)

    def mod_vec(layer, j):
        return mod[layer, :, j]

    cm, sm, c64, s64 = _rope_tables()
    g_final2 = g_final.reshape(1, d)

    def router_weights(layer):
        wt = w_router[layer].T
        hi = wt.astype(BF16)
        lo = (wt - hi.astype(F32)).astype(BF16)
        return hi, lo, b_router[layer].reshape(N_EXPERTS, 1)

    def shared_weights(layer):
        return ws_gate[layer].astype(BF16), ws_up[layer].astype(BF16), ws_down[layer].astype(BF16)

    win, wuq, wukv, wout0 = _layer0_weights(wa_in[0], wa_uq[0], wa_ukv[0], wa_out[0])
    qm, km, vm, qs, ks, vs = _proj0(h, g_mix[0].reshape(1, d), mod_vec(0, 0), mod_vec(0, 1), win,
                                    ga_q[0].reshape(1, -1), ga_kv[0].reshape(1, -1), wuq, wukv, cm, sm, c64, s64)
    sink = wa_sink[0].astype(F32)
    o_mla = jnp.concatenate([_mla_attention(qm, km, vm, ctx_queries=False),
                             _mla_attention(qm, km, vm, ctx_queries=True)], axis=0)
    o_swa = jnp.concatenate([_swa_attention(sink, qs, ks, vs, ctx_queries=False),
                             _swa_attention(sink, qs, ks, vs, ctx_queries=True)], axis=0)
    o = jnp.concatenate([o_mla, o_swa], axis=-1)
    hs, f, top_idx, top_w, rank, counts = _post(
        o, h, wout0, mod_vec(0, 2), g_ffn[0].reshape(1, d), mod_vec(0, 3), mod_vec(0, 4), mod_vec(0, 5),
        *router_weights(0), *shared_weights(0), N_TOK)
    h = _moe_layer(f, top_idx, top_w, rank, counts, hs, mod_vec(0, 5), g_final2,
                   we_gate[0], we_up[0], we_down[0], final=False)

    layer = 1
    lam_init = 0.8 - 0.6 * math.exp(-0.3 * layer)
    q1, k1, v1 = _proj1(h, g_mix[1].reshape(1, d), mod_vec(1, 0), mod_vec(1, 1), _layer1_weights(wc_in[0]),
                        c64, s64)
    lam_vecs = jnp.zeros((8, LANES), F32).at[:4, :DIFF_HALF].set(
        jnp.stack([lam_q1[0], lam_k1[0], lam_q2[0], lam_k2[0]]).astype(F32))
    o = _diff_attention(q1, k1, v1, lam_vecs, gc_sub[0].reshape(1, DIFF_V), lam_init)
    hs, f, top_idx, top_w, rank, counts = _post(
        o, h, wc_out[0].astype(BF16), mod_vec(1, 2), g_ffn[1].reshape(1, d), mod_vec(1, 3), mod_vec(1, 4),
        mod_vec(1, 5), *router_weights(1), *shared_weights(1), N_LAT)
    out = _moe_layer(f, top_idx, top_w, rank, counts, hs, mod_vec(1, 5), g_final2,
                     we_gate[1], we_up[1], we_down[1], final=True)
    return out.reshape(BATCH, SEQ, d)
```

```python
import functools
import math

import jax
import jax.numpy as jnp
from jax import lax
from jax.experimental import pallas as pl
from jax.experimental.pallas import tpu as pltpu
from jax.experimental.pallas import tpu_sc as plsc

F32 = jnp.float32
BF16 = jnp.bfloat16

D_MODEL = 1024
BATCH = 8
SEQ = 2048
DEPTH = 2
CTX_LEN = 256
GRID_W = 64
ROPE_BASE = 10000.0
EPS = 1e-6
NEG_INF = -1e30

MLA_HEADS = 8
MLA_Q_RANK = 384
MLA_KV_RANK = 256
MLA_NOPE = 64
MLA_ROPE = 32
MLA_V = 64
SWA_HEADS = 8
SWA_KV_HEADS = 2
SWA_HEAD_DIM = 64
SWA_GROUP = SWA_HEADS // SWA_KV_HEADS
WINDOW = 128
DIFF_HEADS = 8
DIFF_HALF = 64
DIFF_V = 128
N_EXPERTS = 64
N_GROUPS = 8
GROUP_SIZE = N_EXPERTS // N_GROUPS
TOPK_GROUPS = 4
TOP_K = 8
EXPERT_FF = 256
SHARED_FF = 256
ROUTED_SCALE = 2.5

LANES = 128
N_LAT = BATCH * SEQ
N_CTX = BATCH * CTX_LEN
N_TOK = N_LAT + N_CTX
MOD_ROWS = 16

TM = 256
LAT_BLOCKS_PER_BATCH = SEQ // TM
TQ_MLA = 512
TQ_DIFF = 512
SWA_BLOCK = 128
MOE_ROWS = 256
TM_COMB = 256
VMEM_LIMIT = 56 * 1024 * 1024

C_CQ = 0
C_CKV = C_CQ + MLA_Q_RANK
C_KR = C_CKV + MLA_KV_RANK
C_QS = C_KR + LANES
C_KS = C_QS + SWA_HEADS * SWA_HEAD_DIM
C_VS = C_KS + SWA_KV_HEADS * SWA_HEAD_DIM
C_END = C_VS + SWA_KV_HEADS * SWA_HEAD_DIM


def _cparams(*sem):
    return pltpu.CompilerParams(dimension_semantics=sem, vmem_limit_bytes=VMEM_LIMIT)


def _dot(a, b):
    return jnp.dot(a, b, preferred_element_type=F32)


def _dot_nt(a, b):
    return lax.dot_general(a, b, (((1,), (1,)), ((), ())), preferred_element_type=F32)


def _rms(x):
    return x * lax.rsqrt(jnp.mean(x * x, axis=-1, keepdims=True) + EPS)


def _silu(x):
    return x * jax.nn.sigmoid(x)


def _rope(x, cos, sin_signed, shift):
    n = x.shape[-1]
    lane = lax.broadcasted_iota(jnp.int32, x.shape, 1)
    first = (lane & shift) == 0
    rot = jnp.where(first, pltpu.roll(x, n - shift, 1), pltpu.roll(x, shift, 1))
    return x * cos + rot * sin_signed


def _mod_index(i):
    return jnp.minimum(i // LAT_BLOCKS_PER_BATCH, BATCH)


def _rope_index(i):
    return jnp.where(i < N_LAT // TM, i % LAT_BLOCKS_PER_BATCH, LAT_BLOCKS_PER_BATCH)


ADA_TN = 1536


def _ada_kernel(c_ref, w_ref, b_ref, o_ref):
    s = _silu(c_ref[...]).astype(BF16)
    o_ref[0] = _dot(s, w_ref[0].astype(BF16)) + b_ref[0]


def _ada(cc, w_ada, b_ada):
    n_out = w_ada.shape[-1]
    return pl.pallas_call(
        _ada_kernel,
        grid=(DEPTH, n_out // ADA_TN),
        in_specs=[
            pl.BlockSpec((MOD_ROWS, D_MODEL), lambda l, j: (0, 0)),
            pl.BlockSpec((1, D_MODEL, ADA_TN), lambda l, j: (l, 0, j)),
            pl.BlockSpec((1, 1, ADA_TN), lambda l, j: (l, 0, j)),
        ],
        out_specs=pl.BlockSpec((1, MOD_ROWS, ADA_TN), lambda l, j: (l, 0, j)),
        out_shape=jax.ShapeDtypeStruct((DEPTH, MOD_ROWS, n_out), F32),
        compiler_params=_cparams("parallel", "parallel"),
        name="ada",
    )(cc, w_ada, b_ada.reshape(DEPTH, 1, n_out))


def _proj0_kernel(h_ref, g_ref, sh_ref, sc_ref, win_ref, gq_ref, gkv_ref, wuq_ref, wukv_ref,
                  cm_ref, sm_ref, c64_ref, s64_ref,
                  qm_ref, km_ref, vm_ref, qs_ref, ks_ref, vs_ref):
    a = _rms(h_ref[...]) * g_ref[...]
    a = a * (1.0 + sc_ref[0]) + sh_ref[0]
    p = _dot(a.astype(BF16), win_ref[...])
    nq = _rms(p[:, C_CQ:C_CKV]) * gq_ref[...]
    nkv = _rms(p[:, C_CKV:C_KR]) * gkv_ref[...]
    q = _dot(nq.astype(BF16), wuq_ref[...])
    kv = _dot(nkv.astype(BF16), wukv_ref[...])
    cm, sm = cm_ref[...], sm_ref[...]
    c64, s64 = c64_ref[...], s64_ref[...]
    kr = _rope(p[:, C_KR:C_QS], cm, sm, MLA_ROPE // 4)
    q_scale = (MLA_NOPE + MLA_ROPE) ** -0.5
    for h in range(MLA_HEADS):
        sl = slice(h * LANES, (h + 1) * LANES)
        qm_ref[:, sl] = (_rope(q[:, sl], cm, sm, MLA_ROPE // 4) * q_scale).astype(BF16)
        km_ref[:, sl] = (kv[:, sl] + kr).astype(BF16)
    vm_ref[...] = kv[:, MLA_HEADS * LANES:].astype(BF16)
    s_scale = SWA_HEAD_DIM ** -0.5
    for g in range(SWA_GROUP):
        sl = slice(g * LANES, (g + 1) * LANES)
        qs_ref[:, sl] = (_rope(p[:, C_QS + g * LANES:C_QS + (g + 1) * LANES], c64, s64,
                               SWA_HEAD_DIM // 4) * s_scale).astype(BF16)
    ks_ref[...] = _rope(p[:, C_KS:C_VS], c64, s64, SWA_HEAD_DIM // 4).astype(BF16)
    vs_ref[...] = p[:, C_VS:C_END].astype(BF16)


def _proj0(h, g_mix, sh, sc, win, gq, gkv, wuq, wukv, cm, sm, c64, s64):
    row = lambda i: (i, 0)
    const = lambda i: (0, 0)
    mod = lambda i: (_mod_index(i), 0, 0)
    rope = lambda i: (_rope_index(i), 0)
    widths = (MLA_HEADS * LANES, MLA_HEADS * LANES, MLA_HEADS * MLA_V,
              SWA_HEADS * SWA_HEAD_DIM, SWA_KV_HEADS * SWA_HEAD_DIM, SWA_KV_HEADS * SWA_HEAD_DIM)
    return pl.pallas_call(
        _proj0_kernel,
        grid=(N_TOK // TM,),
        in_specs=[
            pl.BlockSpec((TM, D_MODEL), row),
            pl.BlockSpec((1, D_MODEL), const),
            pl.BlockSpec((1, 1, D_MODEL), mod),
            pl.BlockSpec((1, 1, D_MODEL), mod),
            pl.BlockSpec(win.shape, const),
            pl.BlockSpec(gq.shape, const),
            pl.BlockSpec(gkv.shape, const),
            pl.BlockSpec(wuq.shape, const),
            pl.BlockSpec(wukv.shape, const),
            pl.BlockSpec((TM, LANES), rope),
            pl.BlockSpec((TM, LANES), rope),
            pl.BlockSpec((TM, LANES), rope),
            pl.BlockSpec((TM, LANES), rope),
        ],
        out_specs=[pl.BlockSpec((TM, w), row) for w in widths],
        out_shape=[jax.ShapeDtypeStruct((N_TOK, w), BF16) for w in widths],
        compiler_params=_cparams("parallel"),
        name="proj0",
    )(h, g_mix, sh, sc, win, gq, gkv, wuq, wukv, cm, sm, c64, s64)


def _proj1_kernel(h_ref, g_ref, sh_ref, sc_ref, w_ref, c64_ref, s64_ref, q_ref, k_ref, v_ref):
    a = _rms(h_ref[...]) * g_ref[...]
    a = a * (1.0 + sc_ref[0]) + sh_ref[0]
    p = _dot(a.astype(BF16), w_ref[...])
    c64, s64 = c64_ref[...], s64_ref[...]
    width = DIFF_HEADS * LANES
    scale = DIFF_HALF ** -0.5
    for h in range(DIFF_HEADS):
        sl = slice(h * LANES, (h + 1) * LANES)
        q_ref[:, sl] = (_rope(p[:, sl], c64, s64, DIFF_HALF // 4) * scale).astype(BF16)
        k_ref[:, sl] = _rope(p[:, width + h * LANES:width + (h + 1) * LANES], c64, s64,
                             DIFF_HALF // 4).astype(BF16)
    v_ref[...] = p[:, 2 * width:].astype(BF16)


def _proj1(h, g_mix, sh, sc, w, c64, s64):
    row = lambda i: (i, 0)
    const = lambda i: (0, 0)
    mod = lambda i: (_mod_index(i), 0, 0)
    rope = lambda i: (_rope_index(i), 0)
    width = DIFF_HEADS * LANES
    return pl.pallas_call(
        _proj1_kernel,
        grid=(N_TOK // TM,),
        in_specs=[
            pl.BlockSpec((TM, D_MODEL), row),
            pl.BlockSpec((1, D_MODEL), const),
            pl.BlockSpec((1, 1, D_MODEL), mod),
            pl.BlockSpec((1, 1, D_MODEL), mod),
            pl.BlockSpec(w.shape, const),
            pl.BlockSpec((TM, LANES), rope),
            pl.BlockSpec((TM, LANES), rope),
        ],
        out_specs=[pl.BlockSpec((TM, width), row)] * 3,
        out_shape=[jax.ShapeDtypeStruct((N_TOK, width), BF16)] * 3,
        compiler_params=_cparams("parallel"),
        name="proj1",
    )(h, g_mix, sh, sc, w, c64, s64)


def _mla_kernel(*refs, n_kv):
    q_ref = refs[0]
    k_refs = refs[1:1 + n_kv]
    v_refs = refs[1 + n_kv:1 + 2 * n_kv]
    o_ref = refs[1 + 2 * n_kv]
    outs = []
    for hh in range(2):
        sl = slice(hh * LANES, (hh + 1) * LANES)
        qh = q_ref[:, sl]
        ss = [_dot_nt(qh, k[:, sl]) for k in k_refs]
        m = functools.reduce(jnp.maximum, [jnp.max(s, axis=-1, keepdims=True) for s in ss])
        es = [jnp.exp(s - m) for s in ss]
        l = functools.reduce(jnp.add, [jnp.sum(e, axis=-1, keepdims=True) for e in es])
        o = functools.reduce(jnp.add, [_dot(e.astype(BF16), v[...]) for e, v in zip(es, v_refs)])
        outs.append(o / l)
    lane = lax.broadcasted_iota(jnp.int32, outs[0].shape, 1)
    o_ref[...] = jnp.where(lane < MLA_V, outs[0], outs[1]).astype(BF16)


def _into_buffer(kernel_fn, buf):
    if buf is None:
        return kernel_fn, [], []

    def without_alias_ref(*refs):
        return kernel_fn(*refs[:-2], refs[-1])

    return without_alias_ref, [pl.BlockSpec(memory_space=pl.ANY)], [buf]


def _mla_attention(qm, km, vm, buf, *, ctx_queries):
    pairs = MLA_HEADS // 2
    if ctx_queries:
        tq, n_q = CTX_LEN, 1
        q_map = lambda b, h, i: (N_LAT // CTX_LEN + b, h)
        kv_specs = [(CTX_LEN, lambda b, h, i: (N_LAT // CTX_LEN + b, h))]
    else:
        tq, n_q = TQ_MLA, SEQ // TQ_MLA
        q_map = lambda b, h, i: (b * n_q + i, h)
        kv_specs = [(CTX_LEN, lambda b, h, i: (N_LAT // CTX_LEN + b, h)), (SEQ, lambda b, h, i: (b, h))]
    n_kv = len(kv_specs)
    in_specs = [pl.BlockSpec((tq, 2 * LANES), q_map)]
    in_specs += [pl.BlockSpec((n, 2 * LANES), m) for n, m in kv_specs]
    in_specs += [pl.BlockSpec((n, LANES), m) for n, m in kv_specs]
    kernel_fn, extra_specs, extra_args = _into_buffer(functools.partial(_mla_kernel, n_kv=n_kv), buf)
    n_in = len(in_specs)
    return pl.pallas_call(
        kernel_fn,
        grid=(BATCH, pairs, n_q),
        in_specs=in_specs + extra_specs,
        out_specs=pl.BlockSpec((tq, LANES), q_map),
        out_shape=jax.ShapeDtypeStruct((N_TOK, D_MODEL), BF16),
        input_output_aliases={n_in: 0} if extra_args else {},
        compiler_params=_cparams("parallel", "parallel", "parallel"),
        name="mla_ctx" if ctx_queries else "mla_lat",
    )(qm, *([km] * n_kv), *([vm] * n_kv), *extra_args)


def _swa_kernel(sink_ref, *refs, band, n_blocks):
    q_ref = refs[0]
    rows = q_ref.shape[0]
    stacked = SWA_GROUP * rows
    if band:
        kp, kc, kn, kx, vp, vc, vn, vx, o_ref = refs[1:]
        n = pl.program_id(1)
        k_band = jnp.concatenate([kp[...], kc[...], kn[...]], axis=0)
        v_band = jnp.concatenate([vp[...], vc[...], vn[...]], axis=0)
        qq = lax.broadcasted_iota(jnp.int32, (stacked, 3 * SWA_BLOCK), 0) & (SWA_BLOCK - 1)
        kk = lax.broadcasted_iota(jnp.int32, (stacked, 3 * SWA_BLOCK), 1)
        rel = kk - SWA_BLOCK - qq
        key_pos = (n - 1) * SWA_BLOCK + kk
        valid = (jnp.abs(rel) <= WINDOW) & (key_pos >= 0) & (key_pos < n_blocks * SWA_BLOCK)
    else:
        kx, vx, o_ref = refs[1:]
    k_ctx = kx[...]
    v_ctx = vx[...]
    lane = lax.broadcasted_iota(jnp.int32, (rows, LANES), 1)
    low = lane < SWA_HEAD_DIM
    row_group = lax.broadcasted_iota(jnp.int32, (stacked, 1), 0) // rows
    halves = []
    for hk in range(SWA_KV_HEADS):
        keep = low if hk == 0 else jnp.logical_not(low)
        qh = jnp.concatenate(
            [jnp.where(keep, q_ref[:, g * LANES:(g + 1) * LANES], jnp.zeros((rows, LANES), BF16))
             for g in range(SWA_GROUP)], axis=0)
        sink = jnp.zeros((stacked, 1), F32)
        for g in range(SWA_GROUP):
            sink = jnp.where(row_group == g, sink_ref[hk * SWA_GROUP + g], sink)
        s_ctx = _dot_nt(qh, k_ctx)
        m = jnp.maximum(jnp.max(s_ctx, axis=-1, keepdims=True), sink)
        if band:
            s_band = jnp.where(valid, _dot_nt(qh, k_band), NEG_INF)
            m = jnp.maximum(m, jnp.max(s_band, axis=-1, keepdims=True))
        e_ctx = jnp.exp(s_ctx - m)
        denom = jnp.sum(e_ctx, axis=-1, keepdims=True) + jnp.exp(sink - m)
        if band:
            e_band = jnp.exp(s_band - m)
            denom = denom + jnp.sum(e_band, axis=-1, keepdims=True)
        inv = 1.0 / denom
        o = _dot((e_ctx * inv).astype(BF16), v_ctx)
        if band:
            o = o + _dot((e_band * inv).astype(BF16), v_band)
        halves.append(o)
    for g in range(SWA_GROUP):
        rs = slice(g * rows, (g + 1) * rows)
        o_ref[:, g * LANES:(g + 1) * LANES] = jnp.where(low, halves[0][rs], halves[1][rs]).astype(BF16)


def _swa_attention(sink, qs, ks, vs, buf, *, ctx_queries):
    width = SWA_HEADS * SWA_HEAD_DIM
    kvw = SWA_KV_HEADS * SWA_HEAD_DIM
    ctx_map = lambda b, i: (N_LAT // CTX_LEN + b, 0)
    smem = pl.BlockSpec(memory_space=pltpu.SMEM)
    if ctx_queries:
        grid = (BATCH, 1)
        in_specs = [smem, pl.BlockSpec((CTX_LEN, width), ctx_map),
                    pl.BlockSpec((CTX_LEN, kvw), ctx_map), pl.BlockSpec((CTX_LEN, kvw), ctx_map)]
        args = (sink, qs, ks, vs)
        out_spec = pl.BlockSpec((CTX_LEN, width), lambda b, i: (N_LAT // CTX_LEN + b, 1))
        n_blocks = 1
    else:
        n_blocks = SEQ // SWA_BLOCK
        grid = (BATCH, n_blocks)
        cur = lambda b, i: (b * n_blocks + i, 0)
        prev = lambda b, i: (b * n_blocks + jnp.maximum(i - 1, 0), 0)
        nxt = lambda b, i: (b * n_blocks + jnp.minimum(i + 1, n_blocks - 1), 0)
        band_specs = [pl.BlockSpec((SWA_BLOCK, kvw), m) for m in (prev, cur, nxt)]
        in_specs = ([smem, pl.BlockSpec((SWA_BLOCK, width), cur)]
                    + band_specs + [pl.BlockSpec((CTX_LEN, kvw), ctx_map)]
                    + band_specs + [pl.BlockSpec((CTX_LEN, kvw), ctx_map)])
        args = (sink, qs, ks, ks, ks, ks, vs, vs, vs, vs)
        out_spec = pl.BlockSpec((SWA_BLOCK, width), lambda b, i: (b * n_blocks + i, 1))
    kernel_fn, extra_specs, extra_args = _into_buffer(
        functools.partial(_swa_kernel, band=not ctx_queries, n_blocks=n_blocks), buf)
    return pl.pallas_call(
        kernel_fn,
        grid=grid,
        in_specs=in_specs + extra_specs,
        out_specs=out_spec,
        out_shape=jax.ShapeDtypeStruct((N_TOK, D_MODEL), BF16),
        input_output_aliases={len(in_specs): 0} if extra_args else {},
        compiler_params=_cparams("parallel", "parallel"),
        name="swa_ctx" if ctx_queries else "swa_lat",
    )(*args, *extra_args)


def _diff_kernel(q_ref, kc_ref, kl_ref, vc_ref, vl_ref, lam_ref, g_ref, o_ref, *, lam_init):
    lam = (jnp.exp(jnp.sum(lam_ref[0:1, :] * lam_ref[1:2, :], axis=-1, keepdims=True))
           - jnp.exp(jnp.sum(lam_ref[2:3, :] * lam_ref[3:4, :], axis=-1, keepdims=True)) + lam_init)
    q = q_ref[...]
    lane = lax.broadcasted_iota(jnp.int32, q.shape, 1)
    low = lane < DIFF_HALF
    zero = jnp.zeros_like(q)
    k_refs = (kc_ref, kl_ref)
    v_refs = (vc_ref, vl_ref)

    def softmax(qh):
        ss = [_dot_nt(qh, k[...]) for k in k_refs]
        m = functools.reduce(jnp.maximum, [jnp.max(s, axis=-1, keepdims=True) for s in ss])
        es = [jnp.exp(s - m) for s in ss]
        l = functools.reduce(jnp.add, [jnp.sum(e, axis=-1, keepdims=True) for e in es])
        return es, 1.0 / l

    e1, inv1 = softmax(jnp.where(low, q, zero))
    e2, inv2 = softmax(jnp.where(low, zero, q))
    inv2 = inv2 * lam
    o = functools.reduce(jnp.add, [_dot((a * inv1 - b * inv2).astype(BF16), v[...])
                                   for a, b, v in zip(e1, e2, v_refs)])
    o = _rms(o) * g_ref[...] * (1.0 - lam_init)
    o_ref[...] = o.astype(BF16)


def _diff_attention(q, k, v, lam_vecs, g_sub, lam_init):
    n_q = SEQ // TQ_DIFF
    q_map = lambda b, h, i: (b * n_q + i, h)
    ctx_map = lambda b, h, i: (N_LAT // CTX_LEN + b, h)
    lat_map = lambda b, h, i: (b, h)
    const = lambda b, h, i: (0, 0)
    return pl.pallas_call(
        functools.partial(_diff_kernel, lam_init=lam_init),
        grid=(BATCH, DIFF_HEADS, n_q),
        in_specs=[
            pl.BlockSpec((TQ_DIFF, LANES), q_map),
            pl.BlockSpec((CTX_LEN, LANES), ctx_map),
            pl.BlockSpec((SEQ, LANES), lat_map),
            pl.BlockSpec((CTX_LEN, LANES), ctx_map),
            pl.BlockSpec((SEQ, LANES), lat_map),
            pl.BlockSpec(lam_vecs.shape, const),
            pl.BlockSpec(g_sub.shape, const),
        ],
        out_specs=pl.BlockSpec((TQ_DIFF, LANES), q_map),
        out_shape=jax.ShapeDtypeStruct((N_LAT, DIFF_HEADS * DIFF_V), BF16),
        compiler_params=_cparams("parallel", "parallel", "parallel"),
        name="diff_attn",
    )(q, k, k, v, v, lam_vecs, g_sub)


def _pack_bf16_pairs(x):
    n = x.shape[1] // 2
    lo = pltpu.bitcast(x[:, :n].astype(BF16).astype(F32), jnp.uint32)
    hi = pltpu.bitcast(x[:, n:].astype(BF16).astype(F32), jnp.uint32)
    return pltpu.bitcast((lo >> 16) | hi, jnp.int32)


def _unpack_bf16_pairs(w):
    u = pltpu.bitcast(w, jnp.uint32)
    lo = pltpu.bitcast(u << 16, F32)
    hi = pltpu.bitcast(u & jnp.uint32(0xFFFF0000), F32)
    return lo, hi


def _route(logits_t, bias, cnt_ref, tri_ref, idx_ref, w_ref, rank_ref):
    tm = logits_t.shape[1]
    scores = jax.nn.sigmoid(logits_t)
    biased = scores + bias
    sub = lax.broadcasted_iota(jnp.int32, (GROUP_SIZE, tm), 0).astype(F32)
    grp_scores, grp_biased, grp_index = [], [], []
    group_score = []
    for g in range(N_GROUPS):
        sl = slice(g * GROUP_SIZE, (g + 1) * GROUP_SIZE)
        bg = biased[sl, :]
        grp_scores.append(scores[sl, :])
        grp_biased.append(bg)
        grp_index.append(sub + float(g * GROUP_SIZE))
        m1 = jnp.max(bg, axis=0, keepdims=True)
        first = jnp.min(jnp.where(bg == m1, sub, float(GROUP_SIZE)), axis=0, keepdims=True)
        m2 = jnp.max(jnp.where(sub == first, -jnp.inf, bg), axis=0, keepdims=True)
        group_score.append(m1 + m2)
    keep = [jnp.zeros((1, tm), F32) for _ in range(N_GROUPS)]
    for _ in range(TOPK_GROUPS):
        m = functools.reduce(jnp.maximum, group_score)
        found = jnp.zeros((1, tm), F32)
        for g in range(N_GROUPS):
            hit = jnp.where(group_score[g] == m, 1.0 - found, 0.0)
            keep[g] = keep[g] + hit
            found = found + hit
            group_score[g] = jnp.where(hit > 0.0, -jnp.inf, group_score[g])
    vals = [jnp.where(keep[g] > 0.0, grp_biased[g], NEG_INF) for g in range(N_GROUPS)]
    chosen = [jnp.zeros((GROUP_SIZE, tm), F32) for _ in range(N_GROUPS)]
    picked = []
    for _ in range(TOP_K):
        m = jnp.max(functools.reduce(jnp.maximum, vals), axis=0, keepdims=True)
        cand = [jnp.where(vals[g] == m, grp_index[g], float(N_EXPERTS)) for g in range(N_GROUPS)]
        ei = jnp.min(functools.reduce(jnp.minimum, cand), axis=0, keepdims=True)
        sel = [grp_index[g] == ei for g in range(N_GROUPS)]
        s_k = functools.reduce(jnp.add, [jnp.where(sel[g], grp_scores[g], 0.0) for g in range(N_GROUPS)])
        picked.append((ei, jnp.sum(s_k, axis=0, keepdims=True)))
        vals = [jnp.where(sel[g], -jnp.inf, vals[g]) for g in range(N_GROUPS)]
        chosen = [jnp.where(sel[g], 1.0, chosen[g]) for g in range(N_GROUPS)]
    total = functools.reduce(jnp.add, [s for _, s in picked])
    chosen_all = jnp.concatenate(chosen, axis=0)
    rank_all = _dot(chosen_all.astype(BF16), tri_ref[...]) + cnt_ref[...]
    cnt_ref[...] = cnt_ref[...] + jnp.sum(chosen_all, axis=1, keepdims=True)
    for k, (ei, s) in enumerate(picked):
        idx_ref[k:k + 1, :] = ei.astype(jnp.int32)
        w_ref[k:k + 1, :] = s / total * ROUTED_SCALE
        r = functools.reduce(jnp.add, [
            jnp.where(grp_index[g] == ei, rank_all[g * GROUP_SIZE:(g + 1) * GROUP_SIZE, :], 0.0)
            for g in range(N_GROUPS)])
        rank_ref[k:k + 1, :] = jnp.sum(r, axis=0, keepdims=True).astype(jnp.int32)


def _post_kernel(o_ref, h_ref, wout_ref, gtm_ref, g_ref, sh_ref, sc_ref, gtf_ref,
                 wrh_ref, wrl_ref, br_ref, tri_ref, wsg_ref, wsu_ref, wsd_ref,
                 hs_ref, f_ref, idx_ref, w_ref, rank_ref, cnt_out_ref, cnt_ref):
    @pl.when(pl.program_id(0) == 0)
    def _():
        cnt_ref[...] = jnp.zeros_like(cnt_ref)

    h1 = h_ref[...] + gtm_ref[0] * _dot(o_ref[...], wout_ref[...])
    f = _rms(h1) * g_ref[...]
    f = f * (1.0 + sc_ref[0]) + sh_ref[0]
    f_hi = f.astype(BF16)
    f_lo = (f - f_hi.astype(F32)).astype(BF16)
    f_ref[...] = _pack_bf16_pairs(f)
    logits_t = (_dot_nt(wrh_ref[...], f_hi) + _dot_nt(wrh_ref[...], f_lo) + _dot_nt(wrl_ref[...], f_hi))
    _route(logits_t, br_ref[...], cnt_ref, tri_ref, idx_ref, w_ref, rank_ref)
    cnt_out_ref[...] = jnp.broadcast_to(cnt_ref[...], cnt_out_ref.shape).astype(jnp.int32)
    mid = _silu(_dot(f_hi, wsg_ref[...])) * _dot(f_hi, wsu_ref[...])
    shared = _dot(mid.astype(BF16), wsd_ref[...])
    hs_ref[...] = h1 + gtf_ref[0] * shared


def _post(o, h, wout, gtm, g_ffn, sh, sc, gtf, wrh, wrl, br, wsg, wsu, wsd, n_rows):
    row = lambda i: (i, 0)
    col = lambda i: (0, i)
    const = lambda i: (0, 0)
    mod = lambda i: (_mod_index(i), 0, 0)
    vec = pl.BlockSpec((1, 1, D_MODEL), mod)
    full = lambda a: pl.BlockSpec(a.shape, const)
    tri = (jnp.arange(TM)[:, None] < jnp.arange(TM)[None, :]).astype(BF16)
    return pl.pallas_call(
        _post_kernel,
        grid=(n_rows // TM,),
        in_specs=[
            pl.BlockSpec((TM, D_MODEL), row),
            pl.BlockSpec((TM, D_MODEL), row),
            full(wout), vec, full(g_ffn), vec, vec, vec,
            full(wrh), full(wrl), full(br), full(tri), full(wsg), full(wsu), full(wsd),
        ],
        out_specs=[
            pl.BlockSpec((TM, D_MODEL), row),
            pl.BlockSpec((TM, D_MODEL // 2), row),
            pl.BlockSpec((TOP_K, TM), col),
            pl.BlockSpec((TOP_K, TM), col),
            pl.BlockSpec((TOP_K, TM), col),
            pl.BlockSpec((N_EXPERTS, LANES), const),
        ],
        out_shape=[
            jax.ShapeDtypeStruct((n_rows, D_MODEL), F32),
            jax.ShapeDtypeStruct((n_rows, D_MODEL // 2), jnp.int32),
            jax.ShapeDtypeStruct((TOP_K, n_rows), jnp.int32),
            jax.ShapeDtypeStruct((TOP_K, n_rows), F32),
            jax.ShapeDtypeStruct((TOP_K, n_rows), jnp.int32),
            jax.ShapeDtypeStruct((N_EXPERTS, LANES), jnp.int32),
        ],
        scratch_shapes=[pltpu.VMEM((N_EXPERTS, 1), F32)],
        compiler_params=_cparams("arbitrary"),
        name="post_attn",
    )(o, h, wout, gtm, g_ffn, sh, sc, gtf, wrh, wrl, br, tri, wsg, wsu, wsd)


DEST_TN = 2048


def _dest_kernel(ps_ref, idx_ref, rank_ref, o_ref):
    idx = idx_ref[...]
    acc = rank_ref[...]
    for e in range(N_EXPERTS):
        acc = acc + jnp.where(idx == e, ps_ref[e], 0)
    o_ref[...] = acc


def _dest(pad_start, top_idx, rank):
    n_tok = top_idx.shape[1]
    blk = pl.BlockSpec((TOP_K, DEST_TN), lambda i: (0, i))
    return pl.pallas_call(
        _dest_kernel,
        grid=(n_tok // DEST_TN,),
        in_specs=[pl.BlockSpec(memory_space=pltpu.SMEM), blk, blk],
        out_specs=blk,
        out_shape=jax.ShapeDtypeStruct((TOP_K, n_tok), jnp.int32),
        compiler_params=_cparams("parallel"),
        name="dest",
    )(pad_start, top_idx, rank)


SC_CORES = 2
SC_SUBCORES = 16
SC_WORKERS = SC_CORES * SC_SUBCORES
SC_ROWS = 64
ROW_WORDS = D_MODEL // 2


def _sc_mesh():
    return plsc.VectorSubcoreMesh(core_axis_name="core", subcore_axis_name="subcore")


def _sc_worker():
    return lax.axis_index("subcore") * SC_CORES + lax.axis_index("core")


def _sc_dispatch(f_packed, dest_flat, n_slots):
    n_tok = f_packed.shape[0]
    per_worker = n_tok // SC_WORKERS
    n_chunks = per_worker // SC_ROWS
    assert n_chunks * SC_ROWS * SC_WORKERS == n_tok

    def body(f_hbm, dest_hbm, xs_hbm, rows_v, *rest):
        idx_v, sem = rest[:TOP_K], rest[TOP_K]
        wid = _sc_worker()

        @pl.loop(0, n_chunks)
        def _(c):
            base = pl.multiple_of(wid * per_worker + c * SC_ROWS, 8)
            pltpu.sync_copy(f_hbm.at[pl.ds(base, SC_ROWS)], rows_v)
            for k in range(TOP_K):
                pltpu.sync_copy(dest_hbm.at[pl.ds(pl.multiple_of(k * n_tok + base, 8), SC_ROWS)], idx_v[k])
            copies = [pltpu.async_copy(rows_v, xs_hbm.at[idx_v[k]], sem) for k in range(TOP_K)]
            for cp in copies:
                cp.wait()

    return pl.kernel(
        body,
        out_type=jax.ShapeDtypeStruct((n_slots, ROW_WORDS), jnp.int32),
        mesh=_sc_mesh(),
        scratch_types=([pltpu.VMEM((SC_ROWS, ROW_WORDS), jnp.int32)]
                       + [pltpu.VMEM((SC_ROWS,), jnp.int32) for _ in range(TOP_K)]
                       + [pltpu.SemaphoreType.DMA]),
        name="sc_dispatch",
    )(f_packed, dest_flat)


def _sc_gather(ys, dest_flat):
    n_idx = dest_flat.shape[0]
    per_worker = n_idx // SC_WORKERS
    n_chunks = per_worker // SC_ROWS
    assert n_chunks * SC_ROWS * SC_WORKERS == n_idx

    def body(ys_hbm, dest_hbm, out_hbm, rows_v, idx_v, sem):
        wid = _sc_worker()

        @pl.loop(0, n_chunks)
        def _(c):
            base = pl.multiple_of(wid * per_worker + c * SC_ROWS, 8)
            pltpu.sync_copy(dest_hbm.at[pl.ds(base, SC_ROWS)], idx_v)
            pltpu.async_copy(ys_hbm.at[idx_v], rows_v, sem).wait()
            pltpu.sync_copy(rows_v, out_hbm.at[pl.ds(base, SC_ROWS)])

    return pl.kernel(
        body,
        out_type=jax.ShapeDtypeStruct((n_idx, ROW_WORDS), jnp.int32),
        mesh=_sc_mesh(),
        scratch_types=[pltpu.VMEM((SC_ROWS, ROW_WORDS), jnp.int32), pltpu.VMEM((SC_ROWS,), jnp.int32),
                       pltpu.SemaphoreType.DMA],
        name="sc_gather",
    )(ys, dest_flat)


def _moe_kernel(be_ref, nused_ref, x_ref, wg_ref, wu_ref, wd_ref, o_ref, wg_s, wu_s, wd_s):
    i = pl.program_id(0)
    e = be_ref[i]
    changed = jnp.logical_or(i == 0, e != be_ref[jnp.maximum(i - 1, 0)])

    @pl.when(changed)
    def _():
        wg_s[...] = wg_ref[0, 0].astype(BF16)
        wu_s[...] = wu_ref[0, 0].astype(BF16)
        wd_s[...] = wd_ref[0, 0].astype(BF16)

    @pl.when(i < nused_ref[0])
    def _():
        lo, hi = _unpack_bf16_pairs(x_ref[...])
        lo, hi = lo.astype(BF16), hi.astype(BF16)
        half = D_MODEL // 2
        gate = _dot(lo, wg_s[:half, :]) + _dot(hi, wg_s[half:, :])
        up = _dot(lo, wu_s[:half, :]) + _dot(hi, wu_s[half:, :])
        mid = _silu(gate) * up
        o_ref[...] = _pack_bf16_pairs(_dot(mid.astype(BF16), wd_s[...]))


def _moe(block_expert, n_used, xs, wg, wu, wd, layer):
    n_blocks = xs.shape[0] // MOE_ROWS
    expert = lambda i, be, nu: (layer, be[i], 0, 0)
    grid_spec = pltpu.PrefetchScalarGridSpec(
        num_scalar_prefetch=2,
        grid=(n_blocks,),
        in_specs=[
            pl.BlockSpec((MOE_ROWS, ROW_WORDS), lambda i, be, nu: (i, 0)),
            pl.BlockSpec((1, 1, D_MODEL, EXPERT_FF), expert),
            pl.BlockSpec((1, 1, D_MODEL, EXPERT_FF), expert),
            pl.BlockSpec((1, 1, EXPERT_FF, D_MODEL), expert),
        ],
        out_specs=pl.BlockSpec((MOE_ROWS, ROW_WORDS), lambda i, be, nu: (i, 0)),
        scratch_shapes=[
            pltpu.VMEM((D_MODEL, EXPERT_FF), BF16),
            pltpu.VMEM((D_MODEL, EXPERT_FF), BF16),
            pltpu.VMEM((EXPERT_FF, D_MODEL), BF16),
        ],
    )
    return pl.pallas_call(
        _moe_kernel,
        grid_spec=grid_spec,
        out_shape=jax.ShapeDtypeStruct((xs.shape[0], ROW_WORDS), jnp.int32),
        compiler_params=_cparams("arbitrary"),
        name="moe_experts",
    )(block_expert, n_used, xs, wg, wu, wd)


def _combine_kernel(y_ref, w_ref, hs_ref, gtf_ref, gfin_ref, o_ref, *, final):
    w = w_ref[...]
    acc_lo = acc_hi = None
    for k in range(TOP_K):
        lo, hi = _unpack_bf16_pairs(y_ref[k])
        wk = w[:, k:k + 1]
        acc_lo = lo * wk if k == 0 else acc_lo + lo * wk
        acc_hi = hi * wk if k == 0 else acc_hi + hi * wk
    out = hs_ref[...] + gtf_ref[0] * jnp.concatenate([acc_lo, acc_hi], axis=1)
    if final:
        out = _rms(out) * gfin_ref[...]
    o_ref[...] = out


def _combine(y, w, hs, gtf, g_final, *, final):
    n_rows = hs.shape[0]
    return pl.pallas_call(
        functools.partial(_combine_kernel, final=final),
        grid=(n_rows // TM_COMB,),
        in_specs=[
            pl.BlockSpec((TOP_K, TM_COMB, ROW_WORDS), lambda i: (0, i, 0)),
            pl.BlockSpec((TM_COMB, TOP_K), lambda i: (i, 0)),
            pl.BlockSpec((TM_COMB, D_MODEL), lambda i: (i, 0)),
            pl.BlockSpec((1, 1, D_MODEL), lambda i: (jnp.minimum(i // (SEQ // TM_COMB), BATCH), 0, 0)),
            pl.BlockSpec((1, D_MODEL), lambda i: (0, 0)),
        ],
        out_specs=pl.BlockSpec((TM_COMB, D_MODEL), lambda i: (i, 0)),
        out_shape=jax.ShapeDtypeStruct((n_rows, D_MODEL), F32),
        compiler_params=_cparams("parallel"),
        name="combine_final" if final else "combine",
    )(y, w, hs, gtf, g_final)


def _rope_tables():
    rows = SEQ // GRID_W

    def angles(rot_dim):
        half = rot_dim // 2
        inv_freq = ROPE_BASE ** (-jnp.arange(0, half, 2, dtype=F32) / half)
        row = jnp.repeat(jnp.arange(rows, dtype=F32), GRID_W)
        col = jnp.tile(jnp.arange(GRID_W, dtype=F32), rows)
        ang_r = row[:, None] * inv_freq
        ang_c = col[:, None] * inv_freq
        return jnp.concatenate([ang_r, ang_r, ang_c, ang_c], axis=-1)

    def signed(sin, quarter):
        sign = jnp.where((jnp.arange(sin.shape[-1]) // quarter) % 2 == 0, -1.0, 1.0)
        return sin * sign

    def with_identity(cos, sin):
        cos = jnp.concatenate([cos, jnp.ones((TM, LANES), F32)], axis=0)
        sin = jnp.concatenate([sin, jnp.zeros((TM, LANES), F32)], axis=0)
        return cos, sin

    a64 = angles(SWA_HEAD_DIM)
    c64 = jnp.tile(jnp.cos(a64), (1, LANES // SWA_HEAD_DIM))
    s64 = jnp.tile(signed(jnp.sin(a64), SWA_HEAD_DIM // 4), (1, LANES // SWA_HEAD_DIM))
    a32 = angles(MLA_ROPE)
    pad_lo = MLA_NOPE
    pad_hi = LANES - MLA_NOPE - MLA_ROPE
    cm = jnp.concatenate([jnp.ones((SEQ, pad_lo), F32), jnp.cos(a32), jnp.ones((SEQ, pad_hi), F32)], axis=-1)
    sm = jnp.concatenate([jnp.zeros((SEQ, pad_lo), F32), signed(jnp.sin(a32), MLA_ROPE // 4),
                          jnp.zeros((SEQ, pad_hi), F32)], axis=-1)
    return with_identity(cm, sm) + with_identity(c64, s64)


def _layer0_weights(wa_in, wa_uq, wa_ukv, wa_out):
    d = D_MODEL
    cq, ckv, kr, qs, ks, vs = jnp.split(
        wa_in, [C_CKV, C_KR, C_KR + MLA_ROPE, C_KR + MLA_ROPE + SWA_HEADS * SWA_HEAD_DIM,
                C_KR + MLA_ROPE + (SWA_HEADS + SWA_KV_HEADS) * SWA_HEAD_DIM], axis=-1)
    kr_pad = jnp.concatenate([jnp.zeros((d, MLA_NOPE), F32), kr,
                              jnp.zeros((d, LANES - MLA_NOPE - MLA_ROPE), F32)], axis=-1)
    qs_pair = qs.reshape(d, SWA_KV_HEADS, SWA_GROUP, SWA_HEAD_DIM).transpose(0, 2, 1, 3).reshape(d, -1)
    win = jnp.concatenate([cq, ckv, kr_pad, qs_pair, ks, vs], axis=-1).astype(BF16)
    uq = wa_uq.reshape(MLA_Q_RANK, MLA_HEADS, MLA_NOPE + MLA_ROPE)
    uq = jnp.pad(uq, ((0, 0), (0, 0), (0, LANES - MLA_NOPE - MLA_ROPE))).reshape(MLA_Q_RANK, -1)
    ukv = wa_ukv.reshape(MLA_KV_RANK, MLA_HEADS, MLA_NOPE + MLA_V)
    uk = jnp.pad(ukv[:, :, :MLA_NOPE], ((0, 0), (0, 0), (0, LANES - MLA_NOPE))).reshape(MLA_KV_RANK, -1)
    uv = ukv[:, :, MLA_NOPE:].reshape(MLA_KV_RANK, -1)
    wukv = jnp.concatenate([uk, uv], axis=-1)
    n_mla = MLA_HEADS * MLA_V
    out_swa = wa_out[n_mla:].reshape(SWA_KV_HEADS, SWA_GROUP, SWA_HEAD_DIM, d).transpose(1, 0, 2, 3)
    wout = jnp.concatenate([wa_out[:n_mla], out_swa.reshape(-1, d)], axis=0)
    return win, uq.astype(BF16), wukv.astype(BF16), wout.astype(BF16)


def _layer1_weights(wc_in):
    per_head = 4 * DIFF_HALF + DIFF_V
    w = wc_in.reshape(D_MODEL, DIFF_HEADS, per_head)
    q = w[:, :, :2 * DIFF_HALF].reshape(D_MODEL, -1)
    k = w[:, :, 2 * DIFF_HALF:4 * DIFF_HALF].reshape(D_MODEL, -1)
    v = w[:, :, 4 * DIFF_HALF:].reshape(D_MODEL, -1)
    return jnp.concatenate([q, k, v], axis=-1).astype(BF16)


def _block_tables(counts, n_blocks):
    padded = (counts + MOE_ROWS - 1) // MOE_ROWS * MOE_ROWS
    pad_end = jnp.cumsum(padded)
    pad_start = (pad_end - padded).astype(jnp.int32)
    block_start = jnp.arange(n_blocks, dtype=jnp.int32) * MOE_ROWS
    block_expert = jnp.minimum((pad_end[None, :] <= block_start[:, None]).sum(axis=1),
                               N_EXPERTS - 1).astype(jnp.int32)
    n_used = (pad_end[-1] // MOE_ROWS).astype(jnp.int32).reshape(1)
    return pad_start, block_expert, n_used


def _moe_layer(f_packed, top_idx, top_w, rank, counts, hs, gtf, g_final, wg, wu, wd, layer, *, final):
    n_tok = top_idx.shape[1]
    n_blocks = -(-(n_tok * TOP_K) // MOE_ROWS) + N_EXPERTS
    pad_start, block_expert, n_used = _block_tables(counts[:, 0], n_blocks)
    dest_flat = _dest(pad_start, top_idx, rank).reshape(-1)
    xs = _sc_dispatch(f_packed, dest_flat, n_blocks * MOE_ROWS)
    ys = _moe(block_expert, n_used, xs, wg, wu, wd, layer)
    y = _sc_gather(ys, dest_flat).reshape(TOP_K, n_tok, ROW_WORDS)
    return _combine(y, top_w.T, hs, gtf, g_final, final=final)


def kernel(x, c, ctx, c_ctx, w_ada, b_ada, g_mix, g_ffn, wa_in, ga_q, ga_kv, wa_uq, wa_ukv, wa_sink, wa_out,
           wc_in, lam_q1, lam_k1, lam_q2, lam_k2, gc_sub, wc_out, w_router, b_router, we_gate, we_up, we_down,
           ws_gate, ws_up, ws_down, g_final):
    d = D_MODEL
    h = jnp.concatenate([x.reshape(N_LAT, d), ctx.reshape(N_CTX, d)], axis=0)
    cc = jnp.concatenate([c, c_ctx[None, :], jnp.zeros((MOD_ROWS - BATCH - 1, d), F32)], axis=0)
    mod = _ada(cc, w_ada, b_ada).reshape(DEPTH, MOD_ROWS, 6, 1, d)

    def mod_vec(layer, j):
        return mod[layer, :, j]

    cm, sm, c64, s64 = _rope_tables()
    g_final2 = g_final.reshape(1, d)

    def router_weights(layer):
        wt = w_router[layer].T
        hi = wt.astype(BF16)
        lo = (wt - hi.astype(F32)).astype(BF16)
        return hi, lo, b_router[layer].reshape(N_EXPERTS, 1)

    def shared_weights(layer):
        return ws_gate[layer].astype(BF16), ws_up[layer].astype(BF16), ws_down[layer].astype(BF16)

    win, wuq, wukv, wout0 = _layer0_weights(wa_in[0], wa_uq[0], wa_ukv[0], wa_out[0])
    qm, km, vm, qs, ks, vs = _proj0(h, g_mix[0].reshape(1, d), mod_vec(0, 0), mod_vec(0, 1), win,
                                    ga_q[0].reshape(1, -1), ga_kv[0].reshape(1, -1), wuq, wukv, cm, sm, c64, s64)
    sink = wa_sink[0].astype(F32)
    o = _mla_attention(qm, km, vm, None, ctx_queries=False)
    o = _mla_attention(qm, km, vm, o, ctx_queries=True)
    o = _swa_attention(sink, qs, ks, vs, o, ctx_queries=False)
    o = _swa_attention(sink, qs, ks, vs, o, ctx_queries=True)
    hs, f, top_idx, top_w, rank, counts = _post(
        o, h, wout0, mod_vec(0, 2), g_ffn[0].reshape(1, d), mod_vec(0, 3), mod_vec(0, 4), mod_vec(0, 5),
        *router_weights(0), *shared_weights(0), N_TOK)
    h = _moe_layer(f, top_idx, top_w, rank, counts, hs, mod_vec(0, 5), g_final2,
                   we_gate, we_up, we_down, 0, final=False)

    layer = 1
    lam_init = 0.8 - 0.6 * math.exp(-0.3 * layer)
    q1, k1, v1 = _proj1(h, g_mix[1].reshape(1, d), mod_vec(1, 0), mod_vec(1, 1), _layer1_weights(wc_in[0]),
                        c64, s64)
    lam_vecs = jnp.zeros((8, LANES), F32).at[:4, :DIFF_HALF].set(
        jnp.stack([lam_q1[0], lam_k1[0], lam_q2[0], lam_k2[0]]).astype(F32))
    o = _diff_attention(q1, k1, v1, lam_vecs, gc_sub[0].reshape(1, DIFF_V), lam_init)
    hs, f, top_idx, top_w, rank, counts = _post(
        o, h, wc_out[0].astype(BF16), mod_vec(1, 2), g_ffn[1].reshape(1, d), mod_vec(1, 3), mod_vec(1, 4),
        mod_vec(1, 5), *router_weights(1), *shared_weights(1), N_LAT)
    out = _moe_layer(f, top_idx, top_w, rank, counts, hs, mod_vec(1, 5), g_final2,
                     we_gate, we_up, we_down, 1, final=True)
    return out.reshape(BATCH, SEQ, d)
```

```python
import functools
import math

import jax
import jax.numpy as jnp
from jax import lax
from jax.experimental import pallas as pl
from jax.experimental.pallas import tpu as pltpu
from jax.experimental.pallas import tpu_sc as plsc

F32 = jnp.float32
BF16 = jnp.bfloat16

D_MODEL = 1024
BATCH = 8
SEQ = 2048
DEPTH = 2
CTX_LEN = 256
GRID_W = 64
ROPE_BASE = 10000.0
EPS = 1e-6
NEG_INF = -1e30

MLA_HEADS = 8
MLA_Q_RANK = 384
MLA_KV_RANK = 256
MLA_NOPE = 64
MLA_ROPE = 32
MLA_V = 64
SWA_HEADS = 8
SWA_KV_HEADS = 2
SWA_HEAD_DIM = 64
SWA_GROUP = SWA_HEADS // SWA_KV_HEADS
WINDOW = 128
DIFF_HEADS = 8
DIFF_HALF = 64
DIFF_V = 128
N_EXPERTS = 64
N_GROUPS = 8
GROUP_SIZE = N_EXPERTS // N_GROUPS
TOPK_GROUPS = 4
TOP_K = 8
EXPERT_FF = 256
SHARED_FF = 256
ROUTED_SCALE = 2.5

LOG2E = math.log2(math.e)
LANES = 128
N_LAT = BATCH * SEQ
N_CTX = BATCH * CTX_LEN
N_TOK = N_LAT + N_CTX
MOD_ROWS = 16

TM = 256
LAT_BLOCKS_PER_BATCH = SEQ // TM
TQ_MLA = 512
MLA_PAIRS_PER_STEP = 4
TQ_DIFF = 512
DIFF_HEADS_PER_STEP = 4
SWA_BLOCK = 128
MOE_ROWS = 256
TM_COMB = 256
VMEM_LIMIT = 56 * 1024 * 1024

C_CQ = 0
C_CKV = C_CQ + MLA_Q_RANK
C_KR = C_CKV + MLA_KV_RANK
C_QS = C_KR + LANES
C_KS = C_QS + SWA_HEADS * SWA_HEAD_DIM
C_VS = C_KS + SWA_KV_HEADS * SWA_HEAD_DIM
C_END = C_VS + SWA_KV_HEADS * SWA_HEAD_DIM


def _cparams(*sem):
    return pltpu.CompilerParams(dimension_semantics=sem, vmem_limit_bytes=VMEM_LIMIT)


def _dot(a, b):
    return jnp.dot(a, b, preferred_element_type=F32)


def _dot_nt(a, b):
    return lax.dot_general(a, b, (((1,), (1,)), ((), ())), preferred_element_type=F32)


def _rms(x):
    return x * lax.rsqrt(jnp.mean(x * x, axis=-1, keepdims=True) + EPS)


def _silu(x):
    return x * jax.nn.sigmoid(x)


def _rope(x, cos, sin_signed, shift):
    n = x.shape[-1]
    lane = lax.broadcasted_iota(jnp.int32, x.shape, 1)
    first = (lane & shift) == 0
    rot = jnp.where(first, pltpu.roll(x, n - shift, 1), pltpu.roll(x, shift, 1))
    return x * cos + rot * sin_signed


def _mod_index(i):
    return jnp.minimum(i // LAT_BLOCKS_PER_BATCH, BATCH)


def _rope_index(i):
    return jnp.where(i < N_LAT // TM, i % LAT_BLOCKS_PER_BATCH, LAT_BLOCKS_PER_BATCH)


ADA_TN = 1536


def _ada_kernel(c_ref, w_ref, b_ref, o_ref):
    s = _silu(c_ref[...]).astype(BF16)
    o_ref[0] = _dot(s, w_ref[0].astype(BF16)) + b_ref[0]


def _ada(cc, w_ada, b_ada):
    n_out = w_ada.shape[-1]
    return pl.pallas_call(
        _ada_kernel,
        grid=(DEPTH, n_out // ADA_TN),
        in_specs=[
            pl.BlockSpec((MOD_ROWS, D_MODEL), lambda l, j: (0, 0)),
            pl.BlockSpec((1, D_MODEL, ADA_TN), lambda l, j: (l, 0, j)),
            pl.BlockSpec((1, 1, ADA_TN), lambda l, j: (l, 0, j)),
        ],
        out_specs=pl.BlockSpec((1, MOD_ROWS, ADA_TN), lambda l, j: (l, 0, j)),
        out_shape=jax.ShapeDtypeStruct((DEPTH, MOD_ROWS, n_out), F32),
        compiler_params=_cparams("parallel", "parallel"),
        name="ada",
    )(cc, w_ada, b_ada.reshape(DEPTH, 1, n_out))


def _proj0_kernel(h_ref, g_ref, sh_ref, sc_ref, win_ref, gq_ref, gkv_ref, wuq_ref, wukv_ref,
                  cm_ref, sm_ref, c64_ref, s64_ref,
                  qm_ref, km_ref, vm_ref, qs_ref, ks_ref, vs_ref):
    a = _rms(h_ref[...]) * g_ref[...]
    a = a * (1.0 + sc_ref[0]) + sh_ref[0]
    p = _dot(a.astype(BF16), win_ref[...])
    nq = _rms(p[:, C_CQ:C_CKV]) * gq_ref[...]
    nkv = _rms(p[:, C_CKV:C_KR]) * gkv_ref[...]
    q = _dot(nq.astype(BF16), wuq_ref[...])
    kv = _dot(nkv.astype(BF16), wukv_ref[...])
    cm, sm = cm_ref[...], sm_ref[...]
    c64, s64 = c64_ref[...], s64_ref[...]
    kr = _rope(p[:, C_KR:C_QS], cm, sm, MLA_ROPE // 4)
    q_scale = (MLA_NOPE + MLA_ROPE) ** -0.5 * LOG2E
    for h in range(MLA_HEADS):
        sl = slice(h * LANES, (h + 1) * LANES)
        qm_ref[:, sl] = (_rope(q[:, sl], cm, sm, MLA_ROPE // 4) * q_scale).astype(BF16)
        km_ref[:, sl] = (kv[:, sl] + kr).astype(BF16)
    vm_ref[...] = kv[:, MLA_HEADS * LANES:].astype(BF16)
    s_scale = SWA_HEAD_DIM ** -0.5 * LOG2E
    for g in range(SWA_GROUP):
        sl = slice(g * LANES, (g + 1) * LANES)
        qs_ref[:, sl] = (_rope(p[:, C_QS + g * LANES:C_QS + (g + 1) * LANES], c64, s64,
                               SWA_HEAD_DIM // 4) * s_scale).astype(BF16)
    ks_ref[...] = _rope(p[:, C_KS:C_VS], c64, s64, SWA_HEAD_DIM // 4).astype(BF16)
    vs_ref[...] = p[:, C_VS:C_END].astype(BF16)


def _proj0(h, g_mix, sh, sc, win, gq, gkv, wuq, wukv, cm, sm, c64, s64):
    row = lambda i: (i, 0)
    const = lambda i: (0, 0)
    mod = lambda i: (_mod_index(i), 0, 0)
    rope = lambda i: (_rope_index(i), 0)
    widths = (MLA_HEADS * LANES, MLA_HEADS * LANES, MLA_HEADS * MLA_V,
              SWA_HEADS * SWA_HEAD_DIM, SWA_KV_HEADS * SWA_HEAD_DIM, SWA_KV_HEADS * SWA_HEAD_DIM)
    return pl.pallas_call(
        _proj0_kernel,
        grid=(N_TOK // TM,),
        in_specs=[
            pl.BlockSpec((TM, D_MODEL), row),
            pl.BlockSpec((1, D_MODEL), const),
            pl.BlockSpec((1, 1, D_MODEL), mod),
            pl.BlockSpec((1, 1, D_MODEL), mod),
            pl.BlockSpec(win.shape, const),
            pl.BlockSpec(gq.shape, const),
            pl.BlockSpec(gkv.shape, const),
            pl.BlockSpec(wuq.shape, const),
            pl.BlockSpec(wukv.shape, const),
            pl.BlockSpec((TM, LANES), rope),
            pl.BlockSpec((TM, LANES), rope),
            pl.BlockSpec((TM, LANES), rope),
            pl.BlockSpec((TM, LANES), rope),
        ],
        out_specs=[pl.BlockSpec((TM, w), row) for w in widths],
        out_shape=[jax.ShapeDtypeStruct((N_TOK, w), BF16) for w in widths],
        compiler_params=_cparams("parallel"),
        name="proj0",
    )(h, g_mix, sh, sc, win, gq, gkv, wuq, wukv, cm, sm, c64, s64)


def _proj1_kernel(h_ref, g_ref, sh_ref, sc_ref, w_ref, c64_ref, s64_ref, q_ref, k_ref, v_ref):
    a = _rms(h_ref[...]) * g_ref[...]
    a = a * (1.0 + sc_ref[0]) + sh_ref[0]
    p = _dot(a.astype(BF16), w_ref[...])
    c64, s64 = c64_ref[...], s64_ref[...]
    width = DIFF_HEADS * LANES
    scale = DIFF_HALF ** -0.5 * LOG2E
    for h in range(DIFF_HEADS):
        sl = slice(h * LANES, (h + 1) * LANES)
        q_ref[:, sl] = (_rope(p[:, sl], c64, s64, DIFF_HALF // 4) * scale).astype(BF16)
        k_ref[:, sl] = _rope(p[:, width + h * LANES:width + (h + 1) * LANES], c64, s64,
                             DIFF_HALF // 4).astype(BF16)
    v_ref[...] = p[:, 2 * width:].astype(BF16)


def _proj1(h, g_mix, sh, sc, w, c64, s64):
    row = lambda i: (i, 0)
    const = lambda i: (0, 0)
    mod = lambda i: (_mod_index(i), 0, 0)
    rope = lambda i: (_rope_index(i), 0)
    width = DIFF_HEADS * LANES
    return pl.pallas_call(
        _proj1_kernel,
        grid=(N_TOK // TM,),
        in_specs=[
            pl.BlockSpec((TM, D_MODEL), row),
            pl.BlockSpec((1, D_MODEL), const),
            pl.BlockSpec((1, 1, D_MODEL), mod),
            pl.BlockSpec((1, 1, D_MODEL), mod),
            pl.BlockSpec(w.shape, const),
            pl.BlockSpec((TM, LANES), rope),
            pl.BlockSpec((TM, LANES), rope),
        ],
        out_specs=[pl.BlockSpec((TM, width), row)] * 3,
        out_shape=[jax.ShapeDtypeStruct((N_TOK, width), BF16)] * 3,
        compiler_params=_cparams("parallel"),
        name="proj1",
    )(h, g_mix, sh, sc, w, c64, s64)


def _mla_kernel(*refs, n_kv):
    q_ref = refs[0]
    k_refs = refs[1:1 + n_kv]
    v_refs = refs[1 + n_kv:1 + 2 * n_kv]
    o_ref = refs[1 + 2 * n_kv]
    lane = lax.broadcasted_iota(jnp.int32, (q_ref.shape[0], LANES), 1)
    own = [lane < MLA_V, lane >= MLA_V]
    for pair in range(MLA_PAIRS_PER_STEP):
        vsl = slice(pair * LANES, (pair + 1) * LANES)
        outs = []
        for hh in range(2):
            sl = slice((2 * pair + hh) * LANES, (2 * pair + hh + 1) * LANES)
            qh = q_ref[:, sl]
            ss = [_dot_nt(qh, k[:, sl]) for k in k_refs]
            m = functools.reduce(jnp.maximum, [jnp.max(s, axis=-1, keepdims=True) for s in ss])
            o = None
            for s, v in zip(ss, v_refs):
                vp = v[:, vsl]
                vlane = lax.broadcasted_iota(jnp.int32, vp.shape, 1)
                keep = (vlane < MLA_V) if hh == 0 else (vlane >= MLA_V)
                vh = jnp.where(keep, vp, jnp.ones_like(vp))
                part = _dot(jnp.exp2(s - m).astype(BF16), vh)
                o = part if o is None else o + part
            outs.append(o / pltpu.roll(o, MLA_V, 1))
        o_ref[:, vsl] = jnp.where(own[0], outs[0], outs[1]).astype(BF16)


def _into_buffer(kernel_fn, buf):
    if buf is None:
        return kernel_fn, [], []

    def without_alias_ref(*refs):
        return kernel_fn(*refs[:-2], refs[-1])

    return without_alias_ref, [pl.BlockSpec(memory_space=pl.ANY)], [buf]


def _mla_attention(qm, km, vm, buf, *, ctx_queries):
    groups = MLA_HEADS // 2 // MLA_PAIRS_PER_STEP
    qk_w = 2 * LANES * MLA_PAIRS_PER_STEP
    v_w = LANES * MLA_PAIRS_PER_STEP
    if ctx_queries:
        tq, n_q = CTX_LEN, 1
        q_map = lambda b, h, i: (N_LAT // CTX_LEN + b, h)
        kv_specs = [(CTX_LEN, lambda b, h, i: (N_LAT // CTX_LEN + b, h))]
    else:
        tq, n_q = TQ_MLA, SEQ // TQ_MLA
        q_map = lambda b, h, i: (b * n_q + i, h)
        kv_specs = [(CTX_LEN, lambda b, h, i: (N_LAT // CTX_LEN + b, h)), (SEQ, lambda b, h, i: (b, h))]
    n_kv = len(kv_specs)
    in_specs = [pl.BlockSpec((tq, qk_w), q_map)]
    in_specs += [pl.BlockSpec((n, qk_w), m) for n, m in kv_specs]
    in_specs += [pl.BlockSpec((n, v_w), m) for n, m in kv_specs]
    kernel_fn, extra_specs, extra_args = _into_buffer(functools.partial(_mla_kernel, n_kv=n_kv), buf)
    n_in = len(in_specs)
    return pl.pallas_call(
        kernel_fn,
        grid=(BATCH, groups, n_q),
        in_specs=in_specs + extra_specs,
        out_specs=pl.BlockSpec((tq, v_w), q_map),
        out_shape=jax.ShapeDtypeStruct((N_TOK, D_MODEL), BF16),
        input_output_aliases={n_in: 0} if extra_args else {},
        compiler_params=_cparams("parallel", "parallel", "parallel"),
        name="mla_ctx" if ctx_queries else "mla_lat",
    )(qm, *([km] * n_kv), *([vm] * n_kv), *extra_args)


def _swa_kernel(sink_ref, *refs, band, n_blocks):
    q_ref = refs[0]
    rows = q_ref.shape[0]
    stacked = SWA_GROUP * rows
    if band:
        kp, kc, kn, kx, vp, vc, vn, vx, o_ref = refs[1:]
        n = pl.program_id(1)
        k_band = jnp.concatenate([kp[...], kc[...], kn[...]], axis=0)
        v_band = jnp.concatenate([vp[...], vc[...], vn[...]], axis=0)
        qq = lax.broadcasted_iota(jnp.int32, (stacked, 3 * SWA_BLOCK), 0) & (SWA_BLOCK - 1)
        kk = lax.broadcasted_iota(jnp.int32, (stacked, 3 * SWA_BLOCK), 1)
        rel = kk - SWA_BLOCK - qq
        key_pos = (n - 1) * SWA_BLOCK + kk
        valid = (jnp.abs(rel) <= WINDOW) & (key_pos >= 0) & (key_pos < n_blocks * SWA_BLOCK)
    else:
        kx, vx, o_ref = refs[1:]
    k_ctx = kx[...]
    v_ctx = vx[...]
    lane = lax.broadcasted_iota(jnp.int32, (rows, LANES), 1)
    low = lane < SWA_HEAD_DIM
    row_group = lax.broadcasted_iota(jnp.int32, (stacked, 1), 0) // rows
    halves = []
    for hk in range(SWA_KV_HEADS):
        keep = low if hk == 0 else jnp.logical_not(low)
        qh = jnp.concatenate(
            [jnp.where(keep, q_ref[:, g * LANES:(g + 1) * LANES], jnp.zeros((rows, LANES), BF16))
             for g in range(SWA_GROUP)], axis=0)
        sink = jnp.zeros((stacked, 1), F32)
        for g in range(SWA_GROUP):
            sink = jnp.where(row_group == g, sink_ref[hk * SWA_GROUP + g] * LOG2E, sink)
        s_ctx = _dot_nt(qh, k_ctx)
        m = jnp.maximum(jnp.max(s_ctx, axis=-1, keepdims=True), sink)
        if band:
            s_band = jnp.where(valid, _dot_nt(qh, k_band), NEG_INF)
            m = jnp.maximum(m, jnp.max(s_band, axis=-1, keepdims=True))
        def ones_other(v, hk=hk):
            vlane = lax.broadcasted_iota(jnp.int32, v.shape, 1)
            own = (vlane < SWA_HEAD_DIM) if hk == 0 else (vlane >= SWA_HEAD_DIM)
            return jnp.where(own, v, jnp.ones_like(v))

        o = _dot(jnp.exp2(s_ctx - m).astype(BF16), ones_other(v_ctx))
        if band:
            o = o + _dot(jnp.exp2(s_band - m).astype(BF16), ones_other(v_band))
        denom = pltpu.roll(o, SWA_HEAD_DIM, 1) + jnp.exp2(sink - m)
        halves.append(o / denom)
    for g in range(SWA_GROUP):
        rs = slice(g * rows, (g + 1) * rows)
        o_ref[:, g * LANES:(g + 1) * LANES] = jnp.where(low, halves[0][rs], halves[1][rs]).astype(BF16)


def _swa_attention(sink, qs, ks, vs, buf, *, ctx_queries):
    width = SWA_HEADS * SWA_HEAD_DIM
    kvw = SWA_KV_HEADS * SWA_HEAD_DIM
    ctx_map = lambda b, i: (N_LAT // CTX_LEN + b, 0)
    smem = pl.BlockSpec(memory_space=pltpu.SMEM)
    if ctx_queries:
        grid = (BATCH, 1)
        in_specs = [smem, pl.BlockSpec((CTX_LEN, width), ctx_map),
                    pl.BlockSpec((CTX_LEN, kvw), ctx_map), pl.BlockSpec((CTX_LEN, kvw), ctx_map)]
        args = (sink, qs, ks, vs)
        out_spec = pl.BlockSpec((CTX_LEN, width), lambda b, i: (N_LAT // CTX_LEN + b, 1))
        n_blocks = 1
    else:
        n_blocks = SEQ // SWA_BLOCK
        grid = (BATCH, n_blocks)
        cur = lambda b, i: (b * n_blocks + i, 0)
        prev = lambda b, i: (b * n_blocks + jnp.maximum(i - 1, 0), 0)
        nxt = lambda b, i: (b * n_blocks + jnp.minimum(i + 1, n_blocks - 1), 0)
        band_specs = [pl.BlockSpec((SWA_BLOCK, kvw), m) for m in (prev, cur, nxt)]
        in_specs = ([smem, pl.BlockSpec((SWA_BLOCK, width), cur)]
                    + band_specs + [pl.BlockSpec((CTX_LEN, kvw), ctx_map)]
                    + band_specs + [pl.BlockSpec((CTX_LEN, kvw), ctx_map)])
        args = (sink, qs, ks, ks, ks, ks, vs, vs, vs, vs)
        out_spec = pl.BlockSpec((SWA_BLOCK, width), lambda b, i: (b * n_blocks + i, 1))
    kernel_fn, extra_specs, extra_args = _into_buffer(
        functools.partial(_swa_kernel, band=not ctx_queries, n_blocks=n_blocks), buf)
    return pl.pallas_call(
        kernel_fn,
        grid=grid,
        in_specs=in_specs + extra_specs,
        out_specs=out_spec,
        out_shape=jax.ShapeDtypeStruct((N_TOK, D_MODEL), BF16),
        input_output_aliases={len(in_specs): 0} if extra_args else {},
        compiler_params=_cparams("parallel", "parallel"),
        name="swa_ctx" if ctx_queries else "swa_lat",
    )(*args, *extra_args)


def _diff_kernel(q_ref, kc_ref, kl_ref, vc_ref, vl_ref, lam_ref, g_ref, o_ref, *, lam_init):
    lam = (jnp.exp(jnp.sum(lam_ref[0:1, :] * lam_ref[1:2, :], axis=-1, keepdims=True))
           - jnp.exp(jnp.sum(lam_ref[2:3, :] * lam_ref[3:4, :], axis=-1, keepdims=True)) + lam_init)
    lane = lax.broadcasted_iota(jnp.int32, (q_ref.shape[0], LANES), 1)
    low = lane < DIFF_HALF
    k_refs = (kc_ref, kl_ref)
    v_refs = (vc_ref, vl_ref)
    for h in range(DIFF_HEADS_PER_STEP):
        sl = slice(h * LANES, (h + 1) * LANES)
        q = q_ref[:, sl]
        zero = jnp.zeros_like(q)

        def numerators(qh):
            ss = [_dot_nt(qh, k[:, sl]) for k in k_refs]
            m = functools.reduce(jnp.maximum, [jnp.max(s, axis=-1, keepdims=True) for s in ss])
            es = [jnp.exp2(s - m) for s in ss]
            return es, functools.reduce(jnp.add, [jnp.sum(e, axis=-1, keepdims=True) for e in es])

        e1, l1 = numerators(jnp.where(low, q, zero))
        e2, l2 = numerators(jnp.where(low, zero, q))
        c = lam * l1 / l2
        o = functools.reduce(jnp.add, [_dot((a - c * b).astype(BF16), v[:, sl])
                                       for a, b, v in zip(e1, e2, v_refs)])
        o = _rms(o / l1) * g_ref[...] * (1.0 - lam_init)
        o_ref[:, sl] = o.astype(BF16)


def _diff_attention(q, k, v, lam_vecs, g_sub, lam_init):
    n_q = SEQ // TQ_DIFF
    q_map = lambda b, h, i: (b * n_q + i, h)
    ctx_map = lambda b, h, i: (N_LAT // CTX_LEN + b, h)
    lat_map = lambda b, h, i: (b, h)
    const = lambda b, h, i: (0, 0)
    width = LANES * DIFF_HEADS_PER_STEP
    return pl.pallas_call(
        functools.partial(_diff_kernel, lam_init=lam_init),
        grid=(BATCH, DIFF_HEADS // DIFF_HEADS_PER_STEP, n_q),
        in_specs=[
            pl.BlockSpec((TQ_DIFF, width), q_map),
            pl.BlockSpec((CTX_LEN, width), ctx_map),
            pl.BlockSpec((SEQ, width), lat_map),
            pl.BlockSpec((CTX_LEN, width), ctx_map),
            pl.BlockSpec((SEQ, width), lat_map),
            pl.BlockSpec(lam_vecs.shape, const),
            pl.BlockSpec(g_sub.shape, const),
        ],
        out_specs=pl.BlockSpec((TQ_DIFF, width), q_map),
        out_shape=jax.ShapeDtypeStruct((N_LAT, DIFF_HEADS * DIFF_V), BF16),
        compiler_params=_cparams("parallel", "parallel", "parallel"),
        name="diff_attn",
    )(q, k, k, v, v, lam_vecs, g_sub)


def _pack_bf16_pairs(x):
    n = x.shape[1] // 2
    lo = pltpu.bitcast(x[:, :n].astype(BF16).astype(F32), jnp.uint32)
    hi = pltpu.bitcast(x[:, n:].astype(BF16).astype(F32), jnp.uint32)
    return pltpu.bitcast((lo >> 16) | hi, jnp.int32)


def _unpack_bf16_pairs(w):
    u = pltpu.bitcast(w, jnp.uint32)
    lo = pltpu.bitcast(u << 16, F32)
    hi = pltpu.bitcast(u & jnp.uint32(0xFFFF0000), F32)
    return lo, hi


def _route(logits_t, bias, cnt_ref, tri_ref, idx_ref, w_ref, rank_ref):
    tm = logits_t.shape[1]
    scores = jax.nn.sigmoid(logits_t)
    biased = scores + bias
    sub = lax.broadcasted_iota(jnp.int32, (GROUP_SIZE, tm), 0).astype(F32)
    grp_scores, grp_biased, grp_index = [], [], []
    group_score = []
    for g in range(N_GROUPS):
        sl = slice(g * GROUP_SIZE, (g + 1) * GROUP_SIZE)
        bg = biased[sl, :]
        grp_scores.append(scores[sl, :])
        grp_biased.append(bg)
        grp_index.append(sub + float(g * GROUP_SIZE))
        m1 = jnp.max(bg, axis=0, keepdims=True)
        first = jnp.min(jnp.where(bg == m1, sub, float(GROUP_SIZE)), axis=0, keepdims=True)
        m2 = jnp.max(jnp.where(sub == first, -jnp.inf, bg), axis=0, keepdims=True)
        group_score.append(m1 + m2)
    keep = [jnp.zeros((1, tm), F32) for _ in range(N_GROUPS)]
    for _ in range(TOPK_GROUPS):
        m = functools.reduce(jnp.maximum, group_score)
        found = jnp.zeros((1, tm), F32)
        for g in range(N_GROUPS):
            hit = jnp.where(group_score[g] == m, 1.0 - found, 0.0)
            keep[g] = keep[g] + hit
            found = found + hit
            group_score[g] = jnp.where(hit > 0.0, -jnp.inf, group_score[g])
    vals = [jnp.where(keep[g] > 0.0, grp_biased[g], NEG_INF) for g in range(N_GROUPS)]
    chosen = [jnp.zeros((GROUP_SIZE, tm), F32) for _ in range(N_GROUPS)]
    picked = []
    for _ in range(TOP_K):
        m = jnp.max(functools.reduce(jnp.maximum, vals), axis=0, keepdims=True)
        cand = [jnp.where(vals[g] == m, grp_index[g], float(N_EXPERTS)) for g in range(N_GROUPS)]
        ei = jnp.min(functools.reduce(jnp.minimum, cand), axis=0, keepdims=True)
        sel = [grp_index[g] == ei for g in range(N_GROUPS)]
        s_k = functools.reduce(jnp.add, [jnp.where(sel[g], grp_scores[g], 0.0) for g in range(N_GROUPS)])
        picked.append((ei, jnp.sum(s_k, axis=0, keepdims=True)))
        vals = [jnp.where(sel[g], -jnp.inf, vals[g]) for g in range(N_GROUPS)]
        chosen = [jnp.where(sel[g], 1.0, chosen[g]) for g in range(N_GROUPS)]
    total = functools.reduce(jnp.add, [s for _, s in picked])
    chosen_all = jnp.concatenate(chosen, axis=0)
    rank_all = _dot(chosen_all.astype(BF16), tri_ref[...]) + cnt_ref[...]
    cnt_ref[...] = cnt_ref[...] + jnp.sum(chosen_all, axis=1, keepdims=True)
    for k, (ei, s) in enumerate(picked):
        idx_ref[k:k + 1, :] = ei.astype(jnp.int32)
        w_ref[k:k + 1, :] = s / total * ROUTED_SCALE
        r = functools.reduce(jnp.add, [
            jnp.where(grp_index[g] == ei, rank_all[g * GROUP_SIZE:(g + 1) * GROUP_SIZE, :], 0.0)
            for g in range(N_GROUPS)])
        rank_ref[k:k + 1, :] = jnp.sum(r, axis=0, keepdims=True).astype(jnp.int32)


def _post_kernel(o_ref, h_ref, wout_ref, gtm_ref, g_ref, sh_ref, sc_ref, gtf_ref,
                 wrh_ref, wrl_ref, br_ref, tri_ref, wsg_ref, wsu_ref, wsd_ref,
                 hs_ref, f_ref, idx_ref, w_ref, rank_ref, cnt_out_ref, cnt_ref):
    @pl.when(pl.program_id(0) == 0)
    def _():
        cnt_ref[...] = jnp.zeros_like(cnt_ref)

    h1 = h_ref[...] + gtm_ref[0] * _dot(o_ref[...], wout_ref[...])
    f = _rms(h1) * g_ref[...]
    f = f * (1.0 + sc_ref[0]) + sh_ref[0]
    f_hi = f.astype(BF16)
    f_lo = (f - f_hi.astype(F32)).astype(BF16)
    f_ref[...] = _pack_bf16_pairs(f)
    logits_t = (_dot_nt(wrh_ref[...], f_hi) + _dot_nt(wrh_ref[...], f_lo) + _dot_nt(wrl_ref[...], f_hi))
    _route(logits_t, br_ref[...], cnt_ref, tri_ref, idx_ref, w_ref, rank_ref)
    cnt_out_ref[...] = jnp.broadcast_to(cnt_ref[...], cnt_out_ref.shape).astype(jnp.int32)
    mid = _silu(_dot(f_hi, wsg_ref[...])) * _dot(f_hi, wsu_ref[...])
    shared = _dot(mid.astype(BF16), wsd_ref[...])
    hs_ref[...] = h1 + gtf_ref[0] * shared


def _post(o, h, wout, gtm, g_ffn, sh, sc, gtf, wrh, wrl, br, wsg, wsu, wsd, n_rows):
    row = lambda i: (i, 0)
    col = lambda i: (0, i)
    const = lambda i: (0, 0)
    mod = lambda i: (_mod_index(i), 0, 0)
    vec = pl.BlockSpec((1, 1, D_MODEL), mod)
    full = lambda a: pl.BlockSpec(a.shape, const)
    tri = (jnp.arange(TM)[:, None] < jnp.arange(TM)[None, :]).astype(BF16)
    return pl.pallas_call(
        _post_kernel,
        grid=(n_rows // TM,),
        in_specs=[
            pl.BlockSpec((TM, D_MODEL), row),
            pl.BlockSpec((TM, D_MODEL), row),
            full(wout), vec, full(g_ffn), vec, vec, vec,
            full(wrh), full(wrl), full(br), full(tri), full(wsg), full(wsu), full(wsd),
        ],
        out_specs=[
            pl.BlockSpec((TM, D_MODEL), row),
            pl.BlockSpec((TM, D_MODEL // 2), row),
            pl.BlockSpec((TOP_K, TM), col),
            pl.BlockSpec((TOP_K, TM), col),
            pl.BlockSpec((TOP_K, TM), col),
            pl.BlockSpec((N_EXPERTS, LANES), const),
        ],
        out_shape=[
            jax.ShapeDtypeStruct((n_rows, D_MODEL), F32),
            jax.ShapeDtypeStruct((n_rows, D_MODEL // 2), jnp.int32),
            jax.ShapeDtypeStruct((TOP_K, n_rows), jnp.int32),
            jax.ShapeDtypeStruct((TOP_K, n_rows), F32),
            jax.ShapeDtypeStruct((TOP_K, n_rows), jnp.int32),
            jax.ShapeDtypeStruct((N_EXPERTS, LANES), jnp.int32),
        ],
        scratch_shapes=[pltpu.VMEM((N_EXPERTS, 1), F32)],
        compiler_params=_cparams("arbitrary"),
        name="post_attn",
    )(o, h, wout, gtm, g_ffn, sh, sc, gtf, wrh, wrl, br, tri, wsg, wsu, wsd)


DEST_TN = 2048


def _dest_kernel(ps_ref, idx_ref, rank_ref, o_ref):
    idx = idx_ref[...]
    acc = rank_ref[...]
    for e in range(N_EXPERTS):
        acc = acc + jnp.where(idx == e, ps_ref[e], 0)
    o_ref[...] = acc


def _dest(pad_start, top_idx, rank):
    n_tok = top_idx.shape[1]
    blk = pl.BlockSpec((TOP_K, DEST_TN), lambda i: (0, i))
    return pl.pallas_call(
        _dest_kernel,
        grid=(n_tok // DEST_TN,),
        in_specs=[pl.BlockSpec(memory_space=pltpu.SMEM), blk, blk],
        out_specs=blk,
        out_shape=jax.ShapeDtypeStruct((TOP_K, n_tok), jnp.int32),
        compiler_params=_cparams("parallel"),
        name="dest",
    )(pad_start, top_idx, rank)


SC_CORES = 2
SC_SUBCORES = 16
SC_WORKERS = SC_CORES * SC_SUBCORES
SC_ROWS = 64
ROW_WORDS = D_MODEL // 2


def _sc_mesh():
    return plsc.VectorSubcoreMesh(core_axis_name="core", subcore_axis_name="subcore")


def _sc_worker():
    return lax.axis_index("subcore") * SC_CORES + lax.axis_index("core")


def _sc_dispatch(f_packed, dest_flat, n_slots):
    n_tok = f_packed.shape[0]
    per_worker = n_tok // SC_WORKERS
    n_chunks = per_worker // SC_ROWS
    assert n_chunks * SC_ROWS * SC_WORKERS == n_tok

    def body(f_hbm, dest_hbm, xs_hbm, rows_v, *rest):
        idx_v, sem = rest[:TOP_K], rest[TOP_K]
        wid = _sc_worker()

        @pl.loop(0, n_chunks)
        def _(c):
            base = pl.multiple_of(wid * per_worker + c * SC_ROWS, 8)
            pltpu.sync_copy(f_hbm.at[pl.ds(base, SC_ROWS)], rows_v)
            for k in range(TOP_K):
                pltpu.sync_copy(dest_hbm.at[pl.ds(pl.multiple_of(k * n_tok + base, 8), SC_ROWS)], idx_v[k])
            copies = [pltpu.async_copy(rows_v, xs_hbm.at[idx_v[k]], sem) for k in range(TOP_K)]
            for cp in copies:
                cp.wait()

    return pl.kernel(
        body,
        out_type=jax.ShapeDtypeStruct((n_slots, ROW_WORDS), jnp.int32),
        mesh=_sc_mesh(),
        scratch_types=([pltpu.VMEM((SC_ROWS, ROW_WORDS), jnp.int32)]
                       + [pltpu.VMEM((SC_ROWS,), jnp.int32) for _ in range(TOP_K)]
                       + [pltpu.SemaphoreType.DMA]),
        name="sc_dispatch",
    )(f_packed, dest_flat)


def _sc_gather(ys, dest_flat):
    n_idx = dest_flat.shape[0]
    per_worker = n_idx // SC_WORKERS
    n_chunks = per_worker // SC_ROWS
    assert n_chunks * SC_ROWS * SC_WORKERS == n_idx

    def body(ys_hbm, dest_hbm, out_hbm, rows_v, idx_v, sem):
        wid = _sc_worker()

        @pl.loop(0, n_chunks)
        def _(c):
            base = pl.multiple_of(wid * per_worker + c * SC_ROWS, 8)
            pltpu.sync_copy(dest_hbm.at[pl.ds(base, SC_ROWS)], idx_v)
            pltpu.async_copy(ys_hbm.at[idx_v], rows_v, sem).wait()
            pltpu.sync_copy(rows_v, out_hbm.at[pl.ds(base, SC_ROWS)])

    return pl.kernel(
        body,
        out_type=jax.ShapeDtypeStruct((n_idx, ROW_WORDS), jnp.int32),
        mesh=_sc_mesh(),
        scratch_types=[pltpu.VMEM((SC_ROWS, ROW_WORDS), jnp.int32), pltpu.VMEM((SC_ROWS,), jnp.int32),
                       pltpu.SemaphoreType.DMA],
        name="sc_gather",
    )(ys, dest_flat)


def _moe_kernel(be_ref, nused_ref, x_ref, wg_ref, wu_ref, wd_ref, o_ref, wg_s, wu_s, wd_s):
    i = pl.program_id(0)
    e = be_ref[i]
    changed = jnp.logical_or(i == 0, e != be_ref[jnp.maximum(i - 1, 0)])

    @pl.when(changed)
    def _():
        wg_s[...] = wg_ref[0, 0].astype(BF16)
        wu_s[...] = wu_ref[0, 0].astype(BF16)
        wd_s[...] = wd_ref[0, 0].astype(BF16)

    @pl.when(i < nused_ref[0])
    def _():
        lo, hi = _unpack_bf16_pairs(x_ref[...])
        lo, hi = lo.astype(BF16), hi.astype(BF16)
        half = D_MODEL // 2
        gate = _dot(lo, wg_s[:half, :]) + _dot(hi, wg_s[half:, :])
        up = _dot(lo, wu_s[:half, :]) + _dot(hi, wu_s[half:, :])
        mid = _silu(gate) * up
        o_ref[...] = _pack_bf16_pairs(_dot(mid.astype(BF16), wd_s[...]))


def _moe(block_expert, n_used, xs, wg, wu, wd, layer):
    n_blocks = xs.shape[0] // MOE_ROWS
    expert = lambda i, be, nu: (layer, be[i], 0, 0)
    grid_spec = pltpu.PrefetchScalarGridSpec(
        num_scalar_prefetch=2,
        grid=(n_blocks,),
        in_specs=[
            pl.BlockSpec((MOE_ROWS, ROW_WORDS), lambda i, be, nu: (i, 0)),
            pl.BlockSpec((1, 1, D_MODEL, EXPERT_FF), expert),
            pl.BlockSpec((1, 1, D_MODEL, EXPERT_FF), expert),
            pl.BlockSpec((1, 1, EXPERT_FF, D_MODEL), expert),
        ],
        out_specs=pl.BlockSpec((MOE_ROWS, ROW_WORDS), lambda i, be, nu: (i, 0)),
        scratch_shapes=[
            pltpu.VMEM((D_MODEL, EXPERT_FF), BF16),
            pltpu.VMEM((D_MODEL, EXPERT_FF), BF16),
            pltpu.VMEM((EXPERT_FF, D_MODEL), BF16),
        ],
    )
    return pl.pallas_call(
        _moe_kernel,
        grid_spec=grid_spec,
        out_shape=jax.ShapeDtypeStruct((xs.shape[0], ROW_WORDS), jnp.int32),
        compiler_params=_cparams("arbitrary"),
        name="moe_experts",
    )(block_expert, n_used, xs, wg, wu, wd)


def _combine_kernel(y_ref, w_ref, hs_ref, gtf_ref, gfin_ref, o_ref, *, final):
    w = w_ref[...]
    acc_lo = acc_hi = None
    for k in range(TOP_K):
        lo, hi = _unpack_bf16_pairs(y_ref[k])
        wk = w[:, k:k + 1]
        acc_lo = lo * wk if k == 0 else acc_lo + lo * wk
        acc_hi = hi * wk if k == 0 else acc_hi + hi * wk
    out = hs_ref[...] + gtf_ref[0] * jnp.concatenate([acc_lo, acc_hi], axis=1)
    if final:
        out = _rms(out) * gfin_ref[...]
    o_ref[...] = out


def _combine(y, w, hs, gtf, g_final, *, final):
    n_rows = hs.shape[0]
    return pl.pallas_call(
        functools.partial(_combine_kernel, final=final),
        grid=(n_rows // TM_COMB,),
        in_specs=[
            pl.BlockSpec((TOP_K, TM_COMB, ROW_WORDS), lambda i: (0, i, 0)),
            pl.BlockSpec((TM_COMB, TOP_K), lambda i: (i, 0)),
            pl.BlockSpec((TM_COMB, D_MODEL), lambda i: (i, 0)),
            pl.BlockSpec((1, 1, D_MODEL), lambda i: (jnp.minimum(i // (SEQ // TM_COMB), BATCH), 0, 0)),
            pl.BlockSpec((1, D_MODEL), lambda i: (0, 0)),
        ],
        out_specs=pl.BlockSpec((TM_COMB, D_MODEL), lambda i: (i, 0)),
        out_shape=jax.ShapeDtypeStruct((n_rows, D_MODEL), F32),
        compiler_params=_cparams("parallel"),
        name="combine_final" if final else "combine",
    )(y, w, hs, gtf, g_final)


def _rope_tables():
    rows = SEQ // GRID_W

    def angles(rot_dim):
        half = rot_dim // 2
        inv_freq = ROPE_BASE ** (-jnp.arange(0, half, 2, dtype=F32) / half)
        row = jnp.repeat(jnp.arange(rows, dtype=F32), GRID_W)
        col = jnp.tile(jnp.arange(GRID_W, dtype=F32), rows)
        ang_r = row[:, None] * inv_freq
        ang_c = col[:, None] * inv_freq
        return jnp.concatenate([ang_r, ang_r, ang_c, ang_c], axis=-1)

    def signed(sin, quarter):
        sign = jnp.where((jnp.arange(sin.shape[-1]) // quarter) % 2 == 0, -1.0, 1.0)
        return sin * sign

    def with_identity(cos, sin):
        cos = jnp.concatenate([cos, jnp.ones((TM, LANES), F32)], axis=0)
        sin = jnp.concatenate([sin, jnp.zeros((TM, LANES), F32)], axis=0)
        return cos, sin

    a64 = angles(SWA_HEAD_DIM)
    c64 = jnp.tile(jnp.cos(a64), (1, LANES // SWA_HEAD_DIM))
    s64 = jnp.tile(signed(jnp.sin(a64), SWA_HEAD_DIM // 4), (1, LANES // SWA_HEAD_DIM))
    a32 = angles(MLA_ROPE)
    pad_lo = MLA_NOPE
    pad_hi = LANES - MLA_NOPE - MLA_ROPE
    cm = jnp.concatenate([jnp.ones((SEQ, pad_lo), F32), jnp.cos(a32), jnp.ones((SEQ, pad_hi), F32)], axis=-1)
    sm = jnp.concatenate([jnp.zeros((SEQ, pad_lo), F32), signed(jnp.sin(a32), MLA_ROPE // 4),
                          jnp.zeros((SEQ, pad_hi), F32)], axis=-1)
    return with_identity(cm, sm) + with_identity(c64, s64)


def _layer0_weights(wa_in, wa_uq, wa_ukv, wa_out):
    d = D_MODEL
    cq, ckv, kr, qs, ks, vs = jnp.split(
        wa_in, [C_CKV, C_KR, C_KR + MLA_ROPE, C_KR + MLA_ROPE + SWA_HEADS * SWA_HEAD_DIM,
                C_KR + MLA_ROPE + (SWA_HEADS + SWA_KV_HEADS) * SWA_HEAD_DIM], axis=-1)
    kr_pad = jnp.concatenate([jnp.zeros((d, MLA_NOPE), F32), kr,
                              jnp.zeros((d, LANES - MLA_NOPE - MLA_ROPE), F32)], axis=-1)
    qs_pair = qs.reshape(d, SWA_KV_HEADS, SWA_GROUP, SWA_HEAD_DIM).transpose(0, 2, 1, 3).reshape(d, -1)
    win = jnp.concatenate([cq, ckv, kr_pad, qs_pair, ks, vs], axis=-1).astype(BF16)
    uq = wa_uq.reshape(MLA_Q_RANK, MLA_HEADS, MLA_NOPE + MLA_ROPE)
    uq = jnp.pad(uq, ((0, 0), (0, 0), (0, LANES - MLA_NOPE - MLA_ROPE))).reshape(MLA_Q_RANK, -1)
    ukv = wa_ukv.reshape(MLA_KV_RANK, MLA_HEADS, MLA_NOPE + MLA_V)
    uk = jnp.pad(ukv[:, :, :MLA_NOPE], ((0, 0), (0, 0), (0, LANES - MLA_NOPE))).reshape(MLA_KV_RANK, -1)
    uv = ukv[:, :, MLA_NOPE:].reshape(MLA_KV_RANK, -1)
    wukv = jnp.concatenate([uk, uv], axis=-1)
    n_mla = MLA_HEADS * MLA_V
    out_swa = wa_out[n_mla:].reshape(SWA_KV_HEADS, SWA_GROUP, SWA_HEAD_DIM, d).transpose(1, 0, 2, 3)
    wout = jnp.concatenate([wa_out[:n_mla], out_swa.reshape(-1, d)], axis=0)
    return win, uq.astype(BF16), wukv.astype(BF16), wout.astype(BF16)


def _layer1_weights(wc_in):
    per_head = 4 * DIFF_HALF + DIFF_V
    w = wc_in.reshape(D_MODEL, DIFF_HEADS, per_head)
    q = w[:, :, :2 * DIFF_HALF].reshape(D_MODEL, -1)
    k = w[:, :, 2 * DIFF_HALF:4 * DIFF_HALF].reshape(D_MODEL, -1)
    v = w[:, :, 4 * DIFF_HALF:].reshape(D_MODEL, -1)
    return jnp.concatenate([q, k, v], axis=-1).astype(BF16)


def _block_tables(counts, n_blocks):
    padded = (counts + MOE_ROWS - 1) // MOE_ROWS * MOE_ROWS
    pad_end = jnp.cumsum(padded)
    pad_start = (pad_end - padded).astype(jnp.int32)
    block_start = jnp.arange(n_blocks, dtype=jnp.int32) * MOE_ROWS
    block_expert = jnp.minimum((pad_end[None, :] <= block_start[:, None]).sum(axis=1),
                               N_EXPERTS - 1).astype(jnp.int32)
    n_used = (pad_end[-1] // MOE_ROWS).astype(jnp.int32).reshape(1)
    return pad_start, block_expert, n_used


def _moe_layer(f_packed, top_idx, top_w, rank, counts, hs, gtf, g_final, wg, wu, wd, layer, *, final):
    n_tok = top_idx.shape[1]
    n_blocks = -(-(n_tok * TOP_K) // MOE_ROWS) + N_EXPERTS
    pad_start, block_expert, n_used = _block_tables(counts[:, 0], n_blocks)
    dest_flat = _dest(pad_start, top_idx, rank).reshape(-1)
    xs = _sc_dispatch(f_packed, dest_flat, n_blocks * MOE_ROWS)
    ys = _moe(block_expert, n_used, xs, wg, wu, wd, layer)
    y = _sc_gather(ys, dest_flat).reshape(TOP_K, n_tok, ROW_WORDS)
    return _combine(y, top_w.T, hs, gtf, g_final, final=final)


def kernel(x, c, ctx, c_ctx, w_ada, b_ada, g_mix, g_ffn, wa_in, ga_q, ga_kv, wa_uq, wa_ukv, wa_sink, wa_out,
           wc_in, lam_q1, lam_k1, lam_q2, lam_k2, gc_sub, wc_out, w_router, b_router, we_gate, we_up, we_down,
           ws_gate, ws_up, ws_down, g_final):
    d = D_MODEL
    h = jnp.concatenate([x.reshape(N_LAT, d), ctx.reshape(N_CTX, d)], axis=0)
    cc = jnp.concatenate([c, c_ctx[None, :], jnp.zeros((MOD_ROWS - BATCH - 1, d), F32)], axis=0)
    mod = _ada(cc, w_ada, b_ada).reshape(DEPTH, MOD_ROWS, 6, 1, d)

    def mod_vec(layer, j):
        return mod[layer, :, j]

    cm, sm, c64, s64 = _rope_tables()
    g_final2 = g_final.reshape(1, d)

    def router_weights(layer):
        wt = w_router[layer].T
        hi = wt.astype(BF16)
        lo = (wt - hi.astype(F32)).astype(BF16)
        return hi, lo, b_router[layer].reshape(N_EXPERTS, 1)

    def shared_weights(layer):
        return ws_gate[layer].astype(BF16), ws_up[layer].astype(BF16), ws_down[layer].astype(BF16)

    win, wuq, wukv, wout0 = _layer0_weights(wa_in[0], wa_uq[0], wa_ukv[0], wa_out[0])
    qm, km, vm, qs, ks, vs = _proj0(h, g_mix[0].reshape(1, d), mod_vec(0, 0), mod_vec(0, 1), win,
                                    ga_q[0].reshape(1, -1), ga_kv[0].reshape(1, -1), wuq, wukv, cm, sm, c64, s64)
    sink = wa_sink[0].astype(F32)
    o = _mla_attention(qm, km, vm, None, ctx_queries=False)
    o = _mla_attention(qm, km, vm, o, ctx_queries=True)
    o = _swa_attention(sink, qs, ks, vs, o, ctx_queries=False)
    o = _swa_attention(sink, qs, ks, vs, o, ctx_queries=True)
    hs, f, top_idx, top_w, rank, counts = _post(
        o, h, wout0, mod_vec(0, 2), g_ffn[0].reshape(1, d), mod_vec(0, 3), mod_vec(0, 4), mod_vec(0, 5),
        *router_weights(0), *shared_weights(0), N_TOK)
    h = _moe_layer(f, top_idx, top_w, rank, counts, hs, mod_vec(0, 5), g_final2,
                   we_gate, we_up, we_down, 0, final=False)

    layer = 1
    lam_init = 0.8 - 0.6 * math.exp(-0.3 * layer)
    q1, k1, v1 = _proj1(h, g_mix[1].reshape(1, d), mod_vec(1, 0), mod_vec(1, 1), _layer1_weights(wc_in[0]),
                        c64, s64)
    lam_vecs = jnp.zeros((8, LANES), F32).at[:4, :DIFF_HALF].set(
        jnp.stack([lam_q1[0], lam_k1[0], lam_q2[0], lam_k2[0]]).astype(F32))
    o = _diff_attention(q1, k1, v1, lam_vecs, gc_sub[0].reshape(1, DIFF_V), lam_init)
    hs, f, top_idx, top_w, rank, counts = _post(
        o, h, wc_out[0].astype(BF16), mod_vec(1, 2), g_ffn[1].reshape(1, d), mod_vec(1, 3), mod_vec(1, 4),
        mod_vec(1, 5), *router_weights(1), *shared_weights(1), N_LAT)
    out = _moe_layer(f, top_idx, top_w, rank, counts, hs, mod_vec(1, 5), g_final2,
                     we_gate, we_up, we_down, 1, final=True)
    return out.reshape(BATCH, SEQ, d)
```

```python
import functools
import math

import jax
import jax.numpy as jnp
from jax import lax
from jax.experimental import pallas as pl
from jax.experimental.pallas import tpu as pltpu
from jax.experimental.pallas import tpu_sc as plsc

F32 = jnp.float32
BF16 = jnp.bfloat16

D_MODEL = 1024
BATCH = 8
SEQ = 2048
DEPTH = 2
CTX_LEN = 256
GRID_W = 64
ROPE_BASE = 10000.0
EPS = 1e-6
NEG_INF = -1e30

MLA_HEADS = 8
MLA_Q_RANK = 384
MLA_KV_RANK = 256
MLA_NOPE = 64
MLA_ROPE = 32
MLA_V = 64
SWA_HEADS = 8
SWA_KV_HEADS = 2
SWA_HEAD_DIM = 64
SWA_GROUP = SWA_HEADS // SWA_KV_HEADS
WINDOW = 128
DIFF_HEADS = 8
DIFF_HALF = 64
DIFF_V = 128
N_EXPERTS = 64
N_GROUPS = 8
GROUP_SIZE = N_EXPERTS // N_GROUPS
TOPK_GROUPS = 4
TOP_K = 8
EXPERT_FF = 256
SHARED_FF = 256
ROUTED_SCALE = 2.5

LOG2E = math.log2(math.e)
LANES = 128
N_LAT = BATCH * SEQ
N_CTX = BATCH * CTX_LEN
N_TOK = N_LAT + N_CTX
MOD_ROWS = 16

TM = 256
LAT_BLOCKS_PER_BATCH = SEQ // TM
TQ_MLA = 512
MLA_PAIRS_PER_STEP = 4
TQ_DIFF = 512
DIFF_HEADS_PER_STEP = 4
SWA_BLOCK = 128
MOE_ROWS = 512
MOE_CHUNK = 256
TM_COMB = 256
VMEM_LIMIT = 56 * 1024 * 1024

C_CQ = 0
C_CKV = C_CQ + MLA_Q_RANK
C_KR = C_CKV + MLA_KV_RANK
C_QS = C_KR + LANES
C_KS = C_QS + SWA_HEADS * SWA_HEAD_DIM
C_VS = C_KS + SWA_KV_HEADS * SWA_HEAD_DIM
C_END = C_VS + SWA_KV_HEADS * SWA_HEAD_DIM


def _cparams(*sem):
    return pltpu.CompilerParams(dimension_semantics=sem, vmem_limit_bytes=VMEM_LIMIT)


def _dot(a, b):
    return jnp.dot(a, b, preferred_element_type=F32)


def _dot_nt(a, b):
    return lax.dot_general(a, b, (((1,), (1,)), ((), ())), preferred_element_type=F32)


def _rms(x):
    return x * lax.rsqrt(jnp.mean(x * x, axis=-1, keepdims=True) + EPS)


def _silu(x):
    return x * jax.nn.sigmoid(x)


def _rope(x, cos, sin_signed, shift):
    n = x.shape[-1]
    lane = lax.broadcasted_iota(jnp.int32, x.shape, 1)
    first = (lane & shift) == 0
    rot = jnp.where(first, pltpu.roll(x, n - shift, 1), pltpu.roll(x, shift, 1))
    return x * cos + rot * sin_signed


def _mod_index(i):
    return jnp.minimum(i // LAT_BLOCKS_PER_BATCH, BATCH)


def _rope_index(i):
    return jnp.where(i < N_LAT // TM, i % LAT_BLOCKS_PER_BATCH, LAT_BLOCKS_PER_BATCH)


ADA_TN = 1536


def _ada_kernel(c_ref, w_ref, b_ref, o_ref):
    s = _silu(c_ref[...]).astype(BF16)
    o_ref[0] = _dot(s, w_ref[0].astype(BF16)) + b_ref[0]


def _ada(cc, w_ada, b_ada):
    n_out = w_ada.shape[-1]
    return pl.pallas_call(
        _ada_kernel,
        grid=(DEPTH, n_out // ADA_TN),
        in_specs=[
            pl.BlockSpec((MOD_ROWS, D_MODEL), lambda l, j: (0, 0)),
            pl.BlockSpec((1, D_MODEL, ADA_TN), lambda l, j: (l, 0, j)),
            pl.BlockSpec((1, 1, ADA_TN), lambda l, j: (l, 0, j)),
        ],
        out_specs=pl.BlockSpec((1, MOD_ROWS, ADA_TN), lambda l, j: (l, 0, j)),
        out_shape=jax.ShapeDtypeStruct((DEPTH, MOD_ROWS, n_out), F32),
        compiler_params=_cparams("parallel", "parallel"),
        name="ada",
    )(cc, w_ada, b_ada.reshape(DEPTH, 1, n_out))


def _proj0_kernel(h_ref, g_ref, sh_ref, sc_ref, win_ref, gq_ref, gkv_ref, wuq_ref, wukv_ref,
                  cm_ref, sm_ref, c64_ref, s64_ref,
                  qm_ref, km_ref, vm_ref, qs_ref, ks_ref, vs_ref):
    a = _rms(h_ref[...]) * g_ref[...]
    a = a * (1.0 + sc_ref[0]) + sh_ref[0]
    p = _dot(a.astype(BF16), win_ref[...])
    nq = _rms(p[:, C_CQ:C_CKV]) * gq_ref[...]
    nkv = _rms(p[:, C_CKV:C_KR]) * gkv_ref[...]
    q = _dot(nq.astype(BF16), wuq_ref[...])
    kv = _dot(nkv.astype(BF16), wukv_ref[...])
    cm, sm = cm_ref[...], sm_ref[...]
    c64, s64 = c64_ref[...], s64_ref[...]
    kr = _rope(p[:, C_KR:C_QS], cm, sm, MLA_ROPE // 4)
    q_scale = (MLA_NOPE + MLA_ROPE) ** -0.5 * LOG2E
    for h in range(MLA_HEADS):
        sl = slice(h * LANES, (h + 1) * LANES)
        qm_ref[:, sl] = (_rope(q[:, sl], cm, sm, MLA_ROPE // 4) * q_scale).astype(BF16)
        km_ref[:, sl] = (kv[:, sl] + kr).astype(BF16)
    vm_ref[...] = kv[:, MLA_HEADS * LANES:].astype(BF16)
    s_scale = SWA_HEAD_DIM ** -0.5 * LOG2E
    for g in range(SWA_GROUP):
        sl = slice(g * LANES, (g + 1) * LANES)
        qs_ref[:, sl] = (_rope(p[:, C_QS + g * LANES:C_QS + (g + 1) * LANES], c64, s64,
                               SWA_HEAD_DIM // 4) * s_scale).astype(BF16)
    ks_ref[...] = _rope(p[:, C_KS:C_VS], c64, s64, SWA_HEAD_DIM // 4).astype(BF16)
    vs_ref[...] = p[:, C_VS:C_END].astype(BF16)


def _proj0(h, g_mix, sh, sc, win, gq, gkv, wuq, wukv, cm, sm, c64, s64):
    row = lambda i: (i, 0)
    const = lambda i: (0, 0)
    mod = lambda i: (_mod_index(i), 0, 0)
    rope = lambda i: (_rope_index(i), 0)
    widths = (MLA_HEADS * LANES, MLA_HEADS * LANES, MLA_HEADS * MLA_V,
              SWA_HEADS * SWA_HEAD_DIM, SWA_KV_HEADS * SWA_HEAD_DIM, SWA_KV_HEADS * SWA_HEAD_DIM)
    return pl.pallas_call(
        _proj0_kernel,
        grid=(N_TOK // TM,),
        in_specs=[
            pl.BlockSpec((TM, D_MODEL), row),
            pl.BlockSpec((1, D_MODEL), const),
            pl.BlockSpec((1, 1, D_MODEL), mod),
            pl.BlockSpec((1, 1, D_MODEL), mod),
            pl.BlockSpec(win.shape, const),
            pl.BlockSpec(gq.shape, const),
            pl.BlockSpec(gkv.shape, const),
            pl.BlockSpec(wuq.shape, const),
            pl.BlockSpec(wukv.shape, const),
            pl.BlockSpec((TM, LANES), rope),
            pl.BlockSpec((TM, LANES), rope),
            pl.BlockSpec((TM, LANES), rope),
            pl.BlockSpec((TM, LANES), rope),
        ],
        out_specs=[pl.BlockSpec((TM, w), row) for w in widths],
        out_shape=[jax.ShapeDtypeStruct((N_TOK, w), BF16) for w in widths],
        compiler_params=_cparams("parallel"),
        name="proj0",
    )(h, g_mix, sh, sc, win, gq, gkv, wuq, wukv, cm, sm, c64, s64)


def _proj1_kernel(h_ref, g_ref, sh_ref, sc_ref, w_ref, c64_ref, s64_ref, q_ref, k_ref, v_ref):
    a = _rms(h_ref[...]) * g_ref[...]
    a = a * (1.0 + sc_ref[0]) + sh_ref[0]
    p = _dot(a.astype(BF16), w_ref[...])
    c64, s64 = c64_ref[...], s64_ref[...]
    width = DIFF_HEADS * LANES
    scale = DIFF_HALF ** -0.5 * LOG2E
    for h in range(DIFF_HEADS):
        sl = slice(h * LANES, (h + 1) * LANES)
        q_ref[:, sl] = (_rope(p[:, sl], c64, s64, DIFF_HALF // 4) * scale).astype(BF16)
        k_ref[:, sl] = _rope(p[:, width + h * LANES:width + (h + 1) * LANES], c64, s64,
                             DIFF_HALF // 4).astype(BF16)
    v_ref[...] = p[:, 2 * width:].astype(BF16)


def _proj1(h, g_mix, sh, sc, w, c64, s64):
    row = lambda i: (i, 0)
    const = lambda i: (0, 0)
    mod = lambda i: (_mod_index(i), 0, 0)
    rope = lambda i: (_rope_index(i), 0)
    width = DIFF_HEADS * LANES
    return pl.pallas_call(
        _proj1_kernel,
        grid=(N_TOK // TM,),
        in_specs=[
            pl.BlockSpec((TM, D_MODEL), row),
            pl.BlockSpec((1, D_MODEL), const),
            pl.BlockSpec((1, 1, D_MODEL), mod),
            pl.BlockSpec((1, 1, D_MODEL), mod),
            pl.BlockSpec(w.shape, const),
            pl.BlockSpec((TM, LANES), rope),
            pl.BlockSpec((TM, LANES), rope),
        ],
        out_specs=[pl.BlockSpec((TM, width), row)] * 3,
        out_shape=[jax.ShapeDtypeStruct((N_TOK, width), BF16)] * 3,
        compiler_params=_cparams("parallel"),
        name="proj1",
    )(h, g_mix, sh, sc, w, c64, s64)


def _mla_kernel(*refs, n_kv):
    q_ref = refs[0]
    k_refs = refs[1:1 + n_kv]
    v_refs = refs[1 + n_kv:1 + 2 * n_kv]
    o_ref = refs[1 + 2 * n_kv]
    lane = lax.broadcasted_iota(jnp.int32, (q_ref.shape[0], LANES), 1)
    own = [lane < MLA_V, lane >= MLA_V]
    for pair in range(MLA_PAIRS_PER_STEP):
        vsl = slice(pair * LANES, (pair + 1) * LANES)
        outs = []
        for hh in range(2):
            sl = slice((2 * pair + hh) * LANES, (2 * pair + hh + 1) * LANES)
            qh = q_ref[:, sl]
            ss = [_dot_nt(qh, k[:, sl]) for k in k_refs]
            m = functools.reduce(jnp.maximum, [jnp.max(s, axis=-1, keepdims=True) for s in ss])
            o = None
            for s, v in zip(ss, v_refs):
                vp = v[:, vsl]
                vlane = lax.broadcasted_iota(jnp.int32, vp.shape, 1)
                keep = (vlane < MLA_V) if hh == 0 else (vlane >= MLA_V)
                vh = jnp.where(keep, vp, jnp.ones_like(vp))
                part = _dot(jnp.exp2(s - m).astype(BF16), vh)
                o = part if o is None else o + part
            outs.append(o / pltpu.roll(o, MLA_V, 1))
        o_ref[:, vsl] = jnp.where(own[0], outs[0], outs[1]).astype(BF16)


def _into_buffer(kernel_fn, buf):
    if buf is None:
        return kernel_fn, [], []

    def without_alias_ref(*refs):
        return kernel_fn(*refs[:-2], refs[-1])

    return without_alias_ref, [pl.BlockSpec(memory_space=pl.ANY)], [buf]


def _mla_attention(qm, km, vm, buf, *, ctx_queries):
    groups = MLA_HEADS // 2 // MLA_PAIRS_PER_STEP
    qk_w = 2 * LANES * MLA_PAIRS_PER_STEP
    v_w = LANES * MLA_PAIRS_PER_STEP
    if ctx_queries:
        tq, n_q = CTX_LEN, 1
        q_map = lambda b, h, i: (N_LAT // CTX_LEN + b, h)
        kv_specs = [(CTX_LEN, lambda b, h, i: (N_LAT // CTX_LEN + b, h))]
    else:
        tq, n_q = TQ_MLA, SEQ // TQ_MLA
        q_map = lambda b, h, i: (b * n_q + i, h)
        kv_specs = [(CTX_LEN, lambda b, h, i: (N_LAT // CTX_LEN + b, h)), (SEQ, lambda b, h, i: (b, h))]
    n_kv = len(kv_specs)
    in_specs = [pl.BlockSpec((tq, qk_w), q_map)]
    in_specs += [pl.BlockSpec((n, qk_w), m) for n, m in kv_specs]
    in_specs += [pl.BlockSpec((n, v_w), m) for n, m in kv_specs]
    kernel_fn, extra_specs, extra_args = _into_buffer(functools.partial(_mla_kernel, n_kv=n_kv), buf)
    n_in = len(in_specs)
    return pl.pallas_call(
        kernel_fn,
        grid=(BATCH, groups, n_q),
        in_specs=in_specs + extra_specs,
        out_specs=pl.BlockSpec((tq, v_w), q_map),
        out_shape=jax.ShapeDtypeStruct((N_TOK, D_MODEL), BF16),
        input_output_aliases={n_in: 0} if extra_args else {},
        compiler_params=_cparams("parallel", "parallel", "parallel"),
        name="mla_ctx" if ctx_queries else "mla_lat",
    )(qm, *([km] * n_kv), *([vm] * n_kv), *extra_args)


def _swa_kernel(sink_ref, *refs, band, n_blocks):
    q_ref = refs[0]
    rows = q_ref.shape[0]
    stacked = SWA_GROUP * rows
    if band:
        kp, kc, kn, kx, vp, vc, vn, vx, o_ref = refs[1:]
        n = pl.program_id(1)
        k_band = jnp.concatenate([kp[...], kc[...], kn[...]], axis=0)
        v_band = jnp.concatenate([vp[...], vc[...], vn[...]], axis=0)
        qq = lax.broadcasted_iota(jnp.int32, (stacked, 3 * SWA_BLOCK), 0) & (SWA_BLOCK - 1)
        kk = lax.broadcasted_iota(jnp.int32, (stacked, 3 * SWA_BLOCK), 1)
        rel = kk - SWA_BLOCK - qq
        key_pos = (n - 1) * SWA_BLOCK + kk
        valid = (jnp.abs(rel) <= WINDOW) & (key_pos >= 0) & (key_pos < n_blocks * SWA_BLOCK)
    else:
        kx, vx, o_ref = refs[1:]
    k_ctx = kx[...]
    v_ctx = vx[...]
    lane = lax.broadcasted_iota(jnp.int32, (rows, LANES), 1)
    low = lane < SWA_HEAD_DIM
    row_group = lax.broadcasted_iota(jnp.int32, (stacked, 1), 0) // rows
    halves = []
    for hk in range(SWA_KV_HEADS):
        keep = low if hk == 0 else jnp.logical_not(low)
        qh = jnp.concatenate(
            [jnp.where(keep, q_ref[:, g * LANES:(g + 1) * LANES], jnp.zeros((rows, LANES), BF16))
             for g in range(SWA_GROUP)], axis=0)
        sink = jnp.zeros((stacked, 1), F32)
        for g in range(SWA_GROUP):
            sink = jnp.where(row_group == g, sink_ref[hk * SWA_GROUP + g] * LOG2E, sink)
        s_ctx = _dot_nt(qh, k_ctx)
        m = jnp.maximum(jnp.max(s_ctx, axis=-1, keepdims=True), sink)
        if band:
            s_band = jnp.where(valid, _dot_nt(qh, k_band), NEG_INF)
            m = jnp.maximum(m, jnp.max(s_band, axis=-1, keepdims=True))
        def ones_other(v, hk=hk):
            vlane = lax.broadcasted_iota(jnp.int32, v.shape, 1)
            own = (vlane < SWA_HEAD_DIM) if hk == 0 else (vlane >= SWA_HEAD_DIM)
            return jnp.where(own, v, jnp.ones_like(v))

        o = _dot(jnp.exp2(s_ctx - m).astype(BF16), ones_other(v_ctx))
        if band:
            o = o + _dot(jnp.exp2(s_band - m).astype(BF16), ones_other(v_band))
        denom = pltpu.roll(o, SWA_HEAD_DIM, 1) + jnp.exp2(sink - m)
        halves.append(o / denom)
    for g in range(SWA_GROUP):
        rs = slice(g * rows, (g + 1) * rows)
        o_ref[:, g * LANES:(g + 1) * LANES] = jnp.where(low, halves[0][rs], halves[1][rs]).astype(BF16)


def _swa_attention(sink, qs, ks, vs, buf, *, ctx_queries):
    width = SWA_HEADS * SWA_HEAD_DIM
    kvw = SWA_KV_HEADS * SWA_HEAD_DIM
    ctx_map = lambda b, i: (N_LAT // CTX_LEN + b, 0)
    smem = pl.BlockSpec(memory_space=pltpu.SMEM)
    if ctx_queries:
        grid = (BATCH, 1)
        in_specs = [smem, pl.BlockSpec((CTX_LEN, width), ctx_map),
                    pl.BlockSpec((CTX_LEN, kvw), ctx_map), pl.BlockSpec((CTX_LEN, kvw), ctx_map)]
        args = (sink, qs, ks, vs)
        out_spec = pl.BlockSpec((CTX_LEN, width), lambda b, i: (N_LAT // CTX_LEN + b, 1))
        n_blocks = 1
    else:
        n_blocks = SEQ // SWA_BLOCK
        grid = (BATCH, n_blocks)
        cur = lambda b, i: (b * n_blocks + i, 0)
        prev = lambda b, i: (b * n_blocks + jnp.maximum(i - 1, 0), 0)
        nxt = lambda b, i: (b * n_blocks + jnp.minimum(i + 1, n_blocks - 1), 0)
        band_specs = [pl.BlockSpec((SWA_BLOCK, kvw), m) for m in (prev, cur, nxt)]
        in_specs = ([smem, pl.BlockSpec((SWA_BLOCK, width), cur)]
                    + band_specs + [pl.BlockSpec((CTX_LEN, kvw), ctx_map)]
                    + band_specs + [pl.BlockSpec((CTX_LEN, kvw), ctx_map)])
        args = (sink, qs, ks, ks, ks, ks, vs, vs, vs, vs)
        out_spec = pl.BlockSpec((SWA_BLOCK, width), lambda b, i: (b * n_blocks + i, 1))
    kernel_fn, extra_specs, extra_args = _into_buffer(
        functools.partial(_swa_kernel, band=not ctx_queries, n_blocks=n_blocks), buf)
    return pl.pallas_call(
        kernel_fn,
        grid=grid,
        in_specs=in_specs + extra_specs,
        out_specs=out_spec,
        out_shape=jax.ShapeDtypeStruct((N_TOK, D_MODEL), BF16),
        input_output_aliases={len(in_specs): 0} if extra_args else {},
        compiler_params=_cparams("parallel", "parallel"),
        name="swa_ctx" if ctx_queries else "swa_lat",
    )(*args, *extra_args)


def _diff_kernel(q_ref, kc_ref, kl_ref, vc_ref, vl_ref, lam_ref, g_ref, o_ref, *, lam_init):
    lam = (jnp.exp(jnp.sum(lam_ref[0:1, :] * lam_ref[1:2, :], axis=-1, keepdims=True))
           - jnp.exp(jnp.sum(lam_ref[2:3, :] * lam_ref[3:4, :], axis=-1, keepdims=True)) + lam_init)
    lane = lax.broadcasted_iota(jnp.int32, (q_ref.shape[0], LANES), 1)
    low = lane < DIFF_HALF
    k_refs = (kc_ref, kl_ref)
    v_refs = (vc_ref, vl_ref)
    for h in range(DIFF_HEADS_PER_STEP):
        sl = slice(h * LANES, (h + 1) * LANES)
        q = q_ref[:, sl]
        zero = jnp.zeros_like(q)

        def numerators(qh):
            ss = [_dot_nt(qh, k[:, sl]) for k in k_refs]
            m = functools.reduce(jnp.maximum, [jnp.max(s, axis=-1, keepdims=True) for s in ss])
            es = [jnp.exp2(s - m) for s in ss]
            return es, functools.reduce(jnp.add, [jnp.sum(e, axis=-1, keepdims=True) for e in es])

        e1, l1 = numerators(jnp.where(low, q, zero))
        e2, l2 = numerators(jnp.where(low, zero, q))
        c = lam * l1 / l2
        o = functools.reduce(jnp.add, [_dot((a - c * b).astype(BF16), v[:, sl])
                                       for a, b, v in zip(e1, e2, v_refs)])
        o = _rms(o / l1) * g_ref[...] * (1.0 - lam_init)
        o_ref[:, sl] = o.astype(BF16)


def _diff_attention(q, k, v, lam_vecs, g_sub, lam_init):
    n_q = SEQ // TQ_DIFF
    q_map = lambda b, h, i: (b * n_q + i, h)
    ctx_map = lambda b, h, i: (N_LAT // CTX_LEN + b, h)
    lat_map = lambda b, h, i: (b, h)
    const = lambda b, h, i: (0, 0)
    width = LANES * DIFF_HEADS_PER_STEP
    return pl.pallas_call(
        functools.partial(_diff_kernel, lam_init=lam_init),
        grid=(BATCH, DIFF_HEADS // DIFF_HEADS_PER_STEP, n_q),
        in_specs=[
            pl.BlockSpec((TQ_DIFF, width), q_map),
            pl.BlockSpec((CTX_LEN, width), ctx_map),
            pl.BlockSpec((SEQ, width), lat_map),
            pl.BlockSpec((CTX_LEN, width), ctx_map),
            pl.BlockSpec((SEQ, width), lat_map),
            pl.BlockSpec(lam_vecs.shape, const),
            pl.BlockSpec(g_sub.shape, const),
        ],
        out_specs=pl.BlockSpec((TQ_DIFF, width), q_map),
        out_shape=jax.ShapeDtypeStruct((N_LAT, DIFF_HEADS * DIFF_V), BF16),
        compiler_params=_cparams("parallel", "parallel", "parallel"),
        name="diff_attn",
    )(q, k, k, v, v, lam_vecs, g_sub)


def _pack_bf16_pairs(x):
    n = x.shape[1] // 2
    lo = pltpu.bitcast(x[:, :n].astype(BF16).astype(F32), jnp.uint32)
    hi = pltpu.bitcast(x[:, n:].astype(BF16).astype(F32), jnp.uint32)
    return pltpu.bitcast((lo >> 16) | hi, jnp.int32)


def _unpack_bf16_pairs(w):
    u = pltpu.bitcast(w, jnp.uint32)
    lo = pltpu.bitcast(u << 16, F32)
    hi = pltpu.bitcast(u & jnp.uint32(0xFFFF0000), F32)
    return lo, hi


def _route(logits_t, bias, cnt_ref, tri_ref, idx_ref, w_ref, rank_ref):
    tm = logits_t.shape[1]
    scores = jax.nn.sigmoid(logits_t)
    biased = scores + bias
    sub = lax.broadcasted_iota(jnp.int32, (GROUP_SIZE, tm), 0).astype(F32)
    grp_scores, grp_biased, grp_index = [], [], []
    group_score = []
    for g in range(N_GROUPS):
        sl = slice(g * GROUP_SIZE, (g + 1) * GROUP_SIZE)
        bg = biased[sl, :]
        grp_scores.append(scores[sl, :])
        grp_biased.append(bg)
        grp_index.append(sub + float(g * GROUP_SIZE))
        m1 = jnp.max(bg, axis=0, keepdims=True)
        first = jnp.min(jnp.where(bg == m1, sub, float(GROUP_SIZE)), axis=0, keepdims=True)
        m2 = jnp.max(jnp.where(sub == first, -jnp.inf, bg), axis=0, keepdims=True)
        group_score.append(m1 + m2)
    keep = [jnp.zeros((1, tm), F32) for _ in range(N_GROUPS)]
    for _ in range(TOPK_GROUPS):
        m = functools.reduce(jnp.maximum, group_score)
        found = jnp.zeros((1, tm), F32)
        for g in range(N_GROUPS):
            hit = jnp.where(group_score[g] == m, 1.0 - found, 0.0)
            keep[g] = keep[g] + hit
            found = found + hit
            group_score[g] = jnp.where(hit > 0.0, -jnp.inf, group_score[g])
    vals = [jnp.where(keep[g] > 0.0, grp_biased[g], NEG_INF) for g in range(N_GROUPS)]
    chosen = [jnp.zeros((GROUP_SIZE, tm), F32) for _ in range(N_GROUPS)]
    picked = []
    for _ in range(TOP_K):
        m = jnp.max(functools.reduce(jnp.maximum, vals), axis=0, keepdims=True)
        cand = [jnp.where(vals[g] == m, grp_index[g], float(N_EXPERTS)) for g in range(N_GROUPS)]
        ei = jnp.min(functools.reduce(jnp.minimum, cand), axis=0, keepdims=True)
        sel = [grp_index[g] == ei for g in range(N_GROUPS)]
        s_k = functools.reduce(jnp.add, [jnp.where(sel[g], grp_scores[g], 0.0) for g in range(N_GROUPS)])
        picked.append((ei, jnp.sum(s_k, axis=0, keepdims=True)))
        vals = [jnp.where(sel[g], -jnp.inf, vals[g]) for g in range(N_GROUPS)]
        chosen = [jnp.where(sel[g], 1.0, chosen[g]) for g in range(N_GROUPS)]
    total = functools.reduce(jnp.add, [s for _, s in picked])
    chosen_all = jnp.concatenate(chosen, axis=0)
    rank_all = _dot(chosen_all.astype(BF16), tri_ref[...]) + cnt_ref[...]
    cnt_ref[...] = cnt_ref[...] + jnp.sum(chosen_all, axis=1, keepdims=True)
    for k, (ei, s) in enumerate(picked):
        idx_ref[k:k + 1, :] = ei.astype(jnp.int32)
        w_ref[k:k + 1, :] = s / total * ROUTED_SCALE
        r = functools.reduce(jnp.add, [
            jnp.where(grp_index[g] == ei, rank_all[g * GROUP_SIZE:(g + 1) * GROUP_SIZE, :], 0.0)
            for g in range(N_GROUPS)])
        rank_ref[k:k + 1, :] = jnp.sum(r, axis=0, keepdims=True).astype(jnp.int32)


def _post_kernel(o_ref, h_ref, wout_ref, gtm_ref, g_ref, sh_ref, sc_ref, gtf_ref,
                 wrh_ref, wrl_ref, br_ref, tri_ref, wsg_ref, wsu_ref, wsd_ref,
                 hs_ref, f_ref, idx_ref, w_ref, rank_ref, cnt_out_ref, cnt_ref):
    @pl.when(pl.program_id(0) == 0)
    def _():
        cnt_ref[...] = jnp.zeros_like(cnt_ref)

    h1 = h_ref[...] + gtm_ref[0] * _dot(o_ref[...], wout_ref[...])
    f = _rms(h1) * g_ref[...]
    f = f * (1.0 + sc_ref[0]) + sh_ref[0]
    f_hi = f.astype(BF16)
    f_lo = (f - f_hi.astype(F32)).astype(BF16)
    f_ref[...] = _pack_bf16_pairs(f)
    logits_t = (_dot_nt(wrh_ref[...], f_hi) + _dot_nt(wrh_ref[...], f_lo) + _dot_nt(wrl_ref[...], f_hi))
    _route(logits_t, br_ref[...], cnt_ref, tri_ref, idx_ref, w_ref, rank_ref)
    cnt_out_ref[...] = jnp.broadcast_to(cnt_ref[...], cnt_out_ref.shape).astype(jnp.int32)
    mid = _silu(_dot(f_hi, wsg_ref[...])) * _dot(f_hi, wsu_ref[...])
    shared = _dot(mid.astype(BF16), wsd_ref[...])
    hs_ref[...] = h1 + gtf_ref[0] * shared


def _post(o, h, wout, gtm, g_ffn, sh, sc, gtf, wrh, wrl, br, wsg, wsu, wsd, n_rows):
    row = lambda i: (i, 0)
    col = lambda i: (0, i)
    const = lambda i: (0, 0)
    mod = lambda i: (_mod_index(i), 0, 0)
    vec = pl.BlockSpec((1, 1, D_MODEL), mod)
    full = lambda a: pl.BlockSpec(a.shape, const)
    tri = (jnp.arange(TM)[:, None] < jnp.arange(TM)[None, :]).astype(BF16)
    return pl.pallas_call(
        _post_kernel,
        grid=(n_rows // TM,),
        in_specs=[
            pl.BlockSpec((TM, D_MODEL), row),
            pl.BlockSpec((TM, D_MODEL), row),
            full(wout), vec, full(g_ffn), vec, vec, vec,
            full(wrh), full(wrl), full(br), full(tri), full(wsg), full(wsu), full(wsd),
        ],
        out_specs=[
            pl.BlockSpec((TM, D_MODEL), row),
            pl.BlockSpec((TM, D_MODEL // 2), row),
            pl.BlockSpec((TOP_K, TM), col),
            pl.BlockSpec((TOP_K, TM), col),
            pl.BlockSpec((TOP_K, TM), col),
            pl.BlockSpec((N_EXPERTS, LANES), const),
        ],
        out_shape=[
            jax.ShapeDtypeStruct((n_rows, D_MODEL), F32),
            jax.ShapeDtypeStruct((n_rows, D_MODEL // 2), jnp.int32),
            jax.ShapeDtypeStruct((TOP_K, n_rows), jnp.int32),
            jax.ShapeDtypeStruct((TOP_K, n_rows), F32),
            jax.ShapeDtypeStruct((TOP_K, n_rows), jnp.int32),
            jax.ShapeDtypeStruct((N_EXPERTS, LANES), jnp.int32),
        ],
        scratch_shapes=[pltpu.VMEM((N_EXPERTS, 1), F32)],
        compiler_params=_cparams("arbitrary"),
        name="post_attn",
    )(o, h, wout, gtm, g_ffn, sh, sc, gtf, wrh, wrl, br, tri, wsg, wsu, wsd)


DEST_TN = 2048


def _dest_kernel(ps_ref, idx_ref, rank_ref, o_ref):
    idx = idx_ref[...]
    acc = rank_ref[...]
    for e in range(N_EXPERTS):
        acc = acc + jnp.where(idx == e, ps_ref[e], 0)
    o_ref[...] = acc


def _dest(pad_start, top_idx, rank):
    n_tok = top_idx.shape[1]
    blk = pl.BlockSpec((TOP_K, DEST_TN), lambda i: (0, i))
    return pl.pallas_call(
        _dest_kernel,
        grid=(n_tok // DEST_TN,),
        in_specs=[pl.BlockSpec(memory_space=pltpu.SMEM), blk, blk],
        out_specs=blk,
        out_shape=jax.ShapeDtypeStruct((TOP_K, n_tok), jnp.int32),
        compiler_params=_cparams("parallel"),
        name="dest",
    )(pad_start, top_idx, rank)


SC_CORES = 2
SC_SUBCORES = 16
SC_WORKERS = SC_CORES * SC_SUBCORES
SC_DISPATCH_ROWS = 32
SC_GATHER_ROWS = 64
ROW_WORDS = D_MODEL // 2


def _sc_mesh():
    return plsc.VectorSubcoreMesh(core_axis_name="core", subcore_axis_name="subcore")


def _sc_worker():
    return lax.axis_index("subcore") * SC_CORES + lax.axis_index("core")


def _sc_dispatch(f_packed, dest, n_slots):
    n_tok = f_packed.shape[0]
    per_worker = n_tok // SC_WORKERS
    n_chunks = per_worker // SC_DISPATCH_ROWS
    assert n_chunks * SC_DISPATCH_ROWS * SC_WORKERS == n_tok and n_chunks % 2 == 0
    dest4 = dest.reshape(TOP_K, SC_WORKERS, n_chunks, SC_DISPATCH_ROWS)

    def body(f_hbm, dest_hbm, xs_hbm, rows0, rows1, idx_v, load0, load1, scat0, scat1):
        wid = _sc_worker()
        for k in range(TOP_K):
            pltpu.sync_copy(dest_hbm.at[k, wid], idx_v.at[k])

        def load(c, buf, sem):
            start = pl.multiple_of(wid * per_worker + c * SC_DISPATCH_ROWS, 8)
            return pltpu.make_async_copy(f_hbm.at[pl.ds(start, SC_DISPATCH_ROWS)], buf, sem)

        def scatters(c, buf, sem):
            return [pltpu.make_async_copy(buf, xs_hbm.at[idx_v.at[k, c]], sem) for k in range(TOP_K)]

        load(0, rows0, load0).start()

        @pl.loop(0, n_chunks, step=2)
        def _(c):
            @pl.when(c > 0)
            def _():
                for cp in scatters(c - 1, rows1, scat1):
                    cp.wait()

            load(c + 1, rows1, load1).start()
            load(c, rows0, load0).wait()
            for cp in scatters(c, rows0, scat0):
                cp.start()
            load(c + 1, rows1, load1).wait()
            for cp in scatters(c + 1, rows1, scat1):
                cp.start()
            for cp in scatters(c, rows0, scat0):
                cp.wait()

            @pl.when(c + 2 < n_chunks)
            def _():
                load(c + 2, rows0, load0).start()

        for cp in scatters(n_chunks - 1, rows1, scat1):
            cp.wait()

    rows = pltpu.VMEM((SC_DISPATCH_ROWS, ROW_WORDS), jnp.int32)
    return pl.kernel(
        body,
        out_type=jax.ShapeDtypeStruct((n_slots, ROW_WORDS), jnp.int32),
        mesh=_sc_mesh(),
        scratch_types=[rows, rows, pltpu.VMEM((TOP_K, n_chunks, SC_DISPATCH_ROWS), jnp.int32)]
        + [pltpu.SemaphoreType.DMA] * 4,
        name="sc_dispatch",
    )(f_packed, dest4)


def _sc_gather(ys, dest):
    n_idx = dest.shape[0] * dest.shape[1]
    per_worker = n_idx // SC_WORKERS
    n_chunks = per_worker // SC_GATHER_ROWS
    assert n_chunks * SC_GATHER_ROWS * SC_WORKERS == n_idx and n_chunks % 2 == 0
    dest3 = dest.reshape(SC_WORKERS, n_chunks, SC_GATHER_ROWS)

    def body(ys_hbm, dest_hbm, out_hbm, rows0, rows1, idx_v, gat0, gat1, put0, put1):
        wid = _sc_worker()
        pltpu.sync_copy(dest_hbm.at[wid], idx_v)

        def gather(c, buf, sem):
            return pltpu.make_async_copy(ys_hbm.at[idx_v.at[c]], buf, sem)

        def put(c, buf, sem):
            start = pl.multiple_of(wid * per_worker + c * SC_GATHER_ROWS, 8)
            return pltpu.make_async_copy(buf, out_hbm.at[pl.ds(start, SC_GATHER_ROWS)], sem)

        gather(0, rows0, gat0).start()

        @pl.loop(0, n_chunks, step=2)
        def _(c):
            @pl.when(c > 0)
            def _():
                put(c - 1, rows1, put1).wait()

            gather(c + 1, rows1, gat1).start()
            gather(c, rows0, gat0).wait()
            put(c, rows0, put0).start()
            gather(c + 1, rows1, gat1).wait()
            put(c + 1, rows1, put1).start()
            put(c, rows0, put0).wait()

            @pl.when(c + 2 < n_chunks)
            def _():
                gather(c + 2, rows0, gat0).start()

        put(n_chunks - 1, rows1, put1).wait()

    rows = pltpu.VMEM((SC_GATHER_ROWS, ROW_WORDS), jnp.int32)
    return pl.kernel(
        body,
        out_type=jax.ShapeDtypeStruct((n_idx, ROW_WORDS), jnp.int32),
        mesh=_sc_mesh(),
        scratch_types=[rows, rows, pltpu.VMEM((n_chunks, SC_GATHER_ROWS), jnp.int32)]
        + [pltpu.SemaphoreType.DMA] * 4,
        name="sc_gather",
    )(ys, dest3)


def _moe_kernel(be_ref, nused_ref, x_ref, wg_ref, wu_ref, wd_ref, o_ref, wg_s, wu_s, wd_s):
    i = pl.program_id(0)
    e = be_ref[i]
    changed = jnp.logical_or(i == 0, e != be_ref[jnp.maximum(i - 1, 0)])

    @pl.when(changed)
    def _():
        wg_s[...] = wg_ref[0, 0].astype(BF16)
        wu_s[...] = wu_ref[0, 0].astype(BF16)
        wd_s[...] = wd_ref[0, 0].astype(BF16)

    @pl.when(i < nused_ref[0])
    def _():
        half = D_MODEL // 2
        for c in range(MOE_ROWS // MOE_CHUNK):
            rows = slice(c * MOE_CHUNK, (c + 1) * MOE_CHUNK)
            lo, hi = _unpack_bf16_pairs(x_ref[rows, :])
            lo, hi = lo.astype(BF16), hi.astype(BF16)
            gate = _dot(lo, wg_s[:half, :]) + _dot(hi, wg_s[half:, :])
            up = _dot(lo, wu_s[:half, :]) + _dot(hi, wu_s[half:, :])
            mid = _silu(gate) * up
            o_ref[rows, :] = _pack_bf16_pairs(_dot(mid.astype(BF16), wd_s[...]))


def _moe(block_expert, n_used, xs, wg, wu, wd, layer):
    n_blocks = xs.shape[0] // MOE_ROWS
    expert = lambda i, be, nu: (layer, be[i], 0, 0)
    grid_spec = pltpu.PrefetchScalarGridSpec(
        num_scalar_prefetch=2,
        grid=(n_blocks,),
        in_specs=[
            pl.BlockSpec((MOE_ROWS, ROW_WORDS), lambda i, be, nu: (i, 0)),
            pl.BlockSpec((1, 1, D_MODEL, EXPERT_FF), expert),
            pl.BlockSpec((1, 1, D_MODEL, EXPERT_FF), expert),
            pl.BlockSpec((1, 1, EXPERT_FF, D_MODEL), expert),
        ],
        out_specs=pl.BlockSpec((MOE_ROWS, ROW_WORDS), lambda i, be, nu: (i, 0)),
        scratch_shapes=[
            pltpu.VMEM((D_MODEL, EXPERT_FF), BF16),
            pltpu.VMEM((D_MODEL, EXPERT_FF), BF16),
            pltpu.VMEM((EXPERT_FF, D_MODEL), BF16),
        ],
    )
    return pl.pallas_call(
        _moe_kernel,
        grid_spec=grid_spec,
        out_shape=jax.ShapeDtypeStruct((xs.shape[0], ROW_WORDS), jnp.int32),
        compiler_params=_cparams("arbitrary"),
        name="moe_experts",
    )(block_expert, n_used, xs, wg, wu, wd)


def _combine_kernel(y_ref, w_ref, hs_ref, gtf_ref, gfin_ref, o_ref, *, final):
    w = w_ref[...]
    acc_lo = acc_hi = None
    for k in range(TOP_K):
        lo, hi = _unpack_bf16_pairs(y_ref[k])
        wk = w[:, k:k + 1]
        acc_lo = lo * wk if k == 0 else acc_lo + lo * wk
        acc_hi = hi * wk if k == 0 else acc_hi + hi * wk
    out = hs_ref[...] + gtf_ref[0] * jnp.concatenate([acc_lo, acc_hi], axis=1)
    if final:
        out = _rms(out) * gfin_ref[...]
    o_ref[...] = out


def _combine(y, w, hs, gtf, g_final, *, final):
    n_rows = hs.shape[0]
    return pl.pallas_call(
        functools.partial(_combine_kernel, final=final),
        grid=(n_rows // TM_COMB,),
        in_specs=[
            pl.BlockSpec((TOP_K, TM_COMB, ROW_WORDS), lambda i: (0, i, 0)),
            pl.BlockSpec((TM_COMB, TOP_K), lambda i: (i, 0)),
            pl.BlockSpec((TM_COMB, D_MODEL), lambda i: (i, 0)),
            pl.BlockSpec((1, 1, D_MODEL), lambda i: (jnp.minimum(i // (SEQ // TM_COMB), BATCH), 0, 0)),
            pl.BlockSpec((1, D_MODEL), lambda i: (0, 0)),
        ],
        out_specs=pl.BlockSpec((TM_COMB, D_MODEL), lambda i: (i, 0)),
        out_shape=jax.ShapeDtypeStruct((n_rows, D_MODEL), F32),
        compiler_params=_cparams("parallel"),
        name="combine_final" if final else "combine",
    )(y, w, hs, gtf, g_final)


def _rope_tables():
    rows = SEQ // GRID_W

    def angles(rot_dim):
        half = rot_dim // 2
        inv_freq = ROPE_BASE ** (-jnp.arange(0, half, 2, dtype=F32) / half)
        row = jnp.repeat(jnp.arange(rows, dtype=F32), GRID_W)
        col = jnp.tile(jnp.arange(GRID_W, dtype=F32), rows)
        ang_r = row[:, None] * inv_freq
        ang_c = col[:, None] * inv_freq
        return jnp.concatenate([ang_r, ang_r, ang_c, ang_c], axis=-1)

    def signed(sin, quarter):
        sign = jnp.where((jnp.arange(sin.shape[-1]) // quarter) % 2 == 0, -1.0, 1.0)
        return sin * sign

    def with_identity(cos, sin):
        cos = jnp.concatenate([cos, jnp.ones((TM, LANES), F32)], axis=0)
        sin = jnp.concatenate([sin, jnp.zeros((TM, LANES), F32)], axis=0)
        return cos, sin

    a64 = angles(SWA_HEAD_DIM)
    c64 = jnp.tile(jnp.cos(a64), (1, LANES // SWA_HEAD_DIM))
    s64 = jnp.tile(signed(jnp.sin(a64), SWA_HEAD_DIM // 4), (1, LANES // SWA_HEAD_DIM))
    a32 = angles(MLA_ROPE)
    pad_lo = MLA_NOPE
    pad_hi = LANES - MLA_NOPE - MLA_ROPE
    cm = jnp.concatenate([jnp.ones((SEQ, pad_lo), F32), jnp.cos(a32), jnp.ones((SEQ, pad_hi), F32)], axis=-1)
    sm = jnp.concatenate([jnp.zeros((SEQ, pad_lo), F32), signed(jnp.sin(a32), MLA_ROPE // 4),
                          jnp.zeros((SEQ, pad_hi), F32)], axis=-1)
    return with_identity(cm, sm) + with_identity(c64, s64)


def _layer0_weights(wa_in, wa_uq, wa_ukv, wa_out):
    d = D_MODEL
    cq, ckv, kr, qs, ks, vs = jnp.split(
        wa_in, [C_CKV, C_KR, C_KR + MLA_ROPE, C_KR + MLA_ROPE + SWA_HEADS * SWA_HEAD_DIM,
                C_KR + MLA_ROPE + (SWA_HEADS + SWA_KV_HEADS) * SWA_HEAD_DIM], axis=-1)
    kr_pad = jnp.concatenate([jnp.zeros((d, MLA_NOPE), F32), kr,
                              jnp.zeros((d, LANES - MLA_NOPE - MLA_ROPE), F32)], axis=-1)
    qs_pair = qs.reshape(d, SWA_KV_HEADS, SWA_GROUP, SWA_HEAD_DIM).transpose(0, 2, 1, 3).reshape(d, -1)
    win = jnp.concatenate([cq, ckv, kr_pad, qs_pair, ks, vs], axis=-1).astype(BF16)
    uq = wa_uq.reshape(MLA_Q_RANK, MLA_HEADS, MLA_NOPE + MLA_ROPE)
    uq = jnp.pad(uq, ((0, 0), (0, 0), (0, LANES - MLA_NOPE - MLA_ROPE))).reshape(MLA_Q_RANK, -1)
    ukv = wa_ukv.reshape(MLA_KV_RANK, MLA_HEADS, MLA_NOPE + MLA_V)
    uk = jnp.pad(ukv[:, :, :MLA_NOPE], ((0, 0), (0, 0), (0, LANES - MLA_NOPE))).reshape(MLA_KV_RANK, -1)
    uv = ukv[:, :, MLA_NOPE:].reshape(MLA_KV_RANK, -1)
    wukv = jnp.concatenate([uk, uv], axis=-1)
    n_mla = MLA_HEADS * MLA_V
    out_swa = wa_out[n_mla:].reshape(SWA_KV_HEADS, SWA_GROUP, SWA_HEAD_DIM, d).transpose(1, 0, 2, 3)
    wout = jnp.concatenate([wa_out[:n_mla], out_swa.reshape(-1, d)], axis=0)
    return win, uq.astype(BF16), wukv.astype(BF16), wout.astype(BF16)


def _layer1_weights(wc_in):
    per_head = 4 * DIFF_HALF + DIFF_V
    w = wc_in.reshape(D_MODEL, DIFF_HEADS, per_head)
    q = w[:, :, :2 * DIFF_HALF].reshape(D_MODEL, -1)
    k = w[:, :, 2 * DIFF_HALF:4 * DIFF_HALF].reshape(D_MODEL, -1)
    v = w[:, :, 4 * DIFF_HALF:].reshape(D_MODEL, -1)
    return jnp.concatenate([q, k, v], axis=-1).astype(BF16)


def _block_tables(counts, n_blocks):
    padded = (counts + MOE_ROWS - 1) // MOE_ROWS * MOE_ROWS
    pad_end = jnp.cumsum(padded)
    pad_start = (pad_end - padded).astype(jnp.int32)
    block_start = jnp.arange(n_blocks, dtype=jnp.int32) * MOE_ROWS
    block_expert = jnp.minimum((pad_end[None, :] <= block_start[:, None]).sum(axis=1),
                               N_EXPERTS - 1).astype(jnp.int32)
    n_used = (pad_end[-1] // MOE_ROWS).astype(jnp.int32).reshape(1)
    return pad_start, block_expert, n_used


def _moe_layer(f_packed, top_idx, top_w, rank, counts, hs, gtf, g_final, wg, wu, wd, layer, *, final):
    n_tok = top_idx.shape[1]
    n_blocks = -(-(n_tok * TOP_K) // MOE_ROWS) + N_EXPERTS
    pad_start, block_expert, n_used = _block_tables(counts[:, 0], n_blocks)
    dest = _dest(pad_start, top_idx, rank)
    xs = _sc_dispatch(f_packed, dest, n_blocks * MOE_ROWS)
    ys = _moe(block_expert, n_used, xs, wg, wu, wd, layer)
    y = _sc_gather(ys, dest).reshape(TOP_K, n_tok, ROW_WORDS)
    return _combine(y, top_w.T, hs, gtf, g_final, final=final)


def kernel(x, c, ctx, c_ctx, w_ada, b_ada, g_mix, g_ffn, wa_in, ga_q, ga_kv, wa_uq, wa_ukv, wa_sink, wa_out,
           wc_in, lam_q1, lam_k1, lam_q2, lam_k2, gc_sub, wc_out, w_router, b_router, we_gate, we_up, we_down,
           ws_gate, ws_up, ws_down, g_final):
    d = D_MODEL
    h = jnp.concatenate([x.reshape(N_LAT, d), ctx.reshape(N_CTX, d)], axis=0)
    cc = jnp.concatenate([c, c_ctx[None, :], jnp.zeros((MOD_ROWS - BATCH - 1, d), F32)], axis=0)
    mod = _ada(cc, w_ada, b_ada).reshape(DEPTH, MOD_ROWS, 6, 1, d)

    def mod_vec(layer, j):
        return mod[layer, :, j]

    cm, sm, c64, s64 = _rope_tables()
    g_final2 = g_final.reshape(1, d)

    def router_weights(layer):
        wt = w_router[layer].T
        hi = wt.astype(BF16)
        lo = (wt - hi.astype(F32)).astype(BF16)
        return hi, lo, b_router[layer].reshape(N_EXPERTS, 1)

    def shared_weights(layer):
        return ws_gate[layer].astype(BF16), ws_up[layer].astype(BF16), ws_down[layer].astype(BF16)

    win, wuq, wukv, wout0 = _layer0_weights(wa_in[0], wa_uq[0], wa_ukv[0], wa_out[0])
    qm, km, vm, qs, ks, vs = _proj0(h, g_mix[0].reshape(1, d), mod_vec(0, 0), mod_vec(0, 1), win,
                                    ga_q[0].reshape(1, -1), ga_kv[0].reshape(1, -1), wuq, wukv, cm, sm, c64, s64)
    sink = wa_sink[0].astype(F32)
    o = _mla_attention(qm, km, vm, None, ctx_queries=False)
    o = _mla_attention(qm, km, vm, o, ctx_queries=True)
    o = _swa_attention(sink, qs, ks, vs, o, ctx_queries=False)
    o = _swa_attention(sink, qs, ks, vs, o, ctx_queries=True)
    hs, f, top_idx, top_w, rank, counts = _post(
        o, h, wout0, mod_vec(0, 2), g_ffn[0].reshape(1, d), mod_vec(0, 3), mod_vec(0, 4), mod_vec(0, 5),
        *router_weights(0), *shared_weights(0), N_TOK)
    h = _moe_layer(f, top_idx, top_w, rank, counts, hs, mod_vec(0, 5), g_final2,
                   we_gate, we_up, we_down, 0, final=False)

    layer = 1
    lam_init = 0.8 - 0.6 * math.exp(-0.3 * layer)
    q1, k1, v1 = _proj1(h, g_mix[1].reshape(1, d), mod_vec(1, 0), mod_vec(1, 1), _layer1_weights(wc_in[0]),
                        c64, s64)
    lam_vecs = jnp.zeros((8, LANES), F32).at[:4, :DIFF_HALF].set(
        jnp.stack([lam_q1[0], lam_k1[0], lam_q2[0], lam_k2[0]]).astype(F32))
    o = _diff_attention(q1, k1, v1, lam_vecs, gc_sub[0].reshape(1, DIFF_V), lam_init)
    hs, f, top_idx, top_w, rank, counts = _post(
        o, h, wc_out[0].astype(BF16), mod_vec(1, 2), g_ffn[1].reshape(1, d), mod_vec(1, 3), mod_vec(1, 4),
        mod_vec(1, 5), *router_weights(1), *shared_weights(1), N_LAT)
    out = _moe_layer(f, top_idx, top_w, rank, counts, hs, mod_vec(1, 5), g_final2,
                     we_gate, we_up, we_down, 1, final=True)
    return out.reshape(BATCH, SEQ, d)
```

```python
import functools
import math

import jax
import jax.numpy as jnp
from jax import lax
from jax.experimental import pallas as pl
from jax.experimental.pallas import tpu as pltpu
from jax.experimental.pallas import tpu_sc as plsc

F32 = jnp.float32
BF16 = jnp.bfloat16

D_MODEL = 1024
BATCH = 8
SEQ = 2048
DEPTH = 2
CTX_LEN = 256
GRID_W = 64
ROPE_BASE = 10000.0
EPS = 1e-6
NEG_INF = -1e30

MLA_HEADS = 8
MLA_Q_RANK = 384
MLA_KV_RANK = 256
MLA_NOPE = 64
MLA_ROPE = 32
MLA_V = 64
SWA_HEADS = 8
SWA_KV_HEADS = 2
SWA_HEAD_DIM = 64
SWA_GROUP = SWA_HEADS // SWA_KV_HEADS
WINDOW = 128
DIFF_HEADS = 8
DIFF_HALF = 64
DIFF_V = 128
N_EXPERTS = 64
N_GROUPS = 8
GROUP_SIZE = N_EXPERTS // N_GROUPS
TOPK_GROUPS = 4
TOP_K = 8
EXPERT_FF = 256
SHARED_FF = 256
ROUTED_SCALE = 2.5

LOG2E = math.log2(math.e)
LANES = 128
N_LAT = BATCH * SEQ
N_CTX = BATCH * CTX_LEN
N_TOK = N_LAT + N_CTX
MOD_ROWS = 16

TM = 512
TM_CHUNK = 256
LAT_BLOCKS_PER_BATCH = SEQ // TM
TQ_MLA = 512
MLA_PAIRS_PER_STEP = 4
TQ_DIFF = 512
DIFF_HEADS_PER_STEP = 4
SWA_BLOCK = 128
MOE_ROWS = 512
MOE_CHUNK = 256
TM_COMB = 256
VMEM_LIMIT = 56 * 1024 * 1024

C_CQ = 0
C_CKV = C_CQ + MLA_Q_RANK
C_KR = C_CKV + MLA_KV_RANK
C_QS = C_KR + LANES
C_KS = C_QS + SWA_HEADS * SWA_HEAD_DIM
C_VS = C_KS + SWA_KV_HEADS * SWA_HEAD_DIM
C_END = C_VS + SWA_KV_HEADS * SWA_HEAD_DIM


def _cparams(*sem):
    return pltpu.CompilerParams(dimension_semantics=sem, vmem_limit_bytes=VMEM_LIMIT)


def _dot(a, b):
    return jnp.dot(a, b, preferred_element_type=F32)


def _dot_nt(a, b):
    return lax.dot_general(a, b, (((1,), (1,)), ((), ())), preferred_element_type=F32)


def _rms(x):
    return x * lax.rsqrt(jnp.mean(x * x, axis=-1, keepdims=True) + EPS)


def _silu(x):
    return x * jax.nn.sigmoid(x)


def _rope(x, cos, sin_signed, shift):
    n = x.shape[-1]
    lane = lax.broadcasted_iota(jnp.int32, x.shape, 1)
    first = (lane & shift) == 0
    rot = jnp.where(first, pltpu.roll(x, n - shift, 1), pltpu.roll(x, shift, 1))
    return x * cos + rot * sin_signed


def _mod_index(i):
    return jnp.minimum(i // LAT_BLOCKS_PER_BATCH, BATCH)


def _rope_index(i):
    return jnp.where(i < N_LAT // TM, i % LAT_BLOCKS_PER_BATCH, LAT_BLOCKS_PER_BATCH)


ADA_TN = 1536


def _ada_kernel(c_ref, w_ref, b_ref, o_ref):
    s = _silu(c_ref[...]).astype(BF16)
    o_ref[0] = _dot(s, w_ref[0].astype(BF16)) + b_ref[0]


def _ada(cc, w_ada, b_ada):
    n_out = w_ada.shape[-1]
    return pl.pallas_call(
        _ada_kernel,
        grid=(DEPTH, n_out // ADA_TN),
        in_specs=[
            pl.BlockSpec((MOD_ROWS, D_MODEL), lambda l, j: (0, 0)),
            pl.BlockSpec((1, D_MODEL, ADA_TN), lambda l, j: (l, 0, j)),
            pl.BlockSpec((1, 1, ADA_TN), lambda l, j: (l, 0, j)),
        ],
        out_specs=pl.BlockSpec((1, MOD_ROWS, ADA_TN), lambda l, j: (l, 0, j)),
        out_shape=jax.ShapeDtypeStruct((DEPTH, MOD_ROWS, n_out), F32),
        compiler_params=_cparams("parallel", "parallel"),
        name="ada",
    )(cc, w_ada, b_ada.reshape(DEPTH, 1, n_out))


def _proj0_kernel(h_ref, g_ref, sh_ref, sc_ref, win_ref, gq_ref, gkv_ref, wuq_ref, wukv_ref,
                  cm_ref, sm_ref, c64_ref, s64_ref,
                  qm_ref, km_ref, vm_ref, qs_ref, ks_ref, vs_ref):
    q_scale = (MLA_NOPE + MLA_ROPE) ** -0.5 * LOG2E
    s_scale = SWA_HEAD_DIM ** -0.5 * LOG2E
    for c in range(TM // TM_CHUNK):
        rows = slice(c * TM_CHUNK, (c + 1) * TM_CHUNK)
        a = _rms(h_ref[rows, :]) * g_ref[...]
        a = a * (1.0 + sc_ref[0]) + sh_ref[0]
        p = _dot(a.astype(BF16), win_ref[...])
        nq = _rms(p[:, C_CQ:C_CKV]) * gq_ref[...]
        nkv = _rms(p[:, C_CKV:C_KR]) * gkv_ref[...]
        q = _dot(nq.astype(BF16), wuq_ref[...])
        kv = _dot(nkv.astype(BF16), wukv_ref[...])
        cm, sm = cm_ref[rows, :], sm_ref[rows, :]
        c64, s64 = c64_ref[rows, :], s64_ref[rows, :]
        kr = _rope(p[:, C_KR:C_QS], cm, sm, MLA_ROPE // 4)
        for h in range(MLA_HEADS):
            sl = slice(h * LANES, (h + 1) * LANES)
            qm_ref[rows, sl] = (_rope(q[:, sl], cm, sm, MLA_ROPE // 4) * q_scale).astype(BF16)
            km_ref[rows, sl] = (kv[:, sl] + kr).astype(BF16)
        vm_ref[rows, :] = kv[:, MLA_HEADS * LANES:].astype(BF16)
        for g in range(SWA_GROUP):
            sl = slice(g * LANES, (g + 1) * LANES)
            qs_ref[rows, sl] = (_rope(p[:, C_QS + g * LANES:C_QS + (g + 1) * LANES], c64, s64,
                                      SWA_HEAD_DIM // 4) * s_scale).astype(BF16)
        ks_ref[rows, :] = _rope(p[:, C_KS:C_VS], c64, s64, SWA_HEAD_DIM // 4).astype(BF16)
        vs_ref[rows, :] = p[:, C_VS:C_END].astype(BF16)


def _proj0(h, g_mix, sh, sc, win, gq, gkv, wuq, wukv, cm, sm, c64, s64):
    row = lambda i: (i, 0)
    const = lambda i: (0, 0)
    mod = lambda i: (_mod_index(i), 0, 0)
    rope = lambda i: (_rope_index(i), 0)
    widths = (MLA_HEADS * LANES, MLA_HEADS * LANES, MLA_HEADS * MLA_V,
              SWA_HEADS * SWA_HEAD_DIM, SWA_KV_HEADS * SWA_HEAD_DIM, SWA_KV_HEADS * SWA_HEAD_DIM)
    return pl.pallas_call(
        _proj0_kernel,
        grid=(N_TOK // TM,),
        in_specs=[
            pl.BlockSpec((TM, D_MODEL), row),
            pl.BlockSpec((1, D_MODEL), const),
            pl.BlockSpec((1, 1, D_MODEL), mod),
            pl.BlockSpec((1, 1, D_MODEL), mod),
            pl.BlockSpec(win.shape, const),
            pl.BlockSpec(gq.shape, const),
            pl.BlockSpec(gkv.shape, const),
            pl.BlockSpec(wuq.shape, const),
            pl.BlockSpec(wukv.shape, const),
            pl.BlockSpec((TM, LANES), rope),
            pl.BlockSpec((TM, LANES), rope),
            pl.BlockSpec((TM, LANES), rope),
            pl.BlockSpec((TM, LANES), rope),
        ],
        out_specs=[pl.BlockSpec((TM, w), row) for w in widths],
        out_shape=[jax.ShapeDtypeStruct((N_TOK, w), BF16) for w in widths],
        compiler_params=_cparams("parallel"),
        name="proj0",
    )(h, g_mix, sh, sc, win, gq, gkv, wuq, wukv, cm, sm, c64, s64)


def _proj1_kernel(h_ref, g_ref, sh_ref, sc_ref, w_ref, c64_ref, s64_ref, q_ref, k_ref, v_ref):
    width = DIFF_HEADS * LANES
    scale = DIFF_HALF ** -0.5 * LOG2E
    for c in range(TM // TM_CHUNK):
        rows = slice(c * TM_CHUNK, (c + 1) * TM_CHUNK)
        a = _rms(h_ref[rows, :]) * g_ref[...]
        a = a * (1.0 + sc_ref[0]) + sh_ref[0]
        p = _dot(a.astype(BF16), w_ref[...])
        c64, s64 = c64_ref[rows, :], s64_ref[rows, :]
        for h in range(DIFF_HEADS):
            sl = slice(h * LANES, (h + 1) * LANES)
            q_ref[rows, sl] = (_rope(p[:, sl], c64, s64, DIFF_HALF // 4) * scale).astype(BF16)
            k_ref[rows, sl] = _rope(p[:, width + h * LANES:width + (h + 1) * LANES], c64, s64,
                                    DIFF_HALF // 4).astype(BF16)
        v_ref[rows, :] = p[:, 2 * width:].astype(BF16)


def _proj1(h, g_mix, sh, sc, w, c64, s64):
    row = lambda i: (i, 0)
    const = lambda i: (0, 0)
    mod = lambda i: (_mod_index(i), 0, 0)
    rope = lambda i: (_rope_index(i), 0)
    width = DIFF_HEADS * LANES
    return pl.pallas_call(
        _proj1_kernel,
        grid=(N_TOK // TM,),
        in_specs=[
            pl.BlockSpec((TM, D_MODEL), row),
            pl.BlockSpec((1, D_MODEL), const),
            pl.BlockSpec((1, 1, D_MODEL), mod),
            pl.BlockSpec((1, 1, D_MODEL), mod),
            pl.BlockSpec(w.shape, const),
            pl.BlockSpec((TM, LANES), rope),
            pl.BlockSpec((TM, LANES), rope),
        ],
        out_specs=[pl.BlockSpec((TM, width), row)] * 3,
        out_shape=[jax.ShapeDtypeStruct((N_TOK, width), BF16)] * 3,
        compiler_params=_cparams("parallel"),
        name="proj1",
    )(h, g_mix, sh, sc, w, c64, s64)


def _mla_kernel(*refs, n_kv):
    q_ref = refs[0]
    k_refs = refs[1:1 + n_kv]
    v_refs = refs[1 + n_kv:1 + 2 * n_kv]
    o_ref = refs[1 + 2 * n_kv]
    lane = lax.broadcasted_iota(jnp.int32, (q_ref.shape[0], LANES), 1)
    own = [lane < MLA_V, lane >= MLA_V]
    for pair in range(MLA_PAIRS_PER_STEP):
        vsl = slice(pair * LANES, (pair + 1) * LANES)
        outs = []
        for hh in range(2):
            sl = slice((2 * pair + hh) * LANES, (2 * pair + hh + 1) * LANES)
            qh = q_ref[:, sl]
            ss = [_dot_nt(qh, k[:, sl]) for k in k_refs]
            m = functools.reduce(jnp.maximum, [jnp.max(s, axis=-1, keepdims=True) for s in ss])
            o = None
            for s, v in zip(ss, v_refs):
                vp = v[:, vsl]
                vlane = lax.broadcasted_iota(jnp.int32, vp.shape, 1)
                keep = (vlane < MLA_V) if hh == 0 else (vlane >= MLA_V)
                vh = jnp.where(keep, vp, jnp.ones_like(vp))
                part = _dot(jnp.exp2(s - m).astype(BF16), vh)
                o = part if o is None else o + part
            outs.append(o / pltpu.roll(o, MLA_V, 1))
        o_ref[:, vsl] = jnp.where(own[0], outs[0], outs[1]).astype(BF16)


def _into_buffer(kernel_fn, buf):
    if buf is None:
        return kernel_fn, [], []

    def without_alias_ref(*refs):
        return kernel_fn(*refs[:-2], refs[-1])

    return without_alias_ref, [pl.BlockSpec(memory_space=pl.ANY)], [buf]


def _mla_attention(qm, km, vm, buf, *, ctx_queries):
    groups = MLA_HEADS // 2 // MLA_PAIRS_PER_STEP
    qk_w = 2 * LANES * MLA_PAIRS_PER_STEP
    v_w = LANES * MLA_PAIRS_PER_STEP
    if ctx_queries:
        tq, n_q = CTX_LEN, 1
        q_map = lambda b, h, i: (N_LAT // CTX_LEN + b, h)
        kv_specs = [(CTX_LEN, lambda b, h, i: (N_LAT // CTX_LEN + b, h))]
    else:
        tq, n_q = TQ_MLA, SEQ // TQ_MLA
        q_map = lambda b, h, i: (b * n_q + i, h)
        kv_specs = [(CTX_LEN, lambda b, h, i: (N_LAT // CTX_LEN + b, h)), (SEQ, lambda b, h, i: (b, h))]
    n_kv = len(kv_specs)
    in_specs = [pl.BlockSpec((tq, qk_w), q_map)]
    in_specs += [pl.BlockSpec((n, qk_w), m) for n, m in kv_specs]
    in_specs += [pl.BlockSpec((n, v_w), m) for n, m in kv_specs]
    kernel_fn, extra_specs, extra_args = _into_buffer(functools.partial(_mla_kernel, n_kv=n_kv), buf)
    n_in = len(in_specs)
    return pl.pallas_call(
        kernel_fn,
        grid=(BATCH, groups, n_q),
        in_specs=in_specs + extra_specs,
        out_specs=pl.BlockSpec((tq, v_w), q_map),
        out_shape=jax.ShapeDtypeStruct((N_TOK, D_MODEL), BF16),
        input_output_aliases={n_in: 0} if extra_args else {},
        compiler_params=_cparams("parallel", "parallel", "parallel"),
        name="mla_ctx" if ctx_queries else "mla_lat",
    )(qm, *([km] * n_kv), *([vm] * n_kv), *extra_args)


def _swa_kernel(sink_ref, *refs, band, n_blocks):
    q_ref = refs[0]
    rows = q_ref.shape[0]
    stacked = SWA_GROUP * rows
    if band:
        kp, kc, kn, kx, vp, vc, vn, vx, o_ref = refs[1:]
        n = pl.program_id(1)
        k_band = jnp.concatenate([kp[...], kc[...], kn[...]], axis=0)
        v_band = jnp.concatenate([vp[...], vc[...], vn[...]], axis=0)
        qq = lax.broadcasted_iota(jnp.int32, (stacked, 3 * SWA_BLOCK), 0) & (SWA_BLOCK - 1)
        kk = lax.broadcasted_iota(jnp.int32, (stacked, 3 * SWA_BLOCK), 1)
        rel = kk - SWA_BLOCK - qq
        key_pos = (n - 1) * SWA_BLOCK + kk
        valid = (jnp.abs(rel) <= WINDOW) & (key_pos >= 0) & (key_pos < n_blocks * SWA_BLOCK)
    else:
        kx, vx, o_ref = refs[1:]
    k_ctx = kx[...]
    v_ctx = vx[...]
    lane = lax.broadcasted_iota(jnp.int32, (rows, LANES), 1)
    low = lane < SWA_HEAD_DIM
    row_group = lax.broadcasted_iota(jnp.int32, (stacked, 1), 0) // rows
    halves = []
    for hk in range(SWA_KV_HEADS):
        keep = low if hk == 0 else jnp.logical_not(low)
        qh = jnp.concatenate(
            [jnp.where(keep, q_ref[:, g * LANES:(g + 1) * LANES], jnp.zeros((rows, LANES), BF16))
             for g in range(SWA_GROUP)], axis=0)
        sink = jnp.zeros((stacked, 1), F32)
        for g in range(SWA_GROUP):
            sink = jnp.where(row_group == g, sink_ref[hk * SWA_GROUP + g] * LOG2E, sink)
        s_ctx = _dot_nt(qh, k_ctx)
        m = jnp.maximum(jnp.max(s_ctx, axis=-1, keepdims=True), sink)
        if band:
            s_band = jnp.where(valid, _dot_nt(qh, k_band), NEG_INF)
            m = jnp.maximum(m, jnp.max(s_band, axis=-1, keepdims=True))
        def ones_other(v, hk=hk):
            vlane = lax.broadcasted_iota(jnp.int32, v.shape, 1)
            own = (vlane < SWA_HEAD_DIM) if hk == 0 else (vlane >= SWA_HEAD_DIM)
            return jnp.where(own, v, jnp.ones_like(v))

        o = _dot(jnp.exp2(s_ctx - m).astype(BF16), ones_other(v_ctx))
        if band:
            o = o + _dot(jnp.exp2(s_band - m).astype(BF16), ones_other(v_band))
        denom = pltpu.roll(o, SWA_HEAD_DIM, 1) + jnp.exp2(sink - m)
        halves.append(o / denom)
    for g in range(SWA_GROUP):
        rs = slice(g * rows, (g + 1) * rows)
        o_ref[:, g * LANES:(g + 1) * LANES] = jnp.where(low, halves[0][rs], halves[1][rs]).astype(BF16)


def _swa_attention(sink, qs, ks, vs, buf, *, ctx_queries):
    width = SWA_HEADS * SWA_HEAD_DIM
    kvw = SWA_KV_HEADS * SWA_HEAD_DIM
    ctx_map = lambda b, i: (N_LAT // CTX_LEN + b, 0)
    smem = pl.BlockSpec(memory_space=pltpu.SMEM)
    if ctx_queries:
        grid = (BATCH, 1)
        in_specs = [smem, pl.BlockSpec((CTX_LEN, width), ctx_map),
                    pl.BlockSpec((CTX_LEN, kvw), ctx_map), pl.BlockSpec((CTX_LEN, kvw), ctx_map)]
        args = (sink, qs, ks, vs)
        out_spec = pl.BlockSpec((CTX_LEN, width), lambda b, i: (N_LAT // CTX_LEN + b, 1))
        n_blocks = 1
    else:
        n_blocks = SEQ // SWA_BLOCK
        grid = (BATCH, n_blocks)
        cur = lambda b, i: (b * n_blocks + i, 0)
        prev = lambda b, i: (b * n_blocks + jnp.maximum(i - 1, 0), 0)
        nxt = lambda b, i: (b * n_blocks + jnp.minimum(i + 1, n_blocks - 1), 0)
        band_specs = [pl.BlockSpec((SWA_BLOCK, kvw), m) for m in (prev, cur, nxt)]
        in_specs = ([smem, pl.BlockSpec((SWA_BLOCK, width), cur)]
                    + band_specs + [pl.BlockSpec((CTX_LEN, kvw), ctx_map)]
                    + band_specs + [pl.BlockSpec((CTX_LEN, kvw), ctx_map)])
        args = (sink, qs, ks, ks, ks, ks, vs, vs, vs, vs)
        out_spec = pl.BlockSpec((SWA_BLOCK, width), lambda b, i: (b * n_blocks + i, 1))
    kernel_fn, extra_specs, extra_args = _into_buffer(
        functools.partial(_swa_kernel, band=not ctx_queries, n_blocks=n_blocks), buf)
    return pl.pallas_call(
        kernel_fn,
        grid=grid,
        in_specs=in_specs + extra_specs,
        out_specs=out_spec,
        out_shape=jax.ShapeDtypeStruct((N_TOK, D_MODEL), BF16),
        input_output_aliases={len(in_specs): 0} if extra_args else {},
        compiler_params=_cparams("parallel", "parallel"),
        name="swa_ctx" if ctx_queries else "swa_lat",
    )(*args, *extra_args)


def _diff_kernel(q_ref, kc_ref, kl_ref, vc_ref, vl_ref, lam_ref, g_ref, o_ref, *, lam_init):
    lam = (jnp.exp(jnp.sum(lam_ref[0:1, :] * lam_ref[1:2, :], axis=-1, keepdims=True))
           - jnp.exp(jnp.sum(lam_ref[2:3, :] * lam_ref[3:4, :], axis=-1, keepdims=True)) + lam_init)
    lane = lax.broadcasted_iota(jnp.int32, (q_ref.shape[0], LANES), 1)
    low = lane < DIFF_HALF
    k_refs = (kc_ref, kl_ref)
    v_refs = (vc_ref, vl_ref)
    for h in range(DIFF_HEADS_PER_STEP):
        sl = slice(h * LANES, (h + 1) * LANES)
        q = q_ref[:, sl]
        zero = jnp.zeros_like(q)

        def numerators(qh):
            ss = [_dot_nt(qh, k[:, sl]) for k in k_refs]
            m = functools.reduce(jnp.maximum, [jnp.max(s, axis=-1, keepdims=True) for s in ss])
            es = [jnp.exp2(s - m) for s in ss]
            return es, functools.reduce(jnp.add, [jnp.sum(e, axis=-1, keepdims=True) for e in es])

        e1, l1 = numerators(jnp.where(low, q, zero))
        e2, l2 = numerators(jnp.where(low, zero, q))
        c = lam * l1 / l2
        o = functools.reduce(jnp.add, [_dot((a - c * b).astype(BF16), v[:, sl])
                                       for a, b, v in zip(e1, e2, v_refs)])
        o = _rms(o / l1) * g_ref[...] * (1.0 - lam_init)
        o_ref[:, sl] = o.astype(BF16)


def _diff_attention(q, k, v, lam_vecs, g_sub, lam_init):
    n_q = SEQ // TQ_DIFF
    q_map = lambda b, h, i: (b * n_q + i, h)
    ctx_map = lambda b, h, i: (N_LAT // CTX_LEN + b, h)
    lat_map = lambda b, h, i: (b, h)
    const = lambda b, h, i: (0, 0)
    width = LANES * DIFF_HEADS_PER_STEP
    return pl.pallas_call(
        functools.partial(_diff_kernel, lam_init=lam_init),
        grid=(BATCH, DIFF_HEADS // DIFF_HEADS_PER_STEP, n_q),
        in_specs=[
            pl.BlockSpec((TQ_DIFF, width), q_map),
            pl.BlockSpec((CTX_LEN, width), ctx_map),
            pl.BlockSpec((SEQ, width), lat_map),
            pl.BlockSpec((CTX_LEN, width), ctx_map),
            pl.BlockSpec((SEQ, width), lat_map),
            pl.BlockSpec(lam_vecs.shape, const),
            pl.BlockSpec(g_sub.shape, const),
        ],
        out_specs=pl.BlockSpec((TQ_DIFF, width), q_map),
        out_shape=jax.ShapeDtypeStruct((N_LAT, DIFF_HEADS * DIFF_V), BF16),
        compiler_params=_cparams("parallel", "parallel", "parallel"),
        name="diff_attn",
    )(q, k, k, v, v, lam_vecs, g_sub)


def _pack_bf16_pairs(x):
    n = x.shape[1] // 2
    lo = pltpu.bitcast(x[:, :n].astype(BF16).astype(F32), jnp.uint32)
    hi = pltpu.bitcast(x[:, n:].astype(BF16).astype(F32), jnp.uint32)
    return pltpu.bitcast((lo >> 16) | hi, jnp.int32)


def _unpack_bf16_pairs(w):
    u = pltpu.bitcast(w, jnp.uint32)
    lo = pltpu.bitcast(u << 16, F32)
    hi = pltpu.bitcast(u & jnp.uint32(0xFFFF0000), F32)
    return lo, hi


def _route(logits_t, bias, cnt_ref, tri_ref, idx_ref, w_ref, rank_ref):
    tm = logits_t.shape[1]
    scores = jax.nn.sigmoid(logits_t)
    biased = scores + bias
    sub = lax.broadcasted_iota(jnp.int32, (GROUP_SIZE, tm), 0).astype(F32)
    grp_scores, grp_biased, grp_index = [], [], []
    group_score = []
    for g in range(N_GROUPS):
        sl = slice(g * GROUP_SIZE, (g + 1) * GROUP_SIZE)
        bg = biased[sl, :]
        grp_scores.append(scores[sl, :])
        grp_biased.append(bg)
        grp_index.append(sub + float(g * GROUP_SIZE))
        m1 = jnp.max(bg, axis=0, keepdims=True)
        first = jnp.min(jnp.where(bg == m1, sub, float(GROUP_SIZE)), axis=0, keepdims=True)
        m2 = jnp.max(jnp.where(sub == first, -jnp.inf, bg), axis=0, keepdims=True)
        group_score.append(m1 + m2)
    keep = [jnp.zeros((1, tm), F32) for _ in range(N_GROUPS)]
    for _ in range(TOPK_GROUPS):
        m = functools.reduce(jnp.maximum, group_score)
        found = jnp.zeros((1, tm), F32)
        for g in range(N_GROUPS):
            hit = jnp.where(group_score[g] == m, 1.0 - found, 0.0)
            keep[g] = keep[g] + hit
            found = found + hit
            group_score[g] = jnp.where(hit > 0.0, -jnp.inf, group_score[g])
    vals = [jnp.where(keep[g] > 0.0, grp_biased[g], NEG_INF) for g in range(N_GROUPS)]
    chosen = [jnp.zeros((GROUP_SIZE, tm), F32) for _ in range(N_GROUPS)]
    picked = []
    for _ in range(TOP_K):
        m = jnp.max(functools.reduce(jnp.maximum, vals), axis=0, keepdims=True)
        cand = [jnp.where(vals[g] == m, grp_index[g], float(N_EXPERTS)) for g in range(N_GROUPS)]
        ei = jnp.min(functools.reduce(jnp.minimum, cand), axis=0, keepdims=True)
        sel = [grp_index[g] == ei for g in range(N_GROUPS)]
        s_k = functools.reduce(jnp.add, [jnp.where(sel[g], grp_scores[g], 0.0) for g in range(N_GROUPS)])
        picked.append((ei, jnp.sum(s_k, axis=0, keepdims=True)))
        vals = [jnp.where(sel[g], -jnp.inf, vals[g]) for g in range(N_GROUPS)]
        chosen = [jnp.where(sel[g], 1.0, chosen[g]) for g in range(N_GROUPS)]
    total = functools.reduce(jnp.add, [s for _, s in picked])
    chosen_all = jnp.concatenate(chosen, axis=0)
    tri = tri_ref[...]
    width = tri.shape[0]
    cnt = cnt_ref[...]
    parts = []
    for j in range(tm // width):
        ch = chosen_all[:, j * width:(j + 1) * width]
        parts.append(_dot(ch.astype(BF16), tri) + cnt)
        cnt = cnt + jnp.sum(ch, axis=1, keepdims=True)
    cnt_ref[...] = cnt
    rank_all = jnp.concatenate(parts, axis=1)
    for k, (ei, s) in enumerate(picked):
        idx_ref[k:k + 1, :] = ei.astype(jnp.int32)
        w_ref[k:k + 1, :] = s / total * ROUTED_SCALE
        r = functools.reduce(jnp.add, [
            jnp.where(grp_index[g] == ei, rank_all[g * GROUP_SIZE:(g + 1) * GROUP_SIZE, :], 0.0)
            for g in range(N_GROUPS)])
        rank_ref[k:k + 1, :] = jnp.sum(r, axis=0, keepdims=True).astype(jnp.int32)


ROUTE_TN = 2048
TRI_N = 256


def _router_kernel(logits_ref, br_ref, tri_ref, idx_ref, w_ref, rank_ref, cnt_out_ref, cnt_ref):
    @pl.when(pl.program_id(0) == 0)
    def _():
        cnt_ref[...] = jnp.zeros_like(cnt_ref)

    _route(logits_ref[...], br_ref[...], cnt_ref, tri_ref, idx_ref, w_ref, rank_ref)
    cnt_out_ref[...] = jnp.broadcast_to(cnt_ref[...], cnt_out_ref.shape).astype(jnp.int32)


def _router(logits_t, br):
    n_tok = logits_t.shape[1]
    col = lambda i: (0, i)
    const = lambda i: (0, 0)
    tri = (jnp.arange(TRI_N)[:, None] < jnp.arange(TRI_N)[None, :]).astype(BF16)
    blk = pl.BlockSpec((TOP_K, ROUTE_TN), col)
    return pl.pallas_call(
        _router_kernel,
        grid=(n_tok // ROUTE_TN,),
        in_specs=[pl.BlockSpec((N_EXPERTS, ROUTE_TN), col), pl.BlockSpec(br.shape, const),
                  pl.BlockSpec(tri.shape, const)],
        out_specs=[blk, blk, blk, pl.BlockSpec((N_EXPERTS, LANES), const)],
        out_shape=[
            jax.ShapeDtypeStruct((TOP_K, n_tok), jnp.int32),
            jax.ShapeDtypeStruct((TOP_K, n_tok), F32),
            jax.ShapeDtypeStruct((TOP_K, n_tok), jnp.int32),
            jax.ShapeDtypeStruct((N_EXPERTS, LANES), jnp.int32),
        ],
        scratch_shapes=[pltpu.VMEM((N_EXPERTS, 1), F32)],
        compiler_params=_cparams("arbitrary"),
        name="router",
    )(logits_t, br, tri)


def _post_kernel(o_ref, h_ref, wout_ref, gtm_ref, g_ref, sh_ref, sc_ref, gtf_ref,
                 wrh_ref, wrl_ref, wsg_ref, wsu_ref, wsd_ref, hs_ref, f_ref, logits_ref):
    for c in range(TM // TM_CHUNK):
        rows = slice(c * TM_CHUNK, (c + 1) * TM_CHUNK)
        h1 = h_ref[rows, :] + gtm_ref[0] * _dot(o_ref[rows, :], wout_ref[...])
        f = _rms(h1) * g_ref[...]
        f = f * (1.0 + sc_ref[0]) + sh_ref[0]
        f_hi = f.astype(BF16)
        f_lo = (f - f_hi.astype(F32)).astype(BF16)
        f_ref[rows, :] = _pack_bf16_pairs(f)
        logits_ref[:, rows] = (_dot_nt(wrh_ref[...], f_hi) + _dot_nt(wrh_ref[...], f_lo)
                               + _dot_nt(wrl_ref[...], f_hi))
        mid = _silu(_dot(f_hi, wsg_ref[...])) * _dot(f_hi, wsu_ref[...])
        shared = _dot(mid.astype(BF16), wsd_ref[...])
        hs_ref[rows, :] = h1 + gtf_ref[0] * shared


def _post(o, h, wout, gtm, g_ffn, sh, sc, gtf, wrh, wrl, wsg, wsu, wsd, n_rows):
    row = lambda i: (i, 0)
    const = lambda i: (0, 0)
    mod = lambda i: (_mod_index(i), 0, 0)
    vec = pl.BlockSpec((1, 1, D_MODEL), mod)
    full = lambda a: pl.BlockSpec(a.shape, const)
    return pl.pallas_call(
        _post_kernel,
        grid=(n_rows // TM,),
        in_specs=[
            pl.BlockSpec((TM, D_MODEL), row),
            pl.BlockSpec((TM, D_MODEL), row),
            full(wout), vec, full(g_ffn), vec, vec, vec,
            full(wrh), full(wrl), full(wsg), full(wsu), full(wsd),
        ],
        out_specs=[
            pl.BlockSpec((TM, D_MODEL), row),
            pl.BlockSpec((TM, D_MODEL // 2), row),
            pl.BlockSpec((N_EXPERTS, TM), lambda i: (0, i)),
        ],
        out_shape=[
            jax.ShapeDtypeStruct((n_rows, D_MODEL), F32),
            jax.ShapeDtypeStruct((n_rows, D_MODEL // 2), jnp.int32),
            jax.ShapeDtypeStruct((N_EXPERTS, n_rows), F32),
        ],
        compiler_params=_cparams("parallel"),
        name="post_attn",
    )(o, h, wout, gtm, g_ffn, sh, sc, gtf, wrh, wrl, wsg, wsu, wsd)


DEST_TN = 2048


def _dest_kernel(ps_ref, idx_ref, rank_ref, o_ref):
    idx = idx_ref[...]
    acc = rank_ref[...]
    for e in range(N_EXPERTS):
        acc = acc + jnp.where(idx == e, ps_ref[e], 0)
    o_ref[...] = acc


def _dest(pad_start, top_idx, rank):
    n_tok = top_idx.shape[1]
    blk = pl.BlockSpec((TOP_K, DEST_TN), lambda i: (0, i))
    return pl.pallas_call(
        _dest_kernel,
        grid=(n_tok // DEST_TN,),
        in_specs=[pl.BlockSpec(memory_space=pltpu.SMEM), blk, blk],
        out_specs=blk,
        out_shape=jax.ShapeDtypeStruct((TOP_K, n_tok), jnp.int32),
        compiler_params=_cparams("parallel"),
        name="dest",
    )(pad_start, top_idx, rank)


SC_CORES = 2
SC_SUBCORES = 16
SC_WORKERS = SC_CORES * SC_SUBCORES
SC_DISPATCH_ROWS = 32
SC_GATHER_ROWS = 64
ROW_WORDS = D_MODEL // 2


def _sc_mesh():
    return plsc.VectorSubcoreMesh(core_axis_name="core", subcore_axis_name="subcore")


def _sc_worker():
    return lax.axis_index("subcore") * SC_CORES + lax.axis_index("core")


def _sc_dispatch(f_packed, dest, n_slots):
    n_tok = f_packed.shape[0]
    per_worker = n_tok // SC_WORKERS
    n_chunks = per_worker // SC_DISPATCH_ROWS
    assert n_chunks * SC_DISPATCH_ROWS * SC_WORKERS == n_tok and n_chunks % 2 == 0
    dest4 = dest.reshape(TOP_K, SC_WORKERS, n_chunks, SC_DISPATCH_ROWS)

    def body(f_hbm, dest_hbm, xs_hbm, rows0, rows1, idx_v, load0, load1, scat0, scat1):
        wid = _sc_worker()
        for k in range(TOP_K):
            pltpu.sync_copy(dest_hbm.at[k, wid], idx_v.at[k])

        def load(c, buf, sem):
            start = pl.multiple_of(wid * per_worker + c * SC_DISPATCH_ROWS, 8)
            return pltpu.make_async_copy(f_hbm.at[pl.ds(start, SC_DISPATCH_ROWS)], buf, sem)

        def scatters(c, buf, sem):
            return [pltpu.make_async_copy(buf, xs_hbm.at[idx_v.at[k, c]], sem) for k in range(TOP_K)]

        load(0, rows0, load0).start()

        @pl.loop(0, n_chunks, step=2)
        def _(c):
            @pl.when(c > 0)
            def _():
                for cp in scatters(c - 1, rows1, scat1):
                    cp.wait()

            load(c + 1, rows1, load1).start()
            load(c, rows0, load0).wait()
            for cp in scatters(c, rows0, scat0):
                cp.start()
            load(c + 1, rows1, load1).wait()
            for cp in scatters(c + 1, rows1, scat1):
                cp.start()
            for cp in scatters(c, rows0, scat0):
                cp.wait()

            @pl.when(c + 2 < n_chunks)
            def _():
                load(c + 2, rows0, load0).start()

        for cp in scatters(n_chunks - 1, rows1, scat1):
            cp.wait()

    rows = pltpu.VMEM((SC_DISPATCH_ROWS, ROW_WORDS), jnp.int32)
    return pl.kernel(
        body,
        out_type=jax.ShapeDtypeStruct((n_slots, ROW_WORDS), jnp.int32),
        mesh=_sc_mesh(),
        scratch_types=[rows, rows, pltpu.VMEM((TOP_K, n_chunks, SC_DISPATCH_ROWS), jnp.int32)]
        + [pltpu.SemaphoreType.DMA] * 4,
        name="sc_dispatch",
    )(f_packed, dest4)


def _sc_gather(ys, dest):
    n_idx = dest.shape[0] * dest.shape[1]
    per_worker = n_idx // SC_WORKERS
    n_chunks = per_worker // SC_GATHER_ROWS
    assert n_chunks * SC_GATHER_ROWS * SC_WORKERS == n_idx and n_chunks % 2 == 0
    dest3 = dest.reshape(SC_WORKERS, n_chunks, SC_GATHER_ROWS)

    def body(ys_hbm, dest_hbm, out_hbm, rows0, rows1, idx_v, gat0, gat1, put0, put1):
        wid = _sc_worker()
        pltpu.sync_copy(dest_hbm.at[wid], idx_v)

        def gather(c, buf, sem):
            return pltpu.make_async_copy(ys_hbm.at[idx_v.at[c]], buf, sem)

        def put(c, buf, sem):
            start = pl.multiple_of(wid * per_worker + c * SC_GATHER_ROWS, 8)
            return pltpu.make_async_copy(buf, out_hbm.at[pl.ds(start, SC_GATHER_ROWS)], sem)

        gather(0, rows0, gat0).start()

        @pl.loop(0, n_chunks, step=2)
        def _(c):
            @pl.when(c > 0)
            def _():
                put(c - 1, rows1, put1).wait()

            gather(c + 1, rows1, gat1).start()
            gather(c, rows0, gat0).wait()
            put(c, rows0, put0).start()
            gather(c + 1, rows1, gat1).wait()
            put(c + 1, rows1, put1).start()
            put(c, rows0, put0).wait()

            @pl.when(c + 2 < n_chunks)
            def _():
                gather(c + 2, rows0, gat0).start()

        put(n_chunks - 1, rows1, put1).wait()

    rows = pltpu.VMEM((SC_GATHER_ROWS, ROW_WORDS), jnp.int32)
    return pl.kernel(
        body,
        out_type=jax.ShapeDtypeStruct((n_idx, ROW_WORDS), jnp.int32),
        mesh=_sc_mesh(),
        scratch_types=[rows, rows, pltpu.VMEM((n_chunks, SC_GATHER_ROWS), jnp.int32)]
        + [pltpu.SemaphoreType.DMA] * 4,
        name="sc_gather",
    )(ys, dest3)


def _moe_kernel(be_ref, nused_ref, x_ref, wg_ref, wu_ref, wd_ref, o_ref, wg_s, wu_s, wd_s):
    i = pl.program_id(0)
    e = be_ref[i]
    changed = jnp.logical_or(i == 0, e != be_ref[jnp.maximum(i - 1, 0)])

    @pl.when(changed)
    def _():
        wg_s[...] = wg_ref[0, 0].astype(BF16)
        wu_s[...] = wu_ref[0, 0].astype(BF16)
        wd_s[...] = wd_ref[0, 0].astype(BF16)

    @pl.when(i < nused_ref[0])
    def _():
        half = D_MODEL // 2
        for c in range(MOE_ROWS // MOE_CHUNK):
            rows = slice(c * MOE_CHUNK, (c + 1) * MOE_CHUNK)
            lo, hi = _unpack_bf16_pairs(x_ref[rows, :])
            lo, hi = lo.astype(BF16), hi.astype(BF16)
            gate = _dot(lo, wg_s[:half, :]) + _dot(hi, wg_s[half:, :])
            up = _dot(lo, wu_s[:half, :]) + _dot(hi, wu_s[half:, :])
            mid = _silu(gate) * up
            o_ref[rows, :] = _pack_bf16_pairs(_dot(mid.astype(BF16), wd_s[...]))


def _moe(block_expert, n_used, xs, wg, wu, wd, layer):
    n_blocks = xs.shape[0] // MOE_ROWS
    expert = lambda i, be, nu: (layer, be[i], 0, 0)
    grid_spec = pltpu.PrefetchScalarGridSpec(
        num_scalar_prefetch=2,
        grid=(n_blocks,),
        in_specs=[
            pl.BlockSpec((MOE_ROWS, ROW_WORDS), lambda i, be, nu: (i, 0)),
            pl.BlockSpec((1, 1, D_MODEL, EXPERT_FF), expert),
            pl.BlockSpec((1, 1, D_MODEL, EXPERT_FF), expert),
            pl.BlockSpec((1, 1, EXPERT_FF, D_MODEL), expert),
        ],
        out_specs=pl.BlockSpec((MOE_ROWS, ROW_WORDS), lambda i, be, nu: (i, 0)),
        scratch_shapes=[
            pltpu.VMEM((D_MODEL, EXPERT_FF), BF16),
            pltpu.VMEM((D_MODEL, EXPERT_FF), BF16),
            pltpu.VMEM((EXPERT_FF, D_MODEL), BF16),
        ],
    )
    return pl.pallas_call(
        _moe_kernel,
        grid_spec=grid_spec,
        out_shape=jax.ShapeDtypeStruct((xs.shape[0], ROW_WORDS), jnp.int32),
        compiler_params=_cparams("arbitrary"),
        name="moe_experts",
    )(block_expert, n_used, xs, wg, wu, wd)


def _combine_kernel(y_ref, w_ref, hs_ref, gtf_ref, gfin_ref, o_ref, *, final):
    w = w_ref[...]
    acc_lo = acc_hi = None
    for k in range(TOP_K):
        lo, hi = _unpack_bf16_pairs(y_ref[k])
        wk = w[:, k:k + 1]
        acc_lo = lo * wk if k == 0 else acc_lo + lo * wk
        acc_hi = hi * wk if k == 0 else acc_hi + hi * wk
    out = hs_ref[...] + gtf_ref[0] * jnp.concatenate([acc_lo, acc_hi], axis=1)
    if final:
        out = _rms(out) * gfin_ref[...]
    o_ref[...] = out


def _combine(y, w, hs, gtf, g_final, *, final):
    n_rows = hs.shape[0]
    return pl.pallas_call(
        functools.partial(_combine_kernel, final=final),
        grid=(n_rows // TM_COMB,),
        in_specs=[
            pl.BlockSpec((TOP_K, TM_COMB, ROW_WORDS), lambda i: (0, i, 0)),
            pl.BlockSpec((TM_COMB, TOP_K), lambda i: (i, 0)),
            pl.BlockSpec((TM_COMB, D_MODEL), lambda i: (i, 0)),
            pl.BlockSpec((1, 1, D_MODEL), lambda i: (jnp.minimum(i // (SEQ // TM_COMB), BATCH), 0, 0)),
            pl.BlockSpec((1, D_MODEL), lambda i: (0, 0)),
        ],
        out_specs=pl.BlockSpec((TM_COMB, D_MODEL), lambda i: (i, 0)),
        out_shape=jax.ShapeDtypeStruct((n_rows, D_MODEL), F32),
        compiler_params=_cparams("parallel"),
        name="combine_final" if final else "combine",
    )(y, w, hs, gtf, g_final)


def _rope_tables():
    rows = SEQ // GRID_W

    def angles(rot_dim):
        half = rot_dim // 2
        inv_freq = ROPE_BASE ** (-jnp.arange(0, half, 2, dtype=F32) / half)
        row = jnp.repeat(jnp.arange(rows, dtype=F32), GRID_W)
        col = jnp.tile(jnp.arange(GRID_W, dtype=F32), rows)
        ang_r = row[:, None] * inv_freq
        ang_c = col[:, None] * inv_freq
        return jnp.concatenate([ang_r, ang_r, ang_c, ang_c], axis=-1)

    def signed(sin, quarter):
        sign = jnp.where((jnp.arange(sin.shape[-1]) // quarter) % 2 == 0, -1.0, 1.0)
        return sin * sign

    def with_identity(cos, sin):
        cos = jnp.concatenate([cos, jnp.ones((TM, LANES), F32)], axis=0)
        sin = jnp.concatenate([sin, jnp.zeros((TM, LANES), F32)], axis=0)
        return cos, sin

    a64 = angles(SWA_HEAD_DIM)
    c64 = jnp.tile(jnp.cos(a64), (1, LANES // SWA_HEAD_DIM))
    s64 = jnp.tile(signed(jnp.sin(a64), SWA_HEAD_DIM // 4), (1, LANES // SWA_HEAD_DIM))
    a32 = angles(MLA_ROPE)
    pad_lo = MLA_NOPE
    pad_hi = LANES - MLA_NOPE - MLA_ROPE
    cm = jnp.concatenate([jnp.ones((SEQ, pad_lo), F32), jnp.cos(a32), jnp.ones((SEQ, pad_hi), F32)], axis=-1)
    sm = jnp.concatenate([jnp.zeros((SEQ, pad_lo), F32), signed(jnp.sin(a32), MLA_ROPE // 4),
                          jnp.zeros((SEQ, pad_hi), F32)], axis=-1)
    return with_identity(cm, sm) + with_identity(c64, s64)


def _layer0_weights(wa_in, wa_uq, wa_ukv, wa_out):
    d = D_MODEL
    cq, ckv, kr, qs, ks, vs = jnp.split(
        wa_in, [C_CKV, C_KR, C_KR + MLA_ROPE, C_KR + MLA_ROPE + SWA_HEADS * SWA_HEAD_DIM,
                C_KR + MLA_ROPE + (SWA_HEADS + SWA_KV_HEADS) * SWA_HEAD_DIM], axis=-1)
    kr_pad = jnp.concatenate([jnp.zeros((d, MLA_NOPE), F32), kr,
                              jnp.zeros((d, LANES - MLA_NOPE - MLA_ROPE), F32)], axis=-1)
    qs_pair = qs.reshape(d, SWA_KV_HEADS, SWA_GROUP, SWA_HEAD_DIM).transpose(0, 2, 1, 3).reshape(d, -1)
    win = jnp.concatenate([cq, ckv, kr_pad, qs_pair, ks, vs], axis=-1).astype(BF16)
    uq = wa_uq.reshape(MLA_Q_RANK, MLA_HEADS, MLA_NOPE + MLA_ROPE)
    uq = jnp.pad(uq, ((0, 0), (0, 0), (0, LANES - MLA_NOPE - MLA_ROPE))).reshape(MLA_Q_RANK, -1)
    ukv = wa_ukv.reshape(MLA_KV_RANK, MLA_HEADS, MLA_NOPE + MLA_V)
    uk = jnp.pad(ukv[:, :, :MLA_NOPE], ((0, 0), (0, 0), (0, LANES - MLA_NOPE))).reshape(MLA_KV_RANK, -1)
    uv = ukv[:, :, MLA_NOPE:].reshape(MLA_KV_RANK, -1)
    wukv = jnp.concatenate([uk, uv], axis=-1)
    n_mla = MLA_HEADS * MLA_V
    out_swa = wa_out[n_mla:].reshape(SWA_KV_HEADS, SWA_GROUP, SWA_HEAD_DIM, d).transpose(1, 0, 2, 3)
    wout = jnp.concatenate([wa_out[:n_mla], out_swa.reshape(-1, d)], axis=0)
    return win, uq.astype(BF16), wukv.astype(BF16), wout.astype(BF16)


def _layer1_weights(wc_in):
    per_head = 4 * DIFF_HALF + DIFF_V
    w = wc_in.reshape(D_MODEL, DIFF_HEADS, per_head)
    q = w[:, :, :2 * DIFF_HALF].reshape(D_MODEL, -1)
    k = w[:, :, 2 * DIFF_HALF:4 * DIFF_HALF].reshape(D_MODEL, -1)
    v = w[:, :, 4 * DIFF_HALF:].reshape(D_MODEL, -1)
    return jnp.concatenate([q, k, v], axis=-1).astype(BF16)


def _block_tables(counts, n_blocks):
    padded = (counts + MOE_ROWS - 1) // MOE_ROWS * MOE_ROWS
    pad_end = jnp.cumsum(padded)
    pad_start = (pad_end - padded).astype(jnp.int32)
    block_start = jnp.arange(n_blocks, dtype=jnp.int32) * MOE_ROWS
    block_expert = jnp.minimum((pad_end[None, :] <= block_start[:, None]).sum(axis=1),
                               N_EXPERTS - 1).astype(jnp.int32)
    n_used = (pad_end[-1] // MOE_ROWS).astype(jnp.int32).reshape(1)
    return pad_start, block_expert, n_used


def _moe_layer(f_packed, top_idx, top_w, rank, counts, hs, gtf, g_final, wg, wu, wd, layer, *, final):
    n_tok = top_idx.shape[1]
    n_blocks = -(-(n_tok * TOP_K) // MOE_ROWS) + N_EXPERTS
    pad_start, block_expert, n_used = _block_tables(counts[:, 0], n_blocks)
    dest = _dest(pad_start, top_idx, rank)
    xs = _sc_dispatch(f_packed, dest, n_blocks * MOE_ROWS)
    ys = _moe(block_expert, n_used, xs, wg, wu, wd, layer)
    y = _sc_gather(ys, dest).reshape(TOP_K, n_tok, ROW_WORDS)
    return _combine(y, top_w.T, hs, gtf, g_final, final=final)


def kernel(x, c, ctx, c_ctx, w_ada, b_ada, g_mix, g_ffn, wa_in, ga_q, ga_kv, wa_uq, wa_ukv, wa_sink, wa_out,
           wc_in, lam_q1, lam_k1, lam_q2, lam_k2, gc_sub, wc_out, w_router, b_router, we_gate, we_up, we_down,
           ws_gate, ws_up, ws_down, g_final):
    d = D_MODEL
    h = jnp.concatenate([x.reshape(N_LAT, d), ctx.reshape(N_CTX, d)], axis=0)
    cc = jnp.concatenate([c, c_ctx[None, :], jnp.zeros((MOD_ROWS - BATCH - 1, d), F32)], axis=0)
    mod = _ada(cc, w_ada, b_ada).reshape(DEPTH, MOD_ROWS, 6, 1, d)

    def mod_vec(layer, j):
        return mod[layer, :, j]

    cm, sm, c64, s64 = _rope_tables()
    g_final2 = g_final.reshape(1, d)

    def router_weights(layer):
        wt = w_router[layer].T
        hi = wt.astype(BF16)
        lo = (wt - hi.astype(F32)).astype(BF16)
        return hi, lo

    def shared_weights(layer):
        return ws_gate[layer].astype(BF16), ws_up[layer].astype(BF16), ws_down[layer].astype(BF16)

    win, wuq, wukv, wout0 = _layer0_weights(wa_in[0], wa_uq[0], wa_ukv[0], wa_out[0])
    qm, km, vm, qs, ks, vs = _proj0(h, g_mix[0].reshape(1, d), mod_vec(0, 0), mod_vec(0, 1), win,
                                    ga_q[0].reshape(1, -1), ga_kv[0].reshape(1, -1), wuq, wukv, cm, sm, c64, s64)
    sink = wa_sink[0].astype(F32)
    o = _mla_attention(qm, km, vm, None, ctx_queries=False)
    o = _mla_attention(qm, km, vm, o, ctx_queries=True)
    o = _swa_attention(sink, qs, ks, vs, o, ctx_queries=False)
    o = _swa_attention(sink, qs, ks, vs, o, ctx_queries=True)
    hs, f, logits_t = _post(
        o, h, wout0, mod_vec(0, 2), g_ffn[0].reshape(1, d), mod_vec(0, 3), mod_vec(0, 4), mod_vec(0, 5),
        *router_weights(0), *shared_weights(0), N_TOK)
    top_idx, top_w, rank, counts = _router(logits_t, b_router[0].reshape(N_EXPERTS, 1))
    h = _moe_layer(f, top_idx, top_w, rank, counts, hs, mod_vec(0, 5), g_final2,
                   we_gate, we_up, we_down, 0, final=False)

    layer = 1
    lam_init = 0.8 - 0.6 * math.exp(-0.3 * layer)
    q1, k1, v1 = _proj1(h, g_mix[1].reshape(1, d), mod_vec(1, 0), mod_vec(1, 1), _layer1_weights(wc_in[0]),
                        c64, s64)
    lam_vecs = jnp.zeros((8, LANES), F32).at[:4, :DIFF_HALF].set(
        jnp.stack([lam_q1[0], lam_k1[0], lam_q2[0], lam_k2[0]]).astype(F32))
    o = _diff_attention(q1, k1, v1, lam_vecs, gc_sub[0].reshape(1, DIFF_V), lam_init)
    hs, f, logits_t = _post(
        o, h, wc_out[0].astype(BF16), mod_vec(1, 2), g_ffn[1].reshape(1, d), mod_vec(1, 3), mod_vec(1, 4),
        mod_vec(1, 5), *router_weights(1), *shared_weights(1), N_LAT)
    top_idx, top_w, rank, counts = _router(logits_t, b_router[1].reshape(N_EXPERTS, 1))
    out = _moe_layer(f, top_idx, top_w, rank, counts, hs, mod_vec(1, 5), g_final2,
                     we_gate, we_up, we_down, 1, final=True)
    return out.reshape(BATCH, SEQ, d)
```

```python
import functools
import math

import jax
import jax.numpy as jnp
from jax import lax
from jax.experimental import pallas as pl
from jax.experimental.pallas import tpu as pltpu
from jax.experimental.pallas import tpu_sc as plsc

F32 = jnp.float32
BF16 = jnp.bfloat16

D_MODEL = 1024
BATCH = 8
SEQ = 2048
DEPTH = 2
CTX_LEN = 256
GRID_W = 64
ROPE_BASE = 10000.0
EPS = 1e-6
NEG_INF = -1e30

MLA_HEADS = 8
MLA_Q_RANK = 384
MLA_KV_RANK = 256
MLA_NOPE = 64
MLA_ROPE = 32
MLA_V = 64
SWA_HEADS = 8
SWA_KV_HEADS = 2
SWA_HEAD_DIM = 64
SWA_GROUP = SWA_HEADS // SWA_KV_HEADS
WINDOW = 128
DIFF_HEADS = 8
DIFF_HALF = 64
DIFF_V = 128
N_EXPERTS = 64
N_GROUPS = 8
GROUP_SIZE = N_EXPERTS // N_GROUPS
TOPK_GROUPS = 4
TOP_K = 8
EXPERT_FF = 256
SHARED_FF = 256
ROUTED_SCALE = 2.5

LOG2E = math.log2(math.e)
LANES = 128
N_LAT = BATCH * SEQ
N_CTX = BATCH * CTX_LEN
N_TOK = N_LAT + N_CTX
MOD_ROWS = 16

TM = 512
TM_CHUNK = 256
LAT_BLOCKS_PER_BATCH = SEQ // TM
TQ_MLA = 512
MLA_PAIRS_PER_STEP = 4
TQ_DIFF = 512
DIFF_HEADS_PER_STEP = 4
SWA_BLOCK = 128
MOE_ROWS = 512
MOE_CHUNK = 256
TM_COMB = 256
VMEM_LIMIT = 56 * 1024 * 1024

C_CQ = 0
C_CKV = C_CQ + MLA_Q_RANK
C_KR = C_CKV + MLA_KV_RANK
C_QS = C_KR + LANES
C_KS = C_QS + SWA_HEADS * SWA_HEAD_DIM
C_VS = C_KS + SWA_KV_HEADS * SWA_HEAD_DIM
C_END = C_VS + SWA_KV_HEADS * SWA_HEAD_DIM


def _cparams(*sem):
    return pltpu.CompilerParams(dimension_semantics=sem, vmem_limit_bytes=VMEM_LIMIT)


def _dot(a, b):
    return jnp.dot(a, b, preferred_element_type=F32)


def _dot_nt(a, b):
    return lax.dot_general(a, b, (((1,), (1,)), ((), ())), preferred_element_type=F32)


def _rms(x):
    return x * lax.rsqrt(jnp.mean(x * x, axis=-1, keepdims=True) + EPS)


def _silu(x):
    return x * jax.nn.sigmoid(x)


def _rope(x, cos, sin_signed, shift):
    n = x.shape[-1]
    lane = lax.broadcasted_iota(jnp.int32, x.shape, 1)
    first = (lane & shift) == 0
    rot = jnp.where(first, pltpu.roll(x, n - shift, 1), pltpu.roll(x, shift, 1))
    return x * cos + rot * sin_signed


def _mod_index(i):
    return jnp.minimum(i // LAT_BLOCKS_PER_BATCH, BATCH)


def _rope_index(i):
    return jnp.where(i < N_LAT // TM, i % LAT_BLOCKS_PER_BATCH, LAT_BLOCKS_PER_BATCH)


ADA_TN = 1536


def _ada_kernel(c_ref, w_ref, b_ref, o_ref):
    s = _silu(c_ref[...]).astype(BF16)
    o_ref[0] = _dot(s, w_ref[0].astype(BF16)) + b_ref[0]


def _ada(cc, w_ada, b_ada):
    n_out = w_ada.shape[-1]
    return pl.pallas_call(
        _ada_kernel,
        grid=(DEPTH, n_out // ADA_TN),
        in_specs=[
            pl.BlockSpec((MOD_ROWS, D_MODEL), lambda l, j: (0, 0)),
            pl.BlockSpec((1, D_MODEL, ADA_TN), lambda l, j: (l, 0, j)),
            pl.BlockSpec((1, 1, ADA_TN), lambda l, j: (l, 0, j)),
        ],
        out_specs=pl.BlockSpec((1, MOD_ROWS, ADA_TN), lambda l, j: (l, 0, j)),
        out_shape=jax.ShapeDtypeStruct((DEPTH, MOD_ROWS, n_out), F32),
        compiler_params=_cparams("parallel", "parallel"),
        name="ada",
    )(cc, w_ada, b_ada.reshape(DEPTH, 1, n_out))


def _token_rows(lat_ref, ctx_ref, rows):
    return jnp.where(pl.program_id(0) < N_LAT // TM, lat_ref[rows, :], ctx_ref[rows, :])


def _token_specs():
    lat_blocks = N_LAT // TM
    return [pl.BlockSpec((TM, D_MODEL), lambda i: (jnp.minimum(i, lat_blocks - 1), 0)),
            pl.BlockSpec((TM, D_MODEL), lambda i: (jnp.maximum(i - lat_blocks, 0), 0))]


def _proj0_kernel(x_ref, ctx_ref, g_ref, sh_ref, sc_ref, win_ref, gq_ref, gkv_ref, wuq_ref, wukv_ref,
                  cm_ref, sm_ref, c64_ref, s64_ref,
                  qm_ref, km_ref, vm_ref, qs_ref, ks_ref, vs_ref):
    q_scale = (MLA_NOPE + MLA_ROPE) ** -0.5 * LOG2E
    s_scale = SWA_HEAD_DIM ** -0.5 * LOG2E
    for c in range(TM // TM_CHUNK):
        rows = slice(c * TM_CHUNK, (c + 1) * TM_CHUNK)
        a = _rms(_token_rows(x_ref, ctx_ref, rows)) * g_ref[...]
        a = a * (1.0 + sc_ref[0]) + sh_ref[0]
        p = _dot(a.astype(BF16), win_ref[...])
        nq = _rms(p[:, C_CQ:C_CKV]) * gq_ref[...]
        nkv = _rms(p[:, C_CKV:C_KR]) * gkv_ref[...]
        q = _dot(nq.astype(BF16), wuq_ref[...])
        kv = _dot(nkv.astype(BF16), wukv_ref[...])
        cm, sm = cm_ref[rows, :], sm_ref[rows, :]
        c64, s64 = c64_ref[rows, :], s64_ref[rows, :]
        kr = _rope(p[:, C_KR:C_QS], cm, sm, MLA_ROPE // 4)
        for h in range(MLA_HEADS):
            sl = slice(h * LANES, (h + 1) * LANES)
            qm_ref[rows, sl] = (_rope(q[:, sl], cm, sm, MLA_ROPE // 4) * q_scale).astype(BF16)
            km_ref[rows, sl] = (kv[:, sl] + kr).astype(BF16)
        vm_ref[rows, :] = kv[:, MLA_HEADS * LANES:].astype(BF16)
        for g in range(SWA_GROUP):
            sl = slice(g * LANES, (g + 1) * LANES)
            qs_ref[rows, sl] = (_rope(p[:, C_QS + g * LANES:C_QS + (g + 1) * LANES], c64, s64,
                                      SWA_HEAD_DIM // 4) * s_scale).astype(BF16)
        ks_ref[rows, :] = _rope(p[:, C_KS:C_VS], c64, s64, SWA_HEAD_DIM // 4).astype(BF16)
        vs_ref[rows, :] = p[:, C_VS:C_END].astype(BF16)


def _proj0(x, ctx, g_mix, sh, sc, win, gq, gkv, wuq, wukv, cm, sm, c64, s64):
    row = lambda i: (i, 0)
    const = lambda i: (0, 0)
    mod = lambda i: (_mod_index(i), 0, 0)
    rope = lambda i: (_rope_index(i), 0)
    widths = (MLA_HEADS * LANES, MLA_HEADS * LANES, MLA_HEADS * MLA_V,
              SWA_HEADS * SWA_HEAD_DIM, SWA_KV_HEADS * SWA_HEAD_DIM, SWA_KV_HEADS * SWA_HEAD_DIM)
    return pl.pallas_call(
        _proj0_kernel,
        grid=(N_TOK // TM,),
        in_specs=_token_specs() + [
            pl.BlockSpec((1, D_MODEL), const),
            pl.BlockSpec((1, 1, D_MODEL), mod),
            pl.BlockSpec((1, 1, D_MODEL), mod),
            pl.BlockSpec(win.shape, const),
            pl.BlockSpec(gq.shape, const),
            pl.BlockSpec(gkv.shape, const),
            pl.BlockSpec(wuq.shape, const),
            pl.BlockSpec(wukv.shape, const),
            pl.BlockSpec((TM, LANES), rope),
            pl.BlockSpec((TM, LANES), rope),
            pl.BlockSpec((TM, LANES), rope),
            pl.BlockSpec((TM, LANES), rope),
        ],
        out_specs=[pl.BlockSpec((TM, w), row) for w in widths],
        out_shape=[jax.ShapeDtypeStruct((N_TOK, w), BF16) for w in widths],
        compiler_params=_cparams("parallel"),
        name="proj0",
    )(x, ctx, g_mix, sh, sc, win, gq, gkv, wuq, wukv, cm, sm, c64, s64)


def _proj1_kernel(h_ref, g_ref, sh_ref, sc_ref, w_ref, c64_ref, s64_ref, q_ref, k_ref, v_ref):
    width = DIFF_HEADS * LANES
    scale = DIFF_HALF ** -0.5 * LOG2E
    for c in range(TM // TM_CHUNK):
        rows = slice(c * TM_CHUNK, (c + 1) * TM_CHUNK)
        a = _rms(h_ref[rows, :]) * g_ref[...]
        a = a * (1.0 + sc_ref[0]) + sh_ref[0]
        p = _dot(a.astype(BF16), w_ref[...])
        c64, s64 = c64_ref[rows, :], s64_ref[rows, :]
        for h in range(DIFF_HEADS):
            sl = slice(h * LANES, (h + 1) * LANES)
            q_ref[rows, sl] = (_rope(p[:, sl], c64, s64, DIFF_HALF // 4) * scale).astype(BF16)
            k_ref[rows, sl] = _rope(p[:, width + h * LANES:width + (h + 1) * LANES], c64, s64,
                                    DIFF_HALF // 4).astype(BF16)
        v_ref[rows, :] = p[:, 2 * width:].astype(BF16)


def _proj1(h, g_mix, sh, sc, w, c64, s64):
    row = lambda i: (i, 0)
    const = lambda i: (0, 0)
    mod = lambda i: (_mod_index(i), 0, 0)
    rope = lambda i: (_rope_index(i), 0)
    width = DIFF_HEADS * LANES
    return pl.pallas_call(
        _proj1_kernel,
        grid=(N_TOK // TM,),
        in_specs=[
            pl.BlockSpec((TM, D_MODEL), row),
            pl.BlockSpec((1, D_MODEL), const),
            pl.BlockSpec((1, 1, D_MODEL), mod),
            pl.BlockSpec((1, 1, D_MODEL), mod),
            pl.BlockSpec(w.shape, const),
            pl.BlockSpec((TM, LANES), rope),
            pl.BlockSpec((TM, LANES), rope),
        ],
        out_specs=[pl.BlockSpec((TM, width), row)] * 3,
        out_shape=[jax.ShapeDtypeStruct((N_TOK, width), BF16)] * 3,
        compiler_params=_cparams("parallel"),
        name="proj1",
    )(h, g_mix, sh, sc, w, c64, s64)


def _mla_kernel(*refs, n_kv):
    q_ref = refs[0]
    k_refs = refs[1:1 + n_kv]
    v_refs = refs[1 + n_kv:1 + 2 * n_kv]
    o_ref = refs[1 + 2 * n_kv]
    lane = lax.broadcasted_iota(jnp.int32, (q_ref.shape[0], LANES), 1)
    own = [lane < MLA_V, lane >= MLA_V]
    for pair in range(MLA_PAIRS_PER_STEP):
        vsl = slice(pair * LANES, (pair + 1) * LANES)
        outs = []
        for hh in range(2):
            sl = slice((2 * pair + hh) * LANES, (2 * pair + hh + 1) * LANES)
            qh = q_ref[:, sl]
            ss = [_dot_nt(qh, k[:, sl]) for k in k_refs]
            m = functools.reduce(jnp.maximum, [jnp.max(s, axis=-1, keepdims=True) for s in ss])
            o = None
            for s, v in zip(ss, v_refs):
                vp = v[:, vsl]
                vlane = lax.broadcasted_iota(jnp.int32, vp.shape, 1)
                keep = (vlane < MLA_V) if hh == 0 else (vlane >= MLA_V)
                vh = jnp.where(keep, vp, jnp.ones_like(vp))
                part = _dot(jnp.exp2(s - m).astype(BF16), vh)
                o = part if o is None else o + part
            outs.append(o / pltpu.roll(o, MLA_V, 1))
        o_ref[:, vsl] = jnp.where(own[0], outs[0], outs[1]).astype(BF16)


def _into_buffer(kernel_fn, buf):
    if buf is None:
        return kernel_fn, [], []

    def without_alias_ref(*refs):
        return kernel_fn(*refs[:-2], refs[-1])

    return without_alias_ref, [pl.BlockSpec(memory_space=pl.ANY)], [buf]


def _mla_attention(qm, km, vm, buf, *, ctx_queries):
    groups = MLA_HEADS // 2 // MLA_PAIRS_PER_STEP
    qk_w = 2 * LANES * MLA_PAIRS_PER_STEP
    v_w = LANES * MLA_PAIRS_PER_STEP
    if ctx_queries:
        tq, n_q = CTX_LEN, 1
        q_map = lambda b, h, i: (N_LAT // CTX_LEN + b, h)
        kv_specs = [(CTX_LEN, lambda b, h, i: (N_LAT // CTX_LEN + b, h))]
    else:
        tq, n_q = TQ_MLA, SEQ // TQ_MLA
        q_map = lambda b, h, i: (b * n_q + i, h)
        kv_specs = [(CTX_LEN, lambda b, h, i: (N_LAT // CTX_LEN + b, h)), (SEQ, lambda b, h, i: (b, h))]
    n_kv = len(kv_specs)
    in_specs = [pl.BlockSpec((tq, qk_w), q_map)]
    in_specs += [pl.BlockSpec((n, qk_w), m) for n, m in kv_specs]
    in_specs += [pl.BlockSpec((n, v_w), m) for n, m in kv_specs]
    kernel_fn, extra_specs, extra_args = _into_buffer(functools.partial(_mla_kernel, n_kv=n_kv), buf)
    n_in = len(in_specs)
    return pl.pallas_call(
        kernel_fn,
        grid=(BATCH, groups, n_q),
        in_specs=in_specs + extra_specs,
        out_specs=pl.BlockSpec((tq, v_w), q_map),
        out_shape=jax.ShapeDtypeStruct((N_TOK, D_MODEL), BF16),
        input_output_aliases={n_in: 0} if extra_args else {},
        compiler_params=_cparams("parallel", "parallel", "parallel"),
        name="mla_ctx" if ctx_queries else "mla_lat",
    )(qm, *([km] * n_kv), *([vm] * n_kv), *extra_args)


def _swa_kernel(sink_ref, *refs, band, n_blocks):
    q_ref = refs[0]
    rows = q_ref.shape[0]
    stacked = SWA_GROUP * rows
    if band:
        kp, kc, kn, kx, vp, vc, vn, vx, o_ref = refs[1:]
        n = pl.program_id(1)
        k_band = jnp.concatenate([kp[...], kc[...], kn[...]], axis=0)
        v_band = jnp.concatenate([vp[...], vc[...], vn[...]], axis=0)
        qq = lax.broadcasted_iota(jnp.int32, (stacked, 3 * SWA_BLOCK), 0) & (SWA_BLOCK - 1)
        kk = lax.broadcasted_iota(jnp.int32, (stacked, 3 * SWA_BLOCK), 1)
        rel = kk - SWA_BLOCK - qq
        key_pos = (n - 1) * SWA_BLOCK + kk
        valid = (jnp.abs(rel) <= WINDOW) & (key_pos >= 0) & (key_pos < n_blocks * SWA_BLOCK)
    else:
        kx, vx, o_ref = refs[1:]
    k_ctx = kx[...]
    v_ctx = vx[...]
    lane = lax.broadcasted_iota(jnp.int32, (rows, LANES), 1)
    low = lane < SWA_HEAD_DIM
    row_group = lax.broadcasted_iota(jnp.int32, (stacked, 1), 0) // rows
    halves = []
    for hk in range(SWA_KV_HEADS):
        keep = low if hk == 0 else jnp.logical_not(low)
        qh = jnp.concatenate(
            [jnp.where(keep, q_ref[:, g * LANES:(g + 1) * LANES], jnp.zeros((rows, LANES), BF16))
             for g in range(SWA_GROUP)], axis=0)
        sink = jnp.zeros((stacked, 1), F32)
        for g in range(SWA_GROUP):
            sink = jnp.where(row_group == g, sink_ref[hk * SWA_GROUP + g] * LOG2E, sink)
        s_ctx = _dot_nt(qh, k_ctx)
        m = jnp.maximum(jnp.max(s_ctx, axis=-1, keepdims=True), sink)
        if band:
            s_band = jnp.where(valid, _dot_nt(qh, k_band), NEG_INF)
            m = jnp.maximum(m, jnp.max(s_band, axis=-1, keepdims=True))
        def ones_other(v, hk=hk):
            vlane = lax.broadcasted_iota(jnp.int32, v.shape, 1)
            own = (vlane < SWA_HEAD_DIM) if hk == 0 else (vlane >= SWA_HEAD_DIM)
            return jnp.where(own, v, jnp.ones_like(v))

        o = _dot(jnp.exp2(s_ctx - m).astype(BF16), ones_other(v_ctx))
        if band:
            o = o + _dot(jnp.exp2(s_band - m).astype(BF16), ones_other(v_band))
        denom = pltpu.roll(o, SWA_HEAD_DIM, 1) + jnp.exp2(sink - m)
        halves.append(o / denom)
    for g in range(SWA_GROUP):
        rs = slice(g * rows, (g + 1) * rows)
        o_ref[:, g * LANES:(g + 1) * LANES] = jnp.where(low, halves[0][rs], halves[1][rs]).astype(BF16)


def _swa_attention(sink, qs, ks, vs, buf, *, ctx_queries):
    width = SWA_HEADS * SWA_HEAD_DIM
    kvw = SWA_KV_HEADS * SWA_HEAD_DIM
    ctx_map = lambda b, i: (N_LAT // CTX_LEN + b, 0)
    smem = pl.BlockSpec(memory_space=pltpu.SMEM)
    if ctx_queries:
        grid = (BATCH, 1)
        in_specs = [smem, pl.BlockSpec((CTX_LEN, width), ctx_map),
                    pl.BlockSpec((CTX_LEN, kvw), ctx_map), pl.BlockSpec((CTX_LEN, kvw), ctx_map)]
        args = (sink, qs, ks, vs)
        out_spec = pl.BlockSpec((CTX_LEN, width), lambda b, i: (N_LAT // CTX_LEN + b, 1))
        n_blocks = 1
    else:
        n_blocks = SEQ // SWA_BLOCK
        grid = (BATCH, n_blocks)
        cur = lambda b, i: (b * n_blocks + i, 0)
        prev = lambda b, i: (b * n_blocks + jnp.maximum(i - 1, 0), 0)
        nxt = lambda b, i: (b * n_blocks + jnp.minimum(i + 1, n_blocks - 1), 0)
        band_specs = [pl.BlockSpec((SWA_BLOCK, kvw), m) for m in (prev, cur, nxt)]
        in_specs = ([smem, pl.BlockSpec((SWA_BLOCK, width), cur)]
                    + band_specs + [pl.BlockSpec((CTX_LEN, kvw), ctx_map)]
                    + band_specs + [pl.BlockSpec((CTX_LEN, kvw), ctx_map)])
        args = (sink, qs, ks, ks, ks, ks, vs, vs, vs, vs)
        out_spec = pl.BlockSpec((SWA_BLOCK, width), lambda b, i: (b * n_blocks + i, 1))
    kernel_fn, extra_specs, extra_args = _into_buffer(
        functools.partial(_swa_kernel, band=not ctx_queries, n_blocks=n_blocks), buf)
    return pl.pallas_call(
        kernel_fn,
        grid=grid,
        in_specs=in_specs + extra_specs,
        out_specs=out_spec,
        out_shape=jax.ShapeDtypeStruct((N_TOK, D_MODEL), BF16),
        input_output_aliases={len(in_specs): 0} if extra_args else {},
        compiler_params=_cparams("parallel", "parallel"),
        name="swa_ctx" if ctx_queries else "swa_lat",
    )(*args, *extra_args)


def _diff_kernel(q_ref, kc_ref, kl_ref, vc_ref, vl_ref, lam_ref, g_ref, o_ref, *, lam_init):
    lam = (jnp.exp(jnp.sum(lam_ref[0:1, :] * lam_ref[1:2, :], axis=-1, keepdims=True))
           - jnp.exp(jnp.sum(lam_ref[2:3, :] * lam_ref[3:4, :], axis=-1, keepdims=True)) + lam_init)
    lane = lax.broadcasted_iota(jnp.int32, (q_ref.shape[0], LANES), 1)
    low = lane < DIFF_HALF
    k_refs = (kc_ref, kl_ref)
    v_refs = (vc_ref, vl_ref)
    for h in range(DIFF_HEADS_PER_STEP):
        sl = slice(h * LANES, (h + 1) * LANES)
        q = q_ref[:, sl]
        zero = jnp.zeros_like(q)

        def numerators(qh):
            ss = [_dot_nt(qh, k[:, sl]) for k in k_refs]
            m = functools.reduce(jnp.maximum, [jnp.max(s, axis=-1, keepdims=True) for s in ss])
            es = [jnp.exp2(s - m) for s in ss]
            return es, functools.reduce(jnp.add, [jnp.sum(e, axis=-1, keepdims=True) for e in es])

        e1, l1 = numerators(jnp.where(low, q, zero))
        e2, l2 = numerators(jnp.where(low, zero, q))
        c = lam * l1 / l2
        o = functools.reduce(jnp.add, [_dot((a - c * b).astype(BF16), v[:, sl])
                                       for a, b, v in zip(e1, e2, v_refs)])
        o = _rms(o / l1) * g_ref[...] * (1.0 - lam_init)
        o_ref[:, sl] = o.astype(BF16)


def _diff_attention(q, k, v, lam_vecs, g_sub, lam_init):
    n_q = SEQ // TQ_DIFF
    q_map = lambda b, h, i: (b * n_q + i, h)
    ctx_map = lambda b, h, i: (N_LAT // CTX_LEN + b, h)
    lat_map = lambda b, h, i: (b, h)
    const = lambda b, h, i: (0, 0)
    width = LANES * DIFF_HEADS_PER_STEP
    return pl.pallas_call(
        functools.partial(_diff_kernel, lam_init=lam_init),
        grid=(BATCH, DIFF_HEADS // DIFF_HEADS_PER_STEP, n_q),
        in_specs=[
            pl.BlockSpec((TQ_DIFF, width), q_map),
            pl.BlockSpec((CTX_LEN, width), ctx_map),
            pl.BlockSpec((SEQ, width), lat_map),
            pl.BlockSpec((CTX_LEN, width), ctx_map),
            pl.BlockSpec((SEQ, width), lat_map),
            pl.BlockSpec(lam_vecs.shape, const),
            pl.BlockSpec(g_sub.shape, const),
        ],
        out_specs=pl.BlockSpec((TQ_DIFF, width), q_map),
        out_shape=jax.ShapeDtypeStruct((N_LAT, DIFF_HEADS * DIFF_V), BF16),
        compiler_params=_cparams("parallel", "parallel", "parallel"),
        name="diff_attn",
    )(q, k, k, v, v, lam_vecs, g_sub)


def _pack_bf16_pairs(x):
    n = x.shape[1] // 2
    lo = pltpu.bitcast(x[:, :n].astype(BF16).astype(F32), jnp.uint32)
    hi = pltpu.bitcast(x[:, n:].astype(BF16).astype(F32), jnp.uint32)
    return pltpu.bitcast((lo >> 16) | hi, jnp.int32)


def _unpack_bf16_pairs(w):
    u = pltpu.bitcast(w, jnp.uint32)
    lo = pltpu.bitcast(u << 16, F32)
    hi = pltpu.bitcast(u & jnp.uint32(0xFFFF0000), F32)
    return lo, hi


def _route(logits_t, bias, cnt_ref, tri_ref, idx_ref, w_ref, rank_ref):
    tm = logits_t.shape[1]
    scores = jax.nn.sigmoid(logits_t)
    biased = scores + bias
    sub = lax.broadcasted_iota(jnp.int32, (GROUP_SIZE, tm), 0).astype(F32)
    grp_scores, grp_biased, grp_index = [], [], []
    group_score = []
    for g in range(N_GROUPS):
        sl = slice(g * GROUP_SIZE, (g + 1) * GROUP_SIZE)
        bg = biased[sl, :]
        grp_scores.append(scores[sl, :])
        grp_biased.append(bg)
        grp_index.append(sub + float(g * GROUP_SIZE))
        m1 = jnp.max(bg, axis=0, keepdims=True)
        first = jnp.min(jnp.where(bg == m1, sub, float(GROUP_SIZE)), axis=0, keepdims=True)
        m2 = jnp.max(jnp.where(sub == first, -jnp.inf, bg), axis=0, keepdims=True)
        group_score.append(m1 + m2)
    keep = [jnp.zeros((1, tm), F32) for _ in range(N_GROUPS)]
    for _ in range(TOPK_GROUPS):
        m = functools.reduce(jnp.maximum, group_score)
        found = jnp.zeros((1, tm), F32)
        for g in range(N_GROUPS):
            hit = jnp.where(group_score[g] == m, 1.0 - found, 0.0)
            keep[g] = keep[g] + hit
            found = found + hit
            group_score[g] = jnp.where(hit > 0.0, -jnp.inf, group_score[g])
    vals = [jnp.where(keep[g] > 0.0, grp_biased[g], NEG_INF) for g in range(N_GROUPS)]
    chosen = [jnp.zeros((GROUP_SIZE, tm), F32) for _ in range(N_GROUPS)]
    picked = []
    for _ in range(TOP_K):
        m = jnp.max(functools.reduce(jnp.maximum, vals), axis=0, keepdims=True)
        cand = [jnp.where(vals[g] == m, grp_index[g], float(N_EXPERTS)) for g in range(N_GROUPS)]
        ei = jnp.min(functools.reduce(jnp.minimum, cand), axis=0, keepdims=True)
        sel = [grp_index[g] == ei for g in range(N_GROUPS)]
        s_k = functools.reduce(jnp.add, [jnp.where(sel[g], grp_scores[g], 0.0) for g in range(N_GROUPS)])
        picked.append((ei, jnp.sum(s_k, axis=0, keepdims=True)))
        vals = [jnp.where(sel[g], -jnp.inf, vals[g]) for g in range(N_GROUPS)]
        chosen = [jnp.where(sel[g], 1.0, chosen[g]) for g in range(N_GROUPS)]
    total = functools.reduce(jnp.add, [s for _, s in picked])
    chosen_all = jnp.concatenate(chosen, axis=0)
    tri = tri_ref[...]
    width = tri.shape[0]
    cnt = cnt_ref[...]
    parts = []
    for j in range(tm // width):
        ch = chosen_all[:, j * width:(j + 1) * width]
        parts.append(_dot(ch.astype(BF16), tri) + cnt)
        cnt = cnt + jnp.sum(ch, axis=1, keepdims=True)
    cnt_ref[...] = cnt
    rank_all = jnp.concatenate(parts, axis=1)
    for k, (ei, s) in enumerate(picked):
        idx_ref[k:k + 1, :] = ei.astype(jnp.int32)
        w_ref[k:k + 1, :] = s / total * ROUTED_SCALE
        r = functools.reduce(jnp.add, [
            jnp.where(grp_index[g] == ei, rank_all[g * GROUP_SIZE:(g + 1) * GROUP_SIZE, :], 0.0)
            for g in range(N_GROUPS)])
        rank_ref[k:k + 1, :] = jnp.sum(r, axis=0, keepdims=True).astype(jnp.int32)


ROUTE_TN = 2048
TRI_N = 256


def _router_kernel(logits_ref, br_ref, tri_ref, idx_ref, w_ref, rank_ref, cnt_out_ref, cnt_ref):
    @pl.when(pl.program_id(0) == 0)
    def _():
        cnt_ref[...] = jnp.zeros_like(cnt_ref)

    _route(logits_ref[...], br_ref[...], cnt_ref, tri_ref, idx_ref, w_ref, rank_ref)
    cnt_out_ref[...] = jnp.broadcast_to(cnt_ref[...], cnt_out_ref.shape).astype(jnp.int32)


def _router(logits_t, br):
    n_tok = logits_t.shape[1]
    col = lambda i: (0, i)
    const = lambda i: (0, 0)
    tri = (jnp.arange(TRI_N)[:, None] < jnp.arange(TRI_N)[None, :]).astype(BF16)
    blk = pl.BlockSpec((TOP_K, ROUTE_TN), col)
    return pl.pallas_call(
        _router_kernel,
        grid=(n_tok // ROUTE_TN,),
        in_specs=[pl.BlockSpec((N_EXPERTS, ROUTE_TN), col), pl.BlockSpec(br.shape, const),
                  pl.BlockSpec(tri.shape, const)],
        out_specs=[blk, blk, blk, pl.BlockSpec((N_EXPERTS, LANES), const)],
        out_shape=[
            jax.ShapeDtypeStruct((TOP_K, n_tok), jnp.int32),
            jax.ShapeDtypeStruct((TOP_K, n_tok), F32),
            jax.ShapeDtypeStruct((TOP_K, n_tok), jnp.int32),
            jax.ShapeDtypeStruct((N_EXPERTS, LANES), jnp.int32),
        ],
        scratch_shapes=[pltpu.VMEM((N_EXPERTS, 1), F32)],
        compiler_params=_cparams("arbitrary"),
        name="router",
    )(logits_t, br, tri)


def _post_kernel(o_ref, hl_ref, hc_ref, wout_ref, gtm_ref, g_ref, sh_ref, sc_ref, gtf_ref,
                 wrh_ref, wrl_ref, wsg_ref, wsu_ref, wsd_ref, hs_ref, f_ref, logits_ref):
    for c in range(TM // TM_CHUNK):
        rows = slice(c * TM_CHUNK, (c + 1) * TM_CHUNK)
        h1 = _token_rows(hl_ref, hc_ref, rows) + gtm_ref[0] * _dot(o_ref[rows, :], wout_ref[...])
        f = _rms(h1) * g_ref[...]
        f = f * (1.0 + sc_ref[0]) + sh_ref[0]
        f_hi = f.astype(BF16)
        f_lo = (f - f_hi.astype(F32)).astype(BF16)
        f_ref[rows, :] = _pack_bf16_pairs(f)
        logits_ref[:, rows] = (_dot_nt(wrh_ref[...], f_hi) + _dot_nt(wrh_ref[...], f_lo)
                               + _dot_nt(wrl_ref[...], f_hi))
        mid = _silu(_dot(f_hi, wsg_ref[...])) * _dot(f_hi, wsu_ref[...])
        shared = _dot(mid.astype(BF16), wsd_ref[...])
        hs_ref[rows, :] = h1 + gtf_ref[0] * shared


def _post(o, h_lat, h_ctx, wout, gtm, g_ffn, sh, sc, gtf, wrh, wrl, wsg, wsu, wsd, n_rows):
    row = lambda i: (i, 0)
    const = lambda i: (0, 0)
    mod = lambda i: (_mod_index(i), 0, 0)
    vec = pl.BlockSpec((1, 1, D_MODEL), mod)
    full = lambda a: pl.BlockSpec(a.shape, const)
    return pl.pallas_call(
        _post_kernel,
        grid=(n_rows // TM,),
        in_specs=[pl.BlockSpec((TM, D_MODEL), row)] + _token_specs() + [
            full(wout), vec, full(g_ffn), vec, vec, vec,
            full(wrh), full(wrl), full(wsg), full(wsu), full(wsd),
        ],
        out_specs=[
            pl.BlockSpec((TM, D_MODEL), row),
            pl.BlockSpec((TM, D_MODEL // 2), row),
            pl.BlockSpec((N_EXPERTS, TM), lambda i: (0, i)),
        ],
        out_shape=[
            jax.ShapeDtypeStruct((n_rows, D_MODEL), F32),
            jax.ShapeDtypeStruct((n_rows, D_MODEL // 2), jnp.int32),
            jax.ShapeDtypeStruct((N_EXPERTS, n_rows), F32),
        ],
        compiler_params=_cparams("parallel"),
        name="post_attn",
    )(o, h_lat, h_ctx, wout, gtm, g_ffn, sh, sc, gtf, wrh, wrl, wsg, wsu, wsd)


DEST_TN = 2048


def _dest_kernel(ps_ref, idx_ref, rank_ref, o_ref):
    idx = idx_ref[...]
    acc = rank_ref[...]
    for e in range(N_EXPERTS):
        acc = acc + jnp.where(idx == e, ps_ref[e], 0)
    o_ref[...] = acc


def _dest(pad_start, top_idx, rank):
    n_tok = top_idx.shape[1]
    blk = pl.BlockSpec((TOP_K, DEST_TN), lambda i: (0, i))
    return pl.pallas_call(
        _dest_kernel,
        grid=(n_tok // DEST_TN,),
        in_specs=[pl.BlockSpec(memory_space=pltpu.SMEM), blk, blk],
        out_specs=blk,
        out_shape=jax.ShapeDtypeStruct((TOP_K, n_tok), jnp.int32),
        compiler_params=_cparams("parallel"),
        name="dest",
    )(pad_start, top_idx, rank)


SC_CORES = 2
SC_SUBCORES = 16
SC_WORKERS = SC_CORES * SC_SUBCORES
SC_DISPATCH_ROWS = 32
SC_GATHER_ROWS = 64
ROW_WORDS = D_MODEL // 2


def _sc_mesh():
    return plsc.VectorSubcoreMesh(core_axis_name="core", subcore_axis_name="subcore")


def _sc_worker():
    return lax.axis_index("subcore") * SC_CORES + lax.axis_index("core")


def _sc_dispatch(f_packed, dest, n_slots):
    n_tok = f_packed.shape[0]
    per_worker = n_tok // SC_WORKERS
    n_chunks = per_worker // SC_DISPATCH_ROWS
    assert n_chunks * SC_DISPATCH_ROWS * SC_WORKERS == n_tok and n_chunks % 2 == 0
    dest4 = dest.reshape(TOP_K, SC_WORKERS, n_chunks, SC_DISPATCH_ROWS)

    def body(f_hbm, dest_hbm, xs_hbm, rows0, rows1, idx_v, load0, load1, scat0, scat1):
        wid = _sc_worker()
        for k in range(TOP_K):
            pltpu.sync_copy(dest_hbm.at[k, wid], idx_v.at[k])

        def load(c, buf, sem):
            start = pl.multiple_of(wid * per_worker + c * SC_DISPATCH_ROWS, 8)
            return pltpu.make_async_copy(f_hbm.at[pl.ds(start, SC_DISPATCH_ROWS)], buf, sem)

        def scatters(c, buf, sem):
            return [pltpu.make_async_copy(buf, xs_hbm.at[idx_v.at[k, c]], sem) for k in range(TOP_K)]

        load(0, rows0, load0).start()

        @pl.loop(0, n_chunks, step=2)
        def _(c):
            @pl.when(c > 0)
            def _():
                for cp in scatters(c - 1, rows1, scat1):
                    cp.wait()

            load(c + 1, rows1, load1).start()
            load(c, rows0, load0).wait()
            for cp in scatters(c, rows0, scat0):
                cp.start()
            load(c + 1, rows1, load1).wait()
            for cp in scatters(c + 1, rows1, scat1):
                cp.start()
            for cp in scatters(c, rows0, scat0):
                cp.wait()

            @pl.when(c + 2 < n_chunks)
            def _():
                load(c + 2, rows0, load0).start()

        for cp in scatters(n_chunks - 1, rows1, scat1):
            cp.wait()

    rows = pltpu.VMEM((SC_DISPATCH_ROWS, ROW_WORDS), jnp.int32)
    return pl.kernel(
        body,
        out_type=jax.ShapeDtypeStruct((n_slots, ROW_WORDS), jnp.int32),
        mesh=_sc_mesh(),
        scratch_types=[rows, rows, pltpu.VMEM((TOP_K, n_chunks, SC_DISPATCH_ROWS), jnp.int32)]
        + [pltpu.SemaphoreType.DMA] * 4,
        name="sc_dispatch",
    )(f_packed, dest4)


def _sc_gather(ys, dest):
    n_idx = dest.shape[0] * dest.shape[1]
    per_worker = n_idx // SC_WORKERS
    n_chunks = per_worker // SC_GATHER_ROWS
    assert n_chunks * SC_GATHER_ROWS * SC_WORKERS == n_idx and n_chunks % 2 == 0
    dest3 = dest.reshape(SC_WORKERS, n_chunks, SC_GATHER_ROWS)

    def body(ys_hbm, dest_hbm, out_hbm, rows0, rows1, idx_v, gat0, gat1, put0, put1):
        wid = _sc_worker()
        pltpu.sync_copy(dest_hbm.at[wid], idx_v)

        def gather(c, buf, sem):
            return pltpu.make_async_copy(ys_hbm.at[idx_v.at[c]], buf, sem)

        def put(c, buf, sem):
            start = pl.multiple_of(wid * per_worker + c * SC_GATHER_ROWS, 8)
            return pltpu.make_async_copy(buf, out_hbm.at[pl.ds(start, SC_GATHER_ROWS)], sem)

        gather(0, rows0, gat0).start()

        @pl.loop(0, n_chunks, step=2)
        def _(c):
            @pl.when(c > 0)
            def _():
                put(c - 1, rows1, put1).wait()

            gather(c + 1, rows1, gat1).start()
            gather(c, rows0, gat0).wait()
            put(c, rows0, put0).start()
            gather(c + 1, rows1, gat1).wait()
            put(c + 1, rows1, put1).start()
            put(c, rows0, put0).wait()

            @pl.when(c + 2 < n_chunks)
            def _():
                gather(c + 2, rows0, gat0).start()

        put(n_chunks - 1, rows1, put1).wait()

    rows = pltpu.VMEM((SC_GATHER_ROWS, ROW_WORDS), jnp.int32)
    return pl.kernel(
        body,
        out_type=jax.ShapeDtypeStruct((n_idx, ROW_WORDS), jnp.int32),
        mesh=_sc_mesh(),
        scratch_types=[rows, rows, pltpu.VMEM((n_chunks, SC_GATHER_ROWS), jnp.int32)]
        + [pltpu.SemaphoreType.DMA] * 4,
        name="sc_gather",
    )(ys, dest3)


def _moe_kernel(be_ref, nused_ref, slot_ref, next_ref, x_ref, wg_hbm, wu_hbm, wd_hbm, o_ref,
                wg_s, wu_s, wd_s, wg_f, wu_f, wd_f, sem, *, layer):
    i = pl.program_id(0)
    e = be_ref[i]
    used = i < nused_ref[0]
    first = jnp.logical_and(used, jnp.logical_or(i == 0, e != be_ref[jnp.maximum(i - 1, 0)]))

    def weight_copies(expert, slot):
        return [pltpu.make_async_copy(src.at[layer, expert], dst.at[slot], sem.at[slot, j])
                for j, (src, dst) in enumerate(((wg_hbm, wg_f), (wu_hbm, wu_f), (wd_hbm, wd_f)))]

    @pl.when(i == 0)
    def _():
        for cp in weight_copies(e, slot_ref[e]):
            cp.start()

    @pl.when(first)
    def _():
        slot = slot_ref[e]
        for cp in weight_copies(e, slot):
            cp.wait()
        wg_s[...] = wg_f[slot].astype(BF16)
        wu_s[...] = wu_f[slot].astype(BF16)
        wd_s[...] = wd_f[slot].astype(BF16)
        nxt = next_ref[e]

        @pl.when(nxt >= 0)
        def _():
            for cp in weight_copies(nxt, 1 - slot):
                cp.start()

    @pl.when(used)
    def _():
        half = D_MODEL // 2
        for c in range(MOE_ROWS // MOE_CHUNK):
            rows = slice(c * MOE_CHUNK, (c + 1) * MOE_CHUNK)
            lo, hi = _unpack_bf16_pairs(x_ref[rows, :])
            lo, hi = lo.astype(BF16), hi.astype(BF16)
            gate = _dot(lo, wg_s[:half, :]) + _dot(hi, wg_s[half:, :])
            up = _dot(lo, wu_s[:half, :]) + _dot(hi, wu_s[half:, :])
            mid = _silu(gate) * up
            o_ref[rows, :] = _pack_bf16_pairs(_dot(mid.astype(BF16), wd_s[...]))


def _moe(block_expert, n_used, slot, next_expert, xs, wg, wu, wd, layer):
    n_blocks = xs.shape[0] // MOE_ROWS
    rows = lambda i, *_: (i, 0)
    hbm = pl.BlockSpec(memory_space=pl.ANY)
    grid_spec = pltpu.PrefetchScalarGridSpec(
        num_scalar_prefetch=4,
        grid=(n_blocks,),
        in_specs=[pl.BlockSpec((MOE_ROWS, ROW_WORDS), rows), hbm, hbm, hbm],
        out_specs=pl.BlockSpec((MOE_ROWS, ROW_WORDS), rows),
        scratch_shapes=[
            pltpu.VMEM((D_MODEL, EXPERT_FF), BF16),
            pltpu.VMEM((D_MODEL, EXPERT_FF), BF16),
            pltpu.VMEM((EXPERT_FF, D_MODEL), BF16),
            pltpu.VMEM((2, D_MODEL, EXPERT_FF), F32),
            pltpu.VMEM((2, D_MODEL, EXPERT_FF), F32),
            pltpu.VMEM((2, EXPERT_FF, D_MODEL), F32),
            pltpu.SemaphoreType.DMA((2, 3)),
        ],
    )
    return pl.pallas_call(
        functools.partial(_moe_kernel, layer=layer),
        grid_spec=grid_spec,
        out_shape=jax.ShapeDtypeStruct((xs.shape[0], ROW_WORDS), jnp.int32),
        compiler_params=_cparams("arbitrary"),
        name="moe_experts",
    )(block_expert, n_used, slot, next_expert, xs, wg, wu, wd)


def _combine_kernel(y_ref, w_ref, hs_ref, gtf_ref, gfin_ref, o_ref, *, final):
    w = w_ref[...]
    acc_lo = acc_hi = None
    for k in range(TOP_K):
        lo, hi = _unpack_bf16_pairs(y_ref[k])
        wk = w[:, k:k + 1]
        acc_lo = lo * wk if k == 0 else acc_lo + lo * wk
        acc_hi = hi * wk if k == 0 else acc_hi + hi * wk
    out = hs_ref[...] + gtf_ref[0] * jnp.concatenate([acc_lo, acc_hi], axis=1)
    if final:
        out = _rms(out) * gfin_ref[...]
    o_ref[...] = out


def _combine(y, w, hs, gtf, g_final, *, final):
    n_rows = hs.shape[0]
    return pl.pallas_call(
        functools.partial(_combine_kernel, final=final),
        grid=(n_rows // TM_COMB,),
        in_specs=[
            pl.BlockSpec((TOP_K, TM_COMB, ROW_WORDS), lambda i: (0, i, 0)),
            pl.BlockSpec((TM_COMB, TOP_K), lambda i: (i, 0)),
            pl.BlockSpec((TM_COMB, D_MODEL), lambda i: (i, 0)),
            pl.BlockSpec((1, 1, D_MODEL), lambda i: (jnp.minimum(i // (SEQ // TM_COMB), BATCH), 0, 0)),
            pl.BlockSpec((1, D_MODEL), lambda i: (0, 0)),
        ],
        out_specs=pl.BlockSpec((TM_COMB, D_MODEL), lambda i: (i, 0)),
        out_shape=jax.ShapeDtypeStruct((n_rows, D_MODEL), F32),
        compiler_params=_cparams("parallel"),
        name="combine_final" if final else "combine",
    )(y, w, hs, gtf, g_final)


def _rope_tables():
    rows = SEQ // GRID_W

    def angles(rot_dim):
        half = rot_dim // 2
        inv_freq = ROPE_BASE ** (-jnp.arange(0, half, 2, dtype=F32) / half)
        row = jnp.repeat(jnp.arange(rows, dtype=F32), GRID_W)
        col = jnp.tile(jnp.arange(GRID_W, dtype=F32), rows)
        ang_r = row[:, None] * inv_freq
        ang_c = col[:, None] * inv_freq
        return jnp.concatenate([ang_r, ang_r, ang_c, ang_c], axis=-1)

    def signed(sin, quarter):
        sign = jnp.where((jnp.arange(sin.shape[-1]) // quarter) % 2 == 0, -1.0, 1.0)
        return sin * sign

    def with_identity(cos, sin):
        cos = jnp.concatenate([cos, jnp.ones((TM, LANES), F32)], axis=0)
        sin = jnp.concatenate([sin, jnp.zeros((TM, LANES), F32)], axis=0)
        return cos, sin

    a64 = angles(SWA_HEAD_DIM)
    c64 = jnp.tile(jnp.cos(a64), (1, LANES // SWA_HEAD_DIM))
    s64 = jnp.tile(signed(jnp.sin(a64), SWA_HEAD_DIM // 4), (1, LANES // SWA_HEAD_DIM))
    a32 = angles(MLA_ROPE)
    pad_lo = MLA_NOPE
    pad_hi = LANES - MLA_NOPE - MLA_ROPE
    cm = jnp.concatenate([jnp.ones((SEQ, pad_lo), F32), jnp.cos(a32), jnp.ones((SEQ, pad_hi), F32)], axis=-1)
    sm = jnp.concatenate([jnp.zeros((SEQ, pad_lo), F32), signed(jnp.sin(a32), MLA_ROPE // 4),
                          jnp.zeros((SEQ, pad_hi), F32)], axis=-1)
    return with_identity(cm, sm) + with_identity(c64, s64)


def _layer0_weights(wa_in, wa_uq, wa_ukv, wa_out):
    d = D_MODEL
    cq, ckv, kr, qs, ks, vs = jnp.split(
        wa_in, [C_CKV, C_KR, C_KR + MLA_ROPE, C_KR + MLA_ROPE + SWA_HEADS * SWA_HEAD_DIM,
                C_KR + MLA_ROPE + (SWA_HEADS + SWA_KV_HEADS) * SWA_HEAD_DIM], axis=-1)
    kr_pad = jnp.concatenate([jnp.zeros((d, MLA_NOPE), F32), kr,
                              jnp.zeros((d, LANES - MLA_NOPE - MLA_ROPE), F32)], axis=-1)
    qs_pair = qs.reshape(d, SWA_KV_HEADS, SWA_GROUP, SWA_HEAD_DIM).transpose(0, 2, 1, 3).reshape(d, -1)
    win = jnp.concatenate([cq, ckv, kr_pad, qs_pair, ks, vs], axis=-1).astype(BF16)
    uq = wa_uq.reshape(MLA_Q_RANK, MLA_HEADS, MLA_NOPE + MLA_ROPE)
    uq = jnp.pad(uq, ((0, 0), (0, 0), (0, LANES - MLA_NOPE - MLA_ROPE))).reshape(MLA_Q_RANK, -1)
    ukv = wa_ukv.reshape(MLA_KV_RANK, MLA_HEADS, MLA_NOPE + MLA_V)
    uk = jnp.pad(ukv[:, :, :MLA_NOPE], ((0, 0), (0, 0), (0, LANES - MLA_NOPE))).reshape(MLA_KV_RANK, -1)
    uv = ukv[:, :, MLA_NOPE:].reshape(MLA_KV_RANK, -1)
    wukv = jnp.concatenate([uk, uv], axis=-1)
    n_mla = MLA_HEADS * MLA_V
    out_swa = wa_out[n_mla:].reshape(SWA_KV_HEADS, SWA_GROUP, SWA_HEAD_DIM, d).transpose(1, 0, 2, 3)
    wout = jnp.concatenate([wa_out[:n_mla], out_swa.reshape(-1, d)], axis=0)
    return win, uq.astype(BF16), wukv.astype(BF16), wout.astype(BF16)


def _layer1_weights(wc_in):
    per_head = 4 * DIFF_HALF + DIFF_V
    w = wc_in.reshape(D_MODEL, DIFF_HEADS, per_head)
    q = w[:, :, :2 * DIFF_HALF].reshape(D_MODEL, -1)
    k = w[:, :, 2 * DIFF_HALF:4 * DIFF_HALF].reshape(D_MODEL, -1)
    v = w[:, :, 4 * DIFF_HALF:].reshape(D_MODEL, -1)
    return jnp.concatenate([q, k, v], axis=-1).astype(BF16)


def _block_tables(counts, n_blocks):
    padded = (counts + MOE_ROWS - 1) // MOE_ROWS * MOE_ROWS
    pad_end = jnp.cumsum(padded)
    pad_start = (pad_end - padded).astype(jnp.int32)
    block_start = jnp.arange(n_blocks, dtype=jnp.int32) * MOE_ROWS
    block_expert = jnp.minimum((pad_end[None, :] <= block_start[:, None]).sum(axis=1),
                               N_EXPERTS - 1).astype(jnp.int32)
    n_used = (pad_end[-1] // MOE_ROWS).astype(jnp.int32).reshape(1)
    has = padded > 0
    slot = ((jnp.cumsum(has) - 1) & 1).astype(jnp.int32)
    ids = jnp.where(has, jnp.arange(N_EXPERTS, dtype=jnp.int32), N_EXPERTS)
    after = jnp.concatenate([lax.cummin(ids, reverse=True)[1:], jnp.full((1,), N_EXPERTS, jnp.int32)])
    next_expert = jnp.where(after < N_EXPERTS, after, -1).astype(jnp.int32)
    return pad_start, block_expert, n_used, slot, next_expert


def _moe_layer(f_packed, top_idx, top_w, rank, counts, hs, gtf, g_final, wg, wu, wd, layer, *, final):
    n_tok = top_idx.shape[1]
    n_blocks = -(-(n_tok * TOP_K) // MOE_ROWS) + N_EXPERTS
    pad_start, block_expert, n_used, slot, next_expert = _block_tables(counts[:, 0], n_blocks)
    dest = _dest(pad_start, top_idx, rank)
    xs = _sc_dispatch(f_packed, dest, n_blocks * MOE_ROWS)
    ys = _moe(block_expert, n_used, slot, next_expert, xs, wg, wu, wd, layer)
    y = _sc_gather(ys, dest).reshape(TOP_K, n_tok, ROW_WORDS)
    return _combine(y, top_w.T, hs, gtf, g_final, final=final)


def kernel(x, c, ctx, c_ctx, w_ada, b_ada, g_mix, g_ffn, wa_in, ga_q, ga_kv, wa_uq, wa_ukv, wa_sink, wa_out,
           wc_in, lam_q1, lam_k1, lam_q2, lam_k2, gc_sub, wc_out, w_router, b_router, we_gate, we_up, we_down,
           ws_gate, ws_up, ws_down, g_final):
    d = D_MODEL
    x2, ctx2 = x.reshape(N_LAT, d), ctx.reshape(N_CTX, d)
    cc =jnp.concatenate([c, c_ctx[None, :], jnp.zeros((MOD_ROWS - BATCH - 1, d), F32)], axis=0)
    mod = _ada(cc, w_ada, b_ada).reshape(DEPTH, MOD_ROWS, 6, 1, d)

    def mod_vec(layer, j):
        return mod[layer, :, j]

    cm, sm, c64, s64 = _rope_tables()
    g_final2 = g_final.reshape(1, d)

    def router_weights(layer):
        wt = w_router[layer].T
        hi = wt.astype(BF16)
        lo = (wt - hi.astype(F32)).astype(BF16)
        return hi, lo

    def shared_weights(layer):
        return ws_gate[layer].astype(BF16), ws_up[layer].astype(BF16), ws_down[layer].astype(BF16)

    win, wuq, wukv, wout0 = _layer0_weights(wa_in[0], wa_uq[0], wa_ukv[0], wa_out[0])
    qm, km, vm, qs, ks, vs = _proj0(x2, ctx2, g_mix[0].reshape(1, d), mod_vec(0, 0), mod_vec(0, 1), win,
                                    ga_q[0].reshape(1, -1), ga_kv[0].reshape(1, -1), wuq, wukv, cm, sm, c64, s64)
    sink = wa_sink[0].astype(F32)
    o = _mla_attention(qm, km, vm, None, ctx_queries=False)
    o = _mla_attention(qm, km, vm, o, ctx_queries=True)
    o = _swa_attention(sink, qs, ks, vs, o, ctx_queries=False)
    o = _swa_attention(sink, qs, ks, vs, o, ctx_queries=True)
    hs, f, logits_t = _post(
        o, x2, ctx2, wout0, mod_vec(0, 2), g_ffn[0].reshape(1, d), mod_vec(0, 3), mod_vec(0, 4), mod_vec(0, 5),
        *router_weights(0), *shared_weights(0), N_TOK)
    top_idx, top_w, rank, counts = _router(logits_t, b_router[0].reshape(N_EXPERTS, 1))
    h = _moe_layer(f, top_idx, top_w, rank, counts, hs, mod_vec(0, 5), g_final2,
                   we_gate, we_up, we_down, 0, final=False)

    layer = 1
    lam_init = 0.8 - 0.6 * math.exp(-0.3 * layer)
    q1, k1, v1 = _proj1(h, g_mix[1].reshape(1, d), mod_vec(1, 0), mod_vec(1, 1), _layer1_weights(wc_in[0]),
                        c64, s64)
    lam_vecs = jnp.zeros((8, LANES), F32).at[:4, :DIFF_HALF].set(
        jnp.stack([lam_q1[0], lam_k1[0], lam_q2[0], lam_k2[0]]).astype(F32))
    o = _diff_attention(q1, k1, v1, lam_vecs, gc_sub[0].reshape(1, DIFF_V), lam_init)
    hs, f, logits_t = _post(
        o, h, h, wc_out[0].astype(BF16), mod_vec(1, 2), g_ffn[1].reshape(1, d), mod_vec(1, 3), mod_vec(1, 4),
        mod_vec(1, 5), *router_weights(1), *shared_weights(1), N_LAT)
    top_idx, top_w, rank, counts = _router(logits_t, b_router[1].reshape(N_EXPERTS, 1))
    out = _moe_layer(f, top_idx, top_w, rank, counts, hs, mod_vec(1, 5), g_final2,
                     we_gate, we_up, we_down, 1, final=True)
    return out.reshape(BATCH, SEQ, d)
```

```python
import functools
import math

import jax
import jax.numpy as jnp
from jax import lax
from jax.experimental import pallas as pl
from jax.experimental.pallas import tpu as pltpu
from jax.experimental.pallas import tpu_sc as plsc

F32 = jnp.float32
BF16 = jnp.bfloat16

D_MODEL = 1024
BATCH = 8
SEQ = 2048
DEPTH = 2
CTX_LEN = 256
GRID_W = 64
ROPE_BASE = 10000.0
EPS = 1e-6
NEG_INF = -1e30

MLA_HEADS = 8
MLA_Q_RANK = 384
MLA_KV_RANK = 256
MLA_NOPE = 64
MLA_ROPE = 32
MLA_V = 64
SWA_HEADS = 8
SWA_KV_HEADS = 2
SWA_HEAD_DIM = 64
SWA_GROUP = SWA_HEADS // SWA_KV_HEADS
WINDOW = 128
DIFF_HEADS = 8
DIFF_HALF = 64
DIFF_V = 128
N_EXPERTS = 64
N_GROUPS = 8
GROUP_SIZE = N_EXPERTS // N_GROUPS
TOPK_GROUPS = 4
TOP_K = 8
EXPERT_FF = 256
SHARED_FF = 256
ROUTED_SCALE = 2.5

LOG2E = math.log2(math.e)
LANES = 128
N_LAT = BATCH * SEQ
N_CTX = BATCH * CTX_LEN
N_TOK = N_LAT + N_CTX
MOD_ROWS = 16

TM = 512
TM_CHUNK = 256
LAT_BLOCKS_PER_BATCH = SEQ // TM
TQ_MLA = 512
MLA_PAIRS_PER_STEP = 4
TQ_DIFF = 512
DIFF_HEADS_PER_STEP = 4
SWA_BLOCK = 128
SWA_QBLOCKS = 4
MOE_ROWS = 512
MOE_CHUNK = 256
TM_COMB = 256
VMEM_LIMIT = 56 * 1024 * 1024

C_CQ = 0
C_CKV = C_CQ + MLA_Q_RANK
C_KR = C_CKV + MLA_KV_RANK
C_QS = C_KR + LANES
C_KS = C_QS + SWA_HEADS * SWA_HEAD_DIM
C_VS = C_KS + SWA_KV_HEADS * SWA_HEAD_DIM
C_END = C_VS + SWA_KV_HEADS * SWA_HEAD_DIM


def _cparams(*sem):
    return pltpu.CompilerParams(dimension_semantics=sem, vmem_limit_bytes=VMEM_LIMIT)


def _dot(a, b):
    return jnp.dot(a, b, preferred_element_type=F32)


def _dot_nt(a, b):
    return lax.dot_general(a, b, (((1,), (1,)), ((), ())), preferred_element_type=F32)


def _rms(x):
    return x * lax.rsqrt(jnp.mean(x * x, axis=-1, keepdims=True) + EPS)


def _silu(x):
    return x * jax.nn.sigmoid(x)


def _rope(x, cos, sin_signed, shift):
    n = x.shape[-1]
    lane = lax.broadcasted_iota(jnp.int32, x.shape, 1)
    first = (lane & shift) == 0
    rot = jnp.where(first, pltpu.roll(x, n - shift, 1), pltpu.roll(x, shift, 1))
    return x * cos + rot * sin_signed


def _mod_index(i):
    return jnp.minimum(i // LAT_BLOCKS_PER_BATCH, BATCH)


def _rope_index(i):
    return jnp.where(i < N_LAT // TM, i % LAT_BLOCKS_PER_BATCH, LAT_BLOCKS_PER_BATCH)


ADA_TN = 1536


def _ada_kernel(c_ref, w_ref, b_ref, o_ref):
    s = _silu(c_ref[...]).astype(BF16)
    o_ref[0] = _dot(s, w_ref[0].astype(BF16)) + b_ref[0]


def _ada(cc, w_ada, b_ada):
    n_out = w_ada.shape[-1]
    return pl.pallas_call(
        _ada_kernel,
        grid=(DEPTH, n_out // ADA_TN),
        in_specs=[
            pl.BlockSpec((MOD_ROWS, D_MODEL), lambda l, j: (0, 0)),
            pl.BlockSpec((1, D_MODEL, ADA_TN), lambda l, j: (l, 0, j)),
            pl.BlockSpec((1, 1, ADA_TN), lambda l, j: (l, 0, j)),
        ],
        out_specs=pl.BlockSpec((1, MOD_ROWS, ADA_TN), lambda l, j: (l, 0, j)),
        out_shape=jax.ShapeDtypeStruct((DEPTH, MOD_ROWS, n_out), F32),
        compiler_params=_cparams("parallel", "parallel"),
        name="ada",
    )(cc, w_ada, b_ada.reshape(DEPTH, 1, n_out))


def _token_rows(lat_ref, ctx_ref, rows):
    return jnp.where(pl.program_id(0) < N_LAT // TM, lat_ref[rows, :], ctx_ref[rows, :])


def _token_specs():
    lat_blocks = N_LAT // TM
    return [pl.BlockSpec((TM, D_MODEL), lambda i: (jnp.minimum(i, lat_blocks - 1), 0)),
            pl.BlockSpec((TM, D_MODEL), lambda i: (jnp.maximum(i - lat_blocks, 0), 0))]


def _proj0_kernel(x_ref, ctx_ref, g_ref, sh_ref, sc_ref, win_ref, gq_ref, gkv_ref, wuq_ref, wukv_ref,
                  cm_ref, sm_ref, c64_ref, s64_ref,
                  qm_ref, km_ref, vm_ref, qs_ref, ks_ref, vs_ref):
    q_scale = (MLA_NOPE + MLA_ROPE) ** -0.5 * LOG2E
    s_scale = SWA_HEAD_DIM ** -0.5 * LOG2E
    for c in range(TM // TM_CHUNK):
        rows = slice(c * TM_CHUNK, (c + 1) * TM_CHUNK)
        a = _rms(_token_rows(x_ref, ctx_ref, rows)) * g_ref[...]
        a = a * (1.0 + sc_ref[0]) + sh_ref[0]
        p = _dot(a.astype(BF16), win_ref[...])
        nq = _rms(p[:, C_CQ:C_CKV]) * gq_ref[...]
        nkv = _rms(p[:, C_CKV:C_KR]) * gkv_ref[...]
        q = _dot(nq.astype(BF16), wuq_ref[...])
        kv = _dot(nkv.astype(BF16), wukv_ref[...])
        cm, sm = cm_ref[rows, :], sm_ref[rows, :]
        c64, s64 = c64_ref[rows, :], s64_ref[rows, :]
        kr = _rope(p[:, C_KR:C_QS], cm, sm, MLA_ROPE // 4)
        for h in range(MLA_HEADS):
            sl = slice(h * LANES, (h + 1) * LANES)
            qm_ref[rows, sl] = (_rope(q[:, sl], cm, sm, MLA_ROPE // 4) * q_scale).astype(BF16)
            km_ref[rows, sl] = (kv[:, sl] + kr).astype(BF16)
        vm_ref[rows, :] = kv[:, MLA_HEADS * LANES:].astype(BF16)
        for g in range(SWA_GROUP):
            sl = slice(g * LANES, (g + 1) * LANES)
            qs_ref[rows, sl] = (_rope(p[:, C_QS + g * LANES:C_QS + (g + 1) * LANES], c64, s64,
                                      SWA_HEAD_DIM // 4) * s_scale).astype(BF16)
        ks_ref[rows, :] = _rope(p[:, C_KS:C_VS], c64, s64, SWA_HEAD_DIM // 4).astype(BF16)
        vs_ref[rows, :] = p[:, C_VS:C_END].astype(BF16)


def _proj0(x, ctx, g_mix, sh, sc, win, gq, gkv, wuq, wukv, cm, sm, c64, s64):
    row = lambda i: (i, 0)
    const = lambda i: (0, 0)
    mod = lambda i: (_mod_index(i), 0, 0)
    rope = lambda i: (_rope_index(i), 0)
    widths = (MLA_HEADS * LANES, MLA_HEADS * LANES, MLA_HEADS * MLA_V,
              SWA_HEADS * SWA_HEAD_DIM, SWA_KV_HEADS * SWA_HEAD_DIM, SWA_KV_HEADS * SWA_HEAD_DIM)
    return pl.pallas_call(
        _proj0_kernel,
        grid=(N_TOK // TM,),
        in_specs=_token_specs() + [
            pl.BlockSpec((1, D_MODEL), const),
            pl.BlockSpec((1, 1, D_MODEL), mod),
            pl.BlockSpec((1, 1, D_MODEL), mod),
            pl.BlockSpec(win.shape, const),
            pl.BlockSpec(gq.shape, const),
            pl.BlockSpec(gkv.shape, const),
            pl.BlockSpec(wuq.shape, const),
            pl.BlockSpec(wukv.shape, const),
            pl.BlockSpec((TM, LANES), rope),
            pl.BlockSpec((TM, LANES), rope),
            pl.BlockSpec((TM, LANES), rope),
            pl.BlockSpec((TM, LANES), rope),
        ],
        out_specs=[pl.BlockSpec((TM, w), row) for w in widths],
        out_shape=[jax.ShapeDtypeStruct((N_TOK, w), BF16) for w in widths],
        compiler_params=_cparams("parallel"),
        name="proj0",
    )(x, ctx, g_mix, sh, sc, win, gq, gkv, wuq, wukv, cm, sm, c64, s64)


def _proj1_kernel(h_ref, g_ref, sh_ref, sc_ref, w_ref, c64_ref, s64_ref, q_ref, k_ref, v_ref):
    width = DIFF_HEADS * LANES
    scale = DIFF_HALF ** -0.5 * LOG2E
    for c in range(TM // TM_CHUNK):
        rows = slice(c * TM_CHUNK, (c + 1) * TM_CHUNK)
        a = _rms(h_ref[rows, :]) * g_ref[...]
        a = a * (1.0 + sc_ref[0]) + sh_ref[0]
        p = _dot(a.astype(BF16), w_ref[...])
        c64, s64 = c64_ref[rows, :], s64_ref[rows, :]
        for h in range(DIFF_HEADS):
            sl = slice(h * LANES, (h + 1) * LANES)
            q_ref[rows, sl] = (_rope(p[:, sl], c64, s64, DIFF_HALF // 4) * scale).astype(BF16)
            k_ref[rows, sl] = _rope(p[:, width + h * LANES:width + (h + 1) * LANES], c64, s64,
                                    DIFF_HALF // 4).astype(BF16)
        v_ref[rows, :] = p[:, 2 * width:].astype(BF16)


def _proj1(h, g_mix, sh, sc, w, c64, s64):
    row = lambda i: (i, 0)
    const = lambda i: (0, 0)
    mod = lambda i: (_mod_index(i), 0, 0)
    rope = lambda i: (_rope_index(i), 0)
    width = DIFF_HEADS * LANES
    return pl.pallas_call(
        _proj1_kernel,
        grid=(N_TOK // TM,),
        in_specs=[
            pl.BlockSpec((TM, D_MODEL), row),
            pl.BlockSpec((1, D_MODEL), const),
            pl.BlockSpec((1, 1, D_MODEL), mod),
            pl.BlockSpec((1, 1, D_MODEL), mod),
            pl.BlockSpec(w.shape, const),
            pl.BlockSpec((TM, LANES), rope),
            pl.BlockSpec((TM, LANES), rope),
        ],
        out_specs=[pl.BlockSpec((TM, width), row)] * 3,
        out_shape=[jax.ShapeDtypeStruct((N_TOK, width), BF16)] * 3,
        compiler_params=_cparams("parallel"),
        name="proj1",
    )(h, g_mix, sh, sc, w, c64, s64)


def _mla_kernel(*refs, n_kv):
    q_ref = refs[0]
    k_refs = refs[1:1 + n_kv]
    v_refs = refs[1 + n_kv:1 + 2 * n_kv]
    o_ref = refs[1 + 2 * n_kv]
    lane = lax.broadcasted_iota(jnp.int32, (q_ref.shape[0], LANES), 1)
    own = [lane < MLA_V, lane >= MLA_V]
    for pair in range(MLA_PAIRS_PER_STEP):
        vsl = slice(pair * LANES, (pair + 1) * LANES)
        outs = []
        for hh in range(2):
            sl = slice((2 * pair + hh) * LANES, (2 * pair + hh + 1) * LANES)
            qh = q_ref[:, sl]
            ss = [_dot_nt(qh, k[:, sl]) for k in k_refs]
            m = functools.reduce(jnp.maximum, [jnp.max(s, axis=-1, keepdims=True) for s in ss])
            o = None
            for s, v in zip(ss, v_refs):
                vp = v[:, vsl]
                vlane = lax.broadcasted_iota(jnp.int32, vp.shape, 1)
                keep = (vlane < MLA_V) if hh == 0 else (vlane >= MLA_V)
                vh = jnp.where(keep, vp, jnp.ones_like(vp))
                part = _dot(jnp.exp2(s - m).astype(BF16), vh)
                o = part if o is None else o + part
            outs.append(o / pltpu.roll(o, MLA_V, 1))
        o_ref[:, vsl] = jnp.where(own[0], outs[0], outs[1]).astype(BF16)


def _into_buffer(kernel_fn, buf):
    if buf is None:
        return kernel_fn, [], []

    def without_alias_ref(*refs):
        return kernel_fn(*refs[:-2], refs[-1])

    return without_alias_ref, [pl.BlockSpec(memory_space=pl.ANY)], [buf]


def _mla_attention(qm, km, vm, buf, *, ctx_queries):
    groups = MLA_HEADS // 2 // MLA_PAIRS_PER_STEP
    qk_w = 2 * LANES * MLA_PAIRS_PER_STEP
    v_w = LANES * MLA_PAIRS_PER_STEP
    if ctx_queries:
        tq, n_q = CTX_LEN, 1
        q_map = lambda b, h, i: (N_LAT // CTX_LEN + b, h)
        kv_specs = [(CTX_LEN, lambda b, h, i: (N_LAT // CTX_LEN + b, h))]
    else:
        tq, n_q = TQ_MLA, SEQ // TQ_MLA
        q_map = lambda b, h, i: (b * n_q + i, h)
        kv_specs = [(CTX_LEN, lambda b, h, i: (N_LAT // CTX_LEN + b, h)), (SEQ, lambda b, h, i: (b, h))]
    n_kv = len(kv_specs)
    in_specs = [pl.BlockSpec((tq, qk_w), q_map)]
    in_specs += [pl.BlockSpec((n, qk_w), m) for n, m in kv_specs]
    in_specs += [pl.BlockSpec((n, v_w), m) for n, m in kv_specs]
    kernel_fn, extra_specs, extra_args = _into_buffer(functools.partial(_mla_kernel, n_kv=n_kv), buf)
    n_in = len(in_specs)
    return pl.pallas_call(
        kernel_fn,
        grid=(BATCH, groups, n_q),
        in_specs=in_specs + extra_specs,
        out_specs=pl.BlockSpec((tq, v_w), q_map),
        out_shape=jax.ShapeDtypeStruct((N_TOK, D_MODEL), BF16),
        input_output_aliases={n_in: 0} if extra_args else {},
        compiler_params=_cparams("parallel", "parallel", "parallel"),
        name="mla_ctx" if ctx_queries else "mla_lat",
    )(qm, *([km] * n_kv), *([vm] * n_kv), *extra_args)


def _swa_kernel(sink_ref, *refs, band, n_blocks):
    q_ref = refs[0]
    if band:
        n_kv = SWA_QBLOCKS + 2
        k_blocks = refs[1:1 + n_kv]
        kx = refs[1 + n_kv]
        v_blocks = refs[2 + n_kv:2 + 2 * n_kv]
        vx, o_ref = refs[2 + 2 * n_kv], refs[3 + 2 * n_kv]
        chains, rows = SWA_QBLOCKS, SWA_BLOCK
    else:
        kx, vx, o_ref = refs[1:]
        chains, rows = 1, q_ref.shape[0]
    stacked = SWA_GROUP * rows
    k_ctx = kx[...]
    v_ctx = vx[...]
    lane = lax.broadcasted_iota(jnp.int32, (rows, LANES), 1)
    low = lane < SWA_HEAD_DIM
    row_group = lax.broadcasted_iota(jnp.int32, (stacked, 1), 0) // rows
    for c in range(chains):
        qrows = slice(c * rows, (c + 1) * rows)
        if band:
            n = pl.program_id(1) * SWA_QBLOCKS + c
            k_band = jnp.concatenate([k[...] for k in k_blocks[c:c + 3]], axis=0)
            v_band = jnp.concatenate([v[...] for v in v_blocks[c:c + 3]], axis=0)
            qq = lax.broadcasted_iota(jnp.int32, (stacked, 3 * SWA_BLOCK), 0) & (SWA_BLOCK - 1)
            kk = lax.broadcasted_iota(jnp.int32, (stacked, 3 * SWA_BLOCK), 1)
            rel = kk - SWA_BLOCK - qq
            key_pos = (n - 1) * SWA_BLOCK + kk
            valid = (jnp.abs(rel) <= WINDOW) & (key_pos >= 0) & (key_pos < n_blocks * SWA_BLOCK)
        halves = []
        for hk in range(SWA_KV_HEADS):
            keep = low if hk == 0 else jnp.logical_not(low)
            qh = jnp.concatenate(
                [jnp.where(keep, q_ref[qrows, g * LANES:(g + 1) * LANES], jnp.zeros((rows, LANES), BF16))
                 for g in range(SWA_GROUP)], axis=0)
            sink = jnp.zeros((stacked, 1), F32)
            for g in range(SWA_GROUP):
                sink = jnp.where(row_group == g, sink_ref[hk * SWA_GROUP + g] * LOG2E, sink)
            s_ctx = _dot_nt(qh, k_ctx)
            m = jnp.maximum(jnp.max(s_ctx, axis=-1, keepdims=True), sink)
            if band:
                s_band = jnp.where(valid, _dot_nt(qh, k_band), NEG_INF)
                m = jnp.maximum(m, jnp.max(s_band, axis=-1, keepdims=True))

            def ones_other(v, hk=hk):
                vlane = lax.broadcasted_iota(jnp.int32, v.shape, 1)
                own = (vlane < SWA_HEAD_DIM) if hk == 0 else (vlane >= SWA_HEAD_DIM)
                return jnp.where(own, v, jnp.ones_like(v))

            o = _dot(jnp.exp2(s_ctx - m).astype(BF16), ones_other(v_ctx))
            if band:
                o = o + _dot(jnp.exp2(s_band - m).astype(BF16), ones_other(v_band))
            denom = pltpu.roll(o, SWA_HEAD_DIM, 1) + jnp.exp2(sink - m)
            halves.append(o / denom)
        for g in range(SWA_GROUP):
            rs = slice(g * rows, (g + 1) * rows)
            o_ref[qrows, g * LANES:(g + 1) * LANES] = jnp.where(low, halves[0][rs], halves[1][rs]).astype(BF16)


def _swa_attention(sink, qs, ks, vs, buf, *, ctx_queries):
    width = SWA_HEADS * SWA_HEAD_DIM
    kvw = SWA_KV_HEADS * SWA_HEAD_DIM
    ctx_map = lambda b, i: (N_LAT // CTX_LEN + b, 0)
    smem = pl.BlockSpec(memory_space=pltpu.SMEM)
    if ctx_queries:
        grid = (BATCH, 1)
        in_specs = [smem, pl.BlockSpec((CTX_LEN, width), ctx_map),
                    pl.BlockSpec((CTX_LEN, kvw), ctx_map), pl.BlockSpec((CTX_LEN, kvw), ctx_map)]
        args = (sink, qs, ks, vs)
        out_spec = pl.BlockSpec((CTX_LEN, width), lambda b, i: (N_LAT // CTX_LEN + b, 1))
        n_blocks = 1
    else:
        n_blocks = SEQ // SWA_BLOCK
        steps = n_blocks // SWA_QBLOCKS
        grid = (BATCH, steps)
        q_rows = SWA_QBLOCKS * SWA_BLOCK

        def kv_block(offset):
            return lambda b, i: (b * n_blocks + jnp.clip(i * SWA_QBLOCKS + offset, 0, n_blocks - 1), 0)

        band_specs = [pl.BlockSpec((SWA_BLOCK, kvw), kv_block(off)) for off in range(-1, SWA_QBLOCKS + 1)]
        in_specs = ([smem, pl.BlockSpec((q_rows, width), lambda b, i: (b * steps + i, 0))]
                    + band_specs + [pl.BlockSpec((CTX_LEN, kvw), ctx_map)]
                    + band_specs + [pl.BlockSpec((CTX_LEN, kvw), ctx_map)])
        n_band = len(band_specs)
        args = (sink, qs) + (ks,) * (n_band + 1) + (vs,) * (n_band + 1)
        out_spec = pl.BlockSpec((q_rows, width), lambda b, i: (b * steps + i, 1))
    kernel_fn, extra_specs, extra_args = _into_buffer(
        functools.partial(_swa_kernel, band=not ctx_queries, n_blocks=n_blocks), buf)
    return pl.pallas_call(
        kernel_fn,
        grid=grid,
        in_specs=in_specs + extra_specs,
        out_specs=out_spec,
        out_shape=jax.ShapeDtypeStruct((N_TOK, D_MODEL), BF16),
        input_output_aliases={len(in_specs): 0} if extra_args else {},
        compiler_params=_cparams("parallel", "parallel"),
        name="swa_ctx" if ctx_queries else "swa_lat",
    )(*args, *extra_args)


def _diff_kernel(q_ref, kc_ref, kl_ref, vc_ref, vl_ref, lam_ref, g_ref, o_ref, *, lam_init):
    lam = (jnp.exp(jnp.sum(lam_ref[0:1, :] * lam_ref[1:2, :], axis=-1, keepdims=True))
           - jnp.exp(jnp.sum(lam_ref[2:3, :] * lam_ref[3:4, :], axis=-1, keepdims=True)) + lam_init)
    lane = lax.broadcasted_iota(jnp.int32, (q_ref.shape[0], LANES), 1)
    low = lane < DIFF_HALF
    k_refs = (kc_ref, kl_ref)
    v_refs = (vc_ref, vl_ref)
    for h in range(DIFF_HEADS_PER_STEP):
        sl = slice(h * LANES, (h + 1) * LANES)
        q = q_ref[:, sl]
        zero = jnp.zeros_like(q)

        def numerators(qh):
            ss = [_dot_nt(qh, k[:, sl]) for k in k_refs]
            m = functools.reduce(jnp.maximum, [jnp.max(s, axis=-1, keepdims=True) for s in ss])
            es = [jnp.exp2(s - m) for s in ss]
            return es, functools.reduce(jnp.add, [jnp.sum(e, axis=-1, keepdims=True) for e in es])

        e1, l1 = numerators(jnp.where(low, q, zero))
        e2, l2 = numerators(jnp.where(low, zero, q))
        c = lam * l1 / l2
        o = functools.reduce(jnp.add, [_dot((a - c * b).astype(BF16), v[:, sl])
                                       for a, b, v in zip(e1, e2, v_refs)])
        o = _rms(o / l1) * g_ref[...] * (1.0 - lam_init)
        o_ref[:, sl] = o.astype(BF16)


def _diff_attention(q, k, v, lam_vecs, g_sub, lam_init):
    n_q = SEQ // TQ_DIFF
    q_map = lambda b, h, i: (b * n_q + i, h)
    ctx_map = lambda b, h, i: (N_LAT // CTX_LEN + b, h)
    lat_map = lambda b, h, i: (b, h)
    const = lambda b, h, i: (0, 0)
    width = LANES * DIFF_HEADS_PER_STEP
    return pl.pallas_call(
        functools.partial(_diff_kernel, lam_init=lam_init),
        grid=(BATCH, DIFF_HEADS // DIFF_HEADS_PER_STEP, n_q),
        in_specs=[
            pl.BlockSpec((TQ_DIFF, width), q_map),
            pl.BlockSpec((CTX_LEN, width), ctx_map),
            pl.BlockSpec((SEQ, width), lat_map),
            pl.BlockSpec((CTX_LEN, width), ctx_map),
            pl.BlockSpec((SEQ, width), lat_map),
            pl.BlockSpec(lam_vecs.shape, const),
            pl.BlockSpec(g_sub.shape, const),
        ],
        out_specs=pl.BlockSpec((TQ_DIFF, width), q_map),
        out_shape=jax.ShapeDtypeStruct((N_LAT, DIFF_HEADS * DIFF_V), BF16),
        compiler_params=_cparams("parallel", "parallel", "parallel"),
        name="diff_attn",
    )(q, k, k, v, v, lam_vecs, g_sub)


def _pack_bf16_pairs(x):
    n = x.shape[1] // 2
    lo = pltpu.bitcast(x[:, :n].astype(BF16).astype(F32), jnp.uint32)
    hi = pltpu.bitcast(x[:, n:].astype(BF16).astype(F32), jnp.uint32)
    return pltpu.bitcast((lo >> 16) | hi, jnp.int32)


def _unpack_bf16_pairs(w):
    u = pltpu.bitcast(w, jnp.uint32)
    lo = pltpu.bitcast(u << 16, F32)
    hi = pltpu.bitcast(u & jnp.uint32(0xFFFF0000), F32)
    return lo, hi


def _route(logits_t, bias, cnt_ref, tri_ref, idx_ref, w_ref, rank_ref):
    tm = logits_t.shape[1]
    scores = jax.nn.sigmoid(logits_t)
    biased = scores + bias
    sub = lax.broadcasted_iota(jnp.int32, (GROUP_SIZE, tm), 0).astype(F32)
    grp_scores, grp_biased, grp_index = [], [], []
    group_score = []
    for g in range(N_GROUPS):
        sl = slice(g * GROUP_SIZE, (g + 1) * GROUP_SIZE)
        bg = biased[sl, :]
        grp_scores.append(scores[sl, :])
        grp_biased.append(bg)
        grp_index.append(sub + float(g * GROUP_SIZE))
        m1 = jnp.max(bg, axis=0, keepdims=True)
        first = jnp.min(jnp.where(bg == m1, sub, float(GROUP_SIZE)), axis=0, keepdims=True)
        m2 = jnp.max(jnp.where(sub == first, -jnp.inf, bg), axis=0, keepdims=True)
        group_score.append(m1 + m2)
    keep = [jnp.zeros((1, tm), F32) for _ in range(N_GROUPS)]
    for _ in range(TOPK_GROUPS):
        m = functools.reduce(jnp.maximum, group_score)
        found = jnp.zeros((1, tm), F32)
        for g in range(N_GROUPS):
            hit = jnp.where(group_score[g] == m, 1.0 - found, 0.0)
            keep[g] = keep[g] + hit
            found = found + hit
            group_score[g] = jnp.where(hit > 0.0, -jnp.inf, group_score[g])
    vals = [jnp.where(keep[g] > 0.0, grp_biased[g], NEG_INF) for g in range(N_GROUPS)]
    chosen = [jnp.zeros((GROUP_SIZE, tm), F32) for _ in range(N_GROUPS)]
    picked = []
    for _ in range(TOP_K):
        m = jnp.max(functools.reduce(jnp.maximum, vals), axis=0, keepdims=True)
        cand = [jnp.where(vals[g] == m, grp_index[g], float(N_EXPERTS)) for g in range(N_GROUPS)]
        ei = jnp.min(functools.reduce(jnp.minimum, cand), axis=0, keepdims=True)
        sel = [grp_index[g] == ei for g in range(N_GROUPS)]
        s_k = functools.reduce(jnp.add, [jnp.where(sel[g], grp_scores[g], 0.0) for g in range(N_GROUPS)])
        picked.append((ei, jnp.sum(s_k, axis=0, keepdims=True)))
        vals = [jnp.where(sel[g], -jnp.inf, vals[g]) for g in range(N_GROUPS)]
        chosen = [jnp.where(sel[g], 1.0, chosen[g]) for g in range(N_GROUPS)]
    total = functools.reduce(jnp.add, [s for _, s in picked])
    chosen_all = jnp.concatenate(chosen, axis=0)
    tri = tri_ref[...]
    width = tri.shape[0]
    cnt = cnt_ref[...]
    parts = []
    for j in range(tm // width):
        ch = chosen_all[:, j * width:(j + 1) * width]
        parts.append(_dot(ch.astype(BF16), tri) + cnt)
        cnt = cnt + jnp.sum(ch, axis=1, keepdims=True)
    cnt_ref[...] = cnt
    rank_all = jnp.concatenate(parts, axis=1)
    for k, (ei, s) in enumerate(picked):
        idx_ref[k:k + 1, :] = ei.astype(jnp.int32)
        w_ref[k:k + 1, :] = s / total * ROUTED_SCALE
        r = functools.reduce(jnp.add, [
            jnp.where(grp_index[g] == ei, rank_all[g * GROUP_SIZE:(g + 1) * GROUP_SIZE, :], 0.0)
            for g in range(N_GROUPS)])
        rank_ref[k:k + 1, :] = jnp.sum(r, axis=0, keepdims=True).astype(jnp.int32)


ROUTE_TN = 2048
TRI_N = 256


def _router_kernel(logits_ref, br_ref, tri_ref, idx_ref, w_ref, rank_ref, cnt_out_ref, cnt_ref):
    @pl.when(pl.program_id(0) == 0)
    def _():
        cnt_ref[...] = jnp.zeros_like(cnt_ref)

    _route(logits_ref[...], br_ref[...], cnt_ref, tri_ref, idx_ref, w_ref, rank_ref)
    cnt_out_ref[...] = jnp.broadcast_to(cnt_ref[...], cnt_out_ref.shape).astype(jnp.int32)


def _router(logits_t, br):
    n_tok = logits_t.shape[1]
    col = lambda i: (0, i)
    const = lambda i: (0, 0)
    tri = (jnp.arange(TRI_N)[:, None] < jnp.arange(TRI_N)[None, :]).astype(BF16)
    blk = pl.BlockSpec((TOP_K, ROUTE_TN), col)
    return pl.pallas_call(
        _router_kernel,
        grid=(n_tok // ROUTE_TN,),
        in_specs=[pl.BlockSpec((N_EXPERTS, ROUTE_TN), col), pl.BlockSpec(br.shape, const),
                  pl.BlockSpec(tri.shape, const)],
        out_specs=[blk, blk, blk, pl.BlockSpec((N_EXPERTS, LANES), const)],
        out_shape=[
            jax.ShapeDtypeStruct((TOP_K, n_tok), jnp.int32),
            jax.ShapeDtypeStruct((TOP_K, n_tok), F32),
            jax.ShapeDtypeStruct((TOP_K, n_tok), jnp.int32),
            jax.ShapeDtypeStruct((N_EXPERTS, LANES), jnp.int32),
        ],
        scratch_shapes=[pltpu.VMEM((N_EXPERTS, 1), F32)],
        compiler_params=_cparams("arbitrary"),
        name="router",
    )(logits_t, br, tri)


def _post_kernel(o_ref, hl_ref, hc_ref, wout_ref, gtm_ref, g_ref, sh_ref, sc_ref, gtf_ref,
                 wrh_ref, wrl_ref, wsg_ref, wsu_ref, wsd_ref, hs_ref, f_ref, logits_ref):
    for c in range(TM // TM_CHUNK):
        rows = slice(c * TM_CHUNK, (c + 1) * TM_CHUNK)
        h1 = _token_rows(hl_ref, hc_ref, rows) + gtm_ref[0] * _dot(o_ref[rows, :], wout_ref[...])
        f = _rms(h1) * g_ref[...]
        f = f * (1.0 + sc_ref[0]) + sh_ref[0]
        f_hi = f.astype(BF16)
        f_lo = (f - f_hi.astype(F32)).astype(BF16)
        f_ref[rows, :] = _pack_bf16_pairs(f)
        logits_ref[:, rows] = (_dot_nt(wrh_ref[...], f_hi) + _dot_nt(wrh_ref[...], f_lo)
                               + _dot_nt(wrl_ref[...], f_hi))
        mid = _silu(_dot(f_hi, wsg_ref[...])) * _dot(f_hi, wsu_ref[...])
        shared = _dot(mid.astype(BF16), wsd_ref[...])
        hs_ref[rows, :] = h1 + gtf_ref[0] * shared


def _post(o, h_lat, h_ctx, wout, gtm, g_ffn, sh, sc, gtf, wrh, wrl, wsg, wsu, wsd, n_rows):
    row = lambda i: (i, 0)
    const = lambda i: (0, 0)
    mod = lambda i: (_mod_index(i), 0, 0)
    vec = pl.BlockSpec((1, 1, D_MODEL), mod)
    full = lambda a: pl.BlockSpec(a.shape, const)
    return pl.pallas_call(
        _post_kernel,
        grid=(n_rows // TM,),
        in_specs=[pl.BlockSpec((TM, D_MODEL), row)] + _token_specs() + [
            full(wout), vec, full(g_ffn), vec, vec, vec,
            full(wrh), full(wrl), full(wsg), full(wsu), full(wsd),
        ],
        out_specs=[
            pl.BlockSpec((TM, D_MODEL), row),
            pl.BlockSpec((TM, D_MODEL // 2), row),
            pl.BlockSpec((N_EXPERTS, TM), lambda i: (0, i)),
        ],
        out_shape=[
            jax.ShapeDtypeStruct((n_rows, D_MODEL), F32),
            jax.ShapeDtypeStruct((n_rows, D_MODEL // 2), jnp.int32),
            jax.ShapeDtypeStruct((N_EXPERTS, n_rows), F32),
        ],
        compiler_params=_cparams("parallel"),
        name="post_attn",
    )(o, h_lat, h_ctx, wout, gtm, g_ffn, sh, sc, gtf, wrh, wrl, wsg, wsu, wsd)


DEST_TN = 2048


def _dest_kernel(ps_ref, idx_ref, rank_ref, o_ref):
    idx = idx_ref[...]
    acc = rank_ref[...]
    for e in range(N_EXPERTS):
        acc = acc + jnp.where(idx == e, ps_ref[e], 0)
    o_ref[...] = acc


def _dest(pad_start, top_idx, rank):
    n_tok = top_idx.shape[1]
    blk = pl.BlockSpec((TOP_K, DEST_TN), lambda i: (0, i))
    return pl.pallas_call(
        _dest_kernel,
        grid=(n_tok // DEST_TN,),
        in_specs=[pl.BlockSpec(memory_space=pltpu.SMEM), blk, blk],
        out_specs=blk,
        out_shape=jax.ShapeDtypeStruct((TOP_K, n_tok), jnp.int32),
        compiler_params=_cparams("parallel"),
        name="dest",
    )(pad_start, top_idx, rank)


SC_CORES = 2
SC_SUBCORES = 16
SC_WORKERS = SC_CORES * SC_SUBCORES
SC_DISPATCH_ROWS = 32
SC_GATHER_ROWS = 64
ROW_WORDS = D_MODEL // 2


def _sc_mesh():
    return plsc.VectorSubcoreMesh(core_axis_name="core", subcore_axis_name="subcore")


def _sc_worker():
    return lax.axis_index("subcore") * SC_CORES + lax.axis_index("core")


def _sc_dispatch(f_packed, dest, n_slots):
    n_tok = f_packed.shape[0]
    per_worker = n_tok // SC_WORKERS
    n_chunks = per_worker // SC_DISPATCH_ROWS
    assert n_chunks * SC_DISPATCH_ROWS * SC_WORKERS == n_tok and n_chunks % 2 == 0
    dest4 = dest.reshape(TOP_K, SC_WORKERS, n_chunks, SC_DISPATCH_ROWS)

    def body(f_hbm, dest_hbm, xs_hbm, rows0, rows1, idx_v, load0, load1, scat0, scat1):
        wid = _sc_worker()
        for k in range(TOP_K):
            pltpu.sync_copy(dest_hbm.at[k, wid], idx_v.at[k])

        def load(c, buf, sem):
            start = pl.multiple_of(wid * per_worker + c * SC_DISPATCH_ROWS, 8)
            return pltpu.make_async_copy(f_hbm.at[pl.ds(start, SC_DISPATCH_ROWS)], buf, sem)

        def scatters(c, buf, sem):
            return [pltpu.make_async_copy(buf, xs_hbm.at[idx_v.at[k, c]], sem) for k in range(TOP_K)]

        load(0, rows0, load0).start()

        @pl.loop(0, n_chunks, step=2)
        def _(c):
            @pl.when(c > 0)
            def _():
                for cp in scatters(c - 1, rows1, scat1):
                    cp.wait()

            load(c + 1, rows1, load1).start()
            load(c, rows0, load0).wait()
            for cp in scatters(c, rows0, scat0):
                cp.start()
            load(c + 1, rows1, load1).wait()
            for cp in scatters(c + 1, rows1, scat1):
                cp.start()
            for cp in scatters(c, rows0, scat0):
                cp.wait()

            @pl.when(c + 2 < n_chunks)
            def _():
                load(c + 2, rows0, load0).start()

        for cp in scatters(n_chunks - 1, rows1, scat1):
            cp.wait()

    rows = pltpu.VMEM((SC_DISPATCH_ROWS, ROW_WORDS), jnp.int32)
    return pl.kernel(
        body,
        out_type=jax.ShapeDtypeStruct((n_slots, ROW_WORDS), jnp.int32),
        mesh=_sc_mesh(),
        scratch_types=[rows, rows, pltpu.VMEM((TOP_K, n_chunks, SC_DISPATCH_ROWS), jnp.int32)]
        + [pltpu.SemaphoreType.DMA] * 4,
        name="sc_dispatch",
    )(f_packed, dest4)


def _sc_gather(ys, dest):
    n_idx = dest.shape[0] * dest.shape[1]
    per_worker = n_idx // SC_WORKERS
    n_chunks = per_worker // SC_GATHER_ROWS
    assert n_chunks * SC_GATHER_ROWS * SC_WORKERS == n_idx and n_chunks % 2 == 0
    dest3 = dest.reshape(SC_WORKERS, n_chunks, SC_GATHER_ROWS)

    def body(ys_hbm, dest_hbm, out_hbm, rows0, rows1, idx_v, gat0, gat1, put0, put1):
        wid = _sc_worker()
        pltpu.sync_copy(dest_hbm.at[wid], idx_v)

        def gather(c, buf, sem):
            return pltpu.make_async_copy(ys_hbm.at[idx_v.at[c]], buf, sem)

        def put(c, buf, sem):
            start = pl.multiple_of(wid * per_worker + c * SC_GATHER_ROWS, 8)
            return pltpu.make_async_copy(buf, out_hbm.at[pl.ds(start, SC_GATHER_ROWS)], sem)

        gather(0, rows0, gat0).start()

        @pl.loop(0, n_chunks, step=2)
        def _(c):
            @pl.when(c > 0)
            def _():
                put(c - 1, rows1, put1).wait()

            gather(c + 1, rows1, gat1).start()
            gather(c, rows0, gat0).wait()
            put(c, rows0, put0).start()
            gather(c + 1, rows1, gat1).wait()
            put(c + 1, rows1, put1).start()
            put(c, rows0, put0).wait()

            @pl.when(c + 2 < n_chunks)
            def _():
                gather(c + 2, rows0, gat0).start()

        put(n_chunks - 1, rows1, put1).wait()

    rows = pltpu.VMEM((SC_GATHER_ROWS, ROW_WORDS), jnp.int32)
    return pl.kernel(
        body,
        out_type=jax.ShapeDtypeStruct((n_idx, ROW_WORDS), jnp.int32),
        mesh=_sc_mesh(),
        scratch_types=[rows, rows, pltpu.VMEM((n_chunks, SC_GATHER_ROWS), jnp.int32)]
        + [pltpu.SemaphoreType.DMA] * 4,
        name="sc_gather",
    )(ys, dest3)


def _moe_kernel(be_ref, nused_ref, slot_ref, next_ref, xa_ref, xb_ref, wg_hbm, wu_hbm, wd_hbm, o_ref,
                wg_s, wu_s, wd_s, wg_f, wu_f, wd_f, sem, *, layer):
    i = pl.program_id(0)
    e = be_ref[i]
    used = i < nused_ref[0]
    first = jnp.logical_and(used, jnp.logical_or(i == 0, e != be_ref[jnp.maximum(i - 1, 0)]))

    def weight_copies(expert, slot):
        return [pltpu.make_async_copy(src.at[layer, expert], dst.at[slot], sem.at[slot, j])
                for j, (src, dst) in enumerate(((wg_hbm, wg_f), (wu_hbm, wu_f), (wd_hbm, wd_f)))]

    @pl.when(i == 0)
    def _():
        for cp in weight_copies(e, slot_ref[e]):
            cp.start()

    @pl.when(first)
    def _():
        slot = slot_ref[e]
        for cp in weight_copies(e, slot):
            cp.wait()
        wg_s[...] = wg_f[slot].astype(BF16)
        wu_s[...] = wu_f[slot].astype(BF16)
        wd_s[...] = wd_f[slot].astype(BF16)
        nxt = next_ref[e]

        @pl.when(nxt >= 0)
        def _():
            for cp in weight_copies(nxt, 1 - slot):
                cp.start()

    @pl.when(used)
    def _():
        q = D_MODEL // 4
        for c in range(MOE_ROWS // MOE_CHUNK):
            rows = slice(c * MOE_CHUNK, (c + 1) * MOE_CHUNK)
            lo_a, hi_a = _unpack_bf16_pairs(xa_ref[rows, :])
            lo_b, hi_b = _unpack_bf16_pairs(xb_ref[rows, :])
            parts = [p.astype(BF16) for p in (lo_a, lo_b, hi_a, hi_b)]
            gate = functools.reduce(jnp.add, [_dot(p, wg_s[j * q:(j + 1) * q, :]) for j, p in enumerate(parts)])
            up = functools.reduce(jnp.add, [_dot(p, wu_s[j * q:(j + 1) * q, :]) for j, p in enumerate(parts)])
            mid = _silu(gate) * up
            o_ref[rows, :] = _pack_bf16_pairs(_dot(mid.astype(BF16), wd_s[...]))


def _moe(block_expert, n_used, slot, next_expert, xs, wg, wu, wd, layer):
    n_blocks = xs.shape[0] // MOE_ROWS
    rows = lambda i, *_: (i, 0)
    hbm = pl.BlockSpec(memory_space=pl.ANY)
    grid_spec = pltpu.PrefetchScalarGridSpec(
        num_scalar_prefetch=4,
        grid=(n_blocks,),
        in_specs=[pl.BlockSpec((MOE_ROWS, ROW_WORDS // 2), lambda i, *_: (i, 0)),
                  pl.BlockSpec((MOE_ROWS, ROW_WORDS // 2), lambda i, *_: (i, 1)), hbm, hbm, hbm],
        out_specs=pl.BlockSpec((MOE_ROWS, ROW_WORDS), rows),
        scratch_shapes=[
            pltpu.VMEM((D_MODEL, EXPERT_FF), BF16),
            pltpu.VMEM((D_MODEL, EXPERT_FF), BF16),
            pltpu.VMEM((EXPERT_FF, D_MODEL), BF16),
            pltpu.VMEM((2, D_MODEL, EXPERT_FF), F32),
            pltpu.VMEM((2, D_MODEL, EXPERT_FF), F32),
            pltpu.VMEM((2, EXPERT_FF, D_MODEL), F32),
            pltpu.SemaphoreType.DMA((2, 3)),
        ],
    )
    return pl.pallas_call(
        functools.partial(_moe_kernel, layer=layer),
        grid_spec=grid_spec,
        out_shape=jax.ShapeDtypeStruct((xs.shape[0], ROW_WORDS), jnp.int32),
        compiler_params=_cparams("arbitrary"),
        name="moe_experts",
    )(block_expert, n_used, slot, next_expert, xs, xs, wg, wu, wd)


def _combine_kernel(y_ref, w_ref, hs_ref, gtf_ref, gfin_ref, o_ref, *, final):
    w = w_ref[...]
    acc_lo = acc_hi = None
    for k in range(TOP_K):
        lo, hi = _unpack_bf16_pairs(y_ref[k])
        wk = w[:, k:k + 1]
        acc_lo = lo * wk if k == 0 else acc_lo + lo * wk
        acc_hi = hi * wk if k == 0 else acc_hi + hi * wk
    out = hs_ref[...] + gtf_ref[0] * jnp.concatenate([acc_lo, acc_hi], axis=1)
    if final:
        out = _rms(out) * gfin_ref[...]
    o_ref[...] = out


def _combine(y, w, hs, gtf, g_final, *, final):
    n_rows = hs.shape[0]
    return pl.pallas_call(
        functools.partial(_combine_kernel, final=final),
        grid=(n_rows // TM_COMB,),
        in_specs=[
            pl.BlockSpec((TOP_K, TM_COMB, ROW_WORDS), lambda i: (0, i, 0)),
            pl.BlockSpec((TM_COMB, TOP_K), lambda i: (i, 0)),
            pl.BlockSpec((TM_COMB, D_MODEL), lambda i: (i, 0)),
            pl.BlockSpec((1, 1, D_MODEL), lambda i: (jnp.minimum(i // (SEQ // TM_COMB), BATCH), 0, 0)),
            pl.BlockSpec((1, D_MODEL), lambda i: (0, 0)),
        ],
        out_specs=pl.BlockSpec((TM_COMB, D_MODEL), lambda i: (i, 0)),
        out_shape=jax.ShapeDtypeStruct((n_rows, D_MODEL), F32),
        compiler_params=_cparams("parallel"),
        name="combine_final" if final else "combine",
    )(y, w, hs, gtf, g_final)


def _rope_tables():
    rows = SEQ // GRID_W

    def angles(rot_dim):
        half = rot_dim // 2
        inv_freq = ROPE_BASE ** (-jnp.arange(0, half, 2, dtype=F32) / half)
        row = jnp.repeat(jnp.arange(rows, dtype=F32), GRID_W)
        col = jnp.tile(jnp.arange(GRID_W, dtype=F32), rows)
        ang_r = row[:, None] * inv_freq
        ang_c = col[:, None] * inv_freq
        return jnp.concatenate([ang_r, ang_r, ang_c, ang_c], axis=-1)

    def signed(sin, quarter):
        sign = jnp.where((jnp.arange(sin.shape[-1]) // quarter) % 2 == 0, -1.0, 1.0)
        return sin * sign

    def with_identity(cos, sin):
        cos = jnp.concatenate([cos, jnp.ones((TM, LANES), F32)], axis=0)
        sin = jnp.concatenate([sin, jnp.zeros((TM, LANES), F32)], axis=0)
        return cos, sin

    a64 = angles(SWA_HEAD_DIM)
    c64 = jnp.tile(jnp.cos(a64), (1, LANES // SWA_HEAD_DIM))
    s64 = jnp.tile(signed(jnp.sin(a64), SWA_HEAD_DIM // 4), (1, LANES // SWA_HEAD_DIM))
    a32 = angles(MLA_ROPE)
    pad_lo = MLA_NOPE
    pad_hi = LANES - MLA_NOPE - MLA_ROPE
    cm = jnp.concatenate([jnp.ones((SEQ, pad_lo), F32), jnp.cos(a32), jnp.ones((SEQ, pad_hi), F32)], axis=-1)
    sm = jnp.concatenate([jnp.zeros((SEQ, pad_lo), F32), signed(jnp.sin(a32), MLA_ROPE // 4),
                          jnp.zeros((SEQ, pad_hi), F32)], axis=-1)
    return with_identity(cm, sm) + with_identity(c64, s64)


def _layer0_weights(wa_in, wa_uq, wa_ukv, wa_out):
    d = D_MODEL
    cq, ckv, kr, qs, ks, vs = jnp.split(
        wa_in, [C_CKV, C_KR, C_KR + MLA_ROPE, C_KR + MLA_ROPE + SWA_HEADS * SWA_HEAD_DIM,
                C_KR + MLA_ROPE + (SWA_HEADS + SWA_KV_HEADS) * SWA_HEAD_DIM], axis=-1)
    kr_pad = jnp.concatenate([jnp.zeros((d, MLA_NOPE), F32), kr,
                              jnp.zeros((d, LANES - MLA_NOPE - MLA_ROPE), F32)], axis=-1)
    qs_pair = qs.reshape(d, SWA_KV_HEADS, SWA_GROUP, SWA_HEAD_DIM).transpose(0, 2, 1, 3).reshape(d, -1)
    win = jnp.concatenate([cq, ckv, kr_pad, qs_pair, ks, vs], axis=-1).astype(BF16)
    uq = wa_uq.reshape(MLA_Q_RANK, MLA_HEADS, MLA_NOPE + MLA_ROPE)
    uq = jnp.pad(uq, ((0, 0), (0, 0), (0, LANES - MLA_NOPE - MLA_ROPE))).reshape(MLA_Q_RANK, -1)
    ukv = wa_ukv.reshape(MLA_KV_RANK, MLA_HEADS, MLA_NOPE + MLA_V)
    uk = jnp.pad(ukv[:, :, :MLA_NOPE], ((0, 0), (0, 0), (0, LANES - MLA_NOPE))).reshape(MLA_KV_RANK, -1)
    uv = ukv[:, :, MLA_NOPE:].reshape(MLA_KV_RANK, -1)
    wukv = jnp.concatenate([uk, uv], axis=-1)
    n_mla = MLA_HEADS * MLA_V
    out_swa = wa_out[n_mla:].reshape(SWA_KV_HEADS, SWA_GROUP, SWA_HEAD_DIM, d).transpose(1, 0, 2, 3)
    wout = jnp.concatenate([wa_out[:n_mla], out_swa.reshape(-1, d)], axis=0)
    return win, uq.astype(BF16), wukv.astype(BF16), wout.astype(BF16)


def _layer1_weights(wc_in):
    per_head = 4 * DIFF_HALF + DIFF_V
    w = wc_in.reshape(D_MODEL, DIFF_HEADS, per_head)
    q = w[:, :, :2 * DIFF_HALF].reshape(D_MODEL, -1)
    k = w[:, :, 2 * DIFF_HALF:4 * DIFF_HALF].reshape(D_MODEL, -1)
    v = w[:, :, 4 * DIFF_HALF:].reshape(D_MODEL, -1)
    return jnp.concatenate([q, k, v], axis=-1).astype(BF16)


def _block_tables(counts, n_blocks):
    padded = (counts + MOE_ROWS - 1) // MOE_ROWS * MOE_ROWS
    pad_end = jnp.cumsum(padded)
    pad_start = (pad_end - padded).astype(jnp.int32)
    block_start = jnp.arange(n_blocks, dtype=jnp.int32) * MOE_ROWS
    block_expert = jnp.minimum((pad_end[None, :] <= block_start[:, None]).sum(axis=1),
                               N_EXPERTS - 1).astype(jnp.int32)
    n_used = (pad_end[-1] // MOE_ROWS).astype(jnp.int32).reshape(1)
    has = padded > 0
    slot = ((jnp.cumsum(has) - 1) & 1).astype(jnp.int32)
    ids = jnp.where(has, jnp.arange(N_EXPERTS, dtype=jnp.int32), N_EXPERTS)
    after = jnp.concatenate([lax.cummin(ids, reverse=True)[1:], jnp.full((1,), N_EXPERTS, jnp.int32)])
    next_expert = jnp.where(after < N_EXPERTS, after, -1).astype(jnp.int32)
    return pad_start, block_expert, n_used, slot, next_expert


def _moe_layer(f_packed, top_idx, top_w, rank, counts, hs, gtf, g_final, wg, wu, wd, layer, *, final):
    n_tok = top_idx.shape[1]
    n_blocks = -(-(n_tok * TOP_K) // MOE_ROWS) + N_EXPERTS
    pad_start, block_expert, n_used, slot, next_expert = _block_tables(counts[:, 0], n_blocks)
    dest = _dest(pad_start, top_idx, rank)
    xs = _sc_dispatch(f_packed, dest, n_blocks * MOE_ROWS)
    ys = _moe(block_expert, n_used, slot, next_expert, xs, wg, wu, wd, layer)
    y = _sc_gather(ys, dest).reshape(TOP_K, n_tok, ROW_WORDS)
    return _combine(y, top_w.T, hs, gtf, g_final, final=final)


def kernel(x, c, ctx, c_ctx, w_ada, b_ada, g_mix, g_ffn, wa_in, ga_q, ga_kv, wa_uq, wa_ukv, wa_sink, wa_out,
           wc_in, lam_q1, lam_k1, lam_q2, lam_k2, gc_sub, wc_out, w_router, b_router, we_gate, we_up, we_down,
           ws_gate, ws_up, ws_down, g_final):
    d = D_MODEL
    x2, ctx2 = x.reshape(N_LAT, d), ctx.reshape(N_CTX, d)
    cc =jnp.concatenate([c, c_ctx[None, :], jnp.zeros((MOD_ROWS - BATCH - 1, d), F32)], axis=0)
    mod = _ada(cc, w_ada, b_ada).reshape(DEPTH, MOD_ROWS, 6, 1, d)

    def mod_vec(layer, j):
        return mod[layer, :, j]

    cm, sm, c64, s64 = _rope_tables()
    g_final2 = g_final.reshape(1, d)

    def router_weights(layer):
        wt = w_router[layer].T
        hi = wt.astype(BF16)
        lo = (wt - hi.astype(F32)).astype(BF16)
        return hi, lo

    def shared_weights(layer):
        return ws_gate[layer].astype(BF16), ws_up[layer].astype(BF16), ws_down[layer].astype(BF16)

    win, wuq, wukv, wout0 = _layer0_weights(wa_in[0], wa_uq[0], wa_ukv[0], wa_out[0])
    qm, km, vm, qs, ks, vs = _proj0(x2, ctx2, g_mix[0].reshape(1, d), mod_vec(0, 0), mod_vec(0, 1), win,
                                    ga_q[0].reshape(1, -1), ga_kv[0].reshape(1, -1), wuq, wukv, cm, sm, c64, s64)
    sink = wa_sink[0].astype(F32)
    o = _mla_attention(qm, km, vm, None, ctx_queries=False)
    o = _mla_attention(qm, km, vm, o, ctx_queries=True)
    o = _swa_attention(sink, qs, ks, vs, o, ctx_queries=False)
    o = _swa_attention(sink, qs, ks, vs, o, ctx_queries=True)
    hs, f, logits_t = _post(
        o, x2, ctx2, wout0, mod_vec(0, 2), g_ffn[0].reshape(1, d), mod_vec(0, 3), mod_vec(0, 4), mod_vec(0, 5),
        *router_weights(0), *shared_weights(0), N_TOK)
    top_idx, top_w, rank, counts = _router(logits_t, b_router[0].reshape(N_EXPERTS, 1))
    h = _moe_layer(f, top_idx, top_w, rank, counts, hs, mod_vec(0, 5), g_final2,
                   we_gate, we_up, we_down, 0, final=False)

    layer = 1
    lam_init = 0.8 - 0.6 * math.exp(-0.3 * layer)
    q1, k1, v1 = _proj1(h, g_mix[1].reshape(1, d), mod_vec(1, 0), mod_vec(1, 1), _layer1_weights(wc_in[0]),
                        c64, s64)
    lam_vecs = jnp.zeros((8, LANES), F32).at[:4, :DIFF_HALF].set(
        jnp.stack([lam_q1[0], lam_k1[0], lam_q2[0], lam_k2[0]]).astype(F32))
    o = _diff_attention(q1, k1, v1, lam_vecs, gc_sub[0].reshape(1, DIFF_V), lam_init)
    hs, f, logits_t = _post(
        o, h, h, wc_out[0].astype(BF16), mod_vec(1, 2), g_ffn[1].reshape(1, d), mod_vec(1, 3), mod_vec(1, 4),
        mod_vec(1, 5), *router_weights(1), *shared_weights(1), N_LAT)
    top_idx, top_w, rank, counts = _router(logits_t, b_router[1].reshape(N_EXPERTS, 1))
    out = _moe_layer(f, top_idx, top_w, rank, counts, hs, mod_vec(1, 5), g_final2,
                     we_gate, we_up, we_down, 1, final=True)
    return out.reshape(BATCH, SEQ, d)
```

```python
import functools
import math

import jax
import jax.numpy as jnp
from jax import lax
from jax.experimental import pallas as pl
from jax.experimental.pallas import tpu as pltpu
from jax.experimental.pallas import tpu_sc as plsc

F32 = jnp.float32
BF16 = jnp.bfloat16

D_MODEL = 1024
BATCH = 8
SEQ = 2048
DEPTH = 2
CTX_LEN = 256
GRID_W = 64
ROPE_BASE = 10000.0
EPS = 1e-6
NEG_INF = -1e30

MLA_HEADS = 8
MLA_Q_RANK = 384
MLA_KV_RANK = 256
MLA_NOPE = 64
MLA_ROPE = 32
MLA_V = 64
SWA_HEADS = 8
SWA_KV_HEADS = 2
SWA_HEAD_DIM = 64
SWA_GROUP = SWA_HEADS // SWA_KV_HEADS
WINDOW = 128
DIFF_HEADS = 8
DIFF_HALF = 64
DIFF_V = 128
N_EXPERTS = 64
N_GROUPS = 8
GROUP_SIZE = N_EXPERTS // N_GROUPS
TOPK_GROUPS = 4
TOP_K = 8
EXPERT_FF = 256
SHARED_FF = 256
ROUTED_SCALE = 2.5

LOG2E = math.log2(math.e)
LANES = 128
N_LAT = BATCH * SEQ
N_CTX = BATCH * CTX_LEN
N_TOK = N_LAT + N_CTX
MOD_ROWS = 16

TM = 512
TM_CHUNK = 256
LAT_BLOCKS_PER_BATCH = SEQ // TM
TQ_MLA = 512
MLA_PAIRS_PER_STEP = 4
TQ_DIFF = 512
DIFF_HEADS_PER_STEP = 4
SWA_BLOCK = 128
SWA_QBLOCKS = 4
MOE_ROWS = 512
MOE_CHUNK = 256
TM_COMB = 256
VMEM_LIMIT = 56 * 1024 * 1024

C_CQ = 0
C_CKV = C_CQ + MLA_Q_RANK
C_KR = C_CKV + MLA_KV_RANK
C_QS = C_KR + LANES
C_KS = C_QS + SWA_HEADS * SWA_HEAD_DIM
C_VS = C_KS + SWA_KV_HEADS * SWA_HEAD_DIM
C_END = C_VS + SWA_KV_HEADS * SWA_HEAD_DIM


def _cparams(*sem):
    return pltpu.CompilerParams(dimension_semantics=sem, vmem_limit_bytes=VMEM_LIMIT)


def _dot(a, b):
    return jnp.dot(a, b, preferred_element_type=F32)


def _dot_nt(a, b):
    return lax.dot_general(a, b, (((1,), (1,)), ((), ())), preferred_element_type=F32)


def _rms(x):
    return x * lax.rsqrt(jnp.mean(x * x, axis=-1, keepdims=True) + EPS)


def _silu(x):
    return x * jax.nn.sigmoid(x)


def _rope(x, cos, sin_signed, shift):
    n = x.shape[-1]
    lane = lax.broadcasted_iota(jnp.int32, x.shape, 1)
    first = (lane & shift) == 0
    rot = jnp.where(first, pltpu.roll(x, n - shift, 1), pltpu.roll(x, shift, 1))
    return x * cos + rot * sin_signed


def _mod_index(i):
    return jnp.minimum(i // LAT_BLOCKS_PER_BATCH, BATCH)


def _rope_index(i):
    return jnp.where(i < N_LAT // TM, i % LAT_BLOCKS_PER_BATCH, LAT_BLOCKS_PER_BATCH)


ADA_TN = 1536


def _ada_kernel(c_ref, w_ref, b_ref, o_ref):
    s = _silu(c_ref[...]).astype(BF16)
    o_ref[0] = _dot(s, w_ref[0].astype(BF16)) + b_ref[0]


def _ada(cc, w_ada, b_ada):
    n_out = w_ada.shape[-1]
    return pl.pallas_call(
        _ada_kernel,
        grid=(DEPTH, n_out // ADA_TN),
        in_specs=[
            pl.BlockSpec((MOD_ROWS, D_MODEL), lambda l, j: (0, 0)),
            pl.BlockSpec((1, D_MODEL, ADA_TN), lambda l, j: (l, 0, j)),
            pl.BlockSpec((1, 1, ADA_TN), lambda l, j: (l, 0, j)),
        ],
        out_specs=pl.BlockSpec((1, MOD_ROWS, ADA_TN), lambda l, j: (l, 0, j)),
        out_shape=jax.ShapeDtypeStruct((DEPTH, MOD_ROWS, n_out), F32),
        compiler_params=_cparams("parallel", "parallel"),
        name="ada",
    )(cc, w_ada, b_ada.reshape(DEPTH, 1, n_out))


def _token_rows(lat_ref, ctx_ref, rows):
    return jnp.where(pl.program_id(0) < N_LAT // TM, lat_ref[rows, :], ctx_ref[rows, :])


def _token_specs():
    lat_blocks = N_LAT // TM
    return [pl.BlockSpec((TM, D_MODEL), lambda i: (jnp.minimum(i, lat_blocks - 1), 0)),
            pl.BlockSpec((TM, D_MODEL), lambda i: (jnp.maximum(i - lat_blocks, 0), 0))]


def _proj0_kernel(x_ref, ctx_ref, g_ref, sh_ref, sc_ref, win_ref, gq_ref, gkv_ref, wuq_ref, wukv_ref,
                  cm_ref, sm_ref, c64_ref, s64_ref,
                  qm_ref, km_ref, vm_ref, qs_ref, ks_ref, vs_ref):
    q_scale = (MLA_NOPE + MLA_ROPE) ** -0.5 * LOG2E
    s_scale = SWA_HEAD_DIM ** -0.5 * LOG2E
    for c in range(TM // TM_CHUNK):
        rows = slice(c * TM_CHUNK, (c + 1) * TM_CHUNK)
        a = _rms(_token_rows(x_ref, ctx_ref, rows)) * g_ref[...]
        a = a * (1.0 + sc_ref[0]) + sh_ref[0]
        p = _dot(a.astype(BF16), win_ref[...])
        nq = _rms(p[:, C_CQ:C_CKV]) * gq_ref[...]
        nkv = _rms(p[:, C_CKV:C_KR]) * gkv_ref[...]
        q = _dot(nq.astype(BF16), wuq_ref[...])
        kv = _dot(nkv.astype(BF16), wukv_ref[...])
        cm, sm = cm_ref[rows, :], sm_ref[rows, :]
        c64, s64 = c64_ref[rows, :], s64_ref[rows, :]
        kr = _rope(p[:, C_KR:C_QS], cm, sm, MLA_ROPE // 4)
        for h in range(MLA_HEADS):
            sl = slice(h * LANES, (h + 1) * LANES)
            qm_ref[rows, sl] = (_rope(q[:, sl], cm, sm, MLA_ROPE // 4) * q_scale).astype(BF16)
            km_ref[rows, sl] = (kv[:, sl] + kr).astype(BF16)
        vm_ref[rows, :] = kv[:, MLA_HEADS * LANES:].astype(BF16)
        for g in range(SWA_GROUP):
            sl = slice(g * LANES, (g + 1) * LANES)
            qs_ref[rows, sl] = (_rope(p[:, C_QS + g * LANES:C_QS + (g + 1) * LANES], c64, s64,
                                      SWA_HEAD_DIM // 4) * s_scale).astype(BF16)
        ks_ref[rows, :] = _rope(p[:, C_KS:C_VS], c64, s64, SWA_HEAD_DIM // 4).astype(BF16)
        vs_ref[rows, :] = p[:, C_VS:C_END].astype(BF16)


def _proj0(x, ctx, g_mix, sh, sc, win, gq, gkv, wuq, wukv, cm, sm, c64, s64):
    row = lambda i: (i, 0)
    const = lambda i: (0, 0)
    mod = lambda i: (_mod_index(i), 0, 0)
    rope = lambda i: (_rope_index(i), 0)
    widths = (MLA_HEADS * LANES, MLA_HEADS * LANES, MLA_HEADS * MLA_V,
              SWA_HEADS * SWA_HEAD_DIM, SWA_KV_HEADS * SWA_HEAD_DIM, SWA_KV_HEADS * SWA_HEAD_DIM)
    return pl.pallas_call(
        _proj0_kernel,
        grid=(N_TOK // TM,),
        in_specs=_token_specs() + [
            pl.BlockSpec((1, D_MODEL), const),
            pl.BlockSpec((1, 1, D_MODEL), mod),
            pl.BlockSpec((1, 1, D_MODEL), mod),
            pl.BlockSpec(win.shape, const),
            pl.BlockSpec(gq.shape, const),
            pl.BlockSpec(gkv.shape, const),
            pl.BlockSpec(wuq.shape, const),
            pl.BlockSpec(wukv.shape, const),
            pl.BlockSpec((TM, LANES), rope),
            pl.BlockSpec((TM, LANES), rope),
            pl.BlockSpec((TM, LANES), rope),
            pl.BlockSpec((TM, LANES), rope),
        ],
        out_specs=[pl.BlockSpec((TM, w), row) for w in widths],
        out_shape=[jax.ShapeDtypeStruct((N_TOK, w), BF16) for w in widths],
        compiler_params=_cparams("parallel"),
        name="proj0",
    )(x, ctx, g_mix, sh, sc, win, gq, gkv, wuq, wukv, cm, sm, c64, s64)


def _proj1_kernel(h_ref, g_ref, sh_ref, sc_ref, w_ref, c64_ref, s64_ref, q_ref, k_ref, v_ref):
    width = DIFF_HEADS * LANES
    scale = DIFF_HALF ** -0.5 * LOG2E
    for c in range(TM // TM_CHUNK):
        rows = slice(c * TM_CHUNK, (c + 1) * TM_CHUNK)
        a = _rms(h_ref[rows, :]) * g_ref[...]
        a = a * (1.0 + sc_ref[0]) + sh_ref[0]
        p = _dot(a.astype(BF16), w_ref[...])
        c64, s64 = c64_ref[rows, :], s64_ref[rows, :]
        for h in range(DIFF_HEADS):
            sl = slice(h * LANES, (h + 1) * LANES)
            q_ref[rows, sl] = (_rope(p[:, sl], c64, s64, DIFF_HALF // 4) * scale).astype(BF16)
            k_ref[rows, sl] = _rope(p[:, width + h * LANES:width + (h + 1) * LANES], c64, s64,
                                    DIFF_HALF // 4).astype(BF16)
        v_ref[rows, :] = p[:, 2 * width:].astype(BF16)


def _proj1(h, g_mix, sh, sc, w, c64, s64):
    row = lambda i: (i, 0)
    const = lambda i: (0, 0)
    mod = lambda i: (_mod_index(i), 0, 0)
    rope = lambda i: (_rope_index(i), 0)
    width = DIFF_HEADS * LANES
    return pl.pallas_call(
        _proj1_kernel,
        grid=(N_TOK // TM,),
        in_specs=[
            pl.BlockSpec((TM, D_MODEL), row),
            pl.BlockSpec((1, D_MODEL), const),
            pl.BlockSpec((1, 1, D_MODEL), mod),
            pl.BlockSpec((1, 1, D_MODEL), mod),
            pl.BlockSpec(w.shape, const),
            pl.BlockSpec((TM, LANES), rope),
            pl.BlockSpec((TM, LANES), rope),
        ],
        out_specs=[pl.BlockSpec((TM, width), row)] * 3,
        out_shape=[jax.ShapeDtypeStruct((N_TOK, width), BF16)] * 3,
        compiler_params=_cparams("parallel"),
        name="proj1",
    )(h, g_mix, sh, sc, w, c64, s64)


def _mla_kernel(*refs, n_kv):
    q_ref = refs[0]
    k_refs = refs[1:1 + n_kv]
    v_refs = refs[1 + n_kv:1 + 2 * n_kv]
    o_ref = refs[1 + 2 * n_kv]
    lane = lax.broadcasted_iota(jnp.int32, (q_ref.shape[0], LANES), 1)
    own = [lane < MLA_V, lane >= MLA_V]
    for pair in range(MLA_PAIRS_PER_STEP):
        vsl = slice(pair * LANES, (pair + 1) * LANES)
        outs = []
        for hh in range(2):
            sl = slice((2 * pair + hh) * LANES, (2 * pair + hh + 1) * LANES)
            qh = q_ref[:, sl]
            ss = [_dot_nt(qh, k[:, sl]) for k in k_refs]
            m = functools.reduce(jnp.maximum, [jnp.max(s, axis=-1, keepdims=True) for s in ss])
            o = None
            for s, v in zip(ss, v_refs):
                vp = v[:, vsl]
                vlane = lax.broadcasted_iota(jnp.int32, vp.shape, 1)
                keep = (vlane < MLA_V) if hh == 0 else (vlane >= MLA_V)
                vh = jnp.where(keep, vp, jnp.ones_like(vp))
                part = _dot(jnp.exp2(s - m).astype(BF16), vh)
                o = part if o is None else o + part
            outs.append(o / pltpu.roll(o, MLA_V, 1))
        o_ref[:, vsl] = jnp.where(own[0], outs[0], outs[1]).astype(BF16)


def _into_buffer(kernel_fn, buf):
    if buf is None:
        return kernel_fn, [], []

    def without_alias_ref(*refs):
        return kernel_fn(*refs[:-2], refs[-1])

    return without_alias_ref, [pl.BlockSpec(memory_space=pl.ANY)], [buf]


def _mla_attention(qm, km, vm, buf, *, ctx_queries):
    groups = MLA_HEADS // 2 // MLA_PAIRS_PER_STEP
    qk_w = 2 * LANES * MLA_PAIRS_PER_STEP
    v_w = LANES * MLA_PAIRS_PER_STEP
    if ctx_queries:
        tq, n_q = CTX_LEN, 1
        q_map = lambda b, h, i: (N_LAT // CTX_LEN + b, h)
        kv_specs = [(CTX_LEN, lambda b, h, i: (N_LAT // CTX_LEN + b, h))]
    else:
        tq, n_q = TQ_MLA, SEQ // TQ_MLA
        q_map = lambda b, h, i: (b * n_q + i, h)
        kv_specs = [(CTX_LEN, lambda b, h, i: (N_LAT // CTX_LEN + b, h)), (SEQ, lambda b, h, i: (b, h))]
    n_kv = len(kv_specs)
    in_specs = [pl.BlockSpec((tq, qk_w), q_map)]
    in_specs += [pl.BlockSpec((n, qk_w), m) for n, m in kv_specs]
    in_specs += [pl.BlockSpec((n, v_w), m) for n, m in kv_specs]
    kernel_fn, extra_specs, extra_args = _into_buffer(functools.partial(_mla_kernel, n_kv=n_kv), buf)
    n_in = len(in_specs)
    return pl.pallas_call(
        kernel_fn,
        grid=(BATCH, groups, n_q),
        in_specs=in_specs + extra_specs,
        out_specs=pl.BlockSpec((tq, v_w), q_map),
        out_shape=jax.ShapeDtypeStruct((N_TOK, D_MODEL), BF16),
        input_output_aliases={n_in: 0} if extra_args else {},
        compiler_params=_cparams("parallel", "parallel", "parallel"),
        name="mla_ctx" if ctx_queries else "mla_lat",
    )(qm, *([km] * n_kv), *([vm] * n_kv), *extra_args)


def _swa_kernel(sink_ref, *refs, band, n_blocks):
    q_ref = refs[0]
    if band:
        n_kv = SWA_QBLOCKS + 2
        k_blocks = refs[1:1 + n_kv]
        kx = refs[1 + n_kv]
        v_blocks = refs[2 + n_kv:2 + 2 * n_kv]
        vx, o_ref = refs[2 + 2 * n_kv], refs[3 + 2 * n_kv]
        chains, rows = SWA_QBLOCKS, SWA_BLOCK
    else:
        kx, vx, o_ref = refs[1:]
        chains, rows = 1, q_ref.shape[0]
    stacked = SWA_GROUP * rows
    k_ctx = kx[...]
    v_ctx = vx[...]
    lane = lax.broadcasted_iota(jnp.int32, (rows, LANES), 1)
    low = lane < SWA_HEAD_DIM
    row_group = lax.broadcasted_iota(jnp.int32, (stacked, 1), 0) // rows
    for c in range(chains):
        qrows = slice(c * rows, (c + 1) * rows)
        if band:
            n = pl.program_id(1) * SWA_QBLOCKS + c
            k_band = jnp.concatenate([k[...] for k in k_blocks[c:c + 3]], axis=0)
            v_band = jnp.concatenate([v[...] for v in v_blocks[c:c + 3]], axis=0)
            qq = lax.broadcasted_iota(jnp.int32, (stacked, 3 * SWA_BLOCK), 0) & (SWA_BLOCK - 1)
            kk = lax.broadcasted_iota(jnp.int32, (stacked, 3 * SWA_BLOCK), 1)
            rel = kk - SWA_BLOCK - qq
            key_pos = (n - 1) * SWA_BLOCK + kk
            valid = (jnp.abs(rel) <= WINDOW) & (key_pos >= 0) & (key_pos < n_blocks * SWA_BLOCK)
        halves = []
        for hk in range(SWA_KV_HEADS):
            keep = low if hk == 0 else jnp.logical_not(low)
            qh = jnp.concatenate(
                [jnp.where(keep, q_ref[qrows, g * LANES:(g + 1) * LANES], jnp.zeros((rows, LANES), BF16))
                 for g in range(SWA_GROUP)], axis=0)
            sink = jnp.zeros((stacked, 1), F32)
            for g in range(SWA_GROUP):
                sink = jnp.where(row_group == g, sink_ref[hk * SWA_GROUP + g] * LOG2E, sink)
            s_ctx = _dot_nt(qh, k_ctx)
            m = jnp.maximum(jnp.max(s_ctx, axis=-1, keepdims=True), sink)
            if band:
                s_band = jnp.where(valid, _dot_nt(qh, k_band), NEG_INF)
                m = jnp.maximum(m, jnp.max(s_band, axis=-1, keepdims=True))

            def ones_other(v, hk=hk):
                vlane = lax.broadcasted_iota(jnp.int32, v.shape, 1)
                own = (vlane < SWA_HEAD_DIM) if hk == 0 else (vlane >= SWA_HEAD_DIM)
                return jnp.where(own, v, jnp.ones_like(v))

            o = _dot(jnp.exp2(s_ctx - m).astype(BF16), ones_other(v_ctx))
            if band:
                o = o + _dot(jnp.exp2(s_band - m).astype(BF16), ones_other(v_band))
            denom = pltpu.roll(o, SWA_HEAD_DIM, 1) + jnp.exp2(sink - m)
            halves.append(o / denom)
        for g in range(SWA_GROUP):
            rs = slice(g * rows, (g + 1) * rows)
            o_ref[qrows, g * LANES:(g + 1) * LANES] = jnp.where(low, halves[0][rs], halves[1][rs]).astype(BF16)


def _swa_attention(sink, qs, ks, vs, buf, *, ctx_queries):
    width = SWA_HEADS * SWA_HEAD_DIM
    kvw = SWA_KV_HEADS * SWA_HEAD_DIM
    ctx_map = lambda b, i: (N_LAT // CTX_LEN + b, 0)
    smem = pl.BlockSpec(memory_space=pltpu.SMEM)
    if ctx_queries:
        grid = (BATCH, 1)
        in_specs = [smem, pl.BlockSpec((CTX_LEN, width), ctx_map),
                    pl.BlockSpec((CTX_LEN, kvw), ctx_map), pl.BlockSpec((CTX_LEN, kvw), ctx_map)]
        args = (sink, qs, ks, vs)
        out_spec = pl.BlockSpec((CTX_LEN, width), lambda b, i: (N_LAT // CTX_LEN + b, 1))
        n_blocks = 1
    else:
        n_blocks = SEQ // SWA_BLOCK
        steps = n_blocks // SWA_QBLOCKS
        grid = (BATCH, steps)
        q_rows = SWA_QBLOCKS * SWA_BLOCK

        def kv_block(offset):
            return lambda b, i: (b * n_blocks + jnp.clip(i * SWA_QBLOCKS + offset, 0, n_blocks - 1), 0)

        band_specs = [pl.BlockSpec((SWA_BLOCK, kvw), kv_block(off)) for off in range(-1, SWA_QBLOCKS + 1)]
        in_specs = ([smem, pl.BlockSpec((q_rows, width), lambda b, i: (b * steps + i, 0))]
                    + band_specs + [pl.BlockSpec((CTX_LEN, kvw), ctx_map)]
                    + band_specs + [pl.BlockSpec((CTX_LEN, kvw), ctx_map)])
        n_band = len(band_specs)
        args = (sink, qs) + (ks,) * (n_band + 1) + (vs,) * (n_band + 1)
        out_spec = pl.BlockSpec((q_rows, width), lambda b, i: (b * steps + i, 1))
    kernel_fn, extra_specs, extra_args = _into_buffer(
        functools.partial(_swa_kernel, band=not ctx_queries, n_blocks=n_blocks), buf)
    return pl.pallas_call(
        kernel_fn,
        grid=grid,
        in_specs=in_specs + extra_specs,
        out_specs=out_spec,
        out_shape=jax.ShapeDtypeStruct((N_TOK, D_MODEL), BF16),
        input_output_aliases={len(in_specs): 0} if extra_args else {},
        compiler_params=_cparams("parallel", "parallel"),
        name="swa_ctx" if ctx_queries else "swa_lat",
    )(*args, *extra_args)


def _diff_kernel(q_ref, kc_ref, kl_ref, vc_ref, vl_ref, lam_ref, g_ref, o_ref, *, lam_init):
    lam = (jnp.exp(jnp.sum(lam_ref[0:1, :] * lam_ref[1:2, :], axis=-1, keepdims=True))
           - jnp.exp(jnp.sum(lam_ref[2:3, :] * lam_ref[3:4, :], axis=-1, keepdims=True)) + lam_init)
    lane = lax.broadcasted_iota(jnp.int32, (q_ref.shape[0], LANES), 1)
    low = lane < DIFF_HALF
    k_refs = (kc_ref, kl_ref)
    v_refs = (vc_ref, vl_ref)
    for h in range(DIFF_HEADS_PER_STEP):
        sl = slice(h * LANES, (h + 1) * LANES)
        q = q_ref[:, sl]
        zero = jnp.zeros_like(q)

        def numerators(qh):
            ss = [_dot_nt(qh, k[:, sl]) for k in k_refs]
            m = functools.reduce(jnp.maximum, [jnp.max(s, axis=-1, keepdims=True) for s in ss])
            es = [jnp.exp2(s - m) for s in ss]
            return es, functools.reduce(jnp.add, [jnp.sum(e, axis=-1, keepdims=True) for e in es])

        e1, l1 = numerators(jnp.where(low, q, zero))
        e2, l2 = numerators(jnp.where(low, zero, q))
        c = lam * l1 / l2
        o = functools.reduce(jnp.add, [_dot((a - c * b).astype(BF16), v[:, sl])
                                       for a, b, v in zip(e1, e2, v_refs)])
        o = _rms(o / l1) * g_ref[...] * (1.0 - lam_init)
        o_ref[:, sl] = o.astype(BF16)


def _diff_attention(q, k, v, lam_vecs, g_sub, lam_init):
    n_q = SEQ // TQ_DIFF
    q_map = lambda b, h, i: (b * n_q + i, h)
    ctx_map = lambda b, h, i: (N_LAT // CTX_LEN + b, h)
    lat_map = lambda b, h, i: (b, h)
    const = lambda b, h, i: (0, 0)
    width = LANES * DIFF_HEADS_PER_STEP
    return pl.pallas_call(
        functools.partial(_diff_kernel, lam_init=lam_init),
        grid=(BATCH, DIFF_HEADS // DIFF_HEADS_PER_STEP, n_q),
        in_specs=[
            pl.BlockSpec((TQ_DIFF, width), q_map),
            pl.BlockSpec((CTX_LEN, width), ctx_map),
            pl.BlockSpec((SEQ, width), lat_map),
            pl.BlockSpec((CTX_LEN, width), ctx_map),
            pl.BlockSpec((SEQ, width), lat_map),
            pl.BlockSpec(lam_vecs.shape, const),
            pl.BlockSpec(g_sub.shape, const),
        ],
        out_specs=pl.BlockSpec((TQ_DIFF, width), q_map),
        out_shape=jax.ShapeDtypeStruct((N_LAT, DIFF_HEADS * DIFF_V), BF16),
        compiler_params=_cparams("parallel", "parallel", "parallel"),
        name="diff_attn",
    )(q, k, k, v, v, lam_vecs, g_sub)


def _pack_bf16_pairs(x):
    n = x.shape[1] // 2
    lo = pltpu.bitcast(x[:, :n].astype(BF16).astype(F32), jnp.uint32)
    hi = pltpu.bitcast(x[:, n:].astype(BF16).astype(F32), jnp.uint32)
    return pltpu.bitcast((lo >> 16) | hi, jnp.int32)


def _unpack_bf16_pairs(w):
    u = pltpu.bitcast(w, jnp.uint32)
    lo = pltpu.bitcast(u << 16, F32)
    hi = pltpu.bitcast(u & jnp.uint32(0xFFFF0000), F32)
    return lo, hi


def _route(logits_t, bias, cnt_ref, tri_ref, idx_ref, w_ref, rank_ref):
    tm = logits_t.shape[1]
    scores = jax.nn.sigmoid(logits_t)
    biased = scores + bias
    sub = lax.broadcasted_iota(jnp.int32, (GROUP_SIZE, tm), 0).astype(F32)
    grp_scores, grp_biased, grp_index = [], [], []
    group_score = []
    for g in range(N_GROUPS):
        sl = slice(g * GROUP_SIZE, (g + 1) * GROUP_SIZE)
        bg = biased[sl, :]
        grp_scores.append(scores[sl, :])
        grp_biased.append(bg)
        grp_index.append(sub + float(g * GROUP_SIZE))
        m1 = jnp.max(bg, axis=0, keepdims=True)
        first = jnp.min(jnp.where(bg == m1, sub, float(GROUP_SIZE)), axis=0, keepdims=True)
        m2 = jnp.max(jnp.where(sub == first, -jnp.inf, bg), axis=0, keepdims=True)
        group_score.append(m1 + m2)
    keep = [jnp.zeros((1, tm), F32) for _ in range(N_GROUPS)]
    for _ in range(TOPK_GROUPS):
        m = functools.reduce(jnp.maximum, group_score)
        found = jnp.zeros((1, tm), F32)
        for g in range(N_GROUPS):
            hit = jnp.where(group_score[g] == m, 1.0 - found, 0.0)
            keep[g] = keep[g] + hit
            found = found + hit
            group_score[g] = jnp.where(hit > 0.0, -jnp.inf, group_score[g])
    vals = [jnp.where(keep[g] > 0.0, grp_biased[g], NEG_INF) for g in range(N_GROUPS)]
    chosen = [jnp.zeros((GROUP_SIZE, tm), F32) for _ in range(N_GROUPS)]
    picked = []
    for _ in range(TOP_K):
        m = jnp.max(functools.reduce(jnp.maximum, vals), axis=0, keepdims=True)
        cand = [jnp.where(vals[g] == m, grp_index[g], float(N_EXPERTS)) for g in range(N_GROUPS)]
        ei = jnp.min(functools.reduce(jnp.minimum, cand), axis=0, keepdims=True)
        sel = [grp_index[g] == ei for g in range(N_GROUPS)]
        s_k = functools.reduce(jnp.add, [jnp.where(sel[g], grp_scores[g], 0.0) for g in range(N_GROUPS)])
        picked.append((ei, jnp.sum(s_k, axis=0, keepdims=True)))
        vals = [jnp.where(sel[g], -jnp.inf, vals[g]) for g in range(N_GROUPS)]
        chosen = [jnp.where(sel[g], 1.0, chosen[g]) for g in range(N_GROUPS)]
    total = functools.reduce(jnp.add, [s for _, s in picked])
    chosen_all = jnp.concatenate(chosen, axis=0)
    tri = tri_ref[...]
    width = tri.shape[0]
    cnt = cnt_ref[...]
    parts = []
    for j in range(tm // width):
        ch = chosen_all[:, j * width:(j + 1) * width]
        parts.append(_dot(ch.astype(BF16), tri) + cnt)
        cnt = cnt + jnp.sum(ch, axis=1, keepdims=True)
    cnt_ref[...] = cnt
    rank_all = jnp.concatenate(parts, axis=1)
    for k, (ei, s) in enumerate(picked):
        idx_ref[k:k + 1, :] = ei.astype(jnp.int32)
        w_ref[k:k + 1, :] = s / total * ROUTED_SCALE
        r = functools.reduce(jnp.add, [
            jnp.where(grp_index[g] == ei, rank_all[g * GROUP_SIZE:(g + 1) * GROUP_SIZE, :], 0.0)
            for g in range(N_GROUPS)])
        rank_ref[k:k + 1, :] = jnp.sum(r, axis=0, keepdims=True).astype(jnp.int32)


ROUTE_TN = 2048
TRI_N = 256


def _router_kernel(logits_ref, br_ref, tri_ref, idx_ref, w_ref, rank_ref, cnt_out_ref, cnt_ref):
    @pl.when(pl.program_id(0) == 0)
    def _():
        cnt_ref[...] = jnp.zeros_like(cnt_ref)

    _route(logits_ref[...], br_ref[...], cnt_ref, tri_ref, idx_ref, w_ref, rank_ref)
    cnt_out_ref[...] = jnp.broadcast_to(cnt_ref[...], cnt_out_ref.shape).astype(jnp.int32)


def _router(logits_t, br):
    n_tok = logits_t.shape[1]
    col = lambda i: (0, i)
    const = lambda i: (0, 0)
    tri = (jnp.arange(TRI_N)[:, None] < jnp.arange(TRI_N)[None, :]).astype(BF16)
    blk = pl.BlockSpec((TOP_K, ROUTE_TN), col)
    return pl.pallas_call(
        _router_kernel,
        grid=(n_tok // ROUTE_TN,),
        in_specs=[pl.BlockSpec((N_EXPERTS, ROUTE_TN), col), pl.BlockSpec(br.shape, const),
                  pl.BlockSpec(tri.shape, const)],
        out_specs=[blk, blk, blk, pl.BlockSpec((N_EXPERTS, LANES), const)],
        out_shape=[
            jax.ShapeDtypeStruct((TOP_K, n_tok), jnp.int32),
            jax.ShapeDtypeStruct((TOP_K, n_tok), F32),
            jax.ShapeDtypeStruct((TOP_K, n_tok), jnp.int32),
            jax.ShapeDtypeStruct((N_EXPERTS, LANES), jnp.int32),
        ],
        scratch_shapes=[pltpu.VMEM((N_EXPERTS, 1), F32)],
        compiler_params=_cparams("arbitrary"),
        name="router",
    )(logits_t, br, tri)


def _post_kernel(o_ref, hl_ref, hc_ref, wout_ref, gtm_ref, g_ref, sh_ref, sc_ref, gtf_ref,
                 wrh_ref, wrl_ref, wsg_ref, wsu_ref, wsd_ref, hs_ref, f_ref, logits_ref):
    for c in range(TM // TM_CHUNK):
        rows = slice(c * TM_CHUNK, (c + 1) * TM_CHUNK)
        h1 = _token_rows(hl_ref, hc_ref, rows) + gtm_ref[0] * _dot(o_ref[rows, :], wout_ref[...])
        f = _rms(h1) * g_ref[...]
        f = f * (1.0 + sc_ref[0]) + sh_ref[0]
        f_hi = f.astype(BF16)
        f_lo = (f - f_hi.astype(F32)).astype(BF16)
        f_ref[rows, :] = _pack_bf16_pairs(f)
        logits_ref[:, rows] = (_dot_nt(wrh_ref[...], f_hi) + _dot_nt(wrh_ref[...], f_lo)
                               + _dot_nt(wrl_ref[...], f_hi))
        mid = _silu(_dot(f_hi, wsg_ref[...])) * _dot(f_hi, wsu_ref[...])
        shared = _dot(mid.astype(BF16), wsd_ref[...])
        hs_ref[rows, :] = h1 + gtf_ref[0] * shared


def _post(o, h_lat, h_ctx, wout, gtm, g_ffn, sh, sc, gtf, wrh, wrl, wsg, wsu, wsd, n_rows):
    row = lambda i: (i, 0)
    const = lambda i: (0, 0)
    mod = lambda i: (_mod_index(i), 0, 0)
    vec = pl.BlockSpec((1, 1, D_MODEL), mod)
    full = lambda a: pl.BlockSpec(a.shape, const)
    return pl.pallas_call(
        _post_kernel,
        grid=(n_rows // TM,),
        in_specs=[pl.BlockSpec((TM, D_MODEL), row)] + _token_specs() + [
            full(wout), vec, full(g_ffn), vec, vec, vec,
            full(wrh), full(wrl), full(wsg), full(wsu), full(wsd),
        ],
        out_specs=[
            pl.BlockSpec((TM, D_MODEL), row),
            pl.BlockSpec((TM, D_MODEL // 2), row),
            pl.BlockSpec((N_EXPERTS, TM), lambda i: (0, i)),
        ],
        out_shape=[
            jax.ShapeDtypeStruct((n_rows, D_MODEL), F32),
            jax.ShapeDtypeStruct((n_rows, D_MODEL // 2), jnp.int32),
            jax.ShapeDtypeStruct((N_EXPERTS, n_rows), F32),
        ],
        compiler_params=_cparams("parallel"),
        name="post_attn",
    )(o, h_lat, h_ctx, wout, gtm, g_ffn, sh, sc, gtf, wrh, wrl, wsg, wsu, wsd)


DEST_TN = 2048


def _dest_kernel(ps_ref, idx_ref, rank_ref, o_ref):
    idx = idx_ref[...]
    acc = rank_ref[...]
    for e in range(N_EXPERTS):
        acc = acc + jnp.where(idx == e, ps_ref[e], 0)
    o_ref[...] = acc


def _dest(pad_start, top_idx, rank):
    n_tok = top_idx.shape[1]
    blk = pl.BlockSpec((TOP_K, DEST_TN), lambda i: (0, i))
    return pl.pallas_call(
        _dest_kernel,
        grid=(n_tok // DEST_TN,),
        in_specs=[pl.BlockSpec(memory_space=pltpu.SMEM), blk, blk],
        out_specs=blk,
        out_shape=jax.ShapeDtypeStruct((TOP_K, n_tok), jnp.int32),
        compiler_params=_cparams("parallel"),
        name="dest",
    )(pad_start, top_idx, rank)


SC_CORES = 2
SC_SUBCORES = 16
SC_WORKERS = SC_CORES * SC_SUBCORES
SC_DISPATCH_ROWS = 32
SC_GATHER_ROWS = 64
ROW_WORDS = D_MODEL // 2


def _sc_mesh():
    return plsc.VectorSubcoreMesh(core_axis_name="core", subcore_axis_name="subcore")


def _sc_worker():
    return lax.axis_index("subcore") * SC_CORES + lax.axis_index("core")


def _sc_dispatch(f_packed, dest, n_slots):
    n_tok = f_packed.shape[0]
    per_worker = n_tok // SC_WORKERS
    n_chunks = per_worker // SC_DISPATCH_ROWS
    assert n_chunks * SC_DISPATCH_ROWS * SC_WORKERS == n_tok and n_chunks % 2 == 0
    dest4 = dest.reshape(TOP_K, SC_WORKERS, n_chunks, SC_DISPATCH_ROWS)

    def body(f_hbm, dest_hbm, xs_hbm, rows0, rows1, idx_v, load0, load1, scat0, scat1):
        wid = _sc_worker()
        for k in range(TOP_K):
            pltpu.sync_copy(dest_hbm.at[k, wid], idx_v.at[k])

        def load(c, buf, sem):
            start = pl.multiple_of(wid * per_worker + c * SC_DISPATCH_ROWS, 8)
            return pltpu.make_async_copy(f_hbm.at[pl.ds(start, SC_DISPATCH_ROWS)], buf, sem)

        def scatters(c, buf, sem):
            return [pltpu.make_async_copy(buf, xs_hbm.at[idx_v.at[k, c]], sem) for k in range(TOP_K)]

        load(0, rows0, load0).start()

        @pl.loop(0, n_chunks, step=2)
        def _(c):
            @pl.when(c > 0)
            def _():
                for cp in scatters(c - 1, rows1, scat1):
                    cp.wait()

            load(c + 1, rows1, load1).start()
            load(c, rows0, load0).wait()
            for cp in scatters(c, rows0, scat0):
                cp.start()
            load(c + 1, rows1, load1).wait()
            for cp in scatters(c + 1, rows1, scat1):
                cp.start()
            for cp in scatters(c, rows0, scat0):
                cp.wait()

            @pl.when(c + 2 < n_chunks)
            def _():
                load(c + 2, rows0, load0).start()

        for cp in scatters(n_chunks - 1, rows1, scat1):
            cp.wait()

    rows = pltpu.VMEM((SC_DISPATCH_ROWS, ROW_WORDS), jnp.int32)
    return pl.kernel(
        body,
        out_type=jax.ShapeDtypeStruct((n_slots, ROW_WORDS), jnp.int32),
        mesh=_sc_mesh(),
        scratch_types=[rows, rows, pltpu.VMEM((TOP_K, n_chunks, SC_DISPATCH_ROWS), jnp.int32)]
        + [pltpu.SemaphoreType.DMA] * 4,
        name="sc_dispatch",
    )(f_packed, dest4)


def _sc_gather(ys, dest):
    n_idx = dest.shape[0] * dest.shape[1]
    per_worker = n_idx // SC_WORKERS
    n_chunks = per_worker // SC_GATHER_ROWS
    assert n_chunks * SC_GATHER_ROWS * SC_WORKERS == n_idx and n_chunks % 2 == 0
    dest3 = dest.reshape(SC_WORKERS, n_chunks, SC_GATHER_ROWS)

    def body(ys_hbm, dest_hbm, out_hbm, rows0, rows1, idx_v, gat0, gat1, put0, put1):
        wid = _sc_worker()
        pltpu.sync_copy(dest_hbm.at[wid], idx_v)

        def gather(c, buf, sem):
            return pltpu.make_async_copy(ys_hbm.at[idx_v.at[c]], buf, sem)

        def put(c, buf, sem):
            start = pl.multiple_of(wid * per_worker + c * SC_GATHER_ROWS, 8)
            return pltpu.make_async_copy(buf, out_hbm.at[pl.ds(start, SC_GATHER_ROWS)], sem)

        gather(0, rows0, gat0).start()

        @pl.loop(0, n_chunks, step=2)
        def _(c):
            @pl.when(c > 0)
            def _():
                put(c - 1, rows1, put1).wait()

            gather(c + 1, rows1, gat1).start()
            gather(c, rows0, gat0).wait()
            put(c, rows0, put0).start()
            gather(c + 1, rows1, gat1).wait()
            put(c + 1, rows1, put1).start()
            put(c, rows0, put0).wait()

            @pl.when(c + 2 < n_chunks)
            def _():
                gather(c + 2, rows0, gat0).start()

        put(n_chunks - 1, rows1, put1).wait()

    rows = pltpu.VMEM((SC_GATHER_ROWS, ROW_WORDS), jnp.int32)
    return pl.kernel(
        body,
        out_type=jax.ShapeDtypeStruct((n_idx, ROW_WORDS), jnp.int32),
        mesh=_sc_mesh(),
        scratch_types=[rows, rows, pltpu.VMEM((n_chunks, SC_GATHER_ROWS), jnp.int32)]
        + [pltpu.SemaphoreType.DMA] * 4,
        name="sc_gather",
    )(ys, dest3)


X_SLOTS = 3
O_SLOTS = 2


def _moe_kernel(be_ref, nused_ref, slot_ref, next_ref, xs_hbm, wg_hbm, wu_hbm, wd_hbm, ys_hbm,
                xbuf, obuf, wg_s, wu_s, wd_s, wg_f, wu_f, wd_f, xsem, osem, wsem, *, layer):
    n_used = nused_ref[0]

    def x_copy(i, s):
        start = pl.multiple_of(i * MOE_ROWS, MOE_ROWS)
        return pltpu.make_async_copy(xs_hbm.at[pl.ds(start, MOE_ROWS)], xbuf.at[s], xsem.at[s])

    def o_copy(i, s):
        start = pl.multiple_of(i * MOE_ROWS, MOE_ROWS)
        return pltpu.make_async_copy(obuf.at[s], ys_hbm.at[pl.ds(start, MOE_ROWS)], osem.at[s])

    def weight_copies(expert, slot):
        return [pltpu.make_async_copy(src.at[layer, expert], dst.at[slot], wsem.at[slot, j])
                for j, (src, dst) in enumerate(((wg_hbm, wg_f), (wu_hbm, wu_f), (wd_hbm, wd_f)))]

    first_expert = be_ref[0]
    for cp in weight_copies(first_expert, slot_ref[first_expert]):
        cp.start()
    for j in range(X_SLOTS - 1):
        @pl.when(j < n_used)
        def _(j=j):
            x_copy(j, j).start()

    def block(i, carry):
        e = be_ref[i]
        xs = lax.rem(i, X_SLOTS)
        os = lax.rem(i, O_SLOTS)

        @pl.when(jnp.logical_or(i == 0, e != be_ref[jnp.maximum(i - 1, 0)]))
        def _():
            slot = slot_ref[e]
            for cp in weight_copies(e, slot):
                cp.wait()
            wg_s[...] = wg_f[slot].astype(BF16)
            wu_s[...] = wu_f[slot].astype(BF16)
            wd_s[...] = wd_f[slot].astype(BF16)
            nxt = next_ref[e]

            @pl.when(nxt >= 0)
            def _():
                for cp in weight_copies(nxt, 1 - slot):
                    cp.start()

        x_copy(i, xs).wait()

        @pl.when(i + X_SLOTS - 1 < n_used)
        def _():
            x_copy(i + X_SLOTS - 1, lax.rem(i + X_SLOTS - 1, X_SLOTS)).start()

        @pl.when(i >= O_SLOTS)
        def _():
            o_copy(i - O_SLOTS, os).wait()

        half = D_MODEL // 2
        for c in range(MOE_ROWS // MOE_CHUNK):
            rows = pl.ds(c * MOE_CHUNK, MOE_CHUNK)
            lo, hi = _unpack_bf16_pairs(xbuf[xs, rows, :])
            lo, hi = lo.astype(BF16), hi.astype(BF16)
            gate = _dot(lo, wg_s[:half, :]) + _dot(hi, wg_s[half:, :])
            up = _dot(lo, wu_s[:half, :]) + _dot(hi, wu_s[half:, :])
            mid = _silu(gate) * up
            obuf[os, rows, :] = _pack_bf16_pairs(_dot(mid.astype(BF16), wd_s[...]))
        o_copy(i, os).start()
        return carry

    lax.fori_loop(0, n_used, block, 0)
    for j in range(O_SLOTS):
        @pl.when(n_used > j)
        def _(j=j):
            i = n_used - 1 - j
            o_copy(i, lax.rem(i, O_SLOTS)).wait()


def _moe(block_expert, n_used, slot, next_expert, xs, wg, wu, wd, layer):
    hbm = pl.BlockSpec(memory_space=pl.ANY)
    block = (MOE_ROWS, ROW_WORDS)
    grid_spec = pltpu.PrefetchScalarGridSpec(
        num_scalar_prefetch=4,
        grid=(1,),
        in_specs=[hbm, hbm, hbm, hbm],
        out_specs=hbm,
        scratch_shapes=[
            pltpu.VMEM((X_SLOTS,) + block, jnp.int32),
            pltpu.VMEM((O_SLOTS,) + block, jnp.int32),
            pltpu.VMEM((D_MODEL, EXPERT_FF), BF16),
            pltpu.VMEM((D_MODEL, EXPERT_FF), BF16),
            pltpu.VMEM((EXPERT_FF, D_MODEL), BF16),
            pltpu.VMEM((2, D_MODEL, EXPERT_FF), F32),
            pltpu.VMEM((2, D_MODEL, EXPERT_FF), F32),
            pltpu.VMEM((2, EXPERT_FF, D_MODEL), F32),
            pltpu.SemaphoreType.DMA((X_SLOTS,)),
            pltpu.SemaphoreType.DMA((O_SLOTS,)),
            pltpu.SemaphoreType.DMA((2, 3)),
        ],
    )
    return pl.pallas_call(
        functools.partial(_moe_kernel, layer=layer),
        grid_spec=grid_spec,
        out_shape=jax.ShapeDtypeStruct((xs.shape[0], ROW_WORDS), jnp.int32),
        compiler_params=_cparams("arbitrary"),
        name="moe_experts",
    )(block_expert, n_used, slot, next_expert, xs, wg, wu, wd)


def _combine_kernel(y_ref, w_ref, hs_ref, gtf_ref, gfin_ref, o_ref, *, final):
    w = w_ref[...]
    acc_lo = acc_hi = None
    for k in range(TOP_K):
        lo, hi = _unpack_bf16_pairs(y_ref[k])
        wk = w[:, k:k + 1]
        acc_lo = lo * wk if k == 0 else acc_lo + lo * wk
        acc_hi = hi * wk if k == 0 else acc_hi + hi * wk
    out = hs_ref[...] + gtf_ref[0] * jnp.concatenate([acc_lo, acc_hi], axis=1)
    if final:
        out = _rms(out) * gfin_ref[...]
    o_ref[...] = out


def _combine(y, w, hs, gtf, g_final, *, final):
    n_rows = hs.shape[0]
    return pl.pallas_call(
        functools.partial(_combine_kernel, final=final),
        grid=(n_rows // TM_COMB,),
        in_specs=[
            pl.BlockSpec((TOP_K, TM_COMB, ROW_WORDS), lambda i: (0, i, 0)),
            pl.BlockSpec((TM_COMB, TOP_K), lambda i: (i, 0)),
            pl.BlockSpec((TM_COMB, D_MODEL), lambda i: (i, 0)),
            pl.BlockSpec((1, 1, D_MODEL), lambda i: (jnp.minimum(i // (SEQ // TM_COMB), BATCH), 0, 0)),
            pl.BlockSpec((1, D_MODEL), lambda i: (0, 0)),
        ],
        out_specs=pl.BlockSpec((TM_COMB, D_MODEL), lambda i: (i, 0)),
        out_shape=jax.ShapeDtypeStruct((n_rows, D_MODEL), F32),
        compiler_params=_cparams("parallel"),
        name="combine_final" if final else "combine",
    )(y, w, hs, gtf, g_final)


def _rope_tables():
    rows = SEQ // GRID_W

    def angles(rot_dim):
        half = rot_dim // 2
        inv_freq = ROPE_BASE ** (-jnp.arange(0, half, 2, dtype=F32) / half)
        row = jnp.repeat(jnp.arange(rows, dtype=F32), GRID_W)
        col = jnp.tile(jnp.arange(GRID_W, dtype=F32), rows)
        ang_r = row[:, None] * inv_freq
        ang_c = col[:, None] * inv_freq
        return jnp.concatenate([ang_r, ang_r, ang_c, ang_c], axis=-1)

    def signed(sin, quarter):
        sign = jnp.where((jnp.arange(sin.shape[-1]) // quarter) % 2 == 0, -1.0, 1.0)
        return sin * sign

    def with_identity(cos, sin):
        cos = jnp.concatenate([cos, jnp.ones((TM, LANES), F32)], axis=0)
        sin = jnp.concatenate([sin, jnp.zeros((TM, LANES), F32)], axis=0)
        return cos, sin

    a64 = angles(SWA_HEAD_DIM)
    c64 = jnp.tile(jnp.cos(a64), (1, LANES // SWA_HEAD_DIM))
    s64 = jnp.tile(signed(jnp.sin(a64), SWA_HEAD_DIM // 4), (1, LANES // SWA_HEAD_DIM))
    a32 = angles(MLA_ROPE)
    pad_lo = MLA_NOPE
    pad_hi = LANES - MLA_NOPE - MLA_ROPE
    cm = jnp.concatenate([jnp.ones((SEQ, pad_lo), F32), jnp.cos(a32), jnp.ones((SEQ, pad_hi), F32)], axis=-1)
    sm = jnp.concatenate([jnp.zeros((SEQ, pad_lo), F32), signed(jnp.sin(a32), MLA_ROPE // 4),
                          jnp.zeros((SEQ, pad_hi), F32)], axis=-1)
    return with_identity(cm, sm) + with_identity(c64, s64)


def _layer0_weights(wa_in, wa_uq, wa_ukv, wa_out):
    d = D_MODEL
    cq, ckv, kr, qs, ks, vs = jnp.split(
        wa_in, [C_CKV, C_KR, C_KR + MLA_ROPE, C_KR + MLA_ROPE + SWA_HEADS * SWA_HEAD_DIM,
                C_KR + MLA_ROPE + (SWA_HEADS + SWA_KV_HEADS) * SWA_HEAD_DIM], axis=-1)
    kr_pad = jnp.concatenate([jnp.zeros((d, MLA_NOPE), F32), kr,
                              jnp.zeros((d, LANES - MLA_NOPE - MLA_ROPE), F32)], axis=-1)
    qs_pair = qs.reshape(d, SWA_KV_HEADS, SWA_GROUP, SWA_HEAD_DIM).transpose(0, 2, 1, 3).reshape(d, -1)
    win = jnp.concatenate([cq, ckv, kr_pad, qs_pair, ks, vs], axis=-1).astype(BF16)
    uq = wa_uq.reshape(MLA_Q_RANK, MLA_HEADS, MLA_NOPE + MLA_ROPE)
    uq = jnp.pad(uq, ((0, 0), (0, 0), (0, LANES - MLA_NOPE - MLA_ROPE))).reshape(MLA_Q_RANK, -1)
    ukv = wa_ukv.reshape(MLA_KV_RANK, MLA_HEADS, MLA_NOPE + MLA_V)
    uk = jnp.pad(ukv[:, :, :MLA_NOPE], ((0, 0), (0, 0), (0, LANES - MLA_NOPE))).reshape(MLA_KV_RANK, -1)
    uv = ukv[:, :, MLA_NOPE:].reshape(MLA_KV_RANK, -1)
    wukv = jnp.concatenate([uk, uv], axis=-1)
    n_mla = MLA_HEADS * MLA_V
    out_swa = wa_out[n_mla:].reshape(SWA_KV_HEADS, SWA_GROUP, SWA_HEAD_DIM, d).transpose(1, 0, 2, 3)
    wout = jnp.concatenate([wa_out[:n_mla], out_swa.reshape(-1, d)], axis=0)
    return win, uq.astype(BF16), wukv.astype(BF16), wout.astype(BF16)


def _layer1_weights(wc_in):
    per_head = 4 * DIFF_HALF + DIFF_V
    w = wc_in.reshape(D_MODEL, DIFF_HEADS, per_head)
    q = w[:, :, :2 * DIFF_HALF].reshape(D_MODEL, -1)
    k = w[:, :, 2 * DIFF_HALF:4 * DIFF_HALF].reshape(D_MODEL, -1)
    v = w[:, :, 4 * DIFF_HALF:].reshape(D_MODEL, -1)
    return jnp.concatenate([q, k, v], axis=-1).astype(BF16)


def _block_tables(counts, n_blocks):
    padded = (counts + MOE_ROWS - 1) // MOE_ROWS * MOE_ROWS
    pad_end = jnp.cumsum(padded)
    pad_start = (pad_end - padded).astype(jnp.int32)
    block_start = jnp.arange(n_blocks, dtype=jnp.int32) * MOE_ROWS
    block_expert = jnp.minimum((pad_end[None, :] <= block_start[:, None]).sum(axis=1),
                               N_EXPERTS - 1).astype(jnp.int32)
    n_used = (pad_end[-1] // MOE_ROWS).astype(jnp.int32).reshape(1)
    has = padded > 0
    slot = ((jnp.cumsum(has) - 1) & 1).astype(jnp.int32)
    ids = jnp.where(has, jnp.arange(N_EXPERTS, dtype=jnp.int32), N_EXPERTS)
    after = jnp.concatenate([lax.cummin(ids, reverse=True)[1:], jnp.full((1,), N_EXPERTS, jnp.int32)])
    next_expert = jnp.where(after < N_EXPERTS, after, -1).astype(jnp.int32)
    return pad_start, block_expert, n_used, slot, next_expert


def _moe_layer(f_packed, top_idx, top_w, rank, counts, hs, gtf, g_final, wg, wu, wd, layer, *, final):
    n_tok = top_idx.shape[1]
    n_blocks = -(-(n_tok * TOP_K) // MOE_ROWS) + N_EXPERTS
    pad_start, block_expert, n_used, slot, next_expert = _block_tables(counts[:, 0], n_blocks)
    dest = _dest(pad_start, top_idx, rank)
    xs = _sc_dispatch(f_packed, dest, n_blocks * MOE_ROWS)
    ys = _moe(block_expert, n_used, slot, next_expert, xs, wg, wu, wd, layer)
    y = _sc_gather(ys, dest).reshape(TOP_K, n_tok, ROW_WORDS)
    return _combine(y, top_w.T, hs, gtf, g_final, final=final)


def kernel(x, c, ctx, c_ctx, w_ada, b_ada, g_mix, g_ffn, wa_in, ga_q, ga_kv, wa_uq, wa_ukv, wa_sink, wa_out,
           wc_in, lam_q1, lam_k1, lam_q2, lam_k2, gc_sub, wc_out, w_router, b_router, we_gate, we_up, we_down,
           ws_gate, ws_up, ws_down, g_final):
    d = D_MODEL
    x2, ctx2 = x.reshape(N_LAT, d), ctx.reshape(N_CTX, d)
    cc =jnp.concatenate([c, c_ctx[None, :], jnp.zeros((MOD_ROWS - BATCH - 1, d), F32)], axis=0)
    mod = _ada(cc, w_ada, b_ada).reshape(DEPTH, MOD_ROWS, 6, 1, d)

    def mod_vec(layer, j):
        return mod[layer, :, j]

    cm, sm, c64, s64 = _rope_tables()
    g_final2 = g_final.reshape(1, d)

    def router_weights(layer):
        wt = w_router[layer].T
        hi = wt.astype(BF16)
        lo = (wt - hi.astype(F32)).astype(BF16)
        return hi, lo

    def shared_weights(layer):
        return ws_gate[layer].astype(BF16), ws_up[layer].astype(BF16), ws_down[layer].astype(BF16)

    win, wuq, wukv, wout0 = _layer0_weights(wa_in[0], wa_uq[0], wa_ukv[0], wa_out[0])
    qm, km, vm, qs, ks, vs = _proj0(x2, ctx2, g_mix[0].reshape(1, d), mod_vec(0, 0), mod_vec(0, 1), win,
                                    ga_q[0].reshape(1, -1), ga_kv[0].reshape(1, -1), wuq, wukv, cm, sm, c64, s64)
    sink = wa_sink[0].astype(F32)
    o = _mla_attention(qm, km, vm, None, ctx_queries=False)
    o = _mla_attention(qm, km, vm, o, ctx_queries=True)
    o = _swa_attention(sink, qs, ks, vs, o, ctx_queries=False)
    o = _swa_attention(sink, qs, ks, vs, o, ctx_queries=True)
    hs, f, logits_t = _post(
        o, x2, ctx2, wout0, mod_vec(0, 2), g_ffn[0].reshape(1, d), mod_vec(0, 3), mod_vec(0, 4), mod_vec(0, 5),
        *router_weights(0), *shared_weights(0), N_TOK)
    top_idx, top_w, rank, counts = _router(logits_t, b_router[0].reshape(N_EXPERTS, 1))
    h = _moe_layer(f, top_idx, top_w, rank, counts, hs, mod_vec(0, 5), g_final2,
                   we_gate, we_up, we_down, 0, final=False)

    layer = 1
    lam_init = 0.8 - 0.6 * math.exp(-0.3 * layer)
    q1, k1, v1 = _proj1(h, g_mix[1].reshape(1, d), mod_vec(1, 0), mod_vec(1, 1), _layer1_weights(wc_in[0]),
                        c64, s64)
    lam_vecs = jnp.zeros((8, LANES), F32).at[:4, :DIFF_HALF].set(
        jnp.stack([lam_q1[0], lam_k1[0], lam_q2[0], lam_k2[0]]).astype(F32))
    o = _diff_attention(q1, k1, v1, lam_vecs, gc_sub[0].reshape(1, DIFF_V), lam_init)
    hs, f, logits_t = _post(
        o, h, h, wc_out[0].astype(BF16), mod_vec(1, 2), g_ffn[1].reshape(1, d), mod_vec(1, 3), mod_vec(1, 4),
        mod_vec(1, 5), *router_weights(1), *shared_weights(1), N_LAT)
    top_idx, top_w, rank, counts = _router(logits_t, b_router[1].reshape(N_EXPERTS, 1))
    out = _moe_layer(f, top_idx, top_w, rank, counts, hs, mod_vec(1, 5), g_final2,
                     we_gate, we_up, we_down, 1, final=True)
    return out.reshape(BATCH, SEQ, d)
```

```python
import functools
import math

import jax
import jax.numpy as jnp
from jax import lax
from jax.experimental import pallas as pl
from jax.experimental.pallas import tpu as pltpu
from jax.experimental.pallas import tpu_sc as plsc

F32 = jnp.float32
BF16 = jnp.bfloat16

D_MODEL = 1024
BATCH = 8
SEQ = 2048
DEPTH = 2
CTX_LEN = 256
GRID_W = 64
ROPE_BASE = 10000.0
EPS = 1e-6
NEG_INF = -1e30

MLA_HEADS = 8
MLA_Q_RANK = 384
MLA_KV_RANK = 256
MLA_NOPE = 64
MLA_ROPE = 32
MLA_V = 64
SWA_HEADS = 8
SWA_KV_HEADS = 2
SWA_HEAD_DIM = 64
SWA_GROUP = SWA_HEADS // SWA_KV_HEADS
WINDOW = 128
DIFF_HEADS = 8
DIFF_HALF = 64
DIFF_V = 128
N_EXPERTS = 64
N_GROUPS = 8
GROUP_SIZE = N_EXPERTS // N_GROUPS
TOPK_GROUPS = 4
TOP_K = 8
EXPERT_FF = 256
SHARED_FF = 256
ROUTED_SCALE = 2.5

LOG2E = math.log2(math.e)
LANES = 128
N_LAT = BATCH * SEQ
N_CTX = BATCH * CTX_LEN
N_TOK = N_LAT + N_CTX
MOD_ROWS = 16

TM = 512
TM_CHUNK = 256
LAT_BLOCKS_PER_BATCH = SEQ // TM
TQ_MLA = 512
MLA_PAIRS_PER_STEP = 4
TQ_DIFF = 256
DIFF_HEADS_PER_STEP = 8
SWA_BLOCK = 128
SWA_QBLOCKS = 4
MOE_ROWS = 512
MOE_CHUNK = 256
TM_COMB = 256
VMEM_LIMIT = 56 * 1024 * 1024

C_CQ = 0
C_CKV = C_CQ + MLA_Q_RANK
C_KR = C_CKV + MLA_KV_RANK
C_QS = C_KR + LANES
C_KS = C_QS + SWA_HEADS * SWA_HEAD_DIM
C_VS = C_KS + SWA_KV_HEADS * SWA_HEAD_DIM
C_END = C_VS + SWA_KV_HEADS * SWA_HEAD_DIM


def _cparams(*sem):
    return pltpu.CompilerParams(dimension_semantics=sem, vmem_limit_bytes=VMEM_LIMIT)


def _dot(a, b):
    return jnp.dot(a, b, preferred_element_type=F32)


def _dot_nt(a, b):
    return lax.dot_general(a, b, (((1,), (1,)), ((), ())), preferred_element_type=F32)


def _rms(x):
    return x * lax.rsqrt(jnp.mean(x * x, axis=-1, keepdims=True) + EPS)


def _silu(x):
    return x * jax.nn.sigmoid(x)


def _rope(x, cos, sin_signed, shift):
    n = x.shape[-1]
    lane = lax.broadcasted_iota(jnp.int32, x.shape, 1)
    first = (lane & shift) == 0
    rot = jnp.where(first, pltpu.roll(x, n - shift, 1), pltpu.roll(x, shift, 1))
    return x * cos + rot * sin_signed


def _mod_index(i):
    return jnp.minimum(i // LAT_BLOCKS_PER_BATCH, BATCH)


def _rope_index(i):
    return jnp.where(i < N_LAT // TM, i % LAT_BLOCKS_PER_BATCH, LAT_BLOCKS_PER_BATCH)


ADA_TN = 1536


def _ada_kernel(c_ref, w_ref, b_ref, o_ref):
    s = _silu(c_ref[...]).astype(BF16)
    o_ref[0] = _dot(s, w_ref[0].astype(BF16)) + b_ref[0]


def _ada(cc, w_ada, b_ada):
    n_out = w_ada.shape[-1]
    return pl.pallas_call(
        _ada_kernel,
        grid=(DEPTH, n_out // ADA_TN),
        in_specs=[
            pl.BlockSpec((MOD_ROWS, D_MODEL), lambda l, j: (0, 0)),
            pl.BlockSpec((1, D_MODEL, ADA_TN), lambda l, j: (l, 0, j)),
            pl.BlockSpec((1, 1, ADA_TN), lambda l, j: (l, 0, j)),
        ],
        out_specs=pl.BlockSpec((1, MOD_ROWS, ADA_TN), lambda l, j: (l, 0, j)),
        out_shape=jax.ShapeDtypeStruct((DEPTH, MOD_ROWS, n_out), F32),
        compiler_params=_cparams("parallel", "parallel"),
        name="ada",
    )(cc, w_ada, b_ada.reshape(DEPTH, 1, n_out))


def _token_rows(lat_ref, ctx_ref, rows):
    return jnp.where(pl.program_id(0) < N_LAT // TM, lat_ref[rows, :], ctx_ref[rows, :])


def _token_specs():
    lat_blocks = N_LAT // TM
    return [pl.BlockSpec((TM, D_MODEL), lambda i: (jnp.minimum(i, lat_blocks - 1), 0)),
            pl.BlockSpec((TM, D_MODEL), lambda i: (jnp.maximum(i - lat_blocks, 0), 0))]


def _proj0_kernel(x_ref, ctx_ref, g_ref, sh_ref, sc_ref, win_ref, gq_ref, gkv_ref, wuq_ref, wukv_ref,
                  cm_ref, sm_ref, c64_ref, s64_ref,
                  qm_ref, km_ref, vm_ref, qs_ref, ks_ref, vs_ref):
    q_scale = (MLA_NOPE + MLA_ROPE) ** -0.5 * LOG2E
    s_scale = SWA_HEAD_DIM ** -0.5 * LOG2E
    for c in range(TM // TM_CHUNK):
        rows = slice(c * TM_CHUNK, (c + 1) * TM_CHUNK)
        a = _rms(_token_rows(x_ref, ctx_ref, rows)) * g_ref[...]
        a = a * (1.0 + sc_ref[0]) + sh_ref[0]
        p = _dot(a.astype(BF16), win_ref[...])
        nq = _rms(p[:, C_CQ:C_CKV]) * gq_ref[...]
        nkv = _rms(p[:, C_CKV:C_KR]) * gkv_ref[...]
        q = _dot(nq.astype(BF16), wuq_ref[...])
        kv = _dot(nkv.astype(BF16), wukv_ref[...])
        cm, sm = cm_ref[rows, :], sm_ref[rows, :]
        c64, s64 = c64_ref[rows, :], s64_ref[rows, :]
        kr = _rope(p[:, C_KR:C_QS], cm, sm, MLA_ROPE // 4)
        for h in range(MLA_HEADS):
            sl = slice(h * LANES, (h + 1) * LANES)
            qm_ref[rows, sl] = (_rope(q[:, sl], cm, sm, MLA_ROPE // 4) * q_scale).astype(BF16)
            km_ref[rows, sl] = (kv[:, sl] + kr).astype(BF16)
        vm_ref[rows, :] = kv[:, MLA_HEADS * LANES:].astype(BF16)
        for g in range(SWA_GROUP):
            sl = slice(g * LANES, (g + 1) * LANES)
            qs_ref[rows, sl] = (_rope(p[:, C_QS + g * LANES:C_QS + (g + 1) * LANES], c64, s64,
                                      SWA_HEAD_DIM // 4) * s_scale).astype(BF16)
        ks_ref[rows, :] = _rope(p[:, C_KS:C_VS], c64, s64, SWA_HEAD_DIM // 4).astype(BF16)
        vs_ref[rows, :] = p[:, C_VS:C_END].astype(BF16)


def _proj0(x, ctx, g_mix, sh, sc, win, gq, gkv, wuq, wukv, cm, sm, c64, s64):
    row = lambda i: (i, 0)
    const = lambda i: (0, 0)
    mod = lambda i: (_mod_index(i), 0, 0)
    rope = lambda i: (_rope_index(i), 0)
    widths = (MLA_HEADS * LANES, MLA_HEADS * LANES, MLA_HEADS * MLA_V,
              SWA_HEADS * SWA_HEAD_DIM, SWA_KV_HEADS * SWA_HEAD_DIM, SWA_KV_HEADS * SWA_HEAD_DIM)
    return pl.pallas_call(
        _proj0_kernel,
        grid=(N_TOK // TM,),
        in_specs=_token_specs() + [
            pl.BlockSpec((1, D_MODEL), const),
            pl.BlockSpec((1, 1, D_MODEL), mod),
            pl.BlockSpec((1, 1, D_MODEL), mod),
            pl.BlockSpec(win.shape, const),
            pl.BlockSpec(gq.shape, const),
            pl.BlockSpec(gkv.shape, const),
            pl.BlockSpec(wuq.shape, const),
            pl.BlockSpec(wukv.shape, const),
            pl.BlockSpec((TM, LANES), rope),
            pl.BlockSpec((TM, LANES), rope),
            pl.BlockSpec((TM, LANES), rope),
            pl.BlockSpec((TM, LANES), rope),
        ],
        out_specs=[pl.BlockSpec((TM, w), row) for w in widths],
        out_shape=[jax.ShapeDtypeStruct((N_TOK, w), BF16) for w in widths],
        compiler_params=_cparams("parallel"),
        name="proj0",
    )(x, ctx, g_mix, sh, sc, win, gq, gkv, wuq, wukv, cm, sm, c64, s64)


def _proj1_kernel(h_ref, g_ref, sh_ref, sc_ref, w_ref, c64_ref, s64_ref, q_ref, k_ref, v_ref):
    width = DIFF_HEADS * LANES
    scale = DIFF_HALF ** -0.5 * LOG2E
    for c in range(TM // TM_CHUNK):
        rows = slice(c * TM_CHUNK, (c + 1) * TM_CHUNK)
        a = _rms(h_ref[rows, :]) * g_ref[...]
        a = a * (1.0 + sc_ref[0]) + sh_ref[0]
        p = _dot(a.astype(BF16), w_ref[...])
        c64, s64 = c64_ref[rows, :], s64_ref[rows, :]
        for h in range(DIFF_HEADS):
            sl = slice(h * LANES, (h + 1) * LANES)
            q_ref[rows, sl] = (_rope(p[:, sl], c64, s64, DIFF_HALF // 4) * scale).astype(BF16)
            k_ref[rows, sl] = _rope(p[:, width + h * LANES:width + (h + 1) * LANES], c64, s64,
                                    DIFF_HALF // 4).astype(BF16)
        v_ref[rows, :] = p[:, 2 * width:].astype(BF16)


def _proj1(h, g_mix, sh, sc, w, c64, s64):
    row = lambda i: (i, 0)
    const = lambda i: (0, 0)
    mod = lambda i: (_mod_index(i), 0, 0)
    rope = lambda i: (_rope_index(i), 0)
    width = DIFF_HEADS * LANES
    return pl.pallas_call(
        _proj1_kernel,
        grid=(N_TOK // TM,),
        in_specs=[
            pl.BlockSpec((TM, D_MODEL), row),
            pl.BlockSpec((1, D_MODEL), const),
            pl.BlockSpec((1, 1, D_MODEL), mod),
            pl.BlockSpec((1, 1, D_MODEL), mod),
            pl.BlockSpec(w.shape, const),
            pl.BlockSpec((TM, LANES), rope),
            pl.BlockSpec((TM, LANES), rope),
        ],
        out_specs=[pl.BlockSpec((TM, width), row)] * 3,
        out_shape=[jax.ShapeDtypeStruct((N_TOK, width), BF16)] * 3,
        compiler_params=_cparams("parallel"),
        name="proj1",
    )(h, g_mix, sh, sc, w, c64, s64)


def _mla_kernel(*refs, n_kv):
    q_ref = refs[0]
    k_refs = refs[1:1 + n_kv]
    v_refs = refs[1 + n_kv:1 + 2 * n_kv]
    o_ref = refs[1 + 2 * n_kv]
    lane = lax.broadcasted_iota(jnp.int32, (q_ref.shape[0], LANES), 1)
    own = [lane < MLA_V, lane >= MLA_V]
    for pair in range(MLA_PAIRS_PER_STEP):
        vsl = slice(pair * LANES, (pair + 1) * LANES)
        outs = []
        for hh in range(2):
            sl = slice((2 * pair + hh) * LANES, (2 * pair + hh + 1) * LANES)
            qh = q_ref[:, sl]
            ss = [_dot_nt(qh, k[:, sl]) for k in k_refs]
            m = functools.reduce(jnp.maximum, [jnp.max(s, axis=-1, keepdims=True) for s in ss])
            o = None
            for s, v in zip(ss, v_refs):
                vp = v[:, vsl]
                vlane = lax.broadcasted_iota(jnp.int32, vp.shape, 1)
                keep = (vlane < MLA_V) if hh == 0 else (vlane >= MLA_V)
                vh = jnp.where(keep, vp, jnp.ones_like(vp))
                part = _dot(jnp.exp2(s - m).astype(BF16), vh)
                o = part if o is None else o + part
            outs.append(o / pltpu.roll(o, MLA_V, 1))
        o_ref[:, vsl] = jnp.where(own[0], outs[0], outs[1]).astype(BF16)


def _into_buffer(kernel_fn, buf):
    if buf is None:
        return kernel_fn, [], []

    def without_alias_ref(*refs):
        return kernel_fn(*refs[:-2], refs[-1])

    return without_alias_ref, [pl.BlockSpec(memory_space=pl.ANY)], [buf]


def _mla_attention(qm, km, vm, buf, *, ctx_queries):
    groups = MLA_HEADS // 2 // MLA_PAIRS_PER_STEP
    qk_w = 2 * LANES * MLA_PAIRS_PER_STEP
    v_w = LANES * MLA_PAIRS_PER_STEP
    if ctx_queries:
        tq, n_q = CTX_LEN, 1
        q_map = lambda b, h, i: (N_LAT // CTX_LEN + b, h)
        kv_specs = [(CTX_LEN, lambda b, h, i: (N_LAT // CTX_LEN + b, h))]
    else:
        tq, n_q = TQ_MLA, SEQ // TQ_MLA
        q_map = lambda b, h, i: (b * n_q + i, h)
        kv_specs = [(CTX_LEN, lambda b, h, i: (N_LAT // CTX_LEN + b, h)), (SEQ, lambda b, h, i: (b, h))]
    n_kv = len(kv_specs)
    in_specs = [pl.BlockSpec((tq, qk_w), q_map)]
    in_specs += [pl.BlockSpec((n, qk_w), m) for n, m in kv_specs]
    in_specs += [pl.BlockSpec((n, v_w), m) for n, m in kv_specs]
    kernel_fn, extra_specs, extra_args = _into_buffer(functools.partial(_mla_kernel, n_kv=n_kv), buf)
    n_in = len(in_specs)
    return pl.pallas_call(
        kernel_fn,
        grid=(BATCH, groups, n_q),
        in_specs=in_specs + extra_specs,
        out_specs=pl.BlockSpec((tq, v_w), q_map),
        out_shape=jax.ShapeDtypeStruct((N_TOK, D_MODEL), BF16),
        input_output_aliases={n_in: 0} if extra_args else {},
        compiler_params=_cparams("parallel", "parallel", "parallel"),
        name="mla_ctx" if ctx_queries else "mla_lat",
    )(qm, *([km] * n_kv), *([vm] * n_kv), *extra_args)


def _swa_kernel(sink_ref, *refs, band, n_blocks):
    q_ref = refs[0]
    if band:
        n_kv = SWA_QBLOCKS + 2
        k_blocks = refs[1:1 + n_kv]
        kx = refs[1 + n_kv]
        v_blocks = refs[2 + n_kv:2 + 2 * n_kv]
        vx, o_ref = refs[2 + 2 * n_kv], refs[3 + 2 * n_kv]
        chains, rows = SWA_QBLOCKS, SWA_BLOCK
    else:
        kx, vx, o_ref = refs[1:]
        chains, rows = 1, q_ref.shape[0]
    stacked = SWA_GROUP * rows
    k_ctx = kx[...]
    v_ctx = vx[...]
    lane = lax.broadcasted_iota(jnp.int32, (rows, LANES), 1)
    low = lane < SWA_HEAD_DIM
    row_group = lax.broadcasted_iota(jnp.int32, (stacked, 1), 0) // rows
    for c in range(chains):
        qrows = slice(c * rows, (c + 1) * rows)
        if band:
            n = pl.program_id(1) * SWA_QBLOCKS + c
            k_band = jnp.concatenate([k[...] for k in k_blocks[c:c + 3]], axis=0)
            v_band = jnp.concatenate([v[...] for v in v_blocks[c:c + 3]], axis=0)
            qq = lax.broadcasted_iota(jnp.int32, (stacked, 3 * SWA_BLOCK), 0) & (SWA_BLOCK - 1)
            kk = lax.broadcasted_iota(jnp.int32, (stacked, 3 * SWA_BLOCK), 1)
            rel = kk - SWA_BLOCK - qq
            key_pos = (n - 1) * SWA_BLOCK + kk
            valid = (jnp.abs(rel) <= WINDOW) & (key_pos >= 0) & (key_pos < n_blocks * SWA_BLOCK)
        halves = []
        for hk in range(SWA_KV_HEADS):
            keep = low if hk == 0 else jnp.logical_not(low)
            qh = jnp.concatenate(
                [jnp.where(keep, q_ref[qrows, g * LANES:(g + 1) * LANES], jnp.zeros((rows, LANES), BF16))
                 for g in range(SWA_GROUP)], axis=0)
            sink = jnp.zeros((stacked, 1), F32)
            for g in range(SWA_GROUP):
                sink = jnp.where(row_group == g, sink_ref[hk * SWA_GROUP + g] * LOG2E, sink)
            s_ctx = _dot_nt(qh, k_ctx)
            m = jnp.maximum(jnp.max(s_ctx, axis=-1, keepdims=True), sink)
            if band:
                s_band = jnp.where(valid, _dot_nt(qh, k_band), NEG_INF)
                m = jnp.maximum(m, jnp.max(s_band, axis=-1, keepdims=True))

            def ones_other(v, hk=hk):
                vlane = lax.broadcasted_iota(jnp.int32, v.shape, 1)
                own = (vlane < SWA_HEAD_DIM) if hk == 0 else (vlane >= SWA_HEAD_DIM)
                return jnp.where(own, v, jnp.ones_like(v))

            o = _dot(jnp.exp2(s_ctx - m).astype(BF16), ones_other(v_ctx))
            if band:
                o = o + _dot(jnp.exp2(s_band - m).astype(BF16), ones_other(v_band))
            denom = pltpu.roll(o, SWA_HEAD_DIM, 1) + jnp.exp2(sink - m)
            halves.append(o / denom)
        for g in range(SWA_GROUP):
            rs = slice(g * rows, (g + 1) * rows)
            o_ref[qrows, g * LANES:(g + 1) * LANES] = jnp.where(low, halves[0][rs], halves[1][rs]).astype(BF16)


def _swa_attention(sink, qs, ks, vs, buf, *, ctx_queries):
    width = SWA_HEADS * SWA_HEAD_DIM
    kvw = SWA_KV_HEADS * SWA_HEAD_DIM
    ctx_map = lambda b, i: (N_LAT // CTX_LEN + b, 0)
    smem = pl.BlockSpec(memory_space=pltpu.SMEM)
    if ctx_queries:
        grid = (BATCH, 1)
        in_specs = [smem, pl.BlockSpec((CTX_LEN, width), ctx_map),
                    pl.BlockSpec((CTX_LEN, kvw), ctx_map), pl.BlockSpec((CTX_LEN, kvw), ctx_map)]
        args = (sink, qs, ks, vs)
        out_spec = pl.BlockSpec((CTX_LEN, width), lambda b, i: (N_LAT // CTX_LEN + b, 1))
        n_blocks = 1
    else:
        n_blocks = SEQ // SWA_BLOCK
        steps = n_blocks // SWA_QBLOCKS
        grid = (BATCH, steps)
        q_rows = SWA_QBLOCKS * SWA_BLOCK

        def kv_block(offset):
            return lambda b, i: (b * n_blocks + jnp.clip(i * SWA_QBLOCKS + offset, 0, n_blocks - 1), 0)

        band_specs = [pl.BlockSpec((SWA_BLOCK, kvw), kv_block(off)) for off in range(-1, SWA_QBLOCKS + 1)]
        in_specs = ([smem, pl.BlockSpec((q_rows, width), lambda b, i: (b * steps + i, 0))]
                    + band_specs + [pl.BlockSpec((CTX_LEN, kvw), ctx_map)]
                    + band_specs + [pl.BlockSpec((CTX_LEN, kvw), ctx_map)])
        n_band = len(band_specs)
        args = (sink, qs) + (ks,) * (n_band + 1) + (vs,) * (n_band + 1)
        out_spec = pl.BlockSpec((q_rows, width), lambda b, i: (b * steps + i, 1))
    kernel_fn, extra_specs, extra_args = _into_buffer(
        functools.partial(_swa_kernel, band=not ctx_queries, n_blocks=n_blocks), buf)
    return pl.pallas_call(
        kernel_fn,
        grid=grid,
        in_specs=in_specs + extra_specs,
        out_specs=out_spec,
        out_shape=jax.ShapeDtypeStruct((N_TOK, D_MODEL), BF16),
        input_output_aliases={len(in_specs): 0} if extra_args else {},
        compiler_params=_cparams("parallel", "parallel"),
        name="swa_ctx" if ctx_queries else "swa_lat",
    )(*args, *extra_args)


def _diff_kernel(q_ref, kc_ref, kl_ref, vc_ref, vl_ref, lam_ref, g_ref, o_ref, *, lam_init):
    lam = (jnp.exp(jnp.sum(lam_ref[0:1, :] * lam_ref[1:2, :], axis=-1, keepdims=True))
           - jnp.exp(jnp.sum(lam_ref[2:3, :] * lam_ref[3:4, :], axis=-1, keepdims=True)) + lam_init)
    lane = lax.broadcasted_iota(jnp.int32, (q_ref.shape[0], LANES), 1)
    low = lane < DIFF_HALF
    k_refs = (kc_ref, kl_ref)
    v_refs = (vc_ref, vl_ref)
    for h in range(DIFF_HEADS_PER_STEP):
        sl = slice(h * LANES, (h + 1) * LANES)
        q = q_ref[:, sl]
        zero = jnp.zeros_like(q)

        vx = [jnp.concatenate([v[:, sl], jnp.ones((v.shape[0], LANES), BF16)], axis=1) for v in v_refs]

        def attend(qh):
            ss = [_dot_nt(qh, k[:, sl]) for k in k_refs]
            m = functools.reduce(jnp.maximum, [jnp.max(s, axis=-1, keepdims=True) for s in ss])
            ox = functools.reduce(jnp.add, [_dot(jnp.exp2(s - m).astype(BF16), v) for s, v in zip(ss, vx)])
            return ox[:, :LANES] / ox[:, LANES:]

        o = attend(jnp.where(low, q, zero)) - lam * attend(jnp.where(low, zero, q))
        o = _rms(o) * g_ref[...] * (1.0 - lam_init)
        o_ref[:, sl] = o.astype(BF16)


def _diff_attention(q, k, v, lam_vecs, g_sub, lam_init):
    n_q = SEQ // TQ_DIFF
    q_map = lambda b, h, i: (b * n_q + i, h)
    ctx_map = lambda b, h, i: (N_LAT // CTX_LEN + b, h)
    lat_map = lambda b, h, i: (b, h)
    const = lambda b, h, i: (0, 0)
    width = LANES * DIFF_HEADS_PER_STEP
    return pl.pallas_call(
        functools.partial(_diff_kernel, lam_init=lam_init),
        grid=(BATCH, DIFF_HEADS // DIFF_HEADS_PER_STEP, n_q),
        in_specs=[
            pl.BlockSpec((TQ_DIFF, width), q_map),
            pl.BlockSpec((CTX_LEN, width), ctx_map),
            pl.BlockSpec((SEQ, width), lat_map),
            pl.BlockSpec((CTX_LEN, width), ctx_map),
            pl.BlockSpec((SEQ, width), lat_map),
            pl.BlockSpec(lam_vecs.shape, const),
            pl.BlockSpec(g_sub.shape, const),
        ],
        out_specs=pl.BlockSpec((TQ_DIFF, width), q_map),
        out_shape=jax.ShapeDtypeStruct((N_LAT, DIFF_HEADS * DIFF_V), BF16),
        compiler_params=_cparams("parallel", "parallel", "parallel"),
        name="diff_attn",
    )(q, k, k, v, v, lam_vecs, g_sub)


def _pack_bf16_pairs(x):
    n = x.shape[1] // 2
    lo = pltpu.bitcast(x[:, :n].astype(BF16).astype(F32), jnp.uint32)
    hi = pltpu.bitcast(x[:, n:].astype(BF16).astype(F32), jnp.uint32)
    return pltpu.bitcast((lo >> 16) | hi, jnp.int32)


def _unpack_bf16_pairs(w):
    u = pltpu.bitcast(w, jnp.uint32)
    lo = pltpu.bitcast(u << 16, F32)
    hi = pltpu.bitcast(u & jnp.uint32(0xFFFF0000), F32)
    return lo, hi


def _route(logits_t, bias, cnt_ref, tri_ref, idx_ref, w_ref, rank_ref):
    tm = logits_t.shape[1]
    scores = jax.nn.sigmoid(logits_t)
    biased = scores + bias
    sub = lax.broadcasted_iota(jnp.int32, (GROUP_SIZE, tm), 0).astype(F32)
    grp_scores, grp_biased, grp_index = [], [], []
    group_score = []
    for g in range(N_GROUPS):
        sl = slice(g * GROUP_SIZE, (g + 1) * GROUP_SIZE)
        bg = biased[sl, :]
        grp_scores.append(scores[sl, :])
        grp_biased.append(bg)
        grp_index.append(sub + float(g * GROUP_SIZE))
        m1 = jnp.max(bg, axis=0, keepdims=True)
        first = jnp.min(jnp.where(bg == m1, sub, float(GROUP_SIZE)), axis=0, keepdims=True)
        m2 = jnp.max(jnp.where(sub == first, -jnp.inf, bg), axis=0, keepdims=True)
        group_score.append(m1 + m2)
    keep = [jnp.zeros((1, tm), F32) for _ in range(N_GROUPS)]
    for _ in range(TOPK_GROUPS):
        m = functools.reduce(jnp.maximum, group_score)
        found = jnp.zeros((1, tm), F32)
        for g in range(N_GROUPS):
            hit = jnp.where(group_score[g] == m, 1.0 - found, 0.0)
            keep[g] = keep[g] + hit
            found = found + hit
            group_score[g] = jnp.where(hit > 0.0, -jnp.inf, group_score[g])
    vals = [jnp.where(keep[g] > 0.0, grp_biased[g], NEG_INF) for g in range(N_GROUPS)]
    chosen = [jnp.zeros((GROUP_SIZE, tm), F32) for _ in range(N_GROUPS)]
    picked = []
    for _ in range(TOP_K):
        m = jnp.max(functools.reduce(jnp.maximum, vals), axis=0, keepdims=True)
        cand = [jnp.where(vals[g] == m, grp_index[g], float(N_EXPERTS)) for g in range(N_GROUPS)]
        ei = jnp.min(functools.reduce(jnp.minimum, cand), axis=0, keepdims=True)
        sel = [grp_index[g] == ei for g in range(N_GROUPS)]
        s_k = functools.reduce(jnp.add, [jnp.where(sel[g], grp_scores[g], 0.0) for g in range(N_GROUPS)])
        picked.append((ei, jnp.sum(s_k, axis=0, keepdims=True)))
        vals = [jnp.where(sel[g], -jnp.inf, vals[g]) for g in range(N_GROUPS)]
        chosen = [jnp.where(sel[g], 1.0, chosen[g]) for g in range(N_GROUPS)]
    total = functools.reduce(jnp.add, [s for _, s in picked])
    chosen_all = jnp.concatenate(chosen, axis=0)
    tri = tri_ref[...]
    width = tri.shape[0]
    cnt = cnt_ref[...]
    parts = []
    for j in range(tm // width):
        ch = chosen_all[:, j * width:(j + 1) * width]
        parts.append(_dot(ch.astype(BF16), tri) + cnt)
        cnt = cnt + jnp.sum(ch, axis=1, keepdims=True)
    cnt_ref[...] = cnt
    rank_all = jnp.concatenate(parts, axis=1)
    for k, (ei, s) in enumerate(picked):
        idx_ref[k:k + 1, :] = ei.astype(jnp.int32)
        w_ref[k:k + 1, :] = s / total * ROUTED_SCALE
        r = functools.reduce(jnp.add, [
            jnp.where(grp_index[g] == ei, rank_all[g * GROUP_SIZE:(g + 1) * GROUP_SIZE, :], 0.0)
            for g in range(N_GROUPS)])
        rank_ref[k:k + 1, :] = jnp.sum(r, axis=0, keepdims=True).astype(jnp.int32)


ROUTE_TN = 2048
TRI_N = 256


def _router_kernel(logits_ref, br_ref, tri_ref, idx_ref, w_ref, rank_ref, cnt_out_ref, cnt_ref):
    @pl.when(pl.program_id(0) == 0)
    def _():
        cnt_ref[...] = jnp.zeros_like(cnt_ref)

    _route(logits_ref[...], br_ref[...], cnt_ref, tri_ref, idx_ref, w_ref, rank_ref)
    cnt_out_ref[...] = jnp.broadcast_to(cnt_ref[...], cnt_out_ref.shape).astype(jnp.int32)


def _router(logits_t, br):
    n_tok = logits_t.shape[1]
    col = lambda i: (0, i)
    const = lambda i: (0, 0)
    tri = (jnp.arange(TRI_N)[:, None] < jnp.arange(TRI_N)[None, :]).astype(BF16)
    blk = pl.BlockSpec((TOP_K, ROUTE_TN), col)
    return pl.pallas_call(
        _router_kernel,
        grid=(n_tok // ROUTE_TN,),
        in_specs=[pl.BlockSpec((N_EXPERTS, ROUTE_TN), col), pl.BlockSpec(br.shape, const),
                  pl.BlockSpec(tri.shape, const)],
        out_specs=[blk, blk, blk, pl.BlockSpec((N_EXPERTS, LANES), const)],
        out_shape=[
            jax.ShapeDtypeStruct((TOP_K, n_tok), jnp.int32),
            jax.ShapeDtypeStruct((TOP_K, n_tok), F32),
            jax.ShapeDtypeStruct((TOP_K, n_tok), jnp.int32),
            jax.ShapeDtypeStruct((N_EXPERTS, LANES), jnp.int32),
        ],
        scratch_shapes=[pltpu.VMEM((N_EXPERTS, 1), F32)],
        compiler_params=_cparams("arbitrary"),
        name="router",
    )(logits_t, br, tri)


def _post_kernel(o_ref, hl_ref, hc_ref, wout_ref, gtm_ref, g_ref, sh_ref, sc_ref, gtf_ref,
                 wrh_ref, wrl_ref, wsg_ref, wsu_ref, wsd_ref, hs_ref, f_ref, logits_ref):
    for c in range(TM // TM_CHUNK):
        rows = slice(c * TM_CHUNK, (c + 1) * TM_CHUNK)
        h1 = _token_rows(hl_ref, hc_ref, rows) + gtm_ref[0] * _dot(o_ref[rows, :], wout_ref[...])
        f = _rms(h1) * g_ref[...]
        f = f * (1.0 + sc_ref[0]) + sh_ref[0]
        f_hi = f.astype(BF16)
        f_lo = (f - f_hi.astype(F32)).astype(BF16)
        f_ref[rows, :] = _pack_bf16_pairs(f)
        logits_ref[:, rows] = (_dot_nt(wrh_ref[...], f_hi) + _dot_nt(wrh_ref[...], f_lo)
                               + _dot_nt(wrl_ref[...], f_hi))
        mid = _silu(_dot(f_hi, wsg_ref[...])) * _dot(f_hi, wsu_ref[...])
        shared = _dot(mid.astype(BF16), wsd_ref[...])
        hs_ref[rows, :] = h1 + gtf_ref[0] * shared


def _post(o, h_lat, h_ctx, wout, gtm, g_ffn, sh, sc, gtf, wrh, wrl, wsg, wsu, wsd, n_rows):
    row = lambda i: (i, 0)
    const = lambda i: (0, 0)
    mod = lambda i: (_mod_index(i), 0, 0)
    vec = pl.BlockSpec((1, 1, D_MODEL), mod)
    full = lambda a: pl.BlockSpec(a.shape, const)
    return pl.pallas_call(
        _post_kernel,
        grid=(n_rows // TM,),
        in_specs=[pl.BlockSpec((TM, D_MODEL), row)] + _token_specs() + [
            full(wout), vec, full(g_ffn), vec, vec, vec,
            full(wrh), full(wrl), full(wsg), full(wsu), full(wsd),
        ],
        out_specs=[
            pl.BlockSpec((TM, D_MODEL), row),
            pl.BlockSpec((TM, D_MODEL // 2), row),
            pl.BlockSpec((N_EXPERTS, TM), lambda i: (0, i)),
        ],
        out_shape=[
            jax.ShapeDtypeStruct((n_rows, D_MODEL), F32),
            jax.ShapeDtypeStruct((n_rows, D_MODEL // 2), jnp.int32),
            jax.ShapeDtypeStruct((N_EXPERTS, n_rows), F32),
        ],
        compiler_params=_cparams("parallel"),
        name="post_attn",
    )(o, h_lat, h_ctx, wout, gtm, g_ffn, sh, sc, gtf, wrh, wrl, wsg, wsu, wsd)


DEST_TN = 2048


def _dest_kernel(ps_ref, idx_ref, rank_ref, o_ref):
    idx = idx_ref[...]
    acc = rank_ref[...]
    for e in range(N_EXPERTS):
        acc = acc + jnp.where(idx == e, ps_ref[e], 0)
    o_ref[...] = acc


def _dest(pad_start, top_idx, rank):
    n_tok = top_idx.shape[1]
    blk = pl.BlockSpec((TOP_K, DEST_TN), lambda i: (0, i))
    return pl.pallas_call(
        _dest_kernel,
        grid=(n_tok // DEST_TN,),
        in_specs=[pl.BlockSpec(memory_space=pltpu.SMEM), blk, blk],
        out_specs=blk,
        out_shape=jax.ShapeDtypeStruct((TOP_K, n_tok), jnp.int32),
        compiler_params=_cparams("parallel"),
        name="dest",
    )(pad_start, top_idx, rank)


SC_CORES = 2
SC_SUBCORES = 16
SC_WORKERS = SC_CORES * SC_SUBCORES
SC_DISPATCH_ROWS = 32
SC_GATHER_ROWS = 64
ROW_WORDS = D_MODEL // 2


def _sc_mesh():
    return plsc.VectorSubcoreMesh(core_axis_name="core", subcore_axis_name="subcore")


def _sc_worker():
    return lax.axis_index("subcore") * SC_CORES + lax.axis_index("core")


def _sc_dispatch(f_packed, dest, n_slots):
    n_tok = f_packed.shape[0]
    per_worker = n_tok // SC_WORKERS
    n_chunks = per_worker // SC_DISPATCH_ROWS
    assert n_chunks * SC_DISPATCH_ROWS * SC_WORKERS == n_tok and n_chunks % 2 == 0
    dest4 = dest.reshape(TOP_K, SC_WORKERS, n_chunks, SC_DISPATCH_ROWS)

    def body(f_hbm, dest_hbm, xs_hbm, rows0, rows1, idx_v, load0, load1, scat0, scat1):
        wid = _sc_worker()
        for k in range(TOP_K):
            pltpu.sync_copy(dest_hbm.at[k, wid], idx_v.at[k])

        def load(c, buf, sem):
            start = pl.multiple_of(wid * per_worker + c * SC_DISPATCH_ROWS, 8)
            return pltpu.make_async_copy(f_hbm.at[pl.ds(start, SC_DISPATCH_ROWS)], buf, sem)

        def scatters(c, buf, sem):
            return [pltpu.make_async_copy(buf, xs_hbm.at[idx_v.at[k, c]], sem) for k in range(TOP_K)]

        load(0, rows0, load0).start()

        @pl.loop(0, n_chunks, step=2)
        def _(c):
            @pl.when(c > 0)
            def _():
                for cp in scatters(c - 1, rows1, scat1):
                    cp.wait()

            load(c + 1, rows1, load1).start()
            load(c, rows0, load0).wait()
            for cp in scatters(c, rows0, scat0):
                cp.start()
            load(c + 1, rows1, load1).wait()
            for cp in scatters(c + 1, rows1, scat1):
                cp.start()
            for cp in scatters(c, rows0, scat0):
                cp.wait()

            @pl.when(c + 2 < n_chunks)
            def _():
                load(c + 2, rows0, load0).start()

        for cp in scatters(n_chunks - 1, rows1, scat1):
            cp.wait()

    rows = pltpu.VMEM((SC_DISPATCH_ROWS, ROW_WORDS), jnp.int32)
    return pl.kernel(
        body,
        out_type=jax.ShapeDtypeStruct((n_slots, ROW_WORDS), jnp.int32),
        mesh=_sc_mesh(),
        scratch_types=[rows, rows, pltpu.VMEM((TOP_K, n_chunks, SC_DISPATCH_ROWS), jnp.int32)]
        + [pltpu.SemaphoreType.DMA] * 4,
        name="sc_dispatch",
    )(f_packed, dest4)


def _sc_gather(ys, dest):
    n_idx = dest.shape[0] * dest.shape[1]
    per_worker = n_idx // SC_WORKERS
    n_chunks = per_worker // SC_GATHER_ROWS
    assert n_chunks * SC_GATHER_ROWS * SC_WORKERS == n_idx and n_chunks % 2 == 0
    dest3 = dest.reshape(SC_WORKERS, n_chunks, SC_GATHER_ROWS)

    def body(ys_hbm, dest_hbm, out_hbm, rows0, rows1, idx_v, gat0, gat1, put0, put1):
        wid = _sc_worker()
        pltpu.sync_copy(dest_hbm.at[wid], idx_v)

        def gather(c, buf, sem):
            return pltpu.make_async_copy(ys_hbm.at[idx_v.at[c]], buf, sem)

        def put(c, buf, sem):
            start = pl.multiple_of(wid * per_worker + c * SC_GATHER_ROWS, 8)
            return pltpu.make_async_copy(buf, out_hbm.at[pl.ds(start, SC_GATHER_ROWS)], sem)

        gather(0, rows0, gat0).start()

        @pl.loop(0, n_chunks, step=2)
        def _(c):
            @pl.when(c > 0)
            def _():
                put(c - 1, rows1, put1).wait()

            gather(c + 1, rows1, gat1).start()
            gather(c, rows0, gat0).wait()
            put(c, rows0, put0).start()
            gather(c + 1, rows1, gat1).wait()
            put(c + 1, rows1, put1).start()
            put(c, rows0, put0).wait()

            @pl.when(c + 2 < n_chunks)
            def _():
                gather(c + 2, rows0, gat0).start()

        put(n_chunks - 1, rows1, put1).wait()

    rows = pltpu.VMEM((SC_GATHER_ROWS, ROW_WORDS), jnp.int32)
    return pl.kernel(
        body,
        out_type=jax.ShapeDtypeStruct((n_idx, ROW_WORDS), jnp.int32),
        mesh=_sc_mesh(),
        scratch_types=[rows, rows, pltpu.VMEM((n_chunks, SC_GATHER_ROWS), jnp.int32)]
        + [pltpu.SemaphoreType.DMA] * 4,
        name="sc_gather",
    )(ys, dest3)


X_SLOTS = 3
O_SLOTS = 2


def _moe_kernel(be_ref, nused_ref, slot_ref, next_ref, xs_hbm, wg_hbm, wu_hbm, wd_hbm, ys_hbm,
                xbuf, obuf, wg_s, wu_s, wd_s, wg_f, wu_f, wd_f, xsem, osem, wsem, *, layer):
    n_used = nused_ref[0]

    def x_copy(i, s):
        start = pl.multiple_of(i * MOE_ROWS, MOE_ROWS)
        return pltpu.make_async_copy(xs_hbm.at[pl.ds(start, MOE_ROWS)], xbuf.at[s], xsem.at[s])

    def o_copy(i, s):
        start = pl.multiple_of(i * MOE_ROWS, MOE_ROWS)
        return pltpu.make_async_copy(obuf.at[s], ys_hbm.at[pl.ds(start, MOE_ROWS)], osem.at[s])

    def weight_copies(expert, slot):
        return [pltpu.make_async_copy(src.at[layer, expert], dst.at[slot], wsem.at[slot, j])
                for j, (src, dst) in enumerate(((wg_hbm, wg_f), (wu_hbm, wu_f), (wd_hbm, wd_f)))]

    first_expert = be_ref[0]
    for cp in weight_copies(first_expert, slot_ref[first_expert]):
        cp.start()
    for j in range(X_SLOTS - 1):
        @pl.when(j < n_used)
        def _(j=j):
            x_copy(j, j).start()

    def block(i, carry):
        e = be_ref[i]
        xs = lax.rem(i, X_SLOTS)
        os = lax.rem(i, O_SLOTS)

        @pl.when(jnp.logical_or(i == 0, e != be_ref[jnp.maximum(i - 1, 0)]))
        def _():
            slot = slot_ref[e]
            for cp in weight_copies(e, slot):
                cp.wait()
            wg_s[...] = wg_f[slot].astype(BF16)
            wu_s[...] = wu_f[slot].astype(BF16)
            wd_s[...] = wd_f[slot].astype(BF16)
            nxt = next_ref[e]

            @pl.when(nxt >= 0)
            def _():
                for cp in weight_copies(nxt, 1 - slot):
                    cp.start()

        x_copy(i, xs).wait()

        @pl.when(i + X_SLOTS - 1 < n_used)
        def _():
            x_copy(i + X_SLOTS - 1, lax.rem(i + X_SLOTS - 1, X_SLOTS)).start()

        @pl.when(i >= O_SLOTS)
        def _():
            o_copy(i - O_SLOTS, os).wait()

        half = D_MODEL // 2
        for c in range(MOE_ROWS // MOE_CHUNK):
            rows = pl.ds(c * MOE_CHUNK, MOE_CHUNK)
            lo, hi = _unpack_bf16_pairs(xbuf[xs, rows, :])
            lo, hi = lo.astype(BF16), hi.astype(BF16)
            gate = _dot(lo, wg_s[:half, :]) + _dot(hi, wg_s[half:, :])
            up = _dot(lo, wu_s[:half, :]) + _dot(hi, wu_s[half:, :])
            mid = _silu(gate) * up
            obuf[os, rows, :] = _pack_bf16_pairs(_dot(mid.astype(BF16), wd_s[...]))
        o_copy(i, os).start()
        return carry

    lax.fori_loop(0, n_used, block, 0)
    for j in range(O_SLOTS):
        @pl.when(n_used > j)
        def _(j=j):
            i = n_used - 1 - j
            o_copy(i, lax.rem(i, O_SLOTS)).wait()


def _moe(block_expert, n_used, slot, next_expert, xs, wg, wu, wd, layer):
    hbm = pl.BlockSpec(memory_space=pl.ANY)
    block = (MOE_ROWS, ROW_WORDS)
    grid_spec = pltpu.PrefetchScalarGridSpec(
        num_scalar_prefetch=4,
        grid=(1,),
        in_specs=[hbm, hbm, hbm, hbm],
        out_specs=hbm,
        scratch_shapes=[
            pltpu.VMEM((X_SLOTS,) + block, jnp.int32),
            pltpu.VMEM((O_SLOTS,) + block, jnp.int32),
            pltpu.VMEM((D_MODEL, EXPERT_FF), BF16),
            pltpu.VMEM((D_MODEL, EXPERT_FF), BF16),
            pltpu.VMEM((EXPERT_FF, D_MODEL), BF16),
            pltpu.VMEM((2, D_MODEL, EXPERT_FF), F32),
            pltpu.VMEM((2, D_MODEL, EXPERT_FF), F32),
            pltpu.VMEM((2, EXPERT_FF, D_MODEL), F32),
            pltpu.SemaphoreType.DMA((X_SLOTS,)),
            pltpu.SemaphoreType.DMA((O_SLOTS,)),
            pltpu.SemaphoreType.DMA((2, 3)),
        ],
    )
    return pl.pallas_call(
        functools.partial(_moe_kernel, layer=layer),
        grid_spec=grid_spec,
        out_shape=jax.ShapeDtypeStruct((xs.shape[0], ROW_WORDS), jnp.int32),
        compiler_params=_cparams("arbitrary"),
        name="moe_experts",
    )(block_expert, n_used, slot, next_expert, xs, wg, wu, wd)


def _combine_kernel(y_ref, w_ref, hs_ref, gtf_ref, gfin_ref, o_ref, *, final):
    w = w_ref[...]
    acc_lo = acc_hi = None
    for k in range(TOP_K):
        lo, hi = _unpack_bf16_pairs(y_ref[k])
        wk = w[:, k:k + 1]
        acc_lo = lo * wk if k == 0 else acc_lo + lo * wk
        acc_hi = hi * wk if k == 0 else acc_hi + hi * wk
    out = hs_ref[...] + gtf_ref[0] * jnp.concatenate([acc_lo, acc_hi], axis=1)
    if final:
        out = _rms(out) * gfin_ref[...]
    o_ref[...] = out


def _combine(y, w, hs, gtf, g_final, *, final):
    n_rows = hs.shape[0]
    return pl.pallas_call(
        functools.partial(_combine_kernel, final=final),
        grid=(n_rows // TM_COMB,),
        in_specs=[
            pl.BlockSpec((TOP_K, TM_COMB, ROW_WORDS), lambda i: (0, i, 0)),
            pl.BlockSpec((TM_COMB, TOP_K), lambda i: (i, 0)),
            pl.BlockSpec((TM_COMB, D_MODEL), lambda i: (i, 0)),
            pl.BlockSpec((1, 1, D_MODEL), lambda i: (jnp.minimum(i // (SEQ // TM_COMB), BATCH), 0, 0)),
            pl.BlockSpec((1, D_MODEL), lambda i: (0, 0)),
        ],
        out_specs=pl.BlockSpec((TM_COMB, D_MODEL), lambda i: (i, 0)),
        out_shape=jax.ShapeDtypeStruct((n_rows, D_MODEL), F32),
        compiler_params=_cparams("parallel"),
        name="combine_final" if final else "combine",
    )(y, w, hs, gtf, g_final)


def _rope_tables():
    rows = SEQ // GRID_W

    def angles(rot_dim):
        half = rot_dim // 2
        inv_freq = ROPE_BASE ** (-jnp.arange(0, half, 2, dtype=F32) / half)
        row = jnp.repeat(jnp.arange(rows, dtype=F32), GRID_W)
        col = jnp.tile(jnp.arange(GRID_W, dtype=F32), rows)
        ang_r = row[:, None] * inv_freq
        ang_c = col[:, None] * inv_freq
        return jnp.concatenate([ang_r, ang_r, ang_c, ang_c], axis=-1)

    def signed(sin, quarter):
        sign = jnp.where((jnp.arange(sin.shape[-1]) // quarter) % 2 == 0, -1.0, 1.0)
        return sin * sign

    def with_identity(cos, sin):
        cos = jnp.concatenate([cos, jnp.ones((TM, LANES), F32)], axis=0)
        sin = jnp.concatenate([sin, jnp.zeros((TM, LANES), F32)], axis=0)
        return cos, sin

    a64 = angles(SWA_HEAD_DIM)
    c64 = jnp.tile(jnp.cos(a64), (1, LANES // SWA_HEAD_DIM))
    s64 = jnp.tile(signed(jnp.sin(a64), SWA_HEAD_DIM // 4), (1, LANES // SWA_HEAD_DIM))
    a32 = angles(MLA_ROPE)
    pad_lo = MLA_NOPE
    pad_hi = LANES - MLA_NOPE - MLA_ROPE
    cm = jnp.concatenate([jnp.ones((SEQ, pad_lo), F32), jnp.cos(a32), jnp.ones((SEQ, pad_hi), F32)], axis=-1)
    sm = jnp.concatenate([jnp.zeros((SEQ, pad_lo), F32), signed(jnp.sin(a32), MLA_ROPE // 4),
                          jnp.zeros((SEQ, pad_hi), F32)], axis=-1)
    return with_identity(cm, sm) + with_identity(c64, s64)


def _layer0_weights(wa_in, wa_uq, wa_ukv, wa_out):
    d = D_MODEL
    cq, ckv, kr, qs, ks, vs = jnp.split(
        wa_in, [C_CKV, C_KR, C_KR + MLA_ROPE, C_KR + MLA_ROPE + SWA_HEADS * SWA_HEAD_DIM,
                C_KR + MLA_ROPE + (SWA_HEADS + SWA_KV_HEADS) * SWA_HEAD_DIM], axis=-1)
    kr_pad = jnp.concatenate([jnp.zeros((d, MLA_NOPE), F32), kr,
                              jnp.zeros((d, LANES - MLA_NOPE - MLA_ROPE), F32)], axis=-1)
    qs_pair = qs.reshape(d, SWA_KV_HEADS, SWA_GROUP, SWA_HEAD_DIM).transpose(0, 2, 1, 3).reshape(d, -1)
    win = jnp.concatenate([cq, ckv, kr_pad, qs_pair, ks, vs], axis=-1).astype(BF16)
    uq = wa_uq.reshape(MLA_Q_RANK, MLA_HEADS, MLA_NOPE + MLA_ROPE)
    uq = jnp.pad(uq, ((0, 0), (0, 0), (0, LANES - MLA_NOPE - MLA_ROPE))).reshape(MLA_Q_RANK, -1)
    ukv = wa_ukv.reshape(MLA_KV_RANK, MLA_HEADS, MLA_NOPE + MLA_V)
    uk = jnp.pad(ukv[:, :, :MLA_NOPE], ((0, 0), (0, 0), (0, LANES - MLA_NOPE))).reshape(MLA_KV_RANK, -1)
    uv = ukv[:, :, MLA_NOPE:].reshape(MLA_KV_RANK, -1)
    wukv = jnp.concatenate([uk, uv], axis=-1)
    n_mla = MLA_HEADS * MLA_V
    out_swa = wa_out[n_mla:].reshape(SWA_KV_HEADS, SWA_GROUP, SWA_HEAD_DIM, d).transpose(1, 0, 2, 3)
    wout = jnp.concatenate([wa_out[:n_mla], out_swa.reshape(-1, d)], axis=0)
    return win, uq.astype(BF16), wukv.astype(BF16), wout.astype(BF16)


def _layer1_weights(wc_in):
    per_head = 4 * DIFF_HALF + DIFF_V
    w = wc_in.reshape(D_MODEL, DIFF_HEADS, per_head)
    q = w[:, :, :2 * DIFF_HALF].reshape(D_MODEL, -1)
    k = w[:, :, 2 * DIFF_HALF:4 * DIFF_HALF].reshape(D_MODEL, -1)
    v = w[:, :, 4 * DIFF_HALF:].reshape(D_MODEL, -1)
    return jnp.concatenate([q, k, v], axis=-1).astype(BF16)


def _block_tables(counts, n_blocks):
    padded = (counts + MOE_ROWS - 1) // MOE_ROWS * MOE_ROWS
    pad_end = jnp.cumsum(padded)
    pad_start = (pad_end - padded).astype(jnp.int32)
    block_start = jnp.arange(n_blocks, dtype=jnp.int32) * MOE_ROWS
    block_expert = jnp.minimum((pad_end[None, :] <= block_start[:, None]).sum(axis=1),
                               N_EXPERTS - 1).astype(jnp.int32)
    n_used = (pad_end[-1] // MOE_ROWS).astype(jnp.int32).reshape(1)
    has = padded > 0
    slot = ((jnp.cumsum(has) - 1) & 1).astype(jnp.int32)
    ids = jnp.where(has, jnp.arange(N_EXPERTS, dtype=jnp.int32), N_EXPERTS)
    after = jnp.concatenate([lax.cummin(ids, reverse=True)[1:], jnp.full((1,), N_EXPERTS, jnp.int32)])
    next_expert = jnp.where(after < N_EXPERTS, after, -1).astype(jnp.int32)
    return pad_start, block_expert, n_used, slot, next_expert


def _moe_layer(f_packed, top_idx, top_w, rank, counts, hs, gtf, g_final, wg, wu, wd, layer, *, final):
    n_tok = top_idx.shape[1]
    n_blocks = -(-(n_tok * TOP_K) // MOE_ROWS) + N_EXPERTS
    pad_start, block_expert, n_used, slot, next_expert = _block_tables(counts[:, 0], n_blocks)
    dest = _dest(pad_start, top_idx, rank)
    xs = _sc_dispatch(f_packed, dest, n_blocks * MOE_ROWS)
    ys = _moe(block_expert, n_used, slot, next_expert, xs, wg, wu, wd, layer)
    y = _sc_gather(ys, dest).reshape(TOP_K, n_tok, ROW_WORDS)
    return _combine(y, top_w.T, hs, gtf, g_final, final=final)


def kernel(x, c, ctx, c_ctx, w_ada, b_ada, g_mix, g_ffn, wa_in, ga_q, ga_kv, wa_uq, wa_ukv, wa_sink, wa_out,
           wc_in, lam_q1, lam_k1, lam_q2, lam_k2, gc_sub, wc_out, w_router, b_router, we_gate, we_up, we_down,
           ws_gate, ws_up, ws_down, g_final):
    d = D_MODEL
    x2, ctx2 = x.reshape(N_LAT, d), ctx.reshape(N_CTX, d)
    cc =jnp.concatenate([c, c_ctx[None, :], jnp.zeros((MOD_ROWS - BATCH - 1, d), F32)], axis=0)
    mod = _ada(cc, w_ada, b_ada).reshape(DEPTH, MOD_ROWS, 6, 1, d)

    def mod_vec(layer, j):
        return mod[layer, :, j]

    cm, sm, c64, s64 = _rope_tables()
    g_final2 = g_final.reshape(1, d)

    def router_weights(layer):
        wt = w_router[layer].T
        hi = wt.astype(BF16)
        lo = (wt - hi.astype(F32)).astype(BF16)
        return hi, lo

    def shared_weights(layer):
        return ws_gate[layer].astype(BF16), ws_up[layer].astype(BF16), ws_down[layer].astype(BF16)

    win, wuq, wukv, wout0 = _layer0_weights(wa_in[0], wa_uq[0], wa_ukv[0], wa_out[0])
    qm, km, vm, qs, ks, vs = _proj0(x2, ctx2, g_mix[0].reshape(1, d), mod_vec(0, 0), mod_vec(0, 1), win,
                                    ga_q[0].reshape(1, -1), ga_kv[0].reshape(1, -1), wuq, wukv, cm, sm, c64, s64)
    sink = wa_sink[0].astype(F32)
    o = _mla_attention(qm, km, vm, None, ctx_queries=False)
    o = _mla_attention(qm, km, vm, o, ctx_queries=True)
    o = _swa_attention(sink, qs, ks, vs, o, ctx_queries=False)
    o = _swa_attention(sink, qs, ks, vs, o, ctx_queries=True)
    hs, f, logits_t = _post(
        o, x2, ctx2, wout0, mod_vec(0, 2), g_ffn[0].reshape(1, d), mod_vec(0, 3), mod_vec(0, 4), mod_vec(0, 5),
        *router_weights(0), *shared_weights(0), N_TOK)
    top_idx, top_w, rank, counts = _router(logits_t, b_router[0].reshape(N_EXPERTS, 1))
    h = _moe_layer(f, top_idx, top_w, rank, counts, hs, mod_vec(0, 5), g_final2,
                   we_gate, we_up, we_down, 0, final=False)

    layer = 1
    lam_init = 0.8 - 0.6 * math.exp(-0.3 * layer)
    q1, k1, v1 = _proj1(h, g_mix[1].reshape(1, d), mod_vec(1, 0), mod_vec(1, 1), _layer1_weights(wc_in[0]),
                        c64, s64)
    lam_vecs = jnp.zeros((8, LANES), F32).at[:4, :DIFF_HALF].set(
        jnp.stack([lam_q1[0], lam_k1[0], lam_q2[0], lam_k2[0]]).astype(F32))
    o = _diff_attention(q1, k1, v1, lam_vecs, gc_sub[0].reshape(1, DIFF_V), lam_init)
    hs, f, logits_t = _post(
        o, h, h, wc_out[0].astype(BF16), mod_vec(1, 2), g_ffn[1].reshape(1, d), mod_vec(1, 3), mod_vec(1, 4),
        mod_vec(1, 5), *router_weights(1), *shared_weights(1), N_LAT)
    top_idx, top_w, rank, counts = _router(logits_t, b_router[1].reshape(N_EXPERTS, 1))
    out = _moe_layer(f, top_idx, top_w, rank, counts, hs, mod_vec(1, 5), g_final2,
                     we_gate, we_up, we_down, 1, final=True)
    return out.reshape(BATCH, SEQ, d)
```

```python
import functools
import math

import jax
import jax.numpy as jnp
from jax import lax
from jax.experimental import pallas as pl
from jax.experimental.pallas import tpu as pltpu
from jax.experimental.pallas import tpu_sc as plsc

F32 = jnp.float32
BF16 = jnp.bfloat16

D_MODEL = 1024
BATCH = 8
SEQ = 2048
DEPTH = 2
CTX_LEN = 256
GRID_W = 64
ROPE_BASE = 10000.0
EPS = 1e-6
NEG_INF = -1e30

MLA_HEADS = 8
MLA_Q_RANK = 384
MLA_KV_RANK = 256
MLA_NOPE = 64
MLA_ROPE = 32
MLA_V = 64
SWA_HEADS = 8
SWA_KV_HEADS = 2
SWA_HEAD_DIM = 64
SWA_GROUP = SWA_HEADS // SWA_KV_HEADS
WINDOW = 128
DIFF_HEADS = 8
DIFF_HALF = 64
DIFF_V = 128
N_EXPERTS = 64
N_GROUPS = 8
GROUP_SIZE = N_EXPERTS // N_GROUPS
TOPK_GROUPS = 4
TOP_K = 8
EXPERT_FF = 256
SHARED_FF = 256
ROUTED_SCALE = 2.5

LOG2E = math.log2(math.e)
LANES = 128
N_LAT = BATCH * SEQ
N_CTX = BATCH * CTX_LEN
N_TOK = N_LAT + N_CTX
MOD_ROWS = 16

TM = 1024
TM_CHUNK = 256
LAT_BLOCKS_PER_BATCH = SEQ // TM
TQ_MLA = 1024
MLA_PAIRS_PER_STEP = 4
TQ_DIFF = 512
DIFF_HEADS_PER_STEP = 8
SWA_BLOCK = 128
SWA_QBLOCKS = 4
MOE_ROWS = 512
MOE_CHUNK = 256
TM_COMB = 256
VMEM_LIMIT = 56 * 1024 * 1024

C_CQ = 0
C_CKV = C_CQ + MLA_Q_RANK
C_KR = C_CKV + MLA_KV_RANK
C_QS = C_KR + LANES
C_KS = C_QS + SWA_HEADS * SWA_HEAD_DIM
C_VS = C_KS + SWA_KV_HEADS * SWA_HEAD_DIM
C_END = C_VS + SWA_KV_HEADS * SWA_HEAD_DIM


def _cparams(*sem):
    return pltpu.CompilerParams(dimension_semantics=sem, vmem_limit_bytes=VMEM_LIMIT)


def _dot(a, b):
    return jnp.dot(a, b, preferred_element_type=F32)


def _dot_nt(a, b):
    return lax.dot_general(a, b, (((1,), (1,)), ((), ())), preferred_element_type=F32)


def _rms(x):
    return x * lax.rsqrt(jnp.mean(x * x, axis=-1, keepdims=True) + EPS)


def _silu(x):
    return x * jax.nn.sigmoid(x)


def _rope(x, cos, sin_signed, shift):
    n = x.shape[-1]
    lane = lax.broadcasted_iota(jnp.int32, x.shape, 1)
    first = (lane & shift) == 0
    rot = jnp.where(first, pltpu.roll(x, n - shift, 1), pltpu.roll(x, shift, 1))
    return x * cos + rot * sin_signed


def _mod_index(i):
    return jnp.minimum(i // LAT_BLOCKS_PER_BATCH, BATCH)


def _rope_index(i):
    return jnp.where(i < N_LAT // TM, i % LAT_BLOCKS_PER_BATCH, LAT_BLOCKS_PER_BATCH)


ADA_TN = 1536


def _ada_kernel(c_ref, w_ref, b_ref, o_ref):
    s = _silu(c_ref[...]).astype(BF16)
    o_ref[0] = _dot(s, w_ref[0].astype(BF16)) + b_ref[0]


def _ada(cc, w_ada, b_ada):
    n_out = w_ada.shape[-1]
    return pl.pallas_call(
        _ada_kernel,
        grid=(DEPTH, n_out // ADA_TN),
        in_specs=[
            pl.BlockSpec((MOD_ROWS, D_MODEL), lambda l, j: (0, 0)),
            pl.BlockSpec((1, D_MODEL, ADA_TN), lambda l, j: (l, 0, j)),
            pl.BlockSpec((1, 1, ADA_TN), lambda l, j: (l, 0, j)),
        ],
        out_specs=pl.BlockSpec((1, MOD_ROWS, ADA_TN), lambda l, j: (l, 0, j)),
        out_shape=jax.ShapeDtypeStruct((DEPTH, MOD_ROWS, n_out), F32),
        compiler_params=_cparams("parallel", "parallel"),
        name="ada",
    )(cc, w_ada, b_ada.reshape(DEPTH, 1, n_out))


def _token_rows(lat_ref, ctx_ref, rows):
    return jnp.where(pl.program_id(0) < N_LAT // TM, lat_ref[rows, :], ctx_ref[rows, :])


def _token_specs():
    lat_blocks = N_LAT // TM
    return [pl.BlockSpec((TM, D_MODEL), lambda i: (jnp.minimum(i, lat_blocks - 1), 0)),
            pl.BlockSpec((TM, D_MODEL), lambda i: (jnp.maximum(i - lat_blocks, 0), 0))]


def _proj0_kernel(x_ref, ctx_ref, g_ref, sh_ref, sc_ref, win_ref, gq_ref, gkv_ref, wuq_ref, wukv_ref,
                  cm_ref, sm_ref, c64_ref, s64_ref,
                  qm_ref, km_ref, vm_ref, qs_ref, ks_ref, vs_ref):
    q_scale = (MLA_NOPE + MLA_ROPE) ** -0.5 * LOG2E
    s_scale = SWA_HEAD_DIM ** -0.5 * LOG2E
    for c in range(TM // TM_CHUNK):
        rows = slice(c * TM_CHUNK, (c + 1) * TM_CHUNK)
        a = _rms(_token_rows(x_ref, ctx_ref, rows)) * g_ref[...]
        a = a * (1.0 + sc_ref[0]) + sh_ref[0]
        p = _dot(a.astype(BF16), win_ref[...])
        nq = _rms(p[:, C_CQ:C_CKV]) * gq_ref[...]
        nkv = _rms(p[:, C_CKV:C_KR]) * gkv_ref[...]
        q = _dot(nq.astype(BF16), wuq_ref[...])
        kv = _dot(nkv.astype(BF16), wukv_ref[...])
        cm, sm = cm_ref[rows, :], sm_ref[rows, :]
        c64, s64 = c64_ref[rows, :], s64_ref[rows, :]
        kr = _rope(p[:, C_KR:C_QS], cm, sm, MLA_ROPE // 4)
        for h in range(MLA_HEADS):
            sl = slice(h * LANES, (h + 1) * LANES)
            qm_ref[rows, sl] = (_rope(q[:, sl], cm, sm, MLA_ROPE // 4) * q_scale).astype(BF16)
            km_ref[rows, sl] = (kv[:, sl] + kr).astype(BF16)
        vm_ref[rows, :] = kv[:, MLA_HEADS * LANES:].astype(BF16)
        for g in range(SWA_GROUP):
            sl = slice(g * LANES, (g + 1) * LANES)
            qs_ref[rows, sl] = (_rope(p[:, C_QS + g * LANES:C_QS + (g + 1) * LANES], c64, s64,
                                      SWA_HEAD_DIM // 4) * s_scale).astype(BF16)
        ks_ref[rows, :] = _rope(p[:, C_KS:C_VS], c64, s64, SWA_HEAD_DIM // 4).astype(BF16)
        vs_ref[rows, :] = p[:, C_VS:C_END].astype(BF16)


def _proj0(x, ctx, g_mix, sh, sc, win, gq, gkv, wuq, wukv, cm, sm, c64, s64):
    row = lambda i: (i, 0)
    const = lambda i: (0, 0)
    mod = lambda i: (_mod_index(i), 0, 0)
    rope = lambda i: (_rope_index(i), 0)
    widths = (MLA_HEADS * LANES, MLA_HEADS * LANES, MLA_HEADS * MLA_V,
              SWA_HEADS * SWA_HEAD_DIM, SWA_KV_HEADS * SWA_HEAD_DIM, SWA_KV_HEADS * SWA_HEAD_DIM)
    return pl.pallas_call(
        _proj0_kernel,
        grid=(N_TOK // TM,),
        in_specs=_token_specs() + [
            pl.BlockSpec((1, D_MODEL), const),
            pl.BlockSpec((1, 1, D_MODEL), mod),
            pl.BlockSpec((1, 1, D_MODEL), mod),
            pl.BlockSpec(win.shape, const),
            pl.BlockSpec(gq.shape, const),
            pl.BlockSpec(gkv.shape, const),
            pl.BlockSpec(wuq.shape, const),
            pl.BlockSpec(wukv.shape, const),
            pl.BlockSpec((TM, LANES), rope),
            pl.BlockSpec((TM, LANES), rope),
            pl.BlockSpec((TM, LANES), rope),
            pl.BlockSpec((TM, LANES), rope),
        ],
        out_specs=[pl.BlockSpec((TM, w), row) for w in widths],
        out_shape=[jax.ShapeDtypeStruct((N_TOK, w), BF16) for w in widths],
        compiler_params=_cparams("parallel"),
        name="proj0",
    )(x, ctx, g_mix, sh, sc, win, gq, gkv, wuq, wukv, cm, sm, c64, s64)


def _proj1_kernel(h_ref, g_ref, sh_ref, sc_ref, w_ref, c64_ref, s64_ref, q_ref, k_ref, v_ref):
    width = DIFF_HEADS * LANES
    scale = DIFF_HALF ** -0.5 * LOG2E
    for c in range(TM // TM_CHUNK):
        rows = slice(c * TM_CHUNK, (c + 1) * TM_CHUNK)
        a = _rms(h_ref[rows, :]) * g_ref[...]
        a = a * (1.0 + sc_ref[0]) + sh_ref[0]
        p = _dot(a.astype(BF16), w_ref[...])
        c64, s64 = c64_ref[rows, :], s64_ref[rows, :]
        for h in range(DIFF_HEADS):
            sl = slice(h * LANES, (h + 1) * LANES)
            q_ref[rows, sl] = (_rope(p[:, sl], c64, s64, DIFF_HALF // 4) * scale).astype(BF16)
            k_ref[rows, sl] = _rope(p[:, width + h * LANES:width + (h + 1) * LANES], c64, s64,
                                    DIFF_HALF // 4).astype(BF16)
        v_ref[rows, :] = p[:, 2 * width:].astype(BF16)


def _proj1(h, g_mix, sh, sc, w, c64, s64):
    row = lambda i: (i, 0)
    const = lambda i: (0, 0)
    mod = lambda i: (_mod_index(i), 0, 0)
    rope = lambda i: (_rope_index(i), 0)
    width = DIFF_HEADS * LANES
    return pl.pallas_call(
        _proj1_kernel,
        grid=(N_TOK // TM,),
        in_specs=[
            pl.BlockSpec((TM, D_MODEL), row),
            pl.BlockSpec((1, D_MODEL), const),
            pl.BlockSpec((1, 1, D_MODEL), mod),
            pl.BlockSpec((1, 1, D_MODEL), mod),
            pl.BlockSpec(w.shape, const),
            pl.BlockSpec((TM, LANES), rope),
            pl.BlockSpec((TM, LANES), rope),
        ],
        out_specs=[pl.BlockSpec((TM, width), row)] * 3,
        out_shape=[jax.ShapeDtypeStruct((N_TOK, width), BF16)] * 3,
        compiler_params=_cparams("parallel"),
        name="proj1",
    )(h, g_mix, sh, sc, w, c64, s64)


def _mla_kernel(*refs, n_kv):
    q_ref = refs[0]
    k_refs = refs[1:1 + n_kv]
    v_refs = refs[1 + n_kv:1 + 2 * n_kv]
    o_ref = refs[1 + 2 * n_kv]
    lane = lax.broadcasted_iota(jnp.int32, (q_ref.shape[0], LANES), 1)
    own = [lane < MLA_V, lane >= MLA_V]
    for pair in range(MLA_PAIRS_PER_STEP):
        vsl = slice(pair * LANES, (pair + 1) * LANES)
        outs = []
        for hh in range(2):
            sl = slice((2 * pair + hh) * LANES, (2 * pair + hh + 1) * LANES)
            qh = q_ref[:, sl]
            ss = [_dot_nt(qh, k[:, sl]) for k in k_refs]
            m = functools.reduce(jnp.maximum, [jnp.max(s, axis=-1, keepdims=True) for s in ss])
            o = None
            for s, v in zip(ss, v_refs):
                vp = v[:, vsl]
                vlane = lax.broadcasted_iota(jnp.int32, vp.shape, 1)
                keep = (vlane < MLA_V) if hh == 0 else (vlane >= MLA_V)
                vh = jnp.where(keep, vp, jnp.ones_like(vp))
                part = _dot(jnp.exp2(s - m).astype(BF16), vh)
                o = part if o is None else o + part
            outs.append(o / pltpu.roll(o, MLA_V, 1))
        o_ref[:, vsl] = jnp.where(own[0], outs[0], outs[1]).astype(BF16)


def _into_buffer(kernel_fn, buf):
    if buf is None:
        return kernel_fn, [], []

    def without_alias_ref(*refs):
        return kernel_fn(*refs[:-2], refs[-1])

    return without_alias_ref, [pl.BlockSpec(memory_space=pl.ANY)], [buf]


def _mla_attention(qm, km, vm, buf, *, ctx_queries):
    groups = MLA_HEADS // 2 // MLA_PAIRS_PER_STEP
    qk_w = 2 * LANES * MLA_PAIRS_PER_STEP
    v_w = LANES * MLA_PAIRS_PER_STEP
    if ctx_queries:
        tq, n_q = CTX_LEN, 1
        q_map = lambda b, h, i: (N_LAT // CTX_LEN + b, h)
        kv_specs = [(CTX_LEN, lambda b, h, i: (N_LAT // CTX_LEN + b, h))]
    else:
        tq, n_q = TQ_MLA, SEQ // TQ_MLA
        q_map = lambda b, h, i: (b * n_q + i, h)
        kv_specs = [(CTX_LEN, lambda b, h, i: (N_LAT // CTX_LEN + b, h)), (SEQ, lambda b, h, i: (b, h))]
    n_kv = len(kv_specs)
    in_specs = [pl.BlockSpec((tq, qk_w), q_map)]
    in_specs += [pl.BlockSpec((n, qk_w), m) for n, m in kv_specs]
    in_specs += [pl.BlockSpec((n, v_w), m) for n, m in kv_specs]
    kernel_fn, extra_specs, extra_args = _into_buffer(functools.partial(_mla_kernel, n_kv=n_kv), buf)
    n_in = len(in_specs)
    return pl.pallas_call(
        kernel_fn,
        grid=(BATCH, groups, n_q),
        in_specs=in_specs + extra_specs,
        out_specs=pl.BlockSpec((tq, v_w), q_map),
        out_shape=jax.ShapeDtypeStruct((N_TOK, D_MODEL), BF16),
        input_output_aliases={n_in: 0} if extra_args else {},
        compiler_params=_cparams("parallel", "parallel", "parallel"),
        name="mla_ctx" if ctx_queries else "mla_lat",
    )(qm, *([km] * n_kv), *([vm] * n_kv), *extra_args)


def _swa_kernel(sink_ref, *refs, band, n_blocks):
    q_ref = refs[0]
    if band:
        n_kv = SWA_QBLOCKS + 2
        k_blocks = refs[1:1 + n_kv]
        kx = refs[1 + n_kv]
        v_blocks = refs[2 + n_kv:2 + 2 * n_kv]
        vx, o_ref = refs[2 + 2 * n_kv], refs[3 + 2 * n_kv]
        chains, rows = SWA_QBLOCKS, SWA_BLOCK
    else:
        kx, vx, o_ref = refs[1:]
        chains, rows = 1, q_ref.shape[0]
    stacked = SWA_GROUP * rows
    k_ctx = kx[...]
    v_ctx = vx[...]
    lane = lax.broadcasted_iota(jnp.int32, (rows, LANES), 1)
    low = lane < SWA_HEAD_DIM
    row_group = lax.broadcasted_iota(jnp.int32, (stacked, 1), 0) // rows
    for c in range(chains):
        qrows = slice(c * rows, (c + 1) * rows)
        if band:
            n = pl.program_id(1) * SWA_QBLOCKS + c
            k_band = jnp.concatenate([k[...] for k in k_blocks[c:c + 3]], axis=0)
            v_band = jnp.concatenate([v[...] for v in v_blocks[c:c + 3]], axis=0)
            qq = lax.broadcasted_iota(jnp.int32, (stacked, 3 * SWA_BLOCK), 0) & (SWA_BLOCK - 1)
            kk = lax.broadcasted_iota(jnp.int32, (stacked, 3 * SWA_BLOCK), 1)
            rel = kk - SWA_BLOCK - qq
            key_pos = (n - 1) * SWA_BLOCK + kk
            valid = (jnp.abs(rel) <= WINDOW) & (key_pos >= 0) & (key_pos < n_blocks * SWA_BLOCK)
        halves = []
        for hk in range(SWA_KV_HEADS):
            keep = low if hk == 0 else jnp.logical_not(low)
            qh = jnp.concatenate(
                [jnp.where(keep, q_ref[qrows, g * LANES:(g + 1) * LANES], jnp.zeros((rows, LANES), BF16))
                 for g in range(SWA_GROUP)], axis=0)
            sink = jnp.zeros((stacked, 1), F32)
            for g in range(SWA_GROUP):
                sink = jnp.where(row_group == g, sink_ref[hk * SWA_GROUP + g] * LOG2E, sink)
            s_ctx = _dot_nt(qh, k_ctx)
            m = jnp.maximum(jnp.max(s_ctx, axis=-1, keepdims=True), sink)
            if band:
                s_band = jnp.where(valid, _dot_nt(qh, k_band), NEG_INF)
                m = jnp.maximum(m, jnp.max(s_band, axis=-1, keepdims=True))

            def ones_other(v, hk=hk):
                vlane = lax.broadcasted_iota(jnp.int32, v.shape, 1)
                own = (vlane < SWA_HEAD_DIM) if hk == 0 else (vlane >= SWA_HEAD_DIM)
                return jnp.where(own, v, jnp.ones_like(v))

            o = _dot(jnp.exp2(s_ctx - m).astype(BF16), ones_other(v_ctx))
            if band:
                o = o + _dot(jnp.exp2(s_band - m).astype(BF16), ones_other(v_band))
            denom = pltpu.roll(o, SWA_HEAD_DIM, 1) + jnp.exp2(sink - m)
            halves.append(o / denom)
        for g in range(SWA_GROUP):
            rs = slice(g * rows, (g + 1) * rows)
            o_ref[qrows, g * LANES:(g + 1) * LANES] = jnp.where(low, halves[0][rs], halves[1][rs]).astype(BF16)


def _swa_attention(sink, qs, ks, vs, buf, *, ctx_queries):
    width = SWA_HEADS * SWA_HEAD_DIM
    kvw = SWA_KV_HEADS * SWA_HEAD_DIM
    ctx_map = lambda b, i: (N_LAT // CTX_LEN + b, 0)
    smem = pl.BlockSpec(memory_space=pltpu.SMEM)
    if ctx_queries:
        grid = (BATCH, 1)
        in_specs = [smem, pl.BlockSpec((CTX_LEN, width), ctx_map),
                    pl.BlockSpec((CTX_LEN, kvw), ctx_map), pl.BlockSpec((CTX_LEN, kvw), ctx_map)]
        args = (sink, qs, ks, vs)
        out_spec = pl.BlockSpec((CTX_LEN, width), lambda b, i: (N_LAT // CTX_LEN + b, 1))
        n_blocks = 1
    else:
        n_blocks = SEQ // SWA_BLOCK
        steps = n_blocks // SWA_QBLOCKS
        grid = (BATCH, steps)
        q_rows = SWA_QBLOCKS * SWA_BLOCK

        def kv_block(offset):
            return lambda b, i: (b * n_blocks + jnp.clip(i * SWA_QBLOCKS + offset, 0, n_blocks - 1), 0)

        band_specs = [pl.BlockSpec((SWA_BLOCK, kvw), kv_block(off)) for off in range(-1, SWA_QBLOCKS + 1)]
        in_specs = ([smem, pl.BlockSpec((q_rows, width), lambda b, i: (b * steps + i, 0))]
                    + band_specs + [pl.BlockSpec((CTX_LEN, kvw), ctx_map)]
                    + band_specs + [pl.BlockSpec((CTX_LEN, kvw), ctx_map)])
        n_band = len(band_specs)
        args = (sink, qs) + (ks,) * (n_band + 1) + (vs,) * (n_band + 1)
        out_spec = pl.BlockSpec((q_rows, width), lambda b, i: (b * steps + i, 1))
    kernel_fn, extra_specs, extra_args = _into_buffer(
        functools.partial(_swa_kernel, band=not ctx_queries, n_blocks=n_blocks), buf)
    return pl.pallas_call(
        kernel_fn,
        grid=grid,
        in_specs=in_specs + extra_specs,
        out_specs=out_spec,
        out_shape=jax.ShapeDtypeStruct((N_TOK, D_MODEL), BF16),
        input_output_aliases={len(in_specs): 0} if extra_args else {},
        compiler_params=_cparams("parallel", "parallel"),
        name="swa_ctx" if ctx_queries else "swa_lat",
    )(*args, *extra_args)


def _diff_kernel(q_ref, kc_ref, kl_ref, vc_ref, vl_ref, lam_ref, g_ref, o_ref, *, lam_init):
    lam = (jnp.exp(jnp.sum(lam_ref[0:1, :] * lam_ref[1:2, :], axis=-1, keepdims=True))
           - jnp.exp(jnp.sum(lam_ref[2:3, :] * lam_ref[3:4, :], axis=-1, keepdims=True)) + lam_init)
    lane = lax.broadcasted_iota(jnp.int32, (q_ref.shape[0], LANES), 1)
    low = lane < DIFF_HALF
    k_refs = (kc_ref, kl_ref)
    v_refs = (vc_ref, vl_ref)
    for h in range(DIFF_HEADS_PER_STEP):
        sl = slice(h * LANES, (h + 1) * LANES)
        q = q_ref[:, sl]
        zero = jnp.zeros_like(q)

        vx = [jnp.concatenate([v[:, sl], jnp.ones((v.shape[0], LANES), BF16)], axis=1) for v in v_refs]

        def attend(qh):
            ss = [_dot_nt(qh, k[:, sl]) for k in k_refs]
            m = functools.reduce(jnp.maximum, [jnp.max(s, axis=-1, keepdims=True) for s in ss])
            ox = functools.reduce(jnp.add, [_dot(jnp.exp2(s - m).astype(BF16), v) for s, v in zip(ss, vx)])
            return ox[:, :LANES] / ox[:, LANES:]

        o = attend(jnp.where(low, q, zero)) - lam * attend(jnp.where(low, zero, q))
        o = _rms(o) * g_ref[...] * (1.0 - lam_init)
        o_ref[:, sl] = o.astype(BF16)


def _diff_attention(q, k, v, lam_vecs, g_sub, lam_init):
    n_q = SEQ // TQ_DIFF
    q_map = lambda b, h, i: (b * n_q + i, h)
    ctx_map = lambda b, h, i: (N_LAT // CTX_LEN + b, h)
    lat_map = lambda b, h, i: (b, h)
    const = lambda b, h, i: (0, 0)
    width = LANES * DIFF_HEADS_PER_STEP
    return pl.pallas_call(
        functools.partial(_diff_kernel, lam_init=lam_init),
        grid=(BATCH, DIFF_HEADS // DIFF_HEADS_PER_STEP, n_q),
        in_specs=[
            pl.BlockSpec((TQ_DIFF, width), q_map),
            pl.BlockSpec((CTX_LEN, width), ctx_map),
            pl.BlockSpec((SEQ, width), lat_map),
            pl.BlockSpec((CTX_LEN, width), ctx_map),
            pl.BlockSpec((SEQ, width), lat_map),
            pl.BlockSpec(lam_vecs.shape, const),
            pl.BlockSpec(g_sub.shape, const),
        ],
        out_specs=pl.BlockSpec((TQ_DIFF, width), q_map),
        out_shape=jax.ShapeDtypeStruct((N_LAT, DIFF_HEADS * DIFF_V), BF16),
        compiler_params=_cparams("parallel", "parallel", "parallel"),
        name="diff_attn",
    )(q, k, k, v, v, lam_vecs, g_sub)


def _pack_bf16_pairs(x):
    n = x.shape[1] // 2
    lo = pltpu.bitcast(x[:, :n].astype(BF16).astype(F32), jnp.uint32)
    hi = pltpu.bitcast(x[:, n:].astype(BF16).astype(F32), jnp.uint32)
    return pltpu.bitcast((lo >> 16) | hi, jnp.int32)


def _unpack_bf16_pairs(w):
    u = pltpu.bitcast(w, jnp.uint32)
    lo = pltpu.bitcast(u << 16, F32)
    hi = pltpu.bitcast(u & jnp.uint32(0xFFFF0000), F32)
    return lo, hi


def _route(logits_t, bias, cnt_ref, tri_ref, idx_ref, w_ref, rank_ref):
    tm = logits_t.shape[1]
    scores = jax.nn.sigmoid(logits_t)
    biased = scores + bias
    sub = lax.broadcasted_iota(jnp.int32, (GROUP_SIZE, tm), 0).astype(F32)
    grp_scores, grp_biased, grp_index = [], [], []
    group_score = []
    for g in range(N_GROUPS):
        sl = slice(g * GROUP_SIZE, (g + 1) * GROUP_SIZE)
        bg = biased[sl, :]
        grp_scores.append(scores[sl, :])
        grp_biased.append(bg)
        grp_index.append(sub + float(g * GROUP_SIZE))
        m1 = jnp.max(bg, axis=0, keepdims=True)
        first = jnp.min(jnp.where(bg == m1, sub, float(GROUP_SIZE)), axis=0, keepdims=True)
        m2 = jnp.max(jnp.where(sub == first, -jnp.inf, bg), axis=0, keepdims=True)
        group_score.append(m1 + m2)
    keep = [jnp.zeros((1, tm), F32) for _ in range(N_GROUPS)]
    for _ in range(TOPK_GROUPS):
        m = functools.reduce(jnp.maximum, group_score)
        found = jnp.zeros((1, tm), F32)
        for g in range(N_GROUPS):
            hit = jnp.where(group_score[g] == m, 1.0 - found, 0.0)
            keep[g] = keep[g] + hit
            found = found + hit
            group_score[g] = jnp.where(hit > 0.0, -jnp.inf, group_score[g])
    vals = [jnp.where(keep[g] > 0.0, grp_biased[g], NEG_INF) for g in range(N_GROUPS)]
    chosen = [jnp.zeros((GROUP_SIZE, tm), F32) for _ in range(N_GROUPS)]
    picked = []
    for _ in range(TOP_K):
        m = jnp.max(functools.reduce(jnp.maximum, vals), axis=0, keepdims=True)
        cand = [jnp.where(vals[g] == m, grp_index[g], float(N_EXPERTS)) for g in range(N_GROUPS)]
        ei = jnp.min(functools.reduce(jnp.minimum, cand), axis=0, keepdims=True)
        sel = [grp_index[g] == ei for g in range(N_GROUPS)]
        s_k = functools.reduce(jnp.add, [jnp.where(sel[g], grp_scores[g], 0.0) for g in range(N_GROUPS)])
        picked.append((ei, jnp.sum(s_k, axis=0, keepdims=True)))
        vals = [jnp.where(sel[g], -jnp.inf, vals[g]) for g in range(N_GROUPS)]
        chosen = [jnp.where(sel[g], 1.0, chosen[g]) for g in range(N_GROUPS)]
    total = functools.reduce(jnp.add, [s for _, s in picked])
    chosen_all = jnp.concatenate(chosen, axis=0)
    tri = tri_ref[...]
    width = tri.shape[0]
    cnt = cnt_ref[...]
    parts = []
    for j in range(tm // width):
        ch = chosen_all[:, j * width:(j + 1) * width]
        parts.append(_dot(ch.astype(BF16), tri) + cnt)
        cnt = cnt + jnp.sum(ch, axis=1, keepdims=True)
    cnt_ref[...] = cnt
    rank_all = jnp.concatenate(parts, axis=1)
    for k, (ei, s) in enumerate(picked):
        idx_ref[k:k + 1, :] = ei.astype(jnp.int32)
        w_ref[k:k + 1, :] = s / total * ROUTED_SCALE
        r = functools.reduce(jnp.add, [
            jnp.where(grp_index[g] == ei, rank_all[g * GROUP_SIZE:(g + 1) * GROUP_SIZE, :], 0.0)
            for g in range(N_GROUPS)])
        rank_ref[k:k + 1, :] = jnp.sum(r, axis=0, keepdims=True).astype(jnp.int32)


ROUTE_TN = 2048
TRI_N = 256


def _router_kernel(logits_ref, br_ref, tri_ref, idx_ref, w_ref, rank_ref, cnt_out_ref, cnt_ref):
    @pl.when(pl.program_id(0) == 0)
    def _():
        cnt_ref[...] = jnp.zeros_like(cnt_ref)

    _route(logits_ref[...], br_ref[...], cnt_ref, tri_ref, idx_ref, w_ref, rank_ref)
    cnt_out_ref[...] = jnp.broadcast_to(cnt_ref[...], cnt_out_ref.shape).astype(jnp.int32)


def _router(logits_t, br):
    n_tok = logits_t.shape[1]
    col = lambda i: (0, i)
    const = lambda i: (0, 0)
    tri = (jnp.arange(TRI_N)[:, None] < jnp.arange(TRI_N)[None, :]).astype(BF16)
    blk = pl.BlockSpec((TOP_K, ROUTE_TN), col)
    return pl.pallas_call(
        _router_kernel,
        grid=(n_tok // ROUTE_TN,),
        in_specs=[pl.BlockSpec((N_EXPERTS, ROUTE_TN), col), pl.BlockSpec(br.shape, const),
                  pl.BlockSpec(tri.shape, const)],
        out_specs=[blk, blk, blk, pl.BlockSpec((N_EXPERTS, LANES), const)],
        out_shape=[
            jax.ShapeDtypeStruct((TOP_K, n_tok), jnp.int32),
            jax.ShapeDtypeStruct((TOP_K, n_tok), F32),
            jax.ShapeDtypeStruct((TOP_K, n_tok), jnp.int32),
            jax.ShapeDtypeStruct((N_EXPERTS, LANES), jnp.int32),
        ],
        scratch_shapes=[pltpu.VMEM((N_EXPERTS, 1), F32)],
        compiler_params=_cparams("arbitrary"),
        name="router",
    )(logits_t, br, tri)


def _post_kernel(o_ref, hl_ref, hc_ref, wout_ref, gtm_ref, g_ref, sh_ref, sc_ref, gtf_ref,
                 wrh_ref, wrl_ref, wsg_ref, wsu_ref, wsd_ref, hs_ref, f_ref, logits_ref):
    for c in range(TM // TM_CHUNK):
        rows = slice(c * TM_CHUNK, (c + 1) * TM_CHUNK)
        h1 = _token_rows(hl_ref, hc_ref, rows) + gtm_ref[0] * _dot(o_ref[rows, :], wout_ref[...])
        f = _rms(h1) * g_ref[...]
        f = f * (1.0 + sc_ref[0]) + sh_ref[0]
        f_hi = f.astype(BF16)
        f_lo = (f - f_hi.astype(F32)).astype(BF16)
        f_ref[rows, :] = _pack_bf16_pairs(f)
        logits_ref[:, rows] = (_dot_nt(wrh_ref[...], f_hi) + _dot_nt(wrh_ref[...], f_lo)
                               + _dot_nt(wrl_ref[...], f_hi))
        mid = _silu(_dot(f_hi, wsg_ref[...])) * _dot(f_hi, wsu_ref[...])
        shared = _dot(mid.astype(BF16), wsd_ref[...])
        hs_ref[rows, :] = h1 + gtf_ref[0] * shared


def _post(o, h_lat, h_ctx, wout, gtm, g_ffn, sh, sc, gtf, wrh, wrl, wsg, wsu, wsd, n_rows):
    row = lambda i: (i, 0)
    const = lambda i: (0, 0)
    mod = lambda i: (_mod_index(i), 0, 0)
    vec = pl.BlockSpec((1, 1, D_MODEL), mod)
    full = lambda a: pl.BlockSpec(a.shape, const)
    return pl.pallas_call(
        _post_kernel,
        grid=(n_rows // TM,),
        in_specs=[pl.BlockSpec((TM, D_MODEL), row)] + _token_specs() + [
            full(wout), vec, full(g_ffn), vec, vec, vec,
            full(wrh), full(wrl), full(wsg), full(wsu), full(wsd),
        ],
        out_specs=[
            pl.BlockSpec((TM, D_MODEL), row),
            pl.BlockSpec((TM, D_MODEL // 2), row),
            pl.BlockSpec((N_EXPERTS, TM), lambda i: (0, i)),
        ],
        out_shape=[
            jax.ShapeDtypeStruct((n_rows, D_MODEL), F32),
            jax.ShapeDtypeStruct((n_rows, D_MODEL // 2), jnp.int32),
            jax.ShapeDtypeStruct((N_EXPERTS, n_rows), F32),
        ],
        compiler_params=_cparams("parallel"),
        name="post_attn",
    )(o, h_lat, h_ctx, wout, gtm, g_ffn, sh, sc, gtf, wrh, wrl, wsg, wsu, wsd)


DEST_TN = 2048


def _dest_kernel(ps_ref, idx_ref, rank_ref, o_ref):
    idx = idx_ref[...]
    acc = rank_ref[...]
    for e in range(N_EXPERTS):
        acc = acc + jnp.where(idx == e, ps_ref[e], 0)
    o_ref[...] = acc


def _dest(pad_start, top_idx, rank):
    n_tok = top_idx.shape[1]
    blk = pl.BlockSpec((TOP_K, DEST_TN), lambda i: (0, i))
    return pl.pallas_call(
        _dest_kernel,
        grid=(n_tok // DEST_TN,),
        in_specs=[pl.BlockSpec(memory_space=pltpu.SMEM), blk, blk],
        out_specs=blk,
        out_shape=jax.ShapeDtypeStruct((TOP_K, n_tok), jnp.int32),
        compiler_params=_cparams("parallel"),
        name="dest",
    )(pad_start, top_idx, rank)


SC_CORES = 2
SC_SUBCORES = 16
SC_WORKERS = SC_CORES * SC_SUBCORES
SC_DISPATCH_ROWS = 32
SC_GATHER_ROWS = 64
ROW_WORDS = D_MODEL // 2


def _sc_mesh():
    return plsc.VectorSubcoreMesh(core_axis_name="core", subcore_axis_name="subcore")


def _sc_worker():
    return lax.axis_index("subcore") * SC_CORES + lax.axis_index("core")


def _sc_dispatch(f_packed, dest, n_slots):
    n_tok = f_packed.shape[0]
    per_worker = n_tok // SC_WORKERS
    n_chunks = per_worker // SC_DISPATCH_ROWS
    assert n_chunks * SC_DISPATCH_ROWS * SC_WORKERS == n_tok and n_chunks % 2 == 0
    dest4 = dest.reshape(TOP_K, SC_WORKERS, n_chunks, SC_DISPATCH_ROWS)

    def body(f_hbm, dest_hbm, xs_hbm, rows0, rows1, idx_v, load0, load1, scat0, scat1):
        wid = _sc_worker()
        for k in range(TOP_K):
            pltpu.sync_copy(dest_hbm.at[k, wid], idx_v.at[k])

        def load(c, buf, sem):
            start = pl.multiple_of(wid * per_worker + c * SC_DISPATCH_ROWS, 8)
            return pltpu.make_async_copy(f_hbm.at[pl.ds(start, SC_DISPATCH_ROWS)], buf, sem)

        def scatters(c, buf, sem):
            return [pltpu.make_async_copy(buf, xs_hbm.at[idx_v.at[k, c]], sem) for k in range(TOP_K)]

        load(0, rows0, load0).start()

        @pl.loop(0, n_chunks, step=2)
        def _(c):
            @pl.when(c > 0)
            def _():
                for cp in scatters(c - 1, rows1, scat1):
                    cp.wait()

            load(c + 1, rows1, load1).start()
            load(c, rows0, load0).wait()
            for cp in scatters(c, rows0, scat0):
                cp.start()
            load(c + 1, rows1, load1).wait()
            for cp in scatters(c + 1, rows1, scat1):
                cp.start()
            for cp in scatters(c, rows0, scat0):
                cp.wait()

            @pl.when(c + 2 < n_chunks)
            def _():
                load(c + 2, rows0, load0).start()

        for cp in scatters(n_chunks - 1, rows1, scat1):
            cp.wait()

    rows = pltpu.VMEM((SC_DISPATCH_ROWS, ROW_WORDS), jnp.int32)
    return pl.kernel(
        body,
        out_type=jax.ShapeDtypeStruct((n_slots, ROW_WORDS), jnp.int32),
        mesh=_sc_mesh(),
        scratch_types=[rows, rows, pltpu.VMEM((TOP_K, n_chunks, SC_DISPATCH_ROWS), jnp.int32)]
        + [pltpu.SemaphoreType.DMA] * 4,
        name="sc_dispatch",
    )(f_packed, dest4)


def _sc_gather(ys, dest):
    n_idx = dest.shape[0] * dest.shape[1]
    per_worker = n_idx // SC_WORKERS
    n_chunks = per_worker // SC_GATHER_ROWS
    assert n_chunks * SC_GATHER_ROWS * SC_WORKERS == n_idx and n_chunks % 2 == 0
    dest3 = dest.reshape(SC_WORKERS, n_chunks, SC_GATHER_ROWS)

    def body(ys_hbm, dest_hbm, out_hbm, rows0, rows1, idx_v, gat0, gat1, put0, put1):
        wid = _sc_worker()
        pltpu.sync_copy(dest_hbm.at[wid], idx_v)

        def gather(c, buf, sem):
            return pltpu.make_async_copy(ys_hbm.at[idx_v.at[c]], buf, sem)

        def put(c, buf, sem):
            start = pl.multiple_of(wid * per_worker + c * SC_GATHER_ROWS, 8)
            return pltpu.make_async_copy(buf, out_hbm.at[pl.ds(start, SC_GATHER_ROWS)], sem)

        gather(0, rows0, gat0).start()

        @pl.loop(0, n_chunks, step=2)
        def _(c):
            @pl.when(c > 0)
            def _():
                put(c - 1, rows1, put1).wait()

            gather(c + 1, rows1, gat1).start()
            gather(c, rows0, gat0).wait()
            put(c, rows0, put0).start()
            gather(c + 1, rows1, gat1).wait()
            put(c + 1, rows1, put1).start()
            put(c, rows0, put0).wait()

            @pl.when(c + 2 < n_chunks)
            def _():
                gather(c + 2, rows0, gat0).start()

        put(n_chunks - 1, rows1, put1).wait()

    rows = pltpu.VMEM((SC_GATHER_ROWS, ROW_WORDS), jnp.int32)
    return pl.kernel(
        body,
        out_type=jax.ShapeDtypeStruct((n_idx, ROW_WORDS), jnp.int32),
        mesh=_sc_mesh(),
        scratch_types=[rows, rows, pltpu.VMEM((n_chunks, SC_GATHER_ROWS), jnp.int32)]
        + [pltpu.SemaphoreType.DMA] * 4,
        name="sc_gather",
    )(ys, dest3)


X_SLOTS = 3
O_SLOTS = 2


def _moe_kernel(be_ref, nused_ref, slot_ref, next_ref, xs_hbm, wg_hbm, wu_hbm, wd_hbm, ys_hbm,
                xbuf, obuf, wg_s, wu_s, wd_s, wg_f, wu_f, wd_f, xsem, osem, wsem, *, layer):
    n_used = nused_ref[0]

    def x_copy(i, s):
        start = pl.multiple_of(i * MOE_ROWS, MOE_ROWS)
        return pltpu.make_async_copy(xs_hbm.at[pl.ds(start, MOE_ROWS)], xbuf.at[s], xsem.at[s])

    def o_copy(i, s):
        start = pl.multiple_of(i * MOE_ROWS, MOE_ROWS)
        return pltpu.make_async_copy(obuf.at[s], ys_hbm.at[pl.ds(start, MOE_ROWS)], osem.at[s])

    def weight_copies(expert, slot):
        return [pltpu.make_async_copy(src.at[layer, expert], dst.at[slot], wsem.at[slot, j])
                for j, (src, dst) in enumerate(((wg_hbm, wg_f), (wu_hbm, wu_f), (wd_hbm, wd_f)))]

    first_expert = be_ref[0]
    for cp in weight_copies(first_expert, slot_ref[first_expert]):
        cp.start()
    for j in range(X_SLOTS - 1):
        @pl.when(j < n_used)
        def _(j=j):
            x_copy(j, j).start()

    def block(i, carry):
        e = be_ref[i]
        xs = lax.rem(i, X_SLOTS)
        os = lax.rem(i, O_SLOTS)

        @pl.when(jnp.logical_or(i == 0, e != be_ref[jnp.maximum(i - 1, 0)]))
        def _():
            slot = slot_ref[e]
            for cp in weight_copies(e, slot):
                cp.wait()
            wg_s[...] = wg_f[slot].astype(BF16)
            wu_s[...] = wu_f[slot].astype(BF16)
            wd_s[...] = wd_f[slot].astype(BF16)
            nxt = next_ref[e]

            @pl.when(nxt >= 0)
            def _():
                for cp in weight_copies(nxt, 1 - slot):
                    cp.start()

        x_copy(i, xs).wait()

        @pl.when(i + X_SLOTS - 1 < n_used)
        def _():
            x_copy(i + X_SLOTS - 1, lax.rem(i + X_SLOTS - 1, X_SLOTS)).start()

        @pl.when(i >= O_SLOTS)
        def _():
            o_copy(i - O_SLOTS, os).wait()

        half = D_MODEL // 2
        for c in range(MOE_ROWS // MOE_CHUNK):
            rows = pl.ds(c * MOE_CHUNK, MOE_CHUNK)
            lo, hi = _unpack_bf16_pairs(xbuf[xs, rows, :])
            lo, hi = lo.astype(BF16), hi.astype(BF16)
            gate = _dot(lo, wg_s[:half, :]) + _dot(hi, wg_s[half:, :])
            up = _dot(lo, wu_s[:half, :]) + _dot(hi, wu_s[half:, :])
            mid = _silu(gate) * up
            obuf[os, rows, :] = _pack_bf16_pairs(_dot(mid.astype(BF16), wd_s[...]))
        o_copy(i, os).start()
        return carry

    lax.fori_loop(0, n_used, block, 0)
    for j in range(O_SLOTS):
        @pl.when(n_used > j)
        def _(j=j):
            i = n_used - 1 - j
            o_copy(i, lax.rem(i, O_SLOTS)).wait()


def _moe(block_expert, n_used, slot, next_expert, xs, wg, wu, wd, layer):
    hbm = pl.BlockSpec(memory_space=pl.ANY)
    block = (MOE_ROWS, ROW_WORDS)
    grid_spec = pltpu.PrefetchScalarGridSpec(
        num_scalar_prefetch=4,
        grid=(1,),
        in_specs=[hbm, hbm, hbm, hbm],
        out_specs=hbm,
        scratch_shapes=[
            pltpu.VMEM((X_SLOTS,) + block, jnp.int32),
            pltpu.VMEM((O_SLOTS,) + block, jnp.int32),
            pltpu.VMEM((D_MODEL, EXPERT_FF), BF16),
            pltpu.VMEM((D_MODEL, EXPERT_FF), BF16),
            pltpu.VMEM((EXPERT_FF, D_MODEL), BF16),
            pltpu.VMEM((2, D_MODEL, EXPERT_FF), F32),
            pltpu.VMEM((2, D_MODEL, EXPERT_FF), F32),
            pltpu.VMEM((2, EXPERT_FF, D_MODEL), F32),
            pltpu.SemaphoreType.DMA((X_SLOTS,)),
            pltpu.SemaphoreType.DMA((O_SLOTS,)),
            pltpu.SemaphoreType.DMA((2, 3)),
        ],
    )
    return pl.pallas_call(
        functools.partial(_moe_kernel, layer=layer),
        grid_spec=grid_spec,
        out_shape=jax.ShapeDtypeStruct((xs.shape[0], ROW_WORDS), jnp.int32),
        compiler_params=_cparams("arbitrary"),
        name="moe_experts",
    )(block_expert, n_used, slot, next_expert, xs, wg, wu, wd)


def _combine_kernel(y_ref, w_ref, hs_ref, gtf_ref, gfin_ref, o_ref, *, final):
    w = w_ref[...]
    acc_lo = acc_hi = None
    for k in range(TOP_K):
        lo, hi = _unpack_bf16_pairs(y_ref[k])
        wk = w[:, k:k + 1]
        acc_lo = lo * wk if k == 0 else acc_lo + lo * wk
        acc_hi = hi * wk if k == 0 else acc_hi + hi * wk
    out = hs_ref[...] + gtf_ref[0] * jnp.concatenate([acc_lo, acc_hi], axis=1)
    if final:
        out = _rms(out) * gfin_ref[...]
    o_ref[...] = out


def _combine(y, w, hs, gtf, g_final, *, final):
    n_rows = hs.shape[0]
    return pl.pallas_call(
        functools.partial(_combine_kernel, final=final),
        grid=(n_rows // TM_COMB,),
        in_specs=[
            pl.BlockSpec((TOP_K, TM_COMB, ROW_WORDS), lambda i: (0, i, 0)),
            pl.BlockSpec((TM_COMB, TOP_K), lambda i: (i, 0)),
            pl.BlockSpec((TM_COMB, D_MODEL), lambda i: (i, 0)),
            pl.BlockSpec((1, 1, D_MODEL), lambda i: (jnp.minimum(i // (SEQ // TM_COMB), BATCH), 0, 0)),
            pl.BlockSpec((1, D_MODEL), lambda i: (0, 0)),
        ],
        out_specs=pl.BlockSpec((TM_COMB, D_MODEL), lambda i: (i, 0)),
        out_shape=jax.ShapeDtypeStruct((n_rows, D_MODEL), F32),
        compiler_params=_cparams("parallel"),
        name="combine_final" if final else "combine",
    )(y, w, hs, gtf, g_final)


def _rope_tables():
    rows = SEQ // GRID_W

    def angles(rot_dim):
        half = rot_dim // 2
        inv_freq = ROPE_BASE ** (-jnp.arange(0, half, 2, dtype=F32) / half)
        row = jnp.repeat(jnp.arange(rows, dtype=F32), GRID_W)
        col = jnp.tile(jnp.arange(GRID_W, dtype=F32), rows)
        ang_r = row[:, None] * inv_freq
        ang_c = col[:, None] * inv_freq
        return jnp.concatenate([ang_r, ang_r, ang_c, ang_c], axis=-1)

    def signed(sin, quarter):
        sign = jnp.where((jnp.arange(sin.shape[-1]) // quarter) % 2 == 0, -1.0, 1.0)
        return sin * sign

    def with_identity(cos, sin):
        cos = jnp.concatenate([cos, jnp.ones((TM, LANES), F32)], axis=0)
        sin = jnp.concatenate([sin, jnp.zeros((TM, LANES), F32)], axis=0)
        return cos, sin

    a64 = angles(SWA_HEAD_DIM)
    c64 = jnp.tile(jnp.cos(a64), (1, LANES // SWA_HEAD_DIM))
    s64 = jnp.tile(signed(jnp.sin(a64), SWA_HEAD_DIM // 4), (1, LANES // SWA_HEAD_DIM))
    a32 = angles(MLA_ROPE)
    pad_lo = MLA_NOPE
    pad_hi = LANES - MLA_NOPE - MLA_ROPE
    cm = jnp.concatenate([jnp.ones((SEQ, pad_lo), F32), jnp.cos(a32), jnp.ones((SEQ, pad_hi), F32)], axis=-1)
    sm = jnp.concatenate([jnp.zeros((SEQ, pad_lo), F32), signed(jnp.sin(a32), MLA_ROPE // 4),
                          jnp.zeros((SEQ, pad_hi), F32)], axis=-1)
    return with_identity(cm, sm) + with_identity(c64, s64)


def _layer0_weights(wa_in, wa_uq, wa_ukv, wa_out):
    d = D_MODEL
    cq, ckv, kr, qs, ks, vs = jnp.split(
        wa_in, [C_CKV, C_KR, C_KR + MLA_ROPE, C_KR + MLA_ROPE + SWA_HEADS * SWA_HEAD_DIM,
                C_KR + MLA_ROPE + (SWA_HEADS + SWA_KV_HEADS) * SWA_HEAD_DIM], axis=-1)
    kr_pad = jnp.concatenate([jnp.zeros((d, MLA_NOPE), F32), kr,
                              jnp.zeros((d, LANES - MLA_NOPE - MLA_ROPE), F32)], axis=-1)
    qs_pair = qs.reshape(d, SWA_KV_HEADS, SWA_GROUP, SWA_HEAD_DIM).transpose(0, 2, 1, 3).reshape(d, -1)
    win = jnp.concatenate([cq, ckv, kr_pad, qs_pair, ks, vs], axis=-1).astype(BF16)
    uq = wa_uq.reshape(MLA_Q_RANK, MLA_HEADS, MLA_NOPE + MLA_ROPE)
    uq = jnp.pad(uq, ((0, 0), (0, 0), (0, LANES - MLA_NOPE - MLA_ROPE))).reshape(MLA_Q_RANK, -1)
    ukv = wa_ukv.reshape(MLA_KV_RANK, MLA_HEADS, MLA_NOPE + MLA_V)
    uk = jnp.pad(ukv[:, :, :MLA_NOPE], ((0, 0), (0, 0), (0, LANES - MLA_NOPE))).reshape(MLA_KV_RANK, -1)
    uv = ukv[:, :, MLA_NOPE:].reshape(MLA_KV_RANK, -1)
    wukv = jnp.concatenate([uk, uv], axis=-1)
    n_mla = MLA_HEADS * MLA_V
    out_swa = wa_out[n_mla:].reshape(SWA_KV_HEADS, SWA_GROUP, SWA_HEAD_DIM, d).transpose(1, 0, 2, 3)
    wout = jnp.concatenate([wa_out[:n_mla], out_swa.reshape(-1, d)], axis=0)
    return win, uq.astype(BF16), wukv.astype(BF16), wout.astype(BF16)


def _layer1_weights(wc_in):
    per_head = 4 * DIFF_HALF + DIFF_V
    w = wc_in.reshape(D_MODEL, DIFF_HEADS, per_head)
    q = w[:, :, :2 * DIFF_HALF].reshape(D_MODEL, -1)
    k = w[:, :, 2 * DIFF_HALF:4 * DIFF_HALF].reshape(D_MODEL, -1)
    v = w[:, :, 4 * DIFF_HALF:].reshape(D_MODEL, -1)
    return jnp.concatenate([q, k, v], axis=-1).astype(BF16)


def _block_tables(counts, n_blocks):
    padded = (counts + MOE_ROWS - 1) // MOE_ROWS * MOE_ROWS
    pad_end = jnp.cumsum(padded)
    pad_start = (pad_end - padded).astype(jnp.int32)
    block_start = jnp.arange(n_blocks, dtype=jnp.int32) * MOE_ROWS
    block_expert = jnp.minimum((pad_end[None, :] <= block_start[:, None]).sum(axis=1),
                               N_EXPERTS - 1).astype(jnp.int32)
    n_used = (pad_end[-1] // MOE_ROWS).astype(jnp.int32).reshape(1)
    has = padded > 0
    slot = ((jnp.cumsum(has) - 1) & 1).astype(jnp.int32)
    ids = jnp.where(has, jnp.arange(N_EXPERTS, dtype=jnp.int32), N_EXPERTS)
    after = jnp.concatenate([lax.cummin(ids, reverse=True)[1:], jnp.full((1,), N_EXPERTS, jnp.int32)])
    next_expert = jnp.where(after < N_EXPERTS, after, -1).astype(jnp.int32)
    return pad_start, block_expert, n_used, slot, next_expert


def _moe_layer(f_packed, top_idx, top_w, rank, counts, hs, gtf, g_final, wg, wu, wd, layer, *, final):
    n_tok = top_idx.shape[1]
    n_blocks = -(-(n_tok * TOP_K) // MOE_ROWS) + N_EXPERTS
    pad_start, block_expert, n_used, slot, next_expert = _block_tables(counts[:, 0], n_blocks)
    dest = _dest(pad_start, top_idx, rank)
    xs = _sc_dispatch(f_packed, dest, n_blocks * MOE_ROWS)
    ys = _moe(block_expert, n_used, slot, next_expert, xs, wg, wu, wd, layer)
    y = _sc_gather(ys, dest).reshape(TOP_K, n_tok, ROW_WORDS)
    return _combine(y, top_w.T, hs, gtf, g_final, final=final)


def kernel(x, c, ctx, c_ctx, w_ada, b_ada, g_mix, g_ffn, wa_in, ga_q, ga_kv, wa_uq, wa_ukv, wa_sink, wa_out,
           wc_in, lam_q1, lam_k1, lam_q2, lam_k2, gc_sub, wc_out, w_router, b_router, we_gate, we_up, we_down,
           ws_gate, ws_up, ws_down, g_final):
    d = D_MODEL
    x2, ctx2 = x.reshape(N_LAT, d), ctx.reshape(N_CTX, d)
    cc =jnp.concatenate([c, c_ctx[None, :], jnp.zeros((MOD_ROWS - BATCH - 1, d), F32)], axis=0)
    mod = _ada(cc, w_ada, b_ada).reshape(DEPTH, MOD_ROWS, 6, 1, d)

    def mod_vec(layer, j):
        return mod[layer, :, j]

    cm, sm, c64, s64 = _rope_tables()
    g_final2 = g_final.reshape(1, d)

    def router_weights(layer):
        wt = w_router[layer].T
        hi = wt.astype(BF16)
        lo = (wt - hi.astype(F32)).astype(BF16)
        return hi, lo

    def shared_weights(layer):
        return ws_gate[layer].astype(BF16), ws_up[layer].astype(BF16), ws_down[layer].astype(BF16)

    win, wuq, wukv, wout0 = _layer0_weights(wa_in[0], wa_uq[0], wa_ukv[0], wa_out[0])
    qm, km, vm, qs, ks, vs = _proj0(x2, ctx2, g_mix[0].reshape(1, d), mod_vec(0, 0), mod_vec(0, 1), win,
                                    ga_q[0].reshape(1, -1), ga_kv[0].reshape(1, -1), wuq, wukv, cm, sm, c64, s64)
    sink = wa_sink[0].astype(F32)
    o = _mla_attention(qm, km, vm, None, ctx_queries=False)
    o = _mla_attention(qm, km, vm, o, ctx_queries=True)
    o = _swa_attention(sink, qs, ks, vs, o, ctx_queries=False)
    o = _swa_attention(sink, qs, ks, vs, o, ctx_queries=True)
    hs, f, logits_t = _post(
        o, x2, ctx2, wout0, mod_vec(0, 2), g_ffn[0].reshape(1, d), mod_vec(0, 3), mod_vec(0, 4), mod_vec(0, 5),
        *router_weights(0), *shared_weights(0), N_TOK)
    top_idx, top_w, rank, counts = _router(logits_t, b_router[0].reshape(N_EXPERTS, 1))
    h = _moe_layer(f, top_idx, top_w, rank, counts, hs, mod_vec(0, 5), g_final2,
                   we_gate, we_up, we_down, 0, final=False)

    layer = 1
    lam_init = 0.8 - 0.6 * math.exp(-0.3 * layer)
    q1, k1, v1 = _proj1(h, g_mix[1].reshape(1, d), mod_vec(1, 0), mod_vec(1, 1), _layer1_weights(wc_in[0]),
                        c64, s64)
    lam_vecs = jnp.zeros((8, LANES), F32).at[:4, :DIFF_HALF].set(
        jnp.stack([lam_q1[0], lam_k1[0], lam_q2[0], lam_k2[0]]).astype(F32))
    o = _diff_attention(q1, k1, v1, lam_vecs, gc_sub[0].reshape(1, DIFF_V), lam_init)
    hs, f, logits_t = _post(
        o, h, h, wc_out[0].astype(BF16), mod_vec(1, 2), g_ffn[1].reshape(1, d), mod_vec(1, 3), mod_vec(1, 4),
        mod_vec(1, 5), *router_weights(1), *shared_weights(1), N_LAT)
    top_idx, top_w, rank, counts = _router(logits_t, b_router[1].reshape(N_EXPERTS, 1))
    out = _moe_layer(f, top_idx, top_w, rank, counts, hs, mod_vec(1, 5), g_final2,
                     we_gate, we_up, we_down, 1, final=True)
    return out.reshape(BATCH, SEQ, d)
```

```python
import functools
import math

import jax
import jax.numpy as jnp
from jax import lax
from jax.experimental import pallas as pl
from jax.experimental.pallas import tpu as pltpu
from jax.experimental.pallas import tpu_sc as plsc

F32 = jnp.float32
BF16 = jnp.bfloat16

D_MODEL = 1024
BATCH = 8
SEQ = 2048
DEPTH = 2
CTX_LEN = 256
GRID_W = 64
ROPE_BASE = 10000.0
EPS = 1e-6
NEG_INF = -1e30

MLA_HEADS = 8
MLA_Q_RANK = 384
MLA_KV_RANK = 256
MLA_NOPE = 64
MLA_ROPE = 32
MLA_V = 64
SWA_HEADS = 8
SWA_KV_HEADS = 2
SWA_HEAD_DIM = 64
SWA_GROUP = SWA_HEADS // SWA_KV_HEADS
WINDOW = 128
DIFF_HEADS = 8
DIFF_HALF = 64
DIFF_V = 128
N_EXPERTS = 64
N_GROUPS = 8
GROUP_SIZE = N_EXPERTS // N_GROUPS
TOPK_GROUPS = 4
TOP_K = 8
EXPERT_FF = 256
SHARED_FF = 256
ROUTED_SCALE = 2.5

LOG2E = math.log2(math.e)
LANES = 128
N_LAT = BATCH * SEQ
N_CTX = BATCH * CTX_LEN
N_TOK = N_LAT + N_CTX
MOD_ROWS = 16

TM = 1024
TM_CHUNK = 256
LAT_BLOCKS_PER_BATCH = SEQ // TM
TQ_MLA = 512
MLA_PAIRS_PER_STEP = 4
TQ_DIFF = 512
DIFF_HEADS_PER_STEP = 8
SWA_BLOCK = 128
SWA_QBLOCKS = 4
MOE_ROWS = 512
MOE_CHUNK = 256
TM_COMB = 512
VMEM_LIMIT = 56 * 1024 * 1024

C_CQ = 0
C_CKV = C_CQ + MLA_Q_RANK
C_KR = C_CKV + MLA_KV_RANK
C_QS = C_KR + LANES
C_KS = C_QS + SWA_HEADS * SWA_HEAD_DIM
C_VS = C_KS + SWA_KV_HEADS * SWA_HEAD_DIM
C_END = C_VS + SWA_KV_HEADS * SWA_HEAD_DIM


def _cparams(*sem):
    return pltpu.CompilerParams(dimension_semantics=sem, vmem_limit_bytes=VMEM_LIMIT)


def _dot(a, b):
    return jnp.dot(a, b, preferred_element_type=F32)


def _dot_nt(a, b):
    return lax.dot_general(a, b, (((1,), (1,)), ((), ())), preferred_element_type=F32)


def _rms(x):
    return x * lax.rsqrt(jnp.mean(x * x, axis=-1, keepdims=True) + EPS)


def _silu(x):
    return x * jax.nn.sigmoid(x)


def _rope(x, cos, sin_signed, shift):
    n = x.shape[-1]
    lane = lax.broadcasted_iota(jnp.int32, x.shape, 1)
    first = (lane & shift) == 0
    rot = jnp.where(first, pltpu.roll(x, n - shift, 1), pltpu.roll(x, shift, 1))
    return x * cos + rot * sin_signed


def _mod_index(i):
    return jnp.minimum(i // LAT_BLOCKS_PER_BATCH, BATCH)


def _rope_index(i):
    return jnp.where(i < N_LAT // TM, i % LAT_BLOCKS_PER_BATCH, LAT_BLOCKS_PER_BATCH)


ADA_TN = 1536


def _ada_kernel(c_ref, w_ref, b_ref, o_ref):
    s = _silu(c_ref[...]).astype(BF16)
    o_ref[0] = _dot(s, w_ref[0].astype(BF16)) + b_ref[0]


def _ada(cc, w_ada, b_ada):
    n_out = w_ada.shape[-1]
    return pl.pallas_call(
        _ada_kernel,
        grid=(DEPTH, n_out // ADA_TN),
        in_specs=[
            pl.BlockSpec((MOD_ROWS, D_MODEL), lambda l, j: (0, 0)),
            pl.BlockSpec((1, D_MODEL, ADA_TN), lambda l, j: (l, 0, j)),
            pl.BlockSpec((1, 1, ADA_TN), lambda l, j: (l, 0, j)),
        ],
        out_specs=pl.BlockSpec((1, MOD_ROWS, ADA_TN), lambda l, j: (l, 0, j)),
        out_shape=jax.ShapeDtypeStruct((DEPTH, MOD_ROWS, n_out), F32),
        compiler_params=_cparams("parallel", "parallel"),
        name="ada",
    )(cc, w_ada, b_ada.reshape(DEPTH, 1, n_out))


def _token_rows(lat_ref, ctx_ref, rows):
    return jnp.where(pl.program_id(0) < N_LAT // TM, lat_ref[rows, :], ctx_ref[rows, :])


def _token_specs():
    lat_blocks = N_LAT // TM
    return [pl.BlockSpec((TM, D_MODEL), lambda i: (jnp.minimum(i, lat_blocks - 1), 0)),
            pl.BlockSpec((TM, D_MODEL), lambda i: (jnp.maximum(i - lat_blocks, 0), 0))]


def _proj0_kernel(x_ref, ctx_ref, g_ref, sh_ref, sc_ref, win_ref, gq_ref, gkv_ref, wuq_ref, wukv_ref,
                  cm_ref, sm_ref, c64_ref, s64_ref,
                  qm_ref, km_ref, vm_ref, qs_ref, ks_ref, vs_ref):
    q_scale = (MLA_NOPE + MLA_ROPE) ** -0.5 * LOG2E
    s_scale = SWA_HEAD_DIM ** -0.5 * LOG2E
    for c in range(TM // TM_CHUNK):
        rows = slice(c * TM_CHUNK, (c + 1) * TM_CHUNK)
        a = _rms(_token_rows(x_ref, ctx_ref, rows)) * g_ref[...]
        a = a * (1.0 + sc_ref[0]) + sh_ref[0]
        p = _dot(a.astype(BF16), win_ref[...])
        nq = _rms(p[:, C_CQ:C_CKV]) * gq_ref[...]
        nkv = _rms(p[:, C_CKV:C_KR]) * gkv_ref[...]
        q = _dot(nq.astype(BF16), wuq_ref[...])
        kv = _dot(nkv.astype(BF16), wukv_ref[...])
        cm, sm = cm_ref[rows, :], sm_ref[rows, :]
        c64, s64 = c64_ref[rows, :], s64_ref[rows, :]
        kr = _rope(p[:, C_KR:C_QS], cm, sm, MLA_ROPE // 4)
        for h in range(MLA_HEADS):
            sl = slice(h * LANES, (h + 1) * LANES)
            qm_ref[rows, sl] = (_rope(q[:, sl], cm, sm, MLA_ROPE // 4) * q_scale).astype(BF16)
            km_ref[rows, sl] = (kv[:, sl] + kr).astype(BF16)
        vm_ref[rows, :] = kv[:, MLA_HEADS * LANES:].astype(BF16)
        for g in range(SWA_GROUP):
            sl = slice(g * LANES, (g + 1) * LANES)
            qs_ref[rows, sl] = (_rope(p[:, C_QS + g * LANES:C_QS + (g + 1) * LANES], c64, s64,
                                      SWA_HEAD_DIM // 4) * s_scale).astype(BF16)
        ks_ref[rows, :] = _rope(p[:, C_KS:C_VS], c64, s64, SWA_HEAD_DIM // 4).astype(BF16)
        vs_ref[rows, :] = p[:, C_VS:C_END].astype(BF16)


def _proj0(x, ctx, g_mix, sh, sc, win, gq, gkv, wuq, wukv, cm, sm, c64, s64):
    row = lambda i: (i, 0)
    const = lambda i: (0, 0)
    mod = lambda i: (_mod_index(i), 0, 0)
    rope = lambda i: (_rope_index(i), 0)
    widths = (MLA_HEADS * LANES, MLA_HEADS * LANES, MLA_HEADS * MLA_V,
              SWA_HEADS * SWA_HEAD_DIM, SWA_KV_HEADS * SWA_HEAD_DIM, SWA_KV_HEADS * SWA_HEAD_DIM)
    return pl.pallas_call(
        _proj0_kernel,
        grid=(N_TOK // TM,),
        in_specs=_token_specs() + [
            pl.BlockSpec((1, D_MODEL), const),
            pl.BlockSpec((1, 1, D_MODEL), mod),
            pl.BlockSpec((1, 1, D_MODEL), mod),
            pl.BlockSpec(win.shape, const),
            pl.BlockSpec(gq.shape, const),
            pl.BlockSpec(gkv.shape, const),
            pl.BlockSpec(wuq.shape, const),
            pl.BlockSpec(wukv.shape, const),
            pl.BlockSpec((TM, LANES), rope),
            pl.BlockSpec((TM, LANES), rope),
            pl.BlockSpec((TM, LANES), rope),
            pl.BlockSpec((TM, LANES), rope),
        ],
        out_specs=[pl.BlockSpec((TM, w), row) for w in widths],
        out_shape=[jax.ShapeDtypeStruct((N_TOK, w), BF16) for w in widths],
        compiler_params=_cparams("parallel"),
        name="proj0",
    )(x, ctx, g_mix, sh, sc, win, gq, gkv, wuq, wukv, cm, sm, c64, s64)


def _proj1_kernel(h_ref, g_ref, sh_ref, sc_ref, w_ref, c64_ref, s64_ref, q_ref, k_ref, v_ref):
    width = DIFF_HEADS * LANES
    scale = DIFF_HALF ** -0.5 * LOG2E
    for c in range(TM // TM_CHUNK):
        rows = slice(c * TM_CHUNK, (c + 1) * TM_CHUNK)
        a = _rms(h_ref[rows, :]) * g_ref[...]
        a = a * (1.0 + sc_ref[0]) + sh_ref[0]
        p = _dot(a.astype(BF16), w_ref[...])
        c64, s64 = c64_ref[rows, :], s64_ref[rows, :]
        for h in range(DIFF_HEADS):
            sl = slice(h * LANES, (h + 1) * LANES)
            q_ref[rows, sl] = (_rope(p[:, sl], c64, s64, DIFF_HALF // 4) * scale).astype(BF16)
            k_ref[rows, sl] = _rope(p[:, width + h * LANES:width + (h + 1) * LANES], c64, s64,
                                    DIFF_HALF // 4).astype(BF16)
        v_ref[rows, :] = p[:, 2 * width:].astype(BF16)


def _proj1(h, g_mix, sh, sc, w, c64, s64):
    row = lambda i: (i, 0)
    const = lambda i: (0, 0)
    mod = lambda i: (_mod_index(i), 0, 0)
    rope = lambda i: (_rope_index(i), 0)
    width = DIFF_HEADS * LANES
    return pl.pallas_call(
        _proj1_kernel,
        grid=(N_TOK // TM,),
        in_specs=[
            pl.BlockSpec((TM, D_MODEL), row),
            pl.BlockSpec((1, D_MODEL), const),
            pl.BlockSpec((1, 1, D_MODEL), mod),
            pl.BlockSpec((1, 1, D_MODEL), mod),
            pl.BlockSpec(w.shape, const),
            pl.BlockSpec((TM, LANES), rope),
            pl.BlockSpec((TM, LANES), rope),
        ],
        out_specs=[pl.BlockSpec((TM, width), row)] * 3,
        out_shape=[jax.ShapeDtypeStruct((N_TOK, width), BF16)] * 3,
        compiler_params=_cparams("parallel"),
        name="proj1",
    )(h, g_mix, sh, sc, w, c64, s64)


def _mla_kernel(*refs, n_kv):
    q_ref = refs[0]
    k_refs = refs[1:1 + n_kv]
    v_refs = refs[1 + n_kv:1 + 2 * n_kv]
    o_ref = refs[1 + 2 * n_kv]
    lane = lax.broadcasted_iota(jnp.int32, (q_ref.shape[0], LANES), 1)
    own = [lane < MLA_V, lane >= MLA_V]
    for pair in range(MLA_PAIRS_PER_STEP):
        vsl = slice(pair * LANES, (pair + 1) * LANES)
        outs = []
        for hh in range(2):
            sl = slice((2 * pair + hh) * LANES, (2 * pair + hh + 1) * LANES)
            qh = q_ref[:, sl]
            ss = [_dot_nt(qh, k[:, sl]) for k in k_refs]
            m = functools.reduce(jnp.maximum, [jnp.max(s, axis=-1, keepdims=True) for s in ss])
            o = None
            for s, v in zip(ss, v_refs):
                vp = v[:, vsl]
                vlane = lax.broadcasted_iota(jnp.int32, vp.shape, 1)
                keep = (vlane < MLA_V) if hh == 0 else (vlane >= MLA_V)
                vh = jnp.where(keep, vp, jnp.ones_like(vp))
                part = _dot(jnp.exp2(s - m).astype(BF16), vh)
                o = part if o is None else o + part
            outs.append(o / pltpu.roll(o, MLA_V, 1))
        o_ref[:, vsl] = jnp.where(own[0], outs[0], outs[1]).astype(BF16)


def _into_buffer(kernel_fn, buf):
    if buf is None:
        return kernel_fn, [], []

    def without_alias_ref(*refs):
        return kernel_fn(*refs[:-2], refs[-1])

    return without_alias_ref, [pl.BlockSpec(memory_space=pl.ANY)], [buf]


def _mla_attention(qm, km, vm, buf, *, ctx_queries):
    groups = MLA_HEADS // 2 // MLA_PAIRS_PER_STEP
    qk_w = 2 * LANES * MLA_PAIRS_PER_STEP
    v_w = LANES * MLA_PAIRS_PER_STEP
    if ctx_queries:
        tq, n_q = CTX_LEN, 1
        q_map = lambda b, h, i: (N_LAT // CTX_LEN + b, h)
        kv_specs = [(CTX_LEN, lambda b, h, i: (N_LAT // CTX_LEN + b, h))]
    else:
        tq, n_q = TQ_MLA, SEQ // TQ_MLA
        q_map = lambda b, h, i: (b * n_q + i, h)
        kv_specs = [(CTX_LEN, lambda b, h, i: (N_LAT // CTX_LEN + b, h)), (SEQ, lambda b, h, i: (b, h))]
    n_kv = len(kv_specs)
    in_specs = [pl.BlockSpec((tq, qk_w), q_map)]
    in_specs += [pl.BlockSpec((n, qk_w), m) for n, m in kv_specs]
    in_specs += [pl.BlockSpec((n, v_w), m) for n, m in kv_specs]
    kernel_fn, extra_specs, extra_args = _into_buffer(functools.partial(_mla_kernel, n_kv=n_kv), buf)
    n_in = len(in_specs)
    return pl.pallas_call(
        kernel_fn,
        grid=(BATCH, groups, n_q),
        in_specs=in_specs + extra_specs,
        out_specs=pl.BlockSpec((tq, v_w), q_map),
        out_shape=jax.ShapeDtypeStruct((N_TOK, D_MODEL), BF16),
        input_output_aliases={n_in: 0} if extra_args else {},
        compiler_params=_cparams("parallel", "parallel", "parallel"),
        name="mla_ctx" if ctx_queries else "mla_lat",
    )(qm, *([km] * n_kv), *([vm] * n_kv), *extra_args)


def _swa_kernel(sink_ref, *refs, band, n_blocks):
    q_ref = refs[0]
    if band:
        n_kv = SWA_QBLOCKS + 2
        k_blocks = refs[1:1 + n_kv]
        kx = refs[1 + n_kv]
        v_blocks = refs[2 + n_kv:2 + 2 * n_kv]
        vx, o_ref = refs[2 + 2 * n_kv], refs[3 + 2 * n_kv]
        chains, rows = SWA_QBLOCKS, SWA_BLOCK
    else:
        kx, vx, o_ref = refs[1:]
        chains, rows = 1, q_ref.shape[0]
    stacked = SWA_GROUP * rows
    k_ctx = kx[...]
    v_ctx = vx[...]
    lane = lax.broadcasted_iota(jnp.int32, (rows, LANES), 1)
    low = lane < SWA_HEAD_DIM
    row_group = lax.broadcasted_iota(jnp.int32, (stacked, 1), 0) // rows
    for c in range(chains):
        qrows = slice(c * rows, (c + 1) * rows)
        if band:
            n = pl.program_id(1) * SWA_QBLOCKS + c
            k_band = jnp.concatenate([k[...] for k in k_blocks[c:c + 3]], axis=0)
            v_band = jnp.concatenate([v[...] for v in v_blocks[c:c + 3]], axis=0)
            qq = lax.broadcasted_iota(jnp.int32, (stacked, 3 * SWA_BLOCK), 0) & (SWA_BLOCK - 1)
            kk = lax.broadcasted_iota(jnp.int32, (stacked, 3 * SWA_BLOCK), 1)
            rel = kk - SWA_BLOCK - qq
            key_pos = (n - 1) * SWA_BLOCK + kk
            valid = (jnp.abs(rel) <= WINDOW) & (key_pos >= 0) & (key_pos < n_blocks * SWA_BLOCK)
        halves = []
        for hk in range(SWA_KV_HEADS):
            keep = low if hk == 0 else jnp.logical_not(low)
            qh = jnp.concatenate(
                [jnp.where(keep, q_ref[qrows, g * LANES:(g + 1) * LANES], jnp.zeros((rows, LANES), BF16))
                 for g in range(SWA_GROUP)], axis=0)
            sink = jnp.zeros((stacked, 1), F32)
            for g in range(SWA_GROUP):
                sink = jnp.where(row_group == g, sink_ref[hk * SWA_GROUP + g] * LOG2E, sink)
            s_ctx = _dot_nt(qh, k_ctx)
            m = jnp.maximum(jnp.max(s_ctx, axis=-1, keepdims=True), sink)
            if band:
                s_band = jnp.where(valid, _dot_nt(qh, k_band), NEG_INF)
                m = jnp.maximum(m, jnp.max(s_band, axis=-1, keepdims=True))

            def ones_other(v, hk=hk):
                vlane = lax.broadcasted_iota(jnp.int32, v.shape, 1)
                own = (vlane < SWA_HEAD_DIM) if hk == 0 else (vlane >= SWA_HEAD_DIM)
                return jnp.where(own, v, jnp.ones_like(v))

            o = _dot(jnp.exp2(s_ctx - m).astype(BF16), ones_other(v_ctx))
            if band:
                o = o + _dot(jnp.exp2(s_band - m).astype(BF16), ones_other(v_band))
            denom = pltpu.roll(o, SWA_HEAD_DIM, 1) + jnp.exp2(sink - m)
            halves.append(o / denom)
        for g in range(SWA_GROUP):
            rs = slice(g * rows, (g + 1) * rows)
            o_ref[qrows, g * LANES:(g + 1) * LANES] = jnp.where(low, halves[0][rs], halves[1][rs]).astype(BF16)


def _swa_attention(sink, qs, ks, vs, buf, *, ctx_queries):
    width = SWA_HEADS * SWA_HEAD_DIM
    kvw = SWA_KV_HEADS * SWA_HEAD_DIM
    ctx_map = lambda b, i: (N_LAT // CTX_LEN + b, 0)
    smem = pl.BlockSpec(memory_space=pltpu.SMEM)
    if ctx_queries:
        grid = (BATCH, 1)
        in_specs = [smem, pl.BlockSpec((CTX_LEN, width), ctx_map),
                    pl.BlockSpec((CTX_LEN, kvw), ctx_map), pl.BlockSpec((CTX_LEN, kvw), ctx_map)]
        args = (sink, qs, ks, vs)
        out_spec = pl.BlockSpec((CTX_LEN, width), lambda b, i: (N_LAT // CTX_LEN + b, 1))
        n_blocks = 1
    else:
        n_blocks = SEQ // SWA_BLOCK
        steps = n_blocks // SWA_QBLOCKS
        grid = (BATCH, steps)
        q_rows = SWA_QBLOCKS * SWA_BLOCK

        def kv_block(offset):
            return lambda b, i: (b * n_blocks + jnp.clip(i * SWA_QBLOCKS + offset, 0, n_blocks - 1), 0)

        band_specs = [pl.BlockSpec((SWA_BLOCK, kvw), kv_block(off)) for off in range(-1, SWA_QBLOCKS + 1)]
        in_specs = ([smem, pl.BlockSpec((q_rows, width), lambda b, i: (b * steps + i, 0))]
                    + band_specs + [pl.BlockSpec((CTX_LEN, kvw), ctx_map)]
                    + band_specs + [pl.BlockSpec((CTX_LEN, kvw), ctx_map)])
        n_band = len(band_specs)
        args = (sink, qs) + (ks,) * (n_band + 1) + (vs,) * (n_band + 1)
        out_spec = pl.BlockSpec((q_rows, width), lambda b, i: (b * steps + i, 1))
    kernel_fn, extra_specs, extra_args = _into_buffer(
        functools.partial(_swa_kernel, band=not ctx_queries, n_blocks=n_blocks), buf)
    return pl.pallas_call(
        kernel_fn,
        grid=grid,
        in_specs=in_specs + extra_specs,
        out_specs=out_spec,
        out_shape=jax.ShapeDtypeStruct((N_TOK, D_MODEL), BF16),
        input_output_aliases={len(in_specs): 0} if extra_args else {},
        compiler_params=_cparams("parallel", "parallel"),
        name="swa_ctx" if ctx_queries else "swa_lat",
    )(*args, *extra_args)


def _diff_kernel(q_ref, kc_ref, kl_ref, vc_ref, vl_ref, lam_ref, g_ref, o_ref, *, lam_init):
    lam = (jnp.exp(jnp.sum(lam_ref[0:1, :] * lam_ref[1:2, :], axis=-1, keepdims=True))
           - jnp.exp(jnp.sum(lam_ref[2:3, :] * lam_ref[3:4, :], axis=-1, keepdims=True)) + lam_init)
    lane = lax.broadcasted_iota(jnp.int32, (q_ref.shape[0], LANES), 1)
    low = lane < DIFF_HALF
    k_refs = (kc_ref, kl_ref)
    v_refs = (vc_ref, vl_ref)
    for h in range(DIFF_HEADS_PER_STEP):
        sl = slice(h * LANES, (h + 1) * LANES)
        q = q_ref[:, sl]
        zero = jnp.zeros_like(q)

        vx = [jnp.concatenate([v[:, sl], jnp.ones((v.shape[0], LANES), BF16)], axis=1) for v in v_refs]

        def attend(qh):
            ss = [_dot_nt(qh, k[:, sl]) for k in k_refs]
            m = functools.reduce(jnp.maximum, [jnp.max(s, axis=-1, keepdims=True) for s in ss])
            ox = functools.reduce(jnp.add, [_dot(jnp.exp2(s - m).astype(BF16), v) for s, v in zip(ss, vx)])
            return ox[:, :LANES] / ox[:, LANES:]

        o = attend(jnp.where(low, q, zero)) - lam * attend(jnp.where(low, zero, q))
        o = _rms(o) * g_ref[...] * (1.0 - lam_init)
        o_ref[:, sl] = o.astype(BF16)


def _diff_attention(q, k, v, lam_vecs, g_sub, lam_init):
    n_q = SEQ // TQ_DIFF
    q_map = lambda b, h, i: (b * n_q + i, h)
    ctx_map = lambda b, h, i: (N_LAT // CTX_LEN + b, h)
    lat_map = lambda b, h, i: (b, h)
    const = lambda b, h, i: (0, 0)
    width = LANES * DIFF_HEADS_PER_STEP
    return pl.pallas_call(
        functools.partial(_diff_kernel, lam_init=lam_init),
        grid=(BATCH, DIFF_HEADS // DIFF_HEADS_PER_STEP, n_q),
        in_specs=[
            pl.BlockSpec((TQ_DIFF, width), q_map),
            pl.BlockSpec((CTX_LEN, width), ctx_map),
            pl.BlockSpec((SEQ, width), lat_map),
            pl.BlockSpec((CTX_LEN, width), ctx_map),
            pl.BlockSpec((SEQ, width), lat_map),
            pl.BlockSpec(lam_vecs.shape, const),
            pl.BlockSpec(g_sub.shape, const),
        ],
        out_specs=pl.BlockSpec((TQ_DIFF, width), q_map),
        out_shape=jax.ShapeDtypeStruct((N_LAT, DIFF_HEADS * DIFF_V), BF16),
        compiler_params=_cparams("parallel", "parallel", "parallel"),
        name="diff_attn",
    )(q, k, k, v, v, lam_vecs, g_sub)


BF16_BITS = 16
HIGH_HALF_WORD = 0xFFFF0000


def _pack_bf16_valued(x):
    n = x.shape[1] // 2
    lo = pltpu.bitcast(x[:, :n], jnp.uint32)
    hi = pltpu.bitcast(x[:, n:], jnp.uint32)
    return pltpu.bitcast((lo >> BF16_BITS) | hi, jnp.int32)


def _pack_bf16_pairs(x):
    return _pack_bf16_valued(x.astype(BF16).astype(F32))


def _unpack_bf16_pairs(w):
    u = pltpu.bitcast(w, jnp.uint32)
    lo = pltpu.bitcast(u << BF16_BITS, F32)
    hi = pltpu.bitcast(u & jnp.uint32(HIGH_HALF_WORD), F32)
    return lo, hi


def _route(logits_t, bias, cnt_ref, tri_ref, idx_ref, w_ref, rank_ref):
    tm = logits_t.shape[1]
    scores = jax.nn.sigmoid(logits_t)
    biased = scores + bias
    sub = lax.broadcasted_iota(jnp.int32, (GROUP_SIZE, tm), 0).astype(F32)
    grp_scores, grp_biased, grp_index = [], [], []
    group_score = []
    for g in range(N_GROUPS):
        sl = slice(g * GROUP_SIZE, (g + 1) * GROUP_SIZE)
        bg = biased[sl, :]
        grp_scores.append(scores[sl, :])
        grp_biased.append(bg)
        grp_index.append(sub + float(g * GROUP_SIZE))
        m1 = jnp.max(bg, axis=0, keepdims=True)
        first = jnp.min(jnp.where(bg == m1, sub, float(GROUP_SIZE)), axis=0, keepdims=True)
        m2 = jnp.max(jnp.where(sub == first, -jnp.inf, bg), axis=0, keepdims=True)
        group_score.append(m1 + m2)
    keep = [jnp.zeros((1, tm), F32) for _ in range(N_GROUPS)]
    for _ in range(TOPK_GROUPS):
        m = functools.reduce(jnp.maximum, group_score)
        found = jnp.zeros((1, tm), F32)
        for g in range(N_GROUPS):
            hit = jnp.where(group_score[g] == m, 1.0 - found, 0.0)
            keep[g] = keep[g] + hit
            found = found + hit
            group_score[g] = jnp.where(hit > 0.0, -jnp.inf, group_score[g])
    vals = [jnp.where(keep[g] > 0.0, grp_biased[g], NEG_INF) for g in range(N_GROUPS)]
    chosen = [jnp.zeros((GROUP_SIZE, tm), F32) for _ in range(N_GROUPS)]
    picked = []
    for _ in range(TOP_K):
        m = jnp.max(functools.reduce(jnp.maximum, vals), axis=0, keepdims=True)
        cand = [jnp.where(vals[g] == m, grp_index[g], float(N_EXPERTS)) for g in range(N_GROUPS)]
        ei = jnp.min(functools.reduce(jnp.minimum, cand), axis=0, keepdims=True)
        sel = [grp_index[g] == ei for g in range(N_GROUPS)]
        s_k = functools.reduce(jnp.add, [jnp.where(sel[g], grp_scores[g], 0.0) for g in range(N_GROUPS)])
        picked.append((ei, jnp.sum(s_k, axis=0, keepdims=True)))
        vals = [jnp.where(sel[g], -jnp.inf, vals[g]) for g in range(N_GROUPS)]
        chosen = [jnp.where(sel[g], 1.0, chosen[g]) for g in range(N_GROUPS)]
    total = functools.reduce(jnp.add, [s for _, s in picked])
    chosen_all = jnp.concatenate(chosen, axis=0)
    tri = tri_ref[...]
    width = tri.shape[0]
    cnt = cnt_ref[...]
    parts = []
    for j in range(tm // width):
        ch = chosen_all[:, j * width:(j + 1) * width]
        parts.append(_dot(ch.astype(BF16), tri) + cnt)
        cnt = cnt + jnp.sum(ch, axis=1, keepdims=True)
    cnt_ref[...] = cnt
    rank_all = jnp.concatenate(parts, axis=1)
    for k, (ei, s) in enumerate(picked):
        idx_ref[k:k + 1, :] = ei.astype(jnp.int32)
        w_ref[k:k + 1, :] = s / total * ROUTED_SCALE
        r = functools.reduce(jnp.add, [
            jnp.where(grp_index[g] == ei, rank_all[g * GROUP_SIZE:(g + 1) * GROUP_SIZE, :], 0.0)
            for g in range(N_GROUPS)])
        rank_ref[k:k + 1, :] = jnp.sum(r, axis=0, keepdims=True).astype(jnp.int32)


ROUTE_TN = 2048
TRI_N = 256


def _router_kernel(logits_ref, br_ref, tri_ref, idx_ref, w_ref, rank_ref, cnt_out_ref, cnt_ref):
    @pl.when(pl.program_id(0) == 0)
    def _():
        cnt_ref[...] = jnp.zeros_like(cnt_ref)

    _route(logits_ref[...], br_ref[...], cnt_ref, tri_ref, idx_ref, w_ref, rank_ref)
    cnt_out_ref[...] = jnp.broadcast_to(cnt_ref[...], cnt_out_ref.shape).astype(jnp.int32)


def _router(logits_t, br):
    n_tok = logits_t.shape[1]
    col = lambda i: (0, i)
    const = lambda i: (0, 0)
    tri = (jnp.arange(TRI_N)[:, None] < jnp.arange(TRI_N)[None, :]).astype(BF16)
    blk = pl.BlockSpec((TOP_K, ROUTE_TN), col)
    return pl.pallas_call(
        _router_kernel,
        grid=(n_tok // ROUTE_TN,),
        in_specs=[pl.BlockSpec((N_EXPERTS, ROUTE_TN), col), pl.BlockSpec(br.shape, const),
                  pl.BlockSpec(tri.shape, const)],
        out_specs=[blk, blk, blk, pl.BlockSpec((N_EXPERTS, LANES), const)],
        out_shape=[
            jax.ShapeDtypeStruct((TOP_K, n_tok), jnp.int32),
            jax.ShapeDtypeStruct((TOP_K, n_tok), F32),
            jax.ShapeDtypeStruct((TOP_K, n_tok), jnp.int32),
            jax.ShapeDtypeStruct((N_EXPERTS, LANES), jnp.int32),
        ],
        scratch_shapes=[pltpu.VMEM((N_EXPERTS, 1), F32)],
        compiler_params=_cparams("arbitrary"),
        name="router",
    )(logits_t, br, tri)


def _post_kernel(o_ref, hl_ref, hc_ref, wout_ref, gtm_ref, g_ref, sh_ref, sc_ref, gtf_ref,
                 wrh_ref, wrl_ref, wsg_ref, wsu_ref, wsd_ref, hs_ref, f_ref, logits_ref):
    for c in range(TM // TM_CHUNK):
        rows = slice(c * TM_CHUNK, (c + 1) * TM_CHUNK)
        h1 = _token_rows(hl_ref, hc_ref, rows) + gtm_ref[0] * _dot(o_ref[rows, :], wout_ref[...])
        f = _rms(h1) * g_ref[...]
        f = f * (1.0 + sc_ref[0]) + sh_ref[0]
        f_hi = f.astype(BF16)
        f_hi32 = f_hi.astype(F32)
        f_lo = (f - f_hi32).astype(BF16)
        f_ref[rows, :] = _pack_bf16_valued(f_hi32)
        logits_ref[:, rows] = (_dot_nt(wrh_ref[...], f_hi) + _dot_nt(wrh_ref[...], f_lo)
                               + _dot_nt(wrl_ref[...], f_hi))
        mid = _silu(_dot(f_hi, wsg_ref[...])) * _dot(f_hi, wsu_ref[...])
        shared = _dot(mid.astype(BF16), wsd_ref[...])
        hs_ref[rows, :] = h1 + gtf_ref[0] * shared


def _post(o, h_lat, h_ctx, wout, gtm, g_ffn, sh, sc, gtf, wrh, wrl, wsg, wsu, wsd, n_rows):
    row = lambda i: (i, 0)
    const = lambda i: (0, 0)
    mod = lambda i: (_mod_index(i), 0, 0)
    vec = pl.BlockSpec((1, 1, D_MODEL), mod)
    full = lambda a: pl.BlockSpec(a.shape, const)
    return pl.pallas_call(
        _post_kernel,
        grid=(n_rows // TM,),
        in_specs=[pl.BlockSpec((TM, D_MODEL), row)] + _token_specs() + [
            full(wout), vec, full(g_ffn), vec, vec, vec,
            full(wrh), full(wrl), full(wsg), full(wsu), full(wsd),
        ],
        out_specs=[
            pl.BlockSpec((TM, D_MODEL), row),
            pl.BlockSpec((TM, D_MODEL // 2), row),
            pl.BlockSpec((N_EXPERTS, TM), lambda i: (0, i)),
        ],
        out_shape=[
            jax.ShapeDtypeStruct((n_rows, D_MODEL), F32),
            jax.ShapeDtypeStruct((n_rows, D_MODEL // 2), jnp.int32),
            jax.ShapeDtypeStruct((N_EXPERTS, n_rows), F32),
        ],
        compiler_params=_cparams("parallel"),
        name="post_attn",
    )(o, h_lat, h_ctx, wout, gtm, g_ffn, sh, sc, gtf, wrh, wrl, wsg, wsu, wsd)


DEST_TN = 2048


def _dest_kernel(ps_ref, idx_ref, rank_ref, o_ref):
    idx = idx_ref[...]
    acc = rank_ref[...]
    for e in range(N_EXPERTS):
        acc = acc + jnp.where(idx == e, ps_ref[e], 0)
    o_ref[...] = acc


def _dest(pad_start, top_idx, rank):
    n_tok = top_idx.shape[1]
    blk = pl.BlockSpec((TOP_K, DEST_TN), lambda i: (0, i))
    return pl.pallas_call(
        _dest_kernel,
        grid=(n_tok // DEST_TN,),
        in_specs=[pl.BlockSpec(memory_space=pltpu.SMEM), blk, blk],
        out_specs=blk,
        out_shape=jax.ShapeDtypeStruct((TOP_K, n_tok), jnp.int32),
        compiler_params=_cparams("parallel"),
        name="dest",
    )(pad_start, top_idx, rank)


SC_CORES = 2
SC_SUBCORES = 16
SC_WORKERS = SC_CORES * SC_SUBCORES
SC_DISPATCH_ROWS = 32
SC_GATHER_ROWS = 64
ROW_WORDS = D_MODEL // 2


def _sc_mesh():
    return plsc.VectorSubcoreMesh(core_axis_name="core", subcore_axis_name="subcore")


def _sc_worker():
    return lax.axis_index("subcore") * SC_CORES + lax.axis_index("core")


def _sc_dispatch(f_packed, dest, n_slots):
    n_tok = f_packed.shape[0]
    per_worker = n_tok // SC_WORKERS
    n_chunks = per_worker // SC_DISPATCH_ROWS
    assert n_chunks * SC_DISPATCH_ROWS * SC_WORKERS == n_tok and n_chunks % 2 == 0
    dest4 = dest.reshape(TOP_K, SC_WORKERS, n_chunks, SC_DISPATCH_ROWS)

    def body(f_hbm, dest_hbm, xs_hbm, rows0, rows1, idx_v, load0, load1, scat0, scat1):
        wid = _sc_worker()
        for k in range(TOP_K):
            pltpu.sync_copy(dest_hbm.at[k, wid], idx_v.at[k])

        def load(c, buf, sem):
            start = pl.multiple_of(wid * per_worker + c * SC_DISPATCH_ROWS, 8)
            return pltpu.make_async_copy(f_hbm.at[pl.ds(start, SC_DISPATCH_ROWS)], buf, sem)

        def scatters(c, buf, sem):
            return [pltpu.make_async_copy(buf, xs_hbm.at[idx_v.at[k, c]], sem) for k in range(TOP_K)]

        load(0, rows0, load0).start()

        @pl.loop(0, n_chunks, step=2)
        def _(c):
            @pl.when(c > 0)
            def _():
                for cp in scatters(c - 1, rows1, scat1):
                    cp.wait()

            load(c + 1, rows1, load1).start()
            load(c, rows0, load0).wait()
            for cp in scatters(c, rows0, scat0):
                cp.start()
            load(c + 1, rows1, load1).wait()
            for cp in scatters(c + 1, rows1, scat1):
                cp.start()
            for cp in scatters(c, rows0, scat0):
                cp.wait()

            @pl.when(c + 2 < n_chunks)
            def _():
                load(c + 2, rows0, load0).start()

        for cp in scatters(n_chunks - 1, rows1, scat1):
            cp.wait()

    rows = pltpu.VMEM((SC_DISPATCH_ROWS, ROW_WORDS), jnp.int32)
    return pl.kernel(
        body,
        out_type=jax.ShapeDtypeStruct((n_slots, ROW_WORDS), jnp.int32),
        mesh=_sc_mesh(),
        scratch_types=[rows, rows, pltpu.VMEM((TOP_K, n_chunks, SC_DISPATCH_ROWS), jnp.int32)]
        + [pltpu.SemaphoreType.DMA] * 4,
        name="sc_dispatch",
    )(f_packed, dest4)


def _sc_gather(ys, dest):
    n_idx = dest.shape[0] * dest.shape[1]
    per_worker = n_idx // SC_WORKERS
    n_chunks = per_worker // SC_GATHER_ROWS
    assert n_chunks * SC_GATHER_ROWS * SC_WORKERS == n_idx and n_chunks % 2 == 0
    dest3 = dest.reshape(SC_WORKERS, n_chunks, SC_GATHER_ROWS)

    def body(ys_hbm, dest_hbm, out_hbm, rows0, rows1, idx_v, gat0, gat1, put0, put1):
        wid = _sc_worker()
        pltpu.sync_copy(dest_hbm.at[wid], idx_v)

        def gather(c, buf, sem):
            return pltpu.make_async_copy(ys_hbm.at[idx_v.at[c]], buf, sem)

        def put(c, buf, sem):
            start = pl.multiple_of(wid * per_worker + c * SC_GATHER_ROWS, 8)
            return pltpu.make_async_copy(buf, out_hbm.at[pl.ds(start, SC_GATHER_ROWS)], sem)

        gather(0, rows0, gat0).start()

        @pl.loop(0, n_chunks, step=2)
        def _(c):
            @pl.when(c > 0)
            def _():
                put(c - 1, rows1, put1).wait()

            gather(c + 1, rows1, gat1).start()
            gather(c, rows0, gat0).wait()
            put(c, rows0, put0).start()
            gather(c + 1, rows1, gat1).wait()
            put(c + 1, rows1, put1).start()
            put(c, rows0, put0).wait()

            @pl.when(c + 2 < n_chunks)
            def _():
                gather(c + 2, rows0, gat0).start()

        put(n_chunks - 1, rows1, put1).wait()

    rows = pltpu.VMEM((SC_GATHER_ROWS, ROW_WORDS), jnp.int32)
    return pl.kernel(
        body,
        out_type=jax.ShapeDtypeStruct((n_idx, ROW_WORDS), jnp.int32),
        mesh=_sc_mesh(),
        scratch_types=[rows, rows, pltpu.VMEM((n_chunks, SC_GATHER_ROWS), jnp.int32)]
        + [pltpu.SemaphoreType.DMA] * 4,
        name="sc_gather",
    )(ys, dest3)


X_SLOTS = 3
O_SLOTS = 2


def _moe_kernel(be_ref, nused_ref, slot_ref, next_ref, xs_hbm, wg_hbm, wu_hbm, wd_hbm, ys_hbm,
                xbuf, obuf, wg_s, wu_s, wd_s, wg_f, wu_f, wd_f, xsem, osem, wsem, *, layer):
    n_used = nused_ref[0]

    def x_copy(i, s):
        start = pl.multiple_of(i * MOE_ROWS, MOE_ROWS)
        return pltpu.make_async_copy(xs_hbm.at[pl.ds(start, MOE_ROWS)], xbuf.at[s], xsem.at[s])

    def o_copy(i, s):
        start = pl.multiple_of(i * MOE_ROWS, MOE_ROWS)
        return pltpu.make_async_copy(obuf.at[s], ys_hbm.at[pl.ds(start, MOE_ROWS)], osem.at[s])

    def weight_copies(expert, slot):
        return [pltpu.make_async_copy(src.at[layer, expert], dst.at[slot], wsem.at[slot, j])
                for j, (src, dst) in enumerate(((wg_hbm, wg_f), (wu_hbm, wu_f), (wd_hbm, wd_f)))]

    first_expert = be_ref[0]
    for cp in weight_copies(first_expert, slot_ref[first_expert]):
        cp.start()
    for j in range(X_SLOTS - 1):
        @pl.when(j < n_used)
        def _(j=j):
            x_copy(j, j).start()

    def block(i, carry):
        e = be_ref[i]
        xs = lax.rem(i, X_SLOTS)
        os = lax.rem(i, O_SLOTS)

        @pl.when(jnp.logical_or(i == 0, e != be_ref[jnp.maximum(i - 1, 0)]))
        def _():
            slot = slot_ref[e]
            for cp in weight_copies(e, slot):
                cp.wait()
            wg_s[...] = wg_f[slot].astype(BF16)
            wu_s[...] = wu_f[slot].astype(BF16)
            wd_s[...] = wd_f[slot].astype(BF16)
            nxt = next_ref[e]

            @pl.when(nxt >= 0)
            def _():
                for cp in weight_copies(nxt, 1 - slot):
                    cp.start()

        x_copy(i, xs).wait()

        @pl.when(i + X_SLOTS - 1 < n_used)
        def _():
            x_copy(i + X_SLOTS - 1, lax.rem(i + X_SLOTS - 1, X_SLOTS)).start()

        @pl.when(i >= O_SLOTS)
        def _():
            o_copy(i - O_SLOTS, os).wait()

        half = D_MODEL // 2
        for c in range(MOE_ROWS // MOE_CHUNK):
            rows = pl.ds(c * MOE_CHUNK, MOE_CHUNK)
            lo, hi = _unpack_bf16_pairs(xbuf[xs, rows, :])
            lo, hi = lo.astype(BF16), hi.astype(BF16)
            gate = _dot(lo, wg_s[:half, :]) + _dot(hi, wg_s[half:, :])
            up = _dot(lo, wu_s[:half, :]) + _dot(hi, wu_s[half:, :])
            mid = _silu(gate) * up
            obuf[os, rows, :] = _pack_bf16_pairs(_dot(mid.astype(BF16), wd_s[...]))
        o_copy(i, os).start()
        return carry

    lax.fori_loop(0, n_used, block, 0)
    for j in range(O_SLOTS):
        @pl.when(n_used > j)
        def _(j=j):
            i = n_used - 1 - j
            o_copy(i, lax.rem(i, O_SLOTS)).wait()


def _moe(block_expert, n_used, slot, next_expert, xs, wg, wu, wd, layer):
    hbm = pl.BlockSpec(memory_space=pl.ANY)
    block = (MOE_ROWS, ROW_WORDS)
    grid_spec = pltpu.PrefetchScalarGridSpec(
        num_scalar_prefetch=4,
        grid=(1,),
        in_specs=[hbm, hbm, hbm, hbm],
        out_specs=hbm,
        scratch_shapes=[
            pltpu.VMEM((X_SLOTS,) + block, jnp.int32),
            pltpu.VMEM((O_SLOTS,) + block, jnp.int32),
            pltpu.VMEM((D_MODEL, EXPERT_FF), BF16),
            pltpu.VMEM((D_MODEL, EXPERT_FF), BF16),
            pltpu.VMEM((EXPERT_FF, D_MODEL), BF16),
            pltpu.VMEM((2, D_MODEL, EXPERT_FF), F32),
            pltpu.VMEM((2, D_MODEL, EXPERT_FF), F32),
            pltpu.VMEM((2, EXPERT_FF, D_MODEL), F32),
            pltpu.SemaphoreType.DMA((X_SLOTS,)),
            pltpu.SemaphoreType.DMA((O_SLOTS,)),
            pltpu.SemaphoreType.DMA((2, 3)),
        ],
    )
    return pl.pallas_call(
        functools.partial(_moe_kernel, layer=layer),
        grid_spec=grid_spec,
        out_shape=jax.ShapeDtypeStruct((xs.shape[0], ROW_WORDS), jnp.int32),
        compiler_params=_cparams("arbitrary"),
        name="moe_experts",
    )(block_expert, n_used, slot, next_expert, xs, wg, wu, wd)


def _combine_kernel(y_ref, w_ref, hs_ref, gtf_ref, gfin_ref, o_ref, *, final):
    w = w_ref[...]
    acc_lo = acc_hi = None
    for k in range(TOP_K):
        lo, hi = _unpack_bf16_pairs(y_ref[k])
        wk = w[:, k:k + 1]
        acc_lo = lo * wk if k == 0 else acc_lo + lo * wk
        acc_hi = hi * wk if k == 0 else acc_hi + hi * wk
    out = hs_ref[...] + gtf_ref[0] * jnp.concatenate([acc_lo, acc_hi], axis=1)
    if final:
        out = _rms(out) * gfin_ref[...]
    o_ref[...] = out


def _combine(y, w, hs, gtf, g_final, *, final):
    n_rows = hs.shape[0]
    return pl.pallas_call(
        functools.partial(_combine_kernel, final=final),
        grid=(n_rows // TM_COMB,),
        in_specs=[
            pl.BlockSpec((TOP_K, TM_COMB, ROW_WORDS), lambda i: (0, i, 0)),
            pl.BlockSpec((TM_COMB, TOP_K), lambda i: (i, 0)),
            pl.BlockSpec((TM_COMB, D_MODEL), lambda i: (i, 0)),
            pl.BlockSpec((1, 1, D_MODEL), lambda i: (jnp.minimum(i // (SEQ // TM_COMB), BATCH), 0, 0)),
            pl.BlockSpec((1, D_MODEL), lambda i: (0, 0)),
        ],
        out_specs=pl.BlockSpec((TM_COMB, D_MODEL), lambda i: (i, 0)),
        out_shape=jax.ShapeDtypeStruct((n_rows, D_MODEL), F32),
        compiler_params=_cparams("parallel"),
        name="combine_final" if final else "combine",
    )(y, w, hs, gtf, g_final)


def _rope_tables():
    rows = SEQ // GRID_W

    def angles(rot_dim):
        half = rot_dim // 2
        inv_freq = ROPE_BASE ** (-jnp.arange(0, half, 2, dtype=F32) / half)
        row = jnp.repeat(jnp.arange(rows, dtype=F32), GRID_W)
        col = jnp.tile(jnp.arange(GRID_W, dtype=F32), rows)
        ang_r = row[:, None] * inv_freq
        ang_c = col[:, None] * inv_freq
        return jnp.concatenate([ang_r, ang_r, ang_c, ang_c], axis=-1)

    def signed(sin, quarter):
        sign = jnp.where((jnp.arange(sin.shape[-1]) // quarter) % 2 == 0, -1.0, 1.0)
        return sin * sign

    def with_identity(cos, sin):
        cos = jnp.concatenate([cos, jnp.ones((TM, LANES), F32)], axis=0)
        sin = jnp.concatenate([sin, jnp.zeros((TM, LANES), F32)], axis=0)
        return cos, sin

    a64 = angles(SWA_HEAD_DIM)
    c64 = jnp.tile(jnp.cos(a64), (1, LANES // SWA_HEAD_DIM))
    s64 = jnp.tile(signed(jnp.sin(a64), SWA_HEAD_DIM // 4), (1, LANES // SWA_HEAD_DIM))
    a32 = angles(MLA_ROPE)
    pad_lo = MLA_NOPE
    pad_hi = LANES - MLA_NOPE - MLA_ROPE
    cm = jnp.concatenate([jnp.ones((SEQ, pad_lo), F32), jnp.cos(a32), jnp.ones((SEQ, pad_hi), F32)], axis=-1)
    sm = jnp.concatenate([jnp.zeros((SEQ, pad_lo), F32), signed(jnp.sin(a32), MLA_ROPE // 4),
                          jnp.zeros((SEQ, pad_hi), F32)], axis=-1)
    return with_identity(cm, sm) + with_identity(c64, s64)


def _layer0_weights(wa_in, wa_uq, wa_ukv, wa_out):
    d = D_MODEL
    cq, ckv, kr, qs, ks, vs = jnp.split(
        wa_in, [C_CKV, C_KR, C_KR + MLA_ROPE, C_KR + MLA_ROPE + SWA_HEADS * SWA_HEAD_DIM,
                C_KR + MLA_ROPE + (SWA_HEADS + SWA_KV_HEADS) * SWA_HEAD_DIM], axis=-1)
    kr_pad = jnp.concatenate([jnp.zeros((d, MLA_NOPE), F32), kr,
                              jnp.zeros((d, LANES - MLA_NOPE - MLA_ROPE), F32)], axis=-1)
    qs_pair = qs.reshape(d, SWA_KV_HEADS, SWA_GROUP, SWA_HEAD_DIM).transpose(0, 2, 1, 3).reshape(d, -1)
    win = jnp.concatenate([cq, ckv, kr_pad, qs_pair, ks, vs], axis=-1).astype(BF16)
    uq = wa_uq.reshape(MLA_Q_RANK, MLA_HEADS, MLA_NOPE + MLA_ROPE)
    uq = jnp.pad(uq, ((0, 0), (0, 0), (0, LANES - MLA_NOPE - MLA_ROPE))).reshape(MLA_Q_RANK, -1)
    ukv = wa_ukv.reshape(MLA_KV_RANK, MLA_HEADS, MLA_NOPE + MLA_V)
    uk = jnp.pad(ukv[:, :, :MLA_NOPE], ((0, 0), (0, 0), (0, LANES - MLA_NOPE))).reshape(MLA_KV_RANK, -1)
    uv = ukv[:, :, MLA_NOPE:].reshape(MLA_KV_RANK, -1)
    wukv = jnp.concatenate([uk, uv], axis=-1)
    n_mla = MLA_HEADS * MLA_V
    out_swa = wa_out[n_mla:].reshape(SWA_KV_HEADS, SWA_GROUP, SWA_HEAD_DIM, d).transpose(1, 0, 2, 3)
    wout = jnp.concatenate([wa_out[:n_mla], out_swa.reshape(-1, d)], axis=0)
    return win, uq.astype(BF16), wukv.astype(BF16), wout.astype(BF16)


def _layer1_weights(wc_in):
    per_head = 4 * DIFF_HALF + DIFF_V
    w = wc_in.reshape(D_MODEL, DIFF_HEADS, per_head)
    q = w[:, :, :2 * DIFF_HALF].reshape(D_MODEL, -1)
    k = w[:, :, 2 * DIFF_HALF:4 * DIFF_HALF].reshape(D_MODEL, -1)
    v = w[:, :, 4 * DIFF_HALF:].reshape(D_MODEL, -1)
    return jnp.concatenate([q, k, v], axis=-1).astype(BF16)


def _block_tables(counts, n_blocks):
    padded = (counts + MOE_ROWS - 1) // MOE_ROWS * MOE_ROWS
    pad_end = jnp.cumsum(padded)
    pad_start = (pad_end - padded).astype(jnp.int32)
    block_start = jnp.arange(n_blocks, dtype=jnp.int32) * MOE_ROWS
    block_expert = jnp.minimum((pad_end[None, :] <= block_start[:, None]).sum(axis=1),
                               N_EXPERTS - 1).astype(jnp.int32)
    n_used = (pad_end[-1] // MOE_ROWS).astype(jnp.int32).reshape(1)
    has = padded > 0
    slot = ((jnp.cumsum(has) - 1) & 1).astype(jnp.int32)
    ids = jnp.where(has, jnp.arange(N_EXPERTS, dtype=jnp.int32), N_EXPERTS)
    after = jnp.concatenate([lax.cummin(ids, reverse=True)[1:], jnp.full((1,), N_EXPERTS, jnp.int32)])
    next_expert = jnp.where(after < N_EXPERTS, after, -1).astype(jnp.int32)
    return pad_start, block_expert, n_used, slot, next_expert


def _moe_layer(f_packed, top_idx, top_w, rank, counts, hs, gtf, g_final, wg, wu, wd, layer, *, final):
    n_tok = top_idx.shape[1]
    n_blocks = -(-(n_tok * TOP_K) // MOE_ROWS) + N_EXPERTS
    pad_start, block_expert, n_used, slot, next_expert = _block_tables(counts[:, 0], n_blocks)
    dest = _dest(pad_start, top_idx, rank)
    xs = _sc_dispatch(f_packed, dest, n_blocks * MOE_ROWS)
    ys = _moe(block_expert, n_used, slot, next_expert, xs, wg, wu, wd, layer)
    y = _sc_gather(ys, dest).reshape(TOP_K, n_tok, ROW_WORDS)
    return _combine(y, top_w.T, hs, gtf, g_final, final=final)


def kernel(x, c, ctx, c_ctx, w_ada, b_ada, g_mix, g_ffn, wa_in, ga_q, ga_kv, wa_uq, wa_ukv, wa_sink, wa_out,
           wc_in, lam_q1, lam_k1, lam_q2, lam_k2, gc_sub, wc_out, w_router, b_router, we_gate, we_up, we_down,
           ws_gate, ws_up, ws_down, g_final):
    d = D_MODEL
    x2, ctx2 = x.reshape(N_LAT, d), ctx.reshape(N_CTX, d)
    cc =jnp.concatenate([c, c_ctx[None, :], jnp.zeros((MOD_ROWS - BATCH - 1, d), F32)], axis=0)
    mod = _ada(cc, w_ada, b_ada).reshape(DEPTH, MOD_ROWS, 6, 1, d)

    def mod_vec(layer, j):
        return mod[layer, :, j]

    cm, sm, c64, s64 = _rope_tables()
    g_final2 = g_final.reshape(1, d)

    def router_weights(layer):
        wt = w_router[layer].T
        hi = wt.astype(BF16)
        lo = (wt - hi.astype(F32)).astype(BF16)
        return hi, lo

    def shared_weights(layer):
        return ws_gate[layer].astype(BF16), ws_up[layer].astype(BF16), ws_down[layer].astype(BF16)

    win, wuq, wukv, wout0 = _layer0_weights(wa_in[0], wa_uq[0], wa_ukv[0], wa_out[0])
    qm, km, vm, qs, ks, vs = _proj0(x2, ctx2, g_mix[0].reshape(1, d), mod_vec(0, 0), mod_vec(0, 1), win,
                                    ga_q[0].reshape(1, -1), ga_kv[0].reshape(1, -1), wuq, wukv, cm, sm, c64, s64)
    sink = wa_sink[0].astype(F32)
    o = _mla_attention(qm, km, vm, None, ctx_queries=False)
    o = _mla_attention(qm, km, vm, o, ctx_queries=True)
    o = _swa_attention(sink, qs, ks, vs, o, ctx_queries=False)
    o = _swa_attention(sink, qs, ks, vs, o, ctx_queries=True)
    hs, f, logits_t = _post(
        o, x2, ctx2, wout0, mod_vec(0, 2), g_ffn[0].reshape(1, d), mod_vec(0, 3), mod_vec(0, 4), mod_vec(0, 5),
        *router_weights(0), *shared_weights(0), N_TOK)
    top_idx, top_w, rank, counts = _router(logits_t, b_router[0].reshape(N_EXPERTS, 1))
    h = _moe_layer(f, top_idx, top_w, rank, counts, hs, mod_vec(0, 5), g_final2,
                   we_gate, we_up, we_down, 0, final=False)

    layer = 1
    lam_init = 0.8 - 0.6 * math.exp(-0.3 * layer)
    q1, k1, v1 = _proj1(h, g_mix[1].reshape(1, d), mod_vec(1, 0), mod_vec(1, 1), _layer1_weights(wc_in[0]),
                        c64, s64)
    lam_vecs = jnp.zeros((8, LANES), F32).at[:4, :DIFF_HALF].set(
        jnp.stack([lam_q1[0], lam_k1[0], lam_q2[0], lam_k2[0]]).astype(F32))
    o = _diff_attention(q1, k1, v1, lam_vecs, gc_sub[0].reshape(1, DIFF_V), lam_init)
    hs, f, logits_t = _post(
        o, h, h, wc_out[0].astype(BF16), mod_vec(1, 2), g_ffn[1].reshape(1, d), mod_vec(1, 3), mod_vec(1, 4),
        mod_vec(1, 5), *router_weights(1), *shared_weights(1), N_LAT)
    top_idx, top_w, rank, counts = _router(logits_t, b_router[1].reshape(N_EXPERTS, 1))
    out = _moe_layer(f, top_idx, top_w, rank, counts, hs, mod_vec(1, 5), g_final2,
                     we_gate, we_up, we_down, 1, final=True)
    return out.reshape(BATCH, SEQ, d)
```

```python
import functools
import math

import jax
import jax.numpy as jnp
from jax import lax
from jax.experimental import pallas as pl
from jax.experimental.pallas import tpu as pltpu
from jax.experimental.pallas import tpu_sc as plsc

F32 = jnp.float32
BF16 = jnp.bfloat16

D_MODEL = 1024
BATCH = 8
SEQ = 2048
DEPTH = 2
CTX_LEN = 256
GRID_W = 64
ROPE_BASE = 10000.0
EPS = 1e-6
NEG_INF = -1e30

MLA_HEADS = 8
MLA_Q_RANK = 384
MLA_KV_RANK = 256
MLA_NOPE = 64
MLA_ROPE = 32
MLA_V = 64
SWA_HEADS = 8
SWA_KV_HEADS = 2
SWA_HEAD_DIM = 64
SWA_GROUP = SWA_HEADS // SWA_KV_HEADS
WINDOW = 128
DIFF_HEADS = 8
DIFF_HALF = 64
DIFF_V = 128
N_EXPERTS = 64
N_GROUPS = 8
GROUP_SIZE = N_EXPERTS // N_GROUPS
TOPK_GROUPS = 4
TOP_K = 8
EXPERT_FF = 256
SHARED_FF = 256
ROUTED_SCALE = 2.5

LOG2E = math.log2(math.e)
LANES = 128
N_LAT = BATCH * SEQ
N_CTX = BATCH * CTX_LEN
N_TOK = N_LAT + N_CTX
MOD_ROWS = 16

TM = 1024
TM_CHUNK = 256
LAT_BLOCKS_PER_BATCH = SEQ // TM
TQ_MLA = 512
MLA_PAIRS_PER_STEP = 4
TQ_DIFF = 512
DIFF_HEADS_PER_STEP = 8
SWA_BLOCK = 128
SWA_QBLOCKS = 4
MOE_ROWS = 512
MOE_CHUNK = 256
TM_COMB = 512
VMEM_LIMIT = 56 * 1024 * 1024

C_CQ = 0
C_CKV = C_CQ + MLA_Q_RANK
C_KR = C_CKV + MLA_KV_RANK
C_QS = C_KR + LANES
C_KS = C_QS + SWA_HEADS * SWA_HEAD_DIM
C_VS = C_KS + SWA_KV_HEADS * SWA_HEAD_DIM
C_END = C_VS + SWA_KV_HEADS * SWA_HEAD_DIM


def _cparams(*sem):
    return pltpu.CompilerParams(dimension_semantics=sem, vmem_limit_bytes=VMEM_LIMIT)


def _dot(a, b):
    return jnp.dot(a, b, preferred_element_type=F32)


def _dot_nt(a, b):
    return lax.dot_general(a, b, (((1,), (1,)), ((), ())), preferred_element_type=F32)


def _rms(x):
    return x * lax.rsqrt(jnp.mean(x * x, axis=-1, keepdims=True) + EPS)


def _silu(x):
    return x * jax.nn.sigmoid(x)


def _rope(x, cos, sin_signed, shift):
    n = x.shape[-1]
    lane = lax.broadcasted_iota(jnp.int32, x.shape, 1)
    first = (lane & shift) == 0
    rot = jnp.where(first, pltpu.roll(x, n - shift, 1), pltpu.roll(x, shift, 1))
    return x * cos + rot * sin_signed


def _mod_index(i):
    return jnp.minimum(i // LAT_BLOCKS_PER_BATCH, BATCH)


def _rope_index(i):
    return jnp.where(i < N_LAT // TM, i % LAT_BLOCKS_PER_BATCH, LAT_BLOCKS_PER_BATCH)


ADA_TN = 1536


def _ada_kernel(c_ref, w_ref, b_ref, o_ref):
    s = _silu(c_ref[...]).astype(BF16)
    o_ref[0] = _dot(s, w_ref[0].astype(BF16)) + b_ref[0]


def _ada(cc, w_ada, b_ada):
    n_out = w_ada.shape[-1]
    return pl.pallas_call(
        _ada_kernel,
        grid=(DEPTH, n_out // ADA_TN),
        in_specs=[
            pl.BlockSpec((MOD_ROWS, D_MODEL), lambda l, j: (0, 0)),
            pl.BlockSpec((1, D_MODEL, ADA_TN), lambda l, j: (l, 0, j)),
            pl.BlockSpec((1, 1, ADA_TN), lambda l, j: (l, 0, j)),
        ],
        out_specs=pl.BlockSpec((1, MOD_ROWS, ADA_TN), lambda l, j: (l, 0, j)),
        out_shape=jax.ShapeDtypeStruct((DEPTH, MOD_ROWS, n_out), F32),
        compiler_params=_cparams("parallel", "parallel"),
        name="ada",
    )(cc, w_ada, b_ada.reshape(DEPTH, 1, n_out))


def _token_rows(lat_ref, ctx_ref, rows):
    return jnp.where(pl.program_id(0) < N_LAT // TM, lat_ref[rows, :], ctx_ref[rows, :])


def _token_specs():
    lat_blocks = N_LAT // TM
    return [pl.BlockSpec((TM, D_MODEL), lambda i: (jnp.minimum(i, lat_blocks - 1), 0)),
            pl.BlockSpec((TM, D_MODEL), lambda i: (jnp.maximum(i - lat_blocks, 0), 0))]


def _proj0_kernel(x_ref, ctx_ref, g_ref, sh_ref, sc_ref, win_ref, gq_ref, gkv_ref, wuq_ref, wukv_ref,
                  cm_ref, sm_ref, c64_ref, s64_ref,
                  qm_ref, km_ref, vm_ref, qs_ref, ks_ref, vs_ref):
    q_scale = (MLA_NOPE + MLA_ROPE) ** -0.5 * LOG2E
    s_scale = SWA_HEAD_DIM ** -0.5 * LOG2E
    for c in range(TM // TM_CHUNK):
        rows = slice(c * TM_CHUNK, (c + 1) * TM_CHUNK)
        a = _rms(_token_rows(x_ref, ctx_ref, rows)) * g_ref[...]
        a = a * (1.0 + sc_ref[0]) + sh_ref[0]
        p = _dot(a.astype(BF16), win_ref[...])
        nq = _rms(p[:, C_CQ:C_CKV]) * gq_ref[...]
        nkv = _rms(p[:, C_CKV:C_KR]) * gkv_ref[...]
        q = _dot(nq.astype(BF16), wuq_ref[...])
        kv = _dot(nkv.astype(BF16), wukv_ref[...])
        cm, sm = cm_ref[rows, :], sm_ref[rows, :]
        c64, s64 = c64_ref[rows, :], s64_ref[rows, :]
        kr = _rope(p[:, C_KR:C_QS], cm, sm, MLA_ROPE // 4)
        for h in range(MLA_HEADS):
            sl = slice(h * LANES, (h + 1) * LANES)
            qm_ref[rows, sl] = (_rope(q[:, sl], cm, sm, MLA_ROPE // 4) * q_scale).astype(BF16)
            km_ref[rows, sl] = (kv[:, sl] + kr).astype(BF16)
        vm_ref[rows, :] = kv[:, MLA_HEADS * LANES:].astype(BF16)
        for g in range(SWA_GROUP):
            sl = slice(g * LANES, (g + 1) * LANES)
            qs_ref[rows, sl] = (_rope(p[:, C_QS + g * LANES:C_QS + (g + 1) * LANES], c64, s64,
                                      SWA_HEAD_DIM // 4) * s_scale).astype(BF16)
        ks_ref[rows, :] = _rope(p[:, C_KS:C_VS], c64, s64, SWA_HEAD_DIM // 4).astype(BF16)
        vs_ref[rows, :] = p[:, C_VS:C_END].astype(BF16)


def _proj0(x, ctx, g_mix, sh, sc, win, gq, gkv, wuq, wukv, cm, sm, c64, s64):
    row = lambda i: (i, 0)
    const = lambda i: (0, 0)
    mod = lambda i: (_mod_index(i), 0, 0)
    rope = lambda i: (_rope_index(i), 0)
    widths = (MLA_HEADS * LANES, MLA_HEADS * LANES, MLA_HEADS * MLA_V,
              SWA_HEADS * SWA_HEAD_DIM, SWA_KV_HEADS * SWA_HEAD_DIM, SWA_KV_HEADS * SWA_HEAD_DIM)
    return pl.pallas_call(
        _proj0_kernel,
        grid=(N_TOK // TM,),
        in_specs=_token_specs() + [
            pl.BlockSpec((1, D_MODEL), const),
            pl.BlockSpec((1, 1, D_MODEL), mod),
            pl.BlockSpec((1, 1, D_MODEL), mod),
            pl.BlockSpec(win.shape, const),
            pl.BlockSpec(gq.shape, const),
            pl.BlockSpec(gkv.shape, const),
            pl.BlockSpec(wuq.shape, const),
            pl.BlockSpec(wukv.shape, const),
            pl.BlockSpec((TM, LANES), rope),
            pl.BlockSpec((TM, LANES), rope),
            pl.BlockSpec((TM, LANES), rope),
            pl.BlockSpec((TM, LANES), rope),
        ],
        out_specs=[pl.BlockSpec((TM, w), row) for w in widths],
        out_shape=[jax.ShapeDtypeStruct((N_TOK, w), BF16) for w in widths],
        compiler_params=_cparams("parallel"),
        name="proj0",
    )(x, ctx, g_mix, sh, sc, win, gq, gkv, wuq, wukv, cm, sm, c64, s64)


def _proj1_kernel(h_ref, g_ref, sh_ref, sc_ref, w_ref, c64_ref, s64_ref, q_ref, k_ref, v_ref):
    width = DIFF_HEADS * LANES
    scale = DIFF_HALF ** -0.5 * LOG2E
    for c in range(TM // TM_CHUNK):
        rows = slice(c * TM_CHUNK, (c + 1) * TM_CHUNK)
        a = _rms(h_ref[rows, :]) * g_ref[...]
        a = a * (1.0 + sc_ref[0]) + sh_ref[0]
        p = _dot(a.astype(BF16), w_ref[...])
        c64, s64 = c64_ref[rows, :], s64_ref[rows, :]
        for h in range(DIFF_HEADS):
            sl = slice(h * LANES, (h + 1) * LANES)
            q_ref[rows, sl] = (_rope(p[:, sl], c64, s64, DIFF_HALF // 4) * scale).astype(BF16)
            k_ref[rows, sl] = _rope(p[:, width + h * LANES:width + (h + 1) * LANES], c64, s64,
                                    DIFF_HALF // 4).astype(BF16)
        v_ref[rows, :] = p[:, 2 * width:].astype(BF16)


def _proj1(h, g_mix, sh, sc, w, c64, s64):
    row = lambda i: (i, 0)
    const = lambda i: (0, 0)
    mod = lambda i: (_mod_index(i), 0, 0)
    rope = lambda i: (_rope_index(i), 0)
    width = DIFF_HEADS * LANES
    return pl.pallas_call(
        _proj1_kernel,
        grid=(N_TOK // TM,),
        in_specs=[
            pl.BlockSpec((TM, D_MODEL), row),
            pl.BlockSpec((1, D_MODEL), const),
            pl.BlockSpec((1, 1, D_MODEL), mod),
            pl.BlockSpec((1, 1, D_MODEL), mod),
            pl.BlockSpec(w.shape, const),
            pl.BlockSpec((TM, LANES), rope),
            pl.BlockSpec((TM, LANES), rope),
        ],
        out_specs=[pl.BlockSpec((TM, width), row)] * 3,
        out_shape=[jax.ShapeDtypeStruct((N_TOK, width), BF16)] * 3,
        compiler_params=_cparams("parallel"),
        name="proj1",
    )(h, g_mix, sh, sc, w, c64, s64)


def _mla_kernel(*refs, n_kv):
    q_ref = refs[0]
    k_refs = refs[1:1 + n_kv]
    v_refs = refs[1 + n_kv:1 + 2 * n_kv]
    o_ref = refs[1 + 2 * n_kv]
    lane = lax.broadcasted_iota(jnp.int32, (q_ref.shape[0], LANES), 1)
    own = [lane < MLA_V, lane >= MLA_V]
    for pair in range(MLA_PAIRS_PER_STEP):
        vsl = slice(pair * LANES, (pair + 1) * LANES)
        outs = []
        for hh in range(2):
            sl = slice((2 * pair + hh) * LANES, (2 * pair + hh + 1) * LANES)
            qh = q_ref[:, sl]
            ss = [_dot_nt(qh, k[:, sl]) for k in k_refs]
            m = functools.reduce(jnp.maximum, [jnp.max(s, axis=-1, keepdims=True) for s in ss])
            o = None
            for s, v in zip(ss, v_refs):
                vp = v[:, vsl]
                vlane = lax.broadcasted_iota(jnp.int32, vp.shape, 1)
                keep = (vlane < MLA_V) if hh == 0 else (vlane >= MLA_V)
                vh = jnp.where(keep, vp, jnp.ones_like(vp))
                part = _dot(jnp.exp2(s - m).astype(BF16), vh)
                o = part if o is None else o + part
            outs.append(o / pltpu.roll(o, MLA_V, 1))
        o_ref[:, vsl] = jnp.where(own[0], outs[0], outs[1]).astype(BF16)


def _into_buffer(kernel_fn, buf):
    if buf is None:
        return kernel_fn, [], []

    def without_alias_ref(*refs):
        return kernel_fn(*refs[:-2], refs[-1])

    return without_alias_ref, [pl.BlockSpec(memory_space=pl.ANY)], [buf]


def _mla_attention(qm, km, vm, buf, *, ctx_queries):
    groups = MLA_HEADS // 2 // MLA_PAIRS_PER_STEP
    qk_w = 2 * LANES * MLA_PAIRS_PER_STEP
    v_w = LANES * MLA_PAIRS_PER_STEP
    if ctx_queries:
        tq, n_q = CTX_LEN, 1
        q_map = lambda b, h, i: (N_LAT // CTX_LEN + b, h)
        kv_specs = [(CTX_LEN, lambda b, h, i: (N_LAT // CTX_LEN + b, h))]
    else:
        tq, n_q = TQ_MLA, SEQ // TQ_MLA
        q_map = lambda b, h, i: (b * n_q + i, h)
        kv_specs = [(CTX_LEN, lambda b, h, i: (N_LAT // CTX_LEN + b, h)), (SEQ, lambda b, h, i: (b, h))]
    n_kv = len(kv_specs)
    in_specs = [pl.BlockSpec((tq, qk_w), q_map)]
    in_specs += [pl.BlockSpec((n, qk_w), m) for n, m in kv_specs]
    in_specs += [pl.BlockSpec((n, v_w), m) for n, m in kv_specs]
    kernel_fn, extra_specs, extra_args = _into_buffer(functools.partial(_mla_kernel, n_kv=n_kv), buf)
    n_in = len(in_specs)
    return pl.pallas_call(
        kernel_fn,
        grid=(BATCH, groups, n_q),
        in_specs=in_specs + extra_specs,
        out_specs=pl.BlockSpec((tq, v_w), q_map),
        out_shape=jax.ShapeDtypeStruct((N_TOK, D_MODEL), BF16),
        input_output_aliases={n_in: 0} if extra_args else {},
        compiler_params=_cparams("parallel", "parallel", "parallel"),
        name="mla_ctx" if ctx_queries else "mla_lat",
    )(qm, *([km] * n_kv), *([vm] * n_kv), *extra_args)


def _swa_kernel(sink_ref, *refs, band, n_blocks):
    q_ref = refs[0]
    if band:
        n_kv = SWA_QBLOCKS + 2
        k_blocks = refs[1:1 + n_kv]
        kx = refs[1 + n_kv]
        v_blocks = refs[2 + n_kv:2 + 2 * n_kv]
        vx, o_ref = refs[2 + 2 * n_kv], refs[3 + 2 * n_kv]
        chains, rows = SWA_QBLOCKS, SWA_BLOCK
    else:
        kx, vx, o_ref = refs[1:]
        chains, rows = 1, q_ref.shape[0]
    stacked = SWA_GROUP * rows
    k_ctx = kx[...]
    v_ctx = vx[...]
    lane = lax.broadcasted_iota(jnp.int32, (rows, LANES), 1)
    low = lane < SWA_HEAD_DIM
    row_group = lax.broadcasted_iota(jnp.int32, (stacked, 1), 0) // rows
    for c in range(chains):
        qrows = slice(c * rows, (c + 1) * rows)
        if band:
            n = pl.program_id(1) * SWA_QBLOCKS + c
            k_band = jnp.concatenate([k[...] for k in k_blocks[c:c + 3]], axis=0)
            v_band = jnp.concatenate([v[...] for v in v_blocks[c:c + 3]], axis=0)
            qq = lax.broadcasted_iota(jnp.int32, (stacked, 3 * SWA_BLOCK), 0) & (SWA_BLOCK - 1)
            kk = lax.broadcasted_iota(jnp.int32, (stacked, 3 * SWA_BLOCK), 1)
            rel = kk - SWA_BLOCK - qq
            key_pos = (n - 1) * SWA_BLOCK + kk
            valid = (jnp.abs(rel) <= WINDOW) & (key_pos >= 0) & (key_pos < n_blocks * SWA_BLOCK)
        halves = []
        for hk in range(SWA_KV_HEADS):
            keep = low if hk == 0 else jnp.logical_not(low)
            qh = jnp.concatenate(
                [jnp.where(keep, q_ref[qrows, g * LANES:(g + 1) * LANES], jnp.zeros((rows, LANES), BF16))
                 for g in range(SWA_GROUP)], axis=0)
            sink = jnp.zeros((stacked, 1), F32)
            for g in range(SWA_GROUP):
                sink = jnp.where(row_group == g, sink_ref[hk * SWA_GROUP + g] * LOG2E, sink)
            s_ctx = _dot_nt(qh, k_ctx)
            m = jnp.maximum(jnp.max(s_ctx, axis=-1, keepdims=True), sink)
            if band:
                s_band = jnp.where(valid, _dot_nt(qh, k_band), NEG_INF)
                m = jnp.maximum(m, jnp.max(s_band, axis=-1, keepdims=True))

            def ones_other(v, hk=hk):
                vlane = lax.broadcasted_iota(jnp.int32, v.shape, 1)
                own = (vlane < SWA_HEAD_DIM) if hk == 0 else (vlane >= SWA_HEAD_DIM)
                return jnp.where(own, v, jnp.ones_like(v))

            o = _dot(jnp.exp2(s_ctx - m).astype(BF16), ones_other(v_ctx))
            if band:
                o = o + _dot(jnp.exp2(s_band - m).astype(BF16), ones_other(v_band))
            denom = pltpu.roll(o, SWA_HEAD_DIM, 1) + jnp.exp2(sink - m)
            halves.append(o / denom)
        for g in range(SWA_GROUP):
            rs = slice(g * rows, (g + 1) * rows)
            o_ref[qrows, g * LANES:(g + 1) * LANES] = jnp.where(low, halves[0][rs], halves[1][rs]).astype(BF16)


def _swa_attention(sink, qs, ks, vs, buf, *, ctx_queries):
    width = SWA_HEADS * SWA_HEAD_DIM
    kvw = SWA_KV_HEADS * SWA_HEAD_DIM
    ctx_map = lambda b, i: (N_LAT // CTX_LEN + b, 0)
    smem = pl.BlockSpec(memory_space=pltpu.SMEM)
    if ctx_queries:
        grid = (BATCH, 1)
        in_specs = [smem, pl.BlockSpec((CTX_LEN, width), ctx_map),
                    pl.BlockSpec((CTX_LEN, kvw), ctx_map), pl.BlockSpec((CTX_LEN, kvw), ctx_map)]
        args = (sink, qs, ks, vs)
        out_spec = pl.BlockSpec((CTX_LEN, width), lambda b, i: (N_LAT // CTX_LEN + b, 1))
        n_blocks = 1
    else:
        n_blocks = SEQ // SWA_BLOCK
        steps = n_blocks // SWA_QBLOCKS
        grid = (BATCH, steps)
        q_rows = SWA_QBLOCKS * SWA_BLOCK

        def kv_block(offset):
            return lambda b, i: (b * n_blocks + jnp.clip(i * SWA_QBLOCKS + offset, 0, n_blocks - 1), 0)

        band_specs = [pl.BlockSpec((SWA_BLOCK, kvw), kv_block(off)) for off in range(-1, SWA_QBLOCKS + 1)]
        in_specs = ([smem, pl.BlockSpec((q_rows, width), lambda b, i: (b * steps + i, 0))]
                    + band_specs + [pl.BlockSpec((CTX_LEN, kvw), ctx_map)]
                    + band_specs + [pl.BlockSpec((CTX_LEN, kvw), ctx_map)])
        n_band = len(band_specs)
        args = (sink, qs) + (ks,) * (n_band + 1) + (vs,) * (n_band + 1)
        out_spec = pl.BlockSpec((q_rows, width), lambda b, i: (b * steps + i, 1))
    kernel_fn, extra_specs, extra_args = _into_buffer(
        functools.partial(_swa_kernel, band=not ctx_queries, n_blocks=n_blocks), buf)
    return pl.pallas_call(
        kernel_fn,
        grid=grid,
        in_specs=in_specs + extra_specs,
        out_specs=out_spec,
        out_shape=jax.ShapeDtypeStruct((N_TOK, D_MODEL), BF16),
        input_output_aliases={len(in_specs): 0} if extra_args else {},
        compiler_params=_cparams("parallel", "parallel"),
        name="swa_ctx" if ctx_queries else "swa_lat",
    )(*args, *extra_args)


def _diff_kernel(q_ref, kc_ref, kl_ref, vc_ref, vl_ref, lam_ref, g_ref, o_ref, *, lam_init):
    lam = (jnp.exp(jnp.sum(lam_ref[0:1, :] * lam_ref[1:2, :], axis=-1, keepdims=True))
           - jnp.exp(jnp.sum(lam_ref[2:3, :] * lam_ref[3:4, :], axis=-1, keepdims=True)) + lam_init)
    lane = lax.broadcasted_iota(jnp.int32, (q_ref.shape[0], LANES), 1)
    low = lane < DIFF_HALF
    k_refs = (kc_ref, kl_ref)
    v_refs = (vc_ref, vl_ref)
    for h in range(DIFF_HEADS_PER_STEP):
        sl = slice(h * LANES, (h + 1) * LANES)
        q = q_ref[:, sl]
        zero = jnp.zeros_like(q)

        vx = [jnp.concatenate([v[:, sl], jnp.ones((v.shape[0], LANES), BF16)], axis=1) for v in v_refs]

        def attend(qh):
            ss = [_dot_nt(qh, k[:, sl]) for k in k_refs]
            m = functools.reduce(jnp.maximum, [jnp.max(s, axis=-1, keepdims=True) for s in ss])
            ox = functools.reduce(jnp.add, [_dot(jnp.exp2(s - m).astype(BF16), v) for s, v in zip(ss, vx)])
            return ox[:, :LANES] / ox[:, LANES:]

        o = attend(jnp.where(low, q, zero)) - lam * attend(jnp.where(low, zero, q))
        o = _rms(o) * g_ref[...] * (1.0 - lam_init)
        o_ref[:, sl] = o.astype(BF16)


def _diff_attention(q, k, v, lam_vecs, g_sub, lam_init):
    n_q = SEQ // TQ_DIFF
    q_map = lambda b, h, i: (b * n_q + i, h)
    ctx_map = lambda b, h, i: (N_LAT // CTX_LEN + b, h)
    lat_map = lambda b, h, i: (b, h)
    const = lambda b, h, i: (0, 0)
    width = LANES * DIFF_HEADS_PER_STEP
    return pl.pallas_call(
        functools.partial(_diff_kernel, lam_init=lam_init),
        grid=(BATCH, DIFF_HEADS // DIFF_HEADS_PER_STEP, n_q),
        in_specs=[
            pl.BlockSpec((TQ_DIFF, width), q_map),
            pl.BlockSpec((CTX_LEN, width), ctx_map),
            pl.BlockSpec((SEQ, width), lat_map),
            pl.BlockSpec((CTX_LEN, width), ctx_map),
            pl.BlockSpec((SEQ, width), lat_map),
            pl.BlockSpec(lam_vecs.shape, const),
            pl.BlockSpec(g_sub.shape, const),
        ],
        out_specs=pl.BlockSpec((TQ_DIFF, width), q_map),
        out_shape=jax.ShapeDtypeStruct((N_LAT, DIFF_HEADS * DIFF_V), BF16),
        compiler_params=_cparams("parallel", "parallel", "parallel"),
        name="diff_attn",
    )(q, k, k, v, v, lam_vecs, g_sub)


BF16_BITS = 16
HIGH_HALF_WORD = 0xFFFF0000


def _pack_bf16_valued(x):
    n = x.shape[1] // 2
    lo = pltpu.bitcast(x[:, :n], jnp.uint32)
    hi = pltpu.bitcast(x[:, n:], jnp.uint32)
    return pltpu.bitcast((lo >> BF16_BITS) | hi, jnp.int32)


def _pack_bf16_pairs(x):
    return _pack_bf16_valued(x.astype(BF16).astype(F32))


def _unpack_bf16_pairs(w):
    u = pltpu.bitcast(w, jnp.uint32)
    lo = pltpu.bitcast(u << BF16_BITS, F32)
    hi = pltpu.bitcast(u & jnp.uint32(HIGH_HALF_WORD), F32)
    return lo, hi


def _route(logits_t, bias, cnt_ref, tri_ref, idx_ref, w_ref, rank_ref):
    tm = logits_t.shape[1]
    scores = jax.nn.sigmoid(logits_t)
    biased = scores + bias
    sub = lax.broadcasted_iota(jnp.int32, (GROUP_SIZE, tm), 0).astype(F32)
    grp_scores, grp_biased, grp_index = [], [], []
    group_score = []
    for g in range(N_GROUPS):
        sl = slice(g * GROUP_SIZE, (g + 1) * GROUP_SIZE)
        bg = biased[sl, :]
        grp_scores.append(scores[sl, :])
        grp_biased.append(bg)
        grp_index.append(sub + float(g * GROUP_SIZE))
        m1 = jnp.max(bg, axis=0, keepdims=True)
        first = jnp.min(jnp.where(bg == m1, sub, float(GROUP_SIZE)), axis=0, keepdims=True)
        m2 = jnp.max(jnp.where(sub == first, -jnp.inf, bg), axis=0, keepdims=True)
        group_score.append(m1 + m2)
    keep = [jnp.zeros((1, tm), F32) for _ in range(N_GROUPS)]
    for _ in range(TOPK_GROUPS):
        m = functools.reduce(jnp.maximum, group_score)
        found = jnp.zeros((1, tm), F32)
        for g in range(N_GROUPS):
            hit = jnp.where(group_score[g] == m, 1.0 - found, 0.0)
            keep[g] = keep[g] + hit
            found = found + hit
            group_score[g] = jnp.where(hit > 0.0, -jnp.inf, group_score[g])
    vals = [jnp.where(keep[g] > 0.0, grp_biased[g], NEG_INF) for g in range(N_GROUPS)]
    chosen = [jnp.zeros((GROUP_SIZE, tm), F32) for _ in range(N_GROUPS)]
    picked = []
    for _ in range(TOP_K):
        m = jnp.max(functools.reduce(jnp.maximum, vals), axis=0, keepdims=True)
        cand = [jnp.where(vals[g] == m, grp_index[g], float(N_EXPERTS)) for g in range(N_GROUPS)]
        ei = jnp.min(functools.reduce(jnp.minimum, cand), axis=0, keepdims=True)
        sel = [grp_index[g] == ei for g in range(N_GROUPS)]
        s_k = functools.reduce(jnp.add, [jnp.where(sel[g], grp_scores[g], 0.0) for g in range(N_GROUPS)])
        picked.append((ei, jnp.sum(s_k, axis=0, keepdims=True)))
        vals = [jnp.where(sel[g], -jnp.inf, vals[g]) for g in range(N_GROUPS)]
        chosen = [jnp.where(sel[g], 1.0, chosen[g]) for g in range(N_GROUPS)]
    total = functools.reduce(jnp.add, [s for _, s in picked])
    chosen_all = jnp.concatenate(chosen, axis=0)
    tri = tri_ref[...]
    width = tri.shape[0]
    cnt = cnt_ref[...]
    parts = []
    for j in range(tm // width):
        ch = chosen_all[:, j * width:(j + 1) * width]
        parts.append(_dot(ch.astype(BF16), tri) + cnt)
        cnt = cnt + jnp.sum(ch, axis=1, keepdims=True)
    cnt_ref[...] = cnt
    rank_all = jnp.concatenate(parts, axis=1)
    for k, (ei, s) in enumerate(picked):
        idx_ref[k:k + 1, :] = ei.astype(jnp.int32)
        w_ref[k:k + 1, :] = s / total * ROUTED_SCALE
        r = functools.reduce(jnp.add, [
            jnp.where(grp_index[g] == ei, rank_all[g * GROUP_SIZE:(g + 1) * GROUP_SIZE, :], 0.0)
            for g in range(N_GROUPS)])
        rank_ref[k:k + 1, :] = jnp.sum(r, axis=0, keepdims=True).astype(jnp.int32)


ROUTE_TN = 2048
TRI_N = 256


def _router_kernel(logits_ref, br_ref, tri_ref, idx_ref, w_ref, rank_ref, cnt_out_ref, cnt_ref):
    @pl.when(pl.program_id(0) == 0)
    def _():
        cnt_ref[...] = jnp.zeros_like(cnt_ref)

    _route(logits_ref[...], br_ref[...], cnt_ref, tri_ref, idx_ref, w_ref, rank_ref)
    cnt_out_ref[...] = jnp.broadcast_to(cnt_ref[...], cnt_out_ref.shape).astype(jnp.int32)


def _router(logits_t, br):
    n_tok = logits_t.shape[1]
    col = lambda i: (0, i)
    const = lambda i: (0, 0)
    tri = (jnp.arange(TRI_N)[:, None] < jnp.arange(TRI_N)[None, :]).astype(BF16)
    blk = pl.BlockSpec((TOP_K, ROUTE_TN), col)
    return pl.pallas_call(
        _router_kernel,
        grid=(n_tok // ROUTE_TN,),
        in_specs=[pl.BlockSpec((N_EXPERTS, ROUTE_TN), col), pl.BlockSpec(br.shape, const),
                  pl.BlockSpec(tri.shape, const)],
        out_specs=[blk, blk, blk, pl.BlockSpec((N_EXPERTS, LANES), const)],
        out_shape=[
            jax.ShapeDtypeStruct((TOP_K, n_tok), jnp.int32),
            jax.ShapeDtypeStruct((TOP_K, n_tok), F32),
            jax.ShapeDtypeStruct((TOP_K, n_tok), jnp.int32),
            jax.ShapeDtypeStruct((N_EXPERTS, LANES), jnp.int32),
        ],
        scratch_shapes=[pltpu.VMEM((N_EXPERTS, 1), F32)],
        compiler_params=_cparams("arbitrary"),
        name="router",
    )(logits_t, br, tri)


def _post_kernel(o_ref, hl_ref, hc_ref, wout_ref, gtm_ref, g_ref, sh_ref, sc_ref, gtf_ref,
                 wrh_ref, wrl_ref, wsg_ref, wsu_ref, wsd_ref, hs_ref, f_ref, logits_ref):
    for c in range(TM // TM_CHUNK):
        rows = slice(c * TM_CHUNK, (c + 1) * TM_CHUNK)
        h1 = _token_rows(hl_ref, hc_ref, rows) + gtm_ref[0] * _dot(o_ref[rows, :], wout_ref[...])
        f = _rms(h1) * g_ref[...]
        f = f * (1.0 + sc_ref[0]) + sh_ref[0]
        f_hi = f.astype(BF16)
        f_hi32 = f_hi.astype(F32)
        f_lo = (f - f_hi32).astype(BF16)
        f_ref[rows, :] = _pack_bf16_valued(f_hi32)
        logits_ref[:, rows] = (_dot_nt(wrh_ref[...], f_hi) + _dot_nt(wrh_ref[...], f_lo)
                               + _dot_nt(wrl_ref[...], f_hi))
        mid = _silu(_dot(f_hi, wsg_ref[...])) * _dot(f_hi, wsu_ref[...])
        shared = _dot(mid.astype(BF16), wsd_ref[...])
        hs_ref[rows, :] = h1 + gtf_ref[0] * shared


def _post(o, h_lat, h_ctx, wout, gtm, g_ffn, sh, sc, gtf, wrh, wrl, wsg, wsu, wsd, n_rows):
    row = lambda i: (i, 0)
    const = lambda i: (0, 0)
    mod = lambda i: (_mod_index(i), 0, 0)
    vec = pl.BlockSpec((1, 1, D_MODEL), mod)
    full = lambda a: pl.BlockSpec(a.shape, const)
    return pl.pallas_call(
        _post_kernel,
        grid=(n_rows // TM,),
        in_specs=[pl.BlockSpec((TM, D_MODEL), row)] + _token_specs() + [
            full(wout), vec, full(g_ffn), vec, vec, vec,
            full(wrh), full(wrl), full(wsg), full(wsu), full(wsd),
        ],
        out_specs=[
            pl.BlockSpec((TM, D_MODEL), row),
            pl.BlockSpec((TM, D_MODEL // 2), row),
            pl.BlockSpec((N_EXPERTS, TM), lambda i: (0, i)),
        ],
        out_shape=[
            jax.ShapeDtypeStruct((n_rows, D_MODEL), F32),
            jax.ShapeDtypeStruct((n_rows, D_MODEL // 2), jnp.int32),
            jax.ShapeDtypeStruct((N_EXPERTS, n_rows), F32),
        ],
        compiler_params=_cparams("parallel"),
        name="post_attn",
    )(o, h_lat, h_ctx, wout, gtm, g_ffn, sh, sc, gtf, wrh, wrl, wsg, wsu, wsd)


DEST_TN = 2048


def _dest_kernel(ps_ref, idx_ref, rank_ref, o_ref):
    idx = idx_ref[...]
    acc = rank_ref[...]
    for e in range(N_EXPERTS):
        acc = acc + jnp.where(idx == e, ps_ref[e], 0)
    o_ref[...] = acc


def _dest(pad_start, top_idx, rank):
    n_tok = top_idx.shape[1]
    blk = pl.BlockSpec((TOP_K, DEST_TN), lambda i: (0, i))
    return pl.pallas_call(
        _dest_kernel,
        grid=(n_tok // DEST_TN,),
        in_specs=[pl.BlockSpec(memory_space=pltpu.SMEM), blk, blk],
        out_specs=blk,
        out_shape=jax.ShapeDtypeStruct((TOP_K, n_tok), jnp.int32),
        compiler_params=_cparams("parallel"),
        name="dest",
    )(pad_start, top_idx, rank)


SC_CORES = 2
SC_SUBCORES = 16
SC_WORKERS = SC_CORES * SC_SUBCORES
SC_DISPATCH_ROWS = 32
SC_GATHER_ROWS = 64
ROW_WORDS = D_MODEL // 2


def _sc_mesh():
    return plsc.VectorSubcoreMesh(core_axis_name="core", subcore_axis_name="subcore")


def _sc_worker():
    return lax.axis_index("subcore") * SC_CORES + lax.axis_index("core")


def _sc_dispatch(f_packed, dest, n_slots):
    n_tok = f_packed.shape[0]
    per_worker = n_tok // SC_WORKERS
    n_chunks = per_worker // SC_DISPATCH_ROWS
    assert n_chunks * SC_DISPATCH_ROWS * SC_WORKERS == n_tok and n_chunks % 2 == 0
    dest4 = dest.reshape(TOP_K, SC_WORKERS, n_chunks, SC_DISPATCH_ROWS)

    def body(f_hbm, dest_hbm, xs_hbm, rows0, rows1, idx_v, load0, load1, scat0, scat1):
        wid = _sc_worker()
        for k in range(TOP_K):
            pltpu.sync_copy(dest_hbm.at[k, wid], idx_v.at[k])

        def load(c, buf, sem):
            start = pl.multiple_of(wid * per_worker + c * SC_DISPATCH_ROWS, 8)
            return pltpu.make_async_copy(f_hbm.at[pl.ds(start, SC_DISPATCH_ROWS)], buf, sem)

        def scatters(c, buf, sem):
            return [pltpu.make_async_copy(buf, xs_hbm.at[idx_v.at[k, c]], sem) for k in range(TOP_K)]

        load(0, rows0, load0).start()

        @pl.loop(0, n_chunks, step=2)
        def _(c):
            @pl.when(c > 0)
            def _():
                for cp in scatters(c - 1, rows1, scat1):
                    cp.wait()

            load(c + 1, rows1, load1).start()
            load(c, rows0, load0).wait()
            for cp in scatters(c, rows0, scat0):
                cp.start()
            load(c + 1, rows1, load1).wait()
            for cp in scatters(c + 1, rows1, scat1):
                cp.start()
            for cp in scatters(c, rows0, scat0):
                cp.wait()

            @pl.when(c + 2 < n_chunks)
            def _():
                load(c + 2, rows0, load0).start()

        for cp in scatters(n_chunks - 1, rows1, scat1):
            cp.wait()

    rows = pltpu.VMEM((SC_DISPATCH_ROWS, ROW_WORDS), jnp.int32)
    return pl.kernel(
        body,
        out_type=jax.ShapeDtypeStruct((n_slots, ROW_WORDS), jnp.int32),
        mesh=_sc_mesh(),
        scratch_types=[rows, rows, pltpu.VMEM((TOP_K, n_chunks, SC_DISPATCH_ROWS), jnp.int32)]
        + [pltpu.SemaphoreType.DMA] * 4,
        name="sc_dispatch",
    )(f_packed, dest4)


def _sc_gather(ys, dest):
    n_idx = dest.shape[0] * dest.shape[1]
    per_worker = n_idx // SC_WORKERS
    n_chunks = per_worker // SC_GATHER_ROWS
    assert n_chunks * SC_GATHER_ROWS * SC_WORKERS == n_idx and n_chunks % 2 == 0
    dest3 = dest.reshape(SC_WORKERS, n_chunks, SC_GATHER_ROWS)

    def body(ys_hbm, dest_hbm, out_hbm, rows0, rows1, idx_v, gat0, gat1, put0, put1):
        wid = _sc_worker()
        pltpu.sync_copy(dest_hbm.at[wid], idx_v)

        def gather(c, buf, sem):
            return pltpu.make_async_copy(ys_hbm.at[idx_v.at[c]], buf, sem)

        def put(c, buf, sem):
            start = pl.multiple_of(wid * per_worker + c * SC_GATHER_ROWS, 8)
            return pltpu.make_async_copy(buf, out_hbm.at[pl.ds(start, SC_GATHER_ROWS)], sem)

        gather(0, rows0, gat0).start()

        @pl.loop(0, n_chunks, step=2)
        def _(c):
            @pl.when(c > 0)
            def _():
                put(c - 1, rows1, put1).wait()

            gather(c + 1, rows1, gat1).start()
            gather(c, rows0, gat0).wait()
            put(c, rows0, put0).start()
            gather(c + 1, rows1, gat1).wait()
            put(c + 1, rows1, put1).start()
            put(c, rows0, put0).wait()

            @pl.when(c + 2 < n_chunks)
            def _():
                gather(c + 2, rows0, gat0).start()

        put(n_chunks - 1, rows1, put1).wait()

    rows = pltpu.VMEM((SC_GATHER_ROWS, ROW_WORDS), jnp.int32)
    return pl.kernel(
        body,
        out_type=jax.ShapeDtypeStruct((n_idx, ROW_WORDS), jnp.int32),
        mesh=_sc_mesh(),
        scratch_types=[rows, rows, pltpu.VMEM((n_chunks, SC_GATHER_ROWS), jnp.int32)]
        + [pltpu.SemaphoreType.DMA] * 4,
        name="sc_gather",
    )(ys, dest3)


X_SLOTS = 4
O_SLOTS = 3


def _moe_kernel(be_ref, nused_ref, slot_ref, next_ref, xs_hbm, wg_hbm, wu_hbm, wd_hbm, ys_hbm,
                xbuf, obuf, wg_s, wu_s, wd_s, wg_f, wu_f, wd_f, xsem, osem, wsem, *, layer):
    n_used = nused_ref[0]

    def x_copy(i, s):
        start = pl.multiple_of(i * MOE_ROWS, MOE_ROWS)
        return pltpu.make_async_copy(xs_hbm.at[pl.ds(start, MOE_ROWS)], xbuf.at[s], xsem.at[s])

    def o_copy(i, s):
        start = pl.multiple_of(i * MOE_ROWS, MOE_ROWS)
        return pltpu.make_async_copy(obuf.at[s], ys_hbm.at[pl.ds(start, MOE_ROWS)], osem.at[s])

    def weight_copies(expert, slot):
        return [pltpu.make_async_copy(src.at[layer, expert], dst.at[slot], wsem.at[slot, j])
                for j, (src, dst) in enumerate(((wg_hbm, wg_f), (wu_hbm, wu_f), (wd_hbm, wd_f)))]

    first_expert = be_ref[0]
    for cp in weight_copies(first_expert, slot_ref[first_expert]):
        cp.start()
    for j in range(X_SLOTS - 1):
        @pl.when(j < n_used)
        def _(j=j):
            x_copy(j, j).start()

    def block(i, carry):
        e = be_ref[i]
        xs = lax.rem(i, X_SLOTS)
        os = lax.rem(i, O_SLOTS)

        @pl.when(jnp.logical_or(i == 0, e != be_ref[jnp.maximum(i - 1, 0)]))
        def _():
            slot = slot_ref[e]
            for cp in weight_copies(e, slot):
                cp.wait()
            wg_s[...] = wg_f[slot].astype(BF16)
            wu_s[...] = wu_f[slot].astype(BF16)
            wd_s[...] = wd_f[slot].astype(BF16)
            nxt = next_ref[e]

            @pl.when(nxt >= 0)
            def _():
                for cp in weight_copies(nxt, 1 - slot):
                    cp.start()

        x_copy(i, xs).wait()

        @pl.when(i + X_SLOTS - 1 < n_used)
        def _():
            x_copy(i + X_SLOTS - 1, lax.rem(i + X_SLOTS - 1, X_SLOTS)).start()

        @pl.when(i >= O_SLOTS)
        def _():
            o_copy(i - O_SLOTS, os).wait()

        half = D_MODEL // 2
        for c in range(MOE_ROWS // MOE_CHUNK):
            rows = pl.ds(c * MOE_CHUNK, MOE_CHUNK)
            lo, hi = _unpack_bf16_pairs(xbuf[xs, rows, :])
            lo, hi = lo.astype(BF16), hi.astype(BF16)
            gate = _dot(lo, wg_s[:half, :]) + _dot(hi, wg_s[half:, :])
            up = _dot(lo, wu_s[:half, :]) + _dot(hi, wu_s[half:, :])
            mid = _silu(gate) * up
            obuf[os, rows, :] = _pack_bf16_pairs(_dot(mid.astype(BF16), wd_s[...]))
        o_copy(i, os).start()
        return carry

    lax.fori_loop(0, n_used, block, 0)
    for j in range(O_SLOTS):
        @pl.when(n_used > j)
        def _(j=j):
            i = n_used - 1 - j
            o_copy(i, lax.rem(i, O_SLOTS)).wait()


def _moe(block_expert, n_used, slot, next_expert, xs, wg, wu, wd, layer):
    hbm = pl.BlockSpec(memory_space=pl.ANY)
    block = (MOE_ROWS, ROW_WORDS)
    grid_spec = pltpu.PrefetchScalarGridSpec(
        num_scalar_prefetch=4,
        grid=(1,),
        in_specs=[hbm, hbm, hbm, hbm],
        out_specs=hbm,
        scratch_shapes=[
            pltpu.VMEM((X_SLOTS,) + block, jnp.int32),
            pltpu.VMEM((O_SLOTS,) + block, jnp.int32),
            pltpu.VMEM((D_MODEL, EXPERT_FF), BF16),
            pltpu.VMEM((D_MODEL, EXPERT_FF), BF16),
            pltpu.VMEM((EXPERT_FF, D_MODEL), BF16),
            pltpu.VMEM((2, D_MODEL, EXPERT_FF), F32),
            pltpu.VMEM((2, D_MODEL, EXPERT_FF), F32),
            pltpu.VMEM((2, EXPERT_FF, D_MODEL), F32),
            pltpu.SemaphoreType.DMA((X_SLOTS,)),
            pltpu.SemaphoreType.DMA((O_SLOTS,)),
            pltpu.SemaphoreType.DMA((2, 3)),
        ],
    )
    return pl.pallas_call(
        functools.partial(_moe_kernel, layer=layer),
        grid_spec=grid_spec,
        out_shape=jax.ShapeDtypeStruct((xs.shape[0], ROW_WORDS), jnp.int32),
        compiler_params=_cparams("arbitrary"),
        name="moe_experts",
    )(block_expert, n_used, slot, next_expert, xs, wg, wu, wd)


def _combine_kernel(y_ref, w_ref, hs_ref, gtf_ref, gfin_ref, o_ref, *, final):
    w = w_ref[...]
    acc_lo = acc_hi = None
    for k in range(TOP_K):
        lo, hi = _unpack_bf16_pairs(y_ref[k])
        wk = w[:, k:k + 1]
        acc_lo = lo * wk if k == 0 else acc_lo + lo * wk
        acc_hi = hi * wk if k == 0 else acc_hi + hi * wk
    out = hs_ref[...] + gtf_ref[0] * jnp.concatenate([acc_lo, acc_hi], axis=1)
    if final:
        out = _rms(out) * gfin_ref[...]
    o_ref[...] = out


def _combine(y, w, hs, gtf, g_final, *, final):
    n_rows = hs.shape[0]
    return pl.pallas_call(
        functools.partial(_combine_kernel, final=final),
        grid=(n_rows // TM_COMB,),
        in_specs=[
            pl.BlockSpec((TOP_K, TM_COMB, ROW_WORDS), lambda i: (0, i, 0)),
            pl.BlockSpec((TM_COMB, TOP_K), lambda i: (i, 0)),
            pl.BlockSpec((TM_COMB, D_MODEL), lambda i: (i, 0)),
            pl.BlockSpec((1, 1, D_MODEL), lambda i: (jnp.minimum(i // (SEQ // TM_COMB), BATCH), 0, 0)),
            pl.BlockSpec((1, D_MODEL), lambda i: (0, 0)),
        ],
        out_specs=pl.BlockSpec((TM_COMB, D_MODEL), lambda i: (i, 0)),
        out_shape=jax.ShapeDtypeStruct((n_rows, D_MODEL), F32),
        compiler_params=_cparams("parallel"),
        name="combine_final" if final else "combine",
    )(y, w, hs, gtf, g_final)


def _rope_tables():
    rows = SEQ // GRID_W

    def angles(rot_dim):
        half = rot_dim // 2
        inv_freq = ROPE_BASE ** (-jnp.arange(0, half, 2, dtype=F32) / half)
        row = jnp.repeat(jnp.arange(rows, dtype=F32), GRID_W)
        col = jnp.tile(jnp.arange(GRID_W, dtype=F32), rows)
        ang_r = row[:, None] * inv_freq
        ang_c = col[:, None] * inv_freq
        return jnp.concatenate([ang_r, ang_r, ang_c, ang_c], axis=-1)

    def signed(sin, quarter):
        sign = jnp.where((jnp.arange(sin.shape[-1]) // quarter) % 2 == 0, -1.0, 1.0)
        return sin * sign

    def with_identity(cos, sin):
        cos = jnp.concatenate([cos, jnp.ones((TM, LANES), F32)], axis=0)
        sin = jnp.concatenate([sin, jnp.zeros((TM, LANES), F32)], axis=0)
        return cos, sin

    a64 = angles(SWA_HEAD_DIM)
    c64 = jnp.tile(jnp.cos(a64), (1, LANES // SWA_HEAD_DIM))
    s64 = jnp.tile(signed(jnp.sin(a64), SWA_HEAD_DIM // 4), (1, LANES // SWA_HEAD_DIM))
    a32 = angles(MLA_ROPE)
    pad_lo = MLA_NOPE
    pad_hi = LANES - MLA_NOPE - MLA_ROPE
    cm = jnp.concatenate([jnp.ones((SEQ, pad_lo), F32), jnp.cos(a32), jnp.ones((SEQ, pad_hi), F32)], axis=-1)
    sm = jnp.concatenate([jnp.zeros((SEQ, pad_lo), F32), signed(jnp.sin(a32), MLA_ROPE // 4),
                          jnp.zeros((SEQ, pad_hi), F32)], axis=-1)
    return with_identity(cm, sm) + with_identity(c64, s64)


def _layer0_weights(wa_in, wa_uq, wa_ukv, wa_out):
    d = D_MODEL
    cq, ckv, kr, qs, ks, vs = jnp.split(
        wa_in, [C_CKV, C_KR, C_KR + MLA_ROPE, C_KR + MLA_ROPE + SWA_HEADS * SWA_HEAD_DIM,
                C_KR + MLA_ROPE + (SWA_HEADS + SWA_KV_HEADS) * SWA_HEAD_DIM], axis=-1)
    kr_pad = jnp.concatenate([jnp.zeros((d, MLA_NOPE), F32), kr,
                              jnp.zeros((d, LANES - MLA_NOPE - MLA_ROPE), F32)], axis=-1)
    qs_pair = qs.reshape(d, SWA_KV_HEADS, SWA_GROUP, SWA_HEAD_DIM).transpose(0, 2, 1, 3).reshape(d, -1)
    win = jnp.concatenate([cq, ckv, kr_pad, qs_pair, ks, vs], axis=-1).astype(BF16)
    uq = wa_uq.reshape(MLA_Q_RANK, MLA_HEADS, MLA_NOPE + MLA_ROPE)
    uq = jnp.pad(uq, ((0, 0), (0, 0), (0, LANES - MLA_NOPE - MLA_ROPE))).reshape(MLA_Q_RANK, -1)
    ukv = wa_ukv.reshape(MLA_KV_RANK, MLA_HEADS, MLA_NOPE + MLA_V)
    uk = jnp.pad(ukv[:, :, :MLA_NOPE], ((0, 0), (0, 0), (0, LANES - MLA_NOPE))).reshape(MLA_KV_RANK, -1)
    uv = ukv[:, :, MLA_NOPE:].reshape(MLA_KV_RANK, -1)
    wukv = jnp.concatenate([uk, uv], axis=-1)
    n_mla = MLA_HEADS * MLA_V
    out_swa = wa_out[n_mla:].reshape(SWA_KV_HEADS, SWA_GROUP, SWA_HEAD_DIM, d).transpose(1, 0, 2, 3)
    wout = jnp.concatenate([wa_out[:n_mla], out_swa.reshape(-1, d)], axis=0)
    return win, uq.astype(BF16), wukv.astype(BF16), wout.astype(BF16)


def _layer1_weights(wc_in):
    per_head = 4 * DIFF_HALF + DIFF_V
    w = wc_in.reshape(D_MODEL, DIFF_HEADS, per_head)
    q = w[:, :, :2 * DIFF_HALF].reshape(D_MODEL, -1)
    k = w[:, :, 2 * DIFF_HALF:4 * DIFF_HALF].reshape(D_MODEL, -1)
    v = w[:, :, 4 * DIFF_HALF:].reshape(D_MODEL, -1)
    return jnp.concatenate([q, k, v], axis=-1).astype(BF16)


def _block_tables(counts, n_blocks):
    padded = (counts + MOE_ROWS - 1) // MOE_ROWS * MOE_ROWS
    pad_end = jnp.cumsum(padded)
    pad_start = (pad_end - padded).astype(jnp.int32)
    block_start = jnp.arange(n_blocks, dtype=jnp.int32) * MOE_ROWS
    block_expert = jnp.minimum((pad_end[None, :] <= block_start[:, None]).sum(axis=1),
                               N_EXPERTS - 1).astype(jnp.int32)
    n_used = (pad_end[-1] // MOE_ROWS).astype(jnp.int32).reshape(1)
    has = padded > 0
    slot = ((jnp.cumsum(has) - 1) & 1).astype(jnp.int32)
    ids = jnp.where(has, jnp.arange(N_EXPERTS, dtype=jnp.int32), N_EXPERTS)
    after = jnp.concatenate([lax.cummin(ids, reverse=True)[1:], jnp.full((1,), N_EXPERTS, jnp.int32)])
    next_expert = jnp.where(after < N_EXPERTS, after, -1).astype(jnp.int32)
    return pad_start, block_expert, n_used, slot, next_expert


def _moe_layer(f_packed, top_idx, top_w, rank, counts, hs, gtf, g_final, wg, wu, wd, layer, *, final):
    n_tok = top_idx.shape[1]
    n_blocks = -(-(n_tok * TOP_K) // MOE_ROWS) + N_EXPERTS
    pad_start, block_expert, n_used, slot, next_expert = _block_tables(counts[:, 0], n_blocks)
    dest = _dest(pad_start, top_idx, rank)
    xs = _sc_dispatch(f_packed, dest, n_blocks * MOE_ROWS)
    ys = _moe(block_expert, n_used, slot, next_expert, xs, wg, wu, wd, layer)
    y = _sc_gather(ys, dest).reshape(TOP_K, n_tok, ROW_WORDS)
    return _combine(y, top_w.T, hs, gtf, g_final, final=final)


def kernel(x, c, ctx, c_ctx, w_ada, b_ada, g_mix, g_ffn, wa_in, ga_q, ga_kv, wa_uq, wa_ukv, wa_sink, wa_out,
           wc_in, lam_q1, lam_k1, lam_q2, lam_k2, gc_sub, wc_out, w_router, b_router, we_gate, we_up, we_down,
           ws_gate, ws_up, ws_down, g_final):
    d = D_MODEL
    x2, ctx2 = x.reshape(N_LAT, d), ctx.reshape(N_CTX, d)
    cc =jnp.concatenate([c, c_ctx[None, :], jnp.zeros((MOD_ROWS - BATCH - 1, d), F32)], axis=0)
    mod = _ada(cc, w_ada, b_ada).reshape(DEPTH, MOD_ROWS, 6, 1, d)

    def mod_vec(layer, j):
        return mod[layer, :, j]

    cm, sm, c64, s64 = _rope_tables()
    g_final2 = g_final.reshape(1, d)

    def router_weights(layer):
        wt = w_router[layer].T
        hi = wt.astype(BF16)
        lo = (wt - hi.astype(F32)).astype(BF16)
        return hi, lo

    def shared_weights(layer):
        return ws_gate[layer].astype(BF16), ws_up[layer].astype(BF16), ws_down[layer].astype(BF16)

    win, wuq, wukv, wout0 = _layer0_weights(wa_in[0], wa_uq[0], wa_ukv[0], wa_out[0])
    qm, km, vm, qs, ks, vs = _proj0(x2, ctx2, g_mix[0].reshape(1, d), mod_vec(0, 0), mod_vec(0, 1), win,
                                    ga_q[0].reshape(1, -1), ga_kv[0].reshape(1, -1), wuq, wukv, cm, sm, c64, s64)
    sink = wa_sink[0].astype(F32)
    o = _mla_attention(qm, km, vm, None, ctx_queries=False)
    o = _mla_attention(qm, km, vm, o, ctx_queries=True)
    o = _swa_attention(sink, qs, ks, vs, o, ctx_queries=False)
    o = _swa_attention(sink, qs, ks, vs, o, ctx_queries=True)
    hs, f, logits_t = _post(
        o, x2, ctx2, wout0, mod_vec(0, 2), g_ffn[0].reshape(1, d), mod_vec(0, 3), mod_vec(0, 4), mod_vec(0, 5),
        *router_weights(0), *shared_weights(0), N_TOK)
    top_idx, top_w, rank, counts = _router(logits_t, b_router[0].reshape(N_EXPERTS, 1))
    h = _moe_layer(f, top_idx, top_w, rank, counts, hs, mod_vec(0, 5), g_final2,
                   we_gate, we_up, we_down, 0, final=False)

    layer = 1
    lam_init = 0.8 - 0.6 * math.exp(-0.3 * layer)
    q1, k1, v1 = _proj1(h, g_mix[1].reshape(1, d), mod_vec(1, 0), mod_vec(1, 1), _layer1_weights(wc_in[0]),
                        c64, s64)
    lam_vecs = jnp.zeros((8, LANES), F32).at[:4, :DIFF_HALF].set(
        jnp.stack([lam_q1[0], lam_k1[0], lam_q2[0], lam_k2[0]]).astype(F32))
    o = _diff_attention(q1, k1, v1, lam_vecs, gc_sub[0].reshape(1, DIFF_V), lam_init)
    hs, f, logits_t = _post(
        o, h, h, wc_out[0].astype(BF16), mod_vec(1, 2), g_ffn[1].reshape(1, d), mod_vec(1, 3), mod_vec(1, 4),
        mod_vec(1, 5), *router_weights(1), *shared_weights(1), N_LAT)
    top_idx, top_w, rank, counts = _router(logits_t, b_router[1].reshape(N_EXPERTS, 1))
    out = _moe_layer(f, top_idx, top_w, rank, counts, hs, mod_vec(1, 5), g_final2,
                     we_gate, we_up, we_down, 1, final=True)
    return out.reshape(BATCH, SEQ, d)
```

```python
import functools
import math

import jax
import jax.numpy as jnp
from jax import lax
from jax.experimental import pallas as pl
from jax.experimental.pallas import tpu as pltpu
from jax.experimental.pallas import tpu_sc as plsc

F32 = jnp.float32
BF16 = jnp.bfloat16

D_MODEL = 1024
BATCH = 8
SEQ = 2048
DEPTH = 2
CTX_LEN = 256
GRID_W = 64
ROPE_BASE = 10000.0
EPS = 1e-6
NEG_INF = -1e30

MLA_HEADS = 8
MLA_Q_RANK = 384
MLA_KV_RANK = 256
MLA_NOPE = 64
MLA_ROPE = 32
MLA_V = 64
SWA_HEADS = 8
SWA_KV_HEADS = 2
SWA_HEAD_DIM = 64
SWA_GROUP = SWA_HEADS // SWA_KV_HEADS
WINDOW = 128
DIFF_HEADS = 8
DIFF_HALF = 64
DIFF_V = 128
N_EXPERTS = 64
N_GROUPS = 8
GROUP_SIZE = N_EXPERTS // N_GROUPS
TOPK_GROUPS = 4
TOP_K = 8
EXPERT_FF = 256
SHARED_FF = 256
ROUTED_SCALE = 2.5

LOG2E = math.log2(math.e)
LANES = 128
N_LAT = BATCH * SEQ
N_CTX = BATCH * CTX_LEN
N_TOK = N_LAT + N_CTX
MOD_ROWS = 16

TM = 1024
TM_CHUNK = 256
LAT_BLOCKS_PER_BATCH = SEQ // TM
TQ_MLA = 512
MLA_PAIRS_PER_STEP = 4
TQ_DIFF = 512
DIFF_HEADS_PER_STEP = 8
SWA_BLOCK = 128
SWA_QBLOCKS = 4
MOE_ROWS = 512
MOE_CHUNK = 256
TM_COMB = 512
VMEM_LIMIT = 56 * 1024 * 1024

C_CQ = 0
C_CKV = C_CQ + MLA_Q_RANK
C_KR = C_CKV + MLA_KV_RANK
C_QS = C_KR + LANES
C_KS = C_QS + SWA_HEADS * SWA_HEAD_DIM
C_VS = C_KS + SWA_KV_HEADS * SWA_HEAD_DIM
C_END = C_VS + SWA_KV_HEADS * SWA_HEAD_DIM


def _cparams(*sem):
    return pltpu.CompilerParams(dimension_semantics=sem, vmem_limit_bytes=VMEM_LIMIT)


def _dot(a, b):
    return jnp.dot(a, b, preferred_element_type=F32)


def _dot_nt(a, b):
    return lax.dot_general(a, b, (((1,), (1,)), ((), ())), preferred_element_type=F32)


def _rms(x):
    return x * lax.rsqrt(jnp.mean(x * x, axis=-1, keepdims=True) + EPS)


def _silu(x):
    return x * jax.nn.sigmoid(x)


def _rope(x, cos, sin_signed, shift):
    n = x.shape[-1]
    lane = lax.broadcasted_iota(jnp.int32, x.shape, 1)
    first = (lane & shift) == 0
    rot = jnp.where(first, pltpu.roll(x, n - shift, 1), pltpu.roll(x, shift, 1))
    return x * cos + rot * sin_signed


def _mod_index(i):
    return jnp.minimum(i // LAT_BLOCKS_PER_BATCH, BATCH)


def _rope_index(i):
    return jnp.where(i < N_LAT // TM, i % LAT_BLOCKS_PER_BATCH, LAT_BLOCKS_PER_BATCH)


ADA_TN = 1536


def _ada_kernel(c_ref, w_ref, b_ref, o_ref):
    s = _silu(c_ref[...]).astype(BF16)
    o_ref[0] = _dot(s, w_ref[0].astype(BF16)) + b_ref[0]


def _ada(cc, w_ada, b_ada):
    n_out = w_ada.shape[-1]
    return pl.pallas_call(
        _ada_kernel,
        grid=(DEPTH, n_out // ADA_TN),
        in_specs=[
            pl.BlockSpec((MOD_ROWS, D_MODEL), lambda l, j: (0, 0)),
            pl.BlockSpec((1, D_MODEL, ADA_TN), lambda l, j: (l, 0, j)),
            pl.BlockSpec((1, 1, ADA_TN), lambda l, j: (l, 0, j)),
        ],
        out_specs=pl.BlockSpec((1, MOD_ROWS, ADA_TN), lambda l, j: (l, 0, j)),
        out_shape=jax.ShapeDtypeStruct((DEPTH, MOD_ROWS, n_out), F32),
        compiler_params=_cparams("parallel", "parallel"),
        name="ada",
    )(cc, w_ada, b_ada.reshape(DEPTH, 1, n_out))


def _token_rows(lat_ref, ctx_ref, rows):
    return jnp.where(pl.program_id(0) < N_LAT // TM, lat_ref[rows, :], ctx_ref[rows, :])


def _token_specs(width=D_MODEL, col=0):
    lat_blocks = N_LAT // TM
    return [pl.BlockSpec((TM, width), lambda i: (jnp.minimum(i, lat_blocks - 1), col)),
            pl.BlockSpec((TM, width), lambda i: (jnp.maximum(i - lat_blocks, 0), col))]


def _proj0_kernel(x_ref, ctx_ref, g_ref, sh_ref, sc_ref, win_ref, gq_ref, gkv_ref, wuq_ref, wukv_ref,
                  cm_ref, sm_ref, c64_ref, s64_ref,
                  qm_ref, km_ref, vm_ref, qs_ref, ks_ref, vs_ref):
    q_scale = (MLA_NOPE + MLA_ROPE) ** -0.5 * LOG2E
    s_scale = SWA_HEAD_DIM ** -0.5 * LOG2E
    for c in range(TM // TM_CHUNK):
        rows = slice(c * TM_CHUNK, (c + 1) * TM_CHUNK)
        a = _rms(_token_rows(x_ref, ctx_ref, rows)) * g_ref[...]
        a = a * (1.0 + sc_ref[0]) + sh_ref[0]
        p = _dot(a.astype(BF16), win_ref[...])
        nq = _rms(p[:, C_CQ:C_CKV]) * gq_ref[...]
        nkv = _rms(p[:, C_CKV:C_KR]) * gkv_ref[...]
        q = _dot(nq.astype(BF16), wuq_ref[...])
        kv = _dot(nkv.astype(BF16), wukv_ref[...])
        cm, sm = cm_ref[rows, :], sm_ref[rows, :]
        c64, s64 = c64_ref[rows, :], s64_ref[rows, :]
        kr = _rope(p[:, C_KR:C_QS], cm, sm, MLA_ROPE // 4)
        for h in range(MLA_HEADS):
            sl = slice(h * LANES, (h + 1) * LANES)
            qm_ref[rows, sl] = (_rope(q[:, sl], cm, sm, MLA_ROPE // 4) * q_scale).astype(BF16)
            km_ref[rows, sl] = (kv[:, sl] + kr).astype(BF16)
        vm_ref[rows, :] = kv[:, MLA_HEADS * LANES:].astype(BF16)
        for g in range(SWA_GROUP):
            sl = slice(g * LANES, (g + 1) * LANES)
            qs_ref[rows, sl] = (_rope(p[:, C_QS + g * LANES:C_QS + (g + 1) * LANES], c64, s64,
                                      SWA_HEAD_DIM // 4) * s_scale).astype(BF16)
        ks_ref[rows, :] = _rope(p[:, C_KS:C_VS], c64, s64, SWA_HEAD_DIM // 4).astype(BF16)
        vs_ref[rows, :] = p[:, C_VS:C_END].astype(BF16)


def _proj0(x, ctx, g_mix, sh, sc, win, gq, gkv, wuq, wukv, cm, sm, c64, s64):
    row = lambda i: (i, 0)
    const = lambda i: (0, 0)
    mod = lambda i: (_mod_index(i), 0, 0)
    rope = lambda i: (_rope_index(i), 0)
    widths = (MLA_HEADS * LANES, MLA_HEADS * LANES, MLA_HEADS * MLA_V,
              SWA_HEADS * SWA_HEAD_DIM, SWA_KV_HEADS * SWA_HEAD_DIM, SWA_KV_HEADS * SWA_HEAD_DIM)
    return pl.pallas_call(
        _proj0_kernel,
        grid=(N_TOK // TM,),
        in_specs=_token_specs() + [
            pl.BlockSpec((1, D_MODEL), const),
            pl.BlockSpec((1, 1, D_MODEL), mod),
            pl.BlockSpec((1, 1, D_MODEL), mod),
            pl.BlockSpec(win.shape, const),
            pl.BlockSpec(gq.shape, const),
            pl.BlockSpec(gkv.shape, const),
            pl.BlockSpec(wuq.shape, const),
            pl.BlockSpec(wukv.shape, const),
            pl.BlockSpec((TM, LANES), rope),
            pl.BlockSpec((TM, LANES), rope),
            pl.BlockSpec((TM, LANES), rope),
            pl.BlockSpec((TM, LANES), rope),
        ],
        out_specs=[pl.BlockSpec((TM, w), row) for w in widths],
        out_shape=[jax.ShapeDtypeStruct((N_TOK, w), BF16) for w in widths],
        compiler_params=_cparams("parallel"),
        name="proj0",
    )(x, ctx, g_mix, sh, sc, win, gq, gkv, wuq, wukv, cm, sm, c64, s64)


def _proj1_kernel(h_ref, g_ref, sh_ref, sc_ref, w_ref, c64_ref, s64_ref, q_ref, k_ref, v_ref):
    width = DIFF_HEADS * LANES
    scale = DIFF_HALF ** -0.5 * LOG2E
    for c in range(TM // TM_CHUNK):
        rows = slice(c * TM_CHUNK, (c + 1) * TM_CHUNK)
        a = _rms(h_ref[rows, :]) * g_ref[...]
        a = a * (1.0 + sc_ref[0]) + sh_ref[0]
        p = _dot(a.astype(BF16), w_ref[...])
        c64, s64 = c64_ref[rows, :], s64_ref[rows, :]
        for h in range(DIFF_HEADS):
            sl = slice(h * LANES, (h + 1) * LANES)
            q_ref[rows, sl] = (_rope(p[:, sl], c64, s64, DIFF_HALF // 4) * scale).astype(BF16)
            k_ref[rows, sl] = _rope(p[:, width + h * LANES:width + (h + 1) * LANES], c64, s64,
                                    DIFF_HALF // 4).astype(BF16)
        v_ref[rows, :] = p[:, 2 * width:].astype(BF16)


def _proj1(h, g_mix, sh, sc, w, c64, s64):
    row = lambda i: (i, 0)
    const = lambda i: (0, 0)
    mod = lambda i: (_mod_index(i), 0, 0)
    rope = lambda i: (_rope_index(i), 0)
    width = DIFF_HEADS * LANES
    return pl.pallas_call(
        _proj1_kernel,
        grid=(N_TOK // TM,),
        in_specs=[
            pl.BlockSpec((TM, D_MODEL), row),
            pl.BlockSpec((1, D_MODEL), const),
            pl.BlockSpec((1, 1, D_MODEL), mod),
            pl.BlockSpec((1, 1, D_MODEL), mod),
            pl.BlockSpec(w.shape, const),
            pl.BlockSpec((TM, LANES), rope),
            pl.BlockSpec((TM, LANES), rope),
        ],
        out_specs=[pl.BlockSpec((TM, width), row)] * 3,
        out_shape=[jax.ShapeDtypeStruct((N_TOK, width), BF16)] * 3,
        compiler_params=_cparams("parallel"),
        name="proj1",
    )(h, g_mix, sh, sc, w, c64, s64)


def _mla_kernel(*refs, n_kv):
    q_ref = refs[0]
    k_refs = refs[1:1 + n_kv]
    v_refs = refs[1 + n_kv:1 + 2 * n_kv]
    o_ref = refs[1 + 2 * n_kv]
    lane = lax.broadcasted_iota(jnp.int32, (q_ref.shape[0], LANES), 1)
    own = [lane < MLA_V, lane >= MLA_V]
    for pair in range(MLA_PAIRS_PER_STEP):
        vsl = slice(pair * LANES, (pair + 1) * LANES)
        outs = []
        for hh in range(2):
            sl = slice((2 * pair + hh) * LANES, (2 * pair + hh + 1) * LANES)
            qh = q_ref[:, sl]
            ss = [_dot_nt(qh, k[:, sl]) for k in k_refs]
            m = functools.reduce(jnp.maximum, [jnp.max(s, axis=-1, keepdims=True) for s in ss])
            o = None
            for s, v in zip(ss, v_refs):
                vp = v[:, vsl]
                vlane = lax.broadcasted_iota(jnp.int32, vp.shape, 1)
                keep = (vlane < MLA_V) if hh == 0 else (vlane >= MLA_V)
                vh = jnp.where(keep, vp, jnp.ones_like(vp))
                part = _dot(jnp.exp2(s - m).astype(BF16), vh)
                o = part if o is None else o + part
            outs.append(o / pltpu.roll(o, MLA_V, 1))
        o_ref[:, vsl] = jnp.where(own[0], outs[0], outs[1]).astype(BF16)


def _mla_attention(qm, km, vm, *, ctx_queries):
    groups = MLA_HEADS // 2 // MLA_PAIRS_PER_STEP
    qk_w = 2 * LANES * MLA_PAIRS_PER_STEP
    v_w = LANES * MLA_PAIRS_PER_STEP
    if ctx_queries:
        tq, n_q, rows = CTX_LEN, 1, N_CTX
        q_map = lambda b, h, i: (N_LAT // CTX_LEN + b, h)
        o_map = lambda b, h, i: (b, h)
        kv_specs = [(CTX_LEN, lambda b, h, i: (N_LAT // CTX_LEN + b, h))]
    else:
        tq, n_q, rows = TQ_MLA, SEQ // TQ_MLA, N_LAT
        q_map = o_map = lambda b, h, i: (b * n_q + i, h)
        kv_specs = [(CTX_LEN, lambda b, h, i: (N_LAT // CTX_LEN + b, h)), (SEQ, lambda b, h, i: (b, h))]
    n_kv = len(kv_specs)
    in_specs = [pl.BlockSpec((tq, qk_w), q_map)]
    in_specs += [pl.BlockSpec((n, qk_w), m) for n, m in kv_specs]
    in_specs += [pl.BlockSpec((n, v_w), m) for n, m in kv_specs]
    return pl.pallas_call(
        functools.partial(_mla_kernel, n_kv=n_kv),
        grid=(BATCH, groups, n_q),
        in_specs=in_specs,
        out_specs=pl.BlockSpec((tq, v_w), o_map),
        out_shape=jax.ShapeDtypeStruct((rows, MLA_HEADS * MLA_V), BF16),
        compiler_params=_cparams("parallel", "parallel", "parallel"),
        name="mla_ctx" if ctx_queries else "mla_lat",
    )(qm, *([km] * n_kv), *([vm] * n_kv))


def _swa_kernel(sink_ref, *refs, band, n_blocks):
    q_ref = refs[0]
    if band:
        n_kv = SWA_QBLOCKS + 2
        k_blocks = refs[1:1 + n_kv]
        kx = refs[1 + n_kv]
        v_blocks = refs[2 + n_kv:2 + 2 * n_kv]
        vx, o_ref = refs[2 + 2 * n_kv], refs[3 + 2 * n_kv]
        chains, rows = SWA_QBLOCKS, SWA_BLOCK
    else:
        kx, vx, o_ref = refs[1:]
        chains, rows = 1, q_ref.shape[0]
    stacked = SWA_GROUP * rows
    k_ctx = kx[...]
    v_ctx = vx[...]
    lane = lax.broadcasted_iota(jnp.int32, (rows, LANES), 1)
    low = lane < SWA_HEAD_DIM
    row_group = lax.broadcasted_iota(jnp.int32, (stacked, 1), 0) // rows
    for c in range(chains):
        qrows = slice(c * rows, (c + 1) * rows)
        if band:
            n = pl.program_id(1) * SWA_QBLOCKS + c
            k_band = jnp.concatenate([k[...] for k in k_blocks[c:c + 3]], axis=0)
            v_band = jnp.concatenate([v[...] for v in v_blocks[c:c + 3]], axis=0)
            qq = lax.broadcasted_iota(jnp.int32, (stacked, 3 * SWA_BLOCK), 0) & (SWA_BLOCK - 1)
            kk = lax.broadcasted_iota(jnp.int32, (stacked, 3 * SWA_BLOCK), 1)
            rel = kk - SWA_BLOCK - qq
            key_pos = (n - 1) * SWA_BLOCK + kk
            valid = (jnp.abs(rel) <= WINDOW) & (key_pos >= 0) & (key_pos < n_blocks * SWA_BLOCK)
        halves = []
        for hk in range(SWA_KV_HEADS):
            keep = low if hk == 0 else jnp.logical_not(low)
            qh = jnp.concatenate(
                [jnp.where(keep, q_ref[qrows, g * LANES:(g + 1) * LANES], jnp.zeros((rows, LANES), BF16))
                 for g in range(SWA_GROUP)], axis=0)
            sink = jnp.zeros((stacked, 1), F32)
            for g in range(SWA_GROUP):
                sink = jnp.where(row_group == g, sink_ref[hk * SWA_GROUP + g] * LOG2E, sink)
            s_ctx = _dot_nt(qh, k_ctx)
            m = jnp.maximum(jnp.max(s_ctx, axis=-1, keepdims=True), sink)
            if band:
                s_band = jnp.where(valid, _dot_nt(qh, k_band), NEG_INF)
                m = jnp.maximum(m, jnp.max(s_band, axis=-1, keepdims=True))

            def ones_other(v, hk=hk):
                vlane = lax.broadcasted_iota(jnp.int32, v.shape, 1)
                own = (vlane < SWA_HEAD_DIM) if hk == 0 else (vlane >= SWA_HEAD_DIM)
                return jnp.where(own, v, jnp.ones_like(v))

            o = _dot(jnp.exp2(s_ctx - m).astype(BF16), ones_other(v_ctx))
            if band:
                o = o + _dot(jnp.exp2(s_band - m).astype(BF16), ones_other(v_band))
            denom = pltpu.roll(o, SWA_HEAD_DIM, 1) + jnp.exp2(sink - m)
            halves.append(o / denom)
        for g in range(SWA_GROUP):
            rs = slice(g * rows, (g + 1) * rows)
            o_ref[qrows, g * LANES:(g + 1) * LANES] = jnp.where(low, halves[0][rs], halves[1][rs]).astype(BF16)


def _swa_attention(sink, qs, ks, vs, *, ctx_queries):
    width = SWA_HEADS * SWA_HEAD_DIM
    kvw = SWA_KV_HEADS * SWA_HEAD_DIM
    ctx_map = lambda b, i: (N_LAT // CTX_LEN + b, 0)
    smem = pl.BlockSpec(memory_space=pltpu.SMEM)
    if ctx_queries:
        grid, rows = (BATCH, 1), N_CTX
        in_specs = [smem, pl.BlockSpec((CTX_LEN, width), ctx_map),
                    pl.BlockSpec((CTX_LEN, kvw), ctx_map), pl.BlockSpec((CTX_LEN, kvw), ctx_map)]
        args = (sink, qs, ks, vs)
        out_spec = pl.BlockSpec((CTX_LEN, width), lambda b, i: (b, 0))
        n_blocks = 1
    else:
        rows = N_LAT
        n_blocks = SEQ // SWA_BLOCK
        steps = n_blocks // SWA_QBLOCKS
        grid = (BATCH, steps)
        q_rows = SWA_QBLOCKS * SWA_BLOCK

        def kv_block(offset):
            return lambda b, i: (b * n_blocks + jnp.clip(i * SWA_QBLOCKS + offset, 0, n_blocks - 1), 0)

        band_specs = [pl.BlockSpec((SWA_BLOCK, kvw), kv_block(off)) for off in range(-1, SWA_QBLOCKS + 1)]
        in_specs = ([smem, pl.BlockSpec((q_rows, width), lambda b, i: (b * steps + i, 0))]
                    + band_specs + [pl.BlockSpec((CTX_LEN, kvw), ctx_map)]
                    + band_specs + [pl.BlockSpec((CTX_LEN, kvw), ctx_map)])
        n_band = len(band_specs)
        args = (sink, qs) + (ks,) * (n_band + 1) + (vs,) * (n_band + 1)
        out_spec = pl.BlockSpec((q_rows, width), lambda b, i: (b * steps + i, 0))
    return pl.pallas_call(
        functools.partial(_swa_kernel, band=not ctx_queries, n_blocks=n_blocks),
        grid=grid,
        in_specs=in_specs,
        out_specs=out_spec,
        out_shape=jax.ShapeDtypeStruct((rows, width), BF16),
        compiler_params=_cparams("parallel", "parallel"),
        name="swa_ctx" if ctx_queries else "swa_lat",
    )(*args)


def _diff_kernel(q_ref, kc_ref, kl_ref, vc_ref, vl_ref, lam_ref, g_ref, o_ref, *, lam_init):
    lam = (jnp.exp(jnp.sum(lam_ref[0:1, :] * lam_ref[1:2, :], axis=-1, keepdims=True))
           - jnp.exp(jnp.sum(lam_ref[2:3, :] * lam_ref[3:4, :], axis=-1, keepdims=True)) + lam_init)
    lane = lax.broadcasted_iota(jnp.int32, (q_ref.shape[0], LANES), 1)
    low = lane < DIFF_HALF
    k_refs = (kc_ref, kl_ref)
    v_refs = (vc_ref, vl_ref)
    for h in range(DIFF_HEADS_PER_STEP):
        sl = slice(h * LANES, (h + 1) * LANES)
        q = q_ref[:, sl]
        zero = jnp.zeros_like(q)

        vx = [jnp.concatenate([v[:, sl], jnp.ones((v.shape[0], LANES), BF16)], axis=1) for v in v_refs]

        def attend(qh):
            ss = [_dot_nt(qh, k[:, sl]) for k in k_refs]
            m = functools.reduce(jnp.maximum, [jnp.max(s, axis=-1, keepdims=True) for s in ss])
            ox = functools.reduce(jnp.add, [_dot(jnp.exp2(s - m).astype(BF16), v) for s, v in zip(ss, vx)])
            return ox[:, :LANES] / ox[:, LANES:]

        o = attend(jnp.where(low, q, zero)) - lam * attend(jnp.where(low, zero, q))
        o = _rms(o) * g_ref[...] * (1.0 - lam_init)
        o_ref[:, sl] = o.astype(BF16)


def _diff_attention(q, k, v, lam_vecs, g_sub, lam_init):
    n_q = SEQ // TQ_DIFF
    q_map = lambda b, h, i: (b * n_q + i, h)
    ctx_map = lambda b, h, i: (N_LAT // CTX_LEN + b, h)
    lat_map = lambda b, h, i: (b, h)
    const = lambda b, h, i: (0, 0)
    width = LANES * DIFF_HEADS_PER_STEP
    return pl.pallas_call(
        functools.partial(_diff_kernel, lam_init=lam_init),
        grid=(BATCH, DIFF_HEADS // DIFF_HEADS_PER_STEP, n_q),
        in_specs=[
            pl.BlockSpec((TQ_DIFF, width), q_map),
            pl.BlockSpec((CTX_LEN, width), ctx_map),
            pl.BlockSpec((SEQ, width), lat_map),
            pl.BlockSpec((CTX_LEN, width), ctx_map),
            pl.BlockSpec((SEQ, width), lat_map),
            pl.BlockSpec(lam_vecs.shape, const),
            pl.BlockSpec(g_sub.shape, const),
        ],
        out_specs=pl.BlockSpec((TQ_DIFF, width), q_map),
        out_shape=jax.ShapeDtypeStruct((N_LAT, DIFF_HEADS * DIFF_V), BF16),
        compiler_params=_cparams("parallel", "parallel", "parallel"),
        name="diff_attn",
    )(q, k, k, v, v, lam_vecs, g_sub)


BF16_BITS = 16
HIGH_HALF_WORD = 0xFFFF0000


def _pack_bf16_valued(x):
    n = x.shape[1] // 2
    lo = pltpu.bitcast(x[:, :n], jnp.uint32)
    hi = pltpu.bitcast(x[:, n:], jnp.uint32)
    return pltpu.bitcast((lo >> BF16_BITS) | hi, jnp.int32)


def _pack_bf16_pairs(x):
    return _pack_bf16_valued(x.astype(BF16).astype(F32))


def _unpack_bf16_pairs(w):
    u = pltpu.bitcast(w, jnp.uint32)
    lo = pltpu.bitcast(u << BF16_BITS, F32)
    hi = pltpu.bitcast(u & jnp.uint32(HIGH_HALF_WORD), F32)
    return lo, hi


def _route(logits_t, bias, cnt_ref, tri_ref, idx_ref, w_ref, rank_ref):
    tm = logits_t.shape[1]
    scores = jax.nn.sigmoid(logits_t)
    biased = scores + bias
    sub = lax.broadcasted_iota(jnp.int32, (GROUP_SIZE, tm), 0).astype(F32)
    grp_scores, grp_biased, grp_index = [], [], []
    group_score = []
    for g in range(N_GROUPS):
        sl = slice(g * GROUP_SIZE, (g + 1) * GROUP_SIZE)
        bg = biased[sl, :]
        grp_scores.append(scores[sl, :])
        grp_biased.append(bg)
        grp_index.append(sub + float(g * GROUP_SIZE))
        m1 = jnp.max(bg, axis=0, keepdims=True)
        first = jnp.min(jnp.where(bg == m1, sub, float(GROUP_SIZE)), axis=0, keepdims=True)
        m2 = jnp.max(jnp.where(sub == first, -jnp.inf, bg), axis=0, keepdims=True)
        group_score.append(m1 + m2)
    keep = [jnp.zeros((1, tm), F32) for _ in range(N_GROUPS)]
    for _ in range(TOPK_GROUPS):
        m = functools.reduce(jnp.maximum, group_score)
        found = jnp.zeros((1, tm), F32)
        for g in range(N_GROUPS):
            hit = jnp.where(group_score[g] == m, 1.0 - found, 0.0)
            keep[g] = keep[g] + hit
            found = found + hit
            group_score[g] = jnp.where(hit > 0.0, -jnp.inf, group_score[g])
    vals = [jnp.where(keep[g] > 0.0, grp_biased[g], NEG_INF) for g in range(N_GROUPS)]
    chosen = [jnp.zeros((GROUP_SIZE, tm), F32) for _ in range(N_GROUPS)]
    picked = []
    for _ in range(TOP_K):
        m = jnp.max(functools.reduce(jnp.maximum, vals), axis=0, keepdims=True)
        cand = [jnp.where(vals[g] == m, grp_index[g], float(N_EXPERTS)) for g in range(N_GROUPS)]
        ei = jnp.min(functools.reduce(jnp.minimum, cand), axis=0, keepdims=True)
        sel = [grp_index[g] == ei for g in range(N_GROUPS)]
        s_k = functools.reduce(jnp.add, [jnp.where(sel[g], grp_scores[g], 0.0) for g in range(N_GROUPS)])
        picked.append((ei, jnp.sum(s_k, axis=0, keepdims=True)))
        vals = [jnp.where(sel[g], -jnp.inf, vals[g]) for g in range(N_GROUPS)]
        chosen = [jnp.where(sel[g], 1.0, chosen[g]) for g in range(N_GROUPS)]
    total = functools.reduce(jnp.add, [s for _, s in picked])
    chosen_all = jnp.concatenate(chosen, axis=0)
    tri = tri_ref[...]
    width = tri.shape[0]
    cnt = cnt_ref[...]
    parts = []
    for j in range(tm // width):
        ch = chosen_all[:, j * width:(j + 1) * width]
        parts.append(_dot(ch.astype(BF16), tri) + cnt)
        cnt = cnt + jnp.sum(ch, axis=1, keepdims=True)
    cnt_ref[...] = cnt
    rank_all = jnp.concatenate(parts, axis=1)
    for k, (ei, s) in enumerate(picked):
        idx_ref[k:k + 1, :] = ei.astype(jnp.int32)
        w_ref[k:k + 1, :] = s / total * ROUTED_SCALE
        r = functools.reduce(jnp.add, [
            jnp.where(grp_index[g] == ei, rank_all[g * GROUP_SIZE:(g + 1) * GROUP_SIZE, :], 0.0)
            for g in range(N_GROUPS)])
        rank_ref[k:k + 1, :] = jnp.sum(r, axis=0, keepdims=True).astype(jnp.int32)


ROUTE_TN = 2048
TRI_N = 256


def _router_kernel(logits_ref, br_ref, tri_ref, idx_ref, w_ref, rank_ref, cnt_out_ref, cnt_ref):
    @pl.when(pl.program_id(0) == 0)
    def _():
        cnt_ref[...] = jnp.zeros_like(cnt_ref)

    _route(logits_ref[...], br_ref[...], cnt_ref, tri_ref, idx_ref, w_ref, rank_ref)
    cnt_out_ref[...] = jnp.broadcast_to(cnt_ref[...], cnt_out_ref.shape).astype(jnp.int32)


def _router(logits_t, br):
    n_tok = logits_t.shape[1]
    col = lambda i: (0, i)
    const = lambda i: (0, 0)
    tri = (jnp.arange(TRI_N)[:, None] < jnp.arange(TRI_N)[None, :]).astype(BF16)
    blk = pl.BlockSpec((TOP_K, ROUTE_TN), col)
    return pl.pallas_call(
        _router_kernel,
        grid=(n_tok // ROUTE_TN,),
        in_specs=[pl.BlockSpec((N_EXPERTS, ROUTE_TN), col), pl.BlockSpec(br.shape, const),
                  pl.BlockSpec(tri.shape, const)],
        out_specs=[blk, blk, blk, pl.BlockSpec((N_EXPERTS, LANES), const)],
        out_shape=[
            jax.ShapeDtypeStruct((TOP_K, n_tok), jnp.int32),
            jax.ShapeDtypeStruct((TOP_K, n_tok), F32),
            jax.ShapeDtypeStruct((TOP_K, n_tok), jnp.int32),
            jax.ShapeDtypeStruct((N_EXPERTS, LANES), jnp.int32),
        ],
        scratch_shapes=[pltpu.VMEM((N_EXPERTS, 1), F32)],
        compiler_params=_cparams("arbitrary"),
        name="router",
    )(logits_t, br, tri)


def _post_kernel(oal_ref, oac_ref, obl_ref, obc_ref, hl_ref, hc_ref, wout_ref, gtm_ref, g_ref, sh_ref, sc_ref,
                 gtf_ref, wrh_ref, wrl_ref, wsg_ref, wsu_ref, wsd_ref, hs_ref, f_ref, logits_ref):
    half = D_MODEL // 2
    for c in range(TM // TM_CHUNK):
        rows = slice(c * TM_CHUNK, (c + 1) * TM_CHUNK)
        proj = (_dot(_token_rows(oal_ref, oac_ref, rows), wout_ref[:half, :])
                + _dot(_token_rows(obl_ref, obc_ref, rows), wout_ref[half:, :]))
        h1 = _token_rows(hl_ref, hc_ref, rows) + gtm_ref[0] * proj
        f = _rms(h1) * g_ref[...]
        f = f * (1.0 + sc_ref[0]) + sh_ref[0]
        f_hi = f.astype(BF16)
        f_hi32 = f_hi.astype(F32)
        f_lo = (f - f_hi32).astype(BF16)
        f_ref[rows, :] = _pack_bf16_valued(f_hi32)
        logits_ref[:, rows] = (_dot_nt(wrh_ref[...], f_hi) + _dot_nt(wrh_ref[...], f_lo)
                               + _dot_nt(wrl_ref[...], f_hi))
        mid = _silu(_dot(f_hi, wsg_ref[...])) * _dot(f_hi, wsu_ref[...])
        shared = _dot(mid.astype(BF16), wsd_ref[...])
        hs_ref[rows, :] = h1 + gtf_ref[0] * shared


def _post(o_halves, h_lat, h_ctx, wout, gtm, g_ffn, sh, sc, gtf, wrh, wrl, wsg, wsu, wsd, n_rows):
    row = lambda i: (i, 0)
    const = lambda i: (0, 0)
    mod = lambda i: (_mod_index(i), 0, 0)
    vec = pl.BlockSpec((1, 1, D_MODEL), mod)
    full = lambda a: pl.BlockSpec(a.shape, const)
    o_specs, o_args = [], []
    for lat, ctx, col in o_halves:
        o_specs += _token_specs(D_MODEL // 2, col)
        o_args += [lat, ctx]
    return pl.pallas_call(
        _post_kernel,
        grid=(n_rows // TM,),
        in_specs=o_specs + _token_specs() + [
            full(wout), vec, full(g_ffn), vec, vec, vec,
            full(wrh), full(wrl), full(wsg), full(wsu), full(wsd),
        ],
        out_specs=[
            pl.BlockSpec((TM, D_MODEL), row),
            pl.BlockSpec((TM, D_MODEL // 2), row),
            pl.BlockSpec((N_EXPERTS, TM), lambda i: (0, i)),
        ],
        out_shape=[
            jax.ShapeDtypeStruct((n_rows, D_MODEL), F32),
            jax.ShapeDtypeStruct((n_rows, D_MODEL // 2), jnp.int32),
            jax.ShapeDtypeStruct((N_EXPERTS, n_rows), F32),
        ],
        compiler_params=_cparams("parallel"),
        name="post_attn",
    )(*o_args, h_lat, h_ctx, wout, gtm, g_ffn, sh, sc, gtf, wrh, wrl, wsg, wsu, wsd)


DEST_TN = 2048


def _dest_kernel(ps_ref, idx_ref, rank_ref, o_ref):
    idx = idx_ref[...]
    acc = rank_ref[...]
    for e in range(N_EXPERTS):
        acc = acc + jnp.where(idx == e, ps_ref[e], 0)
    o_ref[...] = acc


def _dest(pad_start, top_idx, rank):
    n_tok = top_idx.shape[1]
    blk = pl.BlockSpec((TOP_K, DEST_TN), lambda i: (0, i))
    return pl.pallas_call(
        _dest_kernel,
        grid=(n_tok // DEST_TN,),
        in_specs=[pl.BlockSpec(memory_space=pltpu.SMEM), blk, blk],
        out_specs=blk,
        out_shape=jax.ShapeDtypeStruct((TOP_K, n_tok), jnp.int32),
        compiler_params=_cparams("parallel"),
        name="dest",
    )(pad_start, top_idx, rank)


SC_CORES = 2
SC_SUBCORES = 16
SC_WORKERS = SC_CORES * SC_SUBCORES
SC_DISPATCH_ROWS = 32
SC_GATHER_ROWS = 64
ROW_WORDS = D_MODEL // 2


def _sc_mesh():
    return plsc.VectorSubcoreMesh(core_axis_name="core", subcore_axis_name="subcore")


def _sc_worker():
    return lax.axis_index("subcore") * SC_CORES + lax.axis_index("core")


def _sc_dispatch(f_packed, dest, n_slots):
    n_tok = f_packed.shape[0]
    per_worker = n_tok // SC_WORKERS
    n_chunks = per_worker // SC_DISPATCH_ROWS
    assert n_chunks * SC_DISPATCH_ROWS * SC_WORKERS == n_tok and n_chunks % 2 == 0
    dest4 = dest.reshape(TOP_K, SC_WORKERS, n_chunks, SC_DISPATCH_ROWS)

    def body(f_hbm, dest_hbm, xs_hbm, rows0, rows1, idx_v, load0, load1, scat0, scat1):
        wid = _sc_worker()
        for k in range(TOP_K):
            pltpu.sync_copy(dest_hbm.at[k, wid], idx_v.at[k])

        def load(c, buf, sem):
            start = pl.multiple_of(wid * per_worker + c * SC_DISPATCH_ROWS, 8)
            return pltpu.make_async_copy(f_hbm.at[pl.ds(start, SC_DISPATCH_ROWS)], buf, sem)

        def scatters(c, buf, sem):
            return [pltpu.make_async_copy(buf, xs_hbm.at[idx_v.at[k, c]], sem) for k in range(TOP_K)]

        load(0, rows0, load0).start()

        @pl.loop(0, n_chunks, step=2)
        def _(c):
            @pl.when(c > 0)
            def _():
                for cp in scatters(c - 1, rows1, scat1):
                    cp.wait()

            load(c + 1, rows1, load1).start()
            load(c, rows0, load0).wait()
            for cp in scatters(c, rows0, scat0):
                cp.start()
            load(c + 1, rows1, load1).wait()
            for cp in scatters(c + 1, rows1, scat1):
                cp.start()
            for cp in scatters(c, rows0, scat0):
                cp.wait()

            @pl.when(c + 2 < n_chunks)
            def _():
                load(c + 2, rows0, load0).start()

        for cp in scatters(n_chunks - 1, rows1, scat1):
            cp.wait()

    rows = pltpu.VMEM((SC_DISPATCH_ROWS, ROW_WORDS), jnp.int32)
    return pl.kernel(
        body,
        out_type=jax.ShapeDtypeStruct((n_slots, ROW_WORDS), jnp.int32),
        mesh=_sc_mesh(),
        scratch_types=[rows, rows, pltpu.VMEM((TOP_K, n_chunks, SC_DISPATCH_ROWS), jnp.int32)]
        + [pltpu.SemaphoreType.DMA] * 4,
        name="sc_dispatch",
    )(f_packed, dest4)


def _sc_gather(ys, dest):
    n_idx = dest.shape[0] * dest.shape[1]
    per_worker = n_idx // SC_WORKERS
    n_chunks = per_worker // SC_GATHER_ROWS
    assert n_chunks * SC_GATHER_ROWS * SC_WORKERS == n_idx and n_chunks % 2 == 0
    dest3 = dest.reshape(SC_WORKERS, n_chunks, SC_GATHER_ROWS)

    def body(ys_hbm, dest_hbm, out_hbm, rows0, rows1, idx_v, gat0, gat1, put0, put1):
        wid = _sc_worker()
        pltpu.sync_copy(dest_hbm.at[wid], idx_v)

        def gather(c, buf, sem):
            return pltpu.make_async_copy(ys_hbm.at[idx_v.at[c]], buf, sem)

        def put(c, buf, sem):
            start = pl.multiple_of(wid * per_worker + c * SC_GATHER_ROWS, 8)
            return pltpu.make_async_copy(buf, out_hbm.at[pl.ds(start, SC_GATHER_ROWS)], sem)

        gather(0, rows0, gat0).start()

        @pl.loop(0, n_chunks, step=2)
        def _(c):
            @pl.when(c > 0)
            def _():
                put(c - 1, rows1, put1).wait()

            gather(c + 1, rows1, gat1).start()
            gather(c, rows0, gat0).wait()
            put(c, rows0, put0).start()
            gather(c + 1, rows1, gat1).wait()
            put(c + 1, rows1, put1).start()
            put(c, rows0, put0).wait()

            @pl.when(c + 2 < n_chunks)
            def _():
                gather(c + 2, rows0, gat0).start()

        put(n_chunks - 1, rows1, put1).wait()

    rows = pltpu.VMEM((SC_GATHER_ROWS, ROW_WORDS), jnp.int32)
    return pl.kernel(
        body,
        out_type=jax.ShapeDtypeStruct((n_idx, ROW_WORDS), jnp.int32),
        mesh=_sc_mesh(),
        scratch_types=[rows, rows, pltpu.VMEM((n_chunks, SC_GATHER_ROWS), jnp.int32)]
        + [pltpu.SemaphoreType.DMA] * 4,
        name="sc_gather",
    )(ys, dest3)


X_SLOTS = 4
O_SLOTS = 3


def _moe_kernel(be_ref, nused_ref, slot_ref, next_ref, xs_hbm, wg_hbm, wu_hbm, wd_hbm, ys_hbm,
                xbuf, obuf, wg_s, wu_s, wd_s, wg_f, wu_f, wd_f, xsem, osem, wsem, *, layer):
    n_used = nused_ref[0]

    def x_copy(i, s):
        start = pl.multiple_of(i * MOE_ROWS, MOE_ROWS)
        return pltpu.make_async_copy(xs_hbm.at[pl.ds(start, MOE_ROWS)], xbuf.at[s], xsem.at[s])

    def o_copy(i, s):
        start = pl.multiple_of(i * MOE_ROWS, MOE_ROWS)
        return pltpu.make_async_copy(obuf.at[s], ys_hbm.at[pl.ds(start, MOE_ROWS)], osem.at[s])

    def weight_copies(expert, slot):
        return [pltpu.make_async_copy(src.at[layer, expert], dst.at[slot], wsem.at[slot, j])
                for j, (src, dst) in enumerate(((wg_hbm, wg_f), (wu_hbm, wu_f), (wd_hbm, wd_f)))]

    first_expert = be_ref[0]
    for cp in weight_copies(first_expert, slot_ref[first_expert]):
        cp.start()
    for j in range(X_SLOTS - 1):
        @pl.when(j < n_used)
        def _(j=j):
            x_copy(j, j).start()

    def block(i, carry):
        e = be_ref[i]
        xs = lax.rem(i, X_SLOTS)
        os = lax.rem(i, O_SLOTS)

        @pl.when(jnp.logical_or(i == 0, e != be_ref[jnp.maximum(i - 1, 0)]))
        def _():
            slot = slot_ref[e]
            for cp in weight_copies(e, slot):
                cp.wait()
            wg_s[...] = wg_f[slot].astype(BF16)
            wu_s[...] = wu_f[slot].astype(BF16)
            wd_s[...] = wd_f[slot].astype(BF16)
            nxt = next_ref[e]

            @pl.when(nxt >= 0)
            def _():
                for cp in weight_copies(nxt, 1 - slot):
                    cp.start()

        x_copy(i, xs).wait()

        @pl.when(i + X_SLOTS - 1 < n_used)
        def _():
            x_copy(i + X_SLOTS - 1, lax.rem(i + X_SLOTS - 1, X_SLOTS)).start()

        @pl.when(i >= O_SLOTS)
        def _():
            o_copy(i - O_SLOTS, os).wait()

        half = D_MODEL // 2
        for c in range(MOE_ROWS // MOE_CHUNK):
            rows = pl.ds(c * MOE_CHUNK, MOE_CHUNK)
            lo, hi = _unpack_bf16_pairs(xbuf[xs, rows, :])
            lo, hi = lo.astype(BF16), hi.astype(BF16)
            gate = _dot(lo, wg_s[:half, :]) + _dot(hi, wg_s[half:, :])
            up = _dot(lo, wu_s[:half, :]) + _dot(hi, wu_s[half:, :])
            mid = _silu(gate) * up
            obuf[os, rows, :] = _pack_bf16_pairs(_dot(mid.astype(BF16), wd_s[...]))
        o_copy(i, os).start()
        return carry

    lax.fori_loop(0, n_used, block, 0)
    for j in range(O_SLOTS):
        @pl.when(n_used > j)
        def _(j=j):
            i = n_used - 1 - j
            o_copy(i, lax.rem(i, O_SLOTS)).wait()


def _moe(block_expert, n_used, slot, next_expert, xs, wg, wu, wd, layer):
    hbm = pl.BlockSpec(memory_space=pl.ANY)
    block = (MOE_ROWS, ROW_WORDS)
    grid_spec = pltpu.PrefetchScalarGridSpec(
        num_scalar_prefetch=4,
        grid=(1,),
        in_specs=[hbm, hbm, hbm, hbm],
        out_specs=hbm,
        scratch_shapes=[
            pltpu.VMEM((X_SLOTS,) + block, jnp.int32),
            pltpu.VMEM((O_SLOTS,) + block, jnp.int32),
            pltpu.VMEM((D_MODEL, EXPERT_FF), BF16),
            pltpu.VMEM((D_MODEL, EXPERT_FF), BF16),
            pltpu.VMEM((EXPERT_FF, D_MODEL), BF16),
            pltpu.VMEM((2, D_MODEL, EXPERT_FF), F32),
            pltpu.VMEM((2, D_MODEL, EXPERT_FF), F32),
            pltpu.VMEM((2, EXPERT_FF, D_MODEL), F32),
            pltpu.SemaphoreType.DMA((X_SLOTS,)),
            pltpu.SemaphoreType.DMA((O_SLOTS,)),
            pltpu.SemaphoreType.DMA((2, 3)),
        ],
    )
    return pl.pallas_call(
        functools.partial(_moe_kernel, layer=layer),
        grid_spec=grid_spec,
        out_shape=jax.ShapeDtypeStruct((xs.shape[0], ROW_WORDS), jnp.int32),
        compiler_params=_cparams("arbitrary"),
        name="moe_experts",
    )(block_expert, n_used, slot, next_expert, xs, wg, wu, wd)


def _combine_kernel(y_ref, w_ref, hs_ref, gtf_ref, gfin_ref, o_ref, *, final):
    w = w_ref[...]
    acc_lo = acc_hi = None
    for k in range(TOP_K):
        lo, hi = _unpack_bf16_pairs(y_ref[k])
        wk = w[:, k:k + 1]
        acc_lo = lo * wk if k == 0 else acc_lo + lo * wk
        acc_hi = hi * wk if k == 0 else acc_hi + hi * wk
    out = hs_ref[...] + gtf_ref[0] * jnp.concatenate([acc_lo, acc_hi], axis=1)
    if final:
        out = _rms(out) * gfin_ref[...]
    o_ref[...] = out


def _combine(y, w, hs, gtf, g_final, *, final):
    n_rows = hs.shape[0]
    return pl.pallas_call(
        functools.partial(_combine_kernel, final=final),
        grid=(n_rows // TM_COMB,),
        in_specs=[
            pl.BlockSpec((TOP_K, TM_COMB, ROW_WORDS), lambda i: (0, i, 0)),
            pl.BlockSpec((TM_COMB, TOP_K), lambda i: (i, 0)),
            pl.BlockSpec((TM_COMB, D_MODEL), lambda i: (i, 0)),
            pl.BlockSpec((1, 1, D_MODEL), lambda i: (jnp.minimum(i // (SEQ // TM_COMB), BATCH), 0, 0)),
            pl.BlockSpec((1, D_MODEL), lambda i: (0, 0)),
        ],
        out_specs=pl.BlockSpec((TM_COMB, D_MODEL), lambda i: (i, 0)),
        out_shape=jax.ShapeDtypeStruct((n_rows, D_MODEL), F32),
        compiler_params=_cparams("parallel"),
        name="combine_final" if final else "combine",
    )(y, w, hs, gtf, g_final)


def _rope_tables():
    rows = SEQ // GRID_W

    def angles(rot_dim):
        half = rot_dim // 2
        inv_freq = ROPE_BASE ** (-jnp.arange(0, half, 2, dtype=F32) / half)
        row = jnp.repeat(jnp.arange(rows, dtype=F32), GRID_W)
        col = jnp.tile(jnp.arange(GRID_W, dtype=F32), rows)
        ang_r = row[:, None] * inv_freq
        ang_c = col[:, None] * inv_freq
        return jnp.concatenate([ang_r, ang_r, ang_c, ang_c], axis=-1)

    def signed(sin, quarter):
        sign = jnp.where((jnp.arange(sin.shape[-1]) // quarter) % 2 == 0, -1.0, 1.0)
        return sin * sign

    def with_identity(cos, sin):
        cos = jnp.concatenate([cos, jnp.ones((TM, LANES), F32)], axis=0)
        sin = jnp.concatenate([sin, jnp.zeros((TM, LANES), F32)], axis=0)
        return cos, sin

    a64 = angles(SWA_HEAD_DIM)
    c64 = jnp.tile(jnp.cos(a64), (1, LANES // SWA_HEAD_DIM))
    s64 = jnp.tile(signed(jnp.sin(a64), SWA_HEAD_DIM // 4), (1, LANES // SWA_HEAD_DIM))
    a32 = angles(MLA_ROPE)
    pad_lo = MLA_NOPE
    pad_hi = LANES - MLA_NOPE - MLA_ROPE
    cm = jnp.concatenate([jnp.ones((SEQ, pad_lo), F32), jnp.cos(a32), jnp.ones((SEQ, pad_hi), F32)], axis=-1)
    sm = jnp.concatenate([jnp.zeros((SEQ, pad_lo), F32), signed(jnp.sin(a32), MLA_ROPE // 4),
                          jnp.zeros((SEQ, pad_hi), F32)], axis=-1)
    return with_identity(cm, sm) + with_identity(c64, s64)


def _layer0_weights(wa_in, wa_uq, wa_ukv, wa_out):
    d = D_MODEL
    cq, ckv, kr, qs, ks, vs = jnp.split(
        wa_in, [C_CKV, C_KR, C_KR + MLA_ROPE, C_KR + MLA_ROPE + SWA_HEADS * SWA_HEAD_DIM,
                C_KR + MLA_ROPE + (SWA_HEADS + SWA_KV_HEADS) * SWA_HEAD_DIM], axis=-1)
    kr_pad = jnp.concatenate([jnp.zeros((d, MLA_NOPE), F32), kr,
                              jnp.zeros((d, LANES - MLA_NOPE - MLA_ROPE), F32)], axis=-1)
    qs_pair = qs.reshape(d, SWA_KV_HEADS, SWA_GROUP, SWA_HEAD_DIM).transpose(0, 2, 1, 3).reshape(d, -1)
    win = jnp.concatenate([cq, ckv, kr_pad, qs_pair, ks, vs], axis=-1).astype(BF16)
    uq = wa_uq.reshape(MLA_Q_RANK, MLA_HEADS, MLA_NOPE + MLA_ROPE)
    uq = jnp.pad(uq, ((0, 0), (0, 0), (0, LANES - MLA_NOPE - MLA_ROPE))).reshape(MLA_Q_RANK, -1)
    ukv = wa_ukv.reshape(MLA_KV_RANK, MLA_HEADS, MLA_NOPE + MLA_V)
    uk = jnp.pad(ukv[:, :, :MLA_NOPE], ((0, 0), (0, 0), (0, LANES - MLA_NOPE))).reshape(MLA_KV_RANK, -1)
    uv = ukv[:, :, MLA_NOPE:].reshape(MLA_KV_RANK, -1)
    wukv = jnp.concatenate([uk, uv], axis=-1)
    n_mla = MLA_HEADS * MLA_V
    out_swa = wa_out[n_mla:].reshape(SWA_KV_HEADS, SWA_GROUP, SWA_HEAD_DIM, d).transpose(1, 0, 2, 3)
    wout = jnp.concatenate([wa_out[:n_mla], out_swa.reshape(-1, d)], axis=0)
    return win, uq.astype(BF16), wukv.astype(BF16), wout.astype(BF16)


def _layer1_weights(wc_in):
    per_head = 4 * DIFF_HALF + DIFF_V
    w = wc_in.reshape(D_MODEL, DIFF_HEADS, per_head)
    q = w[:, :, :2 * DIFF_HALF].reshape(D_MODEL, -1)
    k = w[:, :, 2 * DIFF_HALF:4 * DIFF_HALF].reshape(D_MODEL, -1)
    v = w[:, :, 4 * DIFF_HALF:].reshape(D_MODEL, -1)
    return jnp.concatenate([q, k, v], axis=-1).astype(BF16)


def _block_tables(counts, n_blocks):
    padded = (counts + MOE_ROWS - 1) // MOE_ROWS * MOE_ROWS
    pad_end = jnp.cumsum(padded)
    pad_start = (pad_end - padded).astype(jnp.int32)
    block_start = jnp.arange(n_blocks, dtype=jnp.int32) * MOE_ROWS
    block_expert = jnp.minimum((pad_end[None, :] <= block_start[:, None]).sum(axis=1),
                               N_EXPERTS - 1).astype(jnp.int32)
    n_used = (pad_end[-1] // MOE_ROWS).astype(jnp.int32).reshape(1)
    has = padded > 0
    slot = ((jnp.cumsum(has) - 1) & 1).astype(jnp.int32)
    ids = jnp.where(has, jnp.arange(N_EXPERTS, dtype=jnp.int32), N_EXPERTS)
    after = jnp.concatenate([lax.cummin(ids, reverse=True)[1:], jnp.full((1,), N_EXPERTS, jnp.int32)])
    next_expert = jnp.where(after < N_EXPERTS, after, -1).astype(jnp.int32)
    return pad_start, block_expert, n_used, slot, next_expert


def _moe_layer(f_packed, top_idx, top_w, rank, counts, hs, gtf, g_final, wg, wu, wd, layer, *, final):
    n_tok = top_idx.shape[1]
    n_blocks = -(-(n_tok * TOP_K) // MOE_ROWS) + N_EXPERTS
    pad_start, block_expert, n_used, slot, next_expert = _block_tables(counts[:, 0], n_blocks)
    dest = _dest(pad_start, top_idx, rank)
    xs = _sc_dispatch(f_packed, dest, n_blocks * MOE_ROWS)
    ys = _moe(block_expert, n_used, slot, next_expert, xs, wg, wu, wd, layer)
    y = _sc_gather(ys, dest).reshape(TOP_K, n_tok, ROW_WORDS)
    return _combine(y, top_w.T, hs, gtf, g_final, final=final)


def kernel(x, c, ctx, c_ctx, w_ada, b_ada, g_mix, g_ffn, wa_in, ga_q, ga_kv, wa_uq, wa_ukv, wa_sink, wa_out,
           wc_in, lam_q1, lam_k1, lam_q2, lam_k2, gc_sub, wc_out, w_router, b_router, we_gate, we_up, we_down,
           ws_gate, ws_up, ws_down, g_final):
    d = D_MODEL
    x2, ctx2 = x.reshape(N_LAT, d), ctx.reshape(N_CTX, d)
    cc =jnp.concatenate([c, c_ctx[None, :], jnp.zeros((MOD_ROWS - BATCH - 1, d), F32)], axis=0)
    mod = _ada(cc, w_ada, b_ada).reshape(DEPTH, MOD_ROWS, 6, 1, d)

    def mod_vec(layer, j):
        return mod[layer, :, j]

    cm, sm, c64, s64 = _rope_tables()
    g_final2 = g_final.reshape(1, d)

    def router_weights(layer):
        wt = w_router[layer].T
        hi = wt.astype(BF16)
        lo = (wt - hi.astype(F32)).astype(BF16)
        return hi, lo

    def shared_weights(layer):
        return ws_gate[layer].astype(BF16), ws_up[layer].astype(BF16), ws_down[layer].astype(BF16)

    win, wuq, wukv, wout0 = _layer0_weights(wa_in[0], wa_uq[0], wa_ukv[0], wa_out[0])
    qm, km, vm, qs, ks, vs = _proj0(x2, ctx2, g_mix[0].reshape(1, d), mod_vec(0, 0), mod_vec(0, 1), win,
                                    ga_q[0].reshape(1, -1), ga_kv[0].reshape(1, -1), wuq, wukv, cm, sm, c64, s64)
    sink = wa_sink[0].astype(F32)
    o_halves = [(_mla_attention(qm, km, vm, ctx_queries=False), _mla_attention(qm, km, vm, ctx_queries=True), 0),
                (_swa_attention(sink, qs, ks, vs, ctx_queries=False),
                 _swa_attention(sink, qs, ks, vs, ctx_queries=True), 0)]
    hs, f, logits_t = _post(
        o_halves, x2, ctx2, wout0, mod_vec(0, 2), g_ffn[0].reshape(1, d), mod_vec(0, 3), mod_vec(0, 4),
        mod_vec(0, 5),
        *router_weights(0), *shared_weights(0), N_TOK)
    top_idx, top_w, rank, counts = _router(logits_t, b_router[0].reshape(N_EXPERTS, 1))
    h = _moe_layer(f, top_idx, top_w, rank, counts, hs, mod_vec(0, 5), g_final2,
                   we_gate, we_up, we_down, 0, final=False)

    layer = 1
    lam_init = 0.8 - 0.6 * math.exp(-0.3 * layer)
    q1, k1, v1 = _proj1(h, g_mix[1].reshape(1, d), mod_vec(1, 0), mod_vec(1, 1), _layer1_weights(wc_in[0]),
                        c64, s64)
    lam_vecs = jnp.zeros((8, LANES), F32).at[:4, :DIFF_HALF].set(
        jnp.stack([lam_q1[0], lam_k1[0], lam_q2[0], lam_k2[0]]).astype(F32))
    o = _diff_attention(q1, k1, v1, lam_vecs, gc_sub[0].reshape(1, DIFF_V), lam_init)
    hs, f, logits_t = _post(
        [(o, o, 0), (o, o, 1)], h, h, wc_out[0].astype(BF16), mod_vec(1, 2), g_ffn[1].reshape(1, d),
        mod_vec(1, 3), mod_vec(1, 4),
        mod_vec(1, 5), *router_weights(1), *shared_weights(1), N_LAT)
    top_idx, top_w, rank, counts = _router(logits_t, b_router[1].reshape(N_EXPERTS, 1))
    out = _moe_layer(f, top_idx, top_w, rank, counts, hs, mod_vec(1, 5), g_final2,
                     we_gate, we_up, we_down, 1, final=True)
    return out.reshape(BATCH, SEQ, d)
```

```python
import functools
import math

import jax
import jax.numpy as jnp
from jax import lax
from jax.experimental import pallas as pl
from jax.experimental.pallas import tpu as pltpu
from jax.experimental.pallas import tpu_sc as plsc

F32 = jnp.float32
BF16 = jnp.bfloat16

D_MODEL = 1024
BATCH = 8
SEQ = 2048
DEPTH = 2
CTX_LEN = 256
GRID_W = 64
ROPE_BASE = 10000.0
EPS = 1e-6
NEG_INF = -1e30

MLA_HEADS = 8
MLA_Q_RANK = 384
MLA_KV_RANK = 256
MLA_NOPE = 64
MLA_ROPE = 32
MLA_V = 64
SWA_HEADS = 8
SWA_KV_HEADS = 2
SWA_HEAD_DIM = 64
SWA_GROUP = SWA_HEADS // SWA_KV_HEADS
WINDOW = 128
DIFF_HEADS = 8
DIFF_HALF = 64
DIFF_V = 128
N_EXPERTS = 64
N_GROUPS = 8
GROUP_SIZE = N_EXPERTS // N_GROUPS
TOPK_GROUPS = 4
TOP_K = 8
EXPERT_FF = 256
SHARED_FF = 256
ROUTED_SCALE = 2.5

LOG2E = math.log2(math.e)
LANES = 128
N_LAT = BATCH * SEQ
N_CTX = BATCH * CTX_LEN
N_TOK = N_LAT + N_CTX
MOD_ROWS = 16

TM = 1024
TM_CHUNK = 256
LAT_BLOCKS_PER_BATCH = SEQ // TM
TQ_MLA = 512
MLA_PAIRS_PER_STEP = 4
TQ_DIFF = 512
DIFF_HEADS_PER_STEP = 8
SWA_BLOCK = 128
SWA_QBLOCKS = 4
assert SWA_BLOCK == WINDOW
MOE_ROWS = 512
MOE_CHUNK = 256
TM_COMB = 512
VMEM_LIMIT = 56 * 1024 * 1024

C_CQ = 0
C_CKV = C_CQ + MLA_Q_RANK
C_KR = C_CKV + MLA_KV_RANK
C_QS = C_KR + LANES
C_KS = C_QS + SWA_HEADS * SWA_HEAD_DIM
C_VS = C_KS + SWA_KV_HEADS * SWA_HEAD_DIM
C_END = C_VS + SWA_KV_HEADS * SWA_HEAD_DIM


def _cparams(*sem):
    return pltpu.CompilerParams(dimension_semantics=sem, vmem_limit_bytes=VMEM_LIMIT)


def _dot(a, b):
    return jnp.dot(a, b, preferred_element_type=F32)


def _dot_nt(a, b):
    return lax.dot_general(a, b, (((1,), (1,)), ((), ())), preferred_element_type=F32)


def _rms(x):
    return x * lax.rsqrt(jnp.mean(x * x, axis=-1, keepdims=True) + EPS)


def _silu(x):
    return x * jax.nn.sigmoid(x)


def _rope(x, cos, sin_signed, shift):
    n = x.shape[-1]
    lane = lax.broadcasted_iota(jnp.int32, x.shape, 1)
    first = (lane & shift) == 0
    rot = jnp.where(first, pltpu.roll(x, n - shift, 1), pltpu.roll(x, shift, 1))
    return x * cos + rot * sin_signed


def _mod_index(i):
    return jnp.minimum(i // LAT_BLOCKS_PER_BATCH, BATCH)


def _rope_index(i):
    return jnp.where(i < N_LAT // TM, i % LAT_BLOCKS_PER_BATCH, LAT_BLOCKS_PER_BATCH)


ADA_TN = 1536


def _ada_kernel(c_ref, w_ref, b_ref, o_ref):
    s = _silu(c_ref[...]).astype(BF16)
    o_ref[0] = _dot(s, w_ref[0].astype(BF16)) + b_ref[0]


def _ada(cc, w_ada, b_ada):
    n_out = w_ada.shape[-1]
    return pl.pallas_call(
        _ada_kernel,
        grid=(DEPTH, n_out // ADA_TN),
        in_specs=[
            pl.BlockSpec((MOD_ROWS, D_MODEL), lambda l, j: (0, 0)),
            pl.BlockSpec((1, D_MODEL, ADA_TN), lambda l, j: (l, 0, j)),
            pl.BlockSpec((1, 1, ADA_TN), lambda l, j: (l, 0, j)),
        ],
        out_specs=pl.BlockSpec((1, MOD_ROWS, ADA_TN), lambda l, j: (l, 0, j)),
        out_shape=jax.ShapeDtypeStruct((DEPTH, MOD_ROWS, n_out), F32),
        compiler_params=_cparams("parallel", "parallel"),
        name="ada",
    )(cc, w_ada, b_ada.reshape(DEPTH, 1, n_out))


def _token_rows(lat_ref, ctx_ref, rows):
    return jnp.where(pl.program_id(0) < N_LAT // TM, lat_ref[rows, :], ctx_ref[rows, :])


def _token_specs(width=D_MODEL, col=0):
    lat_blocks = N_LAT // TM
    return [pl.BlockSpec((TM, width), lambda i: (jnp.minimum(i, lat_blocks - 1), col)),
            pl.BlockSpec((TM, width), lambda i: (jnp.maximum(i - lat_blocks, 0), col))]


def _proj0_kernel(x_ref, ctx_ref, g_ref, sh_ref, sc_ref, win_ref, gq_ref, gkv_ref, wuq_ref, wukv_ref,
                  cm_ref, sm_ref, c64_ref, s64_ref,
                  qm_ref, km_ref, vm_ref, qs_ref, ks_ref, vs_ref):
    q_scale = (MLA_NOPE + MLA_ROPE) ** -0.5 * LOG2E
    s_scale = SWA_HEAD_DIM ** -0.5 * LOG2E
    for c in range(TM // TM_CHUNK):
        rows = slice(c * TM_CHUNK, (c + 1) * TM_CHUNK)
        a = _rms(_token_rows(x_ref, ctx_ref, rows)) * g_ref[...]
        a = a * (1.0 + sc_ref[0]) + sh_ref[0]
        p = _dot(a.astype(BF16), win_ref[...])
        nq = _rms(p[:, C_CQ:C_CKV]) * gq_ref[...]
        nkv = _rms(p[:, C_CKV:C_KR]) * gkv_ref[...]
        q = _dot(nq.astype(BF16), wuq_ref[...])
        kv = _dot(nkv.astype(BF16), wukv_ref[...])
        cm, sm = cm_ref[rows, :], sm_ref[rows, :]
        c64, s64 = c64_ref[rows, :], s64_ref[rows, :]
        kr = _rope(p[:, C_KR:C_QS], cm, sm, MLA_ROPE // 4)
        for h in range(MLA_HEADS):
            sl = slice(h * LANES, (h + 1) * LANES)
            qm_ref[rows, sl] = (_rope(q[:, sl], cm, sm, MLA_ROPE // 4) * q_scale).astype(BF16)
            km_ref[rows, sl] = (kv[:, sl] + kr).astype(BF16)
        vm_ref[rows, :] = kv[:, MLA_HEADS * LANES:].astype(BF16)
        for g in range(SWA_GROUP):
            sl = slice(g * LANES, (g + 1) * LANES)
            qs_ref[rows, sl] = (_rope(p[:, C_QS + g * LANES:C_QS + (g + 1) * LANES], c64, s64,
                                      SWA_HEAD_DIM // 4) * s_scale).astype(BF16)
        ks_ref[rows, :] = _rope(p[:, C_KS:C_VS], c64, s64, SWA_HEAD_DIM // 4).astype(BF16)
        vs_ref[rows, :] = p[:, C_VS:C_END].astype(BF16)


def _proj0(x, ctx, g_mix, sh, sc, win, gq, gkv, wuq, wukv, cm, sm, c64, s64):
    row = lambda i: (i, 0)
    const = lambda i: (0, 0)
    mod = lambda i: (_mod_index(i), 0, 0)
    rope = lambda i: (_rope_index(i), 0)
    widths = (MLA_HEADS * LANES, MLA_HEADS * LANES, MLA_HEADS * MLA_V,
              SWA_HEADS * SWA_HEAD_DIM, SWA_KV_HEADS * SWA_HEAD_DIM, SWA_KV_HEADS * SWA_HEAD_DIM)
    return pl.pallas_call(
        _proj0_kernel,
        grid=(N_TOK // TM,),
        in_specs=_token_specs() + [
            pl.BlockSpec((1, D_MODEL), const),
            pl.BlockSpec((1, 1, D_MODEL), mod),
            pl.BlockSpec((1, 1, D_MODEL), mod),
            pl.BlockSpec(win.shape, const),
            pl.BlockSpec(gq.shape, const),
            pl.BlockSpec(gkv.shape, const),
            pl.BlockSpec(wuq.shape, const),
            pl.BlockSpec(wukv.shape, const),
            pl.BlockSpec((TM, LANES), rope),
            pl.BlockSpec((TM, LANES), rope),
            pl.BlockSpec((TM, LANES), rope),
            pl.BlockSpec((TM, LANES), rope),
        ],
        out_specs=[pl.BlockSpec((TM, w), row) for w in widths],
        out_shape=[jax.ShapeDtypeStruct((N_TOK, w), BF16) for w in widths],
        compiler_params=_cparams("parallel"),
        name="proj0",
    )(x, ctx, g_mix, sh, sc, win, gq, gkv, wuq, wukv, cm, sm, c64, s64)


def _proj1_kernel(h_ref, g_ref, sh_ref, sc_ref, w_ref, c64_ref, s64_ref, q_ref, k_ref, v_ref):
    width = DIFF_HEADS * LANES
    scale = DIFF_HALF ** -0.5 * LOG2E
    for c in range(TM // TM_CHUNK):
        rows = slice(c * TM_CHUNK, (c + 1) * TM_CHUNK)
        a = _rms(h_ref[rows, :]) * g_ref[...]
        a = a * (1.0 + sc_ref[0]) + sh_ref[0]
        p = _dot(a.astype(BF16), w_ref[...])
        c64, s64 = c64_ref[rows, :], s64_ref[rows, :]
        for h in range(DIFF_HEADS):
            sl = slice(h * LANES, (h + 1) * LANES)
            q_ref[rows, sl] = (_rope(p[:, sl], c64, s64, DIFF_HALF // 4) * scale).astype(BF16)
            k_ref[rows, sl] = _rope(p[:, width + h * LANES:width + (h + 1) * LANES], c64, s64,
                                    DIFF_HALF // 4).astype(BF16)
        v_ref[rows, :] = p[:, 2 * width:].astype(BF16)


def _proj1(h, g_mix, sh, sc, w, c64, s64):
    row = lambda i: (i, 0)
    const = lambda i: (0, 0)
    mod = lambda i: (_mod_index(i), 0, 0)
    rope = lambda i: (_rope_index(i), 0)
    width = DIFF_HEADS * LANES
    return pl.pallas_call(
        _proj1_kernel,
        grid=(N_TOK // TM,),
        in_specs=[
            pl.BlockSpec((TM, D_MODEL), row),
            pl.BlockSpec((1, D_MODEL), const),
            pl.BlockSpec((1, 1, D_MODEL), mod),
            pl.BlockSpec((1, 1, D_MODEL), mod),
            pl.BlockSpec(w.shape, const),
            pl.BlockSpec((TM, LANES), rope),
            pl.BlockSpec((TM, LANES), rope),
        ],
        out_specs=[pl.BlockSpec((TM, width), row)] * 3,
        out_shape=[jax.ShapeDtypeStruct((N_TOK, width), BF16)] * 3,
        compiler_params=_cparams("parallel"),
        name="proj1",
    )(h, g_mix, sh, sc, w, c64, s64)


def _mla_kernel(*refs, n_kv):
    q_ref = refs[0]
    k_refs = refs[1:1 + n_kv]
    v_refs = refs[1 + n_kv:1 + 2 * n_kv]
    o_ref = refs[1 + 2 * n_kv]
    lane = lax.broadcasted_iota(jnp.int32, (q_ref.shape[0], LANES), 1)
    own = [lane < MLA_V, lane >= MLA_V]
    for pair in range(MLA_PAIRS_PER_STEP):
        vsl = slice(pair * LANES, (pair + 1) * LANES)
        outs = []
        for hh in range(2):
            sl = slice((2 * pair + hh) * LANES, (2 * pair + hh + 1) * LANES)
            qh = q_ref[:, sl]
            ss = [_dot_nt(qh, k[:, sl]) for k in k_refs]
            m = functools.reduce(jnp.maximum, [jnp.max(s, axis=-1, keepdims=True) for s in ss])
            o = None
            for s, v in zip(ss, v_refs):
                vp = v[:, vsl]
                vlane = lax.broadcasted_iota(jnp.int32, vp.shape, 1)
                keep = (vlane < MLA_V) if hh == 0 else (vlane >= MLA_V)
                vh = jnp.where(keep, vp, jnp.ones_like(vp))
                part = _dot(jnp.exp2(s - m).astype(BF16), vh)
                o = part if o is None else o + part
            outs.append(o / pltpu.roll(o, MLA_V, 1))
        o_ref[:, vsl] = jnp.where(own[0], outs[0], outs[1]).astype(BF16)


def _mla_attention(qm, km, vm, *, ctx_queries):
    groups = MLA_HEADS // 2 // MLA_PAIRS_PER_STEP
    qk_w = 2 * LANES * MLA_PAIRS_PER_STEP
    v_w = LANES * MLA_PAIRS_PER_STEP
    if ctx_queries:
        tq, n_q, rows = CTX_LEN, 1, N_CTX
        q_map = lambda b, h, i: (N_LAT // CTX_LEN + b, h)
        o_map = lambda b, h, i: (b, h)
        kv_specs = [(CTX_LEN, lambda b, h, i: (N_LAT // CTX_LEN + b, h))]
    else:
        tq, n_q, rows = TQ_MLA, SEQ // TQ_MLA, N_LAT
        q_map = o_map = lambda b, h, i: (b * n_q + i, h)
        kv_specs = [(CTX_LEN, lambda b, h, i: (N_LAT // CTX_LEN + b, h)), (SEQ, lambda b, h, i: (b, h))]
    n_kv = len(kv_specs)
    in_specs = [pl.BlockSpec((tq, qk_w), q_map)]
    in_specs += [pl.BlockSpec((n, qk_w), m) for n, m in kv_specs]
    in_specs += [pl.BlockSpec((n, v_w), m) for n, m in kv_specs]
    return pl.pallas_call(
        functools.partial(_mla_kernel, n_kv=n_kv),
        grid=(BATCH, groups, n_q),
        in_specs=in_specs,
        out_specs=pl.BlockSpec((tq, v_w), o_map),
        out_shape=jax.ShapeDtypeStruct((rows, MLA_HEADS * MLA_V), BF16),
        compiler_params=_cparams("parallel", "parallel", "parallel"),
        name="mla_ctx" if ctx_queries else "mla_lat",
    )(qm, *([km] * n_kv), *([vm] * n_kv))


def _swa_kernel(sink_ref, *refs, band, n_blocks):
    q_ref = refs[0]
    if band:
        n_kv = SWA_QBLOCKS + 2
        k_blocks = refs[1:1 + n_kv]
        kx = refs[1 + n_kv]
        v_blocks = refs[2 + n_kv:2 + 2 * n_kv]
        vx, o_ref = refs[2 + 2 * n_kv], refs[3 + 2 * n_kv]
        chains, rows = SWA_QBLOCKS, SWA_BLOCK
    else:
        kx, vx, o_ref = refs[1:]
        chains, rows = 1, q_ref.shape[0]
    stacked = SWA_GROUP * rows
    k_ctx = kx[...]
    v_ctx = vx[...]
    lane = lax.broadcasted_iota(jnp.int32, (rows, LANES), 1)
    low = lane < SWA_HEAD_DIM
    row_group = lax.broadcasted_iota(jnp.int32, (stacked, 1), 0) // rows
    for c in range(chains):
        qrows = slice(c * rows, (c + 1) * rows)
        if band:
            n = pl.program_id(1) * SWA_QBLOCKS + c
            k_band = jnp.concatenate([k[...] for k in k_blocks[c:c + 3]], axis=0)
            v_band = jnp.concatenate([v[...] for v in v_blocks[c:c + 3]], axis=0)
            qq = lax.broadcasted_iota(jnp.int32, (stacked, SWA_BLOCK), 0) & (SWA_BLOCK - 1)
            kk = lax.broadcasted_iota(jnp.int32, (stacked, SWA_BLOCK), 1)
            valid_prev = kk >= qq + jnp.where(n > 0, 0, SWA_BLOCK)
            valid_next = kk <= qq - jnp.where(n < n_blocks - 1, 0, SWA_BLOCK)
        halves = []
        for hk in range(SWA_KV_HEADS):
            keep = low if hk == 0 else jnp.logical_not(low)
            qh = jnp.concatenate(
                [jnp.where(keep, q_ref[qrows, g * LANES:(g + 1) * LANES], jnp.zeros((rows, LANES), BF16))
                 for g in range(SWA_GROUP)], axis=0)
            sink = jnp.zeros((stacked, 1), F32)
            for g in range(SWA_GROUP):
                sink = jnp.where(row_group == g, sink_ref[hk * SWA_GROUP + g] * LOG2E, sink)
            s_ctx = _dot_nt(qh, k_ctx)
            m = jnp.maximum(jnp.max(s_ctx, axis=-1, keepdims=True), sink)
            if band:
                s = _dot_nt(qh, k_band)
                s_band = jnp.concatenate([jnp.where(valid_prev, s[:, :SWA_BLOCK], NEG_INF),
                                          s[:, SWA_BLOCK:2 * SWA_BLOCK],
                                          jnp.where(valid_next, s[:, 2 * SWA_BLOCK:], NEG_INF)], axis=1)
                m = jnp.maximum(m, jnp.max(s_band, axis=-1, keepdims=True))

            def ones_other(v, hk=hk):
                vlane = lax.broadcasted_iota(jnp.int32, v.shape, 1)
                own = (vlane < SWA_HEAD_DIM) if hk == 0 else (vlane >= SWA_HEAD_DIM)
                return jnp.where(own, v, jnp.ones_like(v))

            o = _dot(jnp.exp2(s_ctx - m).astype(BF16), ones_other(v_ctx))
            if band:
                o = o + _dot(jnp.exp2(s_band - m).astype(BF16), ones_other(v_band))
            denom = pltpu.roll(o, SWA_HEAD_DIM, 1) + jnp.exp2(sink - m)
            halves.append(o / denom)
        for g in range(SWA_GROUP):
            rs = slice(g * rows, (g + 1) * rows)
            o_ref[qrows, g * LANES:(g + 1) * LANES] = jnp.where(low, halves[0][rs], halves[1][rs]).astype(BF16)


def _swa_attention(sink, qs, ks, vs, *, ctx_queries):
    width = SWA_HEADS * SWA_HEAD_DIM
    kvw = SWA_KV_HEADS * SWA_HEAD_DIM
    ctx_map = lambda b, i: (N_LAT // CTX_LEN + b, 0)
    smem = pl.BlockSpec(memory_space=pltpu.SMEM)
    if ctx_queries:
        grid, rows = (BATCH, 1), N_CTX
        in_specs = [smem, pl.BlockSpec((CTX_LEN, width), ctx_map),
                    pl.BlockSpec((CTX_LEN, kvw), ctx_map), pl.BlockSpec((CTX_LEN, kvw), ctx_map)]
        args = (sink, qs, ks, vs)
        out_spec = pl.BlockSpec((CTX_LEN, width), lambda b, i: (b, 0))
        n_blocks = 1
    else:
        rows = N_LAT
        n_blocks = SEQ // SWA_BLOCK
        steps = n_blocks // SWA_QBLOCKS
        grid = (BATCH, steps)
        q_rows = SWA_QBLOCKS * SWA_BLOCK

        def kv_block(offset):
            return lambda b, i: (b * n_blocks + jnp.clip(i * SWA_QBLOCKS + offset, 0, n_blocks - 1), 0)

        band_specs = [pl.BlockSpec((SWA_BLOCK, kvw), kv_block(off)) for off in range(-1, SWA_QBLOCKS + 1)]
        in_specs = ([smem, pl.BlockSpec((q_rows, width), lambda b, i: (b * steps + i, 0))]
                    + band_specs + [pl.BlockSpec((CTX_LEN, kvw), ctx_map)]
                    + band_specs + [pl.BlockSpec((CTX_LEN, kvw), ctx_map)])
        n_band = len(band_specs)
        args = (sink, qs) + (ks,) * (n_band + 1) + (vs,) * (n_band + 1)
        out_spec = pl.BlockSpec((q_rows, width), lambda b, i: (b * steps + i, 0))
    return pl.pallas_call(
        functools.partial(_swa_kernel, band=not ctx_queries, n_blocks=n_blocks),
        grid=grid,
        in_specs=in_specs,
        out_specs=out_spec,
        out_shape=jax.ShapeDtypeStruct((rows, width), BF16),
        compiler_params=_cparams("parallel", "parallel"),
        name="swa_ctx" if ctx_queries else "swa_lat",
    )(*args)


def _diff_kernel(q_ref, kc_ref, kl_ref, vc_ref, vl_ref, lam_ref, g_ref, o_ref, *, lam_init):
    lam = (jnp.exp(jnp.sum(lam_ref[0:1, :] * lam_ref[1:2, :], axis=-1, keepdims=True))
           - jnp.exp(jnp.sum(lam_ref[2:3, :] * lam_ref[3:4, :], axis=-1, keepdims=True)) + lam_init)
    lane = lax.broadcasted_iota(jnp.int32, (q_ref.shape[0], LANES), 1)
    low = lane < DIFF_HALF
    k_refs = (kc_ref, kl_ref)
    v_refs = (vc_ref, vl_ref)
    for h in range(DIFF_HEADS_PER_STEP):
        sl = slice(h * LANES, (h + 1) * LANES)
        q = q_ref[:, sl]
        zero = jnp.zeros_like(q)

        vx = [jnp.concatenate([v[:, sl], jnp.ones((v.shape[0], LANES), BF16)], axis=1) for v in v_refs]

        def attend(qh):
            ss = [_dot_nt(qh, k[:, sl]) for k in k_refs]
            m = functools.reduce(jnp.maximum, [jnp.max(s, axis=-1, keepdims=True) for s in ss])
            ox = functools.reduce(jnp.add, [_dot(jnp.exp2(s - m).astype(BF16), v) for s, v in zip(ss, vx)])
            return ox[:, :LANES] / ox[:, LANES:]

        o = attend(jnp.where(low, q, zero)) - lam * attend(jnp.where(low, zero, q))
        o = _rms(o) * g_ref[...] * (1.0 - lam_init)
        o_ref[:, sl] = o.astype(BF16)


def _diff_attention(q, k, v, lam_vecs, g_sub, lam_init):
    n_q = SEQ // TQ_DIFF
    q_map = lambda b, h, i: (b * n_q + i, h)
    ctx_map = lambda b, h, i: (N_LAT // CTX_LEN + b, h)
    lat_map = lambda b, h, i: (b, h)
    const = lambda b, h, i: (0, 0)
    width = LANES * DIFF_HEADS_PER_STEP
    return pl.pallas_call(
        functools.partial(_diff_kernel, lam_init=lam_init),
        grid=(BATCH, DIFF_HEADS // DIFF_HEADS_PER_STEP, n_q),
        in_specs=[
            pl.BlockSpec((TQ_DIFF, width), q_map),
            pl.BlockSpec((CTX_LEN, width), ctx_map),
            pl.BlockSpec((SEQ, width), lat_map),
            pl.BlockSpec((CTX_LEN, width), ctx_map),
            pl.BlockSpec((SEQ, width), lat_map),
            pl.BlockSpec(lam_vecs.shape, const),
            pl.BlockSpec(g_sub.shape, const),
        ],
        out_specs=pl.BlockSpec((TQ_DIFF, width), q_map),
        out_shape=jax.ShapeDtypeStruct((N_LAT, DIFF_HEADS * DIFF_V), BF16),
        compiler_params=_cparams("parallel", "parallel", "parallel"),
        name="diff_attn",
    )(q, k, k, v, v, lam_vecs, g_sub)


BF16_BITS = 16
HIGH_HALF_WORD = 0xFFFF0000


def _pack_bf16_valued(x):
    n = x.shape[1] // 2
    lo = pltpu.bitcast(x[:, :n], jnp.uint32)
    hi = pltpu.bitcast(x[:, n:], jnp.uint32)
    return pltpu.bitcast((lo >> BF16_BITS) | hi, jnp.int32)


def _pack_bf16_pairs(x):
    return _pack_bf16_valued(x.astype(BF16).astype(F32))


def _unpack_bf16_pairs(w):
    u = pltpu.bitcast(w, jnp.uint32)
    lo = pltpu.bitcast(u << BF16_BITS, F32)
    hi = pltpu.bitcast(u & jnp.uint32(HIGH_HALF_WORD), F32)
    return lo, hi


def _route(logits_t, bias, cnt_ref, tri_ref, idx_ref, w_ref, rank_ref):
    tm = logits_t.shape[1]
    scores = jax.nn.sigmoid(logits_t)
    biased = scores + bias
    sub = lax.broadcasted_iota(jnp.int32, (GROUP_SIZE, tm), 0).astype(F32)
    grp_scores, grp_biased, grp_index = [], [], []
    group_score = []
    for g in range(N_GROUPS):
        sl = slice(g * GROUP_SIZE, (g + 1) * GROUP_SIZE)
        bg = biased[sl, :]
        grp_scores.append(scores[sl, :])
        grp_biased.append(bg)
        grp_index.append(sub + float(g * GROUP_SIZE))
        m1 = jnp.max(bg, axis=0, keepdims=True)
        first = jnp.min(jnp.where(bg == m1, sub, float(GROUP_SIZE)), axis=0, keepdims=True)
        m2 = jnp.max(jnp.where(sub == first, -jnp.inf, bg), axis=0, keepdims=True)
        group_score.append(m1 + m2)
    keep = [jnp.zeros((1, tm), F32) for _ in range(N_GROUPS)]
    for _ in range(TOPK_GROUPS):
        m = functools.reduce(jnp.maximum, group_score)
        found = jnp.zeros((1, tm), F32)
        for g in range(N_GROUPS):
            hit = jnp.where(group_score[g] == m, 1.0 - found, 0.0)
            keep[g] = keep[g] + hit
            found = found + hit
            group_score[g] = jnp.where(hit > 0.0, -jnp.inf, group_score[g])
    vals = [jnp.where(keep[g] > 0.0, grp_biased[g], NEG_INF) for g in range(N_GROUPS)]
    chosen = [jnp.zeros((GROUP_SIZE, tm), F32) for _ in range(N_GROUPS)]
    picked = []
    for _ in range(TOP_K):
        m = jnp.max(functools.reduce(jnp.maximum, vals), axis=0, keepdims=True)
        cand = [jnp.where(vals[g] == m, grp_index[g], float(N_EXPERTS)) for g in range(N_GROUPS)]
        ei = jnp.min(functools.reduce(jnp.minimum, cand), axis=0, keepdims=True)
        sel = [grp_index[g] == ei for g in range(N_GROUPS)]
        s_k = functools.reduce(jnp.add, [jnp.where(sel[g], grp_scores[g], 0.0) for g in range(N_GROUPS)])
        picked.append((ei, jnp.sum(s_k, axis=0, keepdims=True)))
        vals = [jnp.where(sel[g], -jnp.inf, vals[g]) for g in range(N_GROUPS)]
        chosen = [jnp.where(sel[g], 1.0, chosen[g]) for g in range(N_GROUPS)]
    total = functools.reduce(jnp.add, [s for _, s in picked])
    chosen_all = jnp.concatenate(chosen, axis=0)
    tri = tri_ref[...]
    width = tri.shape[0]
    cnt = cnt_ref[...]
    parts = []
    for j in range(tm // width):
        ch = chosen_all[:, j * width:(j + 1) * width]
        parts.append(_dot(ch.astype(BF16), tri) + cnt)
        cnt = cnt + jnp.sum(ch, axis=1, keepdims=True)
    cnt_ref[...] = cnt
    rank_all = jnp.concatenate(parts, axis=1)
    for k, (ei, s) in enumerate(picked):
        idx_ref[k:k + 1, :] = ei.astype(jnp.int32)
        w_ref[k:k + 1, :] = s / total * ROUTED_SCALE
        r = functools.reduce(jnp.add, [
            jnp.where(grp_index[g] == ei, rank_all[g * GROUP_SIZE:(g + 1) * GROUP_SIZE, :], 0.0)
            for g in range(N_GROUPS)])
        rank_ref[k:k + 1, :] = jnp.sum(r, axis=0, keepdims=True).astype(jnp.int32)


ROUTE_TN = 2048
TRI_N = 256


def _router_kernel(logits_ref, br_ref, tri_ref, idx_ref, w_ref, rank_ref, cnt_out_ref, cnt_ref):
    @pl.when(pl.program_id(0) == 0)
    def _():
        cnt_ref[...] = jnp.zeros_like(cnt_ref)

    _route(logits_ref[...], br_ref[...], cnt_ref, tri_ref, idx_ref, w_ref, rank_ref)
    cnt_out_ref[...] = jnp.broadcast_to(cnt_ref[...], cnt_out_ref.shape).astype(jnp.int32)


def _router(logits_t, br):
    n_tok = logits_t.shape[1]
    col = lambda i: (0, i)
    const = lambda i: (0, 0)
    tri = (jnp.arange(TRI_N)[:, None] < jnp.arange(TRI_N)[None, :]).astype(BF16)
    blk = pl.BlockSpec((TOP_K, ROUTE_TN), col)
    return pl.pallas_call(
        _router_kernel,
        grid=(n_tok // ROUTE_TN,),
        in_specs=[pl.BlockSpec((N_EXPERTS, ROUTE_TN), col), pl.BlockSpec(br.shape, const),
                  pl.BlockSpec(tri.shape, const)],
        out_specs=[blk, blk, blk, pl.BlockSpec((N_EXPERTS, LANES), const)],
        out_shape=[
            jax.ShapeDtypeStruct((TOP_K, n_tok), jnp.int32),
            jax.ShapeDtypeStruct((TOP_K, n_tok), F32),
            jax.ShapeDtypeStruct((TOP_K, n_tok), jnp.int32),
            jax.ShapeDtypeStruct((N_EXPERTS, LANES), jnp.int32),
        ],
        scratch_shapes=[pltpu.VMEM((N_EXPERTS, 1), F32)],
        compiler_params=_cparams("arbitrary"),
        name="router",
    )(logits_t, br, tri)


def _post_kernel(oal_ref, oac_ref, obl_ref, obc_ref, hl_ref, hc_ref, wout_ref, gtm_ref, g_ref, sh_ref, sc_ref,
                 gtf_ref, wrh_ref, wrl_ref, wsg_ref, wsu_ref, wsd_ref, hs_ref, f_ref, logits_ref):
    half = D_MODEL // 2
    for c in range(TM // TM_CHUNK):
        rows = slice(c * TM_CHUNK, (c + 1) * TM_CHUNK)
        proj = (_dot(_token_rows(oal_ref, oac_ref, rows), wout_ref[:half, :])
                + _dot(_token_rows(obl_ref, obc_ref, rows), wout_ref[half:, :]))
        h1 = _token_rows(hl_ref, hc_ref, rows) + gtm_ref[0] * proj
        f = _rms(h1) * g_ref[...]
        f = f * (1.0 + sc_ref[0]) + sh_ref[0]
        f_hi = f.astype(BF16)
        f_hi32 = f_hi.astype(F32)
        f_lo = (f - f_hi32).astype(BF16)
        f_ref[rows, :] = _pack_bf16_valued(f_hi32)
        logits_ref[:, rows] = (_dot_nt(wrh_ref[...], f_hi) + _dot_nt(wrh_ref[...], f_lo)
                               + _dot_nt(wrl_ref[...], f_hi))
        mid = _silu(_dot(f_hi, wsg_ref[...])) * _dot(f_hi, wsu_ref[...])
        shared = _dot(mid.astype(BF16), wsd_ref[...])
        hs_ref[rows, :] = h1 + gtf_ref[0] * shared


def _post(o_halves, h_lat, h_ctx, wout, gtm, g_ffn, sh, sc, gtf, wrh, wrl, wsg, wsu, wsd, n_rows):
    row = lambda i: (i, 0)
    const = lambda i: (0, 0)
    mod = lambda i: (_mod_index(i), 0, 0)
    vec = pl.BlockSpec((1, 1, D_MODEL), mod)
    full = lambda a: pl.BlockSpec(a.shape, const)
    o_specs, o_args = [], []
    for lat, ctx, col in o_halves:
        o_specs += _token_specs(D_MODEL // 2, col)
        o_args += [lat, ctx]
    return pl.pallas_call(
        _post_kernel,
        grid=(n_rows // TM,),
        in_specs=o_specs + _token_specs() + [
            full(wout), vec, full(g_ffn), vec, vec, vec,
            full(wrh), full(wrl), full(wsg), full(wsu), full(wsd),
        ],
        out_specs=[
            pl.BlockSpec((TM, D_MODEL), row),
            pl.BlockSpec((TM, D_MODEL // 2), row),
            pl.BlockSpec((N_EXPERTS, TM), lambda i: (0, i)),
        ],
        out_shape=[
            jax.ShapeDtypeStruct((n_rows, D_MODEL), F32),
            jax.ShapeDtypeStruct((n_rows, D_MODEL // 2), jnp.int32),
            jax.ShapeDtypeStruct((N_EXPERTS, n_rows), F32),
        ],
        compiler_params=_cparams("parallel"),
        name="post_attn",
    )(*o_args, h_lat, h_ctx, wout, gtm, g_ffn, sh, sc, gtf, wrh, wrl, wsg, wsu, wsd)


DEST_TN = 2048


def _dest_kernel(ps_ref, idx_ref, rank_ref, o_ref):
    idx = idx_ref[...]
    acc = rank_ref[...]
    for e in range(N_EXPERTS):
        acc = acc + jnp.where(idx == e, ps_ref[e], 0)
    o_ref[...] = acc


def _dest(pad_start, top_idx, rank):
    n_tok = top_idx.shape[1]
    blk = pl.BlockSpec((TOP_K, DEST_TN), lambda i: (0, i))
    return pl.pallas_call(
        _dest_kernel,
        grid=(n_tok // DEST_TN,),
        in_specs=[pl.BlockSpec(memory_space=pltpu.SMEM), blk, blk],
        out_specs=blk,
        out_shape=jax.ShapeDtypeStruct((TOP_K, n_tok), jnp.int32),
        compiler_params=_cparams("parallel"),
        name="dest",
    )(pad_start, top_idx, rank)


SC_CORES = 2
SC_SUBCORES = 16
SC_WORKERS = SC_CORES * SC_SUBCORES
SC_DISPATCH_ROWS = 32
SC_GATHER_ROWS = 64
ROW_WORDS = D_MODEL // 2


def _sc_mesh():
    return plsc.VectorSubcoreMesh(core_axis_name="core", subcore_axis_name="subcore")


def _sc_worker():
    return lax.axis_index("subcore") * SC_CORES + lax.axis_index("core")


def _sc_dispatch(f_packed, dest, n_slots):
    n_tok = f_packed.shape[0]
    per_worker = n_tok // SC_WORKERS
    n_chunks = per_worker // SC_DISPATCH_ROWS
    assert n_chunks * SC_DISPATCH_ROWS * SC_WORKERS == n_tok and n_chunks % 2 == 0
    dest4 = dest.reshape(TOP_K, SC_WORKERS, n_chunks, SC_DISPATCH_ROWS)

    def body(f_hbm, dest_hbm, xs_hbm, rows0, rows1, idx_v, load0, load1, scat0, scat1):
        wid = _sc_worker()
        for k in range(TOP_K):
            pltpu.sync_copy(dest_hbm.at[k, wid], idx_v.at[k])

        def load(c, buf, sem):
            start = pl.multiple_of(wid * per_worker + c * SC_DISPATCH_ROWS, 8)
            return pltpu.make_async_copy(f_hbm.at[pl.ds(start, SC_DISPATCH_ROWS)], buf, sem)

        def scatters(c, buf, sem):
            return [pltpu.make_async_copy(buf, xs_hbm.at[idx_v.at[k, c]], sem) for k in range(TOP_K)]

        load(0, rows0, load0).start()

        @pl.loop(0, n_chunks, step=2)
        def _(c):
            @pl.when(c > 0)
            def _():
                for cp in scatters(c - 1, rows1, scat1):
                    cp.wait()

            load(c + 1, rows1, load1).start()
            load(c, rows0, load0).wait()
            for cp in scatters(c, rows0, scat0):
                cp.start()
            load(c + 1, rows1, load1).wait()
            for cp in scatters(c + 1, rows1, scat1):
                cp.start()
            for cp in scatters(c, rows0, scat0):
                cp.wait()

            @pl.when(c + 2 < n_chunks)
            def _():
                load(c + 2, rows0, load0).start()

        for cp in scatters(n_chunks - 1, rows1, scat1):
            cp.wait()

    rows = pltpu.VMEM((SC_DISPATCH_ROWS, ROW_WORDS), jnp.int32)
    return pl.kernel(
        body,
        out_type=jax.ShapeDtypeStruct((n_slots, ROW_WORDS), jnp.int32),
        mesh=_sc_mesh(),
        scratch_types=[rows, rows, pltpu.VMEM((TOP_K, n_chunks, SC_DISPATCH_ROWS), jnp.int32)]
        + [pltpu.SemaphoreType.DMA] * 4,
        name="sc_dispatch",
    )(f_packed, dest4)


def _sc_gather(ys, dest):
    n_idx = dest.shape[0] * dest.shape[1]
    per_worker = n_idx // SC_WORKERS
    n_chunks = per_worker // SC_GATHER_ROWS
    assert n_chunks * SC_GATHER_ROWS * SC_WORKERS == n_idx and n_chunks % 2 == 0
    dest3 = dest.reshape(SC_WORKERS, n_chunks, SC_GATHER_ROWS)

    def body(ys_hbm, dest_hbm, out_hbm, rows0, rows1, idx_v, gat0, gat1, put0, put1):
        wid = _sc_worker()
        pltpu.sync_copy(dest_hbm.at[wid], idx_v)

        def gather(c, buf, sem):
            return pltpu.make_async_copy(ys_hbm.at[idx_v.at[c]], buf, sem)

        def put(c, buf, sem):
            start = pl.multiple_of(wid * per_worker + c * SC_GATHER_ROWS, 8)
            return pltpu.make_async_copy(buf, out_hbm.at[pl.ds(start, SC_GATHER_ROWS)], sem)

        gather(0, rows0, gat0).start()

        @pl.loop(0, n_chunks, step=2)
        def _(c):
            @pl.when(c > 0)
            def _():
                put(c - 1, rows1, put1).wait()

            gather(c + 1, rows1, gat1).start()
            gather(c, rows0, gat0).wait()
            put(c, rows0, put0).start()
            gather(c + 1, rows1, gat1).wait()
            put(c + 1, rows1, put1).start()
            put(c, rows0, put0).wait()

            @pl.when(c + 2 < n_chunks)
            def _():
                gather(c + 2, rows0, gat0).start()

        put(n_chunks - 1, rows1, put1).wait()

    rows = pltpu.VMEM((SC_GATHER_ROWS, ROW_WORDS), jnp.int32)
    return pl.kernel(
        body,
        out_type=jax.ShapeDtypeStruct((n_idx, ROW_WORDS), jnp.int32),
        mesh=_sc_mesh(),
        scratch_types=[rows, rows, pltpu.VMEM((n_chunks, SC_GATHER_ROWS), jnp.int32)]
        + [pltpu.SemaphoreType.DMA] * 4,
        name="sc_gather",
    )(ys, dest3)


X_SLOTS = 4
O_SLOTS = 3


def _moe_kernel(be_ref, nused_ref, slot_ref, next_ref, xs_hbm, wg_hbm, wu_hbm, wd_hbm, ys_hbm,
                xbuf, obuf, wg_s, wu_s, wd_s, wg_f, wu_f, wd_f, xsem, osem, wsem, *, layer):
    n_used = nused_ref[0]

    def x_copy(i, s):
        start = pl.multiple_of(i * MOE_ROWS, MOE_ROWS)
        return pltpu.make_async_copy(xs_hbm.at[pl.ds(start, MOE_ROWS)], xbuf.at[s], xsem.at[s])

    def o_copy(i, s):
        start = pl.multiple_of(i * MOE_ROWS, MOE_ROWS)
        return pltpu.make_async_copy(obuf.at[s], ys_hbm.at[pl.ds(start, MOE_ROWS)], osem.at[s])

    def weight_copies(expert, slot):
        return [pltpu.make_async_copy(src.at[layer, expert], dst.at[slot], wsem.at[slot, j])
                for j, (src, dst) in enumerate(((wg_hbm, wg_f), (wu_hbm, wu_f), (wd_hbm, wd_f)))]

    first_expert = be_ref[0]
    for cp in weight_copies(first_expert, slot_ref[first_expert]):
        cp.start()
    for j in range(X_SLOTS - 1):
        @pl.when(j < n_used)
        def _(j=j):
            x_copy(j, j).start()

    def block(i, carry):
        e = be_ref[i]
        xs = lax.rem(i, X_SLOTS)
        os = lax.rem(i, O_SLOTS)

        @pl.when(jnp.logical_or(i == 0, e != be_ref[jnp.maximum(i - 1, 0)]))
        def _():
            slot = slot_ref[e]
            for cp in weight_copies(e, slot):
                cp.wait()
            wg_s[...] = wg_f[slot].astype(BF16)
            wu_s[...] = wu_f[slot].astype(BF16)
            wd_s[...] = wd_f[slot].astype(BF16)
            nxt = next_ref[e]

            @pl.when(nxt >= 0)
            def _():
                for cp in weight_copies(nxt, 1 - slot):
                    cp.start()

        x_copy(i, xs).wait()

        @pl.when(i + X_SLOTS - 1 < n_used)
        def _():
            x_copy(i + X_SLOTS - 1, lax.rem(i + X_SLOTS - 1, X_SLOTS)).start()

        @pl.when(i >= O_SLOTS)
        def _():
            o_copy(i - O_SLOTS, os).wait()

        half = D_MODEL // 2
        for c in range(MOE_ROWS // MOE_CHUNK):
            rows = pl.ds(c * MOE_CHUNK, MOE_CHUNK)
            lo, hi = _unpack_bf16_pairs(xbuf[xs, rows, :])
            lo, hi = lo.astype(BF16), hi.astype(BF16)
            gate = _dot(lo, wg_s[:half, :]) + _dot(hi, wg_s[half:, :])
            up = _dot(lo, wu_s[:half, :]) + _dot(hi, wu_s[half:, :])
            mid = _silu(gate) * up
            obuf[os, rows, :] = _pack_bf16_pairs(_dot(mid.astype(BF16), wd_s[...]))
        o_copy(i, os).start()
        return carry

    lax.fori_loop(0, n_used, block, 0)
    for j in range(O_SLOTS):
        @pl.when(n_used > j)
        def _(j=j):
            i = n_used - 1 - j
            o_copy(i, lax.rem(i, O_SLOTS)).wait()


def _moe(block_expert, n_used, slot, next_expert, xs, wg, wu, wd, layer):
    hbm = pl.BlockSpec(memory_space=pl.ANY)
    block = (MOE_ROWS, ROW_WORDS)
    grid_spec = pltpu.PrefetchScalarGridSpec(
        num_scalar_prefetch=4,
        grid=(1,),
        in_specs=[hbm, hbm, hbm, hbm],
        out_specs=hbm,
        scratch_shapes=[
            pltpu.VMEM((X_SLOTS,) + block, jnp.int32),
            pltpu.VMEM((O_SLOTS,) + block, jnp.int32),
            pltpu.VMEM((D_MODEL, EXPERT_FF), BF16),
            pltpu.VMEM((D_MODEL, EXPERT_FF), BF16),
            pltpu.VMEM((EXPERT_FF, D_MODEL), BF16),
            pltpu.VMEM((2, D_MODEL, EXPERT_FF), F32),
            pltpu.VMEM((2, D_MODEL, EXPERT_FF), F32),
            pltpu.VMEM((2, EXPERT_FF, D_MODEL), F32),
            pltpu.SemaphoreType.DMA((X_SLOTS,)),
            pltpu.SemaphoreType.DMA((O_SLOTS,)),
            pltpu.SemaphoreType.DMA((2, 3)),
        ],
    )
    return pl.pallas_call(
        functools.partial(_moe_kernel, layer=layer),
        grid_spec=grid_spec,
        out_shape=jax.ShapeDtypeStruct((xs.shape[0], ROW_WORDS), jnp.int32),
        compiler_params=_cparams("arbitrary"),
        name="moe_experts",
    )(block_expert, n_used, slot, next_expert, xs, wg, wu, wd)


def _combine_kernel(y_ref, w_ref, hs_ref, gtf_ref, gfin_ref, o_ref, *, final):
    w = w_ref[...]
    acc_lo = acc_hi = None
    for k in range(TOP_K):
        lo, hi = _unpack_bf16_pairs(y_ref[k])
        wk = w[:, k:k + 1]
        acc_lo = lo * wk if k == 0 else acc_lo + lo * wk
        acc_hi = hi * wk if k == 0 else acc_hi + hi * wk
    out = hs_ref[...] + gtf_ref[0] * jnp.concatenate([acc_lo, acc_hi], axis=1)
    if final:
        out = _rms(out) * gfin_ref[...]
    o_ref[...] = out


def _combine(y, w, hs, gtf, g_final, *, final):
    n_rows = hs.shape[0]
    return pl.pallas_call(
        functools.partial(_combine_kernel, final=final),
        grid=(n_rows // TM_COMB,),
        in_specs=[
            pl.BlockSpec((TOP_K, TM_COMB, ROW_WORDS), lambda i: (0, i, 0)),
            pl.BlockSpec((TM_COMB, TOP_K), lambda i: (i, 0)),
            pl.BlockSpec((TM_COMB, D_MODEL), lambda i: (i, 0)),
            pl.BlockSpec((1, 1, D_MODEL), lambda i: (jnp.minimum(i // (SEQ // TM_COMB), BATCH), 0, 0)),
            pl.BlockSpec((1, D_MODEL), lambda i: (0, 0)),
        ],
        out_specs=pl.BlockSpec((TM_COMB, D_MODEL), lambda i: (i, 0)),
        out_shape=jax.ShapeDtypeStruct((n_rows, D_MODEL), F32),
        compiler_params=_cparams("parallel"),
        name="combine_final" if final else "combine",
    )(y, w, hs, gtf, g_final)


def _rope_tables():
    rows = SEQ // GRID_W

    def angles(rot_dim):
        half = rot_dim // 2
        inv_freq = ROPE_BASE ** (-jnp.arange(0, half, 2, dtype=F32) / half)
        row = jnp.repeat(jnp.arange(rows, dtype=F32), GRID_W)
        col = jnp.tile(jnp.arange(GRID_W, dtype=F32), rows)
        ang_r = row[:, None] * inv_freq
        ang_c = col[:, None] * inv_freq
        return jnp.concatenate([ang_r, ang_r, ang_c, ang_c], axis=-1)

    def signed(sin, quarter):
        sign = jnp.where((jnp.arange(sin.shape[-1]) // quarter) % 2 == 0, -1.0, 1.0)
        return sin * sign

    def with_identity(cos, sin):
        cos = jnp.concatenate([cos, jnp.ones((TM, LANES), F32)], axis=0)
        sin = jnp.concatenate([sin, jnp.zeros((TM, LANES), F32)], axis=0)
        return cos, sin

    a64 = angles(SWA_HEAD_DIM)
    c64 = jnp.tile(jnp.cos(a64), (1, LANES // SWA_HEAD_DIM))
    s64 = jnp.tile(signed(jnp.sin(a64), SWA_HEAD_DIM // 4), (1, LANES // SWA_HEAD_DIM))
    a32 = angles(MLA_ROPE)
    pad_lo = MLA_NOPE
    pad_hi = LANES - MLA_NOPE - MLA_ROPE
    cm = jnp.concatenate([jnp.ones((SEQ, pad_lo), F32), jnp.cos(a32), jnp.ones((SEQ, pad_hi), F32)], axis=-1)
    sm = jnp.concatenate([jnp.zeros((SEQ, pad_lo), F32), signed(jnp.sin(a32), MLA_ROPE // 4),
                          jnp.zeros((SEQ, pad_hi), F32)], axis=-1)
    return with_identity(cm, sm) + with_identity(c64, s64)


def _layer0_weights(wa_in, wa_uq, wa_ukv, wa_out):
    d = D_MODEL
    cq, ckv, kr, qs, ks, vs = jnp.split(
        wa_in, [C_CKV, C_KR, C_KR + MLA_ROPE, C_KR + MLA_ROPE + SWA_HEADS * SWA_HEAD_DIM,
                C_KR + MLA_ROPE + (SWA_HEADS + SWA_KV_HEADS) * SWA_HEAD_DIM], axis=-1)
    kr_pad = jnp.concatenate([jnp.zeros((d, MLA_NOPE), F32), kr,
                              jnp.zeros((d, LANES - MLA_NOPE - MLA_ROPE), F32)], axis=-1)
    qs_pair = qs.reshape(d, SWA_KV_HEADS, SWA_GROUP, SWA_HEAD_DIM).transpose(0, 2, 1, 3).reshape(d, -1)
    win = jnp.concatenate([cq, ckv, kr_pad, qs_pair, ks, vs], axis=-1).astype(BF16)
    uq = wa_uq.reshape(MLA_Q_RANK, MLA_HEADS, MLA_NOPE + MLA_ROPE)
    uq = jnp.pad(uq, ((0, 0), (0, 0), (0, LANES - MLA_NOPE - MLA_ROPE))).reshape(MLA_Q_RANK, -1)
    ukv = wa_ukv.reshape(MLA_KV_RANK, MLA_HEADS, MLA_NOPE + MLA_V)
    uk = jnp.pad(ukv[:, :, :MLA_NOPE], ((0, 0), (0, 0), (0, LANES - MLA_NOPE))).reshape(MLA_KV_RANK, -1)
    uv = ukv[:, :, MLA_NOPE:].reshape(MLA_KV_RANK, -1)
    wukv = jnp.concatenate([uk, uv], axis=-1)
    n_mla = MLA_HEADS * MLA_V
    out_swa = wa_out[n_mla:].reshape(SWA_KV_HEADS, SWA_GROUP, SWA_HEAD_DIM, d).transpose(1, 0, 2, 3)
    wout = jnp.concatenate([wa_out[:n_mla], out_swa.reshape(-1, d)], axis=0)
    return win, uq.astype(BF16), wukv.astype(BF16), wout.astype(BF16)


def _layer1_weights(wc_in):
    per_head = 4 * DIFF_HALF + DIFF_V
    w = wc_in.reshape(D_MODEL, DIFF_HEADS, per_head)
    q = w[:, :, :2 * DIFF_HALF].reshape(D_MODEL, -1)
    k = w[:, :, 2 * DIFF_HALF:4 * DIFF_HALF].reshape(D_MODEL, -1)
    v = w[:, :, 4 * DIFF_HALF:].reshape(D_MODEL, -1)
    return jnp.concatenate([q, k, v], axis=-1).astype(BF16)


def _block_tables(counts, n_blocks):
    padded = (counts + MOE_ROWS - 1) // MOE_ROWS * MOE_ROWS
    pad_end = jnp.cumsum(padded)
    pad_start = (pad_end - padded).astype(jnp.int32)
    block_start = jnp.arange(n_blocks, dtype=jnp.int32) * MOE_ROWS
    block_expert = jnp.minimum((pad_end[None, :] <= block_start[:, None]).sum(axis=1),
                               N_EXPERTS - 1).astype(jnp.int32)
    n_used = (pad_end[-1] // MOE_ROWS).astype(jnp.int32).reshape(1)
    has = padded > 0
    slot = ((jnp.cumsum(has) - 1) & 1).astype(jnp.int32)
    ids = jnp.where(has, jnp.arange(N_EXPERTS, dtype=jnp.int32), N_EXPERTS)
    after = jnp.concatenate([lax.cummin(ids, reverse=True)[1:], jnp.full((1,), N_EXPERTS, jnp.int32)])
    next_expert = jnp.where(after < N_EXPERTS, after, -1).astype(jnp.int32)
    return pad_start, block_expert, n_used, slot, next_expert


def _moe_layer(f_packed, top_idx, top_w, rank, counts, hs, gtf, g_final, wg, wu, wd, layer, *, final):
    n_tok = top_idx.shape[1]
    n_blocks = -(-(n_tok * TOP_K) // MOE_ROWS) + N_EXPERTS
    pad_start, block_expert, n_used, slot, next_expert = _block_tables(counts[:, 0], n_blocks)
    dest = _dest(pad_start, top_idx, rank)
    xs = _sc_dispatch(f_packed, dest, n_blocks * MOE_ROWS)
    ys = _moe(block_expert, n_used, slot, next_expert, xs, wg, wu, wd, layer)
    y = _sc_gather(ys, dest).reshape(TOP_K, n_tok, ROW_WORDS)
    return _combine(y, top_w.T, hs, gtf, g_final, final=final)


def kernel(x, c, ctx, c_ctx, w_ada, b_ada, g_mix, g_ffn, wa_in, ga_q, ga_kv, wa_uq, wa_ukv, wa_sink, wa_out,
           wc_in, lam_q1, lam_k1, lam_q2, lam_k2, gc_sub, wc_out, w_router, b_router, we_gate, we_up, we_down,
           ws_gate, ws_up, ws_down, g_final):
    d = D_MODEL
    x2, ctx2 = x.reshape(N_LAT, d), ctx.reshape(N_CTX, d)
    cc =jnp.concatenate([c, c_ctx[None, :], jnp.zeros((MOD_ROWS - BATCH - 1, d), F32)], axis=0)
    mod = _ada(cc, w_ada, b_ada).reshape(DEPTH, MOD_ROWS, 6, 1, d)

    def mod_vec(layer, j):
        return mod[layer, :, j]

    cm, sm, c64, s64 = _rope_tables()
    g_final2 = g_final.reshape(1, d)

    def router_weights(layer):
        wt = w_router[layer].T
        hi = wt.astype(BF16)
        lo = (wt - hi.astype(F32)).astype(BF16)
        return hi, lo

    def shared_weights(layer):
        return ws_gate[layer].astype(BF16), ws_up[layer].astype(BF16), ws_down[layer].astype(BF16)

    win, wuq, wukv, wout0 = _layer0_weights(wa_in[0], wa_uq[0], wa_ukv[0], wa_out[0])
    qm, km, vm, qs, ks, vs = _proj0(x2, ctx2, g_mix[0].reshape(1, d), mod_vec(0, 0), mod_vec(0, 1), win,
                                    ga_q[0].reshape(1, -1), ga_kv[0].reshape(1, -1), wuq, wukv, cm, sm, c64, s64)
    sink = wa_sink[0].astype(F32)
    o_halves = [(_mla_attention(qm, km, vm, ctx_queries=False), _mla_attention(qm, km, vm, ctx_queries=True), 0),
                (_swa_attention(sink, qs, ks, vs, ctx_queries=False),
                 _swa_attention(sink, qs, ks, vs, ctx_queries=True), 0)]
    hs, f, logits_t = _post(
        o_halves, x2, ctx2, wout0, mod_vec(0, 2), g_ffn[0].reshape(1, d), mod_vec(0, 3), mod_vec(0, 4),
        mod_vec(0, 5),
        *router_weights(0), *shared_weights(0), N_TOK)
    top_idx, top_w, rank, counts = _router(logits_t, b_router[0].reshape(N_EXPERTS, 1))
    h = _moe_layer(f, top_idx, top_w, rank, counts, hs, mod_vec(0, 5), g_final2,
                   we_gate, we_up, we_down, 0, final=False)

    layer = 1
    lam_init = 0.8 - 0.6 * math.exp(-0.3 * layer)
    q1, k1, v1 = _proj1(h, g_mix[1].reshape(1, d), mod_vec(1, 0), mod_vec(1, 1), _layer1_weights(wc_in[0]),
                        c64, s64)
    lam_vecs = jnp.zeros((8, LANES), F32).at[:4, :DIFF_HALF].set(
        jnp.stack([lam_q1[0], lam_k1[0], lam_q2[0], lam_k2[0]]).astype(F32))
    o = _diff_attention(q1, k1, v1, lam_vecs, gc_sub[0].reshape(1, DIFF_V), lam_init)
    hs, f, logits_t = _post(
        [(o, o, 0), (o, o, 1)], h, h, wc_out[0].astype(BF16), mod_vec(1, 2), g_ffn[1].reshape(1, d),
        mod_vec(1, 3), mod_vec(1, 4),
        mod_vec(1, 5), *router_weights(1), *shared_weights(1), N_LAT)
    top_idx, top_w, rank, counts = _router(logits_t, b_router[1].reshape(N_EXPERTS, 1))
    out = _moe_layer(f, top_idx, top_w, rank, counts, hs, mod_vec(1, 5), g_final2,
                     we_gate, we_up, we_down, 1, final=True)
    return out.reshape(BATCH, SEQ, d)
```

```python
import functools
import math

import jax
import jax.numpy as jnp
from jax import lax
from jax.experimental import pallas as pl
from jax.experimental.pallas import tpu as pltpu
from jax.experimental.pallas import tpu_sc as plsc

F32 = jnp.float32
BF16 = jnp.bfloat16

D_MODEL = 1024
BATCH = 8
SEQ = 2048
DEPTH = 2
CTX_LEN = 256
GRID_W = 64
ROPE_BASE = 10000.0
EPS = 1e-6
NEG_INF = -1e30

MLA_HEADS = 8
MLA_Q_RANK = 384
MLA_KV_RANK = 256
MLA_NOPE = 64
MLA_ROPE = 32
MLA_V = 64
SWA_HEADS = 8
SWA_KV_HEADS = 2
SWA_HEAD_DIM = 64
SWA_GROUP = SWA_HEADS // SWA_KV_HEADS
WINDOW = 128
DIFF_HEADS = 8
DIFF_HALF = 64
DIFF_V = 128
N_EXPERTS = 64
N_GROUPS = 8
GROUP_SIZE = N_EXPERTS // N_GROUPS
TOPK_GROUPS = 4
TOP_K = 8
EXPERT_FF = 256
SHARED_FF = 256
ROUTED_SCALE = 2.5

LOG2E = math.log2(math.e)
LANES = 128
N_LAT = BATCH * SEQ
N_CTX = BATCH * CTX_LEN
N_TOK = N_LAT + N_CTX
MOD_ROWS = 16

TM = 1024
TM_CHUNK = 256
LAT_BLOCKS_PER_BATCH = SEQ // TM
TQ_MLA = 512
MLA_PAIRS_PER_STEP = 4
TQ_DIFF = 512
DIFF_HEADS_PER_STEP = 8
SWA_BLOCK = 128
SWA_QBLOCKS = 4
assert SWA_BLOCK == WINDOW
MOE_ROWS = 512
MOE_CHUNK = 256
TM_COMB = 512
VMEM_LIMIT = 56 * 1024 * 1024

C_CQ = 0
C_CKV = C_CQ + MLA_Q_RANK
C_KR = C_CKV + MLA_KV_RANK
C_QS = C_KR + LANES
C_KS = C_QS + SWA_HEADS * SWA_HEAD_DIM
C_VS = C_KS + SWA_KV_HEADS * SWA_HEAD_DIM
C_END = C_VS + SWA_KV_HEADS * SWA_HEAD_DIM


def _cparams(*sem):
    return pltpu.CompilerParams(dimension_semantics=sem, vmem_limit_bytes=VMEM_LIMIT)


def _dot(a, b):
    return jnp.dot(a, b, preferred_element_type=F32)


def _dot_nt(a, b):
    return lax.dot_general(a, b, (((1,), (1,)), ((), ())), preferred_element_type=F32)


def _rms(x):
    return x * lax.rsqrt(jnp.mean(x * x, axis=-1, keepdims=True) + EPS)


def _silu(x):
    return x * jax.nn.sigmoid(x)


def _rope(x, cos, sin_signed, shift):
    n = x.shape[-1]
    lane = lax.broadcasted_iota(jnp.int32, x.shape, 1)
    first = (lane & shift) == 0
    rot = jnp.where(first, pltpu.roll(x, n - shift, 1), pltpu.roll(x, shift, 1))
    return x * cos + rot * sin_signed


def _mod_index(i):
    return jnp.minimum(i // LAT_BLOCKS_PER_BATCH, BATCH)


def _rope_index(i):
    return jnp.where(i < N_LAT // TM, i % LAT_BLOCKS_PER_BATCH, LAT_BLOCKS_PER_BATCH)


ADA_TN = 1536


def _ada_kernel(c_ref, w_ref, b_ref, o_ref):
    s = _silu(c_ref[...]).astype(BF16)
    o_ref[0] = _dot(s, w_ref[0].astype(BF16)) + b_ref[0]


def _ada(cc, w_ada, b_ada):
    n_out = w_ada.shape[-1]
    return pl.pallas_call(
        _ada_kernel,
        grid=(DEPTH, n_out // ADA_TN),
        in_specs=[
            pl.BlockSpec((MOD_ROWS, D_MODEL), lambda l, j: (0, 0)),
            pl.BlockSpec((1, D_MODEL, ADA_TN), lambda l, j: (l, 0, j)),
            pl.BlockSpec((1, 1, ADA_TN), lambda l, j: (l, 0, j)),
        ],
        out_specs=pl.BlockSpec((1, MOD_ROWS, ADA_TN), lambda l, j: (l, 0, j)),
        out_shape=jax.ShapeDtypeStruct((DEPTH, MOD_ROWS, n_out), F32),
        compiler_params=_cparams("parallel", "parallel"),
        name="ada",
    )(cc, w_ada, b_ada.reshape(DEPTH, 1, n_out))


def _token_rows(lat_ref, ctx_ref, rows):
    return jnp.where(pl.program_id(0) < N_LAT // TM, lat_ref[rows, :], ctx_ref[rows, :])


def _token_specs(width=D_MODEL, col=0):
    lat_blocks = N_LAT // TM
    return [pl.BlockSpec((TM, width), lambda i: (jnp.minimum(i, lat_blocks - 1), col)),
            pl.BlockSpec((TM, width), lambda i: (jnp.maximum(i - lat_blocks, 0), col))]


def _proj0_kernel(x_ref, ctx_ref, g_ref, sh_ref, sc_ref, win_ref, gq_ref, gkv_ref, wuq_ref, wukv_ref,
                  cm_ref, sm_ref, c64_ref, s64_ref,
                  qm_ref, km_ref, vm_ref, qs_ref, ks_ref, vs_ref):
    q_scale = (MLA_NOPE + MLA_ROPE) ** -0.5 * LOG2E
    s_scale = SWA_HEAD_DIM ** -0.5 * LOG2E
    for c in range(TM // TM_CHUNK):
        rows = slice(c * TM_CHUNK, (c + 1) * TM_CHUNK)
        a = _rms(_token_rows(x_ref, ctx_ref, rows)) * g_ref[...]
        a = a * (1.0 + sc_ref[0]) + sh_ref[0]
        p = _dot(a.astype(BF16), win_ref[...])
        nq = _rms(p[:, C_CQ:C_CKV]) * gq_ref[...]
        nkv = _rms(p[:, C_CKV:C_KR]) * gkv_ref[...]
        q = _dot(nq.astype(BF16), wuq_ref[...])
        kv = _dot(nkv.astype(BF16), wukv_ref[...])
        cm, sm = cm_ref[rows, :], sm_ref[rows, :]
        c64, s64 = c64_ref[rows, :], s64_ref[rows, :]
        kr = _rope(p[:, C_KR:C_QS], cm, sm, MLA_ROPE // 4)
        for h in range(MLA_HEADS):
            sl = slice(h * LANES, (h + 1) * LANES)
            qm_ref[rows, sl] = (_rope(q[:, sl], cm, sm, MLA_ROPE // 4) * q_scale).astype(BF16)
            km_ref[rows, sl] = (kv[:, sl] + kr).astype(BF16)
        vm_ref[rows, :] = kv[:, MLA_HEADS * LANES:].astype(BF16)
        for g in range(SWA_GROUP):
            sl = slice(g * LANES, (g + 1) * LANES)
            qs_ref[rows, sl] = (_rope(p[:, C_QS + g * LANES:C_QS + (g + 1) * LANES], c64, s64,
                                      SWA_HEAD_DIM // 4) * s_scale).astype(BF16)
        ks_ref[rows, :] = _rope(p[:, C_KS:C_VS], c64, s64, SWA_HEAD_DIM // 4).astype(BF16)
        vs_ref[rows, :] = p[:, C_VS:C_END].astype(BF16)


def _proj0(x, ctx, g_mix, sh, sc, win, gq, gkv, wuq, wukv, cm, sm, c64, s64):
    row = lambda i: (i, 0)
    const = lambda i: (0, 0)
    mod = lambda i: (_mod_index(i), 0, 0)
    rope = lambda i: (_rope_index(i), 0)
    widths = (MLA_HEADS * LANES, MLA_HEADS * LANES, MLA_HEADS * MLA_V,
              SWA_HEADS * SWA_HEAD_DIM, SWA_KV_HEADS * SWA_HEAD_DIM, SWA_KV_HEADS * SWA_HEAD_DIM)
    return pl.pallas_call(
        _proj0_kernel,
        grid=(N_TOK // TM,),
        in_specs=_token_specs() + [
            pl.BlockSpec((1, D_MODEL), const),
            pl.BlockSpec((1, 1, D_MODEL), mod),
            pl.BlockSpec((1, 1, D_MODEL), mod),
            pl.BlockSpec(win.shape, const),
            pl.BlockSpec(gq.shape, const),
            pl.BlockSpec(gkv.shape, const),
            pl.BlockSpec(wuq.shape, const),
            pl.BlockSpec(wukv.shape, const),
            pl.BlockSpec((TM, LANES), rope),
            pl.BlockSpec((TM, LANES), rope),
            pl.BlockSpec((TM, LANES), rope),
            pl.BlockSpec((TM, LANES), rope),
        ],
        out_specs=[pl.BlockSpec((TM, w), row) for w in widths],
        out_shape=[jax.ShapeDtypeStruct((N_TOK, w), BF16) for w in widths],
        compiler_params=_cparams("parallel"),
        name="proj0",
    )(x, ctx, g_mix, sh, sc, win, gq, gkv, wuq, wukv, cm, sm, c64, s64)


def _proj1_kernel(h_ref, g_ref, sh_ref, sc_ref, w_ref, c64_ref, s64_ref, q_ref, k_ref, v_ref):
    width = DIFF_HEADS * LANES
    scale = DIFF_HALF ** -0.5 * LOG2E
    for c in range(TM // TM_CHUNK):
        rows = slice(c * TM_CHUNK, (c + 1) * TM_CHUNK)
        a = _rms(h_ref[rows, :]) * g_ref[...]
        a = a * (1.0 + sc_ref[0]) + sh_ref[0]
        p = _dot(a.astype(BF16), w_ref[...])
        c64, s64 = c64_ref[rows, :], s64_ref[rows, :]
        for h in range(DIFF_HEADS):
            sl = slice(h * LANES, (h + 1) * LANES)
            q_ref[rows, sl] = (_rope(p[:, sl], c64, s64, DIFF_HALF // 4) * scale).astype(BF16)
            k_ref[rows, sl] = _rope(p[:, width + h * LANES:width + (h + 1) * LANES], c64, s64,
                                    DIFF_HALF // 4).astype(BF16)
        v_ref[rows, :] = p[:, 2 * width:].astype(BF16)


def _proj1(h, g_mix, sh, sc, w, c64, s64):
    row = lambda i: (i, 0)
    const = lambda i: (0, 0)
    mod = lambda i: (_mod_index(i), 0, 0)
    rope = lambda i: (_rope_index(i), 0)
    width = DIFF_HEADS * LANES
    return pl.pallas_call(
        _proj1_kernel,
        grid=(N_TOK // TM,),
        in_specs=[
            pl.BlockSpec((TM, D_MODEL), row),
            pl.BlockSpec((1, D_MODEL), const),
            pl.BlockSpec((1, 1, D_MODEL), mod),
            pl.BlockSpec((1, 1, D_MODEL), mod),
            pl.BlockSpec(w.shape, const),
            pl.BlockSpec((TM, LANES), rope),
            pl.BlockSpec((TM, LANES), rope),
        ],
        out_specs=[pl.BlockSpec((TM, width), row)] * 3,
        out_shape=[jax.ShapeDtypeStruct((N_TOK, width), BF16)] * 3,
        compiler_params=_cparams("parallel"),
        name="proj1",
    )(h, g_mix, sh, sc, w, c64, s64)


def _mla_kernel(*refs, n_kv):
    q_ref = refs[0]
    k_refs = refs[1:1 + n_kv]
    v_refs = refs[1 + n_kv:1 + 2 * n_kv]
    o_ref = refs[1 + 2 * n_kv]
    lane = lax.broadcasted_iota(jnp.int32, (q_ref.shape[0], LANES), 1)
    own = [lane < MLA_V, lane >= MLA_V]
    for pair in range(MLA_PAIRS_PER_STEP):
        vsl = slice(pair * LANES, (pair + 1) * LANES)
        outs = []
        for hh in range(2):
            sl = slice((2 * pair + hh) * LANES, (2 * pair + hh + 1) * LANES)
            qh = q_ref[:, sl]
            ss = [_dot_nt(qh, k[:, sl]) for k in k_refs]
            m = functools.reduce(jnp.maximum, [jnp.max(s, axis=-1, keepdims=True) for s in ss])
            o = None
            for s, v in zip(ss, v_refs):
                vp = v[:, vsl]
                vlane = lax.broadcasted_iota(jnp.int32, vp.shape, 1)
                keep = (vlane < MLA_V) if hh == 0 else (vlane >= MLA_V)
                vh = jnp.where(keep, vp, jnp.ones_like(vp))
                part = _dot(jnp.exp2(s - m).astype(BF16), vh)
                o = part if o is None else o + part
            outs.append(o / pltpu.roll(o, MLA_V, 1))
        o_ref[:, vsl] = jnp.where(own[0], outs[0], outs[1]).astype(BF16)


def _mla_attention(qm, km, vm, *, ctx_queries):
    groups = MLA_HEADS // 2 // MLA_PAIRS_PER_STEP
    qk_w = 2 * LANES * MLA_PAIRS_PER_STEP
    v_w = LANES * MLA_PAIRS_PER_STEP
    if ctx_queries:
        tq, n_q, rows = CTX_LEN, 1, N_CTX
        q_map = lambda b, h, i: (N_LAT // CTX_LEN + b, h)
        o_map = lambda b, h, i: (b, h)
        kv_specs = [(CTX_LEN, lambda b, h, i: (N_LAT // CTX_LEN + b, h))]
    else:
        tq, n_q, rows = TQ_MLA, SEQ // TQ_MLA, N_LAT
        q_map = o_map = lambda b, h, i: (b * n_q + i, h)
        kv_specs = [(CTX_LEN, lambda b, h, i: (N_LAT // CTX_LEN + b, h)), (SEQ, lambda b, h, i: (b, h))]
    n_kv = len(kv_specs)
    in_specs = [pl.BlockSpec((tq, qk_w), q_map)]
    in_specs += [pl.BlockSpec((n, qk_w), m) for n, m in kv_specs]
    in_specs += [pl.BlockSpec((n, v_w), m) for n, m in kv_specs]
    return pl.pallas_call(
        functools.partial(_mla_kernel, n_kv=n_kv),
        grid=(BATCH, groups, n_q),
        in_specs=in_specs,
        out_specs=pl.BlockSpec((tq, v_w), o_map),
        out_shape=jax.ShapeDtypeStruct((rows, MLA_HEADS * MLA_V), BF16),
        compiler_params=_cparams("parallel", "parallel", "parallel"),
        name="mla_ctx" if ctx_queries else "mla_lat",
    )(qm, *([km] * n_kv), *([vm] * n_kv))


def _swa_kernel(sink_ref, *refs, band, n_blocks):
    q_ref = refs[0]
    if band:
        n_kv = SWA_QBLOCKS + 2
        k_blocks = refs[1:1 + n_kv]
        kx = refs[1 + n_kv]
        v_blocks = refs[2 + n_kv:2 + 2 * n_kv]
        vx, o_ref = refs[2 + 2 * n_kv], refs[3 + 2 * n_kv]
        chains, rows = SWA_QBLOCKS, SWA_BLOCK
    else:
        kx, vx, o_ref = refs[1:]
        chains, rows = 1, q_ref.shape[0]
    stacked = SWA_GROUP * rows
    k_ctx = kx[...]
    v_ctx = vx[...]
    lane = lax.broadcasted_iota(jnp.int32, (rows, LANES), 1)
    low = lane < SWA_HEAD_DIM
    row_group = lax.broadcasted_iota(jnp.int32, (stacked, 1), 0) // rows
    for c in range(chains):
        qrows = slice(c * rows, (c + 1) * rows)
        if band:
            n = pl.program_id(1) * SWA_QBLOCKS + c
            k_band = jnp.concatenate([k[...] for k in k_blocks[c:c + 3]], axis=0)
            v_band = jnp.concatenate([v[...] for v in v_blocks[c:c + 3]], axis=0)
            qq = lax.broadcasted_iota(jnp.int32, (stacked, SWA_BLOCK), 0) & (SWA_BLOCK - 1)
            kk = lax.broadcasted_iota(jnp.int32, (stacked, SWA_BLOCK), 1)
            valid_prev = kk >= qq + jnp.where(n > 0, 0, SWA_BLOCK)
            valid_next = kk <= qq - jnp.where(n < n_blocks - 1, 0, SWA_BLOCK)
        halves = []
        for hk in range(SWA_KV_HEADS):
            keep = low if hk == 0 else jnp.logical_not(low)
            qh = jnp.concatenate(
                [jnp.where(keep, q_ref[qrows, g * LANES:(g + 1) * LANES], jnp.zeros((rows, LANES), BF16))
                 for g in range(SWA_GROUP)], axis=0)
            sink = jnp.zeros((stacked, 1), F32)
            for g in range(SWA_GROUP):
                sink = jnp.where(row_group == g, sink_ref[hk * SWA_GROUP + g] * LOG2E, sink)
            s_ctx = _dot_nt(qh, k_ctx)
            m = jnp.maximum(jnp.max(s_ctx, axis=-1, keepdims=True), sink)
            if band:
                s = _dot_nt(qh, k_band)
                s_band = jnp.concatenate([jnp.where(valid_prev, s[:, :SWA_BLOCK], NEG_INF),
                                          s[:, SWA_BLOCK:2 * SWA_BLOCK],
                                          jnp.where(valid_next, s[:, 2 * SWA_BLOCK:], NEG_INF)], axis=1)
                m = jnp.maximum(m, jnp.max(s_band, axis=-1, keepdims=True))

            def ones_other(v, hk=hk):
                vlane = lax.broadcasted_iota(jnp.int32, v.shape, 1)
                own = (vlane < SWA_HEAD_DIM) if hk == 0 else (vlane >= SWA_HEAD_DIM)
                return jnp.where(own, v, jnp.ones_like(v))

            o = _dot(jnp.exp2(s_ctx - m).astype(BF16), ones_other(v_ctx))
            if band:
                o = o + _dot(jnp.exp2(s_band - m).astype(BF16), ones_other(v_band))
            denom = pltpu.roll(o, SWA_HEAD_DIM, 1) + jnp.exp2(sink - m)
            halves.append(o / denom)
        for g in range(SWA_GROUP):
            rs = slice(g * rows, (g + 1) * rows)
            o_ref[qrows, g * LANES:(g + 1) * LANES] = jnp.where(low, halves[0][rs], halves[1][rs]).astype(BF16)


def _swa_attention(sink, qs, ks, vs, *, ctx_queries):
    width = SWA_HEADS * SWA_HEAD_DIM
    kvw = SWA_KV_HEADS * SWA_HEAD_DIM
    ctx_map = lambda b, i: (N_LAT // CTX_LEN + b, 0)
    smem = pl.BlockSpec(memory_space=pltpu.SMEM)
    if ctx_queries:
        grid, rows = (BATCH, 1), N_CTX
        in_specs = [smem, pl.BlockSpec((CTX_LEN, width), ctx_map),
                    pl.BlockSpec((CTX_LEN, kvw), ctx_map), pl.BlockSpec((CTX_LEN, kvw), ctx_map)]
        args = (sink, qs, ks, vs)
        out_spec = pl.BlockSpec((CTX_LEN, width), lambda b, i: (b, 0))
        n_blocks = 1
    else:
        rows = N_LAT
        n_blocks = SEQ // SWA_BLOCK
        steps = n_blocks // SWA_QBLOCKS
        grid = (BATCH, steps)
        q_rows = SWA_QBLOCKS * SWA_BLOCK

        def kv_block(offset):
            return lambda b, i: (b * n_blocks + jnp.clip(i * SWA_QBLOCKS + offset, 0, n_blocks - 1), 0)

        band_specs = [pl.BlockSpec((SWA_BLOCK, kvw), kv_block(off)) for off in range(-1, SWA_QBLOCKS + 1)]
        in_specs = ([smem, pl.BlockSpec((q_rows, width), lambda b, i: (b * steps + i, 0))]
                    + band_specs + [pl.BlockSpec((CTX_LEN, kvw), ctx_map)]
                    + band_specs + [pl.BlockSpec((CTX_LEN, kvw), ctx_map)])
        n_band = len(band_specs)
        args = (sink, qs) + (ks,) * (n_band + 1) + (vs,) * (n_band + 1)
        out_spec = pl.BlockSpec((q_rows, width), lambda b, i: (b * steps + i, 0))
    return pl.pallas_call(
        functools.partial(_swa_kernel, band=not ctx_queries, n_blocks=n_blocks),
        grid=grid,
        in_specs=in_specs,
        out_specs=out_spec,
        out_shape=jax.ShapeDtypeStruct((rows, width), BF16),
        compiler_params=_cparams("parallel", "parallel"),
        name="swa_ctx" if ctx_queries else "swa_lat",
    )(*args)


def _diff_kernel(q_ref, kc_ref, kl_ref, vc_ref, vl_ref, lam_ref, g_ref, o_ref, *, lam_init):
    lam = (jnp.exp(jnp.sum(lam_ref[0:1, :] * lam_ref[1:2, :], axis=-1, keepdims=True))
           - jnp.exp(jnp.sum(lam_ref[2:3, :] * lam_ref[3:4, :], axis=-1, keepdims=True)) + lam_init)
    lane = lax.broadcasted_iota(jnp.int32, (q_ref.shape[0], LANES), 1)
    low = lane < DIFF_HALF
    k_refs = (kc_ref, kl_ref)
    v_refs = (vc_ref, vl_ref)
    for h in range(DIFF_HEADS_PER_STEP):
        sl = slice(h * LANES, (h + 1) * LANES)
        q = q_ref[:, sl]
        zero = jnp.zeros_like(q)

        vx = [jnp.concatenate([v[:, sl], jnp.ones((v.shape[0], LANES), BF16)], axis=1) for v in v_refs]

        def attend(qh):
            ss = [_dot_nt(qh, k[:, sl]) for k in k_refs]
            m = functools.reduce(jnp.maximum, [jnp.max(s, axis=-1, keepdims=True) for s in ss])
            ox = functools.reduce(jnp.add, [_dot(jnp.exp2(s - m).astype(BF16), v) for s, v in zip(ss, vx)])
            return ox[:, :LANES] / ox[:, LANES:]

        o = attend(jnp.where(low, q, zero)) - lam * attend(jnp.where(low, zero, q))
        o = _rms(o) * g_ref[...] * (1.0 - lam_init)
        o_ref[:, sl] = o.astype(BF16)


def _diff_attention(q, k, v, lam_vecs, g_sub, lam_init):
    n_q = SEQ // TQ_DIFF
    q_map = lambda b, h, i: (b * n_q + i, h)
    ctx_map = lambda b, h, i: (N_LAT // CTX_LEN + b, h)
    lat_map = lambda b, h, i: (b, h)
    const = lambda b, h, i: (0, 0)
    width = LANES * DIFF_HEADS_PER_STEP
    return pl.pallas_call(
        functools.partial(_diff_kernel, lam_init=lam_init),
        grid=(BATCH, DIFF_HEADS // DIFF_HEADS_PER_STEP, n_q),
        in_specs=[
            pl.BlockSpec((TQ_DIFF, width), q_map),
            pl.BlockSpec((CTX_LEN, width), ctx_map),
            pl.BlockSpec((SEQ, width), lat_map),
            pl.BlockSpec((CTX_LEN, width), ctx_map),
            pl.BlockSpec((SEQ, width), lat_map),
            pl.BlockSpec(lam_vecs.shape, const),
            pl.BlockSpec(g_sub.shape, const),
        ],
        out_specs=pl.BlockSpec((TQ_DIFF, width), q_map),
        out_shape=jax.ShapeDtypeStruct((N_LAT, DIFF_HEADS * DIFF_V), BF16),
        compiler_params=_cparams("parallel", "parallel", "parallel"),
        name="diff_attn",
    )(q, k, k, v, v, lam_vecs, g_sub)


BF16_BITS = 16
HIGH_HALF_WORD = 0xFFFF0000


def _pack_bf16_valued(x):
    n = x.shape[1] // 2
    lo = pltpu.bitcast(x[:, :n], jnp.uint32)
    hi = pltpu.bitcast(x[:, n:], jnp.uint32)
    return pltpu.bitcast((lo >> BF16_BITS) | hi, jnp.int32)


def _pack_bf16_pairs(x):
    return _pack_bf16_valued(x.astype(BF16).astype(F32))


def _unpack_bf16_pairs(w):
    u = pltpu.bitcast(w, jnp.uint32)
    lo = pltpu.bitcast(u << BF16_BITS, F32)
    hi = pltpu.bitcast(u & jnp.uint32(HIGH_HALF_WORD), F32)
    return lo, hi


def _route(logits_t, bias, cnt_ref, tri_ref, idx_ref, w_ref, rank_ref):
    tm = logits_t.shape[1]
    scores = jax.nn.sigmoid(logits_t)
    biased = scores + bias
    sub = lax.broadcasted_iota(jnp.int32, (GROUP_SIZE, tm), 0).astype(F32)
    grp_scores, grp_biased, grp_index = [], [], []
    group_score = []
    for g in range(N_GROUPS):
        sl = slice(g * GROUP_SIZE, (g + 1) * GROUP_SIZE)
        bg = biased[sl, :]
        grp_scores.append(scores[sl, :])
        grp_biased.append(bg)
        grp_index.append(sub + float(g * GROUP_SIZE))
        m1 = jnp.max(bg, axis=0, keepdims=True)
        first = jnp.min(jnp.where(bg == m1, sub, float(GROUP_SIZE)), axis=0, keepdims=True)
        m2 = jnp.max(jnp.where(sub == first, -jnp.inf, bg), axis=0, keepdims=True)
        group_score.append(m1 + m2)
    keep = [jnp.zeros((1, tm), F32) for _ in range(N_GROUPS)]
    for _ in range(TOPK_GROUPS):
        m = functools.reduce(jnp.maximum, group_score)
        found = jnp.zeros((1, tm), F32)
        for g in range(N_GROUPS):
            hit = jnp.where(group_score[g] == m, 1.0 - found, 0.0)
            keep[g] = keep[g] + hit
            found = found + hit
            group_score[g] = jnp.where(hit > 0.0, -jnp.inf, group_score[g])
    vals = [jnp.where(keep[g] > 0.0, grp_biased[g], NEG_INF) for g in range(N_GROUPS)]
    chosen = [jnp.zeros((GROUP_SIZE, tm), F32) for _ in range(N_GROUPS)]
    picked = []
    for _ in range(TOP_K):
        m = jnp.max(functools.reduce(jnp.maximum, vals), axis=0, keepdims=True)
        cand = [jnp.where(vals[g] == m, grp_index[g], float(N_EXPERTS)) for g in range(N_GROUPS)]
        ei = jnp.min(functools.reduce(jnp.minimum, cand), axis=0, keepdims=True)
        sel = [grp_index[g] == ei for g in range(N_GROUPS)]
        s_k = functools.reduce(jnp.add, [jnp.where(sel[g], grp_scores[g], 0.0) for g in range(N_GROUPS)])
        picked.append((ei, jnp.sum(s_k, axis=0, keepdims=True)))
        vals = [jnp.where(sel[g], -jnp.inf, vals[g]) for g in range(N_GROUPS)]
        chosen = [jnp.where(sel[g], 1.0, chosen[g]) for g in range(N_GROUPS)]
    total = functools.reduce(jnp.add, [s for _, s in picked])
    chosen_all = jnp.concatenate(chosen, axis=0)
    tri = tri_ref[...]
    width = tri.shape[0]
    cnt = cnt_ref[...]
    parts = []
    for j in range(tm // width):
        ch = chosen_all[:, j * width:(j + 1) * width]
        parts.append(_dot(ch.astype(BF16), tri) + cnt)
        cnt = cnt + jnp.sum(ch, axis=1, keepdims=True)
    cnt_ref[...] = cnt
    rank_all = jnp.concatenate(parts, axis=1)
    for k, (ei, s) in enumerate(picked):
        idx_ref[k:k + 1, :] = ei.astype(jnp.int32)
        w_ref[k:k + 1, :] = s / total * ROUTED_SCALE
        r = functools.reduce(jnp.add, [
            jnp.where(grp_index[g] == ei, rank_all[g * GROUP_SIZE:(g + 1) * GROUP_SIZE, :], 0.0)
            for g in range(N_GROUPS)])
        rank_ref[k:k + 1, :] = jnp.sum(r, axis=0, keepdims=True).astype(jnp.int32)


ROUTE_TN = 2048
TRI_N = 256


def _router_kernel(logits_ref, br_ref, tri_ref, idx_ref, w_ref, rank_ref, cnt_out_ref, cnt_ref):
    @pl.when(pl.program_id(0) == 0)
    def _():
        cnt_ref[...] = jnp.zeros_like(cnt_ref)

    _route(logits_ref[...], br_ref[...], cnt_ref, tri_ref, idx_ref, w_ref, rank_ref)
    cnt_out_ref[...] = jnp.broadcast_to(cnt_ref[...], cnt_out_ref.shape).astype(jnp.int32)


def _router(logits_t, br):
    n_tok = logits_t.shape[1]
    col = lambda i: (0, i)
    const = lambda i: (0, 0)
    tri = (jnp.arange(TRI_N)[:, None] < jnp.arange(TRI_N)[None, :]).astype(BF16)
    blk = pl.BlockSpec((TOP_K, ROUTE_TN), col)
    return pl.pallas_call(
        _router_kernel,
        grid=(n_tok // ROUTE_TN,),
        in_specs=[pl.BlockSpec((N_EXPERTS, ROUTE_TN), col), pl.BlockSpec(br.shape, const),
                  pl.BlockSpec(tri.shape, const)],
        out_specs=[blk, blk, blk, pl.BlockSpec((N_EXPERTS, LANES), const)],
        out_shape=[
            jax.ShapeDtypeStruct((TOP_K, n_tok), jnp.int32),
            jax.ShapeDtypeStruct((TOP_K, n_tok), F32),
            jax.ShapeDtypeStruct((TOP_K, n_tok), jnp.int32),
            jax.ShapeDtypeStruct((N_EXPERTS, LANES), jnp.int32),
        ],
        scratch_shapes=[pltpu.VMEM((N_EXPERTS, 1), F32)],
        compiler_params=_cparams("arbitrary"),
        name="router",
    )(logits_t, br, tri)


def _post_kernel(oal_ref, oac_ref, obl_ref, obc_ref, hl_ref, hc_ref, wout_ref, gtm_ref, g_ref, sh_ref, sc_ref,
                 gtf_ref, wrh_ref, wrl_ref, wsg_ref, wsu_ref, wsd_ref, hs_ref, f_ref, logits_ref):
    half = D_MODEL // 2
    for c in range(TM // TM_CHUNK):
        rows = slice(c * TM_CHUNK, (c + 1) * TM_CHUNK)
        proj = (_dot(_token_rows(oal_ref, oac_ref, rows), wout_ref[:half, :])
                + _dot(_token_rows(obl_ref, obc_ref, rows), wout_ref[half:, :]))
        h1 = _token_rows(hl_ref, hc_ref, rows) + gtm_ref[0] * proj
        f = _rms(h1) * g_ref[...]
        f = f * (1.0 + sc_ref[0]) + sh_ref[0]
        f_hi = f.astype(BF16)
        f_hi32 = f_hi.astype(F32)
        f_lo = (f - f_hi32).astype(BF16)
        f_ref[rows, :] = _pack_bf16_valued(f_hi32)
        logits_ref[:, rows] = (_dot_nt(wrh_ref[...], f_hi) + _dot_nt(wrh_ref[...], f_lo)
                               + _dot_nt(wrl_ref[...], f_hi))
        mid = _silu(_dot(f_hi, wsg_ref[...])) * _dot(f_hi, wsu_ref[...])
        shared = _dot(mid.astype(BF16), wsd_ref[...])
        hs_ref[rows, :] = h1 + gtf_ref[0] * shared


def _post(o_halves, h_lat, h_ctx, wout, gtm, g_ffn, sh, sc, gtf, wrh, wrl, wsg, wsu, wsd, n_rows):
    row = lambda i: (i, 0)
    const = lambda i: (0, 0)
    mod = lambda i: (_mod_index(i), 0, 0)
    vec = pl.BlockSpec((1, 1, D_MODEL), mod)
    full = lambda a: pl.BlockSpec(a.shape, const)
    o_specs, o_args = [], []
    for lat, ctx, col in o_halves:
        o_specs += _token_specs(D_MODEL // 2, col)
        o_args += [lat, ctx]
    return pl.pallas_call(
        _post_kernel,
        grid=(n_rows // TM,),
        in_specs=o_specs + _token_specs() + [
            full(wout), vec, full(g_ffn), vec, vec, vec,
            full(wrh), full(wrl), full(wsg), full(wsu), full(wsd),
        ],
        out_specs=[
            pl.BlockSpec((TM, D_MODEL), row),
            pl.BlockSpec((TM, D_MODEL // 2), row),
            pl.BlockSpec((N_EXPERTS, TM), lambda i: (0, i)),
        ],
        out_shape=[
            jax.ShapeDtypeStruct((n_rows, D_MODEL), F32),
            jax.ShapeDtypeStruct((n_rows, D_MODEL // 2), jnp.int32),
            jax.ShapeDtypeStruct((N_EXPERTS, n_rows), F32),
        ],
        compiler_params=_cparams("parallel"),
        name="post_attn",
    )(*o_args, h_lat, h_ctx, wout, gtm, g_ffn, sh, sc, gtf, wrh, wrl, wsg, wsu, wsd)


DEST_TN = 2048


def _dest_kernel(ps_ref, idx_ref, rank_ref, o_ref):
    idx = idx_ref[...]
    acc = rank_ref[...]
    for e in range(N_EXPERTS):
        acc = acc + jnp.where(idx == e, ps_ref[e], 0)
    o_ref[...] = acc


def _dest(pad_start, top_idx, rank):
    n_tok = top_idx.shape[1]
    blk = pl.BlockSpec((TOP_K, DEST_TN), lambda i: (0, i))
    return pl.pallas_call(
        _dest_kernel,
        grid=(n_tok // DEST_TN,),
        in_specs=[pl.BlockSpec(memory_space=pltpu.SMEM), blk, blk],
        out_specs=blk,
        out_shape=jax.ShapeDtypeStruct((TOP_K, n_tok), jnp.int32),
        compiler_params=_cparams("parallel"),
        name="dest",
    )(pad_start, top_idx, rank)


SC_CORES = 2
SC_SUBCORES = 16
SC_WORKERS = SC_CORES * SC_SUBCORES
SC_DISPATCH_ROWS = 32
SC_GATHER_ROWS = 64
ROW_WORDS = D_MODEL // 2


def _sc_mesh():
    return plsc.VectorSubcoreMesh(core_axis_name="core", subcore_axis_name="subcore")


def _sc_worker():
    return lax.axis_index("subcore") * SC_CORES + lax.axis_index("core")


def _sc_dispatch(f_packed, dest, n_slots):
    n_tok = f_packed.shape[0]
    per_worker = n_tok // SC_WORKERS
    n_chunks = per_worker // SC_DISPATCH_ROWS
    assert n_chunks * SC_DISPATCH_ROWS * SC_WORKERS == n_tok and n_chunks % 2 == 0
    dest4 = dest.reshape(TOP_K, SC_WORKERS, n_chunks, SC_DISPATCH_ROWS)

    def body(f_hbm, dest_hbm, xs_hbm, rows0, rows1, idx_v, load0, load1, scat0, scat1):
        wid = _sc_worker()
        for k in range(TOP_K):
            pltpu.sync_copy(dest_hbm.at[k, wid], idx_v.at[k])

        def load(c, buf, sem):
            start = pl.multiple_of(wid * per_worker + c * SC_DISPATCH_ROWS, 8)
            return pltpu.make_async_copy(f_hbm.at[pl.ds(start, SC_DISPATCH_ROWS)], buf, sem)

        def scatters(c, buf, sem):
            return [pltpu.make_async_copy(buf, xs_hbm.at[idx_v.at[k, c]], sem) for k in range(TOP_K)]

        load(0, rows0, load0).start()

        @pl.loop(0, n_chunks, step=2)
        def _(c):
            @pl.when(c > 0)
            def _():
                for cp in scatters(c - 1, rows1, scat1):
                    cp.wait()

            load(c + 1, rows1, load1).start()
            load(c, rows0, load0).wait()
            for cp in scatters(c, rows0, scat0):
                cp.start()
            load(c + 1, rows1, load1).wait()
            for cp in scatters(c + 1, rows1, scat1):
                cp.start()
            for cp in scatters(c, rows0, scat0):
                cp.wait()

            @pl.when(c + 2 < n_chunks)
            def _():
                load(c + 2, rows0, load0).start()

        for cp in scatters(n_chunks - 1, rows1, scat1):
            cp.wait()

    rows = pltpu.VMEM((SC_DISPATCH_ROWS, ROW_WORDS), jnp.int32)
    return pl.kernel(
        body,
        out_type=jax.ShapeDtypeStruct((n_slots, ROW_WORDS), jnp.int32),
        mesh=_sc_mesh(),
        scratch_types=[rows, rows, pltpu.VMEM((TOP_K, n_chunks, SC_DISPATCH_ROWS), jnp.int32)]
        + [pltpu.SemaphoreType.DMA] * 4,
        name="sc_dispatch",
    )(f_packed, dest4)


def _sc_gather(ys, dest):
    n_idx = dest.shape[0] * dest.shape[1]
    per_worker = n_idx // SC_WORKERS
    n_chunks = per_worker // SC_GATHER_ROWS
    assert n_chunks * SC_GATHER_ROWS * SC_WORKERS == n_idx and n_chunks % 2 == 0
    dest3 = dest.reshape(SC_WORKERS, n_chunks, SC_GATHER_ROWS)

    def body(ys_hbm, dest_hbm, out_hbm, rows0, rows1, idx_v, gat0, gat1, put0, put1):
        wid = _sc_worker()
        pltpu.sync_copy(dest_hbm.at[wid], idx_v)

        def gather(c, buf, sem):
            return pltpu.make_async_copy(ys_hbm.at[idx_v.at[c]], buf, sem)

        def put(c, buf, sem):
            start = pl.multiple_of(wid * per_worker + c * SC_GATHER_ROWS, 8)
            return pltpu.make_async_copy(buf, out_hbm.at[pl.ds(start, SC_GATHER_ROWS)], sem)

        gather(0, rows0, gat0).start()

        @pl.loop(0, n_chunks, step=2)
        def _(c):
            @pl.when(c > 0)
            def _():
                put(c - 1, rows1, put1).wait()

            gather(c + 1, rows1, gat1).start()
            gather(c, rows0, gat0).wait()
            put(c, rows0, put0).start()
            gather(c + 1, rows1, gat1).wait()
            put(c + 1, rows1, put1).start()
            put(c, rows0, put0).wait()

            @pl.when(c + 2 < n_chunks)
            def _():
                gather(c + 2, rows0, gat0).start()

        put(n_chunks - 1, rows1, put1).wait()

    rows = pltpu.VMEM((SC_GATHER_ROWS, ROW_WORDS), jnp.int32)
    return pl.kernel(
        body,
        out_type=jax.ShapeDtypeStruct((n_idx, ROW_WORDS), jnp.int32),
        mesh=_sc_mesh(),
        scratch_types=[rows, rows, pltpu.VMEM((n_chunks, SC_GATHER_ROWS), jnp.int32)]
        + [pltpu.SemaphoreType.DMA] * 4,
        name="sc_gather",
    )(ys, dest3)


SC_LANES = 16
SC_REDUCE_TOKENS = 4


def _sc_combine(ys, dest, top_w):
    n_tok = dest.shape[1]
    per_worker = n_tok // SC_WORKERS
    n_chunks = per_worker // SC_REDUCE_TOKENS
    assert n_chunks * SC_REDUCE_TOKENS * SC_WORKERS == n_tok and n_chunks % 2 == 0
    pairs = SC_REDUCE_TOKENS // 2
    rows_per_gather = 2 * TOP_K
    assert rows_per_gather == SC_LANES
    idx = dest.T.reshape(SC_WORKERS, n_chunks, pairs, rows_per_gather)
    w_lanes = jnp.broadcast_to(top_w.T[:, :, None], (n_tok, TOP_K, SC_LANES))
    groups = ROW_WORDS // SC_LANES

    def body(ys_hbm, idx_hbm, w_hbm, out_hbm, rows0, rows1, w0, w1, out0, out1, idx_v, in0, in1, put0, put1):
        wid = _sc_worker()
        pltpu.sync_copy(idx_hbm.at[wid], idx_v)

        def fetch(c, rows, wv, sem):
            start = pl.multiple_of(wid * per_worker + c * SC_REDUCE_TOKENS, SC_REDUCE_TOKENS)
            copies = [pltpu.make_async_copy(ys_hbm.at[idx_v.at[c, g]], rows.at[g], sem) for g in range(pairs)]
            return copies + [pltpu.make_async_copy(w_hbm.at[pl.ds(start, SC_REDUCE_TOKENS)], wv, sem)]

        def put(c, out, sem):
            start = pl.multiple_of(wid * per_worker + c * SC_REDUCE_TOKENS, SC_REDUCE_TOKENS)
            return pltpu.make_async_copy(out, out_hbm.at[pl.ds(start, SC_REDUCE_TOKENS)], sem)

        def reduce(rows, wv, out):
            for t in range(SC_REDUCE_TOKENS):
                g, base = t // 2, (t % 2) * TOP_K
                wts = [wv[t, k, :] for k in range(TOP_K)]

                @pl.loop(0, groups)
                def _(j):
                    off = pl.multiple_of(j * SC_LANES, SC_LANES)
                    acc_lo = jnp.zeros((SC_LANES,), F32)
                    acc_hi = jnp.zeros((SC_LANES,), F32)
                    for k in range(TOP_K):
                        word = rows[g, base + k, pl.ds(off, SC_LANES)]
                        lo = plsc.bitcast(word << BF16_BITS, F32)
                        hi = plsc.bitcast(word & jnp.int32(-(1 << BF16_BITS)), F32)
                        acc_lo = acc_lo + wts[k] * lo
                        acc_hi = acc_hi + wts[k] * hi
                    out[t, pl.ds(off, SC_LANES)] = acc_lo
                    out[t, pl.ds(ROW_WORDS + off, SC_LANES)] = acc_hi

        for cp in fetch(0, rows0, w0, in0):
            cp.start()

        @pl.loop(0, n_chunks, step=2)
        def _(c):
            for cp in fetch(c + 1, rows1, w1, in1):
                cp.start()
            for cp in fetch(c, rows0, w0, in0):
                cp.wait()

            @pl.when(c > 0)
            def _():
                put(c - 2, out0, put0).wait()

            reduce(rows0, w0, out0)
            put(c, out0, put0).start()

            @pl.when(c + 2 < n_chunks)
            def _():
                for cp in fetch(c + 2, rows0, w0, in0):
                    cp.start()

            for cp in fetch(c + 1, rows1, w1, in1):
                cp.wait()

            @pl.when(c > 0)
            def _():
                put(c - 1, out1, put1).wait()

            reduce(rows1, w1, out1)
            put(c + 1, out1, put1).start()

        put(n_chunks - 2, out0, put0).wait()
        put(n_chunks - 1, out1, put1).wait()

    rows = pltpu.VMEM((pairs, rows_per_gather, ROW_WORDS), jnp.int32)
    wbuf = pltpu.VMEM((SC_REDUCE_TOKENS, TOP_K, SC_LANES), F32)
    obuf = pltpu.VMEM((SC_REDUCE_TOKENS, D_MODEL), F32)
    return pl.kernel(
        body,
        out_type=jax.ShapeDtypeStruct((n_tok, D_MODEL), F32),
        mesh=_sc_mesh(),
        scratch_types=[rows, rows, wbuf, wbuf, obuf, obuf,
                       pltpu.VMEM((n_chunks, pairs, rows_per_gather), jnp.int32)]
        + [pltpu.SemaphoreType.DMA] * 4,
        compiler_params=pltpu.CompilerParams(needs_layout_passes=False),
        name="sc_combine",
    )(ys, idx, w_lanes)


X_SLOTS = 4
O_SLOTS = 3


def _moe_kernel(be_ref, nused_ref, slot_ref, next_ref, xs_hbm, wg_hbm, wu_hbm, wd_hbm, ys_hbm,
                xbuf, obuf, wg_s, wu_s, wd_s, wg_f, wu_f, wd_f, xsem, osem, wsem, *, layer):
    n_used = nused_ref[0]

    def x_copy(i, s):
        start = pl.multiple_of(i * MOE_ROWS, MOE_ROWS)
        return pltpu.make_async_copy(xs_hbm.at[pl.ds(start, MOE_ROWS)], xbuf.at[s], xsem.at[s])

    def o_copy(i, s):
        start = pl.multiple_of(i * MOE_ROWS, MOE_ROWS)
        return pltpu.make_async_copy(obuf.at[s], ys_hbm.at[pl.ds(start, MOE_ROWS)], osem.at[s])

    def weight_copies(expert, slot):
        return [pltpu.make_async_copy(src.at[layer, expert], dst.at[slot], wsem.at[slot, j])
                for j, (src, dst) in enumerate(((wg_hbm, wg_f), (wu_hbm, wu_f), (wd_hbm, wd_f)))]

    first_expert = be_ref[0]
    for cp in weight_copies(first_expert, slot_ref[first_expert]):
        cp.start()
    for j in range(X_SLOTS - 1):
        @pl.when(j < n_used)
        def _(j=j):
            x_copy(j, j).start()

    def block(i, carry):
        e = be_ref[i]
        xs = lax.rem(i, X_SLOTS)
        os = lax.rem(i, O_SLOTS)

        @pl.when(jnp.logical_or(i == 0, e != be_ref[jnp.maximum(i - 1, 0)]))
        def _():
            slot = slot_ref[e]
            for cp in weight_copies(e, slot):
                cp.wait()
            wg_s[...] = wg_f[slot].astype(BF16)
            wu_s[...] = wu_f[slot].astype(BF16)
            wd_s[...] = wd_f[slot].astype(BF16)
            nxt = next_ref[e]

            @pl.when(nxt >= 0)
            def _():
                for cp in weight_copies(nxt, 1 - slot):
                    cp.start()

        x_copy(i, xs).wait()

        @pl.when(i + X_SLOTS - 1 < n_used)
        def _():
            x_copy(i + X_SLOTS - 1, lax.rem(i + X_SLOTS - 1, X_SLOTS)).start()

        @pl.when(i >= O_SLOTS)
        def _():
            o_copy(i - O_SLOTS, os).wait()

        half = D_MODEL // 2
        for c in range(MOE_ROWS // MOE_CHUNK):
            rows = pl.ds(c * MOE_CHUNK, MOE_CHUNK)
            lo, hi = _unpack_bf16_pairs(xbuf[xs, rows, :])
            lo, hi = lo.astype(BF16), hi.astype(BF16)
            gate = _dot(lo, wg_s[:half, :]) + _dot(hi, wg_s[half:, :])
            up = _dot(lo, wu_s[:half, :]) + _dot(hi, wu_s[half:, :])
            mid = _silu(gate) * up
            obuf[os, rows, :] = _pack_bf16_pairs(_dot(mid.astype(BF16), wd_s[...]))
        o_copy(i, os).start()
        return carry

    lax.fori_loop(0, n_used, block, 0)
    for j in range(O_SLOTS):
        @pl.when(n_used > j)
        def _(j=j):
            i = n_used - 1 - j
            o_copy(i, lax.rem(i, O_SLOTS)).wait()


def _moe(block_expert, n_used, slot, next_expert, xs, wg, wu, wd, layer):
    hbm = pl.BlockSpec(memory_space=pl.ANY)
    block = (MOE_ROWS, ROW_WORDS)
    grid_spec = pltpu.PrefetchScalarGridSpec(
        num_scalar_prefetch=4,
        grid=(1,),
        in_specs=[hbm, hbm, hbm, hbm],
        out_specs=hbm,
        scratch_shapes=[
            pltpu.VMEM((X_SLOTS,) + block, jnp.int32),
            pltpu.VMEM((O_SLOTS,) + block, jnp.int32),
            pltpu.VMEM((D_MODEL, EXPERT_FF), BF16),
            pltpu.VMEM((D_MODEL, EXPERT_FF), BF16),
            pltpu.VMEM((EXPERT_FF, D_MODEL), BF16),
            pltpu.VMEM((2, D_MODEL, EXPERT_FF), F32),
            pltpu.VMEM((2, D_MODEL, EXPERT_FF), F32),
            pltpu.VMEM((2, EXPERT_FF, D_MODEL), F32),
            pltpu.SemaphoreType.DMA((X_SLOTS,)),
            pltpu.SemaphoreType.DMA((O_SLOTS,)),
            pltpu.SemaphoreType.DMA((2, 3)),
        ],
    )
    return pl.pallas_call(
        functools.partial(_moe_kernel, layer=layer),
        grid_spec=grid_spec,
        out_shape=jax.ShapeDtypeStruct((xs.shape[0], ROW_WORDS), jnp.int32),
        compiler_params=_cparams("arbitrary"),
        name="moe_experts",
    )(block_expert, n_used, slot, next_expert, xs, wg, wu, wd)


def _combine_kernel(y_ref, hs_ref, gtf_ref, gfin_ref, o_ref, *, final):
    out = hs_ref[...] + gtf_ref[0] * y_ref[...]
    if final:
        out = _rms(out) * gfin_ref[...]
    o_ref[...] = out


def _combine(y, hs, gtf, g_final, *, final):
    n_rows = hs.shape[0]
    return pl.pallas_call(
        functools.partial(_combine_kernel, final=final),
        grid=(n_rows // TM_COMB,),
        in_specs=[
            pl.BlockSpec((TM_COMB, D_MODEL), lambda i: (i, 0)),
            pl.BlockSpec((TM_COMB, D_MODEL), lambda i: (i, 0)),
            pl.BlockSpec((1, 1, D_MODEL), lambda i: (jnp.minimum(i // (SEQ // TM_COMB), BATCH), 0, 0)),
            pl.BlockSpec((1, D_MODEL), lambda i: (0, 0)),
        ],
        out_specs=pl.BlockSpec((TM_COMB, D_MODEL), lambda i: (i, 0)),
        out_shape=jax.ShapeDtypeStruct((n_rows, D_MODEL), F32),
        compiler_params=_cparams("parallel"),
        name="combine_final" if final else "combine",
    )(y, hs, gtf, g_final)


def _rope_tables():
    rows = SEQ // GRID_W

    def angles(rot_dim):
        half = rot_dim // 2
        inv_freq = ROPE_BASE ** (-jnp.arange(0, half, 2, dtype=F32) / half)
        row = jnp.repeat(jnp.arange(rows, dtype=F32), GRID_W)
        col = jnp.tile(jnp.arange(GRID_W, dtype=F32), rows)
        ang_r = row[:, None] * inv_freq
        ang_c = col[:, None] * inv_freq
        return jnp.concatenate([ang_r, ang_r, ang_c, ang_c], axis=-1)

    def signed(sin, quarter):
        sign = jnp.where((jnp.arange(sin.shape[-1]) // quarter) % 2 == 0, -1.0, 1.0)
        return sin * sign

    def with_identity(cos, sin):
        cos = jnp.concatenate([cos, jnp.ones((TM, LANES), F32)], axis=0)
        sin = jnp.concatenate([sin, jnp.zeros((TM, LANES), F32)], axis=0)
        return cos, sin

    a64 = angles(SWA_HEAD_DIM)
    c64 = jnp.tile(jnp.cos(a64), (1, LANES // SWA_HEAD_DIM))
    s64 = jnp.tile(signed(jnp.sin(a64), SWA_HEAD_DIM // 4), (1, LANES // SWA_HEAD_DIM))
    a32 = angles(MLA_ROPE)
    pad_lo = MLA_NOPE
    pad_hi = LANES - MLA_NOPE - MLA_ROPE
    cm = jnp.concatenate([jnp.ones((SEQ, pad_lo), F32), jnp.cos(a32), jnp.ones((SEQ, pad_hi), F32)], axis=-1)
    sm = jnp.concatenate([jnp.zeros((SEQ, pad_lo), F32), signed(jnp.sin(a32), MLA_ROPE // 4),
                          jnp.zeros((SEQ, pad_hi), F32)], axis=-1)
    return with_identity(cm, sm) + with_identity(c64, s64)


def _layer0_weights(wa_in, wa_uq, wa_ukv, wa_out):
    d = D_MODEL
    cq, ckv, kr, qs, ks, vs = jnp.split(
        wa_in, [C_CKV, C_KR, C_KR + MLA_ROPE, C_KR + MLA_ROPE + SWA_HEADS * SWA_HEAD_DIM,
                C_KR + MLA_ROPE + (SWA_HEADS + SWA_KV_HEADS) * SWA_HEAD_DIM], axis=-1)
    kr_pad = jnp.concatenate([jnp.zeros((d, MLA_NOPE), F32), kr,
                              jnp.zeros((d, LANES - MLA_NOPE - MLA_ROPE), F32)], axis=-1)
    qs_pair = qs.reshape(d, SWA_KV_HEADS, SWA_GROUP, SWA_HEAD_DIM).transpose(0, 2, 1, 3).reshape(d, -1)
    win = jnp.concatenate([cq, ckv, kr_pad, qs_pair, ks, vs], axis=-1).astype(BF16)
    uq = wa_uq.reshape(MLA_Q_RANK, MLA_HEADS, MLA_NOPE + MLA_ROPE)
    uq = jnp.pad(uq, ((0, 0), (0, 0), (0, LANES - MLA_NOPE - MLA_ROPE))).reshape(MLA_Q_RANK, -1)
    ukv = wa_ukv.reshape(MLA_KV_RANK, MLA_HEADS, MLA_NOPE + MLA_V)
    uk = jnp.pad(ukv[:, :, :MLA_NOPE], ((0, 0), (0, 0), (0, LANES - MLA_NOPE))).reshape(MLA_KV_RANK, -1)
    uv = ukv[:, :, MLA_NOPE:].reshape(MLA_KV_RANK, -1)
    wukv = jnp.concatenate([uk, uv], axis=-1)
    n_mla = MLA_HEADS * MLA_V
    out_swa = wa_out[n_mla:].reshape(SWA_KV_HEADS, SWA_GROUP, SWA_HEAD_DIM, d).transpose(1, 0, 2, 3)
    wout = jnp.concatenate([wa_out[:n_mla], out_swa.reshape(-1, d)], axis=0)
    return win, uq.astype(BF16), wukv.astype(BF16), wout.astype(BF16)


def _layer1_weights(wc_in):
    per_head = 4 * DIFF_HALF + DIFF_V
    w = wc_in.reshape(D_MODEL, DIFF_HEADS, per_head)
    q = w[:, :, :2 * DIFF_HALF].reshape(D_MODEL, -1)
    k = w[:, :, 2 * DIFF_HALF:4 * DIFF_HALF].reshape(D_MODEL, -1)
    v = w[:, :, 4 * DIFF_HALF:].reshape(D_MODEL, -1)
    return jnp.concatenate([q, k, v], axis=-1).astype(BF16)


def _block_tables(counts, n_blocks):
    padded = (counts + MOE_ROWS - 1) // MOE_ROWS * MOE_ROWS
    pad_end = jnp.cumsum(padded)
    pad_start = (pad_end - padded).astype(jnp.int32)
    block_start = jnp.arange(n_blocks, dtype=jnp.int32) * MOE_ROWS
    block_expert = jnp.minimum((pad_end[None, :] <= block_start[:, None]).sum(axis=1),
                               N_EXPERTS - 1).astype(jnp.int32)
    n_used = (pad_end[-1] // MOE_ROWS).astype(jnp.int32).reshape(1)
    has = padded > 0
    slot = ((jnp.cumsum(has) - 1) & 1).astype(jnp.int32)
    ids = jnp.where(has, jnp.arange(N_EXPERTS, dtype=jnp.int32), N_EXPERTS)
    after = jnp.concatenate([lax.cummin(ids, reverse=True)[1:], jnp.full((1,), N_EXPERTS, jnp.int32)])
    next_expert = jnp.where(after < N_EXPERTS, after, -1).astype(jnp.int32)
    return pad_start, block_expert, n_used, slot, next_expert


def _moe_layer(f_packed, top_idx, top_w, rank, counts, hs, gtf, g_final, wg, wu, wd, layer, *, final):
    n_tok = top_idx.shape[1]
    n_blocks = -(-(n_tok * TOP_K) // MOE_ROWS) + N_EXPERTS
    pad_start, block_expert, n_used, slot, next_expert = _block_tables(counts[:, 0], n_blocks)
    dest = _dest(pad_start, top_idx, rank)
    xs = _sc_dispatch(f_packed, dest, n_blocks * MOE_ROWS)
    ys = _moe(block_expert, n_used, slot, next_expert, xs, wg, wu, wd, layer)
    y = _sc_combine(ys, dest, top_w)
    return _combine(y, hs, gtf, g_final, final=final)


def kernel(x, c, ctx, c_ctx, w_ada, b_ada, g_mix, g_ffn, wa_in, ga_q, ga_kv, wa_uq, wa_ukv, wa_sink, wa_out,
           wc_in, lam_q1, lam_k1, lam_q2, lam_k2, gc_sub, wc_out, w_router, b_router, we_gate, we_up, we_down,
           ws_gate, ws_up, ws_down, g_final):
    d = D_MODEL
    x2, ctx2 = x.reshape(N_LAT, d), ctx.reshape(N_CTX, d)
    cc =jnp.concatenate([c, c_ctx[None, :], jnp.zeros((MOD_ROWS - BATCH - 1, d), F32)], axis=0)
    mod = _ada(cc, w_ada, b_ada).reshape(DEPTH, MOD_ROWS, 6, 1, d)

    def mod_vec(layer, j):
        return mod[layer, :, j]

    cm, sm, c64, s64 = _rope_tables()
    g_final2 = g_final.reshape(1, d)

    def router_weights(layer):
        wt = w_router[layer].T
        hi = wt.astype(BF16)
        lo = (wt - hi.astype(F32)).astype(BF16)
        return hi, lo

    def shared_weights(layer):
        return ws_gate[layer].astype(BF16), ws_up[layer].astype(BF16), ws_down[layer].astype(BF16)

    win, wuq, wukv, wout0 = _layer0_weights(wa_in[0], wa_uq[0], wa_ukv[0], wa_out[0])
    qm, km, vm, qs, ks, vs = _proj0(x2, ctx2, g_mix[0].reshape(1, d), mod_vec(0, 0), mod_vec(0, 1), win,
                                    ga_q[0].reshape(1, -1), ga_kv[0].reshape(1, -1), wuq, wukv, cm, sm, c64, s64)
    sink = wa_sink[0].astype(F32)
    o_halves = [(_mla_attention(qm, km, vm, ctx_queries=False), _mla_attention(qm, km, vm, ctx_queries=True), 0),
                (_swa_attention(sink, qs, ks, vs, ctx_queries=False),
                 _swa_attention(sink, qs, ks, vs, ctx_queries=True), 0)]
    hs, f, logits_t = _post(
        o_halves, x2, ctx2, wout0, mod_vec(0, 2), g_ffn[0].reshape(1, d), mod_vec(0, 3), mod_vec(0, 4),
        mod_vec(0, 5),
        *router_weights(0), *shared_weights(0), N_TOK)
    top_idx, top_w, rank, counts = _router(logits_t, b_router[0].reshape(N_EXPERTS, 1))
    h = _moe_layer(f, top_idx, top_w, rank, counts, hs, mod_vec(0, 5), g_final2,
                   we_gate, we_up, we_down, 0, final=False)

    layer = 1
    lam_init = 0.8 - 0.6 * math.exp(-0.3 * layer)
    q1, k1, v1 = _proj1(h, g_mix[1].reshape(1, d), mod_vec(1, 0), mod_vec(1, 1), _layer1_weights(wc_in[0]),
                        c64, s64)
    lam_vecs = jnp.zeros((8, LANES), F32).at[:4, :DIFF_HALF].set(
        jnp.stack([lam_q1[0], lam_k1[0], lam_q2[0], lam_k2[0]]).astype(F32))
    o = _diff_attention(q1, k1, v1, lam_vecs, gc_sub[0].reshape(1, DIFF_V), lam_init)
    hs, f, logits_t = _post(
        [(o, o, 0), (o, o, 1)], h, h, wc_out[0].astype(BF16), mod_vec(1, 2), g_ffn[1].reshape(1, d),
        mod_vec(1, 3), mod_vec(1, 4),
        mod_vec(1, 5), *router_weights(1), *shared_weights(1), N_LAT)
    top_idx, top_w, rank, counts = _router(logits_t, b_router[1].reshape(N_EXPERTS, 1))
    out = _moe_layer(f, top_idx, top_w, rank, counts, hs, mod_vec(1, 5), g_final2,
                     we_gate, we_up, we_down, 1, final=True)
    return out.reshape(BATCH, SEQ, d)
```

```python
import functools
import math

import jax
import jax.numpy as jnp
from jax import lax
from jax.experimental import pallas as pl
from jax.experimental.pallas import tpu as pltpu
from jax.experimental.pallas import tpu_sc as plsc

F32 = jnp.float32
BF16 = jnp.bfloat16

D_MODEL = 1024
BATCH = 8
SEQ = 2048
DEPTH = 2
CTX_LEN = 256
GRID_W = 64
ROPE_BASE = 10000.0
EPS = 1e-6
NEG_INF = -1e30

MLA_HEADS = 8
MLA_Q_RANK = 384
MLA_KV_RANK = 256
MLA_NOPE = 64
MLA_ROPE = 32
MLA_V = 64
SWA_HEADS = 8
SWA_KV_HEADS = 2
SWA_HEAD_DIM = 64
SWA_GROUP = SWA_HEADS // SWA_KV_HEADS
WINDOW = 128
DIFF_HEADS = 8
DIFF_HALF = 64
DIFF_V = 128
N_EXPERTS = 64
N_GROUPS = 8
GROUP_SIZE = N_EXPERTS // N_GROUPS
TOPK_GROUPS = 4
TOP_K = 8
EXPERT_FF = 256
SHARED_FF = 256
ROUTED_SCALE = 2.5

LOG2E = math.log2(math.e)
LANES = 128
N_LAT = BATCH * SEQ
N_CTX = BATCH * CTX_LEN
N_TOK = N_LAT + N_CTX
MOD_ROWS = 16

TM = 1024
TM_CHUNK = 256
LAT_BLOCKS_PER_BATCH = SEQ // TM
TQ_MLA = 512
MLA_PAIRS_PER_STEP = 4
TQ_DIFF = 512
DIFF_HEADS_PER_STEP = 8
SWA_BLOCK = 128
SWA_QBLOCKS = 4
assert SWA_BLOCK == WINDOW
MOE_ROWS = 512
MOE_CHUNK = 256
TM_COMB = 512
VMEM_LIMIT = 56 * 1024 * 1024

C_CQ = 0
C_CKV = C_CQ + MLA_Q_RANK
C_KR = C_CKV + MLA_KV_RANK
C_QS = C_KR + LANES
C_KS = C_QS + SWA_HEADS * SWA_HEAD_DIM
C_VS = C_KS + SWA_KV_HEADS * SWA_HEAD_DIM
C_END = C_VS + SWA_KV_HEADS * SWA_HEAD_DIM


def _cparams(*sem):
    return pltpu.CompilerParams(dimension_semantics=sem, vmem_limit_bytes=VMEM_LIMIT)


def _dot(a, b):
    return jnp.dot(a, b, preferred_element_type=F32)


def _dot_nt(a, b):
    return lax.dot_general(a, b, (((1,), (1,)), ((), ())), preferred_element_type=F32)


def _rms(x):
    return x * lax.rsqrt(jnp.mean(x * x, axis=-1, keepdims=True) + EPS)


def _silu(x):
    return x * jax.nn.sigmoid(x)


def _rope(x, cos, sin_signed, shift):
    n = x.shape[-1]
    lane = lax.broadcasted_iota(jnp.int32, x.shape, 1)
    first = (lane & shift) == 0
    rot = jnp.where(first, pltpu.roll(x, n - shift, 1), pltpu.roll(x, shift, 1))
    return x * cos + rot * sin_signed


def _mod_index(i):
    return jnp.minimum(i // LAT_BLOCKS_PER_BATCH, BATCH)


def _rope_index(i):
    return jnp.where(i < N_LAT // TM, i % LAT_BLOCKS_PER_BATCH, LAT_BLOCKS_PER_BATCH)


ADA_TN = 1536


def _ada_kernel(c_ref, w_ref, b_ref, o_ref):
    s = _silu(c_ref[...]).astype(BF16)
    o_ref[0] = _dot(s, w_ref[0].astype(BF16)) + b_ref[0]


def _ada(cc, w_ada, b_ada):
    n_out = w_ada.shape[-1]
    return pl.pallas_call(
        _ada_kernel,
        grid=(DEPTH, n_out // ADA_TN),
        in_specs=[
            pl.BlockSpec((MOD_ROWS, D_MODEL), lambda l, j: (0, 0)),
            pl.BlockSpec((1, D_MODEL, ADA_TN), lambda l, j: (l, 0, j)),
            pl.BlockSpec((1, 1, ADA_TN), lambda l, j: (l, 0, j)),
        ],
        out_specs=pl.BlockSpec((1, MOD_ROWS, ADA_TN), lambda l, j: (l, 0, j)),
        out_shape=jax.ShapeDtypeStruct((DEPTH, MOD_ROWS, n_out), F32),
        compiler_params=_cparams("parallel", "parallel"),
        name="ada",
    )(cc, w_ada, b_ada.reshape(DEPTH, 1, n_out))


def _token_rows(lat_ref, ctx_ref, rows):
    return jnp.where(pl.program_id(0) < N_LAT // TM, lat_ref[rows, :], ctx_ref[rows, :])


def _token_specs(width=D_MODEL, col=0):
    lat_blocks = N_LAT // TM
    return [pl.BlockSpec((TM, width), lambda i: (jnp.minimum(i, lat_blocks - 1), col)),
            pl.BlockSpec((TM, width), lambda i: (jnp.maximum(i - lat_blocks, 0), col))]


def _proj0_kernel(x_ref, ctx_ref, g_ref, sh_ref, sc_ref, win_ref, gq_ref, gkv_ref, wuq_ref, wukv_ref,
                  cm_ref, sm_ref, c64_ref, s64_ref,
                  qm_ref, km_ref, vm_ref, qs_ref, ks_ref, vs_ref):
    q_scale = (MLA_NOPE + MLA_ROPE) ** -0.5 * LOG2E
    s_scale = SWA_HEAD_DIM ** -0.5 * LOG2E
    for c in range(TM // TM_CHUNK):
        rows = slice(c * TM_CHUNK, (c + 1) * TM_CHUNK)
        a = _rms(_token_rows(x_ref, ctx_ref, rows)) * g_ref[...]
        a = a * (1.0 + sc_ref[0]) + sh_ref[0]
        p = _dot(a.astype(BF16), win_ref[...])
        nq = _rms(p[:, C_CQ:C_CKV]) * gq_ref[...]
        nkv = _rms(p[:, C_CKV:C_KR]) * gkv_ref[...]
        q = _dot(nq.astype(BF16), wuq_ref[...])
        kv = _dot(nkv.astype(BF16), wukv_ref[...])
        cm, sm = cm_ref[rows, :], sm_ref[rows, :]
        c64, s64 = c64_ref[rows, :], s64_ref[rows, :]
        kr = _rope(p[:, C_KR:C_QS], cm, sm, MLA_ROPE // 4)
        for h in range(MLA_HEADS):
            sl = slice(h * LANES, (h + 1) * LANES)
            qm_ref[rows, sl] = (_rope(q[:, sl], cm, sm, MLA_ROPE // 4) * q_scale).astype(BF16)
            km_ref[rows, sl] = (kv[:, sl] + kr).astype(BF16)
        vm_ref[rows, :] = kv[:, MLA_HEADS * LANES:].astype(BF16)
        for g in range(SWA_GROUP):
            sl = slice(g * LANES, (g + 1) * LANES)
            qs_ref[rows, sl] = (_rope(p[:, C_QS + g * LANES:C_QS + (g + 1) * LANES], c64, s64,
                                      SWA_HEAD_DIM // 4) * s_scale).astype(BF16)
        ks_ref[rows, :] = _rope(p[:, C_KS:C_VS], c64, s64, SWA_HEAD_DIM // 4).astype(BF16)
        vs_ref[rows, :] = p[:, C_VS:C_END].astype(BF16)


def _proj0(x, ctx, g_mix, sh, sc, win, gq, gkv, wuq, wukv, cm, sm, c64, s64):
    row = lambda i: (i, 0)
    const = lambda i: (0, 0)
    mod = lambda i: (_mod_index(i), 0, 0)
    rope = lambda i: (_rope_index(i), 0)
    widths = (MLA_HEADS * LANES, MLA_HEADS * LANES, MLA_HEADS * MLA_V,
              SWA_HEADS * SWA_HEAD_DIM, SWA_KV_HEADS * SWA_HEAD_DIM, SWA_KV_HEADS * SWA_HEAD_DIM)
    return pl.pallas_call(
        _proj0_kernel,
        grid=(N_TOK // TM,),
        in_specs=_token_specs() + [
            pl.BlockSpec((1, D_MODEL), const),
            pl.BlockSpec((1, 1, D_MODEL), mod),
            pl.BlockSpec((1, 1, D_MODEL), mod),
            pl.BlockSpec(win.shape, const),
            pl.BlockSpec(gq.shape, const),
            pl.BlockSpec(gkv.shape, const),
            pl.BlockSpec(wuq.shape, const),
            pl.BlockSpec(wukv.shape, const),
            pl.BlockSpec((TM, LANES), rope),
            pl.BlockSpec((TM, LANES), rope),
            pl.BlockSpec((TM, LANES), rope),
            pl.BlockSpec((TM, LANES), rope),
        ],
        out_specs=[pl.BlockSpec((TM, w), row) for w in widths],
        out_shape=[jax.ShapeDtypeStruct((N_TOK, w), BF16) for w in widths],
        compiler_params=_cparams("parallel"),
        name="proj0",
    )(x, ctx, g_mix, sh, sc, win, gq, gkv, wuq, wukv, cm, sm, c64, s64)


def _proj1_kernel(h_ref, g_ref, sh_ref, sc_ref, w_ref, c64_ref, s64_ref, q_ref, k_ref, v_ref):
    width = DIFF_HEADS * LANES
    scale = DIFF_HALF ** -0.5 * LOG2E
    for c in range(TM // TM_CHUNK):
        rows = slice(c * TM_CHUNK, (c + 1) * TM_CHUNK)
        a = _rms(h_ref[rows, :]) * g_ref[...]
        a = a * (1.0 + sc_ref[0]) + sh_ref[0]
        p = _dot(a.astype(BF16), w_ref[...])
        c64, s64 = c64_ref[rows, :], s64_ref[rows, :]
        for h in range(DIFF_HEADS):
            sl = slice(h * LANES, (h + 1) * LANES)
            q_ref[rows, sl] = (_rope(p[:, sl], c64, s64, DIFF_HALF // 4) * scale).astype(BF16)
            k_ref[rows, sl] = _rope(p[:, width + h * LANES:width + (h + 1) * LANES], c64, s64,
                                    DIFF_HALF // 4).astype(BF16)
        v_ref[rows, :] = p[:, 2 * width:].astype(BF16)


def _proj1(h, g_mix, sh, sc, w, c64, s64):
    row = lambda i: (i, 0)
    const = lambda i: (0, 0)
    mod = lambda i: (_mod_index(i), 0, 0)
    rope = lambda i: (_rope_index(i), 0)
    width = DIFF_HEADS * LANES
    return pl.pallas_call(
        _proj1_kernel,
        grid=(N_TOK // TM,),
        in_specs=[
            pl.BlockSpec((TM, D_MODEL), row),
            pl.BlockSpec((1, D_MODEL), const),
            pl.BlockSpec((1, 1, D_MODEL), mod),
            pl.BlockSpec((1, 1, D_MODEL), mod),
            pl.BlockSpec(w.shape, const),
            pl.BlockSpec((TM, LANES), rope),
            pl.BlockSpec((TM, LANES), rope),
        ],
        out_specs=[pl.BlockSpec((TM, width), row)] * 3,
        out_shape=[jax.ShapeDtypeStruct((N_TOK, width), BF16)] * 3,
        compiler_params=_cparams("parallel"),
        name="proj1",
    )(h, g_mix, sh, sc, w, c64, s64)


def _mla_kernel(*refs, n_kv):
    q_ref = refs[0]
    k_refs = refs[1:1 + n_kv]
    v_refs = refs[1 + n_kv:1 + 2 * n_kv]
    o_ref = refs[1 + 2 * n_kv]
    lane = lax.broadcasted_iota(jnp.int32, (q_ref.shape[0], LANES), 1)
    own = [lane < MLA_V, lane >= MLA_V]
    for pair in range(MLA_PAIRS_PER_STEP):
        vsl = slice(pair * LANES, (pair + 1) * LANES)
        outs = []
        for hh in range(2):
            sl = slice((2 * pair + hh) * LANES, (2 * pair + hh + 1) * LANES)
            qh = q_ref[:, sl]
            ss = [_dot_nt(qh, k[:, sl]) for k in k_refs]
            m = functools.reduce(jnp.maximum, [jnp.max(s, axis=-1, keepdims=True) for s in ss])
            o = None
            for s, v in zip(ss, v_refs):
                vp = v[:, vsl]
                vlane = lax.broadcasted_iota(jnp.int32, vp.shape, 1)
                keep = (vlane < MLA_V) if hh == 0 else (vlane >= MLA_V)
                vh = jnp.where(keep, vp, jnp.ones_like(vp))
                part = _dot(jnp.exp2(s - m).astype(BF16), vh)
                o = part if o is None else o + part
            outs.append(o / pltpu.roll(o, MLA_V, 1))
        o_ref[:, vsl] = jnp.where(own[0], outs[0], outs[1]).astype(BF16)


def _mla_attention(qm, km, vm, *, ctx_queries):
    groups = MLA_HEADS // 2 // MLA_PAIRS_PER_STEP
    qk_w = 2 * LANES * MLA_PAIRS_PER_STEP
    v_w = LANES * MLA_PAIRS_PER_STEP
    if ctx_queries:
        tq, n_q, rows = CTX_LEN, 1, N_CTX
        q_map = lambda b, h, i: (N_LAT // CTX_LEN + b, h)
        o_map = lambda b, h, i: (b, h)
        kv_specs = [(CTX_LEN, lambda b, h, i: (N_LAT // CTX_LEN + b, h))]
    else:
        tq, n_q, rows = TQ_MLA, SEQ // TQ_MLA, N_LAT
        q_map = o_map = lambda b, h, i: (b * n_q + i, h)
        kv_specs = [(CTX_LEN, lambda b, h, i: (N_LAT // CTX_LEN + b, h)), (SEQ, lambda b, h, i: (b, h))]
    n_kv = len(kv_specs)
    in_specs = [pl.BlockSpec((tq, qk_w), q_map)]
    in_specs += [pl.BlockSpec((n, qk_w), m) for n, m in kv_specs]
    in_specs += [pl.BlockSpec((n, v_w), m) for n, m in kv_specs]
    return pl.pallas_call(
        functools.partial(_mla_kernel, n_kv=n_kv),
        grid=(BATCH, groups, n_q),
        in_specs=in_specs,
        out_specs=pl.BlockSpec((tq, v_w), o_map),
        out_shape=jax.ShapeDtypeStruct((rows, MLA_HEADS * MLA_V), BF16),
        compiler_params=_cparams("parallel", "parallel", "parallel"),
        name="mla_ctx" if ctx_queries else "mla_lat",
    )(qm, *([km] * n_kv), *([vm] * n_kv))


def _swa_kernel(sink_ref, *refs, band, n_blocks):
    q_ref = refs[0]
    if band:
        n_kv = SWA_QBLOCKS + 2
        k_blocks = refs[1:1 + n_kv]
        kx = refs[1 + n_kv]
        v_blocks = refs[2 + n_kv:2 + 2 * n_kv]
        vx, o_ref = refs[2 + 2 * n_kv], refs[3 + 2 * n_kv]
        chains, rows = SWA_QBLOCKS, SWA_BLOCK
    else:
        kx, vx, o_ref = refs[1:]
        chains, rows = 1, q_ref.shape[0]
    stacked = SWA_GROUP * rows
    k_ctx = kx[...]
    v_ctx = vx[...]
    lane = lax.broadcasted_iota(jnp.int32, (rows, LANES), 1)
    low = lane < SWA_HEAD_DIM
    row_group = lax.broadcasted_iota(jnp.int32, (stacked, 1), 0) // rows
    for c in range(chains):
        qrows = slice(c * rows, (c + 1) * rows)
        if band:
            n = pl.program_id(1) * SWA_QBLOCKS + c
            k_band = jnp.concatenate([k[...] for k in k_blocks[c:c + 3]], axis=0)
            v_band = jnp.concatenate([v[...] for v in v_blocks[c:c + 3]], axis=0)
            qq = lax.broadcasted_iota(jnp.int32, (stacked, SWA_BLOCK), 0) & (SWA_BLOCK - 1)
            kk = lax.broadcasted_iota(jnp.int32, (stacked, SWA_BLOCK), 1)
            valid_prev = kk >= qq + jnp.where(n > 0, 0, SWA_BLOCK)
            valid_next = kk <= qq - jnp.where(n < n_blocks - 1, 0, SWA_BLOCK)
        halves = []
        for hk in range(SWA_KV_HEADS):
            keep = low if hk == 0 else jnp.logical_not(low)
            qh = jnp.concatenate(
                [jnp.where(keep, q_ref[qrows, g * LANES:(g + 1) * LANES], jnp.zeros((rows, LANES), BF16))
                 for g in range(SWA_GROUP)], axis=0)
            sink = jnp.zeros((stacked, 1), F32)
            for g in range(SWA_GROUP):
                sink = jnp.where(row_group == g, sink_ref[hk * SWA_GROUP + g] * LOG2E, sink)
            s_ctx = _dot_nt(qh, k_ctx)
            m = jnp.maximum(jnp.max(s_ctx, axis=-1, keepdims=True), sink)
            if band:
                s = _dot_nt(qh, k_band)
                s_band = jnp.concatenate([jnp.where(valid_prev, s[:, :SWA_BLOCK], NEG_INF),
                                          s[:, SWA_BLOCK:2 * SWA_BLOCK],
                                          jnp.where(valid_next, s[:, 2 * SWA_BLOCK:], NEG_INF)], axis=1)
                m = jnp.maximum(m, jnp.max(s_band, axis=-1, keepdims=True))

            def ones_other(v, hk=hk):
                vlane = lax.broadcasted_iota(jnp.int32, v.shape, 1)
                own = (vlane < SWA_HEAD_DIM) if hk == 0 else (vlane >= SWA_HEAD_DIM)
                return jnp.where(own, v, jnp.ones_like(v))

            o = _dot(jnp.exp2(s_ctx - m).astype(BF16), ones_other(v_ctx))
            if band:
                o = o + _dot(jnp.exp2(s_band - m).astype(BF16), ones_other(v_band))
            denom = pltpu.roll(o, SWA_HEAD_DIM, 1) + jnp.exp2(sink - m)
            halves.append(o / denom)
        for g in range(SWA_GROUP):
            rs = slice(g * rows, (g + 1) * rows)
            o_ref[qrows, g * LANES:(g + 1) * LANES] = jnp.where(low, halves[0][rs], halves[1][rs]).astype(BF16)


def _swa_attention(sink, qs, ks, vs, *, ctx_queries):
    width = SWA_HEADS * SWA_HEAD_DIM
    kvw = SWA_KV_HEADS * SWA_HEAD_DIM
    ctx_map = lambda b, i: (N_LAT // CTX_LEN + b, 0)
    smem = pl.BlockSpec(memory_space=pltpu.SMEM)
    if ctx_queries:
        grid, rows = (BATCH, 1), N_CTX
        in_specs = [smem, pl.BlockSpec((CTX_LEN, width), ctx_map),
                    pl.BlockSpec((CTX_LEN, kvw), ctx_map), pl.BlockSpec((CTX_LEN, kvw), ctx_map)]
        args = (sink, qs, ks, vs)
        out_spec = pl.BlockSpec((CTX_LEN, width), lambda b, i: (b, 0))
        n_blocks = 1
    else:
        rows = N_LAT
        n_blocks = SEQ // SWA_BLOCK
        steps = n_blocks // SWA_QBLOCKS
        grid = (BATCH, steps)
        q_rows = SWA_QBLOCKS * SWA_BLOCK

        def kv_block(offset):
            return lambda b, i: (b * n_blocks + jnp.clip(i * SWA_QBLOCKS + offset, 0, n_blocks - 1), 0)

        band_specs = [pl.BlockSpec((SWA_BLOCK, kvw), kv_block(off)) for off in range(-1, SWA_QBLOCKS + 1)]
        in_specs = ([smem, pl.BlockSpec((q_rows, width), lambda b, i: (b * steps + i, 0))]
                    + band_specs + [pl.BlockSpec((CTX_LEN, kvw), ctx_map)]
                    + band_specs + [pl.BlockSpec((CTX_LEN, kvw), ctx_map)])
        n_band = len(band_specs)
        args = (sink, qs) + (ks,) * (n_band + 1) + (vs,) * (n_band + 1)
        out_spec = pl.BlockSpec((q_rows, width), lambda b, i: (b * steps + i, 0))
    return pl.pallas_call(
        functools.partial(_swa_kernel, band=not ctx_queries, n_blocks=n_blocks),
        grid=grid,
        in_specs=in_specs,
        out_specs=out_spec,
        out_shape=jax.ShapeDtypeStruct((rows, width), BF16),
        compiler_params=_cparams("parallel", "parallel"),
        name="swa_ctx" if ctx_queries else "swa_lat",
    )(*args)


def _diff_kernel(q_ref, kc_ref, kl_ref, vc_ref, vl_ref, lam_ref, g_ref, o_ref, *, lam_init):
    lam = (jnp.exp(jnp.sum(lam_ref[0:1, :] * lam_ref[1:2, :], axis=-1, keepdims=True))
           - jnp.exp(jnp.sum(lam_ref[2:3, :] * lam_ref[3:4, :], axis=-1, keepdims=True)) + lam_init)
    lane = lax.broadcasted_iota(jnp.int32, (q_ref.shape[0], LANES), 1)
    low = lane < DIFF_HALF
    k_refs = (kc_ref, kl_ref)
    v_refs = (vc_ref, vl_ref)
    for h in range(DIFF_HEADS_PER_STEP):
        sl = slice(h * LANES, (h + 1) * LANES)
        q = q_ref[:, sl]
        zero = jnp.zeros_like(q)

        vx = [jnp.concatenate([v[:, sl], jnp.ones((v.shape[0], LANES), BF16)], axis=1) for v in v_refs]

        def attend(qh):
            ss = [_dot_nt(qh, k[:, sl]) for k in k_refs]
            m = functools.reduce(jnp.maximum, [jnp.max(s, axis=-1, keepdims=True) for s in ss])
            ox = functools.reduce(jnp.add, [_dot(jnp.exp2(s - m).astype(BF16), v) for s, v in zip(ss, vx)])
            return ox[:, :LANES] / ox[:, LANES:]

        o = attend(jnp.where(low, q, zero)) - lam * attend(jnp.where(low, zero, q))
        o = _rms(o) * g_ref[...] * (1.0 - lam_init)
        o_ref[:, sl] = o.astype(BF16)


def _diff_attention(q, k, v, lam_vecs, g_sub, lam_init):
    n_q = SEQ // TQ_DIFF
    q_map = lambda b, h, i: (b * n_q + i, h)
    ctx_map = lambda b, h, i: (N_LAT // CTX_LEN + b, h)
    lat_map = lambda b, h, i: (b, h)
    const = lambda b, h, i: (0, 0)
    width = LANES * DIFF_HEADS_PER_STEP
    return pl.pallas_call(
        functools.partial(_diff_kernel, lam_init=lam_init),
        grid=(BATCH, DIFF_HEADS // DIFF_HEADS_PER_STEP, n_q),
        in_specs=[
            pl.BlockSpec((TQ_DIFF, width), q_map),
            pl.BlockSpec((CTX_LEN, width), ctx_map),
            pl.BlockSpec((SEQ, width), lat_map),
            pl.BlockSpec((CTX_LEN, width), ctx_map),
            pl.BlockSpec((SEQ, width), lat_map),
            pl.BlockSpec(lam_vecs.shape, const),
            pl.BlockSpec(g_sub.shape, const),
        ],
        out_specs=pl.BlockSpec((TQ_DIFF, width), q_map),
        out_shape=jax.ShapeDtypeStruct((N_LAT, DIFF_HEADS * DIFF_V), BF16),
        compiler_params=_cparams("parallel", "parallel", "parallel"),
        name="diff_attn",
    )(q, k, k, v, v, lam_vecs, g_sub)


BF16_BITS = 16
HIGH_HALF_WORD = 0xFFFF0000


def _pack_bf16_valued(x):
    n = x.shape[1] // 2
    lo = pltpu.bitcast(x[:, :n], jnp.uint32)
    hi = pltpu.bitcast(x[:, n:], jnp.uint32)
    return pltpu.bitcast((lo >> BF16_BITS) | hi, jnp.int32)


def _pack_bf16_pairs(x):
    return _pack_bf16_valued(x.astype(BF16).astype(F32))


def _unpack_bf16_pairs(w):
    u = pltpu.bitcast(w, jnp.uint32)
    lo = pltpu.bitcast(u << BF16_BITS, F32)
    hi = pltpu.bitcast(u & jnp.uint32(HIGH_HALF_WORD), F32)
    return lo, hi


def _route(logits_t, bias, cnt_ref, tri_ref, idx_ref, w_ref, rank_ref):
    tm = logits_t.shape[1]
    scores = jax.nn.sigmoid(logits_t)
    biased = scores + bias
    sub = lax.broadcasted_iota(jnp.int32, (GROUP_SIZE, tm), 0).astype(F32)
    grp_scores, grp_biased, grp_index = [], [], []
    group_score = []
    for g in range(N_GROUPS):
        sl = slice(g * GROUP_SIZE, (g + 1) * GROUP_SIZE)
        bg = biased[sl, :]
        grp_scores.append(scores[sl, :])
        grp_biased.append(bg)
        grp_index.append(sub + float(g * GROUP_SIZE))
        m1 = jnp.max(bg, axis=0, keepdims=True)
        first = jnp.min(jnp.where(bg == m1, sub, float(GROUP_SIZE)), axis=0, keepdims=True)
        m2 = jnp.max(jnp.where(sub == first, -jnp.inf, bg), axis=0, keepdims=True)
        group_score.append(m1 + m2)
    keep = [jnp.zeros((1, tm), F32) for _ in range(N_GROUPS)]
    for _ in range(TOPK_GROUPS):
        m = functools.reduce(jnp.maximum, group_score)
        found = jnp.zeros((1, tm), F32)
        for g in range(N_GROUPS):
            hit = jnp.where(group_score[g] == m, 1.0 - found, 0.0)
            keep[g] = keep[g] + hit
            found = found + hit
            group_score[g] = jnp.where(hit > 0.0, -jnp.inf, group_score[g])
    vals = [jnp.where(keep[g] > 0.0, grp_biased[g], NEG_INF) for g in range(N_GROUPS)]
    chosen = [jnp.zeros((GROUP_SIZE, tm), F32) for _ in range(N_GROUPS)]
    picked = []
    for _ in range(TOP_K):
        m = jnp.max(functools.reduce(jnp.maximum, vals), axis=0, keepdims=True)
        cand = [jnp.where(vals[g] == m, grp_index[g], float(N_EXPERTS)) for g in range(N_GROUPS)]
        ei = jnp.min(functools.reduce(jnp.minimum, cand), axis=0, keepdims=True)
        sel = [grp_index[g] == ei for g in range(N_GROUPS)]
        s_k = functools.reduce(jnp.add, [jnp.where(sel[g], grp_scores[g], 0.0) for g in range(N_GROUPS)])
        picked.append((ei, jnp.sum(s_k, axis=0, keepdims=True)))
        vals = [jnp.where(sel[g], -jnp.inf, vals[g]) for g in range(N_GROUPS)]
        chosen = [jnp.where(sel[g], 1.0, chosen[g]) for g in range(N_GROUPS)]
    total = functools.reduce(jnp.add, [s for _, s in picked])
    chosen_all = jnp.concatenate(chosen, axis=0)
    tri = tri_ref[...]
    width = tri.shape[0]
    cnt = cnt_ref[...]
    parts = []
    for j in range(tm // width):
        ch = chosen_all[:, j * width:(j + 1) * width]
        parts.append(_dot(ch.astype(BF16), tri) + cnt)
        cnt = cnt + jnp.sum(ch, axis=1, keepdims=True)
    cnt_ref[...] = cnt
    rank_all = jnp.concatenate(parts, axis=1)
    for k, (ei, s) in enumerate(picked):
        idx_ref[k:k + 1, :] = ei.astype(jnp.int32)
        w_ref[k:k + 1, :] = s / total * ROUTED_SCALE
        r = functools.reduce(jnp.add, [
            jnp.where(grp_index[g] == ei, rank_all[g * GROUP_SIZE:(g + 1) * GROUP_SIZE, :], 0.0)
            for g in range(N_GROUPS)])
        rank_ref[k:k + 1, :] = jnp.sum(r, axis=0, keepdims=True).astype(jnp.int32)


ROUTE_TN = 2048
TRI_N = 256


def _router_kernel(logits_ref, br_ref, tri_ref, idx_ref, w_ref, rank_ref, cnt_out_ref, cnt_ref):
    @pl.when(pl.program_id(0) == 0)
    def _():
        cnt_ref[...] = jnp.zeros_like(cnt_ref)

    _route(logits_ref[...], br_ref[...], cnt_ref, tri_ref, idx_ref, w_ref, rank_ref)
    cnt_out_ref[...] = jnp.broadcast_to(cnt_ref[...], cnt_out_ref.shape).astype(jnp.int32)


def _router(logits_t, br):
    n_tok = logits_t.shape[1]
    col = lambda i: (0, i)
    const = lambda i: (0, 0)
    tri = (jnp.arange(TRI_N)[:, None] < jnp.arange(TRI_N)[None, :]).astype(BF16)
    blk = pl.BlockSpec((TOP_K, ROUTE_TN), col)
    return pl.pallas_call(
        _router_kernel,
        grid=(n_tok // ROUTE_TN,),
        in_specs=[pl.BlockSpec((N_EXPERTS, ROUTE_TN), col), pl.BlockSpec(br.shape, const),
                  pl.BlockSpec(tri.shape, const)],
        out_specs=[blk, blk, blk, pl.BlockSpec((N_EXPERTS, LANES), const)],
        out_shape=[
            jax.ShapeDtypeStruct((TOP_K, n_tok), jnp.int32),
            jax.ShapeDtypeStruct((TOP_K, n_tok), F32),
            jax.ShapeDtypeStruct((TOP_K, n_tok), jnp.int32),
            jax.ShapeDtypeStruct((N_EXPERTS, LANES), jnp.int32),
        ],
        scratch_shapes=[pltpu.VMEM((N_EXPERTS, 1), F32)],
        compiler_params=_cparams("arbitrary"),
        name="router",
    )(logits_t, br, tri)


def _post_kernel(oal_ref, oac_ref, obl_ref, obc_ref, hl_ref, hc_ref, wout_ref, gtm_ref, g_ref, sh_ref, sc_ref,
                 gtf_ref, wrh_ref, wrl_ref, wsg_ref, wsu_ref, wsd_ref, hs_ref, f_ref, logits_ref):
    half = D_MODEL // 2
    for c in range(TM // TM_CHUNK):
        rows = slice(c * TM_CHUNK, (c + 1) * TM_CHUNK)
        proj = (_dot(_token_rows(oal_ref, oac_ref, rows), wout_ref[:half, :])
                + _dot(_token_rows(obl_ref, obc_ref, rows), wout_ref[half:, :]))
        h1 = _token_rows(hl_ref, hc_ref, rows) + gtm_ref[0] * proj
        f = _rms(h1) * g_ref[...]
        f = f * (1.0 + sc_ref[0]) + sh_ref[0]
        f_hi = f.astype(BF16)
        f_hi32 = f_hi.astype(F32)
        f_lo = (f - f_hi32).astype(BF16)
        f_ref[rows, :] = _pack_bf16_valued(f_hi32)
        logits_ref[:, rows] = (_dot_nt(wrh_ref[...], f_hi) + _dot_nt(wrh_ref[...], f_lo)
                               + _dot_nt(wrl_ref[...], f_hi))
        mid = _silu(_dot(f_hi, wsg_ref[...])) * _dot(f_hi, wsu_ref[...])
        shared = _dot(mid.astype(BF16), wsd_ref[...])
        hs_ref[rows, :] = h1 + gtf_ref[0] * shared


def _post(o_halves, h_lat, h_ctx, wout, gtm, g_ffn, sh, sc, gtf, wrh, wrl, wsg, wsu, wsd, n_rows):
    row = lambda i: (i, 0)
    const = lambda i: (0, 0)
    mod = lambda i: (_mod_index(i), 0, 0)
    vec = pl.BlockSpec((1, 1, D_MODEL), mod)
    full = lambda a: pl.BlockSpec(a.shape, const)
    o_specs, o_args = [], []
    for lat, ctx, col in o_halves:
        o_specs += _token_specs(D_MODEL // 2, col)
        o_args += [lat, ctx]
    return pl.pallas_call(
        _post_kernel,
        grid=(n_rows // TM,),
        in_specs=o_specs + _token_specs() + [
            full(wout), vec, full(g_ffn), vec, vec, vec,
            full(wrh), full(wrl), full(wsg), full(wsu), full(wsd),
        ],
        out_specs=[
            pl.BlockSpec((TM, D_MODEL), row),
            pl.BlockSpec((TM, D_MODEL // 2), row),
            pl.BlockSpec((N_EXPERTS, TM), lambda i: (0, i)),
        ],
        out_shape=[
            jax.ShapeDtypeStruct((n_rows, D_MODEL), F32),
            jax.ShapeDtypeStruct((n_rows, D_MODEL // 2), jnp.int32),
            jax.ShapeDtypeStruct((N_EXPERTS, n_rows), F32),
        ],
        compiler_params=_cparams("parallel"),
        name="post_attn",
    )(*o_args, h_lat, h_ctx, wout, gtm, g_ffn, sh, sc, gtf, wrh, wrl, wsg, wsu, wsd)


DEST_TN = 2048


def _dest_kernel(ps_ref, idx_ref, rank_ref, o_ref):
    idx = idx_ref[...]
    acc = rank_ref[...]
    for e in range(N_EXPERTS):
        acc = acc + jnp.where(idx == e, ps_ref[e], 0)
    o_ref[...] = acc


def _dest(pad_start, top_idx, rank):
    n_tok = top_idx.shape[1]
    blk = pl.BlockSpec((TOP_K, DEST_TN), lambda i: (0, i))
    return pl.pallas_call(
        _dest_kernel,
        grid=(n_tok // DEST_TN,),
        in_specs=[pl.BlockSpec(memory_space=pltpu.SMEM), blk, blk],
        out_specs=blk,
        out_shape=jax.ShapeDtypeStruct((TOP_K, n_tok), jnp.int32),
        compiler_params=_cparams("parallel"),
        name="dest",
    )(pad_start, top_idx, rank)


SC_CORES = 2
SC_SUBCORES = 16
SC_WORKERS = SC_CORES * SC_SUBCORES
SC_DISPATCH_ROWS = 32
SC_GATHER_ROWS = 64
ROW_WORDS = D_MODEL // 2


def _sc_mesh():
    return plsc.VectorSubcoreMesh(core_axis_name="core", subcore_axis_name="subcore")


def _sc_worker():
    return lax.axis_index("subcore") * SC_CORES + lax.axis_index("core")


def _sc_dispatch(f_packed, dest, n_slots):
    n_tok = f_packed.shape[0]
    per_worker = n_tok // SC_WORKERS
    n_chunks = per_worker // SC_DISPATCH_ROWS
    assert n_chunks * SC_DISPATCH_ROWS * SC_WORKERS == n_tok and n_chunks % 2 == 0
    dest4 = dest.reshape(TOP_K, SC_WORKERS, n_chunks, SC_DISPATCH_ROWS)

    def body(f_hbm, dest_hbm, xs_hbm, rows0, rows1, idx_v, load0, load1, scat0, scat1):
        wid = _sc_worker()
        for k in range(TOP_K):
            pltpu.sync_copy(dest_hbm.at[k, wid], idx_v.at[k])

        def load(c, buf, sem):
            start = pl.multiple_of(wid * per_worker + c * SC_DISPATCH_ROWS, 8)
            return pltpu.make_async_copy(f_hbm.at[pl.ds(start, SC_DISPATCH_ROWS)], buf, sem)

        def scatters(c, buf, sem):
            return [pltpu.make_async_copy(buf, xs_hbm.at[idx_v.at[k, c]], sem) for k in range(TOP_K)]

        load(0, rows0, load0).start()

        @pl.loop(0, n_chunks, step=2)
        def _(c):
            @pl.when(c > 0)
            def _():
                for cp in scatters(c - 1, rows1, scat1):
                    cp.wait()

            load(c + 1, rows1, load1).start()
            load(c, rows0, load0).wait()
            for cp in scatters(c, rows0, scat0):
                cp.start()
            load(c + 1, rows1, load1).wait()
            for cp in scatters(c + 1, rows1, scat1):
                cp.start()
            for cp in scatters(c, rows0, scat0):
                cp.wait()

            @pl.when(c + 2 < n_chunks)
            def _():
                load(c + 2, rows0, load0).start()

        for cp in scatters(n_chunks - 1, rows1, scat1):
            cp.wait()

    rows = pltpu.VMEM((SC_DISPATCH_ROWS, ROW_WORDS), jnp.int32)
    return pl.kernel(
        body,
        out_type=jax.ShapeDtypeStruct((n_slots, ROW_WORDS), jnp.int32),
        mesh=_sc_mesh(),
        scratch_types=[rows, rows, pltpu.VMEM((TOP_K, n_chunks, SC_DISPATCH_ROWS), jnp.int32)]
        + [pltpu.SemaphoreType.DMA] * 4,
        name="sc_dispatch",
    )(f_packed, dest4)


def _sc_gather(ys, dest):
    n_idx = dest.shape[0] * dest.shape[1]
    per_worker = n_idx // SC_WORKERS
    n_chunks = per_worker // SC_GATHER_ROWS
    assert n_chunks * SC_GATHER_ROWS * SC_WORKERS == n_idx and n_chunks % 2 == 0
    dest3 = dest.reshape(SC_WORKERS, n_chunks, SC_GATHER_ROWS)

    def body(ys_hbm, dest_hbm, out_hbm, rows0, rows1, idx_v, gat0, gat1, put0, put1):
        wid = _sc_worker()
        pltpu.sync_copy(dest_hbm.at[wid], idx_v)

        def gather(c, buf, sem):
            return pltpu.make_async_copy(ys_hbm.at[idx_v.at[c]], buf, sem)

        def put(c, buf, sem):
            start = pl.multiple_of(wid * per_worker + c * SC_GATHER_ROWS, 8)
            return pltpu.make_async_copy(buf, out_hbm.at[pl.ds(start, SC_GATHER_ROWS)], sem)

        gather(0, rows0, gat0).start()

        @pl.loop(0, n_chunks, step=2)
        def _(c):
            @pl.when(c > 0)
            def _():
                put(c - 1, rows1, put1).wait()

            gather(c + 1, rows1, gat1).start()
            gather(c, rows0, gat0).wait()
            put(c, rows0, put0).start()
            gather(c + 1, rows1, gat1).wait()
            put(c + 1, rows1, put1).start()
            put(c, rows0, put0).wait()

            @pl.when(c + 2 < n_chunks)
            def _():
                gather(c + 2, rows0, gat0).start()

        put(n_chunks - 1, rows1, put1).wait()

    rows = pltpu.VMEM((SC_GATHER_ROWS, ROW_WORDS), jnp.int32)
    return pl.kernel(
        body,
        out_type=jax.ShapeDtypeStruct((n_idx, ROW_WORDS), jnp.int32),
        mesh=_sc_mesh(),
        scratch_types=[rows, rows, pltpu.VMEM((n_chunks, SC_GATHER_ROWS), jnp.int32)]
        + [pltpu.SemaphoreType.DMA] * 4,
        name="sc_gather",
    )(ys, dest3)


SC_LANES = 16
SC_REDUCE_TOKENS = 4


def _sc_combine(ys, dest, top_w):
    n_tok = dest.shape[1]
    per_worker = n_tok // SC_WORKERS
    n_chunks = per_worker // SC_REDUCE_TOKENS
    assert n_chunks * SC_REDUCE_TOKENS * SC_WORKERS == n_tok and n_chunks % 2 == 0
    pairs = SC_REDUCE_TOKENS // 2
    rows_per_gather = 2 * TOP_K
    assert rows_per_gather == SC_LANES
    idx = dest.T.reshape(SC_WORKERS, n_chunks, pairs, rows_per_gather)
    w_lanes = jnp.repeat(top_w.T, SC_LANES, axis=1)
    groups = ROW_WORDS // SC_LANES

    def body(ys_hbm, idx_hbm, w_hbm, out_hbm, rows0, rows1, w0, w1, out0, out1, idx_v, in0, in1, put0, put1):
        wid = _sc_worker()
        pltpu.sync_copy(idx_hbm.at[wid], idx_v)

        def fetch(c, rows, wv, sem):
            start = pl.multiple_of(wid * per_worker + c * SC_REDUCE_TOKENS, SC_REDUCE_TOKENS)
            copies = [pltpu.make_async_copy(ys_hbm.at[idx_v.at[c, g]], rows.at[g], sem) for g in range(pairs)]
            return copies + [pltpu.make_async_copy(w_hbm.at[pl.ds(start, SC_REDUCE_TOKENS)], wv, sem)]

        def put(c, out, sem):
            start = pl.multiple_of(wid * per_worker + c * SC_REDUCE_TOKENS, SC_REDUCE_TOKENS)
            return pltpu.make_async_copy(out, out_hbm.at[pl.ds(start, SC_REDUCE_TOKENS)], sem)

        def reduce(rows, wv, out):
            for t in range(SC_REDUCE_TOKENS):
                g, base = t // 2, (t % 2) * TOP_K
                wts = [wv[t, pl.ds(k * SC_LANES, SC_LANES)] for k in range(TOP_K)]

                @plsc.parallel_loop(0, groups, unroll=2)
                def _(j):
                    off = pl.multiple_of(j * SC_LANES, SC_LANES)
                    acc_lo = jnp.zeros((SC_LANES,), F32)
                    acc_hi = jnp.zeros((SC_LANES,), F32)
                    for k in range(TOP_K):
                        word = rows[g, base + k, pl.ds(off, SC_LANES)]
                        lo = plsc.bitcast(word << BF16_BITS, F32)
                        hi = plsc.bitcast(word & jnp.int32(-(1 << BF16_BITS)), F32)
                        acc_lo = acc_lo + wts[k] * lo
                        acc_hi = acc_hi + wts[k] * hi
                    out[t, pl.ds(off, SC_LANES)] = acc_lo
                    out[t, pl.ds(ROW_WORDS + off, SC_LANES)] = acc_hi

        for cp in fetch(0, rows0, w0, in0):
            cp.start()

        @pl.loop(0, n_chunks, step=2)
        def _(c):
            for cp in fetch(c + 1, rows1, w1, in1):
                cp.start()
            for cp in fetch(c, rows0, w0, in0):
                cp.wait()

            @pl.when(c > 0)
            def _():
                put(c - 2, out0, put0).wait()

            reduce(rows0, w0, out0)
            put(c, out0, put0).start()

            @pl.when(c + 2 < n_chunks)
            def _():
                for cp in fetch(c + 2, rows0, w0, in0):
                    cp.start()

            for cp in fetch(c + 1, rows1, w1, in1):
                cp.wait()

            @pl.when(c > 0)
            def _():
                put(c - 1, out1, put1).wait()

            reduce(rows1, w1, out1)
            put(c + 1, out1, put1).start()

        put(n_chunks - 2, out0, put0).wait()
        put(n_chunks - 1, out1, put1).wait()

    rows = pltpu.VMEM((pairs, rows_per_gather, ROW_WORDS), jnp.int32)
    wbuf = pltpu.VMEM((SC_REDUCE_TOKENS, TOP_K * SC_LANES), F32)
    obuf = pltpu.VMEM((SC_REDUCE_TOKENS, D_MODEL), F32)
    return pl.kernel(
        body,
        out_type=jax.ShapeDtypeStruct((n_tok, D_MODEL), F32),
        mesh=_sc_mesh(),
        scratch_types=[rows, rows, wbuf, wbuf, obuf, obuf,
                       pltpu.VMEM((n_chunks, pairs, rows_per_gather), jnp.int32)]
        + [pltpu.SemaphoreType.DMA] * 4,
        compiler_params=pltpu.CompilerParams(needs_layout_passes=False),
        name="sc_combine",
    )(ys, idx, w_lanes)


X_SLOTS = 4
O_SLOTS = 3


def _moe_kernel(be_ref, nused_ref, slot_ref, next_ref, xs_hbm, wg_hbm, wu_hbm, wd_hbm, ys_hbm,
                xbuf, obuf, wg_s, wu_s, wd_s, wg_f, wu_f, wd_f, xsem, osem, wsem, *, layer):
    n_used = nused_ref[0]

    def x_copy(i, s):
        start = pl.multiple_of(i * MOE_ROWS, MOE_ROWS)
        return pltpu.make_async_copy(xs_hbm.at[pl.ds(start, MOE_ROWS)], xbuf.at[s], xsem.at[s])

    def o_copy(i, s):
        start = pl.multiple_of(i * MOE_ROWS, MOE_ROWS)
        return pltpu.make_async_copy(obuf.at[s], ys_hbm.at[pl.ds(start, MOE_ROWS)], osem.at[s])

    def weight_copies(expert, slot):
        return [pltpu.make_async_copy(src.at[layer, expert], dst.at[slot], wsem.at[slot, j])
                for j, (src, dst) in enumerate(((wg_hbm, wg_f), (wu_hbm, wu_f), (wd_hbm, wd_f)))]

    first_expert = be_ref[0]
    for cp in weight_copies(first_expert, slot_ref[first_expert]):
        cp.start()
    for j in range(X_SLOTS - 1):
        @pl.when(j < n_used)
        def _(j=j):
            x_copy(j, j).start()

    def block(i, carry):
        e = be_ref[i]
        xs = lax.rem(i, X_SLOTS)
        os = lax.rem(i, O_SLOTS)

        @pl.when(jnp.logical_or(i == 0, e != be_ref[jnp.maximum(i - 1, 0)]))
        def _():
            slot = slot_ref[e]
            for cp in weight_copies(e, slot):
                cp.wait()
            wg_s[...] = wg_f[slot].astype(BF16)
            wu_s[...] = wu_f[slot].astype(BF16)
            wd_s[...] = wd_f[slot].astype(BF16)
            nxt = next_ref[e]

            @pl.when(nxt >= 0)
            def _():
                for cp in weight_copies(nxt, 1 - slot):
                    cp.start()

        x_copy(i, xs).wait()

        @pl.when(i + X_SLOTS - 1 < n_used)
        def _():
            x_copy(i + X_SLOTS - 1, lax.rem(i + X_SLOTS - 1, X_SLOTS)).start()

        @pl.when(i >= O_SLOTS)
        def _():
            o_copy(i - O_SLOTS, os).wait()

        half = D_MODEL // 2
        for c in range(MOE_ROWS // MOE_CHUNK):
            rows = pl.ds(c * MOE_CHUNK, MOE_CHUNK)
            lo, hi = _unpack_bf16_pairs(xbuf[xs, rows, :])
            lo, hi = lo.astype(BF16), hi.astype(BF16)
            gate = _dot(lo, wg_s[:half, :]) + _dot(hi, wg_s[half:, :])
            up = _dot(lo, wu_s[:half, :]) + _dot(hi, wu_s[half:, :])
            mid = _silu(gate) * up
            obuf[os, rows, :] = _pack_bf16_pairs(_dot(mid.astype(BF16), wd_s[...]))
        o_copy(i, os).start()
        return carry

    lax.fori_loop(0, n_used, block, 0)
    for j in range(O_SLOTS):
        @pl.when(n_used > j)
        def _(j=j):
            i = n_used - 1 - j
            o_copy(i, lax.rem(i, O_SLOTS)).wait()


def _moe(block_expert, n_used, slot, next_expert, xs, wg, wu, wd, layer):
    hbm = pl.BlockSpec(memory_space=pl.ANY)
    block = (MOE_ROWS, ROW_WORDS)
    grid_spec = pltpu.PrefetchScalarGridSpec(
        num_scalar_prefetch=4,
        grid=(1,),
        in_specs=[hbm, hbm, hbm, hbm],
        out_specs=hbm,
        scratch_shapes=[
            pltpu.VMEM((X_SLOTS,) + block, jnp.int32),
            pltpu.VMEM((O_SLOTS,) + block, jnp.int32),
            pltpu.VMEM((D_MODEL, EXPERT_FF), BF16),
            pltpu.VMEM((D_MODEL, EXPERT_FF), BF16),
            pltpu.VMEM((EXPERT_FF, D_MODEL), BF16),
            pltpu.VMEM((2, D_MODEL, EXPERT_FF), F32),
            pltpu.VMEM((2, D_MODEL, EXPERT_FF), F32),
            pltpu.VMEM((2, EXPERT_FF, D_MODEL), F32),
            pltpu.SemaphoreType.DMA((X_SLOTS,)),
            pltpu.SemaphoreType.DMA((O_SLOTS,)),
            pltpu.SemaphoreType.DMA((2, 3)),
        ],
    )
    return pl.pallas_call(
        functools.partial(_moe_kernel, layer=layer),
        grid_spec=grid_spec,
        out_shape=jax.ShapeDtypeStruct((xs.shape[0], ROW_WORDS), jnp.int32),
        compiler_params=_cparams("arbitrary"),
        name="moe_experts",
    )(block_expert, n_used, slot, next_expert, xs, wg, wu, wd)


def _combine_kernel(y_ref, hs_ref, gtf_ref, gfin_ref, o_ref, *, final):
    out = hs_ref[...] + gtf_ref[0] * y_ref[...]
    if final:
        out = _rms(out) * gfin_ref[...]
    o_ref[...] = out


def _combine(y, hs, gtf, g_final, *, final):
    n_rows = hs.shape[0]
    return pl.pallas_call(
        functools.partial(_combine_kernel, final=final),
        grid=(n_rows // TM_COMB,),
        in_specs=[
            pl.BlockSpec((TM_COMB, D_MODEL), lambda i: (i, 0)),
            pl.BlockSpec((TM_COMB, D_MODEL), lambda i: (i, 0)),
            pl.BlockSpec((1, 1, D_MODEL), lambda i: (jnp.minimum(i // (SEQ // TM_COMB), BATCH), 0, 0)),
            pl.BlockSpec((1, D_MODEL), lambda i: (0, 0)),
        ],
        out_specs=pl.BlockSpec((TM_COMB, D_MODEL), lambda i: (i, 0)),
        out_shape=jax.ShapeDtypeStruct((n_rows, D_MODEL), F32),
        compiler_params=_cparams("parallel"),
        name="combine_final" if final else "combine",
    )(y, hs, gtf, g_final)


def _rope_tables():
    rows = SEQ // GRID_W

    def angles(rot_dim):
        half = rot_dim // 2
        inv_freq = ROPE_BASE ** (-jnp.arange(0, half, 2, dtype=F32) / half)
        row = jnp.repeat(jnp.arange(rows, dtype=F32), GRID_W)
        col = jnp.tile(jnp.arange(GRID_W, dtype=F32), rows)
        ang_r = row[:, None] * inv_freq
        ang_c = col[:, None] * inv_freq
        return jnp.concatenate([ang_r, ang_r, ang_c, ang_c], axis=-1)

    def signed(sin, quarter):
        sign = jnp.where((jnp.arange(sin.shape[-1]) // quarter) % 2 == 0, -1.0, 1.0)
        return sin * sign

    def with_identity(cos, sin):
        cos = jnp.concatenate([cos, jnp.ones((TM, LANES), F32)], axis=0)
        sin = jnp.concatenate([sin, jnp.zeros((TM, LANES), F32)], axis=0)
        return cos, sin

    a64 = angles(SWA_HEAD_DIM)
    c64 = jnp.tile(jnp.cos(a64), (1, LANES // SWA_HEAD_DIM))
    s64 = jnp.tile(signed(jnp.sin(a64), SWA_HEAD_DIM // 4), (1, LANES // SWA_HEAD_DIM))
    a32 = angles(MLA_ROPE)
    pad_lo = MLA_NOPE
    pad_hi = LANES - MLA_NOPE - MLA_ROPE
    cm = jnp.concatenate([jnp.ones((SEQ, pad_lo), F32), jnp.cos(a32), jnp.ones((SEQ, pad_hi), F32)], axis=-1)
    sm = jnp.concatenate([jnp.zeros((SEQ, pad_lo), F32), signed(jnp.sin(a32), MLA_ROPE // 4),
                          jnp.zeros((SEQ, pad_hi), F32)], axis=-1)
    return with_identity(cm, sm) + with_identity(c64, s64)


def _layer0_weights(wa_in, wa_uq, wa_ukv, wa_out):
    d = D_MODEL
    cq, ckv, kr, qs, ks, vs = jnp.split(
        wa_in, [C_CKV, C_KR, C_KR + MLA_ROPE, C_KR + MLA_ROPE + SWA_HEADS * SWA_HEAD_DIM,
                C_KR + MLA_ROPE + (SWA_HEADS + SWA_KV_HEADS) * SWA_HEAD_DIM], axis=-1)
    kr_pad = jnp.concatenate([jnp.zeros((d, MLA_NOPE), F32), kr,
                              jnp.zeros((d, LANES - MLA_NOPE - MLA_ROPE), F32)], axis=-1)
    qs_pair = qs.reshape(d, SWA_KV_HEADS, SWA_GROUP, SWA_HEAD_DIM).transpose(0, 2, 1, 3).reshape(d, -1)
    win = jnp.concatenate([cq, ckv, kr_pad, qs_pair, ks, vs], axis=-1).astype(BF16)
    uq = wa_uq.reshape(MLA_Q_RANK, MLA_HEADS, MLA_NOPE + MLA_ROPE)
    uq = jnp.pad(uq, ((0, 0), (0, 0), (0, LANES - MLA_NOPE - MLA_ROPE))).reshape(MLA_Q_RANK, -1)
    ukv = wa_ukv.reshape(MLA_KV_RANK, MLA_HEADS, MLA_NOPE + MLA_V)
    uk = jnp.pad(ukv[:, :, :MLA_NOPE], ((0, 0), (0, 0), (0, LANES - MLA_NOPE))).reshape(MLA_KV_RANK, -1)
    uv = ukv[:, :, MLA_NOPE:].reshape(MLA_KV_RANK, -1)
    wukv = jnp.concatenate([uk, uv], axis=-1)
    n_mla = MLA_HEADS * MLA_V
    out_swa = wa_out[n_mla:].reshape(SWA_KV_HEADS, SWA_GROUP, SWA_HEAD_DIM, d).transpose(1, 0, 2, 3)
    wout = jnp.concatenate([wa_out[:n_mla], out_swa.reshape(-1, d)], axis=0)
    return win, uq.astype(BF16), wukv.astype(BF16), wout.astype(BF16)


def _layer1_weights(wc_in):
    per_head = 4 * DIFF_HALF + DIFF_V
    w = wc_in.reshape(D_MODEL, DIFF_HEADS, per_head)
    q = w[:, :, :2 * DIFF_HALF].reshape(D_MODEL, -1)
    k = w[:, :, 2 * DIFF_HALF:4 * DIFF_HALF].reshape(D_MODEL, -1)
    v = w[:, :, 4 * DIFF_HALF:].reshape(D_MODEL, -1)
    return jnp.concatenate([q, k, v], axis=-1).astype(BF16)


def _block_tables(counts, n_blocks):
    padded = (counts + MOE_ROWS - 1) // MOE_ROWS * MOE_ROWS
    pad_end = jnp.cumsum(padded)
    pad_start = (pad_end - padded).astype(jnp.int32)
    block_start = jnp.arange(n_blocks, dtype=jnp.int32) * MOE_ROWS
    block_expert = jnp.minimum((pad_end[None, :] <= block_start[:, None]).sum(axis=1),
                               N_EXPERTS - 1).astype(jnp.int32)
    n_used = (pad_end[-1] // MOE_ROWS).astype(jnp.int32).reshape(1)
    has = padded > 0
    slot = ((jnp.cumsum(has) - 1) & 1).astype(jnp.int32)
    ids = jnp.where(has, jnp.arange(N_EXPERTS, dtype=jnp.int32), N_EXPERTS)
    after = jnp.concatenate([lax.cummin(ids, reverse=True)[1:], jnp.full((1,), N_EXPERTS, jnp.int32)])
    next_expert = jnp.where(after < N_EXPERTS, after, -1).astype(jnp.int32)
    return pad_start, block_expert, n_used, slot, next_expert


def _moe_layer(f_packed, top_idx, top_w, rank, counts, hs, gtf, g_final, wg, wu, wd, layer, *, final):
    n_tok = top_idx.shape[1]
    n_blocks = -(-(n_tok * TOP_K) // MOE_ROWS) + N_EXPERTS
    pad_start, block_expert, n_used, slot, next_expert = _block_tables(counts[:, 0], n_blocks)
    dest = _dest(pad_start, top_idx, rank)
    xs = _sc_dispatch(f_packed, dest, n_blocks * MOE_ROWS)
    ys = _moe(block_expert, n_used, slot, next_expert, xs, wg, wu, wd, layer)
    y = _sc_combine(ys, dest, top_w)
    return _combine(y, hs, gtf, g_final, final=final)


def kernel(x, c, ctx, c_ctx, w_ada, b_ada, g_mix, g_ffn, wa_in, ga_q, ga_kv, wa_uq, wa_ukv, wa_sink, wa_out,
           wc_in, lam_q1, lam_k1, lam_q2, lam_k2, gc_sub, wc_out, w_router, b_router, we_gate, we_up, we_down,
           ws_gate, ws_up, ws_down, g_final):
    d = D_MODEL
    x2, ctx2 = x.reshape(N_LAT, d), ctx.reshape(N_CTX, d)
    cc =jnp.concatenate([c, c_ctx[None, :], jnp.zeros((MOD_ROWS - BATCH - 1, d), F32)], axis=0)
    mod = _ada(cc, w_ada, b_ada).reshape(DEPTH, MOD_ROWS, 6, 1, d)

    def mod_vec(layer, j):
        return mod[layer, :, j]

    cm, sm, c64, s64 = _rope_tables()
    g_final2 = g_final.reshape(1, d)

    def router_weights(layer):
        wt = w_router[layer].T
        hi = wt.astype(BF16)
        lo = (wt - hi.astype(F32)).astype(BF16)
        return hi, lo

    def shared_weights(layer):
        return ws_gate[layer].astype(BF16), ws_up[layer].astype(BF16), ws_down[layer].astype(BF16)

    win, wuq, wukv, wout0 = _layer0_weights(wa_in[0], wa_uq[0], wa_ukv[0], wa_out[0])
    qm, km, vm, qs, ks, vs = _proj0(x2, ctx2, g_mix[0].reshape(1, d), mod_vec(0, 0), mod_vec(0, 1), win,
                                    ga_q[0].reshape(1, -1), ga_kv[0].reshape(1, -1), wuq, wukv, cm, sm, c64, s64)
    sink = wa_sink[0].astype(F32)
    o_halves = [(_mla_attention(qm, km, vm, ctx_queries=False), _mla_attention(qm, km, vm, ctx_queries=True), 0),
                (_swa_attention(sink, qs, ks, vs, ctx_queries=False),
                 _swa_attention(sink, qs, ks, vs, ctx_queries=True), 0)]
    hs, f, logits_t = _post(
        o_halves, x2, ctx2, wout0, mod_vec(0, 2), g_ffn[0].reshape(1, d), mod_vec(0, 3), mod_vec(0, 4),
        mod_vec(0, 5),
        *router_weights(0), *shared_weights(0), N_TOK)
    top_idx, top_w, rank, counts = _router(logits_t, b_router[0].reshape(N_EXPERTS, 1))
    h = _moe_layer(f, top_idx, top_w, rank, counts, hs, mod_vec(0, 5), g_final2,
                   we_gate, we_up, we_down, 0, final=False)

    layer = 1
    lam_init = 0.8 - 0.6 * math.exp(-0.3 * layer)
    q1, k1, v1 = _proj1(h, g_mix[1].reshape(1, d), mod_vec(1, 0), mod_vec(1, 1), _layer1_weights(wc_in[0]),
                        c64, s64)
    lam_vecs = jnp.zeros((8, LANES), F32).at[:4, :DIFF_HALF].set(
        jnp.stack([lam_q1[0], lam_k1[0], lam_q2[0], lam_k2[0]]).astype(F32))
    o = _diff_attention(q1, k1, v1, lam_vecs, gc_sub[0].reshape(1, DIFF_V), lam_init)
    hs, f, logits_t = _post(
        [(o, o, 0), (o, o, 1)], h, h, wc_out[0].astype(BF16), mod_vec(1, 2), g_ffn[1].reshape(1, d),
        mod_vec(1, 3), mod_vec(1, 4),
        mod_vec(1, 5), *router_weights(1), *shared_weights(1), N_LAT)
    top_idx, top_w, rank, counts = _router(logits_t, b_router[1].reshape(N_EXPERTS, 1))
    out = _moe_layer(f, top_idx, top_w, rank, counts, hs, mod_vec(1, 5), g_final2,
                     we_gate, we_up, we_down, 1, final=True)
    return out.reshape(BATCH, SEQ, d)
```

```python
import functools
import math

import jax
import jax.numpy as jnp
from jax import lax
from jax.experimental import pallas as pl
from jax.experimental.pallas import tpu as pltpu
from jax.experimental.pallas import tpu_sc as plsc

F32 = jnp.float32
BF16 = jnp.bfloat16

D_MODEL = 1024
BATCH = 8
SEQ = 2048
DEPTH = 2
CTX_LEN = 256
GRID_W = 64
ROPE_BASE = 10000.0
EPS = 1e-6
NEG_INF = -1e30

MLA_HEADS = 8
MLA_Q_RANK = 384
MLA_KV_RANK = 256
MLA_NOPE = 64
MLA_ROPE = 32
MLA_V = 64
SWA_HEADS = 8
SWA_KV_HEADS = 2
SWA_HEAD_DIM = 64
SWA_GROUP = SWA_HEADS // SWA_KV_HEADS
WINDOW = 128
DIFF_HEADS = 8
DIFF_HALF = 64
DIFF_V = 128
N_EXPERTS = 64
N_GROUPS = 8
GROUP_SIZE = N_EXPERTS // N_GROUPS
TOPK_GROUPS = 4
TOP_K = 8
EXPERT_FF = 256
SHARED_FF = 256
ROUTED_SCALE = 2.5

LOG2E = math.log2(math.e)
LANES = 128
N_LAT = BATCH * SEQ
N_CTX = BATCH * CTX_LEN
N_TOK = N_LAT + N_CTX
MOD_ROWS = 16

TM = 1024
TM_CHUNK = 256
LAT_BLOCKS_PER_BATCH = SEQ // TM
TQ_MLA = 512
MLA_PAIRS_PER_STEP = 4
TQ_DIFF = 512
DIFF_HEADS_PER_STEP = 8
SWA_BLOCK = 128
SWA_QBLOCKS = 4
assert SWA_BLOCK == WINDOW
MOE_ROWS = 512
MOE_CHUNK = 256
TM_COMB = 512
VMEM_LIMIT = 56 * 1024 * 1024

C_CQ = 0
C_CKV = C_CQ + MLA_Q_RANK
C_KR = C_CKV + MLA_KV_RANK
C_QS = C_KR + LANES
C_KS = C_QS + SWA_HEADS * SWA_HEAD_DIM
C_VS = C_KS + SWA_KV_HEADS * SWA_HEAD_DIM
C_END = C_VS + SWA_KV_HEADS * SWA_HEAD_DIM


def _cparams(*sem):
    return pltpu.CompilerParams(dimension_semantics=sem, vmem_limit_bytes=VMEM_LIMIT)


def _dot(a, b):
    return jnp.dot(a, b, preferred_element_type=F32)


def _dot_nt(a, b):
    return lax.dot_general(a, b, (((1,), (1,)), ((), ())), preferred_element_type=F32)


def _rms(x):
    return x * lax.rsqrt(jnp.mean(x * x, axis=-1, keepdims=True) + EPS)


def _silu(x):
    return x * jax.nn.sigmoid(x)


def _rope(x, cos, sin_signed, shift):
    n = x.shape[-1]
    lane = lax.broadcasted_iota(jnp.int32, x.shape, 1)
    first = (lane & shift) == 0
    rot = jnp.where(first, pltpu.roll(x, n - shift, 1), pltpu.roll(x, shift, 1))
    return x * cos + rot * sin_signed


def _mod_index(i):
    return jnp.minimum(i // LAT_BLOCKS_PER_BATCH, BATCH)


def _rope_index(i):
    return jnp.where(i < N_LAT // TM, i % LAT_BLOCKS_PER_BATCH, LAT_BLOCKS_PER_BATCH)


ADA_TN = 1536


def _ada_kernel(c_ref, w_ref, b_ref, o_ref):
    s = _silu(c_ref[...]).astype(BF16)
    o_ref[0] = _dot(s, w_ref[0].astype(BF16)) + b_ref[0]


def _ada(cc, w_ada, b_ada):
    n_out = w_ada.shape[-1]
    return pl.pallas_call(
        _ada_kernel,
        grid=(DEPTH, n_out // ADA_TN),
        in_specs=[
            pl.BlockSpec((MOD_ROWS, D_MODEL), lambda l, j: (0, 0)),
            pl.BlockSpec((1, D_MODEL, ADA_TN), lambda l, j: (l, 0, j)),
            pl.BlockSpec((1, 1, ADA_TN), lambda l, j: (l, 0, j)),
        ],
        out_specs=pl.BlockSpec((1, MOD_ROWS, ADA_TN), lambda l, j: (l, 0, j)),
        out_shape=jax.ShapeDtypeStruct((DEPTH, MOD_ROWS, n_out), F32),
        compiler_params=_cparams("parallel", "parallel"),
        name="ada",
    )(cc, w_ada, b_ada.reshape(DEPTH, 1, n_out))


def _token_rows(lat_ref, ctx_ref, rows):
    return jnp.where(pl.program_id(0) < N_LAT // TM, lat_ref[rows, :], ctx_ref[rows, :])


def _token_specs(width=D_MODEL, col=0):
    lat_blocks = N_LAT // TM
    return [pl.BlockSpec((TM, width), lambda i: (jnp.minimum(i, lat_blocks - 1), col)),
            pl.BlockSpec((TM, width), lambda i: (jnp.maximum(i - lat_blocks, 0), col))]


def _proj0_kernel(x_ref, ctx_ref, g_ref, sh_ref, sc_ref, win_ref, gq_ref, gkv_ref, wuq_ref, wukv_ref,
                  cm_ref, sm_ref, c64_ref, s64_ref,
                  qm_ref, km_ref, vm_ref, qs_ref, ks_ref, vs_ref):
    q_scale = (MLA_NOPE + MLA_ROPE) ** -0.5 * LOG2E
    s_scale = SWA_HEAD_DIM ** -0.5 * LOG2E
    for c in range(TM // TM_CHUNK):
        rows = slice(c * TM_CHUNK, (c + 1) * TM_CHUNK)
        a = _rms(_token_rows(x_ref, ctx_ref, rows)) * g_ref[...]
        a = a * (1.0 + sc_ref[0]) + sh_ref[0]
        p = _dot(a.astype(BF16), win_ref[...])
        nq = _rms(p[:, C_CQ:C_CKV]) * gq_ref[...]
        nkv = _rms(p[:, C_CKV:C_KR]) * gkv_ref[...]
        q = _dot(nq.astype(BF16), wuq_ref[...])
        kv = _dot(nkv.astype(BF16), wukv_ref[...])
        cm, sm = cm_ref[rows, :], sm_ref[rows, :]
        c64, s64 = c64_ref[rows, :], s64_ref[rows, :]
        kr = _rope(p[:, C_KR:C_QS], cm, sm, MLA_ROPE // 4)
        for h in range(MLA_HEADS):
            sl = slice(h * LANES, (h + 1) * LANES)
            qm_ref[rows, sl] = (_rope(q[:, sl], cm, sm, MLA_ROPE // 4) * q_scale).astype(BF16)
            km_ref[rows, sl] = (kv[:, sl] + kr).astype(BF16)
        vm_ref[rows, :] = kv[:, MLA_HEADS * LANES:].astype(BF16)
        for g in range(SWA_GROUP):
            sl = slice(g * LANES, (g + 1) * LANES)
            qs_ref[rows, sl] = (_rope(p[:, C_QS + g * LANES:C_QS + (g + 1) * LANES], c64, s64,
                                      SWA_HEAD_DIM // 4) * s_scale).astype(BF16)
        ks_ref[rows, :] = _rope(p[:, C_KS:C_VS], c64, s64, SWA_HEAD_DIM // 4).astype(BF16)
        vs_ref[rows, :] = p[:, C_VS:C_END].astype(BF16)


def _proj0(x, ctx, g_mix, sh, sc, win, gq, gkv, wuq, wukv, cm, sm, c64, s64):
    row = lambda i: (i, 0)
    const = lambda i: (0, 0)
    mod = lambda i: (_mod_index(i), 0, 0)
    rope = lambda i: (_rope_index(i), 0)
    widths = (MLA_HEADS * LANES, MLA_HEADS * LANES, MLA_HEADS * MLA_V,
              SWA_HEADS * SWA_HEAD_DIM, SWA_KV_HEADS * SWA_HEAD_DIM, SWA_KV_HEADS * SWA_HEAD_DIM)
    return pl.pallas_call(
        _proj0_kernel,
        grid=(N_TOK // TM,),
        in_specs=_token_specs() + [
            pl.BlockSpec((1, D_MODEL), const),
            pl.BlockSpec((1, 1, D_MODEL), mod),
            pl.BlockSpec((1, 1, D_MODEL), mod),
            pl.BlockSpec(win.shape, const),
            pl.BlockSpec(gq.shape, const),
            pl.BlockSpec(gkv.shape, const),
            pl.BlockSpec(wuq.shape, const),
            pl.BlockSpec(wukv.shape, const),
            pl.BlockSpec((TM, LANES), rope),
            pl.BlockSpec((TM, LANES), rope),
            pl.BlockSpec((TM, LANES), rope),
            pl.BlockSpec((TM, LANES), rope),
        ],
        out_specs=[pl.BlockSpec((TM, w), row) for w in widths],
        out_shape=[jax.ShapeDtypeStruct((N_TOK, w), BF16) for w in widths],
        compiler_params=_cparams("parallel"),
        name="proj0",
    )(x, ctx, g_mix, sh, sc, win, gq, gkv, wuq, wukv, cm, sm, c64, s64)


def _proj1_kernel(h_ref, g_ref, sh_ref, sc_ref, w_ref, c64_ref, s64_ref, q_ref, k_ref, v_ref):
    width = DIFF_HEADS * LANES
    scale = DIFF_HALF ** -0.5 * LOG2E
    for c in range(TM // TM_CHUNK):
        rows = slice(c * TM_CHUNK, (c + 1) * TM_CHUNK)
        a = _rms(h_ref[rows, :]) * g_ref[...]
        a = a * (1.0 + sc_ref[0]) + sh_ref[0]
        p = _dot(a.astype(BF16), w_ref[...])
        c64, s64 = c64_ref[rows, :], s64_ref[rows, :]
        for h in range(DIFF_HEADS):
            sl = slice(h * LANES, (h + 1) * LANES)
            q_ref[rows, sl] = (_rope(p[:, sl], c64, s64, DIFF_HALF // 4) * scale).astype(BF16)
            k_ref[rows, sl] = _rope(p[:, width + h * LANES:width + (h + 1) * LANES], c64, s64,
                                    DIFF_HALF // 4).astype(BF16)
        v_ref[rows, :] = p[:, 2 * width:].astype(BF16)


def _proj1(h, g_mix, sh, sc, w, c64, s64):
    row = lambda i: (i, 0)
    const = lambda i: (0, 0)
    mod = lambda i: (_mod_index(i), 0, 0)
    rope = lambda i: (_rope_index(i), 0)
    width = DIFF_HEADS * LANES
    return pl.pallas_call(
        _proj1_kernel,
        grid=(N_TOK // TM,),
        in_specs=[
            pl.BlockSpec((TM, D_MODEL), row),
            pl.BlockSpec((1, D_MODEL), const),
            pl.BlockSpec((1, 1, D_MODEL), mod),
            pl.BlockSpec((1, 1, D_MODEL), mod),
            pl.BlockSpec(w.shape, const),
            pl.BlockSpec((TM, LANES), rope),
            pl.BlockSpec((TM, LANES), rope),
        ],
        out_specs=[pl.BlockSpec((TM, width), row)] * 3,
        out_shape=[jax.ShapeDtypeStruct((N_TOK, width), BF16)] * 3,
        compiler_params=_cparams("parallel"),
        name="proj1",
    )(h, g_mix, sh, sc, w, c64, s64)


def _mla_kernel(*refs, n_kv):
    q_ref = refs[0]
    k_refs = refs[1:1 + n_kv]
    v_refs = refs[1 + n_kv:1 + 2 * n_kv]
    o_ref = refs[1 + 2 * n_kv]
    lane = lax.broadcasted_iota(jnp.int32, (q_ref.shape[0], LANES), 1)
    own = [lane < MLA_V, lane >= MLA_V]
    for pair in range(MLA_PAIRS_PER_STEP):
        vsl = slice(pair * LANES, (pair + 1) * LANES)
        outs = []
        for hh in range(2):
            sl = slice((2 * pair + hh) * LANES, (2 * pair + hh + 1) * LANES)
            qh = q_ref[:, sl]
            ss = [_dot_nt(qh, k[:, sl]) for k in k_refs]
            m = functools.reduce(jnp.maximum, [jnp.max(s, axis=-1, keepdims=True) for s in ss])
            o = None
            for s, v in zip(ss, v_refs):
                vp = v[:, vsl]
                vlane = lax.broadcasted_iota(jnp.int32, vp.shape, 1)
                keep = (vlane < MLA_V) if hh == 0 else (vlane >= MLA_V)
                vh = jnp.where(keep, vp, jnp.ones_like(vp))
                part = _dot(jnp.exp2(s - m).astype(BF16), vh)
                o = part if o is None else o + part
            outs.append(o / pltpu.roll(o, MLA_V, 1))
        o_ref[:, vsl] = jnp.where(own[0], outs[0], outs[1]).astype(BF16)


def _mla_attention(qm, km, vm, *, ctx_queries):
    groups = MLA_HEADS // 2 // MLA_PAIRS_PER_STEP
    qk_w = 2 * LANES * MLA_PAIRS_PER_STEP
    v_w = LANES * MLA_PAIRS_PER_STEP
    if ctx_queries:
        tq, n_q, rows = CTX_LEN, 1, N_CTX
        q_map = lambda b, h, i: (N_LAT // CTX_LEN + b, h)
        o_map = lambda b, h, i: (b, h)
        kv_specs = [(CTX_LEN, lambda b, h, i: (N_LAT // CTX_LEN + b, h))]
    else:
        tq, n_q, rows = TQ_MLA, SEQ // TQ_MLA, N_LAT
        q_map = o_map = lambda b, h, i: (b * n_q + i, h)
        kv_specs = [(CTX_LEN, lambda b, h, i: (N_LAT // CTX_LEN + b, h)), (SEQ, lambda b, h, i: (b, h))]
    n_kv = len(kv_specs)
    in_specs = [pl.BlockSpec((tq, qk_w), q_map)]
    in_specs += [pl.BlockSpec((n, qk_w), m) for n, m in kv_specs]
    in_specs += [pl.BlockSpec((n, v_w), m) for n, m in kv_specs]
    return pl.pallas_call(
        functools.partial(_mla_kernel, n_kv=n_kv),
        grid=(BATCH, groups, n_q),
        in_specs=in_specs,
        out_specs=pl.BlockSpec((tq, v_w), o_map),
        out_shape=jax.ShapeDtypeStruct((rows, MLA_HEADS * MLA_V), BF16),
        compiler_params=_cparams("parallel", "parallel", "parallel"),
        name="mla_ctx" if ctx_queries else "mla_lat",
    )(qm, *([km] * n_kv), *([vm] * n_kv))


def _swa_kernel(sink_ref, *refs, band, n_blocks):
    q_ref = refs[0]
    if band:
        n_kv = SWA_QBLOCKS + 2
        k_blocks = refs[1:1 + n_kv]
        kx = refs[1 + n_kv]
        v_blocks = refs[2 + n_kv:2 + 2 * n_kv]
        vx, o_ref = refs[2 + 2 * n_kv], refs[3 + 2 * n_kv]
        chains, rows = SWA_QBLOCKS, SWA_BLOCK
    else:
        kx, vx, o_ref = refs[1:]
        chains, rows = 1, q_ref.shape[0]
    stacked = SWA_GROUP * rows
    k_ctx = kx[...]
    v_ctx = vx[...]
    lane = lax.broadcasted_iota(jnp.int32, (rows, LANES), 1)
    low = lane < SWA_HEAD_DIM
    row_group = lax.broadcasted_iota(jnp.int32, (stacked, 1), 0) // rows
    for c in range(chains):
        qrows = slice(c * rows, (c + 1) * rows)
        if band:
            n = pl.program_id(1) * SWA_QBLOCKS + c
            k_band = jnp.concatenate([k[...] for k in k_blocks[c:c + 3]], axis=0)
            v_band = jnp.concatenate([v[...] for v in v_blocks[c:c + 3]], axis=0)
            qq = lax.broadcasted_iota(jnp.int32, (stacked, SWA_BLOCK), 0) & (SWA_BLOCK - 1)
            kk = lax.broadcasted_iota(jnp.int32, (stacked, SWA_BLOCK), 1)
            valid_prev = kk >= qq + jnp.where(n > 0, 0, SWA_BLOCK)
            valid_next = kk <= qq - jnp.where(n < n_blocks - 1, 0, SWA_BLOCK)
        halves = []
        for hk in range(SWA_KV_HEADS):
            keep = low if hk == 0 else jnp.logical_not(low)
            qh = jnp.concatenate(
                [jnp.where(keep, q_ref[qrows, g * LANES:(g + 1) * LANES], jnp.zeros((rows, LANES), BF16))
                 for g in range(SWA_GROUP)], axis=0)
            sink = jnp.zeros((stacked, 1), F32)
            for g in range(SWA_GROUP):
                sink = jnp.where(row_group == g, sink_ref[hk * SWA_GROUP + g] * LOG2E, sink)
            s_ctx = _dot_nt(qh, k_ctx)
            m = jnp.maximum(jnp.max(s_ctx, axis=-1, keepdims=True), sink)
            if band:
                s = _dot_nt(qh, k_band)
                s_band = jnp.concatenate([jnp.where(valid_prev, s[:, :SWA_BLOCK], NEG_INF),
                                          s[:, SWA_BLOCK:2 * SWA_BLOCK],
                                          jnp.where(valid_next, s[:, 2 * SWA_BLOCK:], NEG_INF)], axis=1)
                m = jnp.maximum(m, jnp.max(s_band, axis=-1, keepdims=True))

            def ones_other(v, hk=hk):
                vlane = lax.broadcasted_iota(jnp.int32, v.shape, 1)
                own = (vlane < SWA_HEAD_DIM) if hk == 0 else (vlane >= SWA_HEAD_DIM)
                return jnp.where(own, v, jnp.ones_like(v))

            o = _dot(jnp.exp2(s_ctx - m).astype(BF16), ones_other(v_ctx))
            if band:
                o = o + _dot(jnp.exp2(s_band - m).astype(BF16), ones_other(v_band))
            denom = pltpu.roll(o, SWA_HEAD_DIM, 1) + jnp.exp2(sink - m)
            halves.append(o / denom)
        for g in range(SWA_GROUP):
            rs = slice(g * rows, (g + 1) * rows)
            o_ref[qrows, g * LANES:(g + 1) * LANES] = jnp.where(low, halves[0][rs], halves[1][rs]).astype(BF16)


def _swa_attention(sink, qs, ks, vs, *, ctx_queries):
    width = SWA_HEADS * SWA_HEAD_DIM
    kvw = SWA_KV_HEADS * SWA_HEAD_DIM
    ctx_map = lambda b, i: (N_LAT // CTX_LEN + b, 0)
    smem = pl.BlockSpec(memory_space=pltpu.SMEM)
    if ctx_queries:
        grid, rows = (BATCH, 1), N_CTX
        in_specs = [smem, pl.BlockSpec((CTX_LEN, width), ctx_map),
                    pl.BlockSpec((CTX_LEN, kvw), ctx_map), pl.BlockSpec((CTX_LEN, kvw), ctx_map)]
        args = (sink, qs, ks, vs)
        out_spec = pl.BlockSpec((CTX_LEN, width), lambda b, i: (b, 0))
        n_blocks = 1
    else:
        rows = N_LAT
        n_blocks = SEQ // SWA_BLOCK
        steps = n_blocks // SWA_QBLOCKS
        grid = (BATCH, steps)
        q_rows = SWA_QBLOCKS * SWA_BLOCK

        def kv_block(offset):
            return lambda b, i: (b * n_blocks + jnp.clip(i * SWA_QBLOCKS + offset, 0, n_blocks - 1), 0)

        band_specs = [pl.BlockSpec((SWA_BLOCK, kvw), kv_block(off)) for off in range(-1, SWA_QBLOCKS + 1)]
        in_specs = ([smem, pl.BlockSpec((q_rows, width), lambda b, i: (b * steps + i, 0))]
                    + band_specs + [pl.BlockSpec((CTX_LEN, kvw), ctx_map)]
                    + band_specs + [pl.BlockSpec((CTX_LEN, kvw), ctx_map)])
        n_band = len(band_specs)
        args = (sink, qs) + (ks,) * (n_band + 1) + (vs,) * (n_band + 1)
        out_spec = pl.BlockSpec((q_rows, width), lambda b, i: (b * steps + i, 0))
    return pl.pallas_call(
        functools.partial(_swa_kernel, band=not ctx_queries, n_blocks=n_blocks),
        grid=grid,
        in_specs=in_specs,
        out_specs=out_spec,
        out_shape=jax.ShapeDtypeStruct((rows, width), BF16),
        compiler_params=_cparams("parallel", "parallel"),
        name="swa_ctx" if ctx_queries else "swa_lat",
    )(*args)


def _diff_kernel(q_ref, kc_ref, kl_ref, vc_ref, vl_ref, lam_ref, g_ref, o_ref, *, lam_init):
    lam = (jnp.exp(jnp.sum(lam_ref[0:1, :] * lam_ref[1:2, :], axis=-1, keepdims=True))
           - jnp.exp(jnp.sum(lam_ref[2:3, :] * lam_ref[3:4, :], axis=-1, keepdims=True)) + lam_init)
    lane = lax.broadcasted_iota(jnp.int32, (q_ref.shape[0], LANES), 1)
    low = lane < DIFF_HALF
    k_refs = (kc_ref, kl_ref)
    v_refs = (vc_ref, vl_ref)
    for h in range(DIFF_HEADS_PER_STEP):
        sl = slice(h * LANES, (h + 1) * LANES)
        q = q_ref[:, sl]
        zero = jnp.zeros_like(q)

        vx = [jnp.concatenate([v[:, sl], jnp.ones((v.shape[0], LANES), BF16)], axis=1) for v in v_refs]

        def attend(qh):
            ss = [_dot_nt(qh, k[:, sl]) for k in k_refs]
            m = functools.reduce(jnp.maximum, [jnp.max(s, axis=-1, keepdims=True) for s in ss])
            ox = functools.reduce(jnp.add, [_dot(jnp.exp2(s - m).astype(BF16), v) for s, v in zip(ss, vx)])
            return ox[:, :LANES] / ox[:, LANES:]

        o = attend(jnp.where(low, q, zero)) - lam * attend(jnp.where(low, zero, q))
        o = _rms(o) * g_ref[...] * (1.0 - lam_init)
        o_ref[:, sl] = o.astype(BF16)


def _diff_attention(q, k, v, lam_vecs, g_sub, lam_init):
    n_q = SEQ // TQ_DIFF
    q_map = lambda b, h, i: (b * n_q + i, h)
    ctx_map = lambda b, h, i: (N_LAT // CTX_LEN + b, h)
    lat_map = lambda b, h, i: (b, h)
    const = lambda b, h, i: (0, 0)
    width = LANES * DIFF_HEADS_PER_STEP
    return pl.pallas_call(
        functools.partial(_diff_kernel, lam_init=lam_init),
        grid=(BATCH, DIFF_HEADS // DIFF_HEADS_PER_STEP, n_q),
        in_specs=[
            pl.BlockSpec((TQ_DIFF, width), q_map),
            pl.BlockSpec((CTX_LEN, width), ctx_map),
            pl.BlockSpec((SEQ, width), lat_map),
            pl.BlockSpec((CTX_LEN, width), ctx_map),
            pl.BlockSpec((SEQ, width), lat_map),
            pl.BlockSpec(lam_vecs.shape, const),
            pl.BlockSpec(g_sub.shape, const),
        ],
        out_specs=pl.BlockSpec((TQ_DIFF, width), q_map),
        out_shape=jax.ShapeDtypeStruct((N_LAT, DIFF_HEADS * DIFF_V), BF16),
        compiler_params=_cparams("parallel", "parallel", "parallel"),
        name="diff_attn",
    )(q, k, k, v, v, lam_vecs, g_sub)


BF16_BITS = 16
HIGH_HALF_WORD = 0xFFFF0000


def _pack_bf16_valued(x):
    n = x.shape[1] // 2
    lo = pltpu.bitcast(x[:, :n], jnp.uint32)
    hi = pltpu.bitcast(x[:, n:], jnp.uint32)
    return pltpu.bitcast((lo >> BF16_BITS) | hi, jnp.int32)


def _pack_bf16_pairs(x):
    return _pack_bf16_valued(x.astype(BF16).astype(F32))


def _unpack_bf16_pairs(w):
    u = pltpu.bitcast(w, jnp.uint32)
    lo = pltpu.bitcast(u << BF16_BITS, F32)
    hi = pltpu.bitcast(u & jnp.uint32(HIGH_HALF_WORD), F32)
    return lo, hi


def _route(logits_t, bias, cnt_ref, tri_ref, idx_ref, w_ref, rank_ref):
    tm = logits_t.shape[1]
    scores = jax.nn.sigmoid(logits_t)
    biased = scores + bias
    sub = lax.broadcasted_iota(jnp.int32, (GROUP_SIZE, tm), 0).astype(F32)
    grp_scores, grp_biased, grp_index = [], [], []
    group_score = []
    for g in range(N_GROUPS):
        sl = slice(g * GROUP_SIZE, (g + 1) * GROUP_SIZE)
        bg = biased[sl, :]
        grp_scores.append(scores[sl, :])
        grp_biased.append(bg)
        grp_index.append(sub + float(g * GROUP_SIZE))
        m1 = jnp.max(bg, axis=0, keepdims=True)
        first = jnp.min(jnp.where(bg == m1, sub, float(GROUP_SIZE)), axis=0, keepdims=True)
        m2 = jnp.max(jnp.where(sub == first, -jnp.inf, bg), axis=0, keepdims=True)
        group_score.append(m1 + m2)
    keep = [jnp.zeros((1, tm), F32) for _ in range(N_GROUPS)]
    for _ in range(TOPK_GROUPS):
        m = functools.reduce(jnp.maximum, group_score)
        found = jnp.zeros((1, tm), F32)
        for g in range(N_GROUPS):
            hit = jnp.where(group_score[g] == m, 1.0 - found, 0.0)
            keep[g] = keep[g] + hit
            found = found + hit
            group_score[g] = jnp.where(hit > 0.0, -jnp.inf, group_score[g])
    vals = [jnp.where(keep[g] > 0.0, grp_biased[g], NEG_INF) for g in range(N_GROUPS)]
    chosen = [jnp.zeros((GROUP_SIZE, tm), F32) for _ in range(N_GROUPS)]
    picked = []
    for _ in range(TOP_K):
        m = jnp.max(functools.reduce(jnp.maximum, vals), axis=0, keepdims=True)
        cand = [jnp.where(vals[g] == m, grp_index[g], float(N_EXPERTS)) for g in range(N_GROUPS)]
        ei = jnp.min(functools.reduce(jnp.minimum, cand), axis=0, keepdims=True)
        sel = [grp_index[g] == ei for g in range(N_GROUPS)]
        s_k = functools.reduce(jnp.add, [jnp.where(sel[g], grp_scores[g], 0.0) for g in range(N_GROUPS)])
        picked.append((ei, jnp.sum(s_k, axis=0, keepdims=True)))
        vals = [jnp.where(sel[g], -jnp.inf, vals[g]) for g in range(N_GROUPS)]
        chosen = [jnp.where(sel[g], 1.0, chosen[g]) for g in range(N_GROUPS)]
    total = functools.reduce(jnp.add, [s for _, s in picked])
    chosen_all = jnp.concatenate(chosen, axis=0)
    tri = tri_ref[...]
    width = tri.shape[0]
    cnt = cnt_ref[...]
    parts = []
    for j in range(tm // width):
        ch = chosen_all[:, j * width:(j + 1) * width]
        parts.append(_dot(ch.astype(BF16), tri) + cnt)
        cnt = cnt + jnp.sum(ch, axis=1, keepdims=True)
    cnt_ref[...] = cnt
    rank_all = jnp.concatenate(parts, axis=1)
    for k, (ei, s) in enumerate(picked):
        idx_ref[k:k + 1, :] = ei.astype(jnp.int32)
        w_ref[k:k + 1, :] = s / total * ROUTED_SCALE
        r = functools.reduce(jnp.add, [
            jnp.where(grp_index[g] == ei, rank_all[g * GROUP_SIZE:(g + 1) * GROUP_SIZE, :], 0.0)
            for g in range(N_GROUPS)])
        rank_ref[k:k + 1, :] = jnp.sum(r, axis=0, keepdims=True).astype(jnp.int32)


ROUTE_TN = 2048
TRI_N = 256


def _router_kernel(logits_ref, br_ref, tri_ref, idx_ref, w_ref, rank_ref, cnt_out_ref, cnt_ref):
    @pl.when(pl.program_id(0) == 0)
    def _():
        cnt_ref[...] = jnp.zeros_like(cnt_ref)

    _route(logits_ref[...], br_ref[...], cnt_ref, tri_ref, idx_ref, w_ref, rank_ref)
    cnt_out_ref[...] = jnp.broadcast_to(cnt_ref[...], cnt_out_ref.shape).astype(jnp.int32)


def _router(logits_t, br):
    n_tok = logits_t.shape[1]
    col = lambda i: (0, i)
    const = lambda i: (0, 0)
    tri = (jnp.arange(TRI_N)[:, None] < jnp.arange(TRI_N)[None, :]).astype(BF16)
    blk = pl.BlockSpec((TOP_K, ROUTE_TN), col)
    return pl.pallas_call(
        _router_kernel,
        grid=(n_tok // ROUTE_TN,),
        in_specs=[pl.BlockSpec((N_EXPERTS, ROUTE_TN), col), pl.BlockSpec(br.shape, const),
                  pl.BlockSpec(tri.shape, const)],
        out_specs=[blk, blk, blk, pl.BlockSpec((N_EXPERTS, LANES), const)],
        out_shape=[
            jax.ShapeDtypeStruct((TOP_K, n_tok), jnp.int32),
            jax.ShapeDtypeStruct((TOP_K, n_tok), F32),
            jax.ShapeDtypeStruct((TOP_K, n_tok), jnp.int32),
            jax.ShapeDtypeStruct((N_EXPERTS, LANES), jnp.int32),
        ],
        scratch_shapes=[pltpu.VMEM((N_EXPERTS, 1), F32)],
        compiler_params=_cparams("arbitrary"),
        name="router",
    )(logits_t, br, tri)


def _post_kernel(oal_ref, oac_ref, obl_ref, obc_ref, hl_ref, hc_ref, wout_ref, gtm_ref, g_ref, sh_ref, sc_ref,
                 gtf_ref, wrh_ref, wrl_ref, wsg_ref, wsu_ref, wsd_ref, hs_ref, f_ref, logits_ref):
    half = D_MODEL // 2
    for c in range(TM // TM_CHUNK):
        rows = slice(c * TM_CHUNK, (c + 1) * TM_CHUNK)
        proj = (_dot(_token_rows(oal_ref, oac_ref, rows), wout_ref[:half, :])
                + _dot(_token_rows(obl_ref, obc_ref, rows), wout_ref[half:, :]))
        h1 = _token_rows(hl_ref, hc_ref, rows) + gtm_ref[0] * proj
        f = _rms(h1) * g_ref[...]
        f = f * (1.0 + sc_ref[0]) + sh_ref[0]
        f_hi = f.astype(BF16)
        f_hi32 = f_hi.astype(F32)
        f_lo = (f - f_hi32).astype(BF16)
        f_ref[rows, :] = _pack_bf16_valued(f_hi32)
        logits_ref[:, rows] = (_dot_nt(wrh_ref[...], f_hi) + _dot_nt(wrh_ref[...], f_lo)
                               + _dot_nt(wrl_ref[...], f_hi))
        mid = _silu(_dot(f_hi, wsg_ref[...])) * _dot(f_hi, wsu_ref[...])
        shared = _dot(mid.astype(BF16), wsd_ref[...])
        hs_ref[rows, :] = h1 + gtf_ref[0] * shared


def _post(o_halves, h_lat, h_ctx, wout, gtm, g_ffn, sh, sc, gtf, wrh, wrl, wsg, wsu, wsd, n_rows):
    row = lambda i: (i, 0)
    const = lambda i: (0, 0)
    mod = lambda i: (_mod_index(i), 0, 0)
    vec = pl.BlockSpec((1, 1, D_MODEL), mod)
    full = lambda a: pl.BlockSpec(a.shape, const)
    o_specs, o_args = [], []
    for lat, ctx, col in o_halves:
        o_specs += _token_specs(D_MODEL // 2, col)
        o_args += [lat, ctx]
    return pl.pallas_call(
        _post_kernel,
        grid=(n_rows // TM,),
        in_specs=o_specs + _token_specs() + [
            full(wout), vec, full(g_ffn), vec, vec, vec,
            full(wrh), full(wrl), full(wsg), full(wsu), full(wsd),
        ],
        out_specs=[
            pl.BlockSpec((TM, D_MODEL), row),
            pl.BlockSpec((TM, D_MODEL // 2), row),
            pl.BlockSpec((N_EXPERTS, TM), lambda i: (0, i)),
        ],
        out_shape=[
            jax.ShapeDtypeStruct((n_rows, D_MODEL), F32),
            jax.ShapeDtypeStruct((n_rows, D_MODEL // 2), jnp.int32),
            jax.ShapeDtypeStruct((N_EXPERTS, n_rows), F32),
        ],
        compiler_params=_cparams("parallel"),
        name="post_attn",
    )(*o_args, h_lat, h_ctx, wout, gtm, g_ffn, sh, sc, gtf, wrh, wrl, wsg, wsu, wsd)


DEST_TN = 2048


def _dest_kernel(ps_ref, idx_ref, rank_ref, o_ref):
    idx = idx_ref[...]
    acc = rank_ref[...]
    for e in range(N_EXPERTS):
        acc = acc + jnp.where(idx == e, ps_ref[e], 0)
    o_ref[...] = acc


def _dest(pad_start, top_idx, rank):
    n_tok = top_idx.shape[1]
    blk = pl.BlockSpec((TOP_K, DEST_TN), lambda i: (0, i))
    return pl.pallas_call(
        _dest_kernel,
        grid=(n_tok // DEST_TN,),
        in_specs=[pl.BlockSpec(memory_space=pltpu.SMEM), blk, blk],
        out_specs=blk,
        out_shape=jax.ShapeDtypeStruct((TOP_K, n_tok), jnp.int32),
        compiler_params=_cparams("parallel"),
        name="dest",
    )(pad_start, top_idx, rank)


SC_CORES = 2
SC_SUBCORES = 16
SC_WORKERS = SC_CORES * SC_SUBCORES
SC_DISPATCH_ROWS = 32
SC_GATHER_ROWS = 64
ROW_WORDS = D_MODEL // 2


def _sc_mesh():
    return plsc.VectorSubcoreMesh(core_axis_name="core", subcore_axis_name="subcore")


def _sc_worker():
    return lax.axis_index("subcore") * SC_CORES + lax.axis_index("core")


def _sc_dispatch(f_packed, dest, n_slots):
    n_tok = f_packed.shape[0]
    per_worker = n_tok // SC_WORKERS
    n_chunks = per_worker // SC_DISPATCH_ROWS
    assert n_chunks * SC_DISPATCH_ROWS * SC_WORKERS == n_tok and n_chunks % 2 == 0
    dest4 = dest.reshape(TOP_K, SC_WORKERS, n_chunks, SC_DISPATCH_ROWS)

    def body(f_hbm, dest_hbm, xs_hbm, rows0, rows1, idx_v, load0, load1, scat0, scat1):
        wid = _sc_worker()
        for k in range(TOP_K):
            pltpu.sync_copy(dest_hbm.at[k, wid], idx_v.at[k])

        def load(c, buf, sem):
            start = pl.multiple_of(wid * per_worker + c * SC_DISPATCH_ROWS, 8)
            return pltpu.make_async_copy(f_hbm.at[pl.ds(start, SC_DISPATCH_ROWS)], buf, sem)

        def scatters(c, buf, sem):
            return [pltpu.make_async_copy(buf, xs_hbm.at[idx_v.at[k, c]], sem) for k in range(TOP_K)]

        load(0, rows0, load0).start()

        @pl.loop(0, n_chunks, step=2)
        def _(c):
            @pl.when(c > 0)
            def _():
                for cp in scatters(c - 1, rows1, scat1):
                    cp.wait()

            load(c + 1, rows1, load1).start()
            load(c, rows0, load0).wait()
            for cp in scatters(c, rows0, scat0):
                cp.start()
            load(c + 1, rows1, load1).wait()
            for cp in scatters(c + 1, rows1, scat1):
                cp.start()
            for cp in scatters(c, rows0, scat0):
                cp.wait()

            @pl.when(c + 2 < n_chunks)
            def _():
                load(c + 2, rows0, load0).start()

        for cp in scatters(n_chunks - 1, rows1, scat1):
            cp.wait()

    rows = pltpu.VMEM((SC_DISPATCH_ROWS, ROW_WORDS), jnp.int32)
    return pl.kernel(
        body,
        out_type=jax.ShapeDtypeStruct((n_slots, ROW_WORDS), jnp.int32),
        mesh=_sc_mesh(),
        scratch_types=[rows, rows, pltpu.VMEM((TOP_K, n_chunks, SC_DISPATCH_ROWS), jnp.int32)]
        + [pltpu.SemaphoreType.DMA] * 4,
        name="sc_dispatch",
    )(f_packed, dest4)


def _sc_gather(ys, dest):
    n_idx = dest.shape[0] * dest.shape[1]
    per_worker = n_idx // SC_WORKERS
    n_chunks = per_worker // SC_GATHER_ROWS
    assert n_chunks * SC_GATHER_ROWS * SC_WORKERS == n_idx and n_chunks % 2 == 0
    dest3 = dest.reshape(SC_WORKERS, n_chunks, SC_GATHER_ROWS)

    def body(ys_hbm, dest_hbm, out_hbm, rows0, rows1, idx_v, gat0, gat1, put0, put1):
        wid = _sc_worker()
        pltpu.sync_copy(dest_hbm.at[wid], idx_v)

        def gather(c, buf, sem):
            return pltpu.make_async_copy(ys_hbm.at[idx_v.at[c]], buf, sem)

        def put(c, buf, sem):
            start = pl.multiple_of(wid * per_worker + c * SC_GATHER_ROWS, 8)
            return pltpu.make_async_copy(buf, out_hbm.at[pl.ds(start, SC_GATHER_ROWS)], sem)

        gather(0, rows0, gat0).start()

        @pl.loop(0, n_chunks, step=2)
        def _(c):
            @pl.when(c > 0)
            def _():
                put(c - 1, rows1, put1).wait()

            gather(c + 1, rows1, gat1).start()
            gather(c, rows0, gat0).wait()
            put(c, rows0, put0).start()
            gather(c + 1, rows1, gat1).wait()
            put(c + 1, rows1, put1).start()
            put(c, rows0, put0).wait()

            @pl.when(c + 2 < n_chunks)
            def _():
                gather(c + 2, rows0, gat0).start()

        put(n_chunks - 1, rows1, put1).wait()

    rows = pltpu.VMEM((SC_GATHER_ROWS, ROW_WORDS), jnp.int32)
    return pl.kernel(
        body,
        out_type=jax.ShapeDtypeStruct((n_idx, ROW_WORDS), jnp.int32),
        mesh=_sc_mesh(),
        scratch_types=[rows, rows, pltpu.VMEM((n_chunks, SC_GATHER_ROWS), jnp.int32)]
        + [pltpu.SemaphoreType.DMA] * 4,
        name="sc_gather",
    )(ys, dest3)


SC_LANES = 16
SC_REDUCE_TOKENS = 4


def _sc_combine(ys, dest, top_w):
    n_tok = dest.shape[1]
    per_worker = n_tok // SC_WORKERS
    n_chunks = per_worker // SC_REDUCE_TOKENS
    assert n_chunks * SC_REDUCE_TOKENS * SC_WORKERS == n_tok and n_chunks % 2 == 0
    pairs = SC_REDUCE_TOKENS // 2
    rows_per_gather = 2 * TOP_K
    assert rows_per_gather == SC_LANES
    idx = dest.T.reshape(SC_WORKERS, n_chunks, pairs, rows_per_gather)
    w_lanes = jnp.repeat(top_w.T, SC_LANES, axis=1)
    groups = ROW_WORDS // SC_LANES

    def body(ys_hbm, idx_hbm, w_hbm, out_hbm, rows0, rows1, w0, w1, out0, out1, idx_v, in0, in1, put0, put1):
        wid = _sc_worker()
        pltpu.sync_copy(idx_hbm.at[wid], idx_v)

        def fetch(c, rows, wv, sem):
            start = pl.multiple_of(wid * per_worker + c * SC_REDUCE_TOKENS, SC_REDUCE_TOKENS)
            copies = [pltpu.make_async_copy(ys_hbm.at[idx_v.at[c, g]], rows.at[g], sem) for g in range(pairs)]
            return copies + [pltpu.make_async_copy(w_hbm.at[pl.ds(start, SC_REDUCE_TOKENS)], wv, sem)]

        def put(c, out, sem):
            start = pl.multiple_of(wid * per_worker + c * SC_REDUCE_TOKENS, SC_REDUCE_TOKENS)
            return pltpu.make_async_copy(out, out_hbm.at[pl.ds(start, SC_REDUCE_TOKENS)], sem)

        def reduce(rows, wv, out):
            for t in range(SC_REDUCE_TOKENS):
                g, base = t // 2, (t % 2) * TOP_K
                wts = [wv[t, pl.ds(k * SC_LANES, SC_LANES)] for k in range(TOP_K)]

                @plsc.parallel_loop(0, groups, unroll=4)
                def _(j):
                    off = pl.multiple_of(j * SC_LANES, SC_LANES)
                    acc_lo = jnp.zeros((SC_LANES,), F32)
                    acc_hi = jnp.zeros((SC_LANES,), F32)
                    for k in range(TOP_K):
                        word = rows[g, base + k, pl.ds(off, SC_LANES)]
                        lo = plsc.bitcast(word << BF16_BITS, F32)
                        hi = plsc.bitcast(word & jnp.int32(-(1 << BF16_BITS)), F32)
                        acc_lo = acc_lo + wts[k] * lo
                        acc_hi = acc_hi + wts[k] * hi
                    out[t, pl.ds(off, SC_LANES)] = acc_lo
                    out[t, pl.ds(ROW_WORDS + off, SC_LANES)] = acc_hi

        for cp in fetch(0, rows0, w0, in0):
            cp.start()

        @pl.loop(0, n_chunks, step=2)
        def _(c):
            for cp in fetch(c + 1, rows1, w1, in1):
                cp.start()
            for cp in fetch(c, rows0, w0, in0):
                cp.wait()

            @pl.when(c > 0)
            def _():
                put(c - 2, out0, put0).wait()

            reduce(rows0, w0, out0)
            put(c, out0, put0).start()

            @pl.when(c + 2 < n_chunks)
            def _():
                for cp in fetch(c + 2, rows0, w0, in0):
                    cp.start()

            for cp in fetch(c + 1, rows1, w1, in1):
                cp.wait()

            @pl.when(c > 0)
            def _():
                put(c - 1, out1, put1).wait()

            reduce(rows1, w1, out1)
            put(c + 1, out1, put1).start()

        put(n_chunks - 2, out0, put0).wait()
        put(n_chunks - 1, out1, put1).wait()

    rows = pltpu.VMEM((pairs, rows_per_gather, ROW_WORDS), jnp.int32)
    wbuf = pltpu.VMEM((SC_REDUCE_TOKENS, TOP_K * SC_LANES), F32)
    obuf = pltpu.VMEM((SC_REDUCE_TOKENS, D_MODEL), F32)
    return pl.kernel(
        body,
        out_type=jax.ShapeDtypeStruct((n_tok, D_MODEL), F32),
        mesh=_sc_mesh(),
        scratch_types=[rows, rows, wbuf, wbuf, obuf, obuf,
                       pltpu.VMEM((n_chunks, pairs, rows_per_gather), jnp.int32)]
        + [pltpu.SemaphoreType.DMA] * 4,
        compiler_params=pltpu.CompilerParams(needs_layout_passes=False),
        name="sc_combine",
    )(ys, idx, w_lanes)


X_SLOTS = 4
O_SLOTS = 3


def _moe_kernel(be_ref, nused_ref, slot_ref, next_ref, xs_hbm, wg_hbm, wu_hbm, wd_hbm, ys_hbm,
                xbuf, obuf, wg_s, wu_s, wd_s, wg_f, wu_f, wd_f, xsem, osem, wsem, *, layer):
    n_used = nused_ref[0]

    def x_copy(i, s):
        start = pl.multiple_of(i * MOE_ROWS, MOE_ROWS)
        return pltpu.make_async_copy(xs_hbm.at[pl.ds(start, MOE_ROWS)], xbuf.at[s], xsem.at[s])

    def o_copy(i, s):
        start = pl.multiple_of(i * MOE_ROWS, MOE_ROWS)
        return pltpu.make_async_copy(obuf.at[s], ys_hbm.at[pl.ds(start, MOE_ROWS)], osem.at[s])

    def weight_copies(expert, slot):
        return [pltpu.make_async_copy(src.at[layer, expert], dst.at[slot], wsem.at[slot, j])
                for j, (src, dst) in enumerate(((wg_hbm, wg_f), (wu_hbm, wu_f), (wd_hbm, wd_f)))]

    first_expert = be_ref[0]
    for cp in weight_copies(first_expert, slot_ref[first_expert]):
        cp.start()
    for j in range(X_SLOTS - 1):
        @pl.when(j < n_used)
        def _(j=j):
            x_copy(j, j).start()

    def block(i, carry):
        e = be_ref[i]
        xs = lax.rem(i, X_SLOTS)
        os = lax.rem(i, O_SLOTS)

        @pl.when(jnp.logical_or(i == 0, e != be_ref[jnp.maximum(i - 1, 0)]))
        def _():
            slot = slot_ref[e]
            for cp in weight_copies(e, slot):
                cp.wait()
            wg_s[...] = wg_f[slot].astype(BF16)
            wu_s[...] = wu_f[slot].astype(BF16)
            wd_s[...] = wd_f[slot].astype(BF16)
            nxt = next_ref[e]

            @pl.when(nxt >= 0)
            def _():
                for cp in weight_copies(nxt, 1 - slot):
                    cp.start()

        x_copy(i, xs).wait()

        @pl.when(i + X_SLOTS - 1 < n_used)
        def _():
            x_copy(i + X_SLOTS - 1, lax.rem(i + X_SLOTS - 1, X_SLOTS)).start()

        @pl.when(i >= O_SLOTS)
        def _():
            o_copy(i - O_SLOTS, os).wait()

        half = D_MODEL // 2
        for c in range(MOE_ROWS // MOE_CHUNK):
            rows = pl.ds(c * MOE_CHUNK, MOE_CHUNK)
            lo, hi = _unpack_bf16_pairs(xbuf[xs, rows, :])
            lo, hi = lo.astype(BF16), hi.astype(BF16)
            gate = _dot(lo, wg_s[:half, :]) + _dot(hi, wg_s[half:, :])
            up = _dot(lo, wu_s[:half, :]) + _dot(hi, wu_s[half:, :])
            mid = _silu(gate) * up
            obuf[os, rows, :] = _pack_bf16_pairs(_dot(mid.astype(BF16), wd_s[...]))
        o_copy(i, os).start()
        return carry

    lax.fori_loop(0, n_used, block, 0)
    for j in range(O_SLOTS):
        @pl.when(n_used > j)
        def _(j=j):
            i = n_used - 1 - j
            o_copy(i, lax.rem(i, O_SLOTS)).wait()


def _moe(block_expert, n_used, slot, next_expert, xs, wg, wu, wd, layer):
    hbm = pl.BlockSpec(memory_space=pl.ANY)
    block = (MOE_ROWS, ROW_WORDS)
    grid_spec = pltpu.PrefetchScalarGridSpec(
        num_scalar_prefetch=4,
        grid=(1,),
        in_specs=[hbm, hbm, hbm, hbm],
        out_specs=hbm,
        scratch_shapes=[
            pltpu.VMEM((X_SLOTS,) + block, jnp.int32),
            pltpu.VMEM((O_SLOTS,) + block, jnp.int32),
            pltpu.VMEM((D_MODEL, EXPERT_FF), BF16),
            pltpu.VMEM((D_MODEL, EXPERT_FF), BF16),
            pltpu.VMEM((EXPERT_FF, D_MODEL), BF16),
            pltpu.VMEM((2, D_MODEL, EXPERT_FF), F32),
            pltpu.VMEM((2, D_MODEL, EXPERT_FF), F32),
            pltpu.VMEM((2, EXPERT_FF, D_MODEL), F32),
            pltpu.SemaphoreType.DMA((X_SLOTS,)),
            pltpu.SemaphoreType.DMA((O_SLOTS,)),
            pltpu.SemaphoreType.DMA((2, 3)),
        ],
    )
    return pl.pallas_call(
        functools.partial(_moe_kernel, layer=layer),
        grid_spec=grid_spec,
        out_shape=jax.ShapeDtypeStruct((xs.shape[0], ROW_WORDS), jnp.int32),
        compiler_params=_cparams("arbitrary"),
        name="moe_experts",
    )(block_expert, n_used, slot, next_expert, xs, wg, wu, wd)


def _combine_kernel(y_ref, hs_ref, gtf_ref, gfin_ref, o_ref, *, final):
    out = hs_ref[...] + gtf_ref[0] * y_ref[...]
    if final:
        out = _rms(out) * gfin_ref[...]
    o_ref[...] = out


def _combine(y, hs, gtf, g_final, *, final):
    n_rows = hs.shape[0]
    return pl.pallas_call(
        functools.partial(_combine_kernel, final=final),
        grid=(n_rows // TM_COMB,),
        in_specs=[
            pl.BlockSpec((TM_COMB, D_MODEL), lambda i: (i, 0)),
            pl.BlockSpec((TM_COMB, D_MODEL), lambda i: (i, 0)),
            pl.BlockSpec((1, 1, D_MODEL), lambda i: (jnp.minimum(i // (SEQ // TM_COMB), BATCH), 0, 0)),
            pl.BlockSpec((1, D_MODEL), lambda i: (0, 0)),
        ],
        out_specs=pl.BlockSpec((TM_COMB, D_MODEL), lambda i: (i, 0)),
        out_shape=jax.ShapeDtypeStruct((n_rows, D_MODEL), F32),
        compiler_params=_cparams("parallel"),
        name="combine_final" if final else "combine",
    )(y, hs, gtf, g_final)


def _rope_tables():
    rows = SEQ // GRID_W

    def angles(rot_dim):
        half = rot_dim // 2
        inv_freq = ROPE_BASE ** (-jnp.arange(0, half, 2, dtype=F32) / half)
        row = jnp.repeat(jnp.arange(rows, dtype=F32), GRID_W)
        col = jnp.tile(jnp.arange(GRID_W, dtype=F32), rows)
        ang_r = row[:, None] * inv_freq
        ang_c = col[:, None] * inv_freq
        return jnp.concatenate([ang_r, ang_r, ang_c, ang_c], axis=-1)

    def signed(sin, quarter):
        sign = jnp.where((jnp.arange(sin.shape[-1]) // quarter) % 2 == 0, -1.0, 1.0)
        return sin * sign

    def with_identity(cos, sin):
        cos = jnp.concatenate([cos, jnp.ones((TM, LANES), F32)], axis=0)
        sin = jnp.concatenate([sin, jnp.zeros((TM, LANES), F32)], axis=0)
        return cos, sin

    a64 = angles(SWA_HEAD_DIM)
    c64 = jnp.tile(jnp.cos(a64), (1, LANES // SWA_HEAD_DIM))
    s64 = jnp.tile(signed(jnp.sin(a64), SWA_HEAD_DIM // 4), (1, LANES // SWA_HEAD_DIM))
    a32 = angles(MLA_ROPE)
    pad_lo = MLA_NOPE
    pad_hi = LANES - MLA_NOPE - MLA_ROPE
    cm = jnp.concatenate([jnp.ones((SEQ, pad_lo), F32), jnp.cos(a32), jnp.ones((SEQ, pad_hi), F32)], axis=-1)
    sm = jnp.concatenate([jnp.zeros((SEQ, pad_lo), F32), signed(jnp.sin(a32), MLA_ROPE // 4),
                          jnp.zeros((SEQ, pad_hi), F32)], axis=-1)
    return with_identity(cm, sm) + with_identity(c64, s64)


def _layer0_weights(wa_in, wa_uq, wa_ukv, wa_out):
    d = D_MODEL
    cq, ckv, kr, qs, ks, vs = jnp.split(
        wa_in, [C_CKV, C_KR, C_KR + MLA_ROPE, C_KR + MLA_ROPE + SWA_HEADS * SWA_HEAD_DIM,
                C_KR + MLA_ROPE + (SWA_HEADS + SWA_KV_HEADS) * SWA_HEAD_DIM], axis=-1)
    kr_pad = jnp.concatenate([jnp.zeros((d, MLA_NOPE), F32), kr,
                              jnp.zeros((d, LANES - MLA_NOPE - MLA_ROPE), F32)], axis=-1)
    qs_pair = qs.reshape(d, SWA_KV_HEADS, SWA_GROUP, SWA_HEAD_DIM).transpose(0, 2, 1, 3).reshape(d, -1)
    win = jnp.concatenate([cq, ckv, kr_pad, qs_pair, ks, vs], axis=-1).astype(BF16)
    uq = wa_uq.reshape(MLA_Q_RANK, MLA_HEADS, MLA_NOPE + MLA_ROPE)
    uq = jnp.pad(uq, ((0, 0), (0, 0), (0, LANES - MLA_NOPE - MLA_ROPE))).reshape(MLA_Q_RANK, -1)
    ukv = wa_ukv.reshape(MLA_KV_RANK, MLA_HEADS, MLA_NOPE + MLA_V)
    uk = jnp.pad(ukv[:, :, :MLA_NOPE], ((0, 0), (0, 0), (0, LANES - MLA_NOPE))).reshape(MLA_KV_RANK, -1)
    uv = ukv[:, :, MLA_NOPE:].reshape(MLA_KV_RANK, -1)
    wukv = jnp.concatenate([uk, uv], axis=-1)
    n_mla = MLA_HEADS * MLA_V
    out_swa = wa_out[n_mla:].reshape(SWA_KV_HEADS, SWA_GROUP, SWA_HEAD_DIM, d).transpose(1, 0, 2, 3)
    wout = jnp.concatenate([wa_out[:n_mla], out_swa.reshape(-1, d)], axis=0)
    return win, uq.astype(BF16), wukv.astype(BF16), wout.astype(BF16)


def _layer1_weights(wc_in):
    per_head = 4 * DIFF_HALF + DIFF_V
    w = wc_in.reshape(D_MODEL, DIFF_HEADS, per_head)
    q = w[:, :, :2 * DIFF_HALF].reshape(D_MODEL, -1)
    k = w[:, :, 2 * DIFF_HALF:4 * DIFF_HALF].reshape(D_MODEL, -1)
    v = w[:, :, 4 * DIFF_HALF:].reshape(D_MODEL, -1)
    return jnp.concatenate([q, k, v], axis=-1).astype(BF16)


def _block_tables(counts, n_blocks):
    padded = (counts + MOE_ROWS - 1) // MOE_ROWS * MOE_ROWS
    pad_end = jnp.cumsum(padded)
    pad_start = (pad_end - padded).astype(jnp.int32)
    block_start = jnp.arange(n_blocks, dtype=jnp.int32) * MOE_ROWS
    block_expert = jnp.minimum((pad_end[None, :] <= block_start[:, None]).sum(axis=1),
                               N_EXPERTS - 1).astype(jnp.int32)
    n_used = (pad_end[-1] // MOE_ROWS).astype(jnp.int32).reshape(1)
    has = padded > 0
    slot = ((jnp.cumsum(has) - 1) & 1).astype(jnp.int32)
    ids = jnp.where(has, jnp.arange(N_EXPERTS, dtype=jnp.int32), N_EXPERTS)
    after = jnp.concatenate([lax.cummin(ids, reverse=True)[1:], jnp.full((1,), N_EXPERTS, jnp.int32)])
    next_expert = jnp.where(after < N_EXPERTS, after, -1).astype(jnp.int32)
    return pad_start, block_expert, n_used, slot, next_expert


def _moe_layer(f_packed, top_idx, top_w, rank, counts, hs, gtf, g_final, wg, wu, wd, layer, *, final):
    n_tok = top_idx.shape[1]
    n_blocks = -(-(n_tok * TOP_K) // MOE_ROWS) + N_EXPERTS
    pad_start, block_expert, n_used, slot, next_expert = _block_tables(counts[:, 0], n_blocks)
    dest = _dest(pad_start, top_idx, rank)
    xs = _sc_dispatch(f_packed, dest, n_blocks * MOE_ROWS)
    ys = _moe(block_expert, n_used, slot, next_expert, xs, wg, wu, wd, layer)
    y = _sc_combine(ys, dest, top_w)
    return _combine(y, hs, gtf, g_final, final=final)


def kernel(x, c, ctx, c_ctx, w_ada, b_ada, g_mix, g_ffn, wa_in, ga_q, ga_kv, wa_uq, wa_ukv, wa_sink, wa_out,
           wc_in, lam_q1, lam_k1, lam_q2, lam_k2, gc_sub, wc_out, w_router, b_router, we_gate, we_up, we_down,
           ws_gate, ws_up, ws_down, g_final):
    d = D_MODEL
    x2, ctx2 = x.reshape(N_LAT, d), ctx.reshape(N_CTX, d)
    cc =jnp.concatenate([c, c_ctx[None, :], jnp.zeros((MOD_ROWS - BATCH - 1, d), F32)], axis=0)
    mod = _ada(cc, w_ada, b_ada).reshape(DEPTH, MOD_ROWS, 6, 1, d)

    def mod_vec(layer, j):
        return mod[layer, :, j]

    cm, sm, c64, s64 = _rope_tables()
    g_final2 = g_final.reshape(1, d)

    def router_weights(layer):
        wt = w_router[layer].T
        hi = wt.astype(BF16)
        lo = (wt - hi.astype(F32)).astype(BF16)
        return hi, lo

    def shared_weights(layer):
        return ws_gate[layer].astype(BF16), ws_up[layer].astype(BF16), ws_down[layer].astype(BF16)

    win, wuq, wukv, wout0 = _layer0_weights(wa_in[0], wa_uq[0], wa_ukv[0], wa_out[0])
    qm, km, vm, qs, ks, vs = _proj0(x2, ctx2, g_mix[0].reshape(1, d), mod_vec(0, 0), mod_vec(0, 1), win,
                                    ga_q[0].reshape(1, -1), ga_kv[0].reshape(1, -1), wuq, wukv, cm, sm, c64, s64)
    sink = wa_sink[0].astype(F32)
    o_halves = [(_mla_attention(qm, km, vm, ctx_queries=False), _mla_attention(qm, km, vm, ctx_queries=True), 0),
                (_swa_attention(sink, qs, ks, vs, ctx_queries=False),
                 _swa_attention(sink, qs, ks, vs, ctx_queries=True), 0)]
    hs, f, logits_t = _post(
        o_halves, x2, ctx2, wout0, mod_vec(0, 2), g_ffn[0].reshape(1, d), mod_vec(0, 3), mod_vec(0, 4),
        mod_vec(0, 5),
        *router_weights(0), *shared_weights(0), N_TOK)
    top_idx, top_w, rank, counts = _router(logits_t, b_router[0].reshape(N_EXPERTS, 1))
    h = _moe_layer(f, top_idx, top_w, rank, counts, hs, mod_vec(0, 5), g_final2,
                   we_gate, we_up, we_down, 0, final=False)

    layer = 1
    lam_init = 0.8 - 0.6 * math.exp(-0.3 * layer)
    q1, k1, v1 = _proj1(h, g_mix[1].reshape(1, d), mod_vec(1, 0), mod_vec(1, 1), _layer1_weights(wc_in[0]),
                        c64, s64)
    lam_vecs = jnp.zeros((8, LANES), F32).at[:4, :DIFF_HALF].set(
        jnp.stack([lam_q1[0], lam_k1[0], lam_q2[0], lam_k2[0]]).astype(F32))
    o = _diff_attention(q1, k1, v1, lam_vecs, gc_sub[0].reshape(1, DIFF_V), lam_init)
    hs, f, logits_t = _post(
        [(o, o, 0), (o, o, 1)], h, h, wc_out[0].astype(BF16), mod_vec(1, 2), g_ffn[1].reshape(1, d),
        mod_vec(1, 3), mod_vec(1, 4),
        mod_vec(1, 5), *router_weights(1), *shared_weights(1), N_LAT)
    top_idx, top_w, rank, counts = _router(logits_t, b_router[1].reshape(N_EXPERTS, 1))
    out = _moe_layer(f, top_idx, top_w, rank, counts, hs, mod_vec(1, 5), g_final2,
                     we_gate, we_up, we_down, 1, final=True)
    return out.reshape(BATCH, SEQ, d)
```
